```python
import math
import jax, jax.numpy as jnp
from jax import lax
import numpy as np

D_MODEL = 1024
BATCH = 8
SEQ = 8192
DEPTH = 1

GRID_W = 64
HEAD_DIM_A = 128
N_Q_HEADS_A = 8
N_KV_HEADS_A = 2
ROPE_THETA = 10000.0
Q_BLOCK = 128
HEAD_DIM_B = 64
N_HEADS_PER_DIL = 4
DIL_PAIRS = ((128, 1), (512, 4), (2048, 16))
N_HEADS_B = N_HEADS_PER_DIL * len(DIL_PAIRS)
BAND_BLOCK = 64
N_REL_BUCKETS = 32
REL_MAX_DIST = 1024
D_FF = 4 * D_MODEL
D_PLE = 256
NORM_EPS = 1e-6
NEG_INF = -1e30

SPLIT_SIZES = (
    N_Q_HEADS_A * HEAD_DIM_A,
    N_KV_HEADS_A * HEAD_DIM_A,
    N_KV_HEADS_A * HEAD_DIM_A,
    N_HEADS_B * HEAD_DIM_B,
    N_HEADS_B * HEAD_DIM_B,
    N_HEADS_B * HEAD_DIM_B,
    D_MODEL,
    D_MODEL,
)
D_IN_PROJ = sum(SPLIT_SIZES)

kernel_name = "hybrid_gqa_axialrope_dilated_swa_sqrelu_ple"


def rmsnorm(x, g):
    xf = x.astype(jnp.float32)
    y = xf * lax.rsqrt(jnp.mean(xf * xf, axis=-1, keepdims=True) + NORM_EPS)
    return (y * g.astype(jnp.float32)).astype(x.dtype)


def rope_1d(x, pos):
    d = x.shape[-1]
    inv_freq = jnp.power(ROPE_THETA, -jnp.arange(0, d, 2, dtype=jnp.float32) / d)
    ang = pos.astype(jnp.float32)[:, None] * inv_freq[None, :]
    cos = jnp.cos(ang)[None, :, None, :]
    sin = jnp.sin(ang)[None, :, None, :]
    xf = x.astype(jnp.float32)
    x1, x2 = xf[..., : d // 2], xf[..., d // 2:]
    out = jnp.concatenate([x1 * cos - x2 * sin, x2 * cos + x1 * sin], axis=-1)
    return out.astype(x.dtype)


def axial_rope(x, row_ids, col_ids):
    half = x.shape[-1] // 2
    return jnp.concatenate([rope_1d(x[..., :half], row_ids),
                            rope_1d(x[..., half:], col_ids)], axis=-1)


def t5_bucket(rel):
    nb = N_REL_BUCKETS // 2
    ret = (rel > 0).astype(np.int32) * nb
    n = np.abs(rel)
    max_exact = nb // 2
    large = max_exact + (np.log(np.maximum(n, 1) / max_exact)
                         / math.log(REL_MAX_DIST / max_exact)
                         * (nb - max_exact)).astype(np.int32)
    large = np.minimum(large, nb - 1)
    return ret + np.where(n < max_exact, n, large).astype(np.int32)


def mixer_a(q, k, v, q_g, k_g, row_ids, col_ids):
    b, s, hq, hd = q.shape
    hkv = k.shape[2]
    grp = hq // hkv
    q = axial_rope(rmsnorm(q, q_g), row_ids, col_ids) * (hd ** -0.5)
    k = axial_rope(rmsnorm(k, k_g), row_ids, col_ids)
    nblk = s // Q_BLOCK
    qb = q.reshape(b, nblk, Q_BLOCK, hkv, grp, hd).transpose(1, 0, 3, 4, 2, 5)
    kt = k.transpose(0, 2, 1, 3)
    vt = v.transpose(0, 2, 1, 3)

    def attend(qblk):
        sc = jnp.einsum('bkgqd,bksd->bkgqs', qblk, kt).astype(jnp.float32)
        pr = jax.nn.softmax(sc, axis=-1).astype(vt.dtype)
        return jnp.einsum('bkgqs,bksd->bkgqd', pr, vt)

    o = lax.map(attend, qb)
    return o.transpose(1, 0, 4, 2, 3, 5).reshape(b, s, hq * hd)


def dilated_group(q, k, v, bias_table, window, dilation):
    b, s, hh, hd = q.shape
    L = s // dilation
    W = window // (2 * dilation)
    qb_len = math.gcd(L, BAND_BLOCK)
    nb = L // qb_len
    kw_len = qb_len + 2 * W

    def to_sub(t):
        return t.reshape(b, L, dilation, hh, hd).transpose(0, 2, 3, 1, 4)

    off = np.arange(kw_len)[None, :] - W - np.arange(qb_len)[:, None]
    band = np.abs(off) <= W
    kpos = np.arange(nb)[:, None] * qb_len - W + np.arange(kw_len)[None, :]
    inb = (kpos >= 0) & (kpos < L)
    mask = band[None, :, :] & inb[:, None, :]
    bucket = t5_bucket(off * dilation)
    bias = bias_table[bucket].astype(jnp.float32).transpose(2, 0, 1)
    idx = np.arange(nb)[:, None] * qb_len + np.arange(kw_len)[None, :]

    qs = to_sub(q).reshape(b, dilation, hh, nb, qb_len, hd) * (hd ** -0.5)
    pad = ((0, 0), (0, 0), (0, 0), (W, W), (0, 0))
    ks = jnp.pad(to_sub(k), pad)[:, :, :, idx, :]
    vs = jnp.pad(to_sub(v), pad)[:, :, :, idx, :]
    sc = jnp.einsum('bchnqd,bchnkd->bchnqk', qs, ks).astype(jnp.float32) + bias[:, None]
    sc = jnp.where(mask, sc, NEG_INF)
    lse = jax.nn.logsumexp(sc, axis=-1)
    pr = jnp.exp(sc - lse[..., None]).astype(vs.dtype)
    o = jnp.einsum('bchnqk,bchnkd->bchnqd', pr, vs)
    o = o.reshape(b, dilation, hh, L, hd).transpose(0, 3, 1, 2, 4).reshape(b, s, hh, hd)
    lse = lse.reshape(b, dilation, hh, L).transpose(0, 3, 1, 2).reshape(b, s, hh)
    return o, lse


def mixer_b(q, k, v, rel_bias):
    b, s, _, hd = q.shape
    outs, lses = [], []
    for g, (window, dilation) in enumerate(DIL_PAIRS):
        sl = slice(g * N_HEADS_PER_DIL, (g + 1) * N_HEADS_PER_DIL)
        o, l = dilated_group(q[:, :, sl], k[:, :, sl], v[:, :, sl], rel_bias[:, sl], window, dilation)
        outs.append(o)
        lses.append(l)
    wts = jax.nn.softmax(jnp.stack(lses, axis=0), axis=0).astype(q.dtype)
    o = jnp.sum(wts[..., None] * jnp.stack(outs, axis=0), axis=0)
    return o.reshape(b, s, N_HEADS_PER_DIL * hd)


def _fwd_setup_inputs(seed: int = 0) -> dict:
    key = jax.random.key(seed)
    ks = jax.random.split(key, 20)
    f32 = jnp.float32

    def w(k, shape, fan_in):
        return jax.random.normal(k, shape, f32) * (fan_in ** -0.5)

    def gain(k, shape):
        return 1.0 + 0.05 * jax.random.normal(k, shape, f32)

    return {
        "x": jax.random.normal(ks[0], (BATCH, SEQ, D_MODEL), f32),
        "p": jax.random.normal(ks[1], (DEPTH, BATCH, SEQ, D_PLE), f32),
        "norm_mix_g": gain(ks[2], (DEPTH, D_MODEL)),
        "w_in": w(ks[3], (DEPTH, D_MODEL, D_IN_PROJ), D_MODEL),
        "b_gate": 0.02 * jax.random.normal(ks[4], (DEPTH, 2 * D_MODEL), f32),
        "q_norm_g": gain(ks[5], (DEPTH, HEAD_DIM_A)),
        "k_norm_g": gain(ks[6], (DEPTH, HEAD_DIM_A)),
        "rel_bias": 0.5 * jax.random.normal(ks[7], (N_REL_BUCKETS, N_HEADS_B), f32),
        "w_out_a": w(ks[8], (DEPTH, N_Q_HEADS_A * HEAD_DIM_A, D_MODEL), N_Q_HEADS_A * HEAD_DIM_A),
        "w_out_b": w(ks[9], (DEPTH, N_HEADS_PER_DIL * HEAD_DIM_B, D_MODEL), N_HEADS_PER_DIL * HEAD_DIM_B),
        "w_out": w(ks[10], (DEPTH, D_MODEL, D_MODEL), D_MODEL),
        "norm_mlp_g": gain(ks[11], (DEPTH, D_MODEL)),
        "w_ff1": w(ks[12], (DEPTH, D_MODEL, D_FF), D_MODEL),
        "w_ff2": w(ks[13], (DEPTH, D_FF, D_MODEL), D_FF),
        "norm_ple_g": gain(ks[14], (DEPTH, D_MODEL)),
        "w_ple_gate": w(ks[15], (DEPTH, D_MODEL, D_MODEL), D_MODEL),
        "w_ple": w(ks[16], (DEPTH, D_PLE, D_MODEL), D_PLE),
        "final_norm_g": gain(ks[17], (D_MODEL,)),
    }


def _fwd_reference(x, p, norm_mix_g, w_in, b_gate, q_norm_g, k_norm_g, rel_bias,
              w_out_a, w_out_b, w_out, norm_mlp_g, w_ff1, w_ff2,
              norm_ple_g, w_ple_gate, w_ple, final_norm_g):
    b, s, _ = x.shape
    rows = s // GRID_W
    row_ids = jnp.repeat(jnp.arange(rows, dtype=jnp.int32), GRID_W)
    col_ids = jnp.arange(s, dtype=jnp.int32) % GRID_W
    split_at = list(np.cumsum(SPLIT_SIZES)[:-1])

    for i in range(DEPTH):
        h = rmsnorm(x, norm_mix_g[i])
        z = h @ w_in[i]
        qa, ka, va, qb, kb, vb, ga, gb = jnp.split(z, split_at, axis=-1)
        qa = qa.reshape(b, s, N_Q_HEADS_A, HEAD_DIM_A)
        ka = ka.reshape(b, s, N_KV_HEADS_A, HEAD_DIM_A)
        va = va.reshape(b, s, N_KV_HEADS_A, HEAD_DIM_A)
        qb = qb.reshape(b, s, N_HEADS_B, HEAD_DIM_B)
        kb = kb.reshape(b, s, N_HEADS_B, HEAD_DIM_B)
        vb = vb.reshape(b, s, N_HEADS_B, HEAD_DIM_B)

        y_a = mixer_a(qa, ka, va, q_norm_g[i], k_norm_g[i], row_ids, col_ids) @ w_out_a[i]
        y_b = mixer_b(qb, kb, vb, rel_bias) @ w_out_b[i]
        gate_a = jax.nn.sigmoid(ga + b_gate[i, :D_MODEL])
        gate_b = jax.nn.sigmoid(gb + b_gate[i, D_MODEL:])
        x = x + (gate_a * y_a + gate_b * y_b) @ w_out[i]

        h = rmsnorm(x, norm_mlp_g[i])
        x = x + jnp.square(jax.nn.relu(h @ w_ff1[i])) @ w_ff2[i]

        gate_p = jax.nn.sigmoid(rmsnorm(x, norm_ple_g[i]) @ w_ple_gate[i])
        x = x + gate_p * (p[i] @ w_ple[i])

    return rmsnorm(x, final_norm_g)


import jax as _jax
import jax.numpy as _jnp

TWIN_FORMAT = 'train_step'
FWD_PARAMS = ['x', 'p', 'norm_mix_g', 'w_in', 'b_gate', 'q_norm_g', 'k_norm_g', 'rel_bias', 'w_out_a', 'w_out_b', 'w_out', 'norm_mlp_g', 'w_ff1', 'w_ff2', 'norm_ple_g', 'w_ple_gate', 'w_ple', 'final_norm_g']
TWIN_WEIGHTS = ['norm_mix_g', 'w_in', 'b_gate', 'q_norm_g', 'k_norm_g', 'rel_bias', 'w_out_a', 'w_out_b', 'w_out', 'norm_mlp_g', 'w_ff1', 'w_ff2', 'norm_ple_g', 'w_ple_gate', 'w_ple', 'final_norm_g']
TWIN_DIFF_INPUT = 'x'
TWIN_INPUTS = ['x', 'p', 'norm_mix_g', 'w_in', 'b_gate', 'q_norm_g', 'k_norm_g', 'rel_bias', 'w_out_a', 'w_out_b', 'w_out', 'norm_mlp_g', 'w_ff1', 'w_ff2', 'norm_ple_g', 'w_ple_gate', 'w_ple', 'final_norm_g', 'loss_target', 'm_norm_mix_g', 'm_w_in', 'm_b_gate', 'm_q_norm_g', 'm_k_norm_g', 'm_rel_bias', 'm_w_out_a', 'm_w_out_b', 'm_w_out', 'm_norm_mlp_g', 'm_w_ff1', 'm_w_ff2', 'm_norm_ple_g', 'm_w_ple_gate', 'm_w_ple', 'm_final_norm_g', 'v_norm_mix_g', 'v_w_in', 'v_b_gate', 'v_q_norm_g', 'v_k_norm_g', 'v_rel_bias', 'v_w_out_a', 'v_w_out_b', 'v_w_out', 'v_norm_mlp_g', 'v_w_ff1', 'v_w_ff2', 'v_norm_ple_g', 'v_w_ple_gate', 'v_w_ple', 'v_final_norm_g']
TWIN_OUTPUTS = ['loss', 'grad_x', 'grad_norm_mix_g', 'grad_w_in', 'grad_b_gate', 'grad_q_norm_g', 'grad_k_norm_g', 'grad_rel_bias', 'grad_w_out_a', 'grad_w_out_b', 'grad_w_out', 'grad_norm_mlp_g', 'grad_w_ff1', 'grad_w_ff2', 'grad_norm_ple_g', 'grad_w_ple_gate', 'grad_w_ple', 'grad_final_norm_g', 'delta_norm_mix_g', 'delta_w_in', 'delta_b_gate', 'delta_q_norm_g', 'delta_k_norm_g', 'delta_rel_bias', 'delta_w_out_a', 'delta_w_out_b', 'delta_w_out', 'delta_norm_mlp_g', 'delta_w_ff1', 'delta_w_ff2', 'delta_norm_ple_g', 'delta_w_ple_gate', 'delta_w_ple', 'delta_final_norm_g', 'new_m_norm_mix_g', 'new_m_w_in', 'new_m_b_gate', 'new_m_q_norm_g', 'new_m_k_norm_g', 'new_m_rel_bias', 'new_m_w_out_a', 'new_m_w_out_b', 'new_m_w_out', 'new_m_norm_mlp_g', 'new_m_w_ff1', 'new_m_w_ff2', 'new_m_norm_ple_g', 'new_m_w_ple_gate', 'new_m_w_ple', 'new_m_final_norm_g', 'new_v_norm_mix_g', 'new_v_w_in', 'new_v_b_gate', 'new_v_q_norm_g', 'new_v_k_norm_g', 'new_v_rel_bias', 'new_v_w_out_a', 'new_v_w_out_b', 'new_v_w_out', 'new_v_norm_mlp_g', 'new_v_w_ff1', 'new_v_w_ff2', 'new_v_norm_ple_g', 'new_v_w_ple_gate', 'new_v_w_ple', 'new_v_final_norm_g']
TWIN_LEAF_KINDS = {'loss': 'loss', 'grad_x': 'grad_x', 'grad_norm_mix_g': 'grad_w', 'grad_w_in': 'grad_w', 'grad_b_gate': 'grad_w', 'grad_q_norm_g': 'grad_w', 'grad_k_norm_g': 'grad_w', 'grad_rel_bias': 'grad_w', 'grad_w_out_a': 'grad_w', 'grad_w_out_b': 'grad_w', 'grad_w_out': 'grad_w', 'grad_norm_mlp_g': 'grad_w', 'grad_w_ff1': 'grad_w', 'grad_w_ff2': 'grad_w', 'grad_norm_ple_g': 'grad_w', 'grad_w_ple_gate': 'grad_w', 'grad_w_ple': 'grad_w', 'grad_final_norm_g': 'grad_w', 'delta_norm_mix_g': 'delta_w', 'delta_w_in': 'delta_w', 'delta_b_gate': 'delta_w', 'delta_q_norm_g': 'delta_w', 'delta_k_norm_g': 'delta_w', 'delta_rel_bias': 'delta_w', 'delta_w_out_a': 'delta_w', 'delta_w_out_b': 'delta_w', 'delta_w_out': 'delta_w', 'delta_norm_mlp_g': 'delta_w', 'delta_w_ff1': 'delta_w', 'delta_w_ff2': 'delta_w', 'delta_norm_ple_g': 'delta_w', 'delta_w_ple_gate': 'delta_w', 'delta_w_ple': 'delta_w', 'delta_final_norm_g': 'delta_w', 'new_m_norm_mix_g': 'new_m', 'new_m_w_in': 'new_m', 'new_m_b_gate': 'new_m', 'new_m_q_norm_g': 'new_m', 'new_m_k_norm_g': 'new_m', 'new_m_rel_bias': 'new_m', 'new_m_w_out_a': 'new_m', 'new_m_w_out_b': 'new_m', 'new_m_w_out': 'new_m', 'new_m_norm_mlp_g': 'new_m', 'new_m_w_ff1': 'new_m', 'new_m_w_ff2': 'new_m', 'new_m_norm_ple_g': 'new_m', 'new_m_w_ple_gate': 'new_m', 'new_m_w_ple': 'new_m', 'new_m_final_norm_g': 'new_m', 'new_v_norm_mix_g': 'new_v', 'new_v_w_in': 'new_v', 'new_v_b_gate': 'new_v', 'new_v_q_norm_g': 'new_v', 'new_v_k_norm_g': 'new_v', 'new_v_rel_bias': 'new_v', 'new_v_w_out_a': 'new_v', 'new_v_w_out_b': 'new_v', 'new_v_w_out': 'new_v', 'new_v_norm_mlp_g': 'new_v', 'new_v_w_ff1': 'new_v', 'new_v_w_ff2': 'new_v', 'new_v_norm_ple_g': 'new_v', 'new_v_w_ple_gate': 'new_v', 'new_v_w_ple': 'new_v', 'new_v_final_norm_g': 'new_v'}


def _forward(args):
    return _fwd_reference(*[args[k] for k in FWD_PARAMS])


def _output_shape():
    def fwd():
        inp = _fwd_setup_inputs(0)
        return _fwd_reference(*[inp[k] for k in FWD_PARAMS])
    out = _jax.eval_shape(fwd)
    return out.shape, out.dtype

N_MICROBATCH = 1
ADAM_LR = 0.001
ADAM_B1 = 0.9
ADAM_B2 = 0.999
ADAM_EPS = 1e-08
ADAM_WD = 0.01
ADAM_STEP = 10
PER_EXAMPLE_BATCH_AXIS = {'x': 0, 'p': 1, 'loss_target': 0}
SHARED_INPUTS = []
_WEIGHT_DTYPES = {'norm_mix_g': _jnp.float32, 'w_in': _jnp.float32, 'b_gate': _jnp.float32, 'q_norm_g': _jnp.float32, 'k_norm_g': _jnp.float32, 'rel_bias': _jnp.float32, 'w_out_a': _jnp.float32, 'w_out_b': _jnp.float32, 'w_out': _jnp.float32, 'norm_mlp_g': _jnp.float32, 'w_ff1': _jnp.float32, 'w_ff2': _jnp.float32, 'norm_ple_g': _jnp.float32, 'w_ple_gate': _jnp.float32, 'w_ple': _jnp.float32, 'final_norm_g': _jnp.float32}
MOMENT_SCALE = {'norm_mix_g': 4.369205e-02, 'w_in': 1.756080e-02, 'b_gate': 6.690241e-03, 'q_norm_g': 4.246981e-02, 'k_norm_g': 4.415138e-02, 'rel_bias': 2.458835e-02, 'w_out_a': 1.527472e-02, 'w_out_b': 1.879117e-02, 'w_out': 2.411713e-02, 'norm_mlp_g': 2.237549e-01, 'w_ff1': 1.102502e-01, 'w_ff2': 4.167861e-01, 'norm_ple_g': 4.779904e-02, 'w_ple_gate': 5.258414e-02, 'w_ple': 8.465824e-02, 'final_norm_g': 6.461908e+01}


def _to_microbatches(a, axis):
    t = _jnp.moveaxis(a, axis, 0)
    t = t.reshape((N_MICROBATCH, t.shape[0] // N_MICROBATCH) + t.shape[1:])
    return _jnp.moveaxis(t, 1, axis + 1)


def setup_inputs(seed: int = 0) -> dict:
    inp = _fwd_setup_inputs(seed)
    key = _jax.random.fold_in(_jax.random.key(seed), 7919)
    shape, _ = _output_shape()
    out = dict(inp)
    out["loss_target"] = _jax.random.normal(_jax.random.fold_in(key, 0), shape, _jnp.float32)
    for i, name in enumerate(TWIN_WEIGHTS):
        w = inp[name].astype(_jnp.float32)
        if MOMENT_SCALE is None:
            s = _jnp.sqrt(_jnp.mean(_jnp.square(w)) + 1e-30)
        else:
            s = MOMENT_SCALE[name]
        km, kv = _jax.random.split(_jax.random.fold_in(key, i + 1))
        out[name] = w
        out["m_" + name] = s * _jax.random.normal(km, w.shape, _jnp.float32)
        out["v_" + name] = (s * s) * _jax.random.uniform(kv, w.shape, _jnp.float32, 0.5, 1.5)
    if N_MICROBATCH > 1:
        for name, axis in PER_EXAMPLE_BATCH_AXIS.items():
            out[name] = _to_microbatches(out[name], axis)
    return {'x': out['x'], 'p': out['p'], 'norm_mix_g': out['norm_mix_g'], 'w_in': out['w_in'], 'b_gate': out['b_gate'], 'q_norm_g': out['q_norm_g'], 'k_norm_g': out['k_norm_g'], 'rel_bias': out['rel_bias'], 'w_out_a': out['w_out_a'], 'w_out_b': out['w_out_b'], 'w_out': out['w_out'], 'norm_mlp_g': out['norm_mlp_g'], 'w_ff1': out['w_ff1'], 'w_ff2': out['w_ff2'], 'norm_ple_g': out['norm_ple_g'], 'w_ple_gate': out['w_ple_gate'], 'w_ple': out['w_ple'], 'final_norm_g': out['final_norm_g'], 'loss_target': out['loss_target'], 'm_norm_mix_g': out['m_norm_mix_g'], 'm_w_in': out['m_w_in'], 'm_b_gate': out['m_b_gate'], 'm_q_norm_g': out['m_q_norm_g'], 'm_k_norm_g': out['m_k_norm_g'], 'm_rel_bias': out['m_rel_bias'], 'm_w_out_a': out['m_w_out_a'], 'm_w_out_b': out['m_w_out_b'], 'm_w_out': out['m_w_out'], 'm_norm_mlp_g': out['m_norm_mlp_g'], 'm_w_ff1': out['m_w_ff1'], 'm_w_ff2': out['m_w_ff2'], 'm_norm_ple_g': out['m_norm_ple_g'], 'm_w_ple_gate': out['m_w_ple_gate'], 'm_w_ple': out['m_w_ple'], 'm_final_norm_g': out['m_final_norm_g'], 'v_norm_mix_g': out['v_norm_mix_g'], 'v_w_in': out['v_w_in'], 'v_b_gate': out['v_b_gate'], 'v_q_norm_g': out['v_q_norm_g'], 'v_k_norm_g': out['v_k_norm_g'], 'v_rel_bias': out['v_rel_bias'], 'v_w_out_a': out['v_w_out_a'], 'v_w_out_b': out['v_w_out_b'], 'v_w_out': out['v_w_out'], 'v_norm_mlp_g': out['v_norm_mlp_g'], 'v_w_ff1': out['v_w_ff1'], 'v_w_ff2': out['v_w_ff2'], 'v_norm_ple_g': out['v_norm_ple_g'], 'v_w_ple_gate': out['v_w_ple_gate'], 'v_w_ple': out['v_w_ple'], 'v_final_norm_g': out['v_final_norm_g']}


def _loss(weights, diff, rest, loss_target):
    with _jax.named_scope("forward"):
        args = {**rest, TWIN_DIFF_INPUT: diff, **{k: w.astype(_WEIGHT_DTYPES[k]) for k, w in weights.items()}}
        y = _forward(args)
    with _jax.named_scope("loss_head"):
        err = _jnp.square(y.astype(_jnp.float32) - loss_target)
        return 0.5 * _jnp.sum(_jnp.mean(err, axis=-1)) if err.ndim else 0.5 * err


def _adamw(w, g, m, v):
    m = ADAM_B1 * m + (1.0 - ADAM_B1) * g
    v = ADAM_B2 * v + (1.0 - ADAM_B2) * _jnp.square(g)
    m_hat = m / (1.0 - ADAM_B1 ** ADAM_STEP)
    v_hat = v / (1.0 - ADAM_B2 ** ADAM_STEP)
    delta = -ADAM_LR * (m_hat / (_jnp.sqrt(v_hat) + ADAM_EPS) + ADAM_WD * w)
    return delta, m, v


def reference(x, p, norm_mix_g, w_in, b_gate, q_norm_g, k_norm_g, rel_bias, w_out_a, w_out_b, w_out, norm_mlp_g, w_ff1, w_ff2, norm_ple_g, w_ple_gate, w_ple, final_norm_g, loss_target, m_norm_mix_g, m_w_in, m_b_gate, m_q_norm_g, m_k_norm_g, m_rel_bias, m_w_out_a, m_w_out_b, m_w_out, m_norm_mlp_g, m_w_ff1, m_w_ff2, m_norm_ple_g, m_w_ple_gate, m_w_ple, m_final_norm_g, v_norm_mix_g, v_w_in, v_b_gate, v_q_norm_g, v_k_norm_g, v_rel_bias, v_w_out_a, v_w_out_b, v_w_out, v_norm_mlp_g, v_w_ff1, v_w_ff2, v_norm_ple_g, v_w_ple_gate, v_w_ple, v_final_norm_g):
    given = dict(x=x, p=p, norm_mix_g=norm_mix_g, w_in=w_in, b_gate=b_gate, q_norm_g=q_norm_g, k_norm_g=k_norm_g, rel_bias=rel_bias, w_out_a=w_out_a, w_out_b=w_out_b, w_out=w_out, norm_mlp_g=norm_mlp_g, w_ff1=w_ff1, w_ff2=w_ff2, norm_ple_g=norm_ple_g, w_ple_gate=w_ple_gate, w_ple=w_ple, final_norm_g=final_norm_g, loss_target=loss_target, m_norm_mix_g=m_norm_mix_g, m_w_in=m_w_in, m_b_gate=m_b_gate, m_q_norm_g=m_q_norm_g, m_k_norm_g=m_k_norm_g, m_rel_bias=m_rel_bias, m_w_out_a=m_w_out_a, m_w_out_b=m_w_out_b, m_w_out=m_w_out, m_norm_mlp_g=m_norm_mlp_g, m_w_ff1=m_w_ff1, m_w_ff2=m_w_ff2, m_norm_ple_g=m_norm_ple_g, m_w_ple_gate=m_w_ple_gate, m_w_ple=m_w_ple, m_final_norm_g=m_final_norm_g, v_norm_mix_g=v_norm_mix_g, v_w_in=v_w_in, v_b_gate=v_b_gate, v_q_norm_g=v_q_norm_g, v_k_norm_g=v_k_norm_g, v_rel_bias=v_rel_bias, v_w_out_a=v_w_out_a, v_w_out_b=v_w_out_b, v_w_out=v_w_out, v_norm_mlp_g=v_norm_mlp_g, v_w_ff1=v_w_ff1, v_w_ff2=v_w_ff2, v_norm_ple_g=v_norm_ple_g, v_w_ple_gate=v_w_ple_gate, v_w_ple=v_w_ple, v_final_norm_g=v_final_norm_g)
    weights = {n: given[n] for n in TWIN_WEIGHTS}
    shared = {n: given[n] for n in SHARED_INPUTS}
    per_example = {n: given[n] for n in ['x', 'p']}
    grad_fn = _jax.value_and_grad(_loss, argnums=(0, 1))

    def one_microbatch(ex, loss_target):
        ex = dict(ex)
        diff = ex.pop(TWIN_DIFF_INPUT)
        return grad_fn(weights, diff, {**shared, **ex}, loss_target)

    if N_MICROBATCH == 1:
        loss, (grad_w, grad_x) = one_microbatch(per_example, given["loss_target"])
    else:
        def body(carry, xs):
            loss_sum, grad_sum = carry
            l_k, (gw_k, gx_k) = one_microbatch(xs[0], xs[1])
            with _jax.named_scope("update"):
                return (loss_sum + l_k, _jax.tree.map(_jnp.add, grad_sum, gw_k)), gx_k

        init = (_jnp.zeros((), _jnp.float32), _jax.tree.map(_jnp.zeros_like, weights))
        (loss, grad_w), grad_x = _jax.lax.scan(body, init, (per_example, given["loss_target"]))
    with _jax.named_scope("update"):
        delta_w, new_m, new_v = {}, {}, {}
        for n in TWIN_WEIGHTS:
            delta_w[n], new_m[n], new_v[n] = _adamw(weights[n], grad_w[n], given["m_" + n], given["v_" + n])
    return (loss, grad_x, *[grad_w[n] for n in TWIN_WEIGHTS], *[delta_w[n] for n in TWIN_WEIGHTS],
            *[new_m[n] for n in TWIN_WEIGHTS], *[new_v[n] for n in TWIN_WEIGHTS])
```

```python
import functools
import math

import numpy as np
import jax
import jax.numpy as jnp
from jax import lax
from jax.experimental import pallas as pl
from jax.experimental.pallas import tpu as pltpu

F32 = jnp.float32
BF16 = jnp.bfloat16
MESH = pl.DeviceIdType.MESH

NORM_EPS = 1e-6
NEG_INF = -1e30
GRID_W = 64
ROPE_THETA = 10000.0
HEAD_DIM_A = 128
N_Q_HEADS_A = 8
N_KV_HEADS_A = 2
Q_PER_KV = N_Q_HEADS_A // N_KV_HEADS_A
HEAD_DIM_B = 64
N_HEADS_PER_DIL = 4
DILATIONS = (1, 4, 16)
BAND = 64
N_REL_BUCKETS = 32
REL_MAX_DIST = 1024
QA_W = N_Q_HEADS_A * HEAD_DIM_A
KA_W = N_KV_HEADS_A * HEAD_DIM_A
GB_W = N_HEADS_PER_DIL * HEAD_DIM_B
QB_W = GB_W * len(DILATIONS)
OFF_QA, OFF_KA, OFF_VA = 0, QA_W, QA_W + KA_W
OFF_QB = QA_W + 2 * KA_W
OFF_KB = OFF_QB + QB_W
OFF_VB = OFF_KB + QB_W
OFF_GA = OFF_VB + QB_W
N_DEV = 8
LANES = 128
VMEM_LIMIT = 56 * 2 ** 20

ADAM_LR, ADAM_B1, ADAM_B2, ADAM_EPS, ADAM_WD, ADAM_STEP = 0.001, 0.9, 0.999, 1e-08, 0.01, 10


def _cparams(sem):
    return pltpu.CompilerParams(dimension_semantics=sem, vmem_limit_bytes=VMEM_LIMIT)


def _resident(shape):
    nd = len(shape)
    return pl.BlockSpec(shape, lambda *_: (0,) * nd, pipeline_mode=pl.Buffered(1))


def _acc_spec(shape):
    nd = len(shape)
    return pl.BlockSpec(shape, lambda *_: (0,) * nd)


def _rows(tb, c):
    return pl.BlockSpec((tb, c), lambda i: (i, 0))


def _rows3(tb, c):
    return pl.BlockSpec((3, tb, c), lambda i: (0, i, 0))


def _dot_nt(a, b):
    return lax.dot_general(a, b, (((1,), (1,)), ((), ())), preferred_element_type=F32)


def _dot_nn(a, b):
    return lax.dot_general(a, b, (((1,), (0,)), ((), ())), preferred_element_type=F32)


def _dot_tn(a, b):
    return lax.dot_general(a, b, (((0,), (0,)), ((), ())), preferred_element_type=F32)


def _rstd(x):
    return lax.rsqrt(jnp.mean(x * x, axis=-1, keepdims=True) + NORM_EPS)


def _rms_bwd(dy, n, r, g):
    dn = dy * g
    return r * (dn - n * jnp.mean(dn * n, axis=-1, keepdims=True))


def _colsum(v):
    return jnp.sum(v, axis=0, keepdims=True)


def _sigmoid(v):
    return 1.0 / (1.0 + jnp.exp(-v))


def _rope_fwd(n, c, s1, s2):
    return n * c + pltpu.roll(n, 32, 1) * s1 + pltpu.roll(n, 96, 1) * s2


def _rope_bwd(d, c, s1, s2):
    return d * c + pltpu.roll(d * s1, 96, 1) + pltpu.roll(d * s2, 32, 1)


def _rope_tables(s):
    half = HEAD_DIM_A // 2
    inv = jnp.power(ROPE_THETA, -jnp.arange(0, half, 2, dtype=F32) / half)
    t = jnp.arange(s, dtype=jnp.int32)
    ang_r = (t // GRID_W).astype(F32)[:, None] * inv[None, :]
    ang_c = (t % GRID_W).astype(F32)[:, None] * inv[None, :]
    cr, sr, cc, sc = jnp.cos(ang_r), jnp.sin(ang_r), jnp.cos(ang_c), jnp.sin(ang_c)
    z = jnp.zeros_like(sr)
    cos = jnp.concatenate([cr, cr, cc, cc], axis=1)
    s1 = jnp.concatenate([z, sr, z, sc], axis=1)
    s2 = jnp.concatenate([-sr, z, -sc, z], axis=1)
    return cos, s1, s2


def _my_place():
    return lax.axis_index("x"), lax.axis_index("y"), lax.axis_index("c")


def _all_gather(shards):
    nw = len(shards)

    def body(*refs):
        ins, outs = refs[:nw], refs[nw:2 * nw]
        send_sems, recv_sems, local_sems = refs[2 * nw:]
        x, y, c = _my_place()
        me, sibling = (x, y, c), (x, y, 1 - c)
        chips = [(1 - x, y), (x, 1 - y), (1 - x, 1 - y)]

        def rows(w, px, py, pc):
            n = ins[w].shape[0]
            return outs[w].at[pl.ds(pl.multiple_of((4 * px + 2 * py + pc) * n, 16), n), :]

        def copy(w, k, block, to, src=None):
            return pltpu.make_async_remote_copy(
                src_ref=rows(w, *block) if src is None else src, dst_ref=rows(w, *block),
                send_sem=send_sems.at[w, k], recv_sem=recv_sems.at[w, k], device_id=to, device_id_type=MESH)

        mine = [pltpu.make_async_copy(ins[w], rows(w, *me), local_sems.at[w]) for w in range(nw)]
        for cp in mine:
            cp.start()
        first = []
        for w in range(nw):
            first.append(copy(w, 0, me, sibling, src=ins[w]))
            first += [copy(w, 1 + j, me, (*chip, c), src=ins[w]) for j, chip in enumerate(chips)]
        for cp in first:
            cp.start()
        passed = []
        for j, chip in enumerate(chips):
            for w in range(nw):
                copy(w, 1 + j, (*chip, c), me).wait_recv()
                fwd = copy(w, 4 + j, (*chip, c), sibling)
                fwd.start()
                passed.append(fwd)
        for w in range(nw):
            copy(w, 0, sibling, me).wait_recv()
        for j, chip in enumerate(chips):
            for w in range(nw):
                copy(w, 4 + j, (*chip, 1 - c), me).wait_recv()
        for cp in first + passed:
            cp.wait_send()
        for cp in mine:
            cp.wait()

    any_spec = pl.BlockSpec(memory_space=pl.ANY)
    return pl.pallas_call(
        body, name="weights_all_gather",
        out_shape=[jax.ShapeDtypeStruct((N_DEV * s.shape[0], s.shape[1]), s.dtype) for s in shards],
        in_specs=[any_spec] * nw, out_specs=[any_spec] * nw,
        scratch_shapes=[pltpu.SemaphoreType.DMA((nw, 7)), pltpu.SemaphoreType.DMA((nw, 7)),
                        pltpu.SemaphoreType.DMA((nw,))],
    )(*shards)


_FLIPS = [(fx, fy, fc) for fx in (0, 1) for fy in (0, 1) for fc in (0, 1)][1:]


def _scatter_blocks(partials):
    nw = len(partials)

    def body(*refs):
        ins, outs = refs[:nw], refs[nw:2 * nw]
        send_sems, recv_sems, local_sems = refs[2 * nw:]
        x, y, c = _my_place()
        my_idx = 4 * x + 2 * y + c

        def block(w, idx):
            n = outs[w].shape[1]
            return ins[w].at[pl.ds(pl.multiple_of(idx * n, 16), n), :]

        def peer(k):
            fx, fy, fc = _FLIPS[k]
            return (1 - x if fx else x, 1 - y if fy else y, 1 - c if fc else c)

        def copy(w, k):
            to = peer(k)
            to_idx = 4 * to[0] + 2 * to[1] + to[2]
            return pltpu.make_async_remote_copy(
                src_ref=block(w, to_idx), dst_ref=outs[w].at[my_idx],
                send_sem=send_sems.at[w, k], recv_sem=recv_sems.at[w, k], device_id=to, device_id_type=MESH)

        def arrival(w, k):
            frm = peer(k)
            frm_idx = 4 * frm[0] + 2 * frm[1] + frm[2]
            return pltpu.make_async_remote_copy(
                src_ref=block(w, my_idx), dst_ref=outs[w].at[frm_idx],
                send_sem=send_sems.at[w, k], recv_sem=recv_sems.at[w, k], device_id=(x, y, c), device_id_type=MESH)

        mine = [pltpu.make_async_copy(block(w, my_idx), outs[w].at[my_idx], local_sems.at[w]) for w in range(nw)]
        for cp in mine:
            cp.start()
        sends = [copy(w, k) for k in range(7) for w in range(nw)]
        for cp in sends:
            cp.start()
        for k in range(7):
            for w in range(nw):
                arrival(w, k).wait_recv()
        for cp in sends:
            cp.wait_send()
        for cp in mine:
            cp.wait()

    any_spec = pl.BlockSpec(memory_space=pl.ANY)
    return pl.pallas_call(
        body, name="grads_scatter",
        out_shape=[jax.ShapeDtypeStruct((N_DEV, p.shape[0] // N_DEV, p.shape[1]), p.dtype) for p in partials],
        in_specs=[any_spec] * nw, out_specs=[any_spec] * nw,
        scratch_shapes=[pltpu.SemaphoreType.DMA((nw, 7)), pltpu.SemaphoreType.DMA((nw, 7)),
                        pltpu.SemaphoreType.DMA((nw,))],
    )(*partials)


def _small_all_gather(v):
    def body(v_ref, out_ref, send_sems, recv_sems):
        x, y, c = _my_place()
        my_idx = 4 * x + 2 * y + c
        out_ref[my_idx] = v_ref[...]
        sends = []
        for k, (fx, fy, fc) in enumerate(_FLIPS):
            to = (1 - x if fx else x, 1 - y if fy else y, 1 - c if fc else c)
            sends.append(pltpu.make_async_remote_copy(
                src_ref=v_ref, dst_ref=out_ref.at[my_idx], send_sem=send_sems.at[k], recv_sem=recv_sems.at[k],
                device_id=to, device_id_type=MESH))
        for cp in sends:
            cp.start()
        for k, (fx, fy, fc) in enumerate(_FLIPS):
            frm_idx = 4 * (1 - x if fx else x) + 2 * (1 - y if fy else y) + (1 - c if fc else c)
            pltpu.make_async_remote_copy(
                src_ref=v_ref, dst_ref=out_ref.at[frm_idx], send_sem=send_sems.at[k], recv_sem=recv_sems.at[k],
                device_id=(x, y, c), device_id_type=MESH).wait_recv()
        for cp in sends:
            cp.wait_send()

    vm = pl.BlockSpec(memory_space=pltpu.VMEM)
    return pl.pallas_call(
        body, name="small_all_gather", out_shape=jax.ShapeDtypeStruct((N_DEV,) + v.shape, v.dtype),
        in_specs=[vm], out_specs=vm,
        scratch_shapes=[pltpu.SemaphoreType.DMA((7,)), pltpu.SemaphoreType.DMA((7,))],
    )(v)


def _in_proj(x, tabs, w_in_t, g_mix, b_gate, q_g, k_g, tb):
    s, d = x.shape
    n_gate_chunks = d // 256
    q_scale = HEAD_DIM_A ** -0.5
    b_scale = HEAD_DIM_B ** -0.5

    def body(x_ref, c_ref, s1_ref, s2_ref, w_ref, gmix_ref, bg_ref, qg_ref, kg_ref,
             h1_ref, qraw_ref, kraw_ref, qrot_ref, krot_ref, va_ref, qb_ref, kb_ref, vb_ref, ga_ref, gb_ref):
        xv = x_ref[...]
        hb = (xv * _rstd(xv) * gmix_ref[...]).astype(BF16)
        h1_ref[...] = hb
        cos, s1, s2 = c_ref[...], s1_ref[...], s2_ref[...]

        def proj(lo, width):
            return _dot_nt(hb, w_ref[lo:lo + width, :])

        def norm_rope(z, g):
            return _rope_fwd(z * _rstd(z) * g, cos, s1, s2)

        for j in range(QA_W // 256):
            z = proj(OFF_QA + 256 * j, 256)
            qraw_ref[:, 256 * j:256 * j + 256] = z
            for hh in range(2):
                lo = 256 * j + 128 * hh
                qrot_ref[:, lo:lo + 128] = (norm_rope(z[:, 128 * hh:128 * hh + 128], qg_ref[...]) * q_scale).astype(BF16)
        z = proj(OFF_KA, 256)
        kraw_ref[...] = z
        for hh in range(2):
            krot_ref[:, 128 * hh:128 * hh + 128] = norm_rope(z[:, 128 * hh:128 * hh + 128], kg_ref[...]).astype(BF16)
        va_ref[...] = proj(OFF_VA, 256).astype(BF16)
        for g in range(3):
            qb_ref[g] = (proj(OFF_QB + GB_W * g, GB_W) * b_scale).astype(BF16)
            kb_ref[g] = proj(OFF_KB + GB_W * g, GB_W).astype(BF16)
            vb_ref[g] = proj(OFF_VB + GB_W * g, GB_W).astype(BF16)
        for j in range(n_gate_chunks):
            sl = slice(256 * j, 256 * j + 256)
            ga_ref[:, sl] = _sigmoid(proj(OFF_GA + 256 * j, 256) + bg_ref[:, sl])
            gb_ref[:, sl] = _sigmoid(proj(OFF_GA + d + 256 * j, 256) + bg_ref[:, d + 256 * j:d + 256 * j + 256])

    sd = jax.ShapeDtypeStruct
    outs = [sd((s, d), BF16), sd((s, QA_W), F32), sd((s, KA_W), F32), sd((s, QA_W), BF16), sd((s, KA_W), BF16),
            sd((s, KA_W), BF16), sd((3, s, GB_W), BF16), sd((3, s, GB_W), BF16), sd((3, s, GB_W), BF16),
            sd((s, d), F32), sd((s, d), F32)]
    out_specs = [_rows(tb, d), _rows(tb, QA_W), _rows(tb, KA_W), _rows(tb, QA_W), _rows(tb, KA_W), _rows(tb, KA_W),
                 _rows3(tb, GB_W), _rows3(tb, GB_W), _rows3(tb, GB_W), _rows(tb, d), _rows(tb, d)]
    in_specs = [_rows(tb, d), _rows(tb, LANES), _rows(tb, LANES), _rows(tb, LANES), _resident(w_in_t.shape),
                _resident(g_mix.shape), _resident(b_gate.shape), _resident(q_g.shape), _resident(k_g.shape)]
    return pl.pallas_call(body, name="in_proj", grid=(s // tb,), in_specs=in_specs, out_specs=out_specs,
                          out_shape=outs, compiler_params=_cparams(("arbitrary",)))(
        x, *tabs, w_in_t, g_mix, b_gate, q_g, k_g)


def _attn_a_fwd(qrot, krot, va, tq, tk):
    s = qrot.shape[0]
    n_kv = s // tk
    gw = Q_PER_KV * HEAD_DIM_A

    def body(q_ref, k_ref, v_ref, o_ref, lse_ref):
        q4 = jnp.concatenate([q_ref[:, 128 * h:128 * h + 128] for h in range(Q_PER_KV)], axis=0)

        def step(j, carry):
            m, l, acc = carry
            sl = pl.ds(pl.multiple_of(j * tk, tk), tk)
            kj, vj = k_ref[sl, :], v_ref[sl, :]
            sc = _dot_nt(q4, kj)
            m_new = jnp.maximum(m, jnp.max(sc, axis=-1, keepdims=True))
            p = jnp.exp(sc - m_new)
            alpha = jnp.exp(m - m_new)
            l = alpha * l + jnp.sum(p, axis=-1, keepdims=True)
            acc = alpha * acc + _dot_nn(p.astype(BF16), vj)
            return m_new, l, acc

        rows = Q_PER_KV * tq
        m, l, acc = lax.fori_loop(0, n_kv, step, (jnp.full((rows, 1), NEG_INF, F32), jnp.zeros((rows, 1), F32),
                                                  jnp.zeros((rows, HEAD_DIM_A), F32)))
        o = acc / l
        lse = m + jnp.log(l)
        for h in range(Q_PER_KV):
            o_ref[:, 128 * h:128 * h + 128] = o[h * tq:(h + 1) * tq].astype(BF16)
            lse_ref[0, :, h:h + 1] = lse[h * tq:(h + 1) * tq]

    return pl.pallas_call(
        body, name="attn_a_fwd", grid=(N_KV_HEADS_A, s // tq),
        in_specs=[pl.BlockSpec((tq, gw), lambda g, i: (i, g)),
                  pl.BlockSpec((s, HEAD_DIM_A), lambda g, i: (0, g)),
                  pl.BlockSpec((s, HEAD_DIM_A), lambda g, i: (0, g))],
        out_specs=[pl.BlockSpec((tq, gw), lambda g, i: (i, g)),
                   pl.BlockSpec((1, tq, Q_PER_KV), lambda g, i: (g, i, 0))],
        out_shape=[jax.ShapeDtypeStruct((s, QA_W), BF16), jax.ShapeDtypeStruct((N_KV_HEADS_A, s, Q_PER_KV), F32)],
        compiler_params=_cparams(("arbitrary", "arbitrary")))(qrot, krot, va)


def _attn_a_bwd(qrot, krot, va, oa, doa, lse, tq, tk):
    s = qrot.shape[0]
    n_kv = s // tk
    gw = Q_PER_KV * HEAD_DIM_A

    def body(q_ref, do_ref, o_ref, lse_ref, k_ref, v_ref, dq_ref, dk_ref, dv_ref):
        @pl.when(pl.program_id(1) == 0)
        def _():
            dk_ref[...] = jnp.zeros_like(dk_ref)
            dv_ref[...] = jnp.zeros_like(dv_ref)

        def stack(ref):
            return jnp.concatenate([ref[:, 128 * h:128 * h + 128] for h in range(Q_PER_KV)], axis=0)

        q4, do4, o4 = stack(q_ref), stack(do_ref), stack(o_ref)
        delta = jnp.sum(do4.astype(F32) * o4.astype(F32), axis=-1, keepdims=True)
        lse4 = jnp.concatenate([lse_ref[0, :, h:h + 1] for h in range(Q_PER_KV)], axis=0)

        def step(j, dq):
            sl = pl.ds(pl.multiple_of(j * tk, tk), tk)
            kj, vj = k_ref[sl, :], v_ref[sl, :]
            p = jnp.exp(_dot_nt(q4, kj) - lse4)
            ds = (p * (_dot_nt(do4, vj) - delta)).astype(BF16)
            dk_ref[sl, :] += _dot_tn(ds, q4)
            dv_ref[sl, :] += _dot_tn(p.astype(BF16), do4)
            return dq + _dot_nn(ds, kj)

        dq = lax.fori_loop(0, n_kv, step, jnp.zeros((Q_PER_KV * tq, HEAD_DIM_A), F32))
        for h in range(Q_PER_KV):
            dq_ref[:, 128 * h:128 * h + 128] = dq[h * tq:(h + 1) * tq]

    qspec = pl.BlockSpec((tq, gw), lambda g, i: (i, g))
    kspec = pl.BlockSpec((s, HEAD_DIM_A), lambda g, i: (0, g))
    return pl.pallas_call(
        body, name="attn_a_bwd", grid=(N_KV_HEADS_A, s // tq),
        in_specs=[qspec, qspec, qspec, pl.BlockSpec((1, tq, Q_PER_KV), lambda g, i: (g, i, 0)), kspec, kspec],
        out_specs=[qspec, kspec, kspec],
        out_shape=[jax.ShapeDtypeStruct((s, QA_W), F32), jax.ShapeDtypeStruct((s, KA_W), F32),
                   jax.ShapeDtypeStruct((s, KA_W), F32)],
        compiler_params=_cparams(("arbitrary", "arbitrary")))(qrot, doa, oa, lse, krot, va)


def _band_specs(s, cb):
    per = cb // BAND
    last = s // BAND - 1
    cur = pl.BlockSpec((1, cb, GB_W), lambda g, i: (g, i, 0))
    prev = pl.BlockSpec((1, BAND, GB_W), lambda g, i: (g, jnp.maximum(i * per - 1, 0), 0))
    nxt = pl.BlockSpec((1, BAND, GB_W), lambda g, i: (g, jnp.minimum(i * per + per, last), 0))
    return cur, prev, nxt


def _blocks_per_segment(s):
    g = pl.program_id(0)
    return jnp.where(g == 0, s // (BAND * DILATIONS[0]),
                     jnp.where(g == 1, s // (BAND * DILATIONS[1]), s // (BAND * DILATIONS[2])))


def _window(prev_ref, cur_ref, next_ref):
    return jnp.concatenate([prev_ref[0], cur_ref[0], next_ref[0]], axis=0)


def _neighbour_valid(b, nbseg):
    bm = lax.rem(b, nbseg)
    return bm != 0, bm != nbseg - 1


def _mask_qk(prev_ok, next_ok):
    qq = lax.broadcasted_iota(jnp.int32, (BAND, 3 * BAND), 0)
    kk = lax.broadcasted_iota(jnp.int32, (BAND, 3 * BAND), 1)
    off = kk - BAND - qq
    return (jnp.abs(off) <= BAND) & ((kk >= BAND) | prev_ok) & ((kk < 2 * BAND) | next_ok)


def _mask_kq(prev_ok, next_ok):
    qq = lax.broadcasted_iota(jnp.int32, (3 * BAND, BAND), 0)
    kk = lax.broadcasted_iota(jnp.int32, (3 * BAND, BAND), 1)
    off = kk + BAND - qq
    return (jnp.abs(off) <= BAND) & ((qq >= BAND) | prev_ok) & ((qq < 2 * BAND) | next_ok)


def _head_lane_masks():
    lane = lax.broadcasted_iota(jnp.int32, (1, LANES), 1)
    return [lane < HEAD_DIM_B, lane >= HEAD_DIM_B]


def _band_fwd(qb, kb, vb, bias, cb):
    s = qb.shape[1]
    per = cb // BAND

    def body(q_ref, kp_ref, kc_ref, kn_ref, vp_ref, vc_ref, vn_ref, bias_ref, o_ref, lse_ref):
        nbseg = _blocks_per_segment(s)
        kw, vw = _window(kp_ref, kc_ref, kn_ref), _window(vp_ref, vc_ref, vn_ref)
        hm = _head_lane_masks()
        for jj in range(per):
            prev_ok, next_ok = _neighbour_valid(pl.program_id(1) * per + jj, nbseg)
            mask = _mask_qk(prev_ok, next_ok)
            r0 = BAND * jj
            for hp in range(2):
                ls = slice(LANES * hp, LANES * hp + LANES)
                qh = q_ref[0, r0:r0 + BAND, ls]
                k3, v3 = kw[r0:r0 + 3 * BAND, ls], vw[r0:r0 + 3 * BAND, ls]
                o_half = jnp.zeros((BAND, LANES), F32)
                lse_half = jnp.zeros((BAND, LANES), F32)
                for hh in range(2):
                    sc = _dot_nt(jnp.where(hm[hh], qh, jnp.zeros_like(qh)), k3) + bias_ref[0, 2 * hp + hh]
                    sc = jnp.where(mask, sc, NEG_INF)
                    m = jnp.max(sc, axis=-1, keepdims=True)
                    lse = m + jnp.log(jnp.sum(jnp.exp(sc - m), axis=-1, keepdims=True))
                    p = jnp.exp(sc - lse).astype(BF16)
                    o_half = o_half + _dot_nn(p, jnp.where(hm[hh], v3, jnp.zeros_like(v3)))
                    lse_half = jnp.where(hm[hh], lse, lse_half)
                o_ref[0, r0:r0 + BAND, ls] = o_half
                lse_ref[0, r0:r0 + BAND, ls] = lse_half

    cur, prev, nxt = _band_specs(s, cb)
    bias_spec = pl.BlockSpec((1, N_HEADS_PER_DIL, BAND, 3 * BAND), lambda g, i: (g, 0, 0, 0))
    return pl.pallas_call(
        body, name="band_fwd", grid=(3, s // cb),
        in_specs=[cur, prev, cur, nxt, prev, cur, nxt, bias_spec], out_specs=[cur, cur],
        out_shape=[jax.ShapeDtypeStruct(qb.shape, F32), jax.ShapeDtypeStruct(qb.shape, F32)],
        compiler_params=_cparams(("arbitrary", "arbitrary")))(qb, kb, kb, kb, vb, vb, vb, bias)


def _band_bwd_q(qb, kb, vb, dob, lse, dd, bias, cb):
    s = qb.shape[1]
    per = cb // BAND

    def body(q_ref, do_ref, lse_ref, dd_ref, kp_ref, kc_ref, kn_ref, vp_ref, vc_ref, vn_ref, bias_ref,
             dq_ref, dsum_ref):
        @pl.when(pl.program_id(1) == 0)
        def _():
            dsum_ref[...] = jnp.zeros_like(dsum_ref)

        nbseg = _blocks_per_segment(s)
        kw, vw = _window(kp_ref, kc_ref, kn_ref), _window(vp_ref, vc_ref, vn_ref)
        hm = _head_lane_masks()
        for jj in range(per):
            prev_ok, next_ok = _neighbour_valid(pl.program_id(1) * per + jj, nbseg)
            mask = _mask_qk(prev_ok, next_ok)
            r0 = BAND * jj
            for hp in range(2):
                ls = slice(LANES * hp, LANES * hp + LANES)
                qh, doh = q_ref[0, r0:r0 + BAND, ls], do_ref[0, r0:r0 + BAND, ls]
                k3, v3 = kw[r0:r0 + 3 * BAND, ls], vw[r0:r0 + 3 * BAND, ls]
                dq_half = jnp.zeros((BAND, LANES), F32)
                for hh in range(2):
                    h = 2 * hp + hh
                    col = LANES * hp + HEAD_DIM_B * hh
                    sc = _dot_nt(jnp.where(hm[hh], qh, jnp.zeros_like(qh)), k3) + bias_ref[0, h]
                    sc = jnp.where(mask, sc, NEG_INF)
                    p = jnp.exp(sc - lse_ref[0, r0:r0 + BAND, col:col + 1])
                    dp = _dot_nt(jnp.where(hm[hh], doh, jnp.zeros_like(doh)), v3)
                    ds = p * (dp - dd_ref[0, r0:r0 + BAND, col:col + 1])
                    dsum_ref[0, h] += ds
                    dq_half = dq_half + _dot_nn(ds.astype(BF16), jnp.where(hm[hh], k3, jnp.zeros_like(k3)))
                dq_ref[0, r0:r0 + BAND, ls] = dq_half

    cur, prev, nxt = _band_specs(s, cb)
    bias_spec = pl.BlockSpec((1, N_HEADS_PER_DIL, BAND, 3 * BAND), lambda g, i: (g, 0, 0, 0))
    return pl.pallas_call(
        body, name="band_bwd_q", grid=(3, s // cb),
        in_specs=[cur, cur, cur, cur, prev, cur, nxt, prev, cur, nxt, bias_spec], out_specs=[cur, bias_spec],
        out_shape=[jax.ShapeDtypeStruct(qb.shape, F32),
                   jax.ShapeDtypeStruct((3, N_HEADS_PER_DIL, BAND, 3 * BAND), F32)],
        compiler_params=_cparams(("arbitrary", "arbitrary")))(qb, dob, lse, dd, kb, kb, kb, vb, vb, vb, bias)


def _band_bwd_kv(qb, kb, vb, dob, lse, dd, bias_t, cb):
    s = qb.shape[1]
    per = cb // BAND

    def body(k_ref, v_ref, qp_ref, qc_ref, qn_ref, dp_ref, dc_ref, dn_ref, lp_ref, lc_ref, ln_ref,
             ep_ref, ec_ref, en_ref, bias_ref, dk_ref, dv_ref):
        nbseg = _blocks_per_segment(s)
        qw, dow = _window(qp_ref, qc_ref, qn_ref), _window(dp_ref, dc_ref, dn_ref)
        lw, ew = _window(lp_ref, lc_ref, ln_ref), _window(ep_ref, ec_ref, en_ref)
        hm = _head_lane_masks()
        for jj in range(per):
            prev_ok, next_ok = _neighbour_valid(pl.program_id(1) * per + jj, nbseg)
            mask = _mask_kq(prev_ok, next_ok)
            r0 = BAND * jj
            for hp in range(2):
                ls = slice(LANES * hp, LANES * hp + LANES)
                kh, vh = k_ref[0, r0:r0 + BAND, ls], v_ref[0, r0:r0 + BAND, ls]
                q3, do3 = qw[r0:r0 + 3 * BAND, ls], dow[r0:r0 + 3 * BAND, ls]
                dk_half = jnp.zeros((BAND, LANES), F32)
                dv_half = jnp.zeros((BAND, LANES), F32)
                for hh in range(2):
                    col = LANES * hp + HEAD_DIM_B * hh
                    q3m = jnp.where(hm[hh], q3, jnp.zeros_like(q3))
                    do3m = jnp.where(hm[hh], do3, jnp.zeros_like(do3))
                    sc = _dot_nt(q3m, kh) + bias_ref[0, 2 * hp + hh]
                    sc = jnp.where(mask, sc, NEG_INF)
                    p = jnp.exp(sc - lw[r0:r0 + 3 * BAND, col:col + 1])
                    ds = p * (_dot_nt(do3m, vh) - ew[r0:r0 + 3 * BAND, col:col + 1])
                    dk_half = dk_half + _dot_tn(ds.astype(BF16), q3m)
                    dv_half = dv_half + _dot_tn(p.astype(BF16), do3m)
                dk_ref[0, r0:r0 + BAND, ls] = dk_half
                dv_ref[0, r0:r0 + BAND, ls] = dv_half

    cur, prev, nxt = _band_specs(s, cb)
    win = [prev, cur, nxt]
    bias_spec = pl.BlockSpec((1, N_HEADS_PER_DIL, 3 * BAND, BAND), lambda g, i: (g, 0, 0, 0))
    return pl.pallas_call(
        body, name="band_bwd_kv", grid=(3, s // cb),
        in_specs=[cur, cur] + win * 4 + [bias_spec], out_specs=[cur, cur],
        out_shape=[jax.ShapeDtypeStruct(qb.shape, F32), jax.ShapeDtypeStruct(qb.shape, F32)],
        compiler_params=_cparams(("arbitrary", "arbitrary")))(
        kb, vb, qb, qb, qb, dob, dob, dob, lse, lse, lse, dd, dd, dd, bias_t)


def _t5_bucket(rel):
    nb = N_REL_BUCKETS // 2
    ret = (rel > 0).astype(np.int32) * nb
    n = np.abs(rel)
    max_exact = nb // 2
    large = max_exact + (np.log(np.maximum(n, 1) / max_exact) / math.log(REL_MAX_DIST / max_exact)
                         * (nb - max_exact)).astype(np.int32)
    large = np.minimum(large, nb - 1)
    return ret + np.where(n < max_exact, n, large).astype(np.int32)


def _bucket_onehots():
    qq, kk = np.arange(BAND)[:, None], np.arange(3 * BAND)[None, :]
    off_qk = kk - BAND - qq
    off_kq = (np.arange(BAND)[None, :] + BAND - np.arange(3 * BAND)[:, None])
    out = []
    for off in (off_qk, off_kq):
        per_group = []
        for d in DILATIONS:
            bucket = np.where(np.abs(off) <= BAND, _t5_bucket(off * d), -1).reshape(-1)
            per_group.append(bucket[None, :] == np.arange(N_REL_BUCKETS)[:, None])
        out.append(np.stack(per_group))
    return out


def _band_bias(rel_bias, onehot, shape):
    tab = rel_bias.reshape(N_REL_BUCKETS, 3, N_HEADS_PER_DIL)
    b = jnp.einsum("gbn,bgh->ghn", onehot.astype(F32), tab, precision=lax.Precision.HIGHEST)
    return b.reshape(3, N_HEADS_PER_DIL, *shape)


def _rel_bias_grad(dsum, onehot):
    def body(d_ref, oh_ref, out_ref):
        for g in range(3):
            v = d_ref[g]
            oh = oh_ref[g]
            acc = jnp.zeros((N_HEADS_PER_DIL, N_REL_BUCKETS), F32)
            for _ in range(3):
                part = v.astype(BF16)
                acc = acc + _dot_nt(part, oh)
                v = v - part.astype(F32)
            out_ref[g] = acc

    return pl.pallas_call(body, name="rel_bias_grad",
                          out_shape=jax.ShapeDtypeStruct((3, N_HEADS_PER_DIL, N_REL_BUCKETS), F32))(dsum, onehot)


def _seg_sum(v):
    lane = lax.broadcasted_iota(jnp.int32, (1, v.shape[1]), 1)
    out = jnp.zeros_like(v)
    for h in range(v.shape[1] // HEAD_DIM_B):
        m = (lane >= HEAD_DIM_B * h) & (lane < HEAD_DIM_B * (h + 1))
        out = jnp.where(m, jnp.sum(jnp.where(m, v, 0.0), axis=-1, keepdims=True), out)
    return out


def _mix_out(x, oa, og, lg, ga, gb, w_oa, w_ob_t, w_o, tb):
    s, d = x.shape

    def body(x_ref, oa_ref, og_ref, lg_ref, ga_ref, gb_ref, woa_ref, wob_ref, wo_ref,
             x2_ref, ob_ref, lse_ref, ya_ref, yb_ref, u_ref):
        l0, l1, l2 = lg_ref[0], lg_ref[1], lg_ref[2]
        lmax = jnp.maximum(jnp.maximum(l0, l1), l2)
        w0, w1, w2 = jnp.exp(l0 - lmax), jnp.exp(l1 - lmax), jnp.exp(l2 - lmax)
        den = w0 + w1 + w2
        ob = ((w0 * og_ref[0] + w1 * og_ref[1] + w2 * og_ref[2]) / den).astype(BF16)
        ob_ref[...] = ob
        lse_ref[...] = lmax + jnp.log(den)
        ya = _dot_nn(oa_ref[...], woa_ref[...])
        yb = _dot_nt(ob, wob_ref[...])
        ya_ref[...] = ya.astype(BF16)
        yb_ref[...] = yb.astype(BF16)
        u = (ga_ref[...] * ya + gb_ref[...] * yb).astype(BF16)
        u_ref[...] = u
        x2_ref[...] = x_ref[...] + _dot_nn(u, wo_ref[...])

    sd = jax.ShapeDtypeStruct
    return pl.pallas_call(
        body, name="mix_out", grid=(s // tb,),
        in_specs=[_rows(tb, d), _rows(tb, QA_W), _rows3(tb, GB_W), _rows3(tb, GB_W), _rows(tb, d), _rows(tb, d),
                  _resident(w_oa.shape), _resident(w_ob_t.shape), _resident(w_o.shape)],
        out_specs=[_rows(tb, d), _rows(tb, GB_W), _rows(tb, GB_W), _rows(tb, d), _rows(tb, d), _rows(tb, d)],
        out_shape=[sd((s, d), F32), sd((s, GB_W), BF16), sd((s, GB_W), F32), sd((s, d), BF16), sd((s, d), BF16),
                   sd((s, d), BF16)],
        compiler_params=_cparams(("arbitrary",)))(x, oa, og, lg, ga, gb, w_oa, w_ob_t, w_o)


def _mlp_fwd(x2, w1_t, w2, g_mlp, tb, tc):
    s, d = x2.shape
    dff = w1_t.shape[0]

    def body(x_ref, w1_ref, w2_ref, g_ref, x3_ref, r_ref, h_ref):
        xv = x_ref[...]
        hb = (xv * _rstd(xv) * g_ref[...]).astype(BF16)
        h_ref[...] = hb
        x3_ref[...] = xv
        for c in range(dff // tc):
            sl = slice(tc * c, tc * c + tc)
            r = jnp.maximum(_dot_nt(hb, w1_ref[sl, :]), 0.0)
            r_ref[:, sl] = r.astype(BF16)
            x3_ref[...] += _dot_nn((r * r).astype(BF16), w2_ref[sl, :])

    sd = jax.ShapeDtypeStruct
    return pl.pallas_call(
        body, name="mlp_fwd", grid=(s // tb,),
        in_specs=[_rows(tb, d), _resident(w1_t.shape), _resident(w2.shape), _resident(g_mlp.shape)],
        out_specs=[_rows(tb, d), _rows(tb, dff), _rows(tb, d)],
        out_shape=[sd((s, d), F32), sd((s, dff), BF16), sd((s, d), BF16)],
        compiler_params=_cparams(("arbitrary",)))(x2, w1_t, w2, g_mlp)


def _ple_loss(x3, p, target, w_pg, w_p_t, g_ple, g_fin, tb):
    s, d = x3.shape
    dp = p.shape[1]

    def body(x_ref, p_ref, t_ref, wpg_ref, wp_ref, gple_ref, gfin_ref,
             dx3_ref, h3_ref, dpre_ref, dpe_ref, pb_ref, loss_ref, dgfin_ref, dgple_ref):
        @pl.when(pl.program_id(0) == 0)
        def _():
            loss_ref[...] = jnp.zeros_like(loss_ref)
            dgfin_ref[...] = jnp.zeros_like(dgfin_ref)
            dgple_ref[...] = jnp.zeros_like(dgple_ref)

        x3v = x_ref[...]
        r3 = _rstd(x3v)
        n3 = x3v * r3
        h3 = (n3 * gple_ref[...]).astype(BF16)
        h3_ref[...] = h3
        gp = _sigmoid(_dot_nn(h3, wpg_ref[...]))
        pb = p_ref[...].astype(BF16)
        pb_ref[...] = pb
        pe = _dot_nt(pb, wp_ref[...])
        x4 = x3v + gp * pe
        r4 = _rstd(x4)
        n4 = x4 * r4
        err = n4 * gfin_ref[...] - t_ref[...]
        loss_ref[...] += jnp.sum(0.5 * jnp.mean(err * err, axis=-1, keepdims=True), axis=0, keepdims=True)
        dy = err / d
        dgfin_ref[...] += _colsum(dy * n4)
        dx4 = _rms_bwd(dy, n4, r4, gfin_ref[...])
        dpe_ref[...] = (dx4 * gp).astype(BF16)
        dpre = (dx4 * pe * gp * (1.0 - gp)).astype(BF16)
        dpre_ref[...] = dpre
        dh3 = _dot_nt(dpre, wpg_ref[...])
        dgple_ref[...] += _colsum(dh3 * n3)
        dx3_ref[...] = dx4 + _rms_bwd(dh3, n3, r3, gple_ref[...])

    sd = jax.ShapeDtypeStruct
    return pl.pallas_call(
        body, name="ple_loss", grid=(s // tb,),
        in_specs=[_rows(tb, d), _rows(tb, dp), _rows(tb, d), _resident(w_pg.shape), _resident(w_p_t.shape),
                  _resident(g_ple.shape), _resident(g_fin.shape)],
        out_specs=[_rows(tb, d), _rows(tb, d), _rows(tb, d), _rows(tb, d), _rows(tb, dp),
                   _acc_spec((1, LANES)), _acc_spec((1, d)), _acc_spec((1, d))],
        out_shape=[sd((s, d), F32), sd((s, d), BF16), sd((s, d), BF16), sd((s, d), BF16), sd((s, dp), BF16),
                   sd((1, LANES), F32), sd((1, d), F32), sd((1, d), F32)],
        compiler_params=_cparams(("arbitrary",)))(x3, p, target, w_pg, w_p_t, g_ple, g_fin)


def _mlp_bwd(dx3, x2, r, w1_t, w2, g_mlp, tb, tc):
    s, d = x2.shape
    dff = w1_t.shape[0]

    def body(dx3_ref, x_ref, r_ref, w1_ref, w2_ref, g_ref, dx2_ref, df_ref, dg_ref, dh_ref):
        @pl.when(pl.program_id(0) == 0)
        def _():
            dg_ref[...] = jnp.zeros_like(dg_ref)

        dx3v = dx3_ref[...]
        dx3b = dx3v.astype(BF16)
        dh_ref[...] = jnp.zeros_like(dh_ref)
        for c in range(dff // tc):
            sl = slice(tc * c, tc * c + tc)
            df = (_dot_nt(dx3b, w2_ref[sl, :]) * (2.0 * r_ref[:, sl].astype(F32))).astype(BF16)
            df_ref[:, sl] = df
            dh_ref[...] += _dot_nn(df, w1_ref[sl, :])
        xv = x_ref[...]
        r2 = _rstd(xv)
        n2 = xv * r2
        dh = dh_ref[...]
        dg_ref[...] += _colsum(dh * n2)
        dx2_ref[...] = dx3v + _rms_bwd(dh, n2, r2, g_ref[...])

    sd = jax.ShapeDtypeStruct
    return pl.pallas_call(
        body, name="mlp_bwd", grid=(s // tb,),
        in_specs=[_rows(tb, d), _rows(tb, d), _rows(tb, dff), _resident(w1_t.shape), _resident(w2.shape),
                  _resident(g_mlp.shape)],
        out_specs=[_rows(tb, d), _rows(tb, dff), _acc_spec((1, d))],
        out_shape=[sd((s, d), F32), sd((s, dff), BF16), sd((1, d), F32)],
        scratch_shapes=[pltpu.VMEM((tb, d), F32)],
        compiler_params=_cparams(("arbitrary",)))(dx3, x2, r, w1_t, w2, g_mlp)


def _mix_out_bwd(dx2, ya, yb, ga, gb, ob, w_oa, w_ob_t, w_o, tb):
    s, d = dx2.shape

    def body(dx_ref, ya_ref, yb_ref, ga_ref, gb_ref, ob_ref, woa_ref, wob_ref, wo_ref,
             doa_ref, dob_ref, dd_ref, dga_ref, dgb_ref, dya_ref, dyb_ref, dbg_ref):
        @pl.when(pl.program_id(0) == 0)
        def _():
            dbg_ref[...] = jnp.zeros_like(dbg_ref)

        du = _dot_nt(dx_ref[...].astype(BF16), wo_ref[...])
        gav, gbv = ga_ref[...], gb_ref[...]
        dya = (du * gav).astype(BF16)
        dyb = (du * gbv).astype(BF16)
        dya_ref[...] = dya
        dyb_ref[...] = dyb
        dga = du * ya_ref[...].astype(F32) * gav * (1.0 - gav)
        dgb = du * yb_ref[...].astype(F32) * gbv * (1.0 - gbv)
        dga_ref[...] = dga.astype(BF16)
        dgb_ref[...] = dgb.astype(BF16)
        dbg_ref[:, 0:d] += _colsum(dga)
        dbg_ref[:, d:2 * d] += _colsum(dgb)
        doa_ref[...] = _dot_nt(dya, woa_ref[...]).astype(BF16)
        dob = _dot_nn(dyb, wob_ref[...])
        dob_ref[...] = dob.astype(BF16)
        dd_ref[...] = _seg_sum(dob * ob_ref[...].astype(F32))

    sd = jax.ShapeDtypeStruct
    return pl.pallas_call(
        body, name="mix_out_bwd", grid=(s // tb,),
        in_specs=[_rows(tb, d)] * 5 + [_rows(tb, GB_W), _resident(w_oa.shape), _resident(w_ob_t.shape),
                                       _resident(w_o.shape)],
        out_specs=[_rows(tb, QA_W), _rows(tb, GB_W), _rows(tb, GB_W), _rows(tb, d), _rows(tb, d), _rows(tb, d),
                   _rows(tb, d), _acc_spec((1, 2 * d))],
        out_shape=[sd((s, QA_W), BF16), sd((s, GB_W), BF16), sd((s, GB_W), F32), sd((s, d), BF16), sd((s, d), BF16),
                   sd((s, d), BF16), sd((s, d), BF16), sd((1, 2 * d), F32)],
        compiler_params=_cparams(("arbitrary",)))(dx2, ya, yb, ga, gb, ob, w_oa, w_ob_t, w_o)


def _in_proj_bwd(dx2, x, dqrot, dkrot, dva, qraw, kraw, tabs, dqb, dkb, dvb, dga, dgb, w_in_t, g_mix, q_g, k_g, tb):
    s, d = x.shape
    din = w_in_t.shape[0]
    q_scale = HEAD_DIM_A ** -0.5
    b_scale = HEAD_DIM_B ** -0.5
    tc = 256

    def body(dx2_ref, x_ref, dq_ref, dk_ref, dv_ref, qraw_ref, kraw_ref, c_ref, s1_ref, s2_ref,
             dqb_ref, dkb_ref, dvb_ref, dga_ref, dgb_ref, w_ref, gmix_ref, qg_ref, kg_ref,
             dx_ref, dz_ref, dgmix_ref, dqg_ref, dkg_ref, dh_ref):
        @pl.when(pl.program_id(0) == 0)
        def _():
            dgmix_ref[...] = jnp.zeros_like(dgmix_ref)
            dqg_ref[...] = jnp.zeros_like(dqg_ref)
            dkg_ref[...] = jnp.zeros_like(dkg_ref)

        cos, s1, s2 = c_ref[...], s1_ref[...], s2_ref[...]

        def head_bwd(drot, z, g_ref, acc_ref):
            dn = _rope_bwd(drot, cos, s1, s2)
            rr = _rstd(z)
            nn = z * rr
            acc_ref[...] += _colsum(dn * nn)
            return _rms_bwd(dn, nn, rr, g_ref[...])

        for h in range(N_Q_HEADS_A):
            sl = slice(128 * h, 128 * h + 128)
            dz_ref[:, OFF_QA + 128 * h:OFF_QA + 128 * h + 128] = head_bwd(
                dq_ref[:, sl] * q_scale, qraw_ref[:, sl], qg_ref, dqg_ref).astype(BF16)
        for h in range(N_KV_HEADS_A):
            sl = slice(128 * h, 128 * h + 128)
            dz_ref[:, OFF_KA + 128 * h:OFF_KA + 128 * h + 128] = head_bwd(
                dk_ref[:, sl], kraw_ref[:, sl], kg_ref, dkg_ref).astype(BF16)
        dz_ref[:, OFF_VA:OFF_VA + KA_W] = dv_ref[...].astype(BF16)
        for g in range(3):
            dz_ref[:, OFF_QB + GB_W * g:OFF_QB + GB_W * (g + 1)] = (dqb_ref[g] * b_scale).astype(BF16)
            dz_ref[:, OFF_KB + GB_W * g:OFF_KB + GB_W * (g + 1)] = dkb_ref[g].astype(BF16)
            dz_ref[:, OFF_VB + GB_W * g:OFF_VB + GB_W * (g + 1)] = dvb_ref[g].astype(BF16)
        dz_ref[:, OFF_GA:OFF_GA + d] = dga_ref[...]
        dz_ref[:, OFF_GA + d:OFF_GA + 2 * d] = dgb_ref[...]
        dh_ref[...] = jnp.zeros_like(dh_ref)
        for c in range(din // tc):
            sl = slice(tc * c, tc * c + tc)
            dh_ref[...] += _dot_nn(dz_ref[:, sl], w_ref[sl, :])
        xv = x_ref[...]
        r1 = _rstd(xv)
        n1 = xv * r1
        dh = dh_ref[...]
        dgmix_ref[...] += _colsum(dh * n1)
        dx_ref[...] = dx2_ref[...] + _rms_bwd(dh, n1, r1, gmix_ref[...])

    sd = jax.ShapeDtypeStruct
    return pl.pallas_call(
        body, name="in_proj_bwd", grid=(s // tb,),
        in_specs=[_rows(tb, d), _rows(tb, d), _rows(tb, QA_W), _rows(tb, KA_W), _rows(tb, KA_W), _rows(tb, QA_W),
                  _rows(tb, KA_W), _rows(tb, LANES), _rows(tb, LANES), _rows(tb, LANES),
                  _rows3(tb, GB_W), _rows3(tb, GB_W), _rows3(tb, GB_W), _rows(tb, d), _rows(tb, d),
                  _resident(w_in_t.shape), _resident(g_mix.shape), _resident(q_g.shape), _resident(k_g.shape)],
        out_specs=[_rows(tb, d), _rows(tb, din), _acc_spec((1, d)), _acc_spec((1, HEAD_DIM_A)),
                   _acc_spec((1, HEAD_DIM_A))],
        out_shape=[sd((s, d), F32), sd((s, din), BF16), sd((1, d), F32), sd((1, HEAD_DIM_A), F32),
                   sd((1, HEAD_DIM_A), F32)],
        scratch_shapes=[pltpu.VMEM((tb, d), F32)],
        compiler_params=_cparams(("arbitrary",)))(
        dx2, x, dqrot, dkrot, dva, qraw, kraw, *tabs, dqb, dkb, dvb, dga, dgb, w_in_t, g_mix, q_g, k_g)


def _identity(v):
    return v


def _to_bf16(v):
    return v.astype(BF16)


def _square_bf16(v):
    vf = v.astype(F32)
    return (vf * vf).astype(BF16)


def _weight_grad(name, a, b, ti, tj, tk, a_fn=_identity, b_fn=_identity):
    t, m = a.shape
    n = b.shape[1]
    n_k = t // tk

    def body(a_ref, b_ref, o_ref, acc_ref):
        k = pl.program_id(2)

        @pl.when(k == 0)
        def _():
            acc_ref[...] = jnp.zeros_like(acc_ref)

        acc_ref[...] += _dot_tn(a_fn(a_ref[...]), b_fn(b_ref[...]))

        @pl.when(k == n_k - 1)
        def _():
            o_ref[...] = acc_ref[...].astype(BF16)

    return pl.pallas_call(
        body, name=name, grid=(m // ti, n // tj, n_k),
        in_specs=[pl.BlockSpec((tk, ti), lambda i, j, k: (k, i)), pl.BlockSpec((tk, tj), lambda i, j, k: (k, j))],
        out_specs=pl.BlockSpec((ti, tj), lambda i, j, k: (i, j)),
        out_shape=jax.ShapeDtypeStruct((m, n), BF16),
        scratch_shapes=[pltpu.VMEM((ti, tj), F32)],
        compiler_params=_cparams(("arbitrary", "arbitrary", "arbitrary")))(a, b)


def _sum_slots(name, recv):
    _, n, k = recv.shape
    tc = min(k, 256)

    def body(r_ref, o_ref):
        acc = r_ref[0].astype(F32)
        for i in range(1, N_DEV):
            acc = acc + r_ref[i].astype(F32)
        o_ref[...] = acc

    return pl.pallas_call(
        body, name=name, grid=(k // tc,),
        in_specs=[pl.BlockSpec((N_DEV, n, tc), lambda j: (0, 0, j))],
        out_specs=pl.BlockSpec((n, tc), lambda j: (0, j)),
        out_shape=jax.ShapeDtypeStruct((n, k), F32),
        compiler_params=_cparams(("arbitrary",)))(recv)


def _adamw_math(w, g, m, v):
    m = ADAM_B1 * m + (1.0 - ADAM_B1) * g
    v = ADAM_B2 * v + (1.0 - ADAM_B2) * (g * g)
    m_hat = m / (1.0 - ADAM_B1 ** ADAM_STEP)
    v_hat = v / (1.0 - ADAM_B2 ** ADAM_STEP)
    delta = -ADAM_LR * (m_hat / (jnp.sqrt(v_hat) + ADAM_EPS) + ADAM_WD * w)
    return delta, m, v


def _adamw(name, w, g, m, v):
    r, c = w.shape
    tr = min(r, 256)

    def body(w_ref, g_ref, m_ref, v_ref, d_ref, mo_ref, vo_ref):
        d_ref[...], mo_ref[...], vo_ref[...] = _adamw_math(w_ref[...], g_ref[...], m_ref[...], v_ref[...])

    spec = pl.BlockSpec((tr, c), lambda i: (i, 0))
    return pl.pallas_call(
        body, name=name, grid=(r // tr,), in_specs=[spec] * 4, out_specs=[spec] * 3,
        out_shape=[jax.ShapeDtypeStruct((r, c), F32)] * 3,
        compiler_params=_cparams(("arbitrary",)))(w, g, m, v)


def _small_update(parts, w, m, v):
    def body(p_ref, w_ref, m_ref, v_ref, g_ref, d_ref, mo_ref, vo_ref):
        g = p_ref[0]
        for i in range(1, N_DEV):
            g = g + p_ref[i]
        g_ref[...] = g
        d_ref[...], mo_ref[...], vo_ref[...] = _adamw_math(w_ref[...], g, m_ref[...], v_ref[...])

    return pl.pallas_call(body, name="small_update", out_shape=[jax.ShapeDtypeStruct(w.shape, F32)] * 4)(
        parts, w, m, v)


def _to_residue_major(a):
    _, s, c = a.shape
    out = [a[0]]
    for g in (1, 2):
        dil = DILATIONS[g]
        out.append(a[g].reshape(s // dil, dil, c).transpose(1, 0, 2).reshape(s, c))
    return jnp.stack(out)


def _from_residue_major(a):
    _, s, c = a.shape
    out = [a[0]]
    for g in (1, 2):
        dil = DILATIONS[g]
        out.append(a[g].reshape(dil, s // dil, c).transpose(1, 0, 2).reshape(s, c))
    return jnp.stack(out)


def _pack_rows(vectors, n_rows):
    flat = jnp.concatenate([v.reshape(-1).astype(F32) for v in vectors])
    flat = jnp.pad(flat, (0, n_rows * LANES - flat.shape[0]))
    return flat.reshape(n_rows, LANES)


def _pick_tile(n, prefs):
    for t in prefs:
        if n % t == 0:
            return t
    return n


def kernel(x, p, norm_mix_g, w_in, b_gate, q_norm_g, k_norm_g, rel_bias, w_out_a, w_out_b, w_out, norm_mlp_g, w_ff1, w_ff2, norm_ple_g, w_ple_gate, w_ple, final_norm_g, loss_target, m_norm_mix_g, m_w_in, m_b_gate, m_q_norm_g, m_k_norm_g, m_rel_bias, m_w_out_a, m_w_out_b, m_w_out, m_norm_mlp_g, m_w_ff1, m_w_ff2, m_norm_ple_g, m_w_ple_gate, m_w_ple, m_final_norm_g, v_norm_mix_g, v_w_in, v_b_gate, v_q_norm_g, v_k_norm_g, v_rel_bias, v_w_out_a, v_w_out_b, v_w_out, v_norm_mlp_g, v_w_ff1, v_w_ff2, v_norm_ple_g, v_w_ple_gate, v_w_ple, v_final_norm_g):
    s, d = x.shape[1], x.shape[2]
    xs, ps, ts = x[0], p[0, 0], loss_target[0]
    tb = _pick_tile(s, (512, 256))
    tq = _pick_tile(s, (256,))
    tk = _pick_tile(s, (512,))
    cb = _pick_tile(s, (512,))
    fin_g = final_norm_g.reshape(1, d)

    col_sharded = {"w_in": w_in[0], "w_out_b": w_out_b[0], "w_ff1": w_ff1[0], "w_ple": w_ple[0]}
    row_sharded = {"w_out_a": w_out_a[0], "w_out": w_out[0], "w_ff2": w_ff2[0], "w_ple_gate": w_ple_gate[0]}
    order = ["w_in", "w_out_a", "w_out_b", "w_out", "w_ff1", "w_ff2", "w_ple_gate", "w_ple"]
    shards = [(col_sharded[n].T if n in col_sharded else row_sharded[n]).astype(BF16) for n in order]
    w_in_t, w_oa, w_ob_t, w_o, w_ff1_t, w_ff2_f, w_pg, w_p_t = _all_gather(shards)

    tabs = _rope_tables(s)
    (h1, qraw, kraw, qrot, krot, va, qb, kb, vb, ga, gb) = _in_proj(
        xs, tabs, w_in_t, norm_mix_g, b_gate, q_norm_g, k_norm_g, tb)
    oa, lse_a = _attn_a_fwd(qrot, krot, va, tq, tk)
    oh_qk, oh_kq = _bucket_onehots()
    bias_qk = _band_bias(rel_bias, jnp.asarray(oh_qk), (BAND, 3 * BAND))
    bias_kq = _band_bias(rel_bias, jnp.asarray(oh_kq), (3 * BAND, BAND))
    qb_r, kb_r, vb_r = _to_residue_major(qb), _to_residue_major(kb), _to_residue_major(vb)
    og_r, lg_r = _band_fwd(qb_r, kb_r, vb_r, bias_qk, cb)
    og, lg = _from_residue_major(og_r), _from_residue_major(lg_r)
    x2, ob, lse_b, ya, yb, u = _mix_out(xs, oa, og, lg, ga, gb, w_oa, w_ob_t, w_o, tb)
    tc = _pick_tile(w_ff1_t.shape[0], (512,))
    x3, r_act, h2 = _mlp_fwd(x2, w_ff1_t, w_ff2_f, norm_mlp_g, tb, tc)

    dx3, h3, dpre, dpe, pb, loss_part, dg_fin, dg_ple = _ple_loss(
        x3, ps, ts, w_pg, w_p_t, norm_ple_g, fin_g, tb)
    dx2, df, dg_mlp = _mlp_bwd(dx3, x2, r_act, w_ff1_t, w_ff2_f, norm_mlp_g, tb, tc)
    doa, dob, dd, dga, dgb, dya, dyb, dbg = _mix_out_bwd(dx2, ya, yb, ga, gb, ob, w_oa, w_ob_t, w_o, tb)
    dqrot, dkrot, dva = _attn_a_bwd(qrot, krot, va, oa, doa, lse_a, tq, tk)
    dob_r = _to_residue_major(jnp.stack([dob] * 3))
    lse_r = _to_residue_major(jnp.stack([lse_b] * 3))
    dd_r = _to_residue_major(jnp.stack([dd] * 3))
    dqb_r, dsum = _band_bwd_q(qb_r, kb_r, vb_r, dob_r, lse_r, dd_r, bias_qk, cb)
    dkb_r, dvb_r = _band_bwd_kv(qb_r, kb_r, vb_r, dob_r, lse_r, dd_r, bias_kq, cb)
    dqb, dkb, dvb = _from_residue_major(dqb_r), _from_residue_major(dkb_r), _from_residue_major(dvb_r)
    grad_x, dz, dg_mix, dg_q, dg_k = _in_proj_bwd(
        dx2, xs, dqrot, dkrot, dva, qraw, kraw, tabs, dqb, dkb, dvb, dga, dgb, w_in_t, norm_mix_g,
        q_norm_g, k_norm_g, _pick_tile(s, (256,)))
    d_rel =_rel_bias_grad(dsum.reshape(3, N_HEADS_PER_DIL, BAND * 3 * BAND), jnp.asarray(oh_qk).astype(BF16))
    d_rel = d_rel.transpose(2, 0, 1).reshape(N_REL_BUCKETS, 3 * N_HEADS_PER_DIL)

    tkk = _pick_tile(s, (1024, 512))
    din = w_in_t.shape[0]
    ti_in = _pick_tile(din, (din // 2,)) if (din // 2) % LANES == 0 else din
    dff = w_ff1_t.shape[0]
    t1k = lambda n: _pick_tile(n, (1024, 512, 256))
    partials = [
        _weight_grad("grad_w_in", dz, h1, ti_in, t1k(d), tkk),
        _weight_grad("grad_w_out_a", oa, dya, t1k(QA_W), t1k(d), tkk),
        _weight_grad("grad_w_out_b", dyb, ob, t1k(d), GB_W, tkk),
        _weight_grad("grad_w_out", u, dx2, t1k(d), t1k(d), tkk, b_fn=_to_bf16),
        _weight_grad("grad_w_ff1", df, h2, t1k(dff), t1k(d), tkk),
        _weight_grad("grad_w_ff2", r_act, dx3, t1k(dff), t1k(d), tkk, a_fn=_square_bf16, b_fn=_to_bf16),
        _weight_grad("grad_w_ple_gate", h3, dpre, t1k(d), t1k(d), tkk),
        _weight_grad("grad_w_ple", dpe, pb, t1k(d), ps.shape[1], tkk),
    ]
    received = _scatter_blocks(partials)
    sums = [_sum_slots("sum_" + n, r) for n, r in zip(order, received)]

    given_w = dict(w_in=w_in, w_out_a=w_out_a, w_out_b=w_out_b, w_out=w_out, w_ff1=w_ff1, w_ff2=w_ff2,
                   w_ple_gate=w_ple_gate, w_ple=w_ple)
    given_m = dict(w_in=m_w_in, w_out_a=m_w_out_a, w_out_b=m_w_out_b, w_out=m_w_out, w_ff1=m_w_ff1, w_ff2=m_w_ff2,
                   w_ple_gate=m_w_ple_gate, w_ple=m_w_ple)
    given_v = dict(w_in=v_w_in, w_out_a=v_w_out_a, w_out_b=v_w_out_b, w_out=v_w_out, w_ff1=v_w_ff1, w_ff2=v_w_ff2,
                   w_ple_gate=v_w_ple_gate, w_ple=v_w_ple)
    big = {}
    for n, gsum in zip(order, sums):
        g = gsum.T if n in col_sharded else gsum
        delta, new_m, new_v = _adamw("adamw_" + n, given_w[n][0], g, given_m[n][0], given_v[n][0])
        big[n] = tuple(a[None] for a in (g, delta, new_m, new_v))

    small_names = ["norm_mix_g", "b_gate", "q_norm_g", "k_norm_g", "rel_bias", "norm_mlp_g", "norm_ple_g",
                   "final_norm_g"]
    small_w = [norm_mix_g, b_gate, q_norm_g, k_norm_g, rel_bias, norm_mlp_g, norm_ple_g, final_norm_g]
    small_m = [m_norm_mix_g, m_b_gate, m_q_norm_g, m_k_norm_g, m_rel_bias, m_norm_mlp_g, m_norm_ple_g,
               m_final_norm_g]
    small_v = [v_norm_mix_g, v_b_gate, v_q_norm_g, v_k_norm_g, v_rel_bias, v_norm_mlp_g, v_norm_ple_g,
               v_final_norm_g]
    small_g = [dg_mix, dbg, dg_q, dg_k, d_rel, dg_mlp, dg_ple, dg_fin]
    sizes = [int(np.prod(w.shape)) for w in small_w]
    n_rows = -(-(sum(-(-sz // LANES) for sz in sizes) + 1) // 8) * 8
    pad = lambda v: jnp.pad(v.reshape(-1).astype(F32), (0, -v.size % LANES))
    pack = lambda vs, last: _pack_rows([pad(v) for v in vs] + [last], n_rows)
    zero_row = jnp.zeros((LANES,), F32)
    parts = _small_all_gather(pack(small_g, loss_part.reshape(-1) * (jnp.arange(LANES) == 0)))
    g_all, d_all, m_all, v_all = _small_update(parts, pack(small_w, zero_row), pack(small_m, zero_row),
                                               pack(small_v, zero_row))
    small = {}
    row = 0
    for n, w, sz in zip(small_names, small_w, sizes):
        nr = -(-sz // LANES)
        small[n] = tuple(a[row:row + nr].reshape(-1)[:sz].reshape(w.shape) for a in (g_all, d_all, m_all, v_all))
        row += nr
    loss = g_all[row, 0]

    names = ["norm_mix_g", "w_in", "b_gate", "q_norm_g", "k_norm_g", "rel_bias", "w_out_a", "w_out_b", "w_out",
             "norm_mlp_g", "w_ff1", "w_ff2", "norm_ple_g", "w_ple_gate", "w_ple", "final_norm_g"]
    res = {n: (big[n] if n in big else small[n]) for n in names}
    return (loss, grad_x[None], *[res[n][0] for n in names], *[res[n][1] for n in names],
            *[res[n][2] for n in names], *[res[n][3] for n in names])
```

```python
import functools
import math

import numpy as np
import jax
import jax.numpy as jnp
from jax import lax
from jax.experimental import pallas as pl
from jax.experimental.pallas import tpu as pltpu

F32 = jnp.float32
BF16 = jnp.bfloat16
MESH = pl.DeviceIdType.MESH

NORM_EPS = 1e-6
NEG_INF = -1e30
LOG2_E = math.log2(math.e)
LN_2 = math.log(2.0)
GRID_W = 64
ROPE_THETA = 10000.0
HEAD_DIM_A = 128
N_Q_HEADS_A = 8
N_KV_HEADS_A = 2
Q_PER_KV = N_Q_HEADS_A // N_KV_HEADS_A
HEAD_DIM_B = 64
N_HEADS_PER_DIL = 4
DILATIONS = (1, 4, 16)
BAND = 64
N_REL_BUCKETS = 32
REL_MAX_DIST = 1024
QA_W = N_Q_HEADS_A * HEAD_DIM_A
KA_W = N_KV_HEADS_A * HEAD_DIM_A
GB_W = N_HEADS_PER_DIL * HEAD_DIM_B
QB_W = GB_W * len(DILATIONS)
OFF_QA, OFF_KA, OFF_VA = 0, QA_W, QA_W + KA_W
OFF_QB = QA_W + 2 * KA_W
OFF_KB = OFF_QB + QB_W
OFF_VB = OFF_KB + QB_W
OFF_GA = OFF_VB + QB_W
N_DEV = 8
LANES = 128
VMEM_LIMIT = 56 * 2 ** 20

ADAM_LR, ADAM_B1, ADAM_B2, ADAM_EPS, ADAM_WD, ADAM_STEP = 0.001, 0.9, 0.999, 1e-08, 0.01, 10


def _cparams(sem):
    return pltpu.CompilerParams(dimension_semantics=sem, vmem_limit_bytes=VMEM_LIMIT)


def _resident(shape):
    nd = len(shape)
    return pl.BlockSpec(shape, lambda *_: (0,) * nd, pipeline_mode=pl.Buffered(1))


def _acc_spec(shape):
    nd = len(shape)
    return pl.BlockSpec(shape, lambda *_: (0,) * nd)


def _rows(tb, c):
    return pl.BlockSpec((tb, c), lambda i: (i, 0))


def _rows3(tb, c):
    return pl.BlockSpec((3, tb, c), lambda i: (0, i, 0))


def _dot_nt(a, b):
    return lax.dot_general(a, b, (((1,), (1,)), ((), ())), preferred_element_type=F32)


def _dot_nn(a, b):
    return lax.dot_general(a, b, (((1,), (0,)), ((), ())), preferred_element_type=F32)


def _dot_tn(a, b):
    return lax.dot_general(a, b, (((0,), (0,)), ((), ())), preferred_element_type=F32)


def _rstd(x):
    return lax.rsqrt(jnp.mean(x * x, axis=-1, keepdims=True) + NORM_EPS)


def _rms_bwd(dy, n, r, g):
    dn = dy * g
    return r * (dn - n * jnp.mean(dn * n, axis=-1, keepdims=True))


def _colsum(v):
    return jnp.sum(v, axis=0, keepdims=True)


def _sigmoid(v):
    return 1.0 / (1.0 + jnp.exp(-v))


def _rope_fwd(n, c, s1, s2):
    return n * c + pltpu.roll(n, 32, 1) * s1 + pltpu.roll(n, 96, 1) * s2


def _rope_bwd(d, c, s1, s2):
    return d * c + pltpu.roll(d * s1, 96, 1) + pltpu.roll(d * s2, 32, 1)


def _rope_tables(s):
    half = HEAD_DIM_A // 2
    inv = jnp.power(ROPE_THETA, -jnp.arange(0, half, 2, dtype=F32) / half)
    t = jnp.arange(s, dtype=jnp.int32)
    ang_r = (t // GRID_W).astype(F32)[:, None] * inv[None, :]
    ang_c = (t % GRID_W).astype(F32)[:, None] * inv[None, :]
    cr, sr, cc, sc = jnp.cos(ang_r), jnp.sin(ang_r), jnp.cos(ang_c), jnp.sin(ang_c)
    z = jnp.zeros_like(sr)
    cos = jnp.concatenate([cr, cr, cc, cc], axis=1)
    s1 = jnp.concatenate([z, sr, z, sc], axis=1)
    s2 = jnp.concatenate([-sr, z, -sc, z], axis=1)
    return cos, s1, s2


def _my_place():
    return lax.axis_index("x"), lax.axis_index("y"), lax.axis_index("c")


def _all_gather(shards):
    nw = len(shards)

    def body(*refs):
        ins, outs = refs[:nw], refs[nw:2 * nw]
        send_sems, recv_sems, local_sems = refs[2 * nw:]
        x, y, c = _my_place()
        me, sibling = (x, y, c), (x, y, 1 - c)
        chips = [(1 - x, y), (x, 1 - y), (1 - x, 1 - y)]

        def rows(w, px, py, pc):
            n = ins[w].shape[0]
            return outs[w].at[pl.ds(pl.multiple_of((4 * px + 2 * py + pc) * n, 16), n), :]

        def copy(w, k, block, to, src=None):
            return pltpu.make_async_remote_copy(
                src_ref=rows(w, *block) if src is None else src, dst_ref=rows(w, *block),
                send_sem=send_sems.at[w, k], recv_sem=recv_sems.at[w, k], device_id=to, device_id_type=MESH)

        mine = [pltpu.make_async_copy(ins[w], rows(w, *me), local_sems.at[w]) for w in range(nw)]
        for cp in mine:
            cp.start()
        first = []
        for w in range(nw):
            first.append(copy(w, 0, me, sibling, src=ins[w]))
            first += [copy(w, 1 + j, me, (*chip, c), src=ins[w]) for j, chip in enumerate(chips)]
        for cp in first:
            cp.start()
        passed = []
        for j, chip in enumerate(chips):
            for w in range(nw):
                copy(w, 1 + j, (*chip, c), me).wait_recv()
                fwd = copy(w, 4 + j, (*chip, c), sibling)
                fwd.start()
                passed.append(fwd)
        for w in range(nw):
            copy(w, 0, sibling, me).wait_recv()
        for j, chip in enumerate(chips):
            for w in range(nw):
                copy(w, 4 + j, (*chip, 1 - c), me).wait_recv()
        for cp in first + passed:
            cp.wait_send()
        for cp in mine:
            cp.wait()

    any_spec = pl.BlockSpec(memory_space=pl.ANY)
    return pl.pallas_call(
        body, name="weights_all_gather",
        out_shape=[jax.ShapeDtypeStruct((N_DEV * s.shape[0], s.shape[1]), s.dtype) for s in shards],
        in_specs=[any_spec] * nw, out_specs=[any_spec] * nw,
        scratch_shapes=[pltpu.SemaphoreType.DMA((nw, 7)), pltpu.SemaphoreType.DMA((nw, 7)),
                        pltpu.SemaphoreType.DMA((nw,))],
    )(*shards)


_FLIPS = [(fx, fy, fc) for fx in (0, 1) for fy in (0, 1) for fc in (0, 1)][1:]


def _scatter_blocks(partials):
    nw = len(partials)

    def body(*refs):
        ins, outs = refs[:nw], refs[nw:2 * nw]
        send_sems, recv_sems, local_sems = refs[2 * nw:]
        x, y, c = _my_place()
        my_idx = 4 * x + 2 * y + c

        def block(w, idx):
            n = outs[w].shape[1]
            return ins[w].at[pl.ds(pl.multiple_of(idx * n, 16), n), :]

        def peer(k):
            fx, fy, fc = _FLIPS[k]
            return (1 - x if fx else x, 1 - y if fy else y, 1 - c if fc else c)

        def copy(w, k):
            to = peer(k)
            to_idx = 4 * to[0] + 2 * to[1] + to[2]
            return pltpu.make_async_remote_copy(
                src_ref=block(w, to_idx), dst_ref=outs[w].at[my_idx],
                send_sem=send_sems.at[w, k], recv_sem=recv_sems.at[w, k], device_id=to, device_id_type=MESH)

        def arrival(w, k):
            frm = peer(k)
            frm_idx = 4 * frm[0] + 2 * frm[1] + frm[2]
            return pltpu.make_async_remote_copy(
                src_ref=block(w, my_idx), dst_ref=outs[w].at[frm_idx],
                send_sem=send_sems.at[w, k], recv_sem=recv_sems.at[w, k], device_id=(x, y, c), device_id_type=MESH)

        mine = [pltpu.make_async_copy(block(w, my_idx), outs[w].at[my_idx], local_sems.at[w]) for w in range(nw)]
        for cp in mine:
            cp.start()
        sends = [copy(w, k) for k in range(7) for w in range(nw)]
        for cp in sends:
            cp.start()
        for k in range(7):
            for w in range(nw):
                arrival(w, k).wait_recv()
        for cp in sends:
            cp.wait_send()
        for cp in mine:
            cp.wait()

    any_spec = pl.BlockSpec(memory_space=pl.ANY)
    return pl.pallas_call(
        body, name="grads_scatter",
        out_shape=[jax.ShapeDtypeStruct((N_DEV, p.shape[0] // N_DEV, p.shape[1]), p.dtype) for p in partials],
        in_specs=[any_spec] * nw, out_specs=[any_spec] * nw,
        scratch_shapes=[pltpu.SemaphoreType.DMA((nw, 7)), pltpu.SemaphoreType.DMA((nw, 7)),
                        pltpu.SemaphoreType.DMA((nw,))],
    )(*partials)


def _small_all_gather(v):
    def body(v_ref, out_ref, send_sems, recv_sems):
        x, y, c = _my_place()
        my_idx = 4 * x + 2 * y + c
        out_ref[my_idx] = v_ref[...]
        sends = []
        for k, (fx, fy, fc) in enumerate(_FLIPS):
            to = (1 - x if fx else x, 1 - y if fy else y, 1 - c if fc else c)
            sends.append(pltpu.make_async_remote_copy(
                src_ref=v_ref, dst_ref=out_ref.at[my_idx], send_sem=send_sems.at[k], recv_sem=recv_sems.at[k],
                device_id=to, device_id_type=MESH))
        for cp in sends:
            cp.start()
        for k, (fx, fy, fc) in enumerate(_FLIPS):
            frm_idx = 4 * (1 - x if fx else x) + 2 * (1 - y if fy else y) + (1 - c if fc else c)
            pltpu.make_async_remote_copy(
                src_ref=v_ref, dst_ref=out_ref.at[frm_idx], send_sem=send_sems.at[k], recv_sem=recv_sems.at[k],
                device_id=(x, y, c), device_id_type=MESH).wait_recv()
        for cp in sends:
            cp.wait_send()

    vm = pl.BlockSpec(memory_space=pltpu.VMEM)
    return pl.pallas_call(
        body, name="small_all_gather", out_shape=jax.ShapeDtypeStruct((N_DEV,) + v.shape, v.dtype),
        in_specs=[vm], out_specs=vm,
        scratch_shapes=[pltpu.SemaphoreType.DMA((7,)), pltpu.SemaphoreType.DMA((7,))],
    )(v)


def _in_proj(x, tabs, w_in_t, g_mix, b_gate, q_g, k_g, tb):
    s, d = x.shape
    n_gate_chunks = d // 256
    q_scale = HEAD_DIM_A ** -0.5 * LOG2_E
    b_scale = HEAD_DIM_B ** -0.5

    def body(x_ref, c_ref, s1_ref, s2_ref, w_ref, gmix_ref, bg_ref, qg_ref, kg_ref,
             h1_ref, qraw_ref, kraw_ref, qrot_ref, krot_ref, va_ref, qb_ref, kb_ref, vb_ref, ga_ref, gb_ref):
        xv = x_ref[...]
        hb = (xv * _rstd(xv) * gmix_ref[...]).astype(BF16)
        h1_ref[...] = hb
        cos, s1, s2 = c_ref[...], s1_ref[...], s2_ref[...]

        def proj(lo, width):
            return _dot_nt(hb, w_ref[lo:lo + width, :])

        def norm_rope(z, g):
            return _rope_fwd(z * _rstd(z) * g, cos, s1, s2)

        for j in range(QA_W // 256):
            z = proj(OFF_QA + 256 * j, 256)
            qraw_ref[:, 256 * j:256 * j + 256] = z
            for hh in range(2):
                lo = 256 * j + 128 * hh
                qrot_ref[:, lo:lo + 128] = (norm_rope(z[:, 128 * hh:128 * hh + 128], qg_ref[...]) * q_scale).astype(BF16)
        z = proj(OFF_KA, 256)
        kraw_ref[...] = z
        for hh in range(2):
            krot_ref[:, 128 * hh:128 * hh + 128] = norm_rope(z[:, 128 * hh:128 * hh + 128], kg_ref[...]).astype(BF16)
        va_ref[...] = proj(OFF_VA, 256).astype(BF16)
        for g in range(3):
            qb_ref[g] = (proj(OFF_QB + GB_W * g, GB_W) * b_scale).astype(BF16)
            kb_ref[g] = proj(OFF_KB + GB_W * g, GB_W).astype(BF16)
            vb_ref[g] = proj(OFF_VB + GB_W * g, GB_W).astype(BF16)
        for j in range(n_gate_chunks):
            sl = slice(256 * j, 256 * j + 256)
            ga_ref[:, sl] = _sigmoid(proj(OFF_GA + 256 * j, 256) + bg_ref[:, sl])
            gb_ref[:, sl] = _sigmoid(proj(OFF_GA + d + 256 * j, 256) + bg_ref[:, d + 256 * j:d + 256 * j + 256])

    sd = jax.ShapeDtypeStruct
    outs = [sd((s, d), BF16), sd((s, QA_W), F32), sd((s, KA_W), F32), sd((s, QA_W), BF16), sd((s, KA_W), BF16),
            sd((s, KA_W), BF16), sd((3, s, GB_W), BF16), sd((3, s, GB_W), BF16), sd((3, s, GB_W), BF16),
            sd((s, d), F32), sd((s, d), F32)]
    out_specs = [_rows(tb, d), _rows(tb, QA_W), _rows(tb, KA_W), _rows(tb, QA_W), _rows(tb, KA_W), _rows(tb, KA_W),
                 _rows3(tb, GB_W), _rows3(tb, GB_W), _rows3(tb, GB_W), _rows(tb, d), _rows(tb, d)]
    in_specs = [_rows(tb, d), _rows(tb, LANES), _rows(tb, LANES), _rows(tb, LANES), _resident(w_in_t.shape),
                _resident(g_mix.shape), _resident(b_gate.shape), _resident(q_g.shape), _resident(k_g.shape)]
    return pl.pallas_call(body, name="in_proj", grid=(s // tb,), in_specs=in_specs, out_specs=out_specs,
                          out_shape=outs, compiler_params=_cparams(("arbitrary",)))(
        x, *tabs, w_in_t, g_mix, b_gate, q_g, k_g)


def _attn_a_fwd(qrot, krot, va, tq, tk):
    s = qrot.shape[0]
    n_kv = s // tk
    gw = Q_PER_KV * HEAD_DIM_A

    def body(q_ref, k_ref, v_ref, o_ref, lse_ref):
        q4 = jnp.concatenate([q_ref[:, 128 * h:128 * h + 128] for h in range(Q_PER_KV)], axis=0)

        def step(j, carry):
            m, l, acc = carry
            sl = pl.ds(pl.multiple_of(j * tk, tk), tk)
            kj, vj = k_ref[sl, :], v_ref[sl, :]
            sc = _dot_nt(q4, kj)
            m_new = jnp.maximum(m, jnp.max(sc, axis=-1, keepdims=True))
            p = jnp.exp2(sc - m_new)
            alpha = jnp.exp2(m - m_new)
            l = alpha * l + jnp.sum(p, axis=-1, keepdims=True)
            acc = alpha * acc + _dot_nn(p.astype(BF16), vj)
            return m_new, l, acc

        rows = Q_PER_KV * tq
        m, l, acc = lax.fori_loop(0, n_kv, step, (jnp.full((rows, 1), NEG_INF, F32), jnp.zeros((rows, 1), F32),
                                                  jnp.zeros((rows, HEAD_DIM_A), F32)))
        o = acc / l
        lse = m + jnp.log2(l)
        for h in range(Q_PER_KV):
            o_ref[:, 128 * h:128 * h + 128] = o[h * tq:(h + 1) * tq].astype(BF16)
            lse_ref[0, :, h:h + 1] = lse[h * tq:(h + 1) * tq]

    return pl.pallas_call(
        body, name="attn_a_fwd", grid=(N_KV_HEADS_A, s // tq),
        in_specs=[pl.BlockSpec((tq, gw), lambda g, i: (i, g)),
                  pl.BlockSpec((s, HEAD_DIM_A), lambda g, i: (0, g)),
                  pl.BlockSpec((s, HEAD_DIM_A), lambda g, i: (0, g))],
        out_specs=[pl.BlockSpec((tq, gw), lambda g, i: (i, g)),
                   pl.BlockSpec((1, tq, Q_PER_KV), lambda g, i: (g, i, 0))],
        out_shape=[jax.ShapeDtypeStruct((s, QA_W), BF16), jax.ShapeDtypeStruct((N_KV_HEADS_A, s, Q_PER_KV), F32)],
        compiler_params=_cparams(("arbitrary", "arbitrary")))(qrot, krot, va)


def _attn_a_bwd(qrot, krot, va, oa, doa, lse, tq, tk):
    s = qrot.shape[0]
    n_kv = s // tk
    gw = Q_PER_KV * HEAD_DIM_A

    def body(q_ref, do_ref, o_ref, lse_ref, k_ref, v_ref, dq_ref, dk_ref, dv_ref):
        @pl.when(pl.program_id(1) == 0)
        def _():
            dk_ref[...] = jnp.zeros_like(dk_ref)
            dv_ref[...] = jnp.zeros_like(dv_ref)

        def stack(ref):
            return jnp.concatenate([ref[:, 128 * h:128 * h + 128] for h in range(Q_PER_KV)], axis=0)

        q4, do4, o4 = stack(q_ref), stack(do_ref), stack(o_ref)
        delta = jnp.sum(do4.astype(F32) * o4.astype(F32), axis=-1, keepdims=True)
        lse4 = jnp.concatenate([lse_ref[0, :, h:h + 1] for h in range(Q_PER_KV)], axis=0)

        def step(j, dq):
            sl = pl.ds(pl.multiple_of(j * tk, tk), tk)
            kj, vj = k_ref[sl, :], v_ref[sl, :]
            p = jnp.exp2(_dot_nt(q4, kj) - lse4)
            ds = (p * (_dot_nt(do4, vj) - delta)).astype(BF16)
            dk_ref[sl, :] += _dot_tn(ds, q4)
            dv_ref[sl, :] += _dot_tn(p.astype(BF16), do4)
            return dq + _dot_nn(ds, kj)

        dq = lax.fori_loop(0, n_kv, step, jnp.zeros((Q_PER_KV * tq, HEAD_DIM_A), F32))
        for h in range(Q_PER_KV):
            dq_ref[:, 128 * h:128 * h + 128] = dq[h * tq:(h + 1) * tq]

    qspec = pl.BlockSpec((tq, gw), lambda g, i: (i, g))
    kspec = pl.BlockSpec((s, HEAD_DIM_A), lambda g, i: (0, g))
    return pl.pallas_call(
        body, name="attn_a_bwd", grid=(N_KV_HEADS_A, s // tq),
        in_specs=[qspec, qspec, qspec, pl.BlockSpec((1, tq, Q_PER_KV), lambda g, i: (g, i, 0)), kspec, kspec],
        out_specs=[qspec, kspec, kspec],
        out_shape=[jax.ShapeDtypeStruct((s, QA_W), F32), jax.ShapeDtypeStruct((s, KA_W), F32),
                   jax.ShapeDtypeStruct((s, KA_W), F32)],
        compiler_params=_cparams(("arbitrary", "arbitrary")))(qrot, doa, oa, lse, krot, va)


def _band_specs(s, cb):
    per = cb // BAND
    last = s // BAND - 1
    cur = pl.BlockSpec((1, cb, GB_W), lambda g, i: (g, i, 0))
    prev = pl.BlockSpec((1, BAND, GB_W), lambda g, i: (g, jnp.maximum(i * per - 1, 0), 0))
    nxt = pl.BlockSpec((1, BAND, GB_W), lambda g, i: (g, jnp.minimum(i * per + per, last), 0))
    return cur, prev, nxt


def _blocks_per_segment(s):
    g = pl.program_id(0)
    return jnp.where(g == 0, s // (BAND * DILATIONS[0]),
                     jnp.where(g == 1, s // (BAND * DILATIONS[1]), s // (BAND * DILATIONS[2])))


def _window(prev_ref, cur_ref, next_ref):
    return jnp.concatenate([prev_ref[0], cur_ref[0], next_ref[0]], axis=0)


def _neighbour_valid(b, nbseg):
    bm = lax.rem(b, nbseg)
    return bm != 0, bm != nbseg - 1


def _mask_qk(prev_ok, next_ok):
    qq = lax.broadcasted_iota(jnp.int32, (BAND, 3 * BAND), 0)
    kk = lax.broadcasted_iota(jnp.int32, (BAND, 3 * BAND), 1)
    off = kk - BAND - qq
    return (jnp.abs(off) <= BAND) & ((kk >= BAND) | prev_ok) & ((kk < 2 * BAND) | next_ok)


def _mask_kq(prev_ok, next_ok):
    qq = lax.broadcasted_iota(jnp.int32, (3 * BAND, BAND), 0)
    kk = lax.broadcasted_iota(jnp.int32, (3 * BAND, BAND), 1)
    off = kk + BAND - qq
    return (jnp.abs(off) <= BAND) & ((qq >= BAND) | prev_ok) & ((qq < 2 * BAND) | next_ok)


def _head_lane_masks():
    lane = lax.broadcasted_iota(jnp.int32, (1, LANES), 1)
    return [lane < HEAD_DIM_B, lane >= HEAD_DIM_B]


def _band_fwd(qb, kb, vb, bias, cb):
    s = qb.shape[1]
    per = cb // BAND

    def body(q_ref, kp_ref, kc_ref, kn_ref, vp_ref, vc_ref, vn_ref, bias_ref, o_ref, lse_ref):
        nbseg = _blocks_per_segment(s)
        kw, vw = _window(kp_ref, kc_ref, kn_ref), _window(vp_ref, vc_ref, vn_ref)
        hm = _head_lane_masks()
        for jj in range(per):
            prev_ok, next_ok = _neighbour_valid(pl.program_id(1) * per + jj, nbseg)
            mask = _mask_qk(prev_ok, next_ok)
            r0 = BAND * jj
            for hp in range(2):
                ls = slice(LANES * hp, LANES * hp + LANES)
                qh = q_ref[0, r0:r0 + BAND, ls]
                k3, v3 = kw[r0:r0 + 3 * BAND, ls], vw[r0:r0 + 3 * BAND, ls]
                o_half = jnp.zeros((BAND, LANES), F32)
                lse_half = jnp.zeros((BAND, LANES), F32)
                for hh in range(2):
                    sc = _dot_nt(jnp.where(hm[hh], qh, jnp.zeros_like(qh)), k3) + bias_ref[0, 2 * hp + hh]
                    sc = jnp.where(mask, sc, NEG_INF)
                    m = jnp.max(sc, axis=-1, keepdims=True)
                    lse = m + jnp.log(jnp.sum(jnp.exp(sc - m), axis=-1, keepdims=True))
                    p = jnp.exp(sc - lse).astype(BF16)
                    o_half = o_half + _dot_nn(p, jnp.where(hm[hh], v3, jnp.zeros_like(v3)))
                    lse_half = jnp.where(hm[hh], lse, lse_half)
                o_ref[0, r0:r0 + BAND, ls] = o_half
                lse_ref[0, r0:r0 + BAND, ls] = lse_half

    cur, prev, nxt = _band_specs(s, cb)
    bias_spec = pl.BlockSpec((1, N_HEADS_PER_DIL, BAND, 3 * BAND), lambda g, i: (g, 0, 0, 0))
    return pl.pallas_call(
        body, name="band_fwd", grid=(3, s // cb),
        in_specs=[cur, prev, cur, nxt, prev, cur, nxt, bias_spec], out_specs=[cur, cur],
        out_shape=[jax.ShapeDtypeStruct(qb.shape, F32), jax.ShapeDtypeStruct(qb.shape, F32)],
        compiler_params=_cparams(("arbitrary", "arbitrary")))(qb, kb, kb, kb, vb, vb, vb, bias)


def _band_bwd_q(qb, kb, vb, dob, lse, dd, bias, cb):
    s = qb.shape[1]
    per = cb // BAND

    def body(q_ref, do_ref, lse_ref, dd_ref, kp_ref, kc_ref, kn_ref, vp_ref, vc_ref, vn_ref, bias_ref,
             dq_ref, dsum_ref):
        @pl.when(pl.program_id(1) == 0)
        def _():
            dsum_ref[...] = jnp.zeros_like(dsum_ref)

        nbseg = _blocks_per_segment(s)
        kw, vw = _window(kp_ref, kc_ref, kn_ref), _window(vp_ref, vc_ref, vn_ref)
        hm = _head_lane_masks()
        for jj in range(per):
            prev_ok, next_ok = _neighbour_valid(pl.program_id(1) * per + jj, nbseg)
            mask = _mask_qk(prev_ok, next_ok)
            r0 = BAND * jj
            for hp in range(2):
                ls = slice(LANES * hp, LANES * hp + LANES)
                qh, doh = q_ref[0, r0:r0 + BAND, ls], do_ref[0, r0:r0 + BAND, ls]
                k3, v3 = kw[r0:r0 + 3 * BAND, ls], vw[r0:r0 + 3 * BAND, ls]
                dq_half = jnp.zeros((BAND, LANES), F32)
                for hh in range(2):
                    h = 2 * hp + hh
                    col = LANES * hp + HEAD_DIM_B * hh
                    sc = _dot_nt(jnp.where(hm[hh], qh, jnp.zeros_like(qh)), k3) + bias_ref[0, h]
                    sc = jnp.where(mask, sc, NEG_INF)
                    p = jnp.exp(sc - lse_ref[0, r0:r0 + BAND, col:col + 1])
                    dp = _dot_nt(jnp.where(hm[hh], doh, jnp.zeros_like(doh)), v3)
                    ds = p * (dp - dd_ref[0, r0:r0 + BAND, col:col + 1])
                    dsum_ref[0, h] += ds
                    dq_half = dq_half + _dot_nn(ds.astype(BF16), jnp.where(hm[hh], k3, jnp.zeros_like(k3)))
                dq_ref[0, r0:r0 + BAND, ls] = dq_half

    cur, prev, nxt = _band_specs(s, cb)
    bias_spec = pl.BlockSpec((1, N_HEADS_PER_DIL, BAND, 3 * BAND), lambda g, i: (g, 0, 0, 0))
    return pl.pallas_call(
        body, name="band_bwd_q", grid=(3, s // cb),
        in_specs=[cur, cur, cur, cur, prev, cur, nxt, prev, cur, nxt, bias_spec], out_specs=[cur, bias_spec],
        out_shape=[jax.ShapeDtypeStruct(qb.shape, F32),
                   jax.ShapeDtypeStruct((3, N_HEADS_PER_DIL, BAND, 3 * BAND), F32)],
        compiler_params=_cparams(("arbitrary", "arbitrary")))(qb, dob, lse, dd, kb, kb, kb, vb, vb, vb, bias)


def _band_bwd_kv(qb, kb, vb, dob, lse, dd, bias_t, cb):
    s = qb.shape[1]
    per = cb // BAND

    def body(k_ref, v_ref, qp_ref, qc_ref, qn_ref, dp_ref, dc_ref, dn_ref, lp_ref, lc_ref, ln_ref,
             ep_ref, ec_ref, en_ref, bias_ref, dk_ref, dv_ref):
        nbseg = _blocks_per_segment(s)
        qw, dow = _window(qp_ref, qc_ref, qn_ref), _window(dp_ref, dc_ref, dn_ref)
        lw, ew = _window(lp_ref, lc_ref, ln_ref), _window(ep_ref, ec_ref, en_ref)
        hm = _head_lane_masks()
        for jj in range(per):
            prev_ok, next_ok = _neighbour_valid(pl.program_id(1) * per + jj, nbseg)
            mask = _mask_kq(prev_ok, next_ok)
            r0 = BAND * jj
            for hp in range(2):
                ls = slice(LANES * hp, LANES * hp + LANES)
                kh, vh = k_ref[0, r0:r0 + BAND, ls], v_ref[0, r0:r0 + BAND, ls]
                q3, do3 = qw[r0:r0 + 3 * BAND, ls], dow[r0:r0 + 3 * BAND, ls]
                dk_half = jnp.zeros((BAND, LANES), F32)
                dv_half = jnp.zeros((BAND, LANES), F32)
                for hh in range(2):
                    col = LANES * hp + HEAD_DIM_B * hh
                    q3m = jnp.where(hm[hh], q3, jnp.zeros_like(q3))
                    do3m = jnp.where(hm[hh], do3, jnp.zeros_like(do3))
                    sc = _dot_nt(q3m, kh) + bias_ref[0, 2 * hp + hh]
                    sc = jnp.where(mask, sc, NEG_INF)
                    p = jnp.exp(sc - lw[r0:r0 + 3 * BAND, col:col + 1])
                    ds = p * (_dot_nt(do3m, vh) - ew[r0:r0 + 3 * BAND, col:col + 1])
                    dk_half = dk_half + _dot_tn(ds.astype(BF16), q3m)
                    dv_half = dv_half + _dot_tn(p.astype(BF16), do3m)
                dk_ref[0, r0:r0 + BAND, ls] = dk_half
                dv_ref[0, r0:r0 + BAND, ls] = dv_half

    cur, prev, nxt = _band_specs(s, cb)
    win = [prev, cur, nxt]
    bias_spec = pl.BlockSpec((1, N_HEADS_PER_DIL, 3 * BAND, BAND), lambda g, i: (g, 0, 0, 0))
    return pl.pallas_call(
        body, name="band_bwd_kv", grid=(3, s // cb),
        in_specs=[cur, cur] + win * 4 + [bias_spec], out_specs=[cur, cur],
        out_shape=[jax.ShapeDtypeStruct(qb.shape, F32), jax.ShapeDtypeStruct(qb.shape, F32)],
        compiler_params=_cparams(("arbitrary", "arbitrary")))(
        kb, vb, qb, qb, qb, dob, dob, dob, lse, lse, lse, dd, dd, dd, bias_t)


def _t5_bucket(rel):
    nb = N_REL_BUCKETS // 2
    ret = (rel > 0).astype(np.int32) * nb
    n = np.abs(rel)
    max_exact = nb // 2
    large = max_exact + (np.log(np.maximum(n, 1) / max_exact) / math.log(REL_MAX_DIST / max_exact)
                         * (nb - max_exact)).astype(np.int32)
    large = np.minimum(large, nb - 1)
    return ret + np.where(n < max_exact, n, large).astype(np.int32)


def _bucket_onehots():
    qq, kk = np.arange(BAND)[:, None], np.arange(3 * BAND)[None, :]
    off_qk = kk - BAND - qq
    off_kq = (np.arange(BAND)[None, :] + BAND - np.arange(3 * BAND)[:, None])
    out = []
    for off in (off_qk, off_kq):
        per_group = []
        for d in DILATIONS:
            bucket = np.where(np.abs(off) <= BAND, _t5_bucket(off * d), -1).reshape(-1)
            per_group.append(bucket[None, :] == np.arange(N_REL_BUCKETS)[:, None])
        out.append(np.stack(per_group))
    return out


def _band_bias(rel_bias, onehot, shape):
    tab = rel_bias.reshape(N_REL_BUCKETS, 3, N_HEADS_PER_DIL)
    b = jnp.einsum("gbn,bgh->ghn", onehot.astype(F32), tab, precision=lax.Precision.HIGHEST)
    return b.reshape(3, N_HEADS_PER_DIL, *shape)


def _rel_bias_grad(dsum, onehot):
    def body(d_ref, oh_ref, out_ref):
        for g in range(3):
            v = d_ref[g]
            oh = oh_ref[g]
            acc = jnp.zeros((N_HEADS_PER_DIL, N_REL_BUCKETS), F32)
            for _ in range(3):
                part = v.astype(BF16)
                acc = acc + _dot_nt(part, oh)
                v = v - part.astype(F32)
            out_ref[g] = acc

    return pl.pallas_call(body, name="rel_bias_grad",
                          out_shape=jax.ShapeDtypeStruct((3, N_HEADS_PER_DIL, N_REL_BUCKETS), F32))(dsum, onehot)


def _seg_sum(v):
    lane = lax.broadcasted_iota(jnp.int32, (1, v.shape[1]), 1)
    out = jnp.zeros_like(v)
    for h in range(v.shape[1] // HEAD_DIM_B):
        m = (lane >= HEAD_DIM_B * h) & (lane < HEAD_DIM_B * (h + 1))
        out = jnp.where(m, jnp.sum(jnp.where(m, v, 0.0), axis=-1, keepdims=True), out)
    return out


def _mix_out(x, oa, og, lg, ga, gb, w_oa, w_ob_t, w_o, tb):
    s, d = x.shape

    def body(x_ref, oa_ref, og_ref, lg_ref, ga_ref, gb_ref, woa_ref, wob_ref, wo_ref,
             x2_ref, ob_ref, lse_ref, ya_ref, yb_ref, u_ref):
        l0, l1, l2 = lg_ref[0], lg_ref[1], lg_ref[2]
        lmax = jnp.maximum(jnp.maximum(l0, l1), l2)
        w0, w1, w2 = jnp.exp(l0 - lmax), jnp.exp(l1 - lmax), jnp.exp(l2 - lmax)
        den = w0 + w1 + w2
        ob = ((w0 * og_ref[0] + w1 * og_ref[1] + w2 * og_ref[2]) / den).astype(BF16)
        ob_ref[...] = ob
        lse_ref[...] = lmax + jnp.log(den)
        ya = _dot_nn(oa_ref[...], woa_ref[...])
        yb = _dot_nt(ob, wob_ref[...])
        ya_ref[...] = ya.astype(BF16)
        yb_ref[...] = yb.astype(BF16)
        u = (ga_ref[...] * ya + gb_ref[...] * yb).astype(BF16)
        u_ref[...] = u
        x2_ref[...] = x_ref[...] + _dot_nn(u, wo_ref[...])

    sd = jax.ShapeDtypeStruct
    return pl.pallas_call(
        body, name="mix_out", grid=(s // tb,),
        in_specs=[_rows(tb, d), _rows(tb, QA_W), _rows3(tb, GB_W), _rows3(tb, GB_W), _rows(tb, d), _rows(tb, d),
                  _resident(w_oa.shape), _resident(w_ob_t.shape), _resident(w_o.shape)],
        out_specs=[_rows(tb, d), _rows(tb, GB_W), _rows(tb, GB_W), _rows(tb, d), _rows(tb, d), _rows(tb, d)],
        out_shape=[sd((s, d), F32), sd((s, GB_W), BF16), sd((s, GB_W), F32), sd((s, d), BF16), sd((s, d), BF16),
                   sd((s, d), BF16)],
        compiler_params=_cparams(("arbitrary",)))(x, oa, og, lg, ga, gb, w_oa, w_ob_t, w_o)


def _mlp_fwd(x2, w1_t, w2, g_mlp, tb, tc):
    s, d = x2.shape
    dff = w1_t.shape[0]

    def body(x_ref, w1_ref, w2_ref, g_ref, x3_ref, r_ref, h_ref):
        xv = x_ref[...]
        hb = (xv * _rstd(xv) * g_ref[...]).astype(BF16)
        h_ref[...] = hb
        x3_ref[...] = xv
        for c in range(dff // tc):
            sl = slice(tc * c, tc * c + tc)
            r = jnp.maximum(_dot_nt(hb, w1_ref[sl, :]), 0.0)
            r_ref[:, sl] = r.astype(BF16)
            x3_ref[...] += _dot_nn((r * r).astype(BF16), w2_ref[sl, :])

    sd = jax.ShapeDtypeStruct
    return pl.pallas_call(
        body, name="mlp_fwd", grid=(s // tb,),
        in_specs=[_rows(tb, d), _resident(w1_t.shape), _resident(w2.shape), _resident(g_mlp.shape)],
        out_specs=[_rows(tb, d), _rows(tb, dff), _rows(tb, d)],
        out_shape=[sd((s, d), F32), sd((s, dff), BF16), sd((s, d), BF16)],
        compiler_params=_cparams(("arbitrary",)))(x2, w1_t, w2, g_mlp)


def _ple_loss(x3, p, target, w_pg, w_p_t, g_ple, g_fin, tb):
    s, d = x3.shape
    dp = p.shape[1]

    def body(x_ref, p_ref, t_ref, wpg_ref, wp_ref, gple_ref, gfin_ref,
             dx3_ref, h3_ref, dpre_ref, dpe_ref, pb_ref, loss_ref, dgfin_ref, dgple_ref):
        @pl.when(pl.program_id(0) == 0)
        def _():
            loss_ref[...] = jnp.zeros_like(loss_ref)
            dgfin_ref[...] = jnp.zeros_like(dgfin_ref)
            dgple_ref[...] = jnp.zeros_like(dgple_ref)

        x3v = x_ref[...]
        r3 = _rstd(x3v)
        n3 = x3v * r3
        h3 = (n3 * gple_ref[...]).astype(BF16)
        h3_ref[...] = h3
        gp = _sigmoid(_dot_nn(h3, wpg_ref[...]))
        pb = p_ref[...].astype(BF16)
        pb_ref[...] = pb
        pe = _dot_nt(pb, wp_ref[...])
        x4 = x3v + gp * pe
        r4 = _rstd(x4)
        n4 = x4 * r4
        err = n4 * gfin_ref[...] - t_ref[...]
        loss_ref[...] += jnp.sum(0.5 * jnp.mean(err * err, axis=-1, keepdims=True), axis=0, keepdims=True)
        dy = err / d
        dgfin_ref[...] += _colsum(dy * n4)
        dx4 = _rms_bwd(dy, n4, r4, gfin_ref[...])
        dpe_ref[...] = (dx4 * gp).astype(BF16)
        dpre = (dx4 * pe * gp * (1.0 - gp)).astype(BF16)
        dpre_ref[...] = dpre
        dh3 = _dot_nt(dpre, wpg_ref[...])
        dgple_ref[...] += _colsum(dh3 * n3)
        dx3_ref[...] = dx4 + _rms_bwd(dh3, n3, r3, gple_ref[...])

    sd = jax.ShapeDtypeStruct
    return pl.pallas_call(
        body, name="ple_loss", grid=(s // tb,),
        in_specs=[_rows(tb, d), _rows(tb, dp), _rows(tb, d), _resident(w_pg.shape), _resident(w_p_t.shape),
                  _resident(g_ple.shape), _resident(g_fin.shape)],
        out_specs=[_rows(tb, d), _rows(tb, d), _rows(tb, d), _rows(tb, d), _rows(tb, dp),
                   _acc_spec((1, LANES)), _acc_spec((1, d)), _acc_spec((1, d))],
        out_shape=[sd((s, d), F32), sd((s, d), BF16), sd((s, d), BF16), sd((s, d), BF16), sd((s, dp), BF16),
                   sd((1, LANES), F32), sd((1, d), F32), sd((1, d), F32)],
        compiler_params=_cparams(("arbitrary",)))(x3, p, target, w_pg, w_p_t, g_ple, g_fin)


def _mlp_bwd(dx3, x2, r, w1_t, w2, g_mlp, tb, tc):
    s, d = x2.shape
    dff = w1_t.shape[0]

    def body(dx3_ref, x_ref, r_ref, w1_ref, w2_ref, g_ref, dx2_ref, df_ref, dg_ref, dh_ref):
        @pl.when(pl.program_id(0) == 0)
        def _():
            dg_ref[...] = jnp.zeros_like(dg_ref)

        dx3v = dx3_ref[...]
        dx3b = dx3v.astype(BF16)
        dh_ref[...] = jnp.zeros_like(dh_ref)
        for c in range(dff // tc):
            sl = slice(tc * c, tc * c + tc)
            df = (_dot_nt(dx3b, w2_ref[sl, :]) * (2.0 * r_ref[:, sl].astype(F32))).astype(BF16)
            df_ref[:, sl] = df
            dh_ref[...] += _dot_nn(df, w1_ref[sl, :])
        xv = x_ref[...]
        r2 = _rstd(xv)
        n2 = xv * r2
        dh = dh_ref[...]
        dg_ref[...] += _colsum(dh * n2)
        dx2_ref[...] = dx3v + _rms_bwd(dh, n2, r2, g_ref[...])

    sd = jax.ShapeDtypeStruct
    return pl.pallas_call(
        body, name="mlp_bwd", grid=(s // tb,),
        in_specs=[_rows(tb, d), _rows(tb, d), _rows(tb, dff), _resident(w1_t.shape), _resident(w2.shape),
                  _resident(g_mlp.shape)],
        out_specs=[_rows(tb, d), _rows(tb, dff), _acc_spec((1, d))],
        out_shape=[sd((s, d), F32), sd((s, dff), BF16), sd((1, d), F32)],
        scratch_shapes=[pltpu.VMEM((tb, d), F32)],
        compiler_params=_cparams(("arbitrary",)))(dx3, x2, r, w1_t, w2, g_mlp)


def _mix_out_bwd(dx2, ya, yb, ga, gb, ob, w_oa, w_ob_t, w_o, tb):
    s, d = dx2.shape

    def body(dx_ref, ya_ref, yb_ref, ga_ref, gb_ref, ob_ref, woa_ref, wob_ref, wo_ref,
             doa_ref, dob_ref, dd_ref, dga_ref, dgb_ref, dya_ref, dyb_ref, dbg_ref):
        @pl.when(pl.program_id(0) == 0)
        def _():
            dbg_ref[...] = jnp.zeros_like(dbg_ref)

        du = _dot_nt(dx_ref[...].astype(BF16), wo_ref[...])
        gav, gbv = ga_ref[...], gb_ref[...]
        dya = (du * gav).astype(BF16)
        dyb = (du * gbv).astype(BF16)
        dya_ref[...] = dya
        dyb_ref[...] = dyb
        dga = du * ya_ref[...].astype(F32) * gav * (1.0 - gav)
        dgb = du * yb_ref[...].astype(F32) * gbv * (1.0 - gbv)
        dga_ref[...] = dga.astype(BF16)
        dgb_ref[...] = dgb.astype(BF16)
        dbg_ref[:, 0:d] += _colsum(dga)
        dbg_ref[:, d:2 * d] += _colsum(dgb)
        doa_ref[...] = _dot_nt(dya, woa_ref[...]).astype(BF16)
        dob = _dot_nn(dyb, wob_ref[...])
        dob_ref[...] = dob.astype(BF16)
        dd_ref[...] = _seg_sum(dob * ob_ref[...].astype(F32))

    sd = jax.ShapeDtypeStruct
    return pl.pallas_call(
        body, name="mix_out_bwd", grid=(s // tb,),
        in_specs=[_rows(tb, d)] * 5 + [_rows(tb, GB_W), _resident(w_oa.shape), _resident(w_ob_t.shape),
                                       _resident(w_o.shape)],
        out_specs=[_rows(tb, QA_W), _rows(tb, GB_W), _rows(tb, GB_W), _rows(tb, d), _rows(tb, d), _rows(tb, d),
                   _rows(tb, d), _acc_spec((1, 2 * d))],
        out_shape=[sd((s, QA_W), BF16), sd((s, GB_W), BF16), sd((s, GB_W), F32), sd((s, d), BF16), sd((s, d), BF16),
                   sd((s, d), BF16), sd((s, d), BF16), sd((1, 2 * d), F32)],
        compiler_params=_cparams(("arbitrary",)))(dx2, ya, yb, ga, gb, ob, w_oa, w_ob_t, w_o)


def _in_proj_bwd(dx2, x, dqrot, dkrot, dva, qraw, kraw, tabs, dqb, dkb, dvb, dga, dgb, w_in_t, g_mix, q_g, k_g, tb):
    s, d = x.shape
    din = w_in_t.shape[0]
    q_scale = HEAD_DIM_A ** -0.5
    b_scale = HEAD_DIM_B ** -0.5
    tc = 256

    def body(dx2_ref, x_ref, dq_ref, dk_ref, dv_ref, qraw_ref, kraw_ref, c_ref, s1_ref, s2_ref,
             dqb_ref, dkb_ref, dvb_ref, dga_ref, dgb_ref, w_ref, gmix_ref, qg_ref, kg_ref,
             dx_ref, dz_ref, dgmix_ref, dqg_ref, dkg_ref, dh_ref):
        @pl.when(pl.program_id(0) == 0)
        def _():
            dgmix_ref[...] = jnp.zeros_like(dgmix_ref)
            dqg_ref[...] = jnp.zeros_like(dqg_ref)
            dkg_ref[...] = jnp.zeros_like(dkg_ref)

        cos, s1, s2 = c_ref[...], s1_ref[...], s2_ref[...]

        def head_bwd(drot, z, g_ref, acc_ref):
            dn = _rope_bwd(drot, cos, s1, s2)
            rr = _rstd(z)
            nn = z * rr
            acc_ref[...] += _colsum(dn * nn)
            return _rms_bwd(dn, nn, rr, g_ref[...])

        for h in range(N_Q_HEADS_A):
            sl = slice(128 * h, 128 * h + 128)
            dz_ref[:, OFF_QA + 128 * h:OFF_QA + 128 * h + 128] = head_bwd(
                dq_ref[:, sl] * q_scale, qraw_ref[:, sl], qg_ref, dqg_ref).astype(BF16)
        for h in range(N_KV_HEADS_A):
            sl = slice(128 * h, 128 * h + 128)
            dz_ref[:, OFF_KA + 128 * h:OFF_KA + 128 * h + 128] = head_bwd(
                dk_ref[:, sl] * LN_2, kraw_ref[:, sl], kg_ref, dkg_ref).astype(BF16)
        dz_ref[:, OFF_VA:OFF_VA + KA_W] = dv_ref[...].astype(BF16)
        for g in range(3):
            dz_ref[:, OFF_QB + GB_W * g:OFF_QB + GB_W * (g + 1)] = (dqb_ref[g] * b_scale).astype(BF16)
            dz_ref[:, OFF_KB + GB_W * g:OFF_KB + GB_W * (g + 1)] = dkb_ref[g].astype(BF16)
            dz_ref[:, OFF_VB + GB_W * g:OFF_VB + GB_W * (g + 1)] = dvb_ref[g].astype(BF16)
        dz_ref[:, OFF_GA:OFF_GA + d] = dga_ref[...]
        dz_ref[:, OFF_GA + d:OFF_GA + 2 * d] = dgb_ref[...]
        dh_ref[...] = jnp.zeros_like(dh_ref)
        for c in range(din // tc):
            sl = slice(tc * c, tc * c + tc)
            dh_ref[...] += _dot_nn(dz_ref[:, sl], w_ref[sl, :])
        xv = x_ref[...]
        r1 = _rstd(xv)
        n1 = xv * r1
        dh = dh_ref[...]
        dgmix_ref[...] += _colsum(dh * n1)
        dx_ref[...] = dx2_ref[...] + _rms_bwd(dh, n1, r1, gmix_ref[...])

    sd = jax.ShapeDtypeStruct
    return pl.pallas_call(
        body, name="in_proj_bwd", grid=(s // tb,),
        in_specs=[_rows(tb, d), _rows(tb, d), _rows(tb, QA_W), _rows(tb, KA_W), _rows(tb, KA_W), _rows(tb, QA_W),
                  _rows(tb, KA_W), _rows(tb, LANES), _rows(tb, LANES), _rows(tb, LANES),
                  _rows3(tb, GB_W), _rows3(tb, GB_W), _rows3(tb, GB_W), _rows(tb, d), _rows(tb, d),
                  _resident(w_in_t.shape), _resident(g_mix.shape), _resident(q_g.shape), _resident(k_g.shape)],
        out_specs=[_rows(tb, d), _rows(tb, din), _acc_spec((1, d)), _acc_spec((1, HEAD_DIM_A)),
                   _acc_spec((1, HEAD_DIM_A))],
        out_shape=[sd((s, d), F32), sd((s, din), BF16), sd((1, d), F32), sd((1, HEAD_DIM_A), F32),
                   sd((1, HEAD_DIM_A), F32)],
        scratch_shapes=[pltpu.VMEM((tb, d), F32)],
        compiler_params=_cparams(("arbitrary",)))(
        dx2, x, dqrot, dkrot, dva, qraw, kraw, *tabs, dqb, dkb, dvb, dga, dgb, w_in_t, g_mix, q_g, k_g)


def _identity(v):
    return v


def _to_bf16(v):
    return v.astype(BF16)


def _square_bf16(v):
    vf = v.astype(F32)
    return (vf * vf).astype(BF16)


def _weight_grad(name, a, b, ti, tj, tk, a_fn=_identity, b_fn=_identity):
    t, m = a.shape
    n = b.shape[1]
    n_k = t // tk

    def body(a_ref, b_ref, o_ref, acc_ref):
        k = pl.program_id(2)

        @pl.when(k == 0)
        def _():
            acc_ref[...] = jnp.zeros_like(acc_ref)

        acc_ref[...] += _dot_tn(a_fn(a_ref[...]), b_fn(b_ref[...]))

        @pl.when(k == n_k - 1)
        def _():
            o_ref[...] = acc_ref[...].astype(BF16)

    return pl.pallas_call(
        body, name=name, grid=(m // ti, n // tj, n_k),
        in_specs=[pl.BlockSpec((tk, ti), lambda i, j, k: (k, i)), pl.BlockSpec((tk, tj), lambda i, j, k: (k, j))],
        out_specs=pl.BlockSpec((ti, tj), lambda i, j, k: (i, j)),
        out_shape=jax.ShapeDtypeStruct((m, n), BF16),
        scratch_shapes=[pltpu.VMEM((ti, tj), F32)],
        compiler_params=_cparams(("arbitrary", "arbitrary", "arbitrary")))(a, b)


def _sum_slots(name, recv):
    _, n, k = recv.shape
    tc = min(k, 256)

    def body(r_ref, o_ref):
        acc = r_ref[0].astype(F32)
        for i in range(1, N_DEV):
            acc = acc + r_ref[i].astype(F32)
        o_ref[...] = acc

    return pl.pallas_call(
        body, name=name, grid=(k // tc,),
        in_specs=[pl.BlockSpec((N_DEV, n, tc), lambda j: (0, 0, j))],
        out_specs=pl.BlockSpec((n, tc), lambda j: (0, j)),
        out_shape=jax.ShapeDtypeStruct((n, k), F32),
        compiler_params=_cparams(("arbitrary",)))(recv)


def _adamw_math(w, g, m, v):
    m = ADAM_B1 * m + (1.0 - ADAM_B1) * g
    v = ADAM_B2 * v + (1.0 - ADAM_B2) * (g * g)
    m_hat = m / (1.0 - ADAM_B1 ** ADAM_STEP)
    v_hat = v / (1.0 - ADAM_B2 ** ADAM_STEP)
    delta = -ADAM_LR * (m_hat / (jnp.sqrt(v_hat) + ADAM_EPS) + ADAM_WD * w)
    return delta, m, v


def _adamw(name, w, g, m, v):
    r, c = w.shape
    tr = min(r, 256)

    def body(w_ref, g_ref, m_ref, v_ref, d_ref, mo_ref, vo_ref):
        d_ref[...], mo_ref[...], vo_ref[...] = _adamw_math(w_ref[...], g_ref[...], m_ref[...], v_ref[...])

    spec = pl.BlockSpec((tr, c), lambda i: (i, 0))
    return pl.pallas_call(
        body, name=name, grid=(r // tr,), in_specs=[spec] * 4, out_specs=[spec] * 3,
        out_shape=[jax.ShapeDtypeStruct((r, c), F32)] * 3,
        compiler_params=_cparams(("arbitrary",)))(w, g, m, v)


def _small_update(parts, w, m, v):
    def body(p_ref, w_ref, m_ref, v_ref, g_ref, d_ref, mo_ref, vo_ref):
        g = p_ref[0]
        for i in range(1, N_DEV):
            g = g + p_ref[i]
        g_ref[...] = g
        d_ref[...], mo_ref[...], vo_ref[...] = _adamw_math(w_ref[...], g, m_ref[...], v_ref[...])

    return pl.pallas_call(body, name="small_update", out_shape=[jax.ShapeDtypeStruct(w.shape, F32)] * 4)(
        parts, w, m, v)


def _to_residue_major(a):
    _, s, c = a.shape
    out = [a[0]]
    for g in (1, 2):
        dil = DILATIONS[g]
        out.append(a[g].reshape(s // dil, dil, c).transpose(1, 0, 2).reshape(s, c))
    return jnp.stack(out)


def _from_residue_major(a):
    _, s, c = a.shape
    out = [a[0]]
    for g in (1, 2):
        dil = DILATIONS[g]
        out.append(a[g].reshape(dil, s // dil, c).transpose(1, 0, 2).reshape(s, c))
    return jnp.stack(out)


def _pack_rows(vectors, n_rows):
    flat = jnp.concatenate([v.reshape(-1).astype(F32) for v in vectors])
    flat = jnp.pad(flat, (0, n_rows * LANES - flat.shape[0]))
    return flat.reshape(n_rows, LANES)


def _pick_tile(n, prefs):
    for t in prefs:
        if n % t == 0:
            return t
    return n


def kernel(x, p, norm_mix_g, w_in, b_gate, q_norm_g, k_norm_g, rel_bias, w_out_a, w_out_b, w_out, norm_mlp_g, w_ff1, w_ff2, norm_ple_g, w_ple_gate, w_ple, final_norm_g, loss_target, m_norm_mix_g, m_w_in, m_b_gate, m_q_norm_g, m_k_norm_g, m_rel_bias, m_w_out_a, m_w_out_b, m_w_out, m_norm_mlp_g, m_w_ff1, m_w_ff2, m_norm_ple_g, m_w_ple_gate, m_w_ple, m_final_norm_g, v_norm_mix_g, v_w_in, v_b_gate, v_q_norm_g, v_k_norm_g, v_rel_bias, v_w_out_a, v_w_out_b, v_w_out, v_norm_mlp_g, v_w_ff1, v_w_ff2, v_norm_ple_g, v_w_ple_gate, v_w_ple, v_final_norm_g):
    s, d = x.shape[1], x.shape[2]
    xs, ps, ts = x[0], p[0, 0], loss_target[0]
    tb = _pick_tile(s, (512, 256))
    tq = _pick_tile(s, (256,))
    tk = _pick_tile(s, (1024, 512))
    cb = _pick_tile(s, (512,))
    fin_g = final_norm_g.reshape(1, d)

    col_sharded = {"w_in": w_in[0], "w_out_b": w_out_b[0], "w_ff1": w_ff1[0], "w_ple": w_ple[0]}
    row_sharded = {"w_out_a": w_out_a[0], "w_out": w_out[0], "w_ff2": w_ff2[0], "w_ple_gate": w_ple_gate[0]}
    order = ["w_in", "w_out_a", "w_out_b", "w_out", "w_ff1", "w_ff2", "w_ple_gate", "w_ple"]
    shards = [(col_sharded[n].T if n in col_sharded else row_sharded[n]).astype(BF16) for n in order]
    w_in_t, w_oa, w_ob_t, w_o, w_ff1_t, w_ff2_f, w_pg, w_p_t = _all_gather(shards)

    tabs = _rope_tables(s)
    (h1, qraw, kraw, qrot, krot, va, qb, kb, vb, ga, gb) = _in_proj(
        xs, tabs, w_in_t, norm_mix_g, b_gate, q_norm_g, k_norm_g, tb)
    oa, lse_a = _attn_a_fwd(qrot, krot, va, tq, tk)
    oh_qk, oh_kq = _bucket_onehots()
    bias_qk = _band_bias(rel_bias, jnp.asarray(oh_qk), (BAND, 3 * BAND))
    bias_kq = _band_bias(rel_bias, jnp.asarray(oh_kq), (3 * BAND, BAND))
    qb_r, kb_r, vb_r = _to_residue_major(qb), _to_residue_major(kb), _to_residue_major(vb)
    og_r, lg_r = _band_fwd(qb_r, kb_r, vb_r, bias_qk, cb)
    og, lg = _from_residue_major(og_r), _from_residue_major(lg_r)
    x2, ob, lse_b, ya, yb, u = _mix_out(xs, oa, og, lg, ga, gb, w_oa, w_ob_t, w_o, tb)
    tc = _pick_tile(w_ff1_t.shape[0], (512,))
    x3, r_act, h2 = _mlp_fwd(x2, w_ff1_t, w_ff2_f, norm_mlp_g, tb, tc)

    dx3, h3, dpre, dpe, pb, loss_part, dg_fin, dg_ple = _ple_loss(
        x3, ps, ts, w_pg, w_p_t, norm_ple_g, fin_g, tb)
    dx2, df, dg_mlp = _mlp_bwd(dx3, x2, r_act, w_ff1_t, w_ff2_f, norm_mlp_g, tb, tc)
    doa, dob, dd, dga, dgb, dya, dyb, dbg = _mix_out_bwd(dx2, ya, yb, ga, gb, ob, w_oa, w_ob_t, w_o, tb)
    dqrot, dkrot, dva = _attn_a_bwd(qrot, krot, va, oa, doa, lse_a, tq, tk)
    dob_r = _to_residue_major(jnp.stack([dob] * 3))
    lse_r = _to_residue_major(jnp.stack([lse_b] * 3))
    dd_r = _to_residue_major(jnp.stack([dd] * 3))
    dqb_r, dsum = _band_bwd_q(qb_r, kb_r, vb_r, dob_r, lse_r, dd_r, bias_qk, cb)
    dkb_r, dvb_r = _band_bwd_kv(qb_r, kb_r, vb_r, dob_r, lse_r, dd_r, bias_kq, cb)
    dqb, dkb, dvb = _from_residue_major(dqb_r), _from_residue_major(dkb_r), _from_residue_major(dvb_r)
    grad_x, dz, dg_mix, dg_q, dg_k = _in_proj_bwd(
        dx2, xs, dqrot, dkrot, dva, qraw, kraw, tabs, dqb, dkb, dvb, dga, dgb, w_in_t, norm_mix_g,
        q_norm_g, k_norm_g, _pick_tile(s, (256,)))
    d_rel =_rel_bias_grad(dsum.reshape(3, N_HEADS_PER_DIL, BAND * 3 * BAND), jnp.asarray(oh_qk).astype(BF16))
    d_rel = d_rel.transpose(2, 0, 1).reshape(N_REL_BUCKETS, 3 * N_HEADS_PER_DIL)

    tkk = _pick_tile(s, (1024, 512))
    din = w_in_t.shape[0]
    ti_in = _pick_tile(din, (din // 2,)) if (din // 2) % LANES == 0 else din
    dff = w_ff1_t.shape[0]
    t1k = lambda n: _pick_tile(n, (1024, 512, 256))
    partials = [
        _weight_grad("grad_w_in", dz, h1, ti_in, t1k(d), tkk),
        _weight_grad("grad_w_out_a", oa, dya, t1k(QA_W), t1k(d), tkk),
        _weight_grad("grad_w_out_b", dyb, ob, t1k(d), GB_W, tkk),
        _weight_grad("grad_w_out", u, dx2, t1k(d), t1k(d), tkk, b_fn=_to_bf16),
        _weight_grad("grad_w_ff1", df, h2, t1k(dff), t1k(d), tkk),
        _weight_grad("grad_w_ff2", r_act, dx3, t1k(dff), t1k(d), tkk, a_fn=_square_bf16, b_fn=_to_bf16),
        _weight_grad("grad_w_ple_gate", h3, dpre, t1k(d), t1k(d), tkk),
        _weight_grad("grad_w_ple", dpe, pb, t1k(d), ps.shape[1], tkk),
    ]
    received = _scatter_blocks(partials)
    sums = [_sum_slots("sum_" + n, r) for n, r in zip(order, received)]

    given_w = dict(w_in=w_in, w_out_a=w_out_a, w_out_b=w_out_b, w_out=w_out, w_ff1=w_ff1, w_ff2=w_ff2,
                   w_ple_gate=w_ple_gate, w_ple=w_ple)
    given_m = dict(w_in=m_w_in, w_out_a=m_w_out_a, w_out_b=m_w_out_b, w_out=m_w_out, w_ff1=m_w_ff1, w_ff2=m_w_ff2,
                   w_ple_gate=m_w_ple_gate, w_ple=m_w_ple)
    given_v = dict(w_in=v_w_in, w_out_a=v_w_out_a, w_out_b=v_w_out_b, w_out=v_w_out, w_ff1=v_w_ff1, w_ff2=v_w_ff2,
                   w_ple_gate=v_w_ple_gate, w_ple=v_w_ple)
    big = {}
    for n, gsum in zip(order, sums):
        g = gsum.T if n in col_sharded else gsum
        delta, new_m, new_v = _adamw("adamw_" + n, given_w[n][0], g, given_m[n][0], given_v[n][0])
        big[n] = tuple(a[None] for a in (g, delta, new_m, new_v))

    small_names = ["norm_mix_g", "b_gate", "q_norm_g", "k_norm_g", "rel_bias", "norm_mlp_g", "norm_ple_g",
                   "final_norm_g"]
    small_w = [norm_mix_g, b_gate, q_norm_g, k_norm_g, rel_bias, norm_mlp_g, norm_ple_g, final_norm_g]
    small_m = [m_norm_mix_g, m_b_gate, m_q_norm_g, m_k_norm_g, m_rel_bias, m_norm_mlp_g, m_norm_ple_g,
               m_final_norm_g]
    small_v = [v_norm_mix_g, v_b_gate, v_q_norm_g, v_k_norm_g, v_rel_bias, v_norm_mlp_g, v_norm_ple_g,
               v_final_norm_g]
    small_g = [dg_mix, dbg, dg_q, dg_k, d_rel, dg_mlp, dg_ple, dg_fin]
    sizes = [int(np.prod(w.shape)) for w in small_w]
    n_rows = -(-(sum(-(-sz // LANES) for sz in sizes) + 1) // 8) * 8
    pad = lambda v: jnp.pad(v.reshape(-1).astype(F32), (0, -v.size % LANES))
    pack = lambda vs, last: _pack_rows([pad(v) for v in vs] + [last], n_rows)
    zero_row = jnp.zeros((LANES,), F32)
    parts = _small_all_gather(pack(small_g, loss_part.reshape(-1) * (jnp.arange(LANES) == 0)))
    g_all, d_all, m_all, v_all = _small_update(parts, pack(small_w, zero_row), pack(small_m, zero_row),
                                               pack(small_v, zero_row))
    small = {}
    row = 0
    for n, w, sz in zip(small_names, small_w, sizes):
        nr = -(-sz // LANES)
        small[n] = tuple(a[row:row + nr].reshape(-1)[:sz].reshape(w.shape) for a in (g_all, d_all, m_all, v_all))
        row += nr
    loss = g_all[row, 0]

    names = ["norm_mix_g", "w_in", "b_gate", "q_norm_g", "k_norm_g", "rel_bias", "w_out_a", "w_out_b", "w_out",
             "norm_mlp_g", "w_ff1", "w_ff2", "norm_ple_g", "w_ple_gate", "w_ple", "final_norm_g"]
    res = {n: (big[n] if n in big else small[n]) for n in names}
    return (loss, grad_x[None], *[res[n][0] for n in names], *[res[n][1] for n in names],
            *[res[n][2] for n in names], *[res[n][3] for n in names])
```

```python
import functools
import math

import numpy as np
import jax
import jax.numpy as jnp
from jax import lax
from jax.experimental import pallas as pl
from jax.experimental.pallas import tpu as pltpu

F32 = jnp.float32
BF16 = jnp.bfloat16
MESH = pl.DeviceIdType.MESH

NORM_EPS = 1e-6
NEG_INF = -1e30
LOG2_E = math.log2(math.e)
LN_2 = math.log(2.0)
GRID_W = 64
ROPE_THETA = 10000.0
HEAD_DIM_A = 128
N_Q_HEADS_A = 8
N_KV_HEADS_A = 2
Q_PER_KV = N_Q_HEADS_A // N_KV_HEADS_A
HEAD_DIM_B = 64
N_HEADS_PER_DIL = 4
DILATIONS = (1, 4, 16)
BAND = 64
N_REL_BUCKETS = 32
REL_MAX_DIST = 1024
QA_W = N_Q_HEADS_A * HEAD_DIM_A
KA_W = N_KV_HEADS_A * HEAD_DIM_A
GB_W = N_HEADS_PER_DIL * HEAD_DIM_B
QB_W = GB_W * len(DILATIONS)
OFF_QA, OFF_KA, OFF_VA = 0, QA_W, QA_W + KA_W
OFF_QB = QA_W + 2 * KA_W
OFF_KB = OFF_QB + QB_W
OFF_VB = OFF_KB + QB_W
OFF_GA = OFF_VB + QB_W
N_DEV = 8
LANES = 128
VMEM_LIMIT = 56 * 2 ** 20

ADAM_LR, ADAM_B1, ADAM_B2, ADAM_EPS, ADAM_WD, ADAM_STEP = 0.001, 0.9, 0.999, 1e-08, 0.01, 10


def _cparams(sem):
    return pltpu.CompilerParams(dimension_semantics=sem, vmem_limit_bytes=VMEM_LIMIT)


def _resident(shape):
    nd = len(shape)
    return pl.BlockSpec(shape, lambda *_: (0,) * nd, pipeline_mode=pl.Buffered(1))


def _acc_spec(shape):
    nd = len(shape)
    return pl.BlockSpec(shape, lambda *_: (0,) * nd)


def _rows(tb, c):
    return pl.BlockSpec((tb, c), lambda i: (i, 0))


def _dil_shapes(s, dtype):
    return [jax.ShapeDtypeStruct((dil, s // dil, GB_W), dtype) for dil in DILATIONS]


def _dil_specs(tb):
    return [pl.BlockSpec((dil, tb // dil, GB_W), lambda i: (0, i, 0)) for dil in DILATIONS]


def _to_residues(val, out_ref, scr_ref, dil, dtype):
    if dil == 1:
        out_ref[0] = val.astype(dtype)
        return
    n = val.shape[0] // dil
    scr_ref[0] = val[:, :LANES]
    scr_ref[1] = val[:, LANES:]
    for r in range(dil):
        out_ref[r] = jnp.concatenate([scr_ref[0, pl.ds(r, n, stride=dil), :],
                                      scr_ref[1, pl.ds(r, n, stride=dil), :]], axis=1).astype(dtype)


def _from_residues(in_ref, scr_ref, dil):
    if dil == 1:
        return in_ref[0]
    n = in_ref.shape[1]
    for r in range(dil):
        v = in_ref[r]
        scr_ref[0, pl.ds(r, n, stride=dil), :] = v[:, :LANES]
        scr_ref[1, pl.ds(r, n, stride=dil), :] = v[:, LANES:]
    return jnp.concatenate([scr_ref[0], scr_ref[1]], axis=1)


def _dot_nt(a, b):
    return lax.dot_general(a, b, (((1,), (1,)), ((), ())), preferred_element_type=F32)


def _dot_nn(a, b):
    return lax.dot_general(a, b, (((1,), (0,)), ((), ())), preferred_element_type=F32)


def _dot_tn(a, b):
    return lax.dot_general(a, b, (((0,), (0,)), ((), ())), preferred_element_type=F32)


def _rstd(x):
    return lax.rsqrt(jnp.mean(x * x, axis=-1, keepdims=True) + NORM_EPS)


def _rms_bwd(dy, n, r, g):
    dn = dy * g
    return r * (dn - n * jnp.mean(dn * n, axis=-1, keepdims=True))


def _colsum(v):
    return jnp.sum(v, axis=0, keepdims=True)


def _sigmoid(v):
    return 1.0 / (1.0 + jnp.exp(-v))


def _rope_fwd(n, c, s1, s2):
    return n * c + pltpu.roll(n, 32, 1) * s1 + pltpu.roll(n, 96, 1) * s2


def _rope_bwd(d, c, s1, s2):
    return d * c + pltpu.roll(d * s1, 96, 1) + pltpu.roll(d * s2, 32, 1)


def _rope_tables(s):
    half = HEAD_DIM_A // 2
    inv = jnp.power(ROPE_THETA, -jnp.arange(0, half, 2, dtype=F32) / half)
    t = jnp.arange(s, dtype=jnp.int32)
    ang_r = (t // GRID_W).astype(F32)[:, None] * inv[None, :]
    ang_c = (t % GRID_W).astype(F32)[:, None] * inv[None, :]
    cr, sr, cc, sc = jnp.cos(ang_r), jnp.sin(ang_r), jnp.cos(ang_c), jnp.sin(ang_c)
    z = jnp.zeros_like(sr)
    cos = jnp.concatenate([cr, cr, cc, cc], axis=1)
    s1 = jnp.concatenate([z, sr, z, sc], axis=1)
    s2 = jnp.concatenate([-sr, z, -sc, z], axis=1)
    return cos, s1, s2


def _my_place():
    return lax.axis_index("x"), lax.axis_index("y"), lax.axis_index("c")


def _all_gather(shards):
    nw = len(shards)

    def body(*refs):
        ins, outs = refs[:nw], refs[nw:2 * nw]
        send_sems, recv_sems, local_sems = refs[2 * nw:]
        x, y, c = _my_place()
        me, sibling = (x, y, c), (x, y, 1 - c)
        chips = [(1 - x, y), (x, 1 - y), (1 - x, 1 - y)]

        def rows(w, px, py, pc):
            n = ins[w].shape[0]
            return outs[w].at[pl.ds(pl.multiple_of((4 * px + 2 * py + pc) * n, 16), n), :]

        def copy(w, k, block, to, src=None):
            return pltpu.make_async_remote_copy(
                src_ref=rows(w, *block) if src is None else src, dst_ref=rows(w, *block),
                send_sem=send_sems.at[w, k], recv_sem=recv_sems.at[w, k], device_id=to, device_id_type=MESH)

        mine = [pltpu.make_async_copy(ins[w], rows(w, *me), local_sems.at[w]) for w in range(nw)]
        for cp in mine:
            cp.start()
        first = []
        for w in range(nw):
            first.append(copy(w, 0, me, sibling, src=ins[w]))
            first += [copy(w, 1 + j, me, (*chip, c), src=ins[w]) for j, chip in enumerate(chips)]
        for cp in first:
            cp.start()
        passed = []
        for j, chip in enumerate(chips):
            for w in range(nw):
                copy(w, 1 + j, (*chip, c), me).wait_recv()
                fwd = copy(w, 4 + j, (*chip, c), sibling)
                fwd.start()
                passed.append(fwd)
        for w in range(nw):
            copy(w, 0, sibling, me).wait_recv()
        for j, chip in enumerate(chips):
            for w in range(nw):
                copy(w, 4 + j, (*chip, 1 - c), me).wait_recv()
        for cp in first + passed:
            cp.wait_send()
        for cp in mine:
            cp.wait()

    any_spec = pl.BlockSpec(memory_space=pl.ANY)
    return pl.pallas_call(
        body, name="weights_all_gather",
        out_shape=[jax.ShapeDtypeStruct((N_DEV * s.shape[0], s.shape[1]), s.dtype) for s in shards],
        in_specs=[any_spec] * nw, out_specs=[any_spec] * nw,
        scratch_shapes=[pltpu.SemaphoreType.DMA((nw, 7)), pltpu.SemaphoreType.DMA((nw, 7)),
                        pltpu.SemaphoreType.DMA((nw,))],
    )(*shards)


_FLIPS = [(fx, fy, fc) for fx in (0, 1) for fy in (0, 1) for fc in (0, 1)][1:]


def _scatter_blocks(partials):
    nw = len(partials)

    def body(*refs):
        ins, outs = refs[:nw], refs[nw:2 * nw]
        send_sems, recv_sems, local_sems = refs[2 * nw:]
        x, y, c = _my_place()
        my_idx = 4 * x + 2 * y + c

        def block(w, idx):
            n = outs[w].shape[1]
            return ins[w].at[pl.ds(pl.multiple_of(idx * n, 16), n), :]

        def peer(k):
            fx, fy, fc = _FLIPS[k]
            return (1 - x if fx else x, 1 - y if fy else y, 1 - c if fc else c)

        def copy(w, k):
            to = peer(k)
            to_idx = 4 * to[0] + 2 * to[1] + to[2]
            return pltpu.make_async_remote_copy(
                src_ref=block(w, to_idx), dst_ref=outs[w].at[my_idx],
                send_sem=send_sems.at[w, k], recv_sem=recv_sems.at[w, k], device_id=to, device_id_type=MESH)

        def arrival(w, k):
            frm = peer(k)
            frm_idx = 4 * frm[0] + 2 * frm[1] + frm[2]
            return pltpu.make_async_remote_copy(
                src_ref=block(w, my_idx), dst_ref=outs[w].at[frm_idx],
                send_sem=send_sems.at[w, k], recv_sem=recv_sems.at[w, k], device_id=(x, y, c), device_id_type=MESH)

        mine = [pltpu.make_async_copy(block(w, my_idx), outs[w].at[my_idx], local_sems.at[w]) for w in range(nw)]
        for cp in mine:
            cp.start()
        sends = [copy(w, k) for k in range(7) for w in range(nw)]
        for cp in sends:
            cp.start()
        for k in range(7):
            for w in range(nw):
                arrival(w, k).wait_recv()
        for cp in sends:
            cp.wait_send()
        for cp in mine:
            cp.wait()

    any_spec = pl.BlockSpec(memory_space=pl.ANY)
    return pl.pallas_call(
        body, name="grads_scatter",
        out_shape=[jax.ShapeDtypeStruct((N_DEV, p.shape[0] // N_DEV, p.shape[1]), p.dtype) for p in partials],
        in_specs=[any_spec] * nw, out_specs=[any_spec] * nw,
        scratch_shapes=[pltpu.SemaphoreType.DMA((nw, 7)), pltpu.SemaphoreType.DMA((nw, 7)),
                        pltpu.SemaphoreType.DMA((nw,))],
    )(*partials)


def _small_all_gather(v):
    def body(v_ref, out_ref, send_sems, recv_sems):
        x, y, c = _my_place()
        my_idx = 4 * x + 2 * y + c
        out_ref[my_idx] = v_ref[...]
        sends = []
        for k, (fx, fy, fc) in enumerate(_FLIPS):
            to = (1 - x if fx else x, 1 - y if fy else y, 1 - c if fc else c)
            sends.append(pltpu.make_async_remote_copy(
                src_ref=v_ref, dst_ref=out_ref.at[my_idx], send_sem=send_sems.at[k], recv_sem=recv_sems.at[k],
                device_id=to, device_id_type=MESH))
        for cp in sends:
            cp.start()
        for k, (fx, fy, fc) in enumerate(_FLIPS):
            frm_idx = 4 * (1 - x if fx else x) + 2 * (1 - y if fy else y) + (1 - c if fc else c)
            pltpu.make_async_remote_copy(
                src_ref=v_ref, dst_ref=out_ref.at[frm_idx], send_sem=send_sems.at[k], recv_sem=recv_sems.at[k],
                device_id=(x, y, c), device_id_type=MESH).wait_recv()
        for cp in sends:
            cp.wait_send()

    vm = pl.BlockSpec(memory_space=pltpu.VMEM)
    return pl.pallas_call(
        body, name="small_all_gather", out_shape=jax.ShapeDtypeStruct((N_DEV,) + v.shape, v.dtype),
        in_specs=[vm], out_specs=vm,
        scratch_shapes=[pltpu.SemaphoreType.DMA((7,)), pltpu.SemaphoreType.DMA((7,))],
    )(v)


def _in_proj(x, tabs, w_in_t, g_mix, b_gate, q_g, k_g, tb):
    s, d = x.shape
    n_gate_chunks = d // 256
    q_scale = HEAD_DIM_A ** -0.5 * LOG2_E
    b_scale = HEAD_DIM_B ** -0.5

    def body(x_ref, c_ref, s1_ref, s2_ref, w_ref, gmix_ref, bg_ref, qg_ref, kg_ref,
             h1_ref, qraw_ref, kraw_ref, qrot_ref, krot_ref, va_ref, *rest):
        qb_refs, kb_refs, vb_refs = rest[0:3], rest[3:6], rest[6:9]
        ga_ref, gb_ref, scr_ref = rest[9:]
        xv = x_ref[...]
        hb = (xv * _rstd(xv) * gmix_ref[...]).astype(BF16)
        h1_ref[...] = hb
        cos, s1, s2 = c_ref[...], s1_ref[...], s2_ref[...]

        def proj(lo, width):
            return _dot_nt(hb, w_ref[lo:lo + width, :])

        def norm_rope(z, g):
            return _rope_fwd(z * _rstd(z) * g, cos, s1, s2)

        for j in range(QA_W // 256):
            z = proj(OFF_QA + 256 * j, 256)
            qraw_ref[:, 256 * j:256 * j + 256] = z
            for hh in range(2):
                lo = 256 * j + 128 * hh
                qrot_ref[:, lo:lo + 128] = (norm_rope(z[:, 128 * hh:128 * hh + 128], qg_ref[...]) * q_scale).astype(BF16)
        z = proj(OFF_KA, 256)
        kraw_ref[...] = z
        for hh in range(2):
            krot_ref[:, 128 * hh:128 * hh + 128] = norm_rope(z[:, 128 * hh:128 * hh + 128], kg_ref[...]).astype(BF16)
        va_ref[...] = proj(OFF_VA, 256).astype(BF16)
        for g, dil in enumerate(DILATIONS):
            _to_residues(proj(OFF_QB + GB_W * g, GB_W) * b_scale, qb_refs[g], scr_ref, dil, BF16)
            _to_residues(proj(OFF_KB + GB_W * g, GB_W), kb_refs[g], scr_ref, dil, BF16)
            _to_residues(proj(OFF_VB + GB_W * g, GB_W), vb_refs[g], scr_ref, dil, BF16)
        for j in range(n_gate_chunks):
            sl = slice(256 * j, 256 * j + 256)
            ga_ref[:, sl] = _sigmoid(proj(OFF_GA + 256 * j, 256) + bg_ref[:, sl])
            gb_ref[:, sl] = _sigmoid(proj(OFF_GA + d + 256 * j, 256) + bg_ref[:, d + 256 * j:d + 256 * j + 256])

    sd = jax.ShapeDtypeStruct
    outs = [sd((s, d), BF16), sd((s, QA_W), F32), sd((s, KA_W), F32), sd((s, QA_W), BF16), sd((s, KA_W), BF16),
            sd((s, KA_W), BF16)] + _dil_shapes(s, BF16) * 3 + [sd((s, d), F32), sd((s, d), F32)]
    out_specs = [_rows(tb, d), _rows(tb, QA_W), _rows(tb, KA_W), _rows(tb, QA_W), _rows(tb, KA_W), _rows(tb, KA_W)
                 ] + _dil_specs(tb) * 3 + [_rows(tb, d), _rows(tb, d)]
    in_specs = [_rows(tb, d), _rows(tb, LANES), _rows(tb, LANES), _rows(tb, LANES), _resident(w_in_t.shape),
                _resident(g_mix.shape), _resident(b_gate.shape), _resident(q_g.shape), _resident(k_g.shape)]
    res = list(pl.pallas_call(body, name="in_proj", grid=(s // tb,), in_specs=in_specs, out_specs=out_specs,
                              out_shape=outs, scratch_shapes=[pltpu.VMEM((2, tb, LANES), F32)],
                              compiler_params=_cparams(("arbitrary",)))(
        x, *tabs, w_in_t, g_mix, b_gate, q_g, k_g))
    return res[:6] + [res[6:9], res[9:12], res[12:15]] + res[15:]


def _attn_a_fwd(qrot, krot, va, tq, tk):
    s = qrot.shape[0]
    n_kv = s // tk
    gw = Q_PER_KV * HEAD_DIM_A

    def body(q_ref, k_ref, v_ref, o_ref, lse_ref):
        q4 = jnp.concatenate([q_ref[:, 128 * h:128 * h + 128] for h in range(Q_PER_KV)], axis=0)

        def step(j, carry):
            m, l, acc = carry
            sl = pl.ds(pl.multiple_of(j * tk, tk), tk)
            kj, vj = k_ref[sl, :], v_ref[sl, :]
            sc = _dot_nt(q4, kj)
            m_new = jnp.maximum(m, jnp.max(sc, axis=-1, keepdims=True))
            p = jnp.exp2(sc - m_new)
            alpha = jnp.exp2(m - m_new)
            l = alpha * l + jnp.sum(p, axis=-1, keepdims=True)
            acc = alpha * acc + _dot_nn(p.astype(BF16), vj)
            return m_new, l, acc

        rows = Q_PER_KV * tq
        m, l, acc = lax.fori_loop(0, n_kv, step, (jnp.full((rows, 1), NEG_INF, F32), jnp.zeros((rows, 1), F32),
                                                  jnp.zeros((rows, HEAD_DIM_A), F32)))
        o = acc / l
        lse = m + jnp.log2(l)
        for h in range(Q_PER_KV):
            o_ref[:, 128 * h:128 * h + 128] = o[h * tq:(h + 1) * tq].astype(BF16)
            lse_ref[0, :, h:h + 1] = lse[h * tq:(h + 1) * tq]

    return pl.pallas_call(
        body, name="attn_a_fwd", grid=(N_KV_HEADS_A, s // tq),
        in_specs=[pl.BlockSpec((tq, gw), lambda g, i: (i, g)),
                  pl.BlockSpec((s, HEAD_DIM_A), lambda g, i: (0, g)),
                  pl.BlockSpec((s, HEAD_DIM_A), lambda g, i: (0, g))],
        out_specs=[pl.BlockSpec((tq, gw), lambda g, i: (i, g)),
                   pl.BlockSpec((1, tq, Q_PER_KV), lambda g, i: (g, i, 0))],
        out_shape=[jax.ShapeDtypeStruct((s, QA_W), BF16), jax.ShapeDtypeStruct((N_KV_HEADS_A, s, Q_PER_KV), F32)],
        compiler_params=_cparams(("arbitrary", "arbitrary")))(qrot, krot, va)


def _attn_a_bwd(qrot, krot, va, oa, doa, lse, tq, tk):
    s = qrot.shape[0]
    n_kv = s // tk
    gw = Q_PER_KV * HEAD_DIM_A

    def body(q_ref, do_ref, o_ref, lse_ref, k_ref, v_ref, dq_ref, dk_ref, dv_ref):
        @pl.when(pl.program_id(1) == 0)
        def _():
            dk_ref[...] = jnp.zeros_like(dk_ref)
            dv_ref[...] = jnp.zeros_like(dv_ref)

        def stack(ref):
            return jnp.concatenate([ref[:, 128 * h:128 * h + 128] for h in range(Q_PER_KV)], axis=0)

        q4, do4, o4 = stack(q_ref), stack(do_ref), stack(o_ref)
        delta = jnp.sum(do4.astype(F32) * o4.astype(F32), axis=-1, keepdims=True)
        lse4 = jnp.concatenate([lse_ref[0, :, h:h + 1] for h in range(Q_PER_KV)], axis=0)

        def step(j, dq):
            sl = pl.ds(pl.multiple_of(j * tk, tk), tk)
            kj, vj = k_ref[sl, :], v_ref[sl, :]
            p = jnp.exp2(_dot_nt(q4, kj) - lse4)
            ds = (p * (_dot_nt(do4, vj) - delta)).astype(BF16)
            dk_ref[sl, :] += _dot_tn(ds, q4)
            dv_ref[sl, :] += _dot_tn(p.astype(BF16), do4)
            return dq + _dot_nn(ds, kj)

        dq = lax.fori_loop(0, n_kv, step, jnp.zeros((Q_PER_KV * tq, HEAD_DIM_A), F32))
        for h in range(Q_PER_KV):
            dq_ref[:, 128 * h:128 * h + 128] = dq[h * tq:(h + 1) * tq]

    qspec = pl.BlockSpec((tq, gw), lambda g, i: (i, g))
    kspec = pl.BlockSpec((s, HEAD_DIM_A), lambda g, i: (0, g))
    return pl.pallas_call(
        body, name="attn_a_bwd", grid=(N_KV_HEADS_A, s // tq),
        in_specs=[qspec, qspec, qspec, pl.BlockSpec((1, tq, Q_PER_KV), lambda g, i: (g, i, 0)), kspec, kspec],
        out_specs=[qspec, kspec, kspec],
        out_shape=[jax.ShapeDtypeStruct((s, QA_W), F32), jax.ShapeDtypeStruct((s, KA_W), F32),
                   jax.ShapeDtypeStruct((s, KA_W), F32)],
        compiler_params=_cparams(("arbitrary", "arbitrary")))(qrot, doa, oa, lse, krot, va)


BAND_QB = 256
BAND_WIN = BAND_QB + 2 * BAND


def _band_specs(s, cb):
    per = cb // BAND
    last = s // BAND - 1
    cur = pl.BlockSpec((cb, GB_W), lambda i: (i, 0))
    prev = pl.BlockSpec((BAND, GB_W), lambda i: (jnp.maximum(i * per - 1, 0), 0))
    nxt = pl.BlockSpec((BAND, GB_W), lambda i: (jnp.minimum(i * per + per, last), 0))
    return cur, prev, nxt


def _window(prev_ref, cur_ref, next_ref):
    return jnp.concatenate([prev_ref[...], cur_ref[...], next_ref[...]], axis=0)


def _band_mask(base, seg_shift, window_rows):
    shape = (BAND_WIN, BAND_QB) if window_rows else (BAND_QB, BAND_WIN)
    a = lax.broadcasted_iota(jnp.int32, shape, 0)
    b = lax.broadcasted_iota(jnp.int32, shape, 1)
    rq, rk = (base - BAND + a, base + b) if window_rows else (base + a, base - BAND + b)
    same_segment = lax.shift_right_arithmetic(rq, jnp.int32(seg_shift)) == lax.shift_right_arithmetic(rk, jnp.int32(seg_shift))
    return (jnp.abs(rk - rq) <= BAND) & same_segment


def _build_bias(bmap_ref, tab_ref, bias_ref):
    bm = bmap_ref[...]
    acc = [jnp.zeros(bm.shape, F32) for _ in range(N_HEADS_PER_DIL)]
    for b in range(N_REL_BUCKETS):
        hit = bm == b
        for h in range(N_HEADS_PER_DIL):
            acc[h] = jnp.where(hit, tab_ref[b, h], acc[h])
    for h in range(N_HEADS_PER_DIL):
        bias_ref[h] = acc[h]


def _head_lane_masks():
    lane = lax.broadcasted_iota(jnp.int32, (1, LANES), 1)
    return [lane < HEAD_DIM_B, lane >= HEAD_DIM_B]


def _seg_shift(s, dil):
    seg = s // dil
    assert seg & (seg - 1) == 0, "segment length must be a power of two"
    return seg.bit_length() - 1


def _band_fwd(dil, qb, kb, vb, bmap, tab, cb):
    s = qb.shape[0]
    shift = _seg_shift(s, dil)

    def body(q_ref, kp_ref, kc_ref, kn_ref, vp_ref, vc_ref, vn_ref, bmap_ref, tab_ref, o_ref, lse_ref, bias_ref):
        @pl.when(pl.program_id(0) == 0)
        def _():
            _build_bias(bmap_ref, tab_ref, bias_ref)

        kw, vw = _window(kp_ref, kc_ref, kn_ref), _window(vp_ref, vc_ref, vn_ref)
        hm = _head_lane_masks()
        for jj in range(cb // BAND_QB):
            r0 = BAND_QB * jj
            mask = _band_mask(pl.program_id(0) * cb + r0, shift, False)
            for hp in range(2):
                ls = slice(LANES * hp, LANES * hp + LANES)
                qh = q_ref[r0:r0 + BAND_QB, ls]
                k3, v3 = kw[r0:r0 + BAND_WIN, ls], vw[r0:r0 + BAND_WIN, ls]
                o_half = jnp.zeros((BAND_QB, LANES), F32)
                lse_half = jnp.zeros((BAND_QB, LANES), F32)
                for hh in range(2):
                    sc = _dot_nt(jnp.where(hm[hh], qh, jnp.zeros_like(qh)), k3) + bias_ref[2 * hp + hh]
                    sc = jnp.where(mask, sc, NEG_INF)
                    m = jnp.max(sc, axis=-1, keepdims=True)
                    e = jnp.exp(sc - m)
                    l = jnp.sum(e, axis=-1, keepdims=True)
                    p = (e * (1.0 / l)).astype(BF16)
                    o_half = o_half + _dot_nn(p, jnp.where(hm[hh], v3, jnp.zeros_like(v3)))
                    lse_half = jnp.where(hm[hh], m + jnp.log(l), lse_half)
                o_ref[r0:r0 + BAND_QB, ls] = o_half
                lse_ref[r0:r0 + BAND_QB, ls] = lse_half

    cur, prev, nxt = _band_specs(s, cb)
    return pl.pallas_call(
        body, name=f"band_fwd_d{dil}", grid=(s // cb,),
        in_specs=[cur, prev, cur, nxt, prev, cur, nxt, _resident(bmap.shape), pl.BlockSpec(memory_space=pltpu.SMEM)],
        out_specs=[cur, cur],
        out_shape=[jax.ShapeDtypeStruct(qb.shape, F32), jax.ShapeDtypeStruct(qb.shape, F32)],
        scratch_shapes=[pltpu.VMEM((N_HEADS_PER_DIL, BAND_QB, BAND_WIN), F32)],
        compiler_params=_cparams(("arbitrary",)))(qb, kb, kb, kb, vb, vb, vb, bmap, tab)


def _band_bwd_q(dil, qb, kb, vb, dob, lse, dd, bmap, tab, cb):
    s = qb.shape[0]
    shift = _seg_shift(s, dil)
    n_steps = s // cb

    def body(q_ref, do_ref, lse_ref, dd_ref, kp_ref, kc_ref, kn_ref, vp_ref, vc_ref, vn_ref, bmap_ref, tab_ref,
             dq_ref, dtab_ref, bias_ref, dsum_ref):
        @pl.when(pl.program_id(0) == 0)
        def _():
            _build_bias(bmap_ref, tab_ref, bias_ref)
            dsum_ref[...] = jnp.zeros_like(dsum_ref)

        kw, vw = _window(kp_ref, kc_ref, kn_ref), _window(vp_ref, vc_ref, vn_ref)
        hm = _head_lane_masks()
        for jj in range(cb // BAND_QB):
            r0 = BAND_QB * jj
            mask = _band_mask(pl.program_id(0) * cb + r0, shift, False)
            for hp in range(2):
                ls = slice(LANES * hp, LANES * hp + LANES)
                qh, doh = q_ref[r0:r0 + BAND_QB, ls], do_ref[r0:r0 + BAND_QB, ls]
                k3, v3 = kw[r0:r0 + BAND_WIN, ls], vw[r0:r0 + BAND_WIN, ls]
                dq_half = jnp.zeros((BAND_QB, LANES), F32)
                for hh in range(2):
                    h = 2 * hp + hh
                    col = LANES * hp + HEAD_DIM_B * hh
                    sc = _dot_nt(jnp.where(hm[hh], qh, jnp.zeros_like(qh)), k3) + bias_ref[h]
                    sc = jnp.where(mask, sc, NEG_INF)
                    p = jnp.exp(sc - lse_ref[r0:r0 + BAND_QB, col:col + 1])
                    dp = _dot_nt(jnp.where(hm[hh], doh, jnp.zeros_like(doh)), v3)
                    ds = p * (dp - dd_ref[r0:r0 + BAND_QB, col:col + 1])
                    dsum_ref[h] += ds
                    dq_half = dq_half + _dot_nn(ds.astype(BF16), jnp.where(hm[hh], k3, jnp.zeros_like(k3)))
                dq_ref[r0:r0 + BAND_QB, ls] = dq_half

        @pl.when(pl.program_id(0) == n_steps - 1)
        def _():
            bm = bmap_ref[...]
            lane = lax.broadcasted_iota(jnp.int32, (1, LANES), 1)
            for b in range(N_REL_BUCKETS):
                hit = bm == b
                row = jnp.zeros((1, LANES), F32)
                for h in range(N_HEADS_PER_DIL):
                    row = jnp.where(lane == h, jnp.sum(jnp.where(hit, dsum_ref[h], 0.0)), row)
                dtab_ref[b:b + 1, :] = row

    cur, prev, nxt = _band_specs(s, cb)
    return pl.pallas_call(
        body, name=f"band_bwd_q_d{dil}", grid=(n_steps,),
        in_specs=[cur, cur, cur, cur, prev, cur, nxt, prev, cur, nxt, _resident(bmap.shape),
                  pl.BlockSpec(memory_space=pltpu.SMEM)],
        out_specs=[cur, _acc_spec((N_REL_BUCKETS, LANES))],
        out_shape=[jax.ShapeDtypeStruct(qb.shape, F32), jax.ShapeDtypeStruct((N_REL_BUCKETS, LANES), F32)],
        scratch_shapes=[pltpu.VMEM((N_HEADS_PER_DIL, BAND_QB, BAND_WIN), F32),
                        pltpu.VMEM((N_HEADS_PER_DIL, BAND_QB, BAND_WIN), F32)],
        compiler_params=_cparams(("arbitrary",)))(qb, dob, lse, dd, kb, kb, kb, vb, vb, vb, bmap, tab)


def _band_bwd_kv(dil, qb, kb, vb, dob, lse, dd, bmap_t, tab, cb):
    s = qb.shape[0]
    shift = _seg_shift(s, dil)

    def body(k_ref, v_ref, qp_ref, qc_ref, qn_ref, dp_ref, dc_ref, dn_ref, lp_ref, lc_ref, ln_ref,
             ep_ref, ec_ref, en_ref, bmap_ref, tab_ref, dk_ref, dv_ref, bias_ref):
        @pl.when(pl.program_id(0) == 0)
        def _():
            _build_bias(bmap_ref, tab_ref, bias_ref)

        qw, dow = _window(qp_ref, qc_ref, qn_ref), _window(dp_ref, dc_ref, dn_ref)
        lw, ew = _window(lp_ref, lc_ref, ln_ref), _window(ep_ref, ec_ref, en_ref)
        hm = _head_lane_masks()
        for jj in range(cb // BAND_QB):
            r0 = BAND_QB * jj
            mask = _band_mask(pl.program_id(0) * cb + r0, shift, True)
            for hp in range(2):
                ls = slice(LANES * hp, LANES * hp + LANES)
                kh, vh = k_ref[r0:r0 + BAND_QB, ls], v_ref[r0:r0 + BAND_QB, ls]
                q3, do3 = qw[r0:r0 + BAND_WIN, ls], dow[r0:r0 + BAND_WIN, ls]
                dk_half = jnp.zeros((BAND_QB, LANES), F32)
                dv_half = jnp.zeros((BAND_QB, LANES), F32)
                for hh in range(2):
                    col = LANES * hp + HEAD_DIM_B * hh
                    q3m = jnp.where(hm[hh], q3, jnp.zeros_like(q3))
                    do3m = jnp.where(hm[hh], do3, jnp.zeros_like(do3))
                    sc = _dot_nt(q3m, kh) + bias_ref[2 * hp + hh]
                    sc = jnp.where(mask, sc, NEG_INF)
                    p = jnp.exp(sc - lw[r0:r0 + BAND_WIN, col:col + 1])
                    ds = p * (_dot_nt(do3m, vh) - ew[r0:r0 + BAND_WIN, col:col + 1])
                    dk_half = dk_half + _dot_tn(ds.astype(BF16), q3m)
                    dv_half = dv_half + _dot_tn(p.astype(BF16), do3m)
                dk_ref[r0:r0 + BAND_QB, ls] = dk_half
                dv_ref[r0:r0 + BAND_QB, ls] = dv_half

    cur, prev, nxt = _band_specs(s, cb)
    win = [prev, cur, nxt]
    return pl.pallas_call(
        body, name=f"band_bwd_kv_d{dil}", grid=(s // cb,),
        in_specs=[cur, cur] + win * 4 + [_resident(bmap_t.shape), pl.BlockSpec(memory_space=pltpu.SMEM)],
        out_specs=[cur, cur],
        out_shape=[jax.ShapeDtypeStruct(qb.shape, F32), jax.ShapeDtypeStruct(qb.shape, F32)],
        scratch_shapes=[pltpu.VMEM((N_HEADS_PER_DIL, BAND_WIN, BAND_QB), F32)],
        compiler_params=_cparams(("arbitrary",)))(
        kb, vb, qb, qb, qb, dob, dob, dob, lse, lse, lse, dd, dd, dd, bmap_t, tab)


def _t5_bucket(rel):
    nb = N_REL_BUCKETS // 2
    ret = (rel > 0).astype(np.int32) * nb
    n = np.abs(rel)
    max_exact = nb // 2
    large = max_exact + (np.log(np.maximum(n, 1) / max_exact) / math.log(REL_MAX_DIST / max_exact)
                         * (nb - max_exact)).astype(np.int32)
    large = np.minimum(large, nb - 1)
    return ret + np.where(n < max_exact, n, large).astype(np.int32)


def _bucket_maps(dil):
    off_qk = np.arange(BAND_WIN)[None, :] - BAND - np.arange(BAND_QB)[:, None]
    off_kq = np.arange(BAND_QB)[None, :] + BAND - np.arange(BAND_WIN)[:, None]
    return [np.where(np.abs(off) <= BAND, _t5_bucket(off * dil), -1).astype(np.int32) for off in (off_qk, off_kq)]


def _seg_sum(v):
    lane = lax.broadcasted_iota(jnp.int32, (1, v.shape[1]), 1)
    out = jnp.zeros_like(v)
    for h in range(v.shape[1] // HEAD_DIM_B):
        m = (lane >= HEAD_DIM_B * h) & (lane < HEAD_DIM_B * (h + 1))
        out = jnp.where(m, jnp.sum(jnp.where(m, v, 0.0), axis=-1, keepdims=True), out)
    return out


def _mix_out(x, oa, og, lg, ga, gb, w_oa, w_ob_t, w_o, tb):
    s, d = x.shape

    def body(x_ref, oa_ref, og0_ref, og1_ref, og2_ref, lg0_ref, lg1_ref, lg2_ref, ga_ref, gb_ref,
             woa_ref, wob_ref, wo_ref, x2_ref, ob_ref, lse0_ref, lse1_ref, lse2_ref, ya_ref, yb_ref, u_ref, scr_ref):
        og_refs, lg_refs = (og0_ref, og1_ref, og2_ref), (lg0_ref, lg1_ref, lg2_ref)
        l0, l1, l2 = [_from_residues(lg_refs[g], scr_ref, dil) for g, dil in enumerate(DILATIONS)]
        lmax = jnp.maximum(jnp.maximum(l0, l1), l2)
        w0, w1, w2 = jnp.exp(l0 - lmax), jnp.exp(l1 - lmax), jnp.exp(l2 - lmax)
        den = w0 + w1 + w2
        o0, o1, o2 = [_from_residues(og_refs[g], scr_ref, dil) for g, dil in enumerate(DILATIONS)]
        ob = ((w0 * o0 + w1 * o1 + w2 * o2) / den).astype(BF16)
        ob_ref[...] = ob
        lse = lmax + jnp.log(den)
        for g, (dil, ref) in enumerate(zip(DILATIONS, (lse0_ref, lse1_ref, lse2_ref))):
            _to_residues(lse, ref, scr_ref, dil, F32)
        ya = _dot_nn(oa_ref[...], woa_ref[...])
        yb = _dot_nt(ob, wob_ref[...])
        ya_ref[...] = ya.astype(BF16)
        yb_ref[...] = yb.astype(BF16)
        u = (ga_ref[...] * ya + gb_ref[...] * yb).astype(BF16)
        u_ref[...] = u
        x2_ref[...] = x_ref[...] + _dot_nn(u, wo_ref[...])

    sd = jax.ShapeDtypeStruct
    res = list(pl.pallas_call(
        body, name="mix_out", grid=(s // tb,),
        in_specs=[_rows(tb, d), _rows(tb, QA_W)] + _dil_specs(tb) * 2 + [
            _rows(tb, d), _rows(tb, d), _resident(w_oa.shape), _resident(w_ob_t.shape), _resident(w_o.shape)],
        out_specs=[_rows(tb, d), _rows(tb, GB_W)] + _dil_specs(tb) + [_rows(tb, d), _rows(tb, d), _rows(tb, d)],
        out_shape=[sd((s, d), F32), sd((s, GB_W), BF16)] + _dil_shapes(s, F32) + [
            sd((s, d), BF16), sd((s, d), BF16), sd((s, d), BF16)],
        scratch_shapes=[pltpu.VMEM((2, tb, LANES), F32)],
        compiler_params=_cparams(("arbitrary",)))(x, oa, *og, *lg, ga, gb, w_oa, w_ob_t, w_o))
    return res[:2] + [res[2:5]] + res[5:]


def _mlp_fwd(x2, w1_t, w2, g_mlp, tb, tc):
    s, d = x2.shape
    dff = w1_t.shape[0]

    def body(x_ref, w1_ref, w2_ref, g_ref, x3_ref, r_ref, h_ref):
        xv = x_ref[...]
        hb = (xv * _rstd(xv) * g_ref[...]).astype(BF16)
        h_ref[...] = hb
        x3_ref[...] = xv
        for c in range(dff // tc):
            sl = slice(tc * c, tc * c + tc)
            r = jnp.maximum(_dot_nt(hb, w1_ref[sl, :]), 0.0)
            r_ref[:, sl] = r.astype(BF16)
            x3_ref[...] += _dot_nn((r * r).astype(BF16), w2_ref[sl, :])

    sd = jax.ShapeDtypeStruct
    return pl.pallas_call(
        body, name="mlp_fwd", grid=(s // tb,),
        in_specs=[_rows(tb, d), _resident(w1_t.shape), _resident(w2.shape), _resident(g_mlp.shape)],
        out_specs=[_rows(tb, d), _rows(tb, dff), _rows(tb, d)],
        out_shape=[sd((s, d), F32), sd((s, dff), BF16), sd((s, d), BF16)],
        compiler_params=_cparams(("arbitrary",)))(x2, w1_t, w2, g_mlp)


def _ple_loss(x3, p, target, w_pg, w_p_t, g_ple, g_fin, tb):
    s, d = x3.shape
    dp = p.shape[1]

    def body(x_ref, p_ref, t_ref, wpg_ref, wp_ref, gple_ref, gfin_ref,
             dx3_ref, h3_ref, dpre_ref, dpe_ref, pb_ref, loss_ref, dgfin_ref, dgple_ref):
        @pl.when(pl.program_id(0) == 0)
        def _():
            loss_ref[...] = jnp.zeros_like(loss_ref)
            dgfin_ref[...] = jnp.zeros_like(dgfin_ref)
            dgple_ref[...] = jnp.zeros_like(dgple_ref)

        x3v = x_ref[...]
        r3 = _rstd(x3v)
        n3 = x3v * r3
        h3 = (n3 * gple_ref[...]).astype(BF16)
        h3_ref[...] = h3
        gp = _sigmoid(_dot_nn(h3, wpg_ref[...]))
        pb = p_ref[...].astype(BF16)
        pb_ref[...] = pb
        pe = _dot_nt(pb, wp_ref[...])
        x4 = x3v + gp * pe
        r4 = _rstd(x4)
        n4 = x4 * r4
        err = n4 * gfin_ref[...] - t_ref[...]
        loss_ref[...] += jnp.sum(0.5 * jnp.mean(err * err, axis=-1, keepdims=True), axis=0, keepdims=True)
        dy = err / d
        dgfin_ref[...] += _colsum(dy * n4)
        dx4 = _rms_bwd(dy, n4, r4, gfin_ref[...])
        dpe_ref[...] = (dx4 * gp).astype(BF16)
        dpre = (dx4 * pe * gp * (1.0 - gp)).astype(BF16)
        dpre_ref[...] = dpre
        dh3 = _dot_nt(dpre, wpg_ref[...])
        dgple_ref[...] += _colsum(dh3 * n3)
        dx3_ref[...] = dx4 + _rms_bwd(dh3, n3, r3, gple_ref[...])

    sd = jax.ShapeDtypeStruct
    return pl.pallas_call(
        body, name="ple_loss", grid=(s // tb,),
        in_specs=[_rows(tb, d), _rows(tb, dp), _rows(tb, d), _resident(w_pg.shape), _resident(w_p_t.shape),
                  _resident(g_ple.shape), _resident(g_fin.shape)],
        out_specs=[_rows(tb, d), _rows(tb, d), _rows(tb, d), _rows(tb, d), _rows(tb, dp),
                   _acc_spec((1, LANES)), _acc_spec((1, d)), _acc_spec((1, d))],
        out_shape=[sd((s, d), F32), sd((s, d), BF16), sd((s, d), BF16), sd((s, d), BF16), sd((s, dp), BF16),
                   sd((1, LANES), F32), sd((1, d), F32), sd((1, d), F32)],
        compiler_params=_cparams(("arbitrary",)))(x3, p, target, w_pg, w_p_t, g_ple, g_fin)


def _mlp_bwd(dx3, x2, r, w1_t, w2, g_mlp, tb, tc):
    s, d = x2.shape
    dff = w1_t.shape[0]

    def body(dx3_ref, x_ref, r_ref, w1_ref, w2_ref, g_ref, dx2_ref, df_ref, dg_ref, dh_ref):
        @pl.when(pl.program_id(0) == 0)
        def _():
            dg_ref[...] = jnp.zeros_like(dg_ref)

        dx3v = dx3_ref[...]
        dx3b = dx3v.astype(BF16)
        dh_ref[...] = jnp.zeros_like(dh_ref)
        for c in range(dff // tc):
            sl = slice(tc * c, tc * c + tc)
            df = (_dot_nt(dx3b, w2_ref[sl, :]) * (2.0 * r_ref[:, sl].astype(F32))).astype(BF16)
            df_ref[:, sl] = df
            dh_ref[...] += _dot_nn(df, w1_ref[sl, :])
        xv = x_ref[...]
        r2 = _rstd(xv)
        n2 = xv * r2
        dh = dh_ref[...]
        dg_ref[...] += _colsum(dh * n2)
        dx2_ref[...] = dx3v + _rms_bwd(dh, n2, r2, g_ref[...])

    sd = jax.ShapeDtypeStruct
    return pl.pallas_call(
        body, name="mlp_bwd", grid=(s // tb,),
        in_specs=[_rows(tb, d), _rows(tb, d), _rows(tb, dff), _resident(w1_t.shape), _resident(w2.shape),
                  _resident(g_mlp.shape)],
        out_specs=[_rows(tb, d), _rows(tb, dff), _acc_spec((1, d))],
        out_shape=[sd((s, d), F32), sd((s, dff), BF16), sd((1, d), F32)],
        scratch_shapes=[pltpu.VMEM((tb, d), F32)],
        compiler_params=_cparams(("arbitrary",)))(dx3, x2, r, w1_t, w2, g_mlp)


def _mix_out_bwd(dx2, ya, yb, ga, gb, ob, w_oa, w_ob_t, w_o, tb):
    s, d = dx2.shape

    def body(dx_ref, ya_ref, yb_ref, ga_ref, gb_ref, ob_ref, woa_ref, wob_ref, wo_ref,
             doa_ref, dob0_ref, dob1_ref, dob2_ref, dd0_ref, dd1_ref, dd2_ref, dga_ref, dgb_ref, dya_ref, dyb_ref,
             dbg_ref, scr_ref):
        @pl.when(pl.program_id(0) == 0)
        def _():
            dbg_ref[...] = jnp.zeros_like(dbg_ref)

        du = _dot_nt(dx_ref[...].astype(BF16), wo_ref[...])
        gav, gbv = ga_ref[...], gb_ref[...]
        dya = (du * gav).astype(BF16)
        dyb = (du * gbv).astype(BF16)
        dya_ref[...] = dya
        dyb_ref[...] = dyb
        dga = du * ya_ref[...].astype(F32) * gav * (1.0 - gav)
        dgb = du * yb_ref[...].astype(F32) * gbv * (1.0 - gbv)
        dga_ref[...] = dga.astype(BF16)
        dgb_ref[...] = dgb.astype(BF16)
        dbg_ref[:, 0:d] += _colsum(dga)
        dbg_ref[:, d:2 * d] += _colsum(dgb)
        doa_ref[...] = _dot_nt(dya, woa_ref[...]).astype(BF16)
        dob = _dot_nn(dyb, wob_ref[...])
        dd = _seg_sum(dob * ob_ref[...].astype(F32))
        for dil, dob_ref, dd_ref in zip(DILATIONS, (dob0_ref, dob1_ref, dob2_ref), (dd0_ref, dd1_ref, dd2_ref)):
            _to_residues(dob, dob_ref, scr_ref, dil, BF16)
            _to_residues(dd, dd_ref, scr_ref, dil, F32)

    sd = jax.ShapeDtypeStruct
    res = list(pl.pallas_call(
        body, name="mix_out_bwd", grid=(s // tb,),
        in_specs=[_rows(tb, d)] * 5 + [_rows(tb, GB_W), _resident(w_oa.shape), _resident(w_ob_t.shape),
                                       _resident(w_o.shape)],
        out_specs=[_rows(tb, QA_W)] + _dil_specs(tb) * 2 + [_rows(tb, d), _rows(tb, d), _rows(tb, d),
                                                           _rows(tb, d), _acc_spec((1, 2 * d))],
        out_shape=[sd((s, QA_W), BF16)] + _dil_shapes(s, BF16) + _dil_shapes(s, F32) + [
            sd((s, d), BF16), sd((s, d), BF16), sd((s, d), BF16), sd((s, d), BF16), sd((1, 2 * d), F32)],
        scratch_shapes=[pltpu.VMEM((2, tb, LANES), F32)],
        compiler_params=_cparams(("arbitrary",)))(dx2, ya, yb, ga, gb, ob, w_oa, w_ob_t, w_o))
    return res[:1] + [res[1:4], res[4:7]] + res[7:]


def _in_proj_bwd(dx2, x, dqrot, dkrot, dva, qraw, kraw, tabs, dqb, dkb, dvb, dga, dgb, w_in_t, g_mix, q_g, k_g, tb):
    s, d = x.shape
    din = w_in_t.shape[0]
    q_scale = HEAD_DIM_A ** -0.5
    b_scale = HEAD_DIM_B ** -0.5
    tc = 256

    def body(dx2_ref, x_ref, dq_ref, dk_ref, dv_ref, qraw_ref, kraw_ref, c_ref, s1_ref, s2_ref, *rest):
        dqb_refs, dkb_refs, dvb_refs = rest[0:3], rest[3:6], rest[6:9]
        (dga_ref, dgb_ref, w_ref, gmix_ref, qg_ref, kg_ref,
         dx_ref, dz_ref, dgmix_ref, dqg_ref, dkg_ref, dh_ref, scr_ref) = rest[9:]

        @pl.when(pl.program_id(0) == 0)
        def _():
            dgmix_ref[...] = jnp.zeros_like(dgmix_ref)
            dqg_ref[...] = jnp.zeros_like(dqg_ref)
            dkg_ref[...] = jnp.zeros_like(dkg_ref)

        cos, s1, s2 = c_ref[...], s1_ref[...], s2_ref[...]

        def head_bwd(drot, z, g_ref, acc_ref):
            dn = _rope_bwd(drot, cos, s1, s2)
            rr = _rstd(z)
            nn = z * rr
            acc_ref[...] += _colsum(dn * nn)
            return _rms_bwd(dn, nn, rr, g_ref[...])

        for h in range(N_Q_HEADS_A):
            sl = slice(128 * h, 128 * h + 128)
            dz_ref[:, OFF_QA + 128 * h:OFF_QA + 128 * h + 128] = head_bwd(
                dq_ref[:, sl] * q_scale, qraw_ref[:, sl], qg_ref, dqg_ref).astype(BF16)
        for h in range(N_KV_HEADS_A):
            sl = slice(128 * h, 128 * h + 128)
            dz_ref[:, OFF_KA + 128 * h:OFF_KA + 128 * h + 128] = head_bwd(
                dk_ref[:, sl] * LN_2, kraw_ref[:, sl], kg_ref, dkg_ref).astype(BF16)
        dz_ref[:, OFF_VA:OFF_VA + KA_W] = dv_ref[...].astype(BF16)
        for g, dil in enumerate(DILATIONS):
            dz_ref[:, OFF_QB + GB_W * g:OFF_QB + GB_W * (g + 1)] = (
                _from_residues(dqb_refs[g], scr_ref, dil) * b_scale).astype(BF16)
            dz_ref[:, OFF_KB + GB_W * g:OFF_KB + GB_W * (g + 1)] = _from_residues(dkb_refs[g], scr_ref, dil).astype(BF16)
            dz_ref[:, OFF_VB + GB_W * g:OFF_VB + GB_W * (g + 1)] = _from_residues(dvb_refs[g], scr_ref, dil).astype(BF16)
        dz_ref[:, OFF_GA:OFF_GA + d] = dga_ref[...]
        dz_ref[:, OFF_GA + d:OFF_GA + 2 * d] = dgb_ref[...]
        dh_ref[...] = jnp.zeros_like(dh_ref)
        for c in range(din // tc):
            sl = slice(tc * c, tc * c + tc)
            dh_ref[...] += _dot_nn(dz_ref[:, sl], w_ref[sl, :])
        xv = x_ref[...]
        r1 = _rstd(xv)
        n1 = xv * r1
        dh = dh_ref[...]
        dgmix_ref[...] += _colsum(dh * n1)
        dx_ref[...] = dx2_ref[...] + _rms_bwd(dh, n1, r1, gmix_ref[...])

    sd = jax.ShapeDtypeStruct
    return pl.pallas_call(
        body, name="in_proj_bwd", grid=(s // tb,),
        in_specs=[_rows(tb, d), _rows(tb, d), _rows(tb, QA_W), _rows(tb, KA_W), _rows(tb, KA_W), _rows(tb, QA_W),
                  _rows(tb, KA_W), _rows(tb, LANES), _rows(tb, LANES), _rows(tb, LANES),
                  ] + _dil_specs(tb) * 3 + [_rows(tb, d), _rows(tb, d),
                  _resident(w_in_t.shape), _resident(g_mix.shape), _resident(q_g.shape), _resident(k_g.shape)],
        out_specs=[_rows(tb, d), _rows(tb, din), _acc_spec((1, d)), _acc_spec((1, HEAD_DIM_A)),
                   _acc_spec((1, HEAD_DIM_A))],
        out_shape=[sd((s, d), F32), sd((s, din), BF16), sd((1, d), F32), sd((1, HEAD_DIM_A), F32),
                   sd((1, HEAD_DIM_A), F32)],
        scratch_shapes=[pltpu.VMEM((tb, d), F32), pltpu.VMEM((2, tb, LANES), F32)],
        compiler_params=_cparams(("arbitrary",)))(
        dx2, x, dqrot, dkrot, dva, qraw, kraw, *tabs, *dqb, *dkb, *dvb, dga, dgb, w_in_t, g_mix, q_g, k_g)


def _identity(v):
    return v


def _to_bf16(v):
    return v.astype(BF16)


def _square_bf16(v):
    vf = v.astype(F32)
    return (vf * vf).astype(BF16)


def _weight_grad(name, a, b, ti, tj, tk, a_fn=_identity, b_fn=_identity):
    t, m = a.shape
    n = b.shape[1]
    n_k = t // tk

    def body(a_ref, b_ref, o_ref, acc_ref):
        k = pl.program_id(2)

        @pl.when(k == 0)
        def _():
            acc_ref[...] = jnp.zeros_like(acc_ref)

        acc_ref[...] += _dot_tn(a_fn(a_ref[...]), b_fn(b_ref[...]))

        @pl.when(k == n_k - 1)
        def _():
            o_ref[...] = acc_ref[...].astype(BF16)

    return pl.pallas_call(
        body, name=name, grid=(m // ti, n // tj, n_k),
        in_specs=[pl.BlockSpec((tk, ti), lambda i, j, k: (k, i)), pl.BlockSpec((tk, tj), lambda i, j, k: (k, j))],
        out_specs=pl.BlockSpec((ti, tj), lambda i, j, k: (i, j)),
        out_shape=jax.ShapeDtypeStruct((m, n), BF16),
        scratch_shapes=[pltpu.VMEM((ti, tj), F32)],
        compiler_params=_cparams(("arbitrary", "arbitrary", "arbitrary")))(a, b)


def _sum_slots(name, recv):
    _, n, k = recv.shape
    tc = min(k, 256)

    def body(r_ref, o_ref):
        acc = r_ref[0].astype(F32)
        for i in range(1, N_DEV):
            acc = acc + r_ref[i].astype(F32)
        o_ref[...] = acc

    return pl.pallas_call(
        body, name=name, grid=(k // tc,),
        in_specs=[pl.BlockSpec((N_DEV, n, tc), lambda j: (0, 0, j))],
        out_specs=pl.BlockSpec((n, tc), lambda j: (0, j)),
        out_shape=jax.ShapeDtypeStruct((n, k), F32),
        compiler_params=_cparams(("arbitrary",)))(recv)


def _adamw_math(w, g, m, v):
    m = ADAM_B1 * m + (1.0 - ADAM_B1) * g
    v = ADAM_B2 * v + (1.0 - ADAM_B2) * (g * g)
    m_hat = m / (1.0 - ADAM_B1 ** ADAM_STEP)
    v_hat = v / (1.0 - ADAM_B2 ** ADAM_STEP)
    delta = -ADAM_LR * (m_hat / (jnp.sqrt(v_hat) + ADAM_EPS) + ADAM_WD * w)
    return delta, m, v


def _adamw(name, w, g, m, v):
    r, c = w.shape
    tr = min(r, 256)

    def body(w_ref, g_ref, m_ref, v_ref, d_ref, mo_ref, vo_ref):
        d_ref[...], mo_ref[...], vo_ref[...] = _adamw_math(w_ref[...], g_ref[...], m_ref[...], v_ref[...])

    spec = pl.BlockSpec((tr, c), lambda i: (i, 0))
    return pl.pallas_call(
        body, name=name, grid=(r // tr,), in_specs=[spec] * 4, out_specs=[spec] * 3,
        out_shape=[jax.ShapeDtypeStruct((r, c), F32)] * 3,
        compiler_params=_cparams(("arbitrary",)))(w, g, m, v)


def _small_update(parts, w, m, v):
    def body(p_ref, w_ref, m_ref, v_ref, g_ref, d_ref, mo_ref, vo_ref):
        g = p_ref[0]
        for i in range(1, N_DEV):
            g = g + p_ref[i]
        g_ref[...] = g
        d_ref[...], mo_ref[...], vo_ref[...] = _adamw_math(w_ref[...], g, m_ref[...], v_ref[...])

    return pl.pallas_call(body, name="small_update", out_shape=[jax.ShapeDtypeStruct(w.shape, F32)] * 4)(
        parts, w, m, v)


def _pack_rows(vectors, n_rows):
    flat = jnp.concatenate([v.reshape(-1).astype(F32) for v in vectors])
    flat = jnp.pad(flat, (0, n_rows * LANES - flat.shape[0]))
    return flat.reshape(n_rows, LANES)


def _pick_tile(n, prefs):
    for t in prefs:
        if n % t == 0:
            return t
    return n


def kernel(x, p, norm_mix_g, w_in, b_gate, q_norm_g, k_norm_g, rel_bias, w_out_a, w_out_b, w_out, norm_mlp_g, w_ff1, w_ff2, norm_ple_g, w_ple_gate, w_ple, final_norm_g, loss_target, m_norm_mix_g, m_w_in, m_b_gate, m_q_norm_g, m_k_norm_g, m_rel_bias, m_w_out_a, m_w_out_b, m_w_out, m_norm_mlp_g, m_w_ff1, m_w_ff2, m_norm_ple_g, m_w_ple_gate, m_w_ple, m_final_norm_g, v_norm_mix_g, v_w_in, v_b_gate, v_q_norm_g, v_k_norm_g, v_rel_bias, v_w_out_a, v_w_out_b, v_w_out, v_norm_mlp_g, v_w_ff1, v_w_ff2, v_norm_ple_g, v_w_ple_gate, v_w_ple, v_final_norm_g):
    s, d = x.shape[1], x.shape[2]
    xs, ps, ts = x[0], p[0, 0], loss_target[0]
    tb = _pick_tile(s, (512, 256))
    tq = _pick_tile(s, (256,))
    tk = _pick_tile(s, (1024, 512))
    cb = _pick_tile(s, (512,))
    fin_g = final_norm_g.reshape(1, d)

    col_sharded = {"w_in": w_in[0], "w_out_b": w_out_b[0], "w_ff1": w_ff1[0], "w_ple": w_ple[0]}
    row_sharded = {"w_out_a": w_out_a[0], "w_out": w_out[0], "w_ff2": w_ff2[0], "w_ple_gate": w_ple_gate[0]}
    order = ["w_in", "w_out_a", "w_out_b", "w_out", "w_ff1", "w_ff2", "w_ple_gate", "w_ple"]
    shards = [(col_sharded[n].T if n in col_sharded else row_sharded[n]).astype(BF16) for n in order]
    w_in_t, w_oa, w_ob_t, w_o, w_ff1_t, w_ff2_f, w_pg, w_p_t = _all_gather(shards)

    tabs = _rope_tables(s)
    (h1, qraw, kraw, qrot, krot, va, qb, kb, vb, ga, gb) = _in_proj(
        xs, tabs, w_in_t, norm_mix_g, b_gate, q_norm_g, k_norm_g, tb)
    oa, lse_a = _attn_a_fwd(qrot, krot, va, tq, tk)
    flat = lambda arrs: [a.reshape(s, GB_W) for a in arrs]
    split = lambda arrs: [a.reshape(dil, s // dil, GB_W) for a, dil in zip(arrs, DILATIONS)]
    qb_r, kb_r, vb_r = flat(qb), flat(kb), flat(vb)
    bmaps = [[jnp.asarray(m) for m in _bucket_maps(dil)] for dil in DILATIONS]
    bias_tabs = [rel_bias[:, N_HEADS_PER_DIL * g:N_HEADS_PER_DIL * (g + 1)] for g in range(3)]
    band_out = [_band_fwd(dil, qb_r[g], kb_r[g], vb_r[g], bmaps[g][0], bias_tabs[g], cb)
                for g, dil in enumerate(DILATIONS)]
    og, lg = split([o for o, _ in band_out]), split([l for _, l in band_out])
    x2, ob, lse_b, ya, yb, u = _mix_out(xs, oa, og, lg, ga, gb, w_oa, w_ob_t, w_o, tb)
    tc = _pick_tile(w_ff1_t.shape[0], (512,))
    x3, r_act, h2 = _mlp_fwd(x2, w_ff1_t, w_ff2_f, norm_mlp_g, tb, tc)

    dx3, h3, dpre, dpe, pb, loss_part, dg_fin, dg_ple = _ple_loss(
        x3, ps, ts, w_pg, w_p_t, norm_ple_g, fin_g, tb)
    dx2, df, dg_mlp = _mlp_bwd(dx3, x2, r_act, w_ff1_t, w_ff2_f, norm_mlp_g, tb, tc)
    doa, dob, dd, dga, dgb, dya, dyb, dbg = _mix_out_bwd(dx2, ya, yb, ga, gb, ob, w_oa, w_ob_t, w_o, tb)
    dqrot, dkrot, dva = _attn_a_bwd(qrot, krot, va, oa, doa, lse_a, tq, tk)
    dob_r, lse_r, dd_r = flat(dob), flat(lse_b), flat(dd)
    bwd_q = [_band_bwd_q(dil, qb_r[g], kb_r[g], vb_r[g], dob_r[g], lse_r[g], dd_r[g], bmaps[g][0], bias_tabs[g], cb)
             for g, dil in enumerate(DILATIONS)]
    bwd_kv = [_band_bwd_kv(dil, qb_r[g], kb_r[g], vb_r[g], dob_r[g], lse_r[g], dd_r[g], bmaps[g][1], bias_tabs[g], cb)
              for g, dil in enumerate(DILATIONS)]
    dqb, dkb, dvb = split([r[0] for r in bwd_q]), split([r[0] for r in bwd_kv]), split([r[1] for r in bwd_kv])
    grad_x, dz, dg_mix, dg_q, dg_k = _in_proj_bwd(
        dx2, xs, dqrot, dkrot, dva, qraw, kraw, tabs, dqb, dkb, dvb, dga, dgb, w_in_t, norm_mix_g,
        q_norm_g, k_norm_g, _pick_tile(s, (256,)))
    d_rel = jnp.concatenate([r[1][:, :N_HEADS_PER_DIL] for r in bwd_q], axis=1)

    tkk = _pick_tile(s, (1024, 512))
    din = w_in_t.shape[0]
    ti_in = _pick_tile(din, (din // 2,)) if (din // 2) % LANES == 0 else din
    dff = w_ff1_t.shape[0]
    t1k = lambda n: _pick_tile(n, (1024, 512, 256))
    partials = [
        _weight_grad("grad_w_in", dz, h1, ti_in, t1k(d), tkk),
        _weight_grad("grad_w_out_a", oa, dya, t1k(QA_W), t1k(d), tkk),
        _weight_grad("grad_w_out_b", dyb, ob, t1k(d), GB_W, tkk),
        _weight_grad("grad_w_out", u, dx2, t1k(d), t1k(d), tkk, b_fn=_to_bf16),
        _weight_grad("grad_w_ff1", df, h2, t1k(dff), t1k(d), tkk),
        _weight_grad("grad_w_ff2", r_act, dx3, t1k(dff), t1k(d), tkk, a_fn=_square_bf16, b_fn=_to_bf16),
        _weight_grad("grad_w_ple_gate", h3, dpre, t1k(d), t1k(d), tkk),
        _weight_grad("grad_w_ple", dpe, pb, t1k(d), ps.shape[1], tkk),
    ]
    received = _scatter_blocks(partials)
    sums = [_sum_slots("sum_" + n, r) for n, r in zip(order, received)]

    given_w = dict(w_in=w_in, w_out_a=w_out_a, w_out_b=w_out_b, w_out=w_out, w_ff1=w_ff1, w_ff2=w_ff2,
                   w_ple_gate=w_ple_gate, w_ple=w_ple)
    given_m = dict(w_in=m_w_in, w_out_a=m_w_out_a, w_out_b=m_w_out_b, w_out=m_w_out, w_ff1=m_w_ff1, w_ff2=m_w_ff2,
                   w_ple_gate=m_w_ple_gate, w_ple=m_w_ple)
    given_v = dict(w_in=v_w_in, w_out_a=v_w_out_a, w_out_b=v_w_out_b, w_out=v_w_out, w_ff1=v_w_ff1, w_ff2=v_w_ff2,
                   w_ple_gate=v_w_ple_gate, w_ple=v_w_ple)
    big = {}
    for n, gsum in zip(order, sums):
        g = gsum.T if n in col_sharded else gsum
        delta, new_m, new_v = _adamw("adamw_" + n, given_w[n][0], g, given_m[n][0], given_v[n][0])
        big[n] = tuple(a[None] for a in (g, delta, new_m, new_v))

    small_names = ["norm_mix_g", "b_gate", "q_norm_g", "k_norm_g", "rel_bias", "norm_mlp_g", "norm_ple_g",
                   "final_norm_g"]
    small_w = [norm_mix_g, b_gate, q_norm_g, k_norm_g, rel_bias, norm_mlp_g, norm_ple_g, final_norm_g]
    small_m = [m_norm_mix_g, m_b_gate, m_q_norm_g, m_k_norm_g, m_rel_bias, m_norm_mlp_g, m_norm_ple_g,
               m_final_norm_g]
    small_v = [v_norm_mix_g, v_b_gate, v_q_norm_g, v_k_norm_g, v_rel_bias, v_norm_mlp_g, v_norm_ple_g,
               v_final_norm_g]
    small_g = [dg_mix, dbg, dg_q, dg_k, d_rel, dg_mlp, dg_ple, dg_fin]
    sizes = [int(np.prod(w.shape)) for w in small_w]
    n_rows = -(-(sum(-(-sz // LANES) for sz in sizes) + 1) // 8) * 8
    pad = lambda v: jnp.pad(v.reshape(-1).astype(F32), (0, -v.size % LANES))
    pack = lambda vs, last: _pack_rows([pad(v) for v in vs] + [last], n_rows)
    zero_row = jnp.zeros((LANES,), F32)
    parts = _small_all_gather(pack(small_g, loss_part.reshape(-1) * (jnp.arange(LANES) == 0)))
    g_all, d_all, m_all, v_all = _small_update(parts, pack(small_w, zero_row), pack(small_m, zero_row),
                                               pack(small_v, zero_row))
    small = {}
    row = 0
    for n, w, sz in zip(small_names, small_w, sizes):
        nr = -(-sz // LANES)
        small[n] = tuple(a[row:row + nr].reshape(-1)[:sz].reshape(w.shape) for a in (g_all, d_all, m_all, v_all))
        row += nr
    loss = g_all[row, 0]

    names = ["norm_mix_g", "w_in", "b_gate", "q_norm_g", "k_norm_g", "rel_bias", "w_out_a", "w_out_b", "w_out",
             "norm_mlp_g", "w_ff1", "w_ff2", "norm_ple_g", "w_ple_gate", "w_ple", "final_norm_g"]
    res = {n: (big[n] if n in big else small[n]) for n in names}
    return (loss, grad_x[None], *[res[n][0] for n in names], *[res[n][1] for n in names],
            *[res[n][2] for n in names], *[res[n][3] for n in names])
```

```python
import functools
import math

import numpy as np
import jax
import jax.numpy as jnp
from jax import lax
from jax.experimental import pallas as pl
from jax.experimental.pallas import tpu as pltpu

F32 = jnp.float32
BF16 = jnp.bfloat16
MESH = pl.DeviceIdType.MESH

NORM_EPS = 1e-6
NEG_INF = -1e30
LOG2_E = math.log2(math.e)
LN_2 = math.log(2.0)
GRID_W = 64
ROPE_THETA = 10000.0
HEAD_DIM_A = 128
N_Q_HEADS_A = 8
N_KV_HEADS_A = 2
Q_PER_KV = N_Q_HEADS_A // N_KV_HEADS_A
HEAD_DIM_B = 64
N_HEADS_PER_DIL = 4
DILATIONS = (1, 4, 16)
BAND = 64
N_REL_BUCKETS = 32
REL_MAX_DIST = 1024
QA_W = N_Q_HEADS_A * HEAD_DIM_A
KA_W = N_KV_HEADS_A * HEAD_DIM_A
GB_W = N_HEADS_PER_DIL * HEAD_DIM_B
QB_W = GB_W * len(DILATIONS)
OFF_QA, OFF_KA, OFF_VA = 0, QA_W, QA_W + KA_W
OFF_QB = QA_W + 2 * KA_W
OFF_KB = OFF_QB + QB_W
OFF_VB = OFF_KB + QB_W
OFF_GA = OFF_VB + QB_W
N_DEV = 8
LANES = 128
VMEM_LIMIT = 56 * 2 ** 20

ADAM_LR, ADAM_B1, ADAM_B2, ADAM_EPS, ADAM_WD, ADAM_STEP = 0.001, 0.9, 0.999, 1e-08, 0.01, 10


def _cparams(sem):
    return pltpu.CompilerParams(dimension_semantics=sem, vmem_limit_bytes=VMEM_LIMIT)


def _resident(shape):
    nd = len(shape)
    return pl.BlockSpec(shape, lambda *_: (0,) * nd, pipeline_mode=pl.Buffered(1))


def _acc_spec(shape):
    nd = len(shape)
    return pl.BlockSpec(shape, lambda *_: (0,) * nd)


def _rows(tb, c):
    return pl.BlockSpec((tb, c), lambda i: (i, 0))


def _dil_shapes(s, dtype):
    return [jax.ShapeDtypeStruct((dil, s // dil, GB_W), dtype) for dil in DILATIONS]


def _dil_specs(tb):
    return [pl.BlockSpec((dil, tb // dil, GB_W), lambda i: (0, i, 0)) for dil in DILATIONS]


def _to_residues(val, out_ref, scr_ref, dil, dtype):
    if dil == 1:
        out_ref[0] = val.astype(dtype)
        return
    n = val.shape[0] // dil
    scr_ref[0] = val[:, :LANES]
    scr_ref[1] = val[:, LANES:]
    for r in range(dil):
        out_ref[r] = jnp.concatenate([scr_ref[0, pl.ds(r, n, stride=dil), :],
                                      scr_ref[1, pl.ds(r, n, stride=dil), :]], axis=1).astype(dtype)


def _from_residues(in_ref, scr_ref, dil):
    if dil == 1:
        return in_ref[0]
    n = in_ref.shape[1]
    for r in range(dil):
        v = in_ref[r]
        scr_ref[0, pl.ds(r, n, stride=dil), :] = v[:, :LANES]
        scr_ref[1, pl.ds(r, n, stride=dil), :] = v[:, LANES:]
    return jnp.concatenate([scr_ref[0], scr_ref[1]], axis=1)


def _dot_nt(a, b):
    return lax.dot_general(a, b, (((1,), (1,)), ((), ())), preferred_element_type=F32)


def _dot_nn(a, b):
    return lax.dot_general(a, b, (((1,), (0,)), ((), ())), preferred_element_type=F32)


def _dot_tn(a, b):
    return lax.dot_general(a, b, (((0,), (0,)), ((), ())), preferred_element_type=F32)


def _rstd(x):
    return lax.rsqrt(jnp.mean(x * x, axis=-1, keepdims=True) + NORM_EPS)


def _rms_bwd(dy, n, r, g):
    dn = dy * g
    return r * (dn - n * jnp.mean(dn * n, axis=-1, keepdims=True))


def _colsum(v):
    return jnp.sum(v, axis=0, keepdims=True)


def _sigmoid(v):
    return 1.0 / (1.0 + jnp.exp(-v))


def _rope_fwd(n, c, s1, s2):
    return n * c + pltpu.roll(n, 32, 1) * s1 + pltpu.roll(n, 96, 1) * s2


def _rope_bwd(d, c, s1, s2):
    return d * c + pltpu.roll(d * s1, 96, 1) + pltpu.roll(d * s2, 32, 1)


def _rope_tables(s):
    half = HEAD_DIM_A // 2
    inv = jnp.power(ROPE_THETA, -jnp.arange(0, half, 2, dtype=F32) / half)
    t = jnp.arange(s, dtype=jnp.int32)
    ang_r = (t // GRID_W).astype(F32)[:, None] * inv[None, :]
    ang_c = (t % GRID_W).astype(F32)[:, None] * inv[None, :]
    cr, sr, cc, sc = jnp.cos(ang_r), jnp.sin(ang_r), jnp.cos(ang_c), jnp.sin(ang_c)
    z = jnp.zeros_like(sr)
    cos = jnp.concatenate([cr, cr, cc, cc], axis=1)
    s1 = jnp.concatenate([z, sr, z, sc], axis=1)
    s2 = jnp.concatenate([-sr, z, -sc, z], axis=1)
    return cos, s1, s2


def _my_place():
    return lax.axis_index("x"), lax.axis_index("y"), lax.axis_index("c")


def _all_gather(shards):
    nw = len(shards)

    def body(*refs):
        ins, outs = refs[:nw], refs[nw:2 * nw]
        send_sems, recv_sems, local_sems = refs[2 * nw:]
        x, y, c = _my_place()
        me, sibling = (x, y, c), (x, y, 1 - c)
        chips = [(1 - x, y), (x, 1 - y), (1 - x, 1 - y)]

        def rows(w, px, py, pc):
            n = ins[w].shape[0]
            return outs[w].at[pl.ds(pl.multiple_of((4 * px + 2 * py + pc) * n, 16), n), :]

        def copy(w, k, block, to, src=None):
            return pltpu.make_async_remote_copy(
                src_ref=rows(w, *block) if src is None else src, dst_ref=rows(w, *block),
                send_sem=send_sems.at[w, k], recv_sem=recv_sems.at[w, k], device_id=to, device_id_type=MESH)

        mine = [pltpu.make_async_copy(ins[w], rows(w, *me), local_sems.at[w]) for w in range(nw)]
        for cp in mine:
            cp.start()
        first = []
        for w in range(nw):
            first.append(copy(w, 0, me, sibling, src=ins[w]))
            first += [copy(w, 1 + j, me, (*chip, c), src=ins[w]) for j, chip in enumerate(chips)]
        for cp in first:
            cp.start()
        passed = []
        for j, chip in enumerate(chips):
            for w in range(nw):
                copy(w, 1 + j, (*chip, c), me).wait_recv()
                fwd = copy(w, 4 + j, (*chip, c), sibling)
                fwd.start()
                passed.append(fwd)
        for w in range(nw):
            copy(w, 0, sibling, me).wait_recv()
        for j, chip in enumerate(chips):
            for w in range(nw):
                copy(w, 4 + j, (*chip, 1 - c), me).wait_recv()
        for cp in first + passed:
            cp.wait_send()
        for cp in mine:
            cp.wait()

    any_spec = pl.BlockSpec(memory_space=pl.ANY)
    return pl.pallas_call(
        body, name="weights_all_gather",
        out_shape=[jax.ShapeDtypeStruct((N_DEV * s.shape[0], s.shape[1]), s.dtype) for s in shards],
        in_specs=[any_spec] * nw, out_specs=[any_spec] * nw,
        scratch_shapes=[pltpu.SemaphoreType.DMA((nw, 7)), pltpu.SemaphoreType.DMA((nw, 7)),
                        pltpu.SemaphoreType.DMA((nw,))],
    )(*shards)


_FLIPS = [(fx, fy, fc) for fx in (0, 1) for fy in (0, 1) for fc in (0, 1)][1:]


def _scatter_blocks(partials):
    nw = len(partials)

    def body(*refs):
        ins, outs = refs[:nw], refs[nw:2 * nw]
        send_sems, recv_sems, local_sems = refs[2 * nw:]
        x, y, c = _my_place()
        my_idx = 4 * x + 2 * y + c

        def block(w, idx):
            n = outs[w].shape[1]
            return ins[w].at[pl.ds(pl.multiple_of(idx * n, 16), n), :]

        def peer(k):
            fx, fy, fc = _FLIPS[k]
            return (1 - x if fx else x, 1 - y if fy else y, 1 - c if fc else c)

        def copy(w, k):
            to = peer(k)
            to_idx = 4 * to[0] + 2 * to[1] + to[2]
            return pltpu.make_async_remote_copy(
                src_ref=block(w, to_idx), dst_ref=outs[w].at[my_idx],
                send_sem=send_sems.at[w, k], recv_sem=recv_sems.at[w, k], device_id=to, device_id_type=MESH)

        def arrival(w, k):
            frm = peer(k)
            frm_idx = 4 * frm[0] + 2 * frm[1] + frm[2]
            return pltpu.make_async_remote_copy(
                src_ref=block(w, my_idx), dst_ref=outs[w].at[frm_idx],
                send_sem=send_sems.at[w, k], recv_sem=recv_sems.at[w, k], device_id=(x, y, c), device_id_type=MESH)

        mine = [pltpu.make_async_copy(block(w, my_idx), outs[w].at[my_idx], local_sems.at[w]) for w in range(nw)]
        for cp in mine:
            cp.start()
        sends = [copy(w, k) for k in range(7) for w in range(nw)]
        for cp in sends:
            cp.start()
        for k in range(7):
            for w in range(nw):
                arrival(w, k).wait_recv()
        for cp in sends:
            cp.wait_send()
        for cp in mine:
            cp.wait()

    any_spec = pl.BlockSpec(memory_space=pl.ANY)
    return pl.pallas_call(
        body, name="grads_scatter",
        out_shape=[jax.ShapeDtypeStruct((N_DEV, p.shape[0] // N_DEV, p.shape[1]), p.dtype) for p in partials],
        in_specs=[any_spec] * nw, out_specs=[any_spec] * nw,
        scratch_shapes=[pltpu.SemaphoreType.DMA((nw, 7)), pltpu.SemaphoreType.DMA((nw, 7)),
                        pltpu.SemaphoreType.DMA((nw,))],
    )(*partials)


def _small_all_gather(v):
    def body(v_ref, out_ref, send_sems, recv_sems):
        x, y, c = _my_place()
        my_idx = 4 * x + 2 * y + c
        out_ref[my_idx] = v_ref[...]
        sends = []
        for k, (fx, fy, fc) in enumerate(_FLIPS):
            to = (1 - x if fx else x, 1 - y if fy else y, 1 - c if fc else c)
            sends.append(pltpu.make_async_remote_copy(
                src_ref=v_ref, dst_ref=out_ref.at[my_idx], send_sem=send_sems.at[k], recv_sem=recv_sems.at[k],
                device_id=to, device_id_type=MESH))
        for cp in sends:
            cp.start()
        for k, (fx, fy, fc) in enumerate(_FLIPS):
            frm_idx = 4 * (1 - x if fx else x) + 2 * (1 - y if fy else y) + (1 - c if fc else c)
            pltpu.make_async_remote_copy(
                src_ref=v_ref, dst_ref=out_ref.at[frm_idx], send_sem=send_sems.at[k], recv_sem=recv_sems.at[k],
                device_id=(x, y, c), device_id_type=MESH).wait_recv()
        for cp in sends:
            cp.wait_send()

    vm = pl.BlockSpec(memory_space=pltpu.VMEM)
    return pl.pallas_call(
        body, name="small_all_gather", out_shape=jax.ShapeDtypeStruct((N_DEV,) + v.shape, v.dtype),
        in_specs=[vm], out_specs=vm,
        scratch_shapes=[pltpu.SemaphoreType.DMA((7,)), pltpu.SemaphoreType.DMA((7,))],
    )(v)


def _in_proj(x, tabs, w_in_t, g_mix, b_gate, q_g, k_g, tb):
    s, d = x.shape
    n_gate_chunks = d // 256
    q_scale = HEAD_DIM_A ** -0.5 * LOG2_E
    b_scale = HEAD_DIM_B ** -0.5

    def body(x_ref, c_ref, s1_ref, s2_ref, w_ref, gmix_ref, bg_ref, qg_ref, kg_ref,
             h1_ref, qraw_ref, kraw_ref, qrot_ref, krot_ref, va_ref, *rest):
        qb_refs, kb_refs, vb_refs = rest[0:3], rest[3:6], rest[6:9]
        ga_ref, gb_ref, scr_ref = rest[9:]
        xv = x_ref[...]
        hb = (xv * _rstd(xv) * gmix_ref[...]).astype(BF16)
        h1_ref[...] = hb
        cos, s1, s2 = c_ref[...], s1_ref[...], s2_ref[...]

        def proj(lo, width):
            return _dot_nt(hb, w_ref[lo:lo + width, :])

        def norm_rope(z, g):
            return _rope_fwd(z * _rstd(z) * g, cos, s1, s2)

        for j in range(QA_W // 256):
            z = proj(OFF_QA + 256 * j, 256)
            qraw_ref[:, 256 * j:256 * j + 256] = z
            for hh in range(2):
                lo = 256 * j + 128 * hh
                qrot_ref[:, lo:lo + 128] = (norm_rope(z[:, 128 * hh:128 * hh + 128], qg_ref[...]) * q_scale).astype(BF16)
        z = proj(OFF_KA, 256)
        kraw_ref[...] = z
        for hh in range(2):
            krot_ref[:, 128 * hh:128 * hh + 128] = norm_rope(z[:, 128 * hh:128 * hh + 128], kg_ref[...]).astype(BF16)
        va_ref[...] = proj(OFF_VA, 256).astype(BF16)
        for g, dil in enumerate(DILATIONS):
            _to_residues(proj(OFF_QB + GB_W * g, GB_W) * b_scale, qb_refs[g], scr_ref, dil, BF16)
            _to_residues(proj(OFF_KB + GB_W * g, GB_W), kb_refs[g], scr_ref, dil, BF16)
            _to_residues(proj(OFF_VB + GB_W * g, GB_W), vb_refs[g], scr_ref, dil, BF16)
        for j in range(n_gate_chunks):
            sl = slice(256 * j, 256 * j + 256)
            ga_ref[:, sl] = _sigmoid(proj(OFF_GA + 256 * j, 256) + bg_ref[:, sl])
            gb_ref[:, sl] = _sigmoid(proj(OFF_GA + d + 256 * j, 256) + bg_ref[:, d + 256 * j:d + 256 * j + 256])

    sd = jax.ShapeDtypeStruct
    outs = [sd((s, d), BF16), sd((s, QA_W), F32), sd((s, KA_W), F32), sd((s, QA_W), BF16), sd((s, KA_W), BF16),
            sd((s, KA_W), BF16)] + _dil_shapes(s, BF16) * 3 + [sd((s, d), F32), sd((s, d), F32)]
    out_specs = [_rows(tb, d), _rows(tb, QA_W), _rows(tb, KA_W), _rows(tb, QA_W), _rows(tb, KA_W), _rows(tb, KA_W)
                 ] + _dil_specs(tb) * 3 + [_rows(tb, d), _rows(tb, d)]
    in_specs = [_rows(tb, d), _rows(tb, LANES), _rows(tb, LANES), _rows(tb, LANES), _resident(w_in_t.shape),
                _resident(g_mix.shape), _resident(b_gate.shape), _resident(q_g.shape), _resident(k_g.shape)]
    res = list(pl.pallas_call(body, name="in_proj", grid=(s // tb,), in_specs=in_specs, out_specs=out_specs,
                              out_shape=outs, scratch_shapes=[pltpu.VMEM((2, tb, LANES), F32)],
                              compiler_params=_cparams(("arbitrary",)))(
        x, *tabs, w_in_t, g_mix, b_gate, q_g, k_g))
    return res[:6] + [res[6:9], res[9:12], res[12:15]] + res[15:]


def _attn_a_fwd(qrot, krot, va, tq, tk):
    s = qrot.shape[0]
    n_kv = s // tk
    gw = Q_PER_KV * HEAD_DIM_A

    def body(q_ref, k_ref, v_ref, o_ref, lse_ref):
        q4 = jnp.concatenate([q_ref[:, 128 * h:128 * h + 128] for h in range(Q_PER_KV)], axis=0)

        def step(j, carry):
            m, l, acc = carry
            sl = pl.ds(pl.multiple_of(j * tk, tk), tk)
            kj, vj = k_ref[sl, :], v_ref[sl, :]
            sc = _dot_nt(kj, q4)
            m_new = jnp.maximum(m, jnp.max(sc, axis=0, keepdims=True))
            p = jnp.exp2(sc - m_new)
            alpha = jnp.exp2(m - m_new)
            l = alpha * l + jnp.sum(p, axis=0, keepdims=True)
            acc = alpha * acc + _dot_tn(vj, p.astype(BF16))
            return m_new, l, acc

        rows = Q_PER_KV * tq
        m, l, acc = lax.fori_loop(0, n_kv, step, (jnp.full((1, rows), NEG_INF, F32), jnp.zeros((1, rows), F32),
                                                  jnp.zeros((HEAD_DIM_A, rows), F32)))
        o = (acc / l).T
        lse = m + jnp.log2(l)
        for h in range(Q_PER_KV):
            o_ref[:, 128 * h:128 * h + 128] = o[h * tq:(h + 1) * tq].astype(BF16)
            lse_ref[0, h:h + 1, :] = lse[:, h * tq:(h + 1) * tq]

    return pl.pallas_call(
        body, name="attn_a_fwd", grid=(N_KV_HEADS_A, s // tq),
        in_specs=[pl.BlockSpec((tq, gw), lambda g, i: (i, g)),
                  pl.BlockSpec((s, HEAD_DIM_A), lambda g, i: (0, g)),
                  pl.BlockSpec((s, HEAD_DIM_A), lambda g, i: (0, g))],
        out_specs=[pl.BlockSpec((tq, gw), lambda g, i: (i, g)),
                   pl.BlockSpec((1, Q_PER_KV, tq), lambda g, i: (g, 0, i))],
        out_shape=[jax.ShapeDtypeStruct((s, QA_W), BF16), jax.ShapeDtypeStruct((N_KV_HEADS_A, Q_PER_KV, s), F32)],
        compiler_params=_cparams(("arbitrary", "arbitrary")))(qrot, krot, va)


def _attn_a_bwd(qrot, krot, va, oa, doa, lse, tq, tk):
    s = qrot.shape[0]
    n_kv = s // tk
    gw = Q_PER_KV * HEAD_DIM_A

    def body(q_ref, do_ref, o_ref, lse_ref, k_ref, v_ref, dq_ref, dk_ref, dv_ref):
        @pl.when(pl.program_id(1) == 0)
        def _():
            dk_ref[...] = jnp.zeros_like(dk_ref)
            dv_ref[...] = jnp.zeros_like(dv_ref)

        def stack(ref):
            return jnp.concatenate([ref[:, 128 * h:128 * h + 128] for h in range(Q_PER_KV)], axis=0)

        q4, do4, o4 = stack(q_ref), stack(do_ref), stack(o_ref)
        q4t, do4t = q4.T, do4.T
        delta = jnp.sum((do4.astype(F32) * o4.astype(F32)).T, axis=0, keepdims=True)
        lse4 = jnp.concatenate([lse_ref[0, h:h + 1, :] for h in range(Q_PER_KV)], axis=1)

        def step(j, dq):
            sl = pl.ds(pl.multiple_of(j * tk, tk), tk)
            kj, vj = k_ref[sl, :], v_ref[sl, :]
            p = jnp.exp2(_dot_nt(kj, q4) - lse4)
            ds = (p * (_dot_nt(vj, do4) - delta)).astype(BF16)
            dk_ref[:, sl] += _dot_nt(q4t, ds)
            dv_ref[:, sl] += _dot_nt(do4t, p.astype(BF16))
            return dq + _dot_tn(kj, ds)

        dq = lax.fori_loop(0, n_kv, step, jnp.zeros((HEAD_DIM_A, Q_PER_KV * tq), F32)).T
        for h in range(Q_PER_KV):
            dq_ref[:, 128 * h:128 * h + 128] = dq[h * tq:(h + 1) * tq]

    qspec = pl.BlockSpec((tq, gw), lambda g, i: (i, g))
    kspec = pl.BlockSpec((s, HEAD_DIM_A), lambda g, i: (0, g))
    ktspec = pl.BlockSpec((HEAD_DIM_A, s), lambda g, i: (g, 0))
    return pl.pallas_call(
        body, name="attn_a_bwd", grid=(N_KV_HEADS_A, s // tq),
        in_specs=[qspec, qspec, qspec, pl.BlockSpec((1, Q_PER_KV, tq), lambda g, i: (g, 0, i)), kspec, kspec],
        out_specs=[qspec, ktspec, ktspec],
        out_shape=[jax.ShapeDtypeStruct((s, QA_W), F32), jax.ShapeDtypeStruct((KA_W, s), F32),
                   jax.ShapeDtypeStruct((KA_W, s), F32)],
        compiler_params=_cparams(("arbitrary", "arbitrary")))(qrot, doa, oa, lse, krot, va)


BAND_QB = 256
BAND_WIN = BAND_QB + 2 * BAND


def _band_specs(s, cb):
    per = cb // BAND
    last = s // BAND - 1
    cur = pl.BlockSpec((cb, GB_W), lambda i: (i, 0))
    prev = pl.BlockSpec((BAND, GB_W), lambda i: (jnp.maximum(i * per - 1, 0), 0))
    nxt = pl.BlockSpec((BAND, GB_W), lambda i: (jnp.minimum(i * per + per, last), 0))
    return cur, prev, nxt


def _window(prev_ref, cur_ref, next_ref):
    return jnp.concatenate([prev_ref[...], cur_ref[...], next_ref[...]], axis=0)


def _band_mask(base, seg_shift, window_rows):
    shape = (BAND_WIN, BAND_QB) if window_rows else (BAND_QB, BAND_WIN)
    a = lax.broadcasted_iota(jnp.int32, shape, 0)
    b = lax.broadcasted_iota(jnp.int32, shape, 1)
    rq, rk = (base - BAND + a, base + b) if window_rows else (base + a, base - BAND + b)
    same_segment = lax.shift_right_arithmetic(rq, jnp.int32(seg_shift)) == lax.shift_right_arithmetic(rk, jnp.int32(seg_shift))
    return (jnp.abs(rk - rq) <= BAND) & same_segment


def _build_bias(bmap_ref, tab_ref, bias_ref):
    bm = bmap_ref[...]
    acc = [jnp.zeros(bm.shape, F32) for _ in range(N_HEADS_PER_DIL)]
    for b in range(N_REL_BUCKETS):
        hit = bm == b
        for h in range(N_HEADS_PER_DIL):
            acc[h] = jnp.where(hit, tab_ref[b, h], acc[h])
    for h in range(N_HEADS_PER_DIL):
        bias_ref[h] = acc[h]


def _head_lane_masks():
    lane = lax.broadcasted_iota(jnp.int32, (1, LANES), 1)
    return [lane < HEAD_DIM_B, lane >= HEAD_DIM_B]


def _seg_shift(s, dil):
    seg = s // dil
    assert seg & (seg - 1) == 0, "segment length must be a power of two"
    return seg.bit_length() - 1


def _band_fwd(dil, qb, kb, vb, bmap, tab, cb):
    s = qb.shape[0]
    shift = _seg_shift(s, dil)

    def body(q_ref, kp_ref, kc_ref, kn_ref, vp_ref, vc_ref, vn_ref, bmap_ref, tab_ref, o_ref, lse_ref, bias_ref):
        @pl.when(pl.program_id(0) == 0)
        def _():
            _build_bias(bmap_ref, tab_ref, bias_ref)

        kw, vw = _window(kp_ref, kc_ref, kn_ref), _window(vp_ref, vc_ref, vn_ref)
        hm = _head_lane_masks()
        for jj in range(cb // BAND_QB):
            r0 = BAND_QB * jj
            mask = _band_mask(pl.program_id(0) * cb + r0, shift, False)
            for hp in range(2):
                ls = slice(LANES * hp, LANES * hp + LANES)
                qh = q_ref[r0:r0 + BAND_QB, ls]
                k3, v3 = kw[r0:r0 + BAND_WIN, ls], vw[r0:r0 + BAND_WIN, ls]
                o_half = jnp.zeros((BAND_QB, LANES), F32)
                lse_half = jnp.zeros((BAND_QB, LANES), F32)
                for hh in range(2):
                    sc = _dot_nt(jnp.where(hm[hh], qh, jnp.zeros_like(qh)), k3) + bias_ref[2 * hp + hh]
                    sc = jnp.where(mask, sc, NEG_INF)
                    m = jnp.max(sc, axis=-1, keepdims=True)
                    e = jnp.exp(sc - m)
                    l = jnp.sum(e, axis=-1, keepdims=True)
                    p = (e * (1.0 / l)).astype(BF16)
                    o_half = o_half + _dot_nn(p, jnp.where(hm[hh], v3, jnp.zeros_like(v3)))
                    lse_half = jnp.where(hm[hh], m + jnp.log(l), lse_half)
                o_ref[r0:r0 + BAND_QB, ls] = o_half
                lse_ref[r0:r0 + BAND_QB, ls] = lse_half

    cur, prev, nxt = _band_specs(s, cb)
    return pl.pallas_call(
        body, name=f"band_fwd_d{dil}", grid=(s // cb,),
        in_specs=[cur, prev, cur, nxt, prev, cur, nxt, _resident(bmap.shape), pl.BlockSpec(memory_space=pltpu.SMEM)],
        out_specs=[cur, cur],
        out_shape=[jax.ShapeDtypeStruct(qb.shape, F32), jax.ShapeDtypeStruct(qb.shape, F32)],
        scratch_shapes=[pltpu.VMEM((N_HEADS_PER_DIL, BAND_QB, BAND_WIN), F32)],
        compiler_params=_cparams(("arbitrary",)))(qb, kb, kb, kb, vb, vb, vb, bmap, tab)


def _band_bwd_q(dil, qb, kb, vb, dob, lse, dd, bmap, tab, cb):
    s = qb.shape[0]
    shift = _seg_shift(s, dil)
    n_steps = s // cb

    def body(q_ref, do_ref, lse_ref, dd_ref, kp_ref, kc_ref, kn_ref, vp_ref, vc_ref, vn_ref, bmap_ref, tab_ref,
             dq_ref, dtab_ref, bias_ref, dsum_ref):
        @pl.when(pl.program_id(0) == 0)
        def _():
            _build_bias(bmap_ref, tab_ref, bias_ref)
            dsum_ref[...] = jnp.zeros_like(dsum_ref)

        kw, vw = _window(kp_ref, kc_ref, kn_ref), _window(vp_ref, vc_ref, vn_ref)
        hm = _head_lane_masks()
        for jj in range(cb // BAND_QB):
            r0 = BAND_QB * jj
            mask = _band_mask(pl.program_id(0) * cb + r0, shift, False)
            for hp in range(2):
                ls = slice(LANES * hp, LANES * hp + LANES)
                qh, doh = q_ref[r0:r0 + BAND_QB, ls], do_ref[r0:r0 + BAND_QB, ls]
                k3, v3 = kw[r0:r0 + BAND_WIN, ls], vw[r0:r0 + BAND_WIN, ls]
                dq_half = jnp.zeros((BAND_QB, LANES), F32)
                for hh in range(2):
                    h = 2 * hp + hh
                    col = LANES * hp + HEAD_DIM_B * hh
                    sc = _dot_nt(jnp.where(hm[hh], qh, jnp.zeros_like(qh)), k3) + bias_ref[h]
                    sc = jnp.where(mask, sc, NEG_INF)
                    p = jnp.exp(sc - lse_ref[r0:r0 + BAND_QB, col:col + 1])
                    dp = _dot_nt(jnp.where(hm[hh], doh, jnp.zeros_like(doh)), v3)
                    ds = p * (dp - dd_ref[r0:r0 + BAND_QB, col:col + 1])
                    dsum_ref[h] += ds
                    dq_half = dq_half + _dot_nn(ds.astype(BF16), jnp.where(hm[hh], k3, jnp.zeros_like(k3)))
                dq_ref[r0:r0 + BAND_QB, ls] = dq_half

        @pl.when(pl.program_id(0) == n_steps - 1)
        def _():
            bm = bmap_ref[...]
            lane = lax.broadcasted_iota(jnp.int32, (1, LANES), 1)
            for b in range(N_REL_BUCKETS):
                hit = bm == b
                row = jnp.zeros((1, LANES), F32)
                for h in range(N_HEADS_PER_DIL):
                    row = jnp.where(lane == h, jnp.sum(jnp.where(hit, dsum_ref[h], 0.0)), row)
                dtab_ref[b:b + 1, :] = row

    cur, prev, nxt = _band_specs(s, cb)
    return pl.pallas_call(
        body, name=f"band_bwd_q_d{dil}", grid=(n_steps,),
        in_specs=[cur, cur, cur, cur, prev, cur, nxt, prev, cur, nxt, _resident(bmap.shape),
                  pl.BlockSpec(memory_space=pltpu.SMEM)],
        out_specs=[cur, _acc_spec((N_REL_BUCKETS, LANES))],
        out_shape=[jax.ShapeDtypeStruct(qb.shape, F32), jax.ShapeDtypeStruct((N_REL_BUCKETS, LANES), F32)],
        scratch_shapes=[pltpu.VMEM((N_HEADS_PER_DIL, BAND_QB, BAND_WIN), F32),
                        pltpu.VMEM((N_HEADS_PER_DIL, BAND_QB, BAND_WIN), F32)],
        compiler_params=_cparams(("arbitrary",)))(qb, dob, lse, dd, kb, kb, kb, vb, vb, vb, bmap, tab)


def _band_bwd_kv(dil, qb, kb, vb, dob, lse, dd, bmap_t, tab, cb):
    s = qb.shape[0]
    shift = _seg_shift(s, dil)

    def body(k_ref, v_ref, qp_ref, qc_ref, qn_ref, dp_ref, dc_ref, dn_ref, lp_ref, lc_ref, ln_ref,
             ep_ref, ec_ref, en_ref, bmap_ref, tab_ref, dk_ref, dv_ref, bias_ref):
        @pl.when(pl.program_id(0) == 0)
        def _():
            _build_bias(bmap_ref, tab_ref, bias_ref)

        qw, dow = _window(qp_ref, qc_ref, qn_ref), _window(dp_ref, dc_ref, dn_ref)
        lw, ew = _window(lp_ref, lc_ref, ln_ref), _window(ep_ref, ec_ref, en_ref)
        hm = _head_lane_masks()
        for jj in range(cb // BAND_QB):
            r0 = BAND_QB * jj
            mask = _band_mask(pl.program_id(0) * cb + r0, shift, True)
            for hp in range(2):
                ls = slice(LANES * hp, LANES * hp + LANES)
                kh, vh = k_ref[r0:r0 + BAND_QB, ls], v_ref[r0:r0 + BAND_QB, ls]
                q3, do3 = qw[r0:r0 + BAND_WIN, ls], dow[r0:r0 + BAND_WIN, ls]
                dk_half = jnp.zeros((BAND_QB, LANES), F32)
                dv_half = jnp.zeros((BAND_QB, LANES), F32)
                for hh in range(2):
                    col = LANES * hp + HEAD_DIM_B * hh
                    q3m = jnp.where(hm[hh], q3, jnp.zeros_like(q3))
                    do3m = jnp.where(hm[hh], do3, jnp.zeros_like(do3))
                    sc = _dot_nt(q3m, kh) + bias_ref[2 * hp + hh]
                    sc = jnp.where(mask, sc, NEG_INF)
                    p = jnp.exp(sc - lw[r0:r0 + BAND_WIN, col:col + 1])
                    ds = p * (_dot_nt(do3m, vh) - ew[r0:r0 + BAND_WIN, col:col + 1])
                    dk_half = dk_half + _dot_tn(ds.astype(BF16), q3m)
                    dv_half = dv_half + _dot_tn(p.astype(BF16), do3m)
                dk_ref[r0:r0 + BAND_QB, ls] = dk_half
                dv_ref[r0:r0 + BAND_QB, ls] = dv_half

    cur, prev, nxt = _band_specs(s, cb)
    win = [prev, cur, nxt]
    return pl.pallas_call(
        body, name=f"band_bwd_kv_d{dil}", grid=(s // cb,),
        in_specs=[cur, cur] + win * 4 + [_resident(bmap_t.shape), pl.BlockSpec(memory_space=pltpu.SMEM)],
        out_specs=[cur, cur],
        out_shape=[jax.ShapeDtypeStruct(qb.shape, F32), jax.ShapeDtypeStruct(qb.shape, F32)],
        scratch_shapes=[pltpu.VMEM((N_HEADS_PER_DIL, BAND_WIN, BAND_QB), F32)],
        compiler_params=_cparams(("arbitrary",)))(
        kb, vb, qb, qb, qb, dob, dob, dob, lse, lse, lse, dd, dd, dd, bmap_t, tab)


def _t5_bucket(rel):
    nb = N_REL_BUCKETS // 2
    ret = (rel > 0).astype(np.int32) * nb
    n = np.abs(rel)
    max_exact = nb // 2
    large = max_exact + (np.log(np.maximum(n, 1) / max_exact) / math.log(REL_MAX_DIST / max_exact)
                         * (nb - max_exact)).astype(np.int32)
    large = np.minimum(large, nb - 1)
    return ret + np.where(n < max_exact, n, large).astype(np.int32)


def _bucket_maps(dil):
    off_qk = np.arange(BAND_WIN)[None, :] - BAND - np.arange(BAND_QB)[:, None]
    off_kq = np.arange(BAND_QB)[None, :] + BAND - np.arange(BAND_WIN)[:, None]
    return [np.where(np.abs(off) <= BAND, _t5_bucket(off * dil), -1).astype(np.int32) for off in (off_qk, off_kq)]


def _seg_sum(v):
    lane = lax.broadcasted_iota(jnp.int32, (1, v.shape[1]), 1)
    out = jnp.zeros_like(v)
    for h in range(v.shape[1] // HEAD_DIM_B):
        m = (lane >= HEAD_DIM_B * h) & (lane < HEAD_DIM_B * (h + 1))
        out = jnp.where(m, jnp.sum(jnp.where(m, v, 0.0), axis=-1, keepdims=True), out)
    return out


def _mix_out(x, oa, og, lg, ga, gb, w_oa, w_ob_t, w_o, tb):
    s, d = x.shape

    def body(x_ref, oa_ref, og0_ref, og1_ref, og2_ref, lg0_ref, lg1_ref, lg2_ref, ga_ref, gb_ref,
             woa_ref, wob_ref, wo_ref, x2_ref, ob_ref, lse0_ref, lse1_ref, lse2_ref, ya_ref, yb_ref, u_ref, scr_ref):
        og_refs, lg_refs = (og0_ref, og1_ref, og2_ref), (lg0_ref, lg1_ref, lg2_ref)
        l0, l1, l2 = [_from_residues(lg_refs[g], scr_ref, dil) for g, dil in enumerate(DILATIONS)]
        lmax = jnp.maximum(jnp.maximum(l0, l1), l2)
        w0, w1, w2 = jnp.exp(l0 - lmax), jnp.exp(l1 - lmax), jnp.exp(l2 - lmax)
        den = w0 + w1 + w2
        o0, o1, o2 = [_from_residues(og_refs[g], scr_ref, dil) for g, dil in enumerate(DILATIONS)]
        ob = ((w0 * o0 + w1 * o1 + w2 * o2) / den).astype(BF16)
        ob_ref[...] = ob
        lse = lmax + jnp.log(den)
        for g, (dil, ref) in enumerate(zip(DILATIONS, (lse0_ref, lse1_ref, lse2_ref))):
            _to_residues(lse, ref, scr_ref, dil, F32)
        ya = _dot_nn(oa_ref[...], woa_ref[...])
        yb = _dot_nt(ob, wob_ref[...])
        ya_ref[...] = ya.astype(BF16)
        yb_ref[...] = yb.astype(BF16)
        u = (ga_ref[...] * ya + gb_ref[...] * yb).astype(BF16)
        u_ref[...] = u
        x2_ref[...] = x_ref[...] + _dot_nn(u, wo_ref[...])

    sd = jax.ShapeDtypeStruct
    res = list(pl.pallas_call(
        body, name="mix_out", grid=(s // tb,),
        in_specs=[_rows(tb, d), _rows(tb, QA_W)] + _dil_specs(tb) * 2 + [
            _rows(tb, d), _rows(tb, d), _resident(w_oa.shape), _resident(w_ob_t.shape), _resident(w_o.shape)],
        out_specs=[_rows(tb, d), _rows(tb, GB_W)] + _dil_specs(tb) + [_rows(tb, d), _rows(tb, d), _rows(tb, d)],
        out_shape=[sd((s, d), F32), sd((s, GB_W), BF16)] + _dil_shapes(s, F32) + [
            sd((s, d), BF16), sd((s, d), BF16), sd((s, d), BF16)],
        scratch_shapes=[pltpu.VMEM((2, tb, LANES), F32)],
        compiler_params=_cparams(("arbitrary",)))(x, oa, *og, *lg, ga, gb, w_oa, w_ob_t, w_o))
    return res[:2] + [res[2:5]] + res[5:]


def _mlp_fwd(x2, w1_t, w2, g_mlp, tb, tc):
    s, d = x2.shape
    dff = w1_t.shape[0]

    def body(x_ref, w1_ref, w2_ref, g_ref, x3_ref, r_ref, h_ref):
        xv = x_ref[...]
        hb = (xv * _rstd(xv) * g_ref[...]).astype(BF16)
        h_ref[...] = hb
        x3_ref[...] = xv
        for c in range(dff // tc):
            sl = slice(tc * c, tc * c + tc)
            r = jnp.maximum(_dot_nt(hb, w1_ref[sl, :]), 0.0)
            r_ref[:, sl] = r.astype(BF16)
            x3_ref[...] += _dot_nn((r * r).astype(BF16), w2_ref[sl, :])

    sd = jax.ShapeDtypeStruct
    return pl.pallas_call(
        body, name="mlp_fwd", grid=(s // tb,),
        in_specs=[_rows(tb, d), _resident(w1_t.shape), _resident(w2.shape), _resident(g_mlp.shape)],
        out_specs=[_rows(tb, d), _rows(tb, dff), _rows(tb, d)],
        out_shape=[sd((s, d), F32), sd((s, dff), BF16), sd((s, d), BF16)],
        compiler_params=_cparams(("arbitrary",)))(x2, w1_t, w2, g_mlp)


def _ple_loss(x3, p, target, w_pg, w_p_t, g_ple, g_fin, tb):
    s, d = x3.shape
    dp = p.shape[1]

    def body(x_ref, p_ref, t_ref, wpg_ref, wp_ref, gple_ref, gfin_ref,
             dx3_ref, h3_ref, dpre_ref, dpe_ref, pb_ref, loss_ref, dgfin_ref, dgple_ref):
        @pl.when(pl.program_id(0) == 0)
        def _():
            loss_ref[...] = jnp.zeros_like(loss_ref)
            dgfin_ref[...] = jnp.zeros_like(dgfin_ref)
            dgple_ref[...] = jnp.zeros_like(dgple_ref)

        x3v = x_ref[...]
        r3 = _rstd(x3v)
        n3 = x3v * r3
        h3 = (n3 * gple_ref[...]).astype(BF16)
        h3_ref[...] = h3
        gp = _sigmoid(_dot_nn(h3, wpg_ref[...]))
        pb = p_ref[...].astype(BF16)
        pb_ref[...] = pb
        pe = _dot_nt(pb, wp_ref[...])
        x4 = x3v + gp * pe
        r4 = _rstd(x4)
        n4 = x4 * r4
        err = n4 * gfin_ref[...] - t_ref[...]
        loss_ref[...] += jnp.sum(0.5 * jnp.mean(err * err, axis=-1, keepdims=True), axis=0, keepdims=True)
        dy = err / d
        dgfin_ref[...] += _colsum(dy * n4)
        dx4 = _rms_bwd(dy, n4, r4, gfin_ref[...])
        dpe_ref[...] = (dx4 * gp).astype(BF16)
        dpre = (dx4 * pe * gp * (1.0 - gp)).astype(BF16)
        dpre_ref[...] = dpre
        dh3 = _dot_nt(dpre, wpg_ref[...])
        dgple_ref[...] += _colsum(dh3 * n3)
        dx3_ref[...] = dx4 + _rms_bwd(dh3, n3, r3, gple_ref[...])

    sd = jax.ShapeDtypeStruct
    return pl.pallas_call(
        body, name="ple_loss", grid=(s // tb,),
        in_specs=[_rows(tb, d), _rows(tb, dp), _rows(tb, d), _resident(w_pg.shape), _resident(w_p_t.shape),
                  _resident(g_ple.shape), _resident(g_fin.shape)],
        out_specs=[_rows(tb, d), _rows(tb, d), _rows(tb, d), _rows(tb, d), _rows(tb, dp),
                   _acc_spec((1, LANES)), _acc_spec((1, d)), _acc_spec((1, d))],
        out_shape=[sd((s, d), F32), sd((s, d), BF16), sd((s, d), BF16), sd((s, d), BF16), sd((s, dp), BF16),
                   sd((1, LANES), F32), sd((1, d), F32), sd((1, d), F32)],
        compiler_params=_cparams(("arbitrary",)))(x3, p, target, w_pg, w_p_t, g_ple, g_fin)


def _mlp_bwd(dx3, x2, r, w1_t, w2, g_mlp, tb, tc):
    s, d = x2.shape
    dff = w1_t.shape[0]

    def body(dx3_ref, x_ref, r_ref, w1_ref, w2_ref, g_ref, dx2_ref, df_ref, dg_ref, dh_ref):
        @pl.when(pl.program_id(0) == 0)
        def _():
            dg_ref[...] = jnp.zeros_like(dg_ref)

        dx3v = dx3_ref[...]
        dx3b = dx3v.astype(BF16)
        dh_ref[...] = jnp.zeros_like(dh_ref)
        for c in range(dff // tc):
            sl = slice(tc * c, tc * c + tc)
            df = (_dot_nt(dx3b, w2_ref[sl, :]) * (2.0 * r_ref[:, sl].astype(F32))).astype(BF16)
            df_ref[:, sl] = df
            dh_ref[...] += _dot_nn(df, w1_ref[sl, :])
        xv = x_ref[...]
        r2 = _rstd(xv)
        n2 = xv * r2
        dh = dh_ref[...]
        dg_ref[...] += _colsum(dh * n2)
        dx2_ref[...] = dx3v + _rms_bwd(dh, n2, r2, g_ref[...])

    sd = jax.ShapeDtypeStruct
    return pl.pallas_call(
        body, name="mlp_bwd", grid=(s // tb,),
        in_specs=[_rows(tb, d), _rows(tb, d), _rows(tb, dff), _resident(w1_t.shape), _resident(w2.shape),
                  _resident(g_mlp.shape)],
        out_specs=[_rows(tb, d), _rows(tb, dff), _acc_spec((1, d))],
        out_shape=[sd((s, d), F32), sd((s, dff), BF16), sd((1, d), F32)],
        scratch_shapes=[pltpu.VMEM((tb, d), F32)],
        compiler_params=_cparams(("arbitrary",)))(dx3, x2, r, w1_t, w2, g_mlp)


def _mix_out_bwd(dx2, ya, yb, ga, gb, ob, w_oa, w_ob_t, w_o, tb):
    s, d = dx2.shape

    def body(dx_ref, ya_ref, yb_ref, ga_ref, gb_ref, ob_ref, woa_ref, wob_ref, wo_ref,
             doa_ref, dob0_ref, dob1_ref, dob2_ref, dd0_ref, dd1_ref, dd2_ref, dga_ref, dgb_ref, dya_ref, dyb_ref,
             dbg_ref, scr_ref):
        @pl.when(pl.program_id(0) == 0)
        def _():
            dbg_ref[...] = jnp.zeros_like(dbg_ref)

        du = _dot_nt(dx_ref[...].astype(BF16), wo_ref[...])
        gav, gbv = ga_ref[...], gb_ref[...]
        dya = (du * gav).astype(BF16)
        dyb = (du * gbv).astype(BF16)
        dya_ref[...] = dya
        dyb_ref[...] = dyb
        dga = du * ya_ref[...].astype(F32) * gav * (1.0 - gav)
        dgb = du * yb_ref[...].astype(F32) * gbv * (1.0 - gbv)
        dga_ref[...] = dga.astype(BF16)
        dgb_ref[...] = dgb.astype(BF16)
        dbg_ref[:, 0:d] += _colsum(dga)
        dbg_ref[:, d:2 * d] += _colsum(dgb)
        doa_ref[...] = _dot_nt(dya, woa_ref[...]).astype(BF16)
        dob = _dot_nn(dyb, wob_ref[...])
        dd = _seg_sum(dob * ob_ref[...].astype(F32))
        for dil, dob_ref, dd_ref in zip(DILATIONS, (dob0_ref, dob1_ref, dob2_ref), (dd0_ref, dd1_ref, dd2_ref)):
            _to_residues(dob, dob_ref, scr_ref, dil, BF16)
            _to_residues(dd, dd_ref, scr_ref, dil, F32)

    sd = jax.ShapeDtypeStruct
    res = list(pl.pallas_call(
        body, name="mix_out_bwd", grid=(s // tb,),
        in_specs=[_rows(tb, d)] * 5 + [_rows(tb, GB_W), _resident(w_oa.shape), _resident(w_ob_t.shape),
                                       _resident(w_o.shape)],
        out_specs=[_rows(tb, QA_W)] + _dil_specs(tb) * 2 + [_rows(tb, d), _rows(tb, d), _rows(tb, d),
                                                           _rows(tb, d), _acc_spec((1, 2 * d))],
        out_shape=[sd((s, QA_W), BF16)] + _dil_shapes(s, BF16) + _dil_shapes(s, F32) + [
            sd((s, d), BF16), sd((s, d), BF16), sd((s, d), BF16), sd((s, d), BF16), sd((1, 2 * d), F32)],
        scratch_shapes=[pltpu.VMEM((2, tb, LANES), F32)],
        compiler_params=_cparams(("arbitrary",)))(dx2, ya, yb, ga, gb, ob, w_oa, w_ob_t, w_o))
    return res[:1] + [res[1:4], res[4:7]] + res[7:]


def _in_proj_bwd(dx2, x, dqrot, dkrot, dva, qraw, kraw, tabs, dqb, dkb, dvb, dga, dgb, w_in_t, g_mix, q_g, k_g, tb):
    s, d = x.shape
    din = w_in_t.shape[0]
    q_scale = HEAD_DIM_A ** -0.5
    b_scale = HEAD_DIM_B ** -0.5
    tc = 256

    def body(dx2_ref, x_ref, dq_ref, dk_ref, dv_ref, qraw_ref, kraw_ref, c_ref, s1_ref, s2_ref, *rest):
        dqb_refs, dkb_refs, dvb_refs = rest[0:3], rest[3:6], rest[6:9]
        (dga_ref, dgb_ref, w_ref, gmix_ref, qg_ref, kg_ref,
         dx_ref, dz_ref, dgmix_ref, dqg_ref, dkg_ref, dh_ref, scr_ref) = rest[9:]

        @pl.when(pl.program_id(0) == 0)
        def _():
            dgmix_ref[...] = jnp.zeros_like(dgmix_ref)
            dqg_ref[...] = jnp.zeros_like(dqg_ref)
            dkg_ref[...] = jnp.zeros_like(dkg_ref)

        cos, s1, s2 = c_ref[...], s1_ref[...], s2_ref[...]

        def head_bwd(drot, z, g_ref, acc_ref):
            dn = _rope_bwd(drot, cos, s1, s2)
            rr = _rstd(z)
            nn = z * rr
            acc_ref[...] += _colsum(dn * nn)
            return _rms_bwd(dn, nn, rr, g_ref[...])

        for h in range(N_Q_HEADS_A):
            sl = slice(128 * h, 128 * h + 128)
            dz_ref[:, OFF_QA + 128 * h:OFF_QA + 128 * h + 128] = head_bwd(
                dq_ref[:, sl] * q_scale, qraw_ref[:, sl], qg_ref, dqg_ref).astype(BF16)
        for h in range(N_KV_HEADS_A):
            sl = slice(128 * h, 128 * h + 128)
            dz_ref[:, OFF_KA + 128 * h:OFF_KA + 128 * h + 128] = head_bwd(
                dk_ref[sl, :].T * LN_2, kraw_ref[:, sl], kg_ref, dkg_ref).astype(BF16)
        dz_ref[:, OFF_VA:OFF_VA + KA_W] = dv_ref[...].T.astype(BF16)
        for g, dil in enumerate(DILATIONS):
            dz_ref[:, OFF_QB + GB_W * g:OFF_QB + GB_W * (g + 1)] = (
                _from_residues(dqb_refs[g], scr_ref, dil) * b_scale).astype(BF16)
            dz_ref[:, OFF_KB + GB_W * g:OFF_KB + GB_W * (g + 1)] = _from_residues(dkb_refs[g], scr_ref, dil).astype(BF16)
            dz_ref[:, OFF_VB + GB_W * g:OFF_VB + GB_W * (g + 1)] = _from_residues(dvb_refs[g], scr_ref, dil).astype(BF16)
        dz_ref[:, OFF_GA:OFF_GA + d] = dga_ref[...]
        dz_ref[:, OFF_GA + d:OFF_GA + 2 * d] = dgb_ref[...]
        dh_ref[...] = jnp.zeros_like(dh_ref)
        for c in range(din // tc):
            sl = slice(tc * c, tc * c + tc)
            dh_ref[...] += _dot_nn(dz_ref[:, sl], w_ref[sl, :])
        xv = x_ref[...]
        r1 = _rstd(xv)
        n1 = xv * r1
        dh = dh_ref[...]
        dgmix_ref[...] += _colsum(dh * n1)
        dx_ref[...] = dx2_ref[...] + _rms_bwd(dh, n1, r1, gmix_ref[...])

    sd = jax.ShapeDtypeStruct
    return pl.pallas_call(
        body, name="in_proj_bwd", grid=(s // tb,),
        in_specs=[_rows(tb, d), _rows(tb, d), _rows(tb, QA_W), pl.BlockSpec((KA_W, tb), lambda i: (0, i)),
                  pl.BlockSpec((KA_W, tb), lambda i: (0, i)), _rows(tb, QA_W),
                  _rows(tb, KA_W), _rows(tb, LANES), _rows(tb, LANES), _rows(tb, LANES),
                  ] + _dil_specs(tb) * 3 + [_rows(tb, d), _rows(tb, d),
                  _resident(w_in_t.shape), _resident(g_mix.shape), _resident(q_g.shape), _resident(k_g.shape)],
        out_specs=[_rows(tb, d), _rows(tb, din), _acc_spec((1, d)), _acc_spec((1, HEAD_DIM_A)),
                   _acc_spec((1, HEAD_DIM_A))],
        out_shape=[sd((s, d), F32), sd((s, din), BF16), sd((1, d), F32), sd((1, HEAD_DIM_A), F32),
                   sd((1, HEAD_DIM_A), F32)],
        scratch_shapes=[pltpu.VMEM((tb, d), F32), pltpu.VMEM((2, tb, LANES), F32)],
        compiler_params=_cparams(("arbitrary",)))(
        dx2, x, dqrot, dkrot, dva, qraw, kraw, *tabs, *dqb, *dkb, *dvb, dga, dgb, w_in_t, g_mix, q_g, k_g)


def _identity(v):
    return v


def _to_bf16(v):
    return v.astype(BF16)


def _square_bf16(v):
    vf = v.astype(F32)
    return (vf * vf).astype(BF16)


def _weight_grad(name, a, b, ti, tj, tk, a_fn=_identity, b_fn=_identity):
    t, m = a.shape
    n = b.shape[1]
    n_k = t // tk

    def body(a_ref, b_ref, o_ref, acc_ref):
        k = pl.program_id(2)

        @pl.when(k == 0)
        def _():
            acc_ref[...] = jnp.zeros_like(acc_ref)

        acc_ref[...] += _dot_tn(a_fn(a_ref[...]), b_fn(b_ref[...]))

        @pl.when(k == n_k - 1)
        def _():
            o_ref[...] = acc_ref[...].astype(BF16)

    return pl.pallas_call(
        body, name=name, grid=(m // ti, n // tj, n_k),
        in_specs=[pl.BlockSpec((tk, ti), lambda i, j, k: (k, i)), pl.BlockSpec((tk, tj), lambda i, j, k: (k, j))],
        out_specs=pl.BlockSpec((ti, tj), lambda i, j, k: (i, j)),
        out_shape=jax.ShapeDtypeStruct((m, n), BF16),
        scratch_shapes=[pltpu.VMEM((ti, tj), F32)],
        compiler_params=_cparams(("arbitrary", "arbitrary", "arbitrary")))(a, b)


def _sum_slots(name, recv):
    _, n, k = recv.shape
    tc = min(k, 256)

    def body(r_ref, o_ref):
        acc = r_ref[0].astype(F32)
        for i in range(1, N_DEV):
            acc = acc + r_ref[i].astype(F32)
        o_ref[...] = acc

    return pl.pallas_call(
        body, name=name, grid=(k // tc,),
        in_specs=[pl.BlockSpec((N_DEV, n, tc), lambda j: (0, 0, j))],
        out_specs=pl.BlockSpec((n, tc), lambda j: (0, j)),
        out_shape=jax.ShapeDtypeStruct((n, k), F32),
        compiler_params=_cparams(("arbitrary",)))(recv)


def _adamw_math(w, g, m, v):
    m = ADAM_B1 * m + (1.0 - ADAM_B1) * g
    v = ADAM_B2 * v + (1.0 - ADAM_B2) * (g * g)
    m_hat = m / (1.0 - ADAM_B1 ** ADAM_STEP)
    v_hat = v / (1.0 - ADAM_B2 ** ADAM_STEP)
    delta = -ADAM_LR * (m_hat / (jnp.sqrt(v_hat) + ADAM_EPS) + ADAM_WD * w)
    return delta, m, v


def _adamw(name, w, g, m, v):
    r, c = w.shape
    tr = min(r, 256)

    def body(w_ref, g_ref, m_ref, v_ref, d_ref, mo_ref, vo_ref):
        d_ref[...], mo_ref[...], vo_ref[...] = _adamw_math(w_ref[...], g_ref[...], m_ref[...], v_ref[...])

    spec = pl.BlockSpec((tr, c), lambda i: (i, 0))
    return pl.pallas_call(
        body, name=name, grid=(r // tr,), in_specs=[spec] * 4, out_specs=[spec] * 3,
        out_shape=[jax.ShapeDtypeStruct((r, c), F32)] * 3,
        compiler_params=_cparams(("arbitrary",)))(w, g, m, v)


def _small_update(parts, w, m, v):
    def body(p_ref, w_ref, m_ref, v_ref, g_ref, d_ref, mo_ref, vo_ref):
        g = p_ref[0]
        for i in range(1, N_DEV):
            g = g + p_ref[i]
        g_ref[...] = g
        d_ref[...], mo_ref[...], vo_ref[...] = _adamw_math(w_ref[...], g, m_ref[...], v_ref[...])

    return pl.pallas_call(body, name="small_update", out_shape=[jax.ShapeDtypeStruct(w.shape, F32)] * 4)(
        parts, w, m, v)


def _pack_rows(vectors, n_rows):
    flat = jnp.concatenate([v.reshape(-1).astype(F32) for v in vectors])
    flat = jnp.pad(flat, (0, n_rows * LANES - flat.shape[0]))
    return flat.reshape(n_rows, LANES)


def _pick_tile(n, prefs):
    for t in prefs:
        if n % t == 0:
            return t
    return n


def kernel(x, p, norm_mix_g, w_in, b_gate, q_norm_g, k_norm_g, rel_bias, w_out_a, w_out_b, w_out, norm_mlp_g, w_ff1, w_ff2, norm_ple_g, w_ple_gate, w_ple, final_norm_g, loss_target, m_norm_mix_g, m_w_in, m_b_gate, m_q_norm_g, m_k_norm_g, m_rel_bias, m_w_out_a, m_w_out_b, m_w_out, m_norm_mlp_g, m_w_ff1, m_w_ff2, m_norm_ple_g, m_w_ple_gate, m_w_ple, m_final_norm_g, v_norm_mix_g, v_w_in, v_b_gate, v_q_norm_g, v_k_norm_g, v_rel_bias, v_w_out_a, v_w_out_b, v_w_out, v_norm_mlp_g, v_w_ff1, v_w_ff2, v_norm_ple_g, v_w_ple_gate, v_w_ple, v_final_norm_g):
    s, d = x.shape[1], x.shape[2]
    xs, ps, ts = x[0], p[0, 0], loss_target[0]
    tb = _pick_tile(s, (512, 256))
    tq = _pick_tile(s, (256,))
    tk = _pick_tile(s, (1024, 512))
    cb = _pick_tile(s, (512,))
    fin_g = final_norm_g.reshape(1, d)

    col_sharded = {"w_in": w_in[0], "w_out_b": w_out_b[0], "w_ff1": w_ff1[0], "w_ple": w_ple[0]}
    row_sharded = {"w_out_a": w_out_a[0], "w_out": w_out[0], "w_ff2": w_ff2[0], "w_ple_gate": w_ple_gate[0]}
    order = ["w_in", "w_out_a", "w_out_b", "w_out", "w_ff1", "w_ff2", "w_ple_gate", "w_ple"]
    shards = [(col_sharded[n].T if n in col_sharded else row_sharded[n]).astype(BF16) for n in order]
    w_in_t, w_oa, w_ob_t, w_o, w_ff1_t, w_ff2_f, w_pg, w_p_t = _all_gather(shards)

    tabs = _rope_tables(s)
    (h1, qraw, kraw, qrot, krot, va, qb, kb, vb, ga, gb) = _in_proj(
        xs, tabs, w_in_t, norm_mix_g, b_gate, q_norm_g, k_norm_g, tb)
    oa, lse_a = _attn_a_fwd(qrot, krot, va, tq, tk)
    flat = lambda arrs: [a.reshape(s, GB_W) for a in arrs]
    split = lambda arrs: [a.reshape(dil, s // dil, GB_W) for a, dil in zip(arrs, DILATIONS)]
    qb_r, kb_r, vb_r = flat(qb), flat(kb), flat(vb)
    bmaps = [[jnp.asarray(m) for m in _bucket_maps(dil)] for dil in DILATIONS]
    bias_tabs = [rel_bias[:, N_HEADS_PER_DIL * g:N_HEADS_PER_DIL * (g + 1)] for g in range(3)]
    band_out = [_band_fwd(dil, qb_r[g], kb_r[g], vb_r[g], bmaps[g][0], bias_tabs[g], cb)
                for g, dil in enumerate(DILATIONS)]
    og, lg = split([o for o, _ in band_out]), split([l for _, l in band_out])
    x2, ob, lse_b, ya, yb, u = _mix_out(xs, oa, og, lg, ga, gb, w_oa, w_ob_t, w_o, tb)
    tc = _pick_tile(w_ff1_t.shape[0], (512,))
    x3, r_act, h2 = _mlp_fwd(x2, w_ff1_t, w_ff2_f, norm_mlp_g, tb, tc)

    dx3, h3, dpre, dpe, pb, loss_part, dg_fin, dg_ple = _ple_loss(
        x3, ps, ts, w_pg, w_p_t, norm_ple_g, fin_g, tb)
    dx2, df, dg_mlp = _mlp_bwd(dx3, x2, r_act, w_ff1_t, w_ff2_f, norm_mlp_g, tb, tc)
    doa, dob, dd, dga, dgb, dya, dyb, dbg = _mix_out_bwd(dx2, ya, yb, ga, gb, ob, w_oa, w_ob_t, w_o, tb)
    dqrot, dkrot, dva = _attn_a_bwd(qrot, krot, va, oa, doa, lse_a, tq, tk)
    dob_r, lse_r, dd_r = flat(dob), flat(lse_b), flat(dd)
    bwd_q = [_band_bwd_q(dil, qb_r[g], kb_r[g], vb_r[g], dob_r[g], lse_r[g], dd_r[g], bmaps[g][0], bias_tabs[g], cb)
             for g, dil in enumerate(DILATIONS)]
    bwd_kv = [_band_bwd_kv(dil, qb_r[g], kb_r[g], vb_r[g], dob_r[g], lse_r[g], dd_r[g], bmaps[g][1], bias_tabs[g], cb)
              for g, dil in enumerate(DILATIONS)]
    dqb, dkb, dvb = split([r[0] for r in bwd_q]), split([r[0] for r in bwd_kv]), split([r[1] for r in bwd_kv])
    grad_x, dz, dg_mix, dg_q, dg_k = _in_proj_bwd(
        dx2, xs, dqrot, dkrot, dva, qraw, kraw, tabs, dqb, dkb, dvb, dga, dgb, w_in_t, norm_mix_g,
        q_norm_g, k_norm_g, _pick_tile(s, (256,)))
    d_rel = jnp.concatenate([r[1][:, :N_HEADS_PER_DIL] for r in bwd_q], axis=1)

    tkk = _pick_tile(s, (1024, 512))
    din = w_in_t.shape[0]
    ti_in = _pick_tile(din, (din // 2,)) if (din // 2) % LANES == 0 else din
    dff = w_ff1_t.shape[0]
    t1k = lambda n: _pick_tile(n, (1024, 512, 256))
    partials = [
        _weight_grad("grad_w_in", dz, h1, ti_in, t1k(d), tkk),
        _weight_grad("grad_w_out_a", oa, dya, t1k(QA_W), t1k(d), tkk),
        _weight_grad("grad_w_out_b", dyb, ob, t1k(d), GB_W, tkk),
        _weight_grad("grad_w_out", u, dx2, t1k(d), t1k(d), tkk, b_fn=_to_bf16),
        _weight_grad("grad_w_ff1", df, h2, t1k(dff), t1k(d), tkk),
        _weight_grad("grad_w_ff2", r_act, dx3, t1k(dff), t1k(d), tkk, a_fn=_square_bf16, b_fn=_to_bf16),
        _weight_grad("grad_w_ple_gate", h3, dpre, t1k(d), t1k(d), tkk),
        _weight_grad("grad_w_ple", dpe, pb, t1k(d), ps.shape[1], tkk),
    ]
    received = _scatter_blocks(partials)
    sums = [_sum_slots("sum_" + n, r) for n, r in zip(order, received)]

    given_w = dict(w_in=w_in, w_out_a=w_out_a, w_out_b=w_out_b, w_out=w_out, w_ff1=w_ff1, w_ff2=w_ff2,
                   w_ple_gate=w_ple_gate, w_ple=w_ple)
    given_m = dict(w_in=m_w_in, w_out_a=m_w_out_a, w_out_b=m_w_out_b, w_out=m_w_out, w_ff1=m_w_ff1, w_ff2=m_w_ff2,
                   w_ple_gate=m_w_ple_gate, w_ple=m_w_ple)
    given_v = dict(w_in=v_w_in, w_out_a=v_w_out_a, w_out_b=v_w_out_b, w_out=v_w_out, w_ff1=v_w_ff1, w_ff2=v_w_ff2,
                   w_ple_gate=v_w_ple_gate, w_ple=v_w_ple)
    big = {}
    for n, gsum in zip(order, sums):
        g = gsum.T if n in col_sharded else gsum
        delta, new_m, new_v = _adamw("adamw_" + n, given_w[n][0], g, given_m[n][0], given_v[n][0])
        big[n] = tuple(a[None] for a in (g, delta, new_m, new_v))

    small_names = ["norm_mix_g", "b_gate", "q_norm_g", "k_norm_g", "rel_bias", "norm_mlp_g", "norm_ple_g",
                   "final_norm_g"]
    small_w = [norm_mix_g, b_gate, q_norm_g, k_norm_g, rel_bias, norm_mlp_g, norm_ple_g, final_norm_g]
    small_m = [m_norm_mix_g, m_b_gate, m_q_norm_g, m_k_norm_g, m_rel_bias, m_norm_mlp_g, m_norm_ple_g,
               m_final_norm_g]
    small_v = [v_norm_mix_g, v_b_gate, v_q_norm_g, v_k_norm_g, v_rel_bias, v_norm_mlp_g, v_norm_ple_g,
               v_final_norm_g]
    small_g = [dg_mix, dbg, dg_q, dg_k, d_rel, dg_mlp, dg_ple, dg_fin]
    sizes = [int(np.prod(w.shape)) for w in small_w]
    n_rows = -(-(sum(-(-sz // LANES) for sz in sizes) + 1) // 8) * 8
    pad = lambda v: jnp.pad(v.reshape(-1).astype(F32), (0, -v.size % LANES))
    pack = lambda vs, last: _pack_rows([pad(v) for v in vs] + [last], n_rows)
    zero_row = jnp.zeros((LANES,), F32)
    parts = _small_all_gather(pack(small_g, loss_part.reshape(-1) * (jnp.arange(LANES) == 0)))
    g_all, d_all, m_all, v_all = _small_update(parts, pack(small_w, zero_row), pack(small_m, zero_row),
                                               pack(small_v, zero_row))
    small = {}
    row = 0
    for n, w, sz in zip(small_names, small_w, sizes):
        nr = -(-sz // LANES)
        small[n] = tuple(a[row:row + nr].reshape(-1)[:sz].reshape(w.shape) for a in (g_all, d_all, m_all, v_all))
        row += nr
    loss = g_all[row, 0]

    names = ["norm_mix_g", "w_in", "b_gate", "q_norm_g", "k_norm_g", "rel_bias", "w_out_a", "w_out_b", "w_out",
             "norm_mlp_g", "w_ff1", "w_ff2", "norm_ple_g", "w_ple_gate", "w_ple", "final_norm_g"]
    res = {n: (big[n] if n in big else small[n]) for n in names}
    return (loss, grad_x[None], *[res[n][0] for n in names], *[res[n][1] for n in names],
            *[res[n][2] for n in names], *[res[n][3] for n in names])
```

```python
import functools
import math

import numpy as np
import jax
import jax.numpy as jnp
from jax import lax
from jax.experimental import pallas as pl
from jax.experimental.pallas import tpu as pltpu

F32 = jnp.float32
BF16 = jnp.bfloat16
MESH = pl.DeviceIdType.MESH

NORM_EPS = 1e-6
NEG_INF = -1e30
LOG2_E = math.log2(math.e)
LN_2 = math.log(2.0)
GRID_W = 64
ROPE_THETA = 10000.0
HEAD_DIM_A = 128
N_Q_HEADS_A = 8
N_KV_HEADS_A = 2
Q_PER_KV = N_Q_HEADS_A // N_KV_HEADS_A
HEAD_DIM_B = 64
N_HEADS_PER_DIL = 4
DILATIONS = (1, 4, 16)
BAND = 64
N_REL_BUCKETS = 32
REL_MAX_DIST = 1024
QA_W = N_Q_HEADS_A * HEAD_DIM_A
KA_W = N_KV_HEADS_A * HEAD_DIM_A
GB_W = N_HEADS_PER_DIL * HEAD_DIM_B
QB_W = GB_W * len(DILATIONS)
OFF_QA, OFF_KA, OFF_VA = 0, QA_W, QA_W + KA_W
OFF_QB = QA_W + 2 * KA_W
OFF_KB = OFF_QB + QB_W
OFF_VB = OFF_KB + QB_W
OFF_GA = OFF_VB + QB_W
N_DEV = 8
LANES = 128
VMEM_LIMIT = 56 * 2 ** 20

ADAM_LR, ADAM_B1, ADAM_B2, ADAM_EPS, ADAM_WD, ADAM_STEP = 0.001, 0.9, 0.999, 1e-08, 0.01, 10


def _cparams(sem):
    return pltpu.CompilerParams(dimension_semantics=sem, vmem_limit_bytes=VMEM_LIMIT)


def _resident(shape):
    nd = len(shape)
    return pl.BlockSpec(shape, lambda *_: (0,) * nd, pipeline_mode=pl.Buffered(1))


def _acc_spec(shape):
    nd = len(shape)
    return pl.BlockSpec(shape, lambda *_: (0,) * nd)


def _rows(tb, c):
    return pl.BlockSpec((tb, c), lambda i: (i, 0))


def _dil_shapes(s, dtype):
    return [jax.ShapeDtypeStruct((dil, s // dil, GB_W), dtype) for dil in DILATIONS]


def _dil_specs(tb):
    return [pl.BlockSpec((dil, tb // dil, GB_W), lambda i: (0, i, 0)) for dil in DILATIONS]


def _to_residues(val, out_ref, scr_ref, dil, dtype):
    if dil == 1:
        out_ref[0] = val.astype(dtype)
        return
    n = val.shape[0] // dil
    scr_ref[0] = val[:, :LANES]
    scr_ref[1] = val[:, LANES:]
    for r in range(dil):
        out_ref[r] = jnp.concatenate([scr_ref[0, pl.ds(r, n, stride=dil), :],
                                      scr_ref[1, pl.ds(r, n, stride=dil), :]], axis=1).astype(dtype)


def _from_residues(in_ref, scr_ref, dil):
    if dil == 1:
        return in_ref[0]
    n = in_ref.shape[1]
    for r in range(dil):
        v = in_ref[r]
        scr_ref[0, pl.ds(r, n, stride=dil), :] = v[:, :LANES]
        scr_ref[1, pl.ds(r, n, stride=dil), :] = v[:, LANES:]
    return jnp.concatenate([scr_ref[0], scr_ref[1]], axis=1)


def _dot_nt(a, b):
    return lax.dot_general(a, b, (((1,), (1,)), ((), ())), preferred_element_type=F32)


def _dot_nn(a, b):
    return lax.dot_general(a, b, (((1,), (0,)), ((), ())), preferred_element_type=F32)


def _dot_tn(a, b):
    return lax.dot_general(a, b, (((0,), (0,)), ((), ())), preferred_element_type=F32)


def _rstd(x):
    return lax.rsqrt(jnp.mean(x * x, axis=-1, keepdims=True) + NORM_EPS)


def _rms_bwd(dy, n, r, g):
    dn = dy * g
    return r * (dn - n * jnp.mean(dn * n, axis=-1, keepdims=True))


def _colsum(v):
    return jnp.sum(v, axis=0, keepdims=True)


def _sigmoid(v):
    return 1.0 / (1.0 + jnp.exp(-v))


def _rope_fwd(n, c, s1, s2):
    return n * c + pltpu.roll(n, 32, 1) * s1 + pltpu.roll(n, 96, 1) * s2


def _rope_bwd(d, c, s1, s2):
    return d * c + pltpu.roll(d * s1, 96, 1) + pltpu.roll(d * s2, 32, 1)


def _rope_tables(s):
    half = HEAD_DIM_A // 2
    inv = jnp.power(ROPE_THETA, -jnp.arange(0, half, 2, dtype=F32) / half)
    t = jnp.arange(s, dtype=jnp.int32)
    ang_r = (t // GRID_W).astype(F32)[:, None] * inv[None, :]
    ang_c = (t % GRID_W).astype(F32)[:, None] * inv[None, :]
    cr, sr, cc, sc = jnp.cos(ang_r), jnp.sin(ang_r), jnp.cos(ang_c), jnp.sin(ang_c)
    z = jnp.zeros_like(sr)
    cos = jnp.concatenate([cr, cr, cc, cc], axis=1)
    s1 = jnp.concatenate([z, sr, z, sc], axis=1)
    s2 = jnp.concatenate([-sr, z, -sc, z], axis=1)
    return cos, s1, s2


def _my_place():
    return lax.axis_index("x"), lax.axis_index("y"), lax.axis_index("c")


def _all_gather(shards):
    nw = len(shards)

    def body(*refs):
        ins, outs = refs[:nw], refs[nw:2 * nw]
        send_sems, recv_sems, local_sems = refs[2 * nw:]
        x, y, c = _my_place()
        me, sibling = (x, y, c), (x, y, 1 - c)
        chips = [(1 - x, y), (x, 1 - y), (1 - x, 1 - y)]

        def rows(w, px, py, pc):
            n = ins[w].shape[0]
            return outs[w].at[pl.ds(pl.multiple_of((4 * px + 2 * py + pc) * n, 16), n), :]

        def copy(w, k, block, to, src=None):
            return pltpu.make_async_remote_copy(
                src_ref=rows(w, *block) if src is None else src, dst_ref=rows(w, *block),
                send_sem=send_sems.at[w, k], recv_sem=recv_sems.at[w, k], device_id=to, device_id_type=MESH)

        mine = [pltpu.make_async_copy(ins[w], rows(w, *me), local_sems.at[w]) for w in range(nw)]
        for cp in mine:
            cp.start()
        first = []
        for w in range(nw):
            first.append(copy(w, 0, me, sibling, src=ins[w]))
            first += [copy(w, 1 + j, me, (*chip, c), src=ins[w]) for j, chip in enumerate(chips)]
        for cp in first:
            cp.start()
        passed = []
        for j, chip in enumerate(chips):
            for w in range(nw):
                copy(w, 1 + j, (*chip, c), me).wait_recv()
                fwd = copy(w, 4 + j, (*chip, c), sibling)
                fwd.start()
                passed.append(fwd)
        for w in range(nw):
            copy(w, 0, sibling, me).wait_recv()
        for j, chip in enumerate(chips):
            for w in range(nw):
                copy(w, 4 + j, (*chip, 1 - c), me).wait_recv()
        for cp in first + passed:
            cp.wait_send()
        for cp in mine:
            cp.wait()

    any_spec = pl.BlockSpec(memory_space=pl.ANY)
    return pl.pallas_call(
        body, name="weights_all_gather",
        out_shape=[jax.ShapeDtypeStruct((N_DEV * s.shape[0], s.shape[1]), s.dtype) for s in shards],
        in_specs=[any_spec] * nw, out_specs=[any_spec] * nw,
        scratch_shapes=[pltpu.SemaphoreType.DMA((nw, 7)), pltpu.SemaphoreType.DMA((nw, 7)),
                        pltpu.SemaphoreType.DMA((nw,))],
    )(*shards)


_FLIPS = [(fx, fy, fc) for fx in (0, 1) for fy in (0, 1) for fc in (0, 1)][1:]


def _small_all_gather(v):
    def body(v_ref, out_ref, send_sems, recv_sems):
        x, y, c = _my_place()
        my_idx = 4 * x + 2 * y + c
        out_ref[my_idx] = v_ref[...]
        sends = []
        for k, (fx, fy, fc) in enumerate(_FLIPS):
            to = (1 - x if fx else x, 1 - y if fy else y, 1 - c if fc else c)
            sends.append(pltpu.make_async_remote_copy(
                src_ref=v_ref, dst_ref=out_ref.at[my_idx], send_sem=send_sems.at[k], recv_sem=recv_sems.at[k],
                device_id=to, device_id_type=MESH))
        for cp in sends:
            cp.start()
        for k, (fx, fy, fc) in enumerate(_FLIPS):
            frm_idx = 4 * (1 - x if fx else x) + 2 * (1 - y if fy else y) + (1 - c if fc else c)
            pltpu.make_async_remote_copy(
                src_ref=v_ref, dst_ref=out_ref.at[frm_idx], send_sem=send_sems.at[k], recv_sem=recv_sems.at[k],
                device_id=(x, y, c), device_id_type=MESH).wait_recv()
        for cp in sends:
            cp.wait_send()

    vm = pl.BlockSpec(memory_space=pltpu.VMEM)
    return pl.pallas_call(
        body, name="small_all_gather", out_shape=jax.ShapeDtypeStruct((N_DEV,) + v.shape, v.dtype),
        in_specs=[vm], out_specs=vm,
        scratch_shapes=[pltpu.SemaphoreType.DMA((7,)), pltpu.SemaphoreType.DMA((7,))],
    )(v)


_HBM = pl.BlockSpec(memory_space=pltpu.HBM)
_SEM = pl.BlockSpec(memory_space=pltpu.SEMAPHORE)
_ANY = pl.BlockSpec(memory_space=pl.ANY)
_SPLIT_COPY = dict(has_side_effects=pltpu.SideEffectType.DATAFLOW_SIDE_EFFECTING)


def _peer(x, y, c, k):
    fx, fy, fc = _FLIPS[k]
    return (1 - x if fx else x, 1 - y if fy else y, 1 - c if fc else c)


def _in_hbm(a):
    return pltpu.with_memory_space_constraint(a, pltpu.HBM)


def _split_copies(srcs, lands, send_sems, recv_sems, gather, arriving):
    x, y, c = _my_place()
    my_idx = 4 * x + 2 * y + c
    out = []
    for k in range(7):
        to = _peer(x, y, c, k)
        to_idx = 4 * to[0] + 2 * to[1] + to[2]
        for w in range(len(srcs)):
            if gather:
                n = srcs[w].shape[0]
                src = srcs[w]
                dst = lands[w].at[pl.ds(pl.multiple_of((to_idx if arriving else my_idx) * n, 16), n), :]
            else:
                n = lands[w].shape[1]
                src = srcs[w].at[pl.ds(pl.multiple_of(to_idx * n, 16), n), :]
                dst = lands[w].at[k]
            out.append(pltpu.make_async_remote_copy(
                src_ref=src, dst_ref=dst, send_sem=send_sems.at[7 * w + k], recv_sem=recv_sems.at[7 * w + k],
                device_id=to, device_id_type=MESH))
    return out


def _copies_start(name, srcs, lands, after, gather):
    nw = len(srcs)

    def body(*refs):
        send_sems, recv_sems = refs[2 * nw + 1], refs[2 * nw + 2]
        for cp in _split_copies(refs[:nw], refs[nw:2 * nw], send_sems, recv_sems, gather, False):
            cp.start()
        refs[-1][...] = jnp.zeros_like(refs[-1])

    sems = pltpu.SemaphoreType.DMA((7 * nw,))
    thru = [pltpu.HBM(a.shape, a.dtype) for a in list(srcs) + list(lands)]
    res = pl.pallas_call(
        body, name=name, out_shape=(sems, sems, *thru, jax.ShapeDtypeStruct((8, LANES), F32)),
        in_specs=[_HBM] * (2 * nw) + [_ANY], out_specs=(_SEM, _SEM, *[_HBM] * (2 * nw), pl.BlockSpec(memory_space=pltpu.VMEM)),
        input_output_aliases={i: 2 + i for i in range(2 * nw)},
        compiler_params=pltpu.CompilerParams(**_SPLIT_COPY),
    )(*[_in_hbm(a) for a in srcs], *[_in_hbm(a) for a in lands], after)
    return res[0], res[1], list(res[2:2 + nw]), list(res[2 + nw:2 + 2 * nw]), res[-1]


def _copies_wait(name, send_sems, recv_sems, srcs, lands, after, gather):
    nw = len(srcs)

    def body(*refs):
        for cp in _split_copies(refs[:nw], refs[nw:2 * nw], refs[2 * nw], refs[2 * nw + 1], gather, False):
            cp.wait_send()
        for cp in _split_copies(refs[:nw], refs[nw:2 * nw], refs[2 * nw], refs[2 * nw + 1], gather, True):
            cp.wait_recv()

    thru = [pltpu.HBM(a.shape, a.dtype) for a in list(srcs) + list(lands)]
    res = pl.pallas_call(
        body, name=name, out_shape=tuple(thru),
        in_specs=[_HBM] * (2 * nw) + [_SEM, _SEM, _ANY], out_specs=tuple([_HBM] * (2 * nw)),
        input_output_aliases={i: i for i in range(2 * nw)},
        compiler_params=pltpu.CompilerParams(**_SPLIT_COPY),
    )(*srcs, *lands, send_sems, recv_sems, after)
    return list(res[:nw]), list(res[nw:])


def _in_proj(x, tabs, w_in_t, g_mix, b_gate, q_g, k_g, tb, after):
    s, d = x.shape
    n_gate_chunks = d // 256
    q_scale = HEAD_DIM_A ** -0.5 * LOG2_E
    b_scale = HEAD_DIM_B ** -0.5

    def body(x_ref, c_ref, s1_ref, s2_ref, w_ref, gmix_ref, bg_ref, qg_ref, kg_ref, after_ref,
             h1_ref, qraw_ref, kraw_ref, qrot_ref, krot_ref, va_ref, *rest):
        qb_refs, kb_refs, vb_refs = rest[0:3], rest[3:6], rest[6:9]
        ga_ref, gb_ref, scr_ref = rest[9:]
        xv = x_ref[...]
        hb = (xv * _rstd(xv) * gmix_ref[...]).astype(BF16)
        h1_ref[...] = hb
        cos, s1, s2 = c_ref[...], s1_ref[...], s2_ref[...]

        def proj(lo, width):
            return _dot_nt(hb, w_ref[lo:lo + width, :])

        def norm_rope(z, g):
            return _rope_fwd(z * _rstd(z) * g, cos, s1, s2)

        for j in range(QA_W // 256):
            z = proj(OFF_QA + 256 * j, 256)
            qraw_ref[:, 256 * j:256 * j + 256] = z
            for hh in range(2):
                lo = 256 * j + 128 * hh
                qrot_ref[:, lo:lo + 128] = (norm_rope(z[:, 128 * hh:128 * hh + 128], qg_ref[...]) * q_scale).astype(BF16)
        z = proj(OFF_KA, 256)
        kraw_ref[...] = z
        for hh in range(2):
            krot_ref[:, 128 * hh:128 * hh + 128] = norm_rope(z[:, 128 * hh:128 * hh + 128], kg_ref[...]).astype(BF16)
        va_ref[...] = proj(OFF_VA, 256).astype(BF16)
        for g, dil in enumerate(DILATIONS):
            _to_residues(proj(OFF_QB + GB_W * g, GB_W) * b_scale, qb_refs[g], scr_ref, dil, BF16)
            _to_residues(proj(OFF_KB + GB_W * g, GB_W), kb_refs[g], scr_ref, dil, BF16)
            _to_residues(proj(OFF_VB + GB_W * g, GB_W), vb_refs[g], scr_ref, dil, BF16)
        for j in range(n_gate_chunks):
            sl = slice(256 * j, 256 * j + 256)
            ga_ref[:, sl] = _sigmoid(proj(OFF_GA + 256 * j, 256) + bg_ref[:, sl])
            gb_ref[:, sl] = _sigmoid(proj(OFF_GA + d + 256 * j, 256) + bg_ref[:, d + 256 * j:d + 256 * j + 256])

    sd = jax.ShapeDtypeStruct
    outs = [sd((s, d), BF16), sd((s, QA_W), F32), sd((s, KA_W), F32), sd((s, QA_W), BF16), sd((s, KA_W), BF16),
            sd((s, KA_W), BF16)] + _dil_shapes(s, BF16) * 3 + [sd((s, d), F32), sd((s, d), F32)]
    out_specs = [_rows(tb, d), _rows(tb, QA_W), _rows(tb, KA_W), _rows(tb, QA_W), _rows(tb, KA_W), _rows(tb, KA_W)
                 ] + _dil_specs(tb) * 3 + [_rows(tb, d), _rows(tb, d)]
    in_specs = [_rows(tb, d), _rows(tb, LANES), _rows(tb, LANES), _rows(tb, LANES), _resident(w_in_t.shape),
                _resident(g_mix.shape), _resident(b_gate.shape), _resident(q_g.shape), _resident(k_g.shape), _ANY]
    res = list(pl.pallas_call(body, name="in_proj", grid=(s // tb,), in_specs=in_specs, out_specs=out_specs,
                              out_shape=outs, scratch_shapes=[pltpu.VMEM((2, tb, LANES), F32)],
                              compiler_params=_cparams(("arbitrary",)))(
        x, *tabs, w_in_t, g_mix, b_gate, q_g, k_g, after))
    return res[:6] + [res[6:9], res[9:12], res[12:15]] + res[15:]


def _attn_a_fwd(qrot, krot, va, tq, tk):
    s = qrot.shape[0]
    n_kv = s // tk
    gw = Q_PER_KV * HEAD_DIM_A

    def body(q_ref, k_ref, v_ref, o_ref, lse_ref):
        q4 = jnp.concatenate([q_ref[:, 128 * h:128 * h + 128] for h in range(Q_PER_KV)], axis=0)

        def step(j, carry):
            m, l, acc = carry
            sl = pl.ds(pl.multiple_of(j * tk, tk), tk)
            kj, vj = k_ref[sl, :], v_ref[sl, :]
            sc = _dot_nt(kj, q4)
            m_new = jnp.maximum(m, jnp.max(sc, axis=0, keepdims=True))
            p = jnp.exp2(sc - m_new)
            alpha = jnp.exp2(m - m_new)
            l = alpha * l + jnp.sum(p, axis=0, keepdims=True)
            acc = alpha * acc + _dot_tn(vj, p.astype(BF16))
            return m_new, l, acc

        rows = Q_PER_KV * tq
        m, l, acc = lax.fori_loop(0, n_kv, step, (jnp.full((1, rows), NEG_INF, F32), jnp.zeros((1, rows), F32),
                                                  jnp.zeros((HEAD_DIM_A, rows), F32)))
        o = (acc / l).T
        lse = m + jnp.log2(l)
        for h in range(Q_PER_KV):
            o_ref[:, 128 * h:128 * h + 128] = o[h * tq:(h + 1) * tq].astype(BF16)
            lse_ref[0, h:h + 1, :] = lse[:, h * tq:(h + 1) * tq]

    return pl.pallas_call(
        body, name="attn_a_fwd", grid=(N_KV_HEADS_A, s // tq),
        in_specs=[pl.BlockSpec((tq, gw), lambda g, i: (i, g)),
                  pl.BlockSpec((s, HEAD_DIM_A), lambda g, i: (0, g)),
                  pl.BlockSpec((s, HEAD_DIM_A), lambda g, i: (0, g))],
        out_specs=[pl.BlockSpec((tq, gw), lambda g, i: (i, g)),
                   pl.BlockSpec((1, Q_PER_KV, tq), lambda g, i: (g, 0, i))],
        out_shape=[jax.ShapeDtypeStruct((s, QA_W), BF16), jax.ShapeDtypeStruct((N_KV_HEADS_A, Q_PER_KV, s), F32)],
        compiler_params=_cparams(("arbitrary", "arbitrary")))(qrot, krot, va)


def _attn_a_bwd(qrot, krot, va, oa, doa, lse, tq, tk, after):
    s = qrot.shape[0]
    n_kv = s // tk
    gw = Q_PER_KV * HEAD_DIM_A

    def body(q_ref, do_ref, o_ref, lse_ref, k_ref, v_ref, after_ref, dq_ref, dk_ref, dv_ref):
        @pl.when(pl.program_id(1) == 0)
        def _():
            dk_ref[...] = jnp.zeros_like(dk_ref)
            dv_ref[...] = jnp.zeros_like(dv_ref)

        def stack(ref):
            return jnp.concatenate([ref[:, 128 * h:128 * h + 128] for h in range(Q_PER_KV)], axis=0)

        q4, do4, o4 = stack(q_ref), stack(do_ref), stack(o_ref)
        q4t, do4t = q4.T, do4.T
        delta = jnp.sum((do4.astype(F32) * o4.astype(F32)).T, axis=0, keepdims=True)
        lse4 = jnp.concatenate([lse_ref[0, h:h + 1, :] for h in range(Q_PER_KV)], axis=1)

        def step(j, dq):
            sl = pl.ds(pl.multiple_of(j * tk, tk), tk)
            kj, vj = k_ref[sl, :], v_ref[sl, :]
            p = jnp.exp2(_dot_nt(kj, q4) - lse4)
            ds = (p * (_dot_nt(vj, do4) - delta)).astype(BF16)
            dk_ref[:, sl] += _dot_nt(q4t, ds)
            dv_ref[:, sl] += _dot_nt(do4t, p.astype(BF16))
            return dq + _dot_tn(kj, ds)

        dq = lax.fori_loop(0, n_kv, step, jnp.zeros((HEAD_DIM_A, Q_PER_KV * tq), F32)).T
        for h in range(Q_PER_KV):
            dq_ref[:, 128 * h:128 * h + 128] = dq[h * tq:(h + 1) * tq]

    qspec = pl.BlockSpec((tq, gw), lambda g, i: (i, g))
    kspec = pl.BlockSpec((s, HEAD_DIM_A), lambda g, i: (0, g))
    ktspec = pl.BlockSpec((HEAD_DIM_A, s), lambda g, i: (g, 0))
    return pl.pallas_call(
        body, name="attn_a_bwd", grid=(N_KV_HEADS_A, s // tq),
        in_specs=[qspec, qspec, qspec, pl.BlockSpec((1, Q_PER_KV, tq), lambda g, i: (g, 0, i)), kspec, kspec, _ANY],
        out_specs=[qspec, ktspec, ktspec],
        out_shape=[jax.ShapeDtypeStruct((s, QA_W), F32), jax.ShapeDtypeStruct((KA_W, s), F32),
                   jax.ShapeDtypeStruct((KA_W, s), F32)],
        compiler_params=_cparams(("arbitrary", "arbitrary")))(qrot, doa, oa, lse, krot, va, after)


BAND_QB = 256
BAND_WIN = BAND_QB + 2 * BAND


def _band_specs(s, cb):
    per = cb // BAND
    last = s // BAND - 1
    cur = pl.BlockSpec((cb, GB_W), lambda i: (i, 0))
    prev = pl.BlockSpec((BAND, GB_W), lambda i: (jnp.maximum(i * per - 1, 0), 0))
    nxt = pl.BlockSpec((BAND, GB_W), lambda i: (jnp.minimum(i * per + per, last), 0))
    return cur, prev, nxt


def _window(prev_ref, cur_ref, next_ref):
    return jnp.concatenate([prev_ref[...], cur_ref[...], next_ref[...]], axis=0)


def _band_mask(base, seg_shift, window_rows):
    shape = (BAND_WIN, BAND_QB) if window_rows else (BAND_QB, BAND_WIN)
    a = lax.broadcasted_iota(jnp.int32, shape, 0)
    b = lax.broadcasted_iota(jnp.int32, shape, 1)
    rq, rk = (base - BAND + a, base + b) if window_rows else (base + a, base - BAND + b)
    same_segment = lax.shift_right_arithmetic(rq, jnp.int32(seg_shift)) == lax.shift_right_arithmetic(rk, jnp.int32(seg_shift))
    return (jnp.abs(rk - rq) <= BAND) & same_segment


def _build_bias(bmap_ref, tab_ref, bias_ref):
    bm = bmap_ref[...]
    acc = [jnp.zeros(bm.shape, F32) for _ in range(N_HEADS_PER_DIL)]
    for b in range(N_REL_BUCKETS):
        hit = bm == b
        for h in range(N_HEADS_PER_DIL):
            acc[h] = jnp.where(hit, tab_ref[b, h], acc[h])
    for h in range(N_HEADS_PER_DIL):
        bias_ref[h] = acc[h]


def _head_lane_masks():
    lane = lax.broadcasted_iota(jnp.int32, (1, LANES), 1)
    return [lane < HEAD_DIM_B, lane >= HEAD_DIM_B]


def _seg_shift(s, dil):
    seg = s // dil
    assert seg & (seg - 1) == 0, "segment length must be a power of two"
    return seg.bit_length() - 1


def _band_fwd(dil, qb, kb, vb, bmap, tab, cb):
    s = qb.shape[0]
    shift = _seg_shift(s, dil)

    def body(q_ref, kp_ref, kc_ref, kn_ref, vp_ref, vc_ref, vn_ref, bmap_ref, tab_ref, o_ref, lse_ref, bias_ref):
        @pl.when(pl.program_id(0) == 0)
        def _():
            _build_bias(bmap_ref, tab_ref, bias_ref)

        kw, vw = _window(kp_ref, kc_ref, kn_ref), _window(vp_ref, vc_ref, vn_ref)
        hm = _head_lane_masks()
        for jj in range(cb // BAND_QB):
            r0 = BAND_QB * jj
            mask = _band_mask(pl.program_id(0) * cb + r0, shift, False)
            for hp in range(2):
                ls = slice(LANES * hp, LANES * hp + LANES)
                qh = q_ref[r0:r0 + BAND_QB, ls]
                k3, v3 = kw[r0:r0 + BAND_WIN, ls], vw[r0:r0 + BAND_WIN, ls]
                o_half = jnp.zeros((BAND_QB, LANES), F32)
                lse_half = jnp.zeros((BAND_QB, LANES), F32)
                for hh in range(2):
                    sc = _dot_nt(jnp.where(hm[hh], qh, jnp.zeros_like(qh)), k3) + bias_ref[2 * hp + hh]
                    sc = jnp.where(mask, sc, NEG_INF)
                    m = jnp.max(sc, axis=-1, keepdims=True)
                    e = jnp.exp(sc - m)
                    l = jnp.sum(e, axis=-1, keepdims=True)
                    p = (e * (1.0 / l)).astype(BF16)
                    o_half = o_half + _dot_nn(p, jnp.where(hm[hh], v3, jnp.zeros_like(v3)))
                    lse_half = jnp.where(hm[hh], m + jnp.log(l), lse_half)
                o_ref[r0:r0 + BAND_QB, ls] = o_half
                lse_ref[r0:r0 + BAND_QB, ls] = lse_half

    cur, prev, nxt = _band_specs(s, cb)
    return pl.pallas_call(
        body, name=f"band_fwd_d{dil}", grid=(s // cb,),
        in_specs=[cur, prev, cur, nxt, prev, cur, nxt, _resident(bmap.shape), pl.BlockSpec(memory_space=pltpu.SMEM)],
        out_specs=[cur, cur],
        out_shape=[jax.ShapeDtypeStruct(qb.shape, F32), jax.ShapeDtypeStruct(qb.shape, F32)],
        scratch_shapes=[pltpu.VMEM((N_HEADS_PER_DIL, BAND_QB, BAND_WIN), F32)],
        compiler_params=_cparams(("arbitrary",)))(qb, kb, kb, kb, vb, vb, vb, bmap, tab)


def _band_bwd_q(dil, qb, kb, vb, dob, lse, dd, bmap, tab, cb):
    s = qb.shape[0]
    shift = _seg_shift(s, dil)
    n_steps = s // cb

    def body(q_ref, do_ref, lse_ref, dd_ref, kp_ref, kc_ref, kn_ref, vp_ref, vc_ref, vn_ref, bmap_ref, tab_ref,
             dq_ref, dtab_ref, bias_ref, dsum_ref):
        @pl.when(pl.program_id(0) == 0)
        def _():
            _build_bias(bmap_ref, tab_ref, bias_ref)
            dsum_ref[...] = jnp.zeros_like(dsum_ref)

        kw, vw = _window(kp_ref, kc_ref, kn_ref), _window(vp_ref, vc_ref, vn_ref)
        hm = _head_lane_masks()
        for jj in range(cb // BAND_QB):
            r0 = BAND_QB * jj
            mask = _band_mask(pl.program_id(0) * cb + r0, shift, False)
            for hp in range(2):
                ls = slice(LANES * hp, LANES * hp + LANES)
                qh, doh = q_ref[r0:r0 + BAND_QB, ls], do_ref[r0:r0 + BAND_QB, ls]
                k3, v3 = kw[r0:r0 + BAND_WIN, ls], vw[r0:r0 + BAND_WIN, ls]
                dq_half = jnp.zeros((BAND_QB, LANES), F32)
                for hh in range(2):
                    h = 2 * hp + hh
                    col = LANES * hp + HEAD_DIM_B * hh
                    sc = _dot_nt(jnp.where(hm[hh], qh, jnp.zeros_like(qh)), k3) + bias_ref[h]
                    sc = jnp.where(mask, sc, NEG_INF)
                    p = jnp.exp(sc - lse_ref[r0:r0 + BAND_QB, col:col + 1])
                    dp = _dot_nt(jnp.where(hm[hh], doh, jnp.zeros_like(doh)), v3)
                    ds = p * (dp - dd_ref[r0:r0 + BAND_QB, col:col + 1])
                    dsum_ref[h] += ds
                    dq_half = dq_half + _dot_nn(ds.astype(BF16), jnp.where(hm[hh], k3, jnp.zeros_like(k3)))
                dq_ref[r0:r0 + BAND_QB, ls] = dq_half

        @pl.when(pl.program_id(0) == n_steps - 1)
        def _():
            bm = bmap_ref[...]
            lane = lax.broadcasted_iota(jnp.int32, (1, LANES), 1)
            for b in range(N_REL_BUCKETS):
                hit = bm == b
                row = jnp.zeros((1, LANES), F32)
                for h in range(N_HEADS_PER_DIL):
                    row = jnp.where(lane == h, jnp.sum(jnp.where(hit, dsum_ref[h], 0.0)), row)
                dtab_ref[b:b + 1, :] = row

    cur, prev, nxt = _band_specs(s, cb)
    return pl.pallas_call(
        body, name=f"band_bwd_q_d{dil}", grid=(n_steps,),
        in_specs=[cur, cur, cur, cur, prev, cur, nxt, prev, cur, nxt, _resident(bmap.shape),
                  pl.BlockSpec(memory_space=pltpu.SMEM)],
        out_specs=[cur, _acc_spec((N_REL_BUCKETS, LANES))],
        out_shape=[jax.ShapeDtypeStruct(qb.shape, F32), jax.ShapeDtypeStruct((N_REL_BUCKETS, LANES), F32)],
        scratch_shapes=[pltpu.VMEM((N_HEADS_PER_DIL, BAND_QB, BAND_WIN), F32),
                        pltpu.VMEM((N_HEADS_PER_DIL, BAND_QB, BAND_WIN), F32)],
        compiler_params=_cparams(("arbitrary",)))(qb, dob, lse, dd, kb, kb, kb, vb, vb, vb, bmap, tab)


def _band_bwd_kv(dil, qb, kb, vb, dob, lse, dd, bmap_t, tab, cb):
    s = qb.shape[0]
    shift = _seg_shift(s, dil)

    def body(k_ref, v_ref, qp_ref, qc_ref, qn_ref, dp_ref, dc_ref, dn_ref, lp_ref, lc_ref, ln_ref,
             ep_ref, ec_ref, en_ref, bmap_ref, tab_ref, dk_ref, dv_ref, bias_ref):
        @pl.when(pl.program_id(0) == 0)
        def _():
            _build_bias(bmap_ref, tab_ref, bias_ref)

        qw, dow = _window(qp_ref, qc_ref, qn_ref), _window(dp_ref, dc_ref, dn_ref)
        lw, ew = _window(lp_ref, lc_ref, ln_ref), _window(ep_ref, ec_ref, en_ref)
        hm = _head_lane_masks()
        for jj in range(cb // BAND_QB):
            r0 = BAND_QB * jj
            mask = _band_mask(pl.program_id(0) * cb + r0, shift, True)
            for hp in range(2):
                ls = slice(LANES * hp, LANES * hp + LANES)
                kh, vh = k_ref[r0:r0 + BAND_QB, ls], v_ref[r0:r0 + BAND_QB, ls]
                q3, do3 = qw[r0:r0 + BAND_WIN, ls], dow[r0:r0 + BAND_WIN, ls]
                dk_half = jnp.zeros((BAND_QB, LANES), F32)
                dv_half = jnp.zeros((BAND_QB, LANES), F32)
                for hh in range(2):
                    col = LANES * hp + HEAD_DIM_B * hh
                    q3m = jnp.where(hm[hh], q3, jnp.zeros_like(q3))
                    do3m = jnp.where(hm[hh], do3, jnp.zeros_like(do3))
                    sc = _dot_nt(q3m, kh) + bias_ref[2 * hp + hh]
                    sc = jnp.where(mask, sc, NEG_INF)
                    p = jnp.exp(sc - lw[r0:r0 + BAND_WIN, col:col + 1])
                    ds = p * (_dot_nt(do3m, vh) - ew[r0:r0 + BAND_WIN, col:col + 1])
                    dk_half = dk_half + _dot_tn(ds.astype(BF16), q3m)
                    dv_half = dv_half + _dot_tn(p.astype(BF16), do3m)
                dk_ref[r0:r0 + BAND_QB, ls] = dk_half
                dv_ref[r0:r0 + BAND_QB, ls] = dv_half

    cur, prev, nxt = _band_specs(s, cb)
    win = [prev, cur, nxt]
    return pl.pallas_call(
        body, name=f"band_bwd_kv_d{dil}", grid=(s // cb,),
        in_specs=[cur, cur] + win * 4 + [_resident(bmap_t.shape), pl.BlockSpec(memory_space=pltpu.SMEM)],
        out_specs=[cur, cur],
        out_shape=[jax.ShapeDtypeStruct(qb.shape, F32), jax.ShapeDtypeStruct(qb.shape, F32)],
        scratch_shapes=[pltpu.VMEM((N_HEADS_PER_DIL, BAND_WIN, BAND_QB), F32)],
        compiler_params=_cparams(("arbitrary",)))(
        kb, vb, qb, qb, qb, dob, dob, dob, lse, lse, lse, dd, dd, dd, bmap_t, tab)


def _t5_bucket(rel):
    nb = N_REL_BUCKETS // 2
    ret = (rel > 0).astype(np.int32) * nb
    n = np.abs(rel)
    max_exact = nb // 2
    large = max_exact + (np.log(np.maximum(n, 1) / max_exact) / math.log(REL_MAX_DIST / max_exact)
                         * (nb - max_exact)).astype(np.int32)
    large = np.minimum(large, nb - 1)
    return ret + np.where(n < max_exact, n, large).astype(np.int32)


def _bucket_maps(dil):
    off_qk = np.arange(BAND_WIN)[None, :] - BAND - np.arange(BAND_QB)[:, None]
    off_kq = np.arange(BAND_QB)[None, :] + BAND - np.arange(BAND_WIN)[:, None]
    return [np.where(np.abs(off) <= BAND, _t5_bucket(off * dil), -1).astype(np.int32) for off in (off_qk, off_kq)]


def _seg_sum(v):
    lane = lax.broadcasted_iota(jnp.int32, (1, v.shape[1]), 1)
    out = jnp.zeros_like(v)
    for h in range(v.shape[1] // HEAD_DIM_B):
        m = (lane >= HEAD_DIM_B * h) & (lane < HEAD_DIM_B * (h + 1))
        out = jnp.where(m, jnp.sum(jnp.where(m, v, 0.0), axis=-1, keepdims=True), out)
    return out


def _mix_out(x, oa, og, lg, ga, gb, w_oa, w_ob_t, w_o, tb):
    s, d = x.shape

    def body(x_ref, oa_ref, og0_ref, og1_ref, og2_ref, lg0_ref, lg1_ref, lg2_ref, ga_ref, gb_ref,
             woa_ref, wob_ref, wo_ref, x2_ref, ob_ref, lse0_ref, lse1_ref, lse2_ref, ya_ref, yb_ref, u_ref, scr_ref):
        og_refs, lg_refs = (og0_ref, og1_ref, og2_ref), (lg0_ref, lg1_ref, lg2_ref)
        l0, l1, l2 = [_from_residues(lg_refs[g], scr_ref, dil) for g, dil in enumerate(DILATIONS)]
        lmax = jnp.maximum(jnp.maximum(l0, l1), l2)
        w0, w1, w2 = jnp.exp(l0 - lmax), jnp.exp(l1 - lmax), jnp.exp(l2 - lmax)
        den = w0 + w1 + w2
        o0, o1, o2 = [_from_residues(og_refs[g], scr_ref, dil) for g, dil in enumerate(DILATIONS)]
        ob = ((w0 * o0 + w1 * o1 + w2 * o2) / den).astype(BF16)
        ob_ref[...] = ob
        lse = lmax + jnp.log(den)
        for g, (dil, ref) in enumerate(zip(DILATIONS, (lse0_ref, lse1_ref, lse2_ref))):
            _to_residues(lse, ref, scr_ref, dil, F32)
        ya = _dot_nn(oa_ref[...], woa_ref[...])
        yb = _dot_nt(ob, wob_ref[...])
        ya_ref[...] = ya.astype(BF16)
        yb_ref[...] = yb.astype(BF16)
        u = (ga_ref[...] * ya + gb_ref[...] * yb).astype(BF16)
        u_ref[...] = u
        x2_ref[...] = x_ref[...] + _dot_nn(u, wo_ref[...])

    sd = jax.ShapeDtypeStruct
    res = list(pl.pallas_call(
        body, name="mix_out", grid=(s // tb,),
        in_specs=[_rows(tb, d), _rows(tb, QA_W)] + _dil_specs(tb) * 2 + [
            _rows(tb, d), _rows(tb, d), _resident(w_oa.shape), _resident(w_ob_t.shape), _resident(w_o.shape)],
        out_specs=[_rows(tb, d), _rows(tb, GB_W)] + _dil_specs(tb) + [_rows(tb, d), _rows(tb, d), _rows(tb, d)],
        out_shape=[sd((s, d), F32), sd((s, GB_W), BF16)] + _dil_shapes(s, F32) + [
            sd((s, d), BF16), sd((s, d), BF16), sd((s, d), BF16)],
        scratch_shapes=[pltpu.VMEM((2, tb, LANES), F32)],
        compiler_params=_cparams(("arbitrary",)))(x, oa, *og, *lg, ga, gb, w_oa, w_ob_t, w_o))
    return res[:2] + [res[2:5]] + res[5:]


def _mlp_fwd(x2, w1_t, w2, g_mlp, tb, tc):
    s, d = x2.shape
    dff = w1_t.shape[0]

    def body(x_ref, w1_ref, w2_ref, g_ref, x3_ref, r_ref, h_ref):
        xv = x_ref[...]
        hb = (xv * _rstd(xv) * g_ref[...]).astype(BF16)
        h_ref[...] = hb
        x3_ref[...] = xv
        for c in range(dff // tc):
            sl = slice(tc * c, tc * c + tc)
            r = jnp.maximum(_dot_nt(hb, w1_ref[sl, :]), 0.0)
            r_ref[:, sl] = r.astype(BF16)
            x3_ref[...] += _dot_nn((r * r).astype(BF16), w2_ref[sl, :])

    sd = jax.ShapeDtypeStruct
    return pl.pallas_call(
        body, name="mlp_fwd", grid=(s // tb,),
        in_specs=[_rows(tb, d), _resident(w1_t.shape), _resident(w2.shape), _resident(g_mlp.shape)],
        out_specs=[_rows(tb, d), _rows(tb, dff), _rows(tb, d)],
        out_shape=[sd((s, d), F32), sd((s, dff), BF16), sd((s, d), BF16)],
        compiler_params=_cparams(("arbitrary",)))(x2, w1_t, w2, g_mlp)


def _ple_loss(x3, p, target, w_pg, w_p_t, g_ple, g_fin, tb):
    s, d = x3.shape
    dp = p.shape[1]

    def body(x_ref, p_ref, t_ref, wpg_ref, wp_ref, gple_ref, gfin_ref,
             dx3_ref, h3_ref, dpre_ref, dpe_ref, pb_ref, loss_ref, dgfin_ref, dgple_ref):
        @pl.when(pl.program_id(0) == 0)
        def _():
            loss_ref[...] = jnp.zeros_like(loss_ref)
            dgfin_ref[...] = jnp.zeros_like(dgfin_ref)
            dgple_ref[...] = jnp.zeros_like(dgple_ref)

        x3v = x_ref[...]
        r3 = _rstd(x3v)
        n3 = x3v * r3
        h3 = (n3 * gple_ref[...]).astype(BF16)
        h3_ref[...] = h3
        gp = _sigmoid(_dot_nn(h3, wpg_ref[...]))
        pb = p_ref[...].astype(BF16)
        pb_ref[...] = pb
        pe = _dot_nt(pb, wp_ref[...])
        x4 = x3v + gp * pe
        r4 = _rstd(x4)
        n4 = x4 * r4
        err = n4 * gfin_ref[...] - t_ref[...]
        loss_ref[...] += jnp.sum(0.5 * jnp.mean(err * err, axis=-1, keepdims=True), axis=0, keepdims=True)
        dy = err / d
        dgfin_ref[...] += _colsum(dy * n4)
        dx4 = _rms_bwd(dy, n4, r4, gfin_ref[...])
        dpe_ref[...] = (dx4 * gp).astype(BF16)
        dpre = (dx4 * pe * gp * (1.0 - gp)).astype(BF16)
        dpre_ref[...] = dpre
        dh3 = _dot_nt(dpre, wpg_ref[...])
        dgple_ref[...] += _colsum(dh3 * n3)
        dx3_ref[...] = dx4 + _rms_bwd(dh3, n3, r3, gple_ref[...])

    sd = jax.ShapeDtypeStruct
    return pl.pallas_call(
        body, name="ple_loss", grid=(s // tb,),
        in_specs=[_rows(tb, d), _rows(tb, dp), _rows(tb, d), _resident(w_pg.shape), _resident(w_p_t.shape),
                  _resident(g_ple.shape), _resident(g_fin.shape)],
        out_specs=[_rows(tb, d), _rows(tb, d), _rows(tb, d), _rows(tb, d), _rows(tb, dp),
                   _acc_spec((1, LANES)), _acc_spec((1, d)), _acc_spec((1, d))],
        out_shape=[sd((s, d), F32), sd((s, d), BF16), sd((s, d), BF16), sd((s, d), BF16), sd((s, dp), BF16),
                   sd((1, LANES), F32), sd((1, d), F32), sd((1, d), F32)],
        compiler_params=_cparams(("arbitrary",)))(x3, p, target, w_pg, w_p_t, g_ple, g_fin)


def _mlp_bwd(dx3, x2, r, w1_t, w2, g_mlp, tb, tc):
    s, d = x2.shape
    dff = w1_t.shape[0]

    def body(dx3_ref, x_ref, r_ref, w1_ref, w2_ref, g_ref, dx2_ref, df_ref, dg_ref, dh_ref):
        @pl.when(pl.program_id(0) == 0)
        def _():
            dg_ref[...] = jnp.zeros_like(dg_ref)

        dx3v = dx3_ref[...]
        dx3b = dx3v.astype(BF16)
        dh_ref[...] = jnp.zeros_like(dh_ref)
        for c in range(dff // tc):
            sl = slice(tc * c, tc * c + tc)
            df = (_dot_nt(dx3b, w2_ref[sl, :]) * (2.0 * r_ref[:, sl].astype(F32))).astype(BF16)
            df_ref[:, sl] = df
            dh_ref[...] += _dot_nn(df, w1_ref[sl, :])
        xv = x_ref[...]
        r2 = _rstd(xv)
        n2 = xv * r2
        dh = dh_ref[...]
        dg_ref[...] += _colsum(dh * n2)
        dx2_ref[...] = dx3v + _rms_bwd(dh, n2, r2, g_ref[...])

    sd = jax.ShapeDtypeStruct
    return pl.pallas_call(
        body, name="mlp_bwd", grid=(s // tb,),
        in_specs=[_rows(tb, d), _rows(tb, d), _rows(tb, dff), _resident(w1_t.shape), _resident(w2.shape),
                  _resident(g_mlp.shape)],
        out_specs=[_rows(tb, d), _rows(tb, dff), _acc_spec((1, d))],
        out_shape=[sd((s, d), F32), sd((s, dff), BF16), sd((1, d), F32)],
        scratch_shapes=[pltpu.VMEM((tb, d), F32)],
        compiler_params=_cparams(("arbitrary",)))(dx3, x2, r, w1_t, w2, g_mlp)


def _mix_out_bwd(dx2, ya, yb, ga, gb, ob, w_oa, w_ob_t, w_o, tb, after):
    s, d = dx2.shape

    def body(dx_ref, ya_ref, yb_ref, ga_ref, gb_ref, ob_ref, woa_ref, wob_ref, wo_ref, after_ref,
             doa_ref, dob0_ref, dob1_ref, dob2_ref, dd0_ref, dd1_ref, dd2_ref, dga_ref, dgb_ref, dya_ref, dyb_ref,
             dbg_ref, scr_ref):
        @pl.when(pl.program_id(0) == 0)
        def _():
            dbg_ref[...] = jnp.zeros_like(dbg_ref)

        du = _dot_nt(dx_ref[...].astype(BF16), wo_ref[...])
        gav, gbv = ga_ref[...], gb_ref[...]
        dya = (du * gav).astype(BF16)
        dyb = (du * gbv).astype(BF16)
        dya_ref[...] = dya
        dyb_ref[...] = dyb
        dga = du * ya_ref[...].astype(F32) * gav * (1.0 - gav)
        dgb = du * yb_ref[...].astype(F32) * gbv * (1.0 - gbv)
        dga_ref[...] = dga.astype(BF16)
        dgb_ref[...] = dgb.astype(BF16)
        dbg_ref[:, 0:d] += _colsum(dga)
        dbg_ref[:, d:2 * d] += _colsum(dgb)
        doa_ref[...] = _dot_nt(dya, woa_ref[...]).astype(BF16)
        dob = _dot_nn(dyb, wob_ref[...])
        dd = _seg_sum(dob * ob_ref[...].astype(F32))
        for dil, dob_ref, dd_ref in zip(DILATIONS, (dob0_ref, dob1_ref, dob2_ref), (dd0_ref, dd1_ref, dd2_ref)):
            _to_residues(dob, dob_ref, scr_ref, dil, BF16)
            _to_residues(dd, dd_ref, scr_ref, dil, F32)

    sd = jax.ShapeDtypeStruct
    res = list(pl.pallas_call(
        body, name="mix_out_bwd", grid=(s // tb,),
        in_specs=[_rows(tb, d)] * 5 + [_rows(tb, GB_W), _resident(w_oa.shape), _resident(w_ob_t.shape),
                                       _resident(w_o.shape), _ANY],
        out_specs=[_rows(tb, QA_W)] + _dil_specs(tb) * 2 + [_rows(tb, d), _rows(tb, d), _rows(tb, d),
                                                           _rows(tb, d), _acc_spec((1, 2 * d))],
        out_shape=[sd((s, QA_W), BF16)] + _dil_shapes(s, BF16) + _dil_shapes(s, F32) + [
            sd((s, d), BF16), sd((s, d), BF16), sd((s, d), BF16), sd((s, d), BF16), sd((1, 2 * d), F32)],
        scratch_shapes=[pltpu.VMEM((2, tb, LANES), F32)],
        compiler_params=_cparams(("arbitrary",)))(dx2, ya, yb, ga, gb, ob, w_oa, w_ob_t, w_o, after))
    return res[:1] + [res[1:4], res[4:7]] + res[7:]


def _in_proj_bwd(dx2, x, dqrot, dkrot, dva, qraw, kraw, tabs, dqb, dkb, dvb, dga, dgb, w_in_t, g_mix, q_g, k_g, tb):
    s, d = x.shape
    din = w_in_t.shape[0]
    q_scale = HEAD_DIM_A ** -0.5
    b_scale = HEAD_DIM_B ** -0.5
    tc = 256

    def body(dx2_ref, x_ref, dq_ref, dk_ref, dv_ref, qraw_ref, kraw_ref, c_ref, s1_ref, s2_ref, *rest):
        dqb_refs, dkb_refs, dvb_refs = rest[0:3], rest[3:6], rest[6:9]
        (dga_ref, dgb_ref, w_ref, gmix_ref, qg_ref, kg_ref,
         dx_ref, dz_ref, dgmix_ref, dqg_ref, dkg_ref, dh_ref, scr_ref) = rest[9:]

        @pl.when(pl.program_id(0) == 0)
        def _():
            dgmix_ref[...] = jnp.zeros_like(dgmix_ref)
            dqg_ref[...] = jnp.zeros_like(dqg_ref)
            dkg_ref[...] = jnp.zeros_like(dkg_ref)

        cos, s1, s2 = c_ref[...], s1_ref[...], s2_ref[...]

        def head_bwd(drot, z, g_ref, acc_ref):
            dn = _rope_bwd(drot, cos, s1, s2)
            rr = _rstd(z)
            nn = z * rr
            acc_ref[...] += _colsum(dn * nn)
            return _rms_bwd(dn, nn, rr, g_ref[...])

        for h in range(N_Q_HEADS_A):
            sl = slice(128 * h, 128 * h + 128)
            dz_ref[:, OFF_QA + 128 * h:OFF_QA + 128 * h + 128] = head_bwd(
                dq_ref[:, sl] * q_scale, qraw_ref[:, sl], qg_ref, dqg_ref).astype(BF16)
        for h in range(N_KV_HEADS_A):
            sl = slice(128 * h, 128 * h + 128)
            dz_ref[:, OFF_KA + 128 * h:OFF_KA + 128 * h + 128] = head_bwd(
                dk_ref[sl, :].T * LN_2, kraw_ref[:, sl], kg_ref, dkg_ref).astype(BF16)
        dz_ref[:, OFF_VA:OFF_VA + KA_W] = dv_ref[...].T.astype(BF16)
        for g, dil in enumerate(DILATIONS):
            dz_ref[:, OFF_QB + GB_W * g:OFF_QB + GB_W * (g + 1)] = (
                _from_residues(dqb_refs[g], scr_ref, dil) * b_scale).astype(BF16)
            dz_ref[:, OFF_KB + GB_W * g:OFF_KB + GB_W * (g + 1)] = _from_residues(dkb_refs[g], scr_ref, dil).astype(BF16)
            dz_ref[:, OFF_VB + GB_W * g:OFF_VB + GB_W * (g + 1)] = _from_residues(dvb_refs[g], scr_ref, dil).astype(BF16)
        dz_ref[:, OFF_GA:OFF_GA + d] = dga_ref[...]
        dz_ref[:, OFF_GA + d:OFF_GA + 2 * d] = dgb_ref[...]
        dh_ref[...] = jnp.zeros_like(dh_ref)
        for c in range(din // tc):
            sl = slice(tc * c, tc * c + tc)
            dh_ref[...] += _dot_nn(dz_ref[:, sl], w_ref[sl, :])
        xv = x_ref[...]
        r1 = _rstd(xv)
        n1 = xv * r1
        dh = dh_ref[...]
        dgmix_ref[...] += _colsum(dh * n1)
        dx_ref[...] = dx2_ref[...] + _rms_bwd(dh, n1, r1, gmix_ref[...])

    sd = jax.ShapeDtypeStruct
    return pl.pallas_call(
        body, name="in_proj_bwd", grid=(s // tb,),
        in_specs=[_rows(tb, d), _rows(tb, d), _rows(tb, QA_W), pl.BlockSpec((KA_W, tb), lambda i: (0, i)),
                  pl.BlockSpec((KA_W, tb), lambda i: (0, i)), _rows(tb, QA_W),
                  _rows(tb, KA_W), _rows(tb, LANES), _rows(tb, LANES), _rows(tb, LANES),
                  ] + _dil_specs(tb) * 3 + [_rows(tb, d), _rows(tb, d),
                  _resident(w_in_t.shape), _resident(g_mix.shape), _resident(q_g.shape), _resident(k_g.shape)],
        out_specs=[_rows(tb, d), _rows(tb, din), _acc_spec((1, d)), _acc_spec((1, HEAD_DIM_A)),
                   _acc_spec((1, HEAD_DIM_A))],
        out_shape=[sd((s, d), F32), sd((s, din), BF16), sd((1, d), F32), sd((1, HEAD_DIM_A), F32),
                   sd((1, HEAD_DIM_A), F32)],
        scratch_shapes=[pltpu.VMEM((tb, d), F32), pltpu.VMEM((2, tb, LANES), F32)],
        compiler_params=_cparams(("arbitrary",)))(
        dx2, x, dqrot, dkrot, dva, qraw, kraw, *tabs, *dqb, *dkb, *dvb, dga, dgb, w_in_t, g_mix, q_g, k_g)


def _identity(v):
    return v


def _to_bf16(v):
    return v.astype(BF16)


def _square_bf16(v):
    vf = v.astype(F32)
    return (vf * vf).astype(BF16)


def _weight_grad(name, a, b, ti, tj, tk, a_fn=_identity, b_fn=_identity):
    t, m = a.shape
    n = b.shape[1]
    n_k = t // tk

    def body(a_ref, b_ref, o_ref, acc_ref):
        k = pl.program_id(2)

        @pl.when(k == 0)
        def _():
            acc_ref[...] = jnp.zeros_like(acc_ref)

        acc_ref[...] += _dot_tn(a_fn(a_ref[...]), b_fn(b_ref[...]))

        @pl.when(k == n_k - 1)
        def _():
            o_ref[...] = acc_ref[...].astype(BF16)

    return pl.pallas_call(
        body, name=name, grid=(m // ti, n // tj, n_k),
        in_specs=[pl.BlockSpec((tk, ti), lambda i, j, k: (k, i)), pl.BlockSpec((tk, tj), lambda i, j, k: (k, j))],
        out_specs=pl.BlockSpec((ti, tj), lambda i, j, k: (i, j)),
        out_shape=jax.ShapeDtypeStruct((m, n), BF16),
        scratch_shapes=[pltpu.VMEM((ti, tj), F32)],
        compiler_params=_cparams(("arbitrary", "arbitrary", "arbitrary")))(a, b)


def _sum_slots(name, recv, own):
    m, n, k = recv.shape
    tc = min(k, 256)

    def body(own_ref, r_ref, o_ref):
        acc = own_ref[...].astype(F32)
        for i in range(m):
            acc = acc + r_ref[i].astype(F32)
        o_ref[...] = acc

    return pl.pallas_call(
        body, name=name, grid=(k // tc,),
        in_specs=[pl.BlockSpec((n, tc), lambda j: (0, j)), pl.BlockSpec((m, n, tc), lambda j: (0, 0, j))],
        out_specs=pl.BlockSpec((n, tc), lambda j: (0, j)),
        out_shape=jax.ShapeDtypeStruct((n, k), F32),
        compiler_params=_cparams(("arbitrary",)))(own, recv)


def _adamw_math(w, g, m, v):
    m = ADAM_B1 * m + (1.0 - ADAM_B1) * g
    v = ADAM_B2 * v + (1.0 - ADAM_B2) * (g * g)
    m_hat = m / (1.0 - ADAM_B1 ** ADAM_STEP)
    v_hat = v / (1.0 - ADAM_B2 ** ADAM_STEP)
    delta = -ADAM_LR * (m_hat / (jnp.sqrt(v_hat) + ADAM_EPS) + ADAM_WD * w)
    return delta, m, v


def _adamw(name, w, g, m, v):
    r, c = w.shape
    tr = min(r, 256)

    def body(w_ref, g_ref, m_ref, v_ref, d_ref, mo_ref, vo_ref):
        d_ref[...], mo_ref[...], vo_ref[...] = _adamw_math(w_ref[...], g_ref[...], m_ref[...], v_ref[...])

    spec = pl.BlockSpec((tr, c), lambda i: (i, 0))
    return pl.pallas_call(
        body, name=name, grid=(r // tr,), in_specs=[spec] * 4, out_specs=[spec] * 3,
        out_shape=[jax.ShapeDtypeStruct((r, c), F32)] * 3,
        compiler_params=_cparams(("arbitrary",)))(w, g, m, v)


def _small_update(parts, w, m, v):
    def body(p_ref, w_ref, m_ref, v_ref, g_ref, d_ref, mo_ref, vo_ref):
        g = p_ref[0]
        for i in range(1, N_DEV):
            g = g + p_ref[i]
        g_ref[...] = g
        d_ref[...], mo_ref[...], vo_ref[...] = _adamw_math(w_ref[...], g, m_ref[...], v_ref[...])

    return pl.pallas_call(body, name="small_update", out_shape=[jax.ShapeDtypeStruct(w.shape, F32)] * 4)(
        parts, w, m, v)


def _pack_rows(vectors, n_rows):
    flat = jnp.concatenate([v.reshape(-1).astype(F32) for v in vectors])
    flat = jnp.pad(flat, (0, n_rows * LANES - flat.shape[0]))
    return flat.reshape(n_rows, LANES)


def _pick_tile(n, prefs):
    for t in prefs:
        if n % t == 0:
            return t
    return n


def kernel(x, p, norm_mix_g, w_in, b_gate, q_norm_g, k_norm_g, rel_bias, w_out_a, w_out_b, w_out, norm_mlp_g, w_ff1, w_ff2, norm_ple_g, w_ple_gate, w_ple, final_norm_g, loss_target, m_norm_mix_g, m_w_in, m_b_gate, m_q_norm_g, m_k_norm_g, m_rel_bias, m_w_out_a, m_w_out_b, m_w_out, m_norm_mlp_g, m_w_ff1, m_w_ff2, m_norm_ple_g, m_w_ple_gate, m_w_ple, m_final_norm_g, v_norm_mix_g, v_w_in, v_b_gate, v_q_norm_g, v_k_norm_g, v_rel_bias, v_w_out_a, v_w_out_b, v_w_out, v_norm_mlp_g, v_w_ff1, v_w_ff2, v_norm_ple_g, v_w_ple_gate, v_w_ple, v_final_norm_g):
    s, d = x.shape[1], x.shape[2]
    xs, ps, ts = x[0], p[0, 0], loss_target[0]
    tb = _pick_tile(s, (512, 256))
    tq = _pick_tile(s, (256,))
    tk = _pick_tile(s, (1024, 512))
    cb = _pick_tile(s, (512,))
    fin_g = final_norm_g.reshape(1, d)

    col_sharded = {"w_in": w_in[0], "w_out_b": w_out_b[0], "w_ff1": w_ff1[0], "w_ple": w_ple[0]}
    row_sharded = {"w_out_a": w_out_a[0], "w_out": w_out[0], "w_ff2": w_ff2[0], "w_ple_gate": w_ple_gate[0]}
    order = ["w_in", "w_out_a", "w_out_b", "w_out", "w_ff1", "w_ff2", "w_ple_gate", "w_ple"]
    shards = [(col_sharded[n].T if n in col_sharded else row_sharded[n]).astype(BF16) for n in order]
    my_idx = 4 * lax.axis_index("x") + 2 * lax.axis_index("y") + lax.axis_index("c")
    (w_in_t,) = _all_gather(shards[:1])
    zones = [lax.dynamic_update_slice(lax.empty((N_DEV * sh.shape[0], sh.shape[1]), BF16), sh,
                                      (my_idx * sh.shape[0], 0)) for sh in shards[1:]]
    ag = _copies_start("weights_gather_start", shards[1:], zones, w_in_t, True)

    tabs = _rope_tables(s)
    (h1, qraw, kraw, qrot, krot, va, qb, kb, vb, ga, gb) = _in_proj(
        xs, tabs, w_in_t, norm_mix_g, b_gate, q_norm_g, k_norm_g, tb, ag[4])
    oa, lse_a = _attn_a_fwd(qrot, krot, va, tq, tk)
    _, (w_oa, w_ob_t, w_o, w_ff1_t, w_ff2_f, w_pg, w_p_t) = _copies_wait(
        "weights_gather_wait", ag[0], ag[1], ag[2], ag[3], lse_a, True)
    flat = lambda arrs: [a.reshape(s, GB_W) for a in arrs]
    split = lambda arrs: [a.reshape(dil, s // dil, GB_W) for a, dil in zip(arrs, DILATIONS)]
    qb_r, kb_r, vb_r = flat(qb), flat(kb), flat(vb)
    bmaps = [[jnp.asarray(m) for m in _bucket_maps(dil)] for dil in DILATIONS]
    bias_tabs = [rel_bias[:, N_HEADS_PER_DIL * g:N_HEADS_PER_DIL * (g + 1)] for g in range(3)]
    band_out = [_band_fwd(dil, qb_r[g], kb_r[g], vb_r[g], bmaps[g][0], bias_tabs[g], cb)
                for g, dil in enumerate(DILATIONS)]
    og, lg = split([o for o, _ in band_out]), split([l for _, l in band_out])
    x2, ob, lse_b, ya, yb, u = _mix_out(xs, oa, og, lg, ga, gb, w_oa, w_ob_t, w_o, tb)
    tc = _pick_tile(w_ff1_t.shape[0], (512,))
    x3, r_act, h2 = _mlp_fwd(x2, w_ff1_t, w_ff2_f, norm_mlp_g, tb, tc)

    dx3, h3, dpre, dpe, pb, loss_part, dg_fin, dg_ple = _ple_loss(
        x3, ps, ts, w_pg, w_p_t, norm_ple_g, fin_g, tb)
    dx2, df, dg_mlp = _mlp_bwd(dx3, x2, r_act, w_ff1_t, w_ff2_f, norm_mlp_g, tb, tc)

    tkk = _pick_tile(s, (1024, 512))
    dff = w_ff1_t.shape[0]
    t1k = lambda n: _pick_tile(n, (1024, 512, 256))
    slots = lambda parts: [lax.empty((7, a.shape[0] // N_DEV, a.shape[1]), BF16) for a in parts]
    part1 = [_weight_grad("grad_w_ff1", df, h2, t1k(dff), t1k(d), tkk),
             _weight_grad("grad_w_ff2", r_act, dx3, t1k(dff), t1k(d), tkk, a_fn=_square_bf16, b_fn=_to_bf16),
             _weight_grad("grad_w_ple_gate", h3, dpre, t1k(d), t1k(d), tkk),
             _weight_grad("grad_w_ple", dpe, pb, t1k(d), ps.shape[1], tkk)]
    rs1 = _copies_start("grads1_start", part1, slots(part1), dx2, False)
    doa, dob, dd, dga, dgb, dya, dyb, dbg = _mix_out_bwd(dx2, ya, yb, ga, gb, ob, w_oa, w_ob_t, w_o, tb, rs1[4])
    part2 = [_weight_grad("grad_w_out_a", oa, dya, t1k(QA_W), t1k(d), tkk),
             _weight_grad("grad_w_out_b", dyb, ob, t1k(d), GB_W, tkk),
             _weight_grad("grad_w_out", u, dx2, t1k(d), t1k(d), tkk, b_fn=_to_bf16)]
    rs2 = _copies_start("grads2_start", part2, slots(part2), doa, False)
    dqrot, dkrot, dva = _attn_a_bwd(qrot, krot, va, oa, doa, lse_a, tq, tk, rs2[4])
    dob_r, lse_r, dd_r = flat(dob), flat(lse_b), flat(dd)
    bwd_q = [_band_bwd_q(dil, qb_r[g], kb_r[g], vb_r[g], dob_r[g], lse_r[g], dd_r[g], bmaps[g][0], bias_tabs[g], cb)
             for g, dil in enumerate(DILATIONS)]
    bwd_kv = [_band_bwd_kv(dil, qb_r[g], kb_r[g], vb_r[g], dob_r[g], lse_r[g], dd_r[g], bmaps[g][1], bias_tabs[g], cb)
              for g, dil in enumerate(DILATIONS)]
    dqb, dkb, dvb = split([r[0] for r in bwd_q]), split([r[0] for r in bwd_kv]), split([r[1] for r in bwd_kv])
    grad_x, dz, dg_mix, dg_q, dg_k = _in_proj_bwd(
        dx2, xs, dqrot, dkrot, dva, qraw, kraw, tabs, dqb, dkb, dvb, dga, dgb, w_in_t, norm_mix_g,
        q_norm_g, k_norm_g, _pick_tile(s, (256,)))
    d_rel = jnp.concatenate([r[1][:, :N_HEADS_PER_DIL] for r in bwd_q], axis=1)

    din = w_in_t.shape[0]
    ti_in = _pick_tile(din, (din // 2,)) if (din // 2) % LANES == 0 else din
    part3 = [_weight_grad("grad_w_in", dz, h1, ti_in, t1k(d), tkk)]
    rs3 = _copies_start("grads3_start", part3, slots(part3), grad_x, False)

    def own_rows(a):
        n = a.shape[0] // N_DEV
        return lax.dynamic_slice(a, (my_idx * n, 0), (n, a.shape[1]))

    sums = {}
    src1, got1 = _copies_wait("grads1_wait", rs1[0], rs1[1], rs1[2], rs1[3], rs3[4], False)
    src2, got2 = _copies_wait("grads2_wait", rs2[0], rs2[1], rs2[2], rs2[3], rs3[4], False)
    for n, a, r in zip(["w_ff1", "w_ff2", "w_ple_gate", "w_ple", "w_out_a", "w_out_b", "w_out"],
                       src1 + src2, got1 + got2):
        sums[n] = _sum_slots("sum_" + n, r, own_rows(a))
    given_w = dict(w_in=w_in, w_out_a=w_out_a, w_out_b=w_out_b, w_out=w_out, w_ff1=w_ff1, w_ff2=w_ff2,
                   w_ple_gate=w_ple_gate, w_ple=w_ple)
    given_m = dict(w_in=m_w_in, w_out_a=m_w_out_a, w_out_b=m_w_out_b, w_out=m_w_out, w_ff1=m_w_ff1, w_ff2=m_w_ff2,
                   w_ple_gate=m_w_ple_gate, w_ple=m_w_ple)
    given_v = dict(w_in=v_w_in, w_out_a=v_w_out_a, w_out_b=v_w_out_b, w_out=v_w_out, w_ff1=v_w_ff1, w_ff2=v_w_ff2,
                   w_ple_gate=v_w_ple_gate, w_ple=v_w_ple)
    big = {}

    def update(n):
        g = sums[n].T if n in col_sharded else sums[n]
        delta, new_m, new_v = _adamw("adamw_" + n, given_w[n][0], g, given_m[n][0], given_v[n][0])
        big[n] = tuple(a[None] for a in (g, delta, new_m, new_v))

    for n in order[1:]:
        update(n)
    src3, got3 = _copies_wait("grads3_wait", rs3[0], rs3[1], rs3[2], rs3[3], big["w_ple"][1], False)
    sums["w_in"] = _sum_slots("sum_w_in", got3[0], own_rows(src3[0]))
    update("w_in")

    small_names = ["norm_mix_g", "b_gate", "q_norm_g", "k_norm_g", "rel_bias", "norm_mlp_g", "norm_ple_g",
                   "final_norm_g"]
    small_w = [norm_mix_g, b_gate, q_norm_g, k_norm_g, rel_bias, norm_mlp_g, norm_ple_g, final_norm_g]
    small_m = [m_norm_mix_g, m_b_gate, m_q_norm_g, m_k_norm_g, m_rel_bias, m_norm_mlp_g, m_norm_ple_g,
               m_final_norm_g]
    small_v = [v_norm_mix_g, v_b_gate, v_q_norm_g, v_k_norm_g, v_rel_bias, v_norm_mlp_g, v_norm_ple_g,
               v_final_norm_g]
    small_g = [dg_mix, dbg, dg_q, dg_k, d_rel, dg_mlp, dg_ple, dg_fin]
    sizes = [int(np.prod(w.shape)) for w in small_w]
    n_rows = -(-(sum(-(-sz // LANES) for sz in sizes) + 1) // 8) * 8
    pad = lambda v: jnp.pad(v.reshape(-1).astype(F32), (0, -v.size % LANES))
    pack = lambda vs, last: _pack_rows([pad(v) for v in vs] + [last], n_rows)
    zero_row = jnp.zeros((LANES,), F32)
    parts = _small_all_gather(pack(small_g, loss_part.reshape(-1) * (jnp.arange(LANES) == 0)))
    g_all, d_all, m_all, v_all = _small_update(parts, pack(small_w, zero_row), pack(small_m, zero_row),
                                               pack(small_v, zero_row))
    small = {}
    row = 0
    for n, w, sz in zip(small_names, small_w, sizes):
        nr = -(-sz // LANES)
        small[n] = tuple(a[row:row + nr].reshape(-1)[:sz].reshape(w.shape) for a in (g_all, d_all, m_all, v_all))
        row += nr
    loss = g_all[row, 0]

    names = ["norm_mix_g", "w_in", "b_gate", "q_norm_g", "k_norm_g", "rel_bias", "w_out_a", "w_out_b", "w_out",
             "norm_mlp_g", "w_ff1", "w_ff2", "norm_ple_g", "w_ple_gate", "w_ple", "final_norm_g"]
    res = {n: (big[n] if n in big else small[n]) for n in names}
    return (loss, grad_x[None], *[res[n][0] for n in names], *[res[n][1] for n in names],
            *[res[n][2] for n in names], *[res[n][3] for n in names])
```

```python
import functools
import math

import numpy as np
import jax
import jax.numpy as jnp
from jax import lax
from jax.experimental import pallas as pl
from jax.experimental.pallas import tpu as pltpu

F32 = jnp.float32
BF16 = jnp.bfloat16
MESH = pl.DeviceIdType.MESH

NORM_EPS = 1e-6
NEG_INF = -1e30
LOG2_E = math.log2(math.e)
LN_2 = math.log(2.0)
GRID_W = 64
ROPE_THETA = 10000.0
HEAD_DIM_A = 128
N_Q_HEADS_A = 8
N_KV_HEADS_A = 2
Q_PER_KV = N_Q_HEADS_A // N_KV_HEADS_A
HEAD_DIM_B = 64
N_HEADS_PER_DIL = 4
DILATIONS = (1, 4, 16)
BAND = 64
N_REL_BUCKETS = 32
REL_MAX_DIST = 1024
QA_W = N_Q_HEADS_A * HEAD_DIM_A
KA_W = N_KV_HEADS_A * HEAD_DIM_A
GB_W = N_HEADS_PER_DIL * HEAD_DIM_B
QB_W = GB_W * len(DILATIONS)
OFF_QA, OFF_KA, OFF_VA = 0, QA_W, QA_W + KA_W
OFF_QB = QA_W + 2 * KA_W
OFF_KB = OFF_QB + QB_W
OFF_VB = OFF_KB + QB_W
OFF_GA = OFF_VB + QB_W
N_DEV = 8
LANES = 128
VMEM_LIMIT = 56 * 2 ** 20

ADAM_LR, ADAM_B1, ADAM_B2, ADAM_EPS, ADAM_WD, ADAM_STEP = 0.001, 0.9, 0.999, 1e-08, 0.01, 10


def _cparams(sem):
    return pltpu.CompilerParams(dimension_semantics=sem, vmem_limit_bytes=VMEM_LIMIT)


def _resident(shape):
    nd = len(shape)
    return pl.BlockSpec(shape, lambda *_: (0,) * nd, pipeline_mode=pl.Buffered(1))


def _acc_spec(shape):
    nd = len(shape)
    return pl.BlockSpec(shape, lambda *_: (0,) * nd)


def _rows(tb, c):
    return pl.BlockSpec((tb, c), lambda i: (i, 0))


def _dil_shapes(s, dtype):
    return [jax.ShapeDtypeStruct((dil, s // dil, GB_W), dtype) for dil in DILATIONS]


def _dil_specs(tb):
    return [pl.BlockSpec((dil, tb // dil, GB_W), lambda i: (0, i, 0)) for dil in DILATIONS]


def _to_residues(val, out_ref, scr_ref, dil, dtype):
    if dil == 1:
        out_ref[0] = val.astype(dtype)
        return
    n = val.shape[0] // dil
    scr_ref[0] = val[:, :LANES]
    scr_ref[1] = val[:, LANES:]
    for r in range(dil):
        out_ref[r] = jnp.concatenate([scr_ref[0, pl.ds(r, n, stride=dil), :],
                                      scr_ref[1, pl.ds(r, n, stride=dil), :]], axis=1).astype(dtype)


def _from_residues(in_ref, scr_ref, dil):
    if dil == 1:
        return in_ref[0]
    n = in_ref.shape[1]
    for r in range(dil):
        v = in_ref[r]
        scr_ref[0, pl.ds(r, n, stride=dil), :] = v[:, :LANES]
        scr_ref[1, pl.ds(r, n, stride=dil), :] = v[:, LANES:]
    return jnp.concatenate([scr_ref[0], scr_ref[1]], axis=1)


def _dot_nt(a, b):
    return lax.dot_general(a, b, (((1,), (1,)), ((), ())), preferred_element_type=F32)


def _dot_nn(a, b):
    return lax.dot_general(a, b, (((1,), (0,)), ((), ())), preferred_element_type=F32)


def _dot_tn(a, b):
    return lax.dot_general(a, b, (((0,), (0,)), ((), ())), preferred_element_type=F32)


def _rstd(x):
    return lax.rsqrt(jnp.mean(x * x, axis=-1, keepdims=True) + NORM_EPS)


def _rms_bwd(dy, n, r, g):
    dn = dy * g
    return r * (dn - n * jnp.mean(dn * n, axis=-1, keepdims=True))


def _colsum(v):
    return jnp.sum(v, axis=0, keepdims=True)


def _sigmoid(v):
    return 1.0 / (1.0 + jnp.exp(-v))


def _rope_fwd(n, c, s1, s2):
    return n * c + pltpu.roll(n, 32, 1) * s1 + pltpu.roll(n, 96, 1) * s2


def _rope_bwd(d, c, s1, s2):
    return d * c + pltpu.roll(d * s1, 96, 1) + pltpu.roll(d * s2, 32, 1)


def _rope_tables(s):
    half = HEAD_DIM_A // 2
    inv = jnp.power(ROPE_THETA, -jnp.arange(0, half, 2, dtype=F32) / half)
    n_rows = s // GRID_W
    ang_r = jnp.arange(n_rows, dtype=F32)[:, None] * inv[None, :]
    ang_c = jnp.arange(GRID_W, dtype=F32)[:, None] * inv[None, :]
    cr, sr = jnp.repeat(jnp.cos(ang_r), GRID_W, axis=0), jnp.repeat(jnp.sin(ang_r), GRID_W, axis=0)
    cc, sc = jnp.tile(jnp.cos(ang_c), (n_rows, 1)), jnp.tile(jnp.sin(ang_c), (n_rows, 1))
    z = jnp.zeros_like(sr)
    cos = jnp.concatenate([cr, cr, cc, cc], axis=1)
    s1 = jnp.concatenate([z, sr, z, sc], axis=1)
    s2 = jnp.concatenate([-sr, z, -sc, z], axis=1)
    return cos, s1, s2


def _my_place():
    return lax.axis_index("x"), lax.axis_index("y"), lax.axis_index("c")


def _all_gather(shards):
    nw = len(shards)

    def body(*refs):
        ins, outs = refs[:nw], refs[nw:2 * nw]
        send_sems, recv_sems, local_sems = refs[2 * nw:]
        x, y, c = _my_place()
        me, sibling = (x, y, c), (x, y, 1 - c)
        chips = [(1 - x, y), (x, 1 - y), (1 - x, 1 - y)]

        def rows(w, px, py, pc):
            n = ins[w].shape[0]
            return outs[w].at[pl.ds(pl.multiple_of((4 * px + 2 * py + pc) * n, 16), n), :]

        def copy(w, k, block, to, src=None):
            return pltpu.make_async_remote_copy(
                src_ref=rows(w, *block) if src is None else src, dst_ref=rows(w, *block),
                send_sem=send_sems.at[w, k], recv_sem=recv_sems.at[w, k], device_id=to, device_id_type=MESH)

        mine = [pltpu.make_async_copy(ins[w], rows(w, *me), local_sems.at[w]) for w in range(nw)]
        for cp in mine:
            cp.start()
        first = []
        for w in range(nw):
            first.append(copy(w, 0, me, sibling, src=ins[w]))
            first += [copy(w, 1 + j, me, (*chip, c), src=ins[w]) for j, chip in enumerate(chips)]
        for cp in first:
            cp.start()
        passed = []
        for j, chip in enumerate(chips):
            for w in range(nw):
                copy(w, 1 + j, (*chip, c), me).wait_recv()
                fwd = copy(w, 4 + j, (*chip, c), sibling)
                fwd.start()
                passed.append(fwd)
        for w in range(nw):
            copy(w, 0, sibling, me).wait_recv()
        for j, chip in enumerate(chips):
            for w in range(nw):
                copy(w, 4 + j, (*chip, 1 - c), me).wait_recv()
        for cp in first + passed:
            cp.wait_send()
        for cp in mine:
            cp.wait()

    any_spec = pl.BlockSpec(memory_space=pl.ANY)
    return pl.pallas_call(
        body, name="weights_all_gather",
        out_shape=[jax.ShapeDtypeStruct((N_DEV * s.shape[0], s.shape[1]), s.dtype) for s in shards],
        in_specs=[any_spec] * nw, out_specs=[any_spec] * nw,
        scratch_shapes=[pltpu.SemaphoreType.DMA((nw, 7)), pltpu.SemaphoreType.DMA((nw, 7)),
                        pltpu.SemaphoreType.DMA((nw,))],
    )(*shards)


_FLIPS = [(fx, fy, fc) for fx in (0, 1) for fy in (0, 1) for fc in (0, 1)][1:]


def _small_all_gather(v):
    def body(v_ref, out_ref, send_sems, recv_sems):
        x, y, c = _my_place()
        my_idx = 4 * x + 2 * y + c
        out_ref[my_idx] = v_ref[...]
        sends = []
        for k, (fx, fy, fc) in enumerate(_FLIPS):
            to = (1 - x if fx else x, 1 - y if fy else y, 1 - c if fc else c)
            sends.append(pltpu.make_async_remote_copy(
                src_ref=v_ref, dst_ref=out_ref.at[my_idx], send_sem=send_sems.at[k], recv_sem=recv_sems.at[k],
                device_id=to, device_id_type=MESH))
        for cp in sends:
            cp.start()
        for k, (fx, fy, fc) in enumerate(_FLIPS):
            frm_idx = 4 * (1 - x if fx else x) + 2 * (1 - y if fy else y) + (1 - c if fc else c)
            pltpu.make_async_remote_copy(
                src_ref=v_ref, dst_ref=out_ref.at[frm_idx], send_sem=send_sems.at[k], recv_sem=recv_sems.at[k],
                device_id=(x, y, c), device_id_type=MESH).wait_recv()
        for cp in sends:
            cp.wait_send()

    vm = pl.BlockSpec(memory_space=pltpu.VMEM)
    return pl.pallas_call(
        body, name="small_all_gather", out_shape=jax.ShapeDtypeStruct((N_DEV,) + v.shape, v.dtype),
        in_specs=[vm], out_specs=vm,
        scratch_shapes=[pltpu.SemaphoreType.DMA((7,)), pltpu.SemaphoreType.DMA((7,))],
    )(v)


_HBM = pl.BlockSpec(memory_space=pltpu.HBM)
_SEM = pl.BlockSpec(memory_space=pltpu.SEMAPHORE)
_ANY = pl.BlockSpec(memory_space=pl.ANY)
_SPLIT_COPY = dict(has_side_effects=pltpu.SideEffectType.DATAFLOW_SIDE_EFFECTING)


def _peer(x, y, c, k):
    fx, fy, fc = _FLIPS[k]
    return (1 - x if fx else x, 1 - y if fy else y, 1 - c if fc else c)


def _in_hbm(a):
    return pltpu.with_memory_space_constraint(a, pltpu.HBM)


def _split_copies(srcs, lands, send_sems, recv_sems, gather, arriving):
    x, y, c = _my_place()
    my_idx = 4 * x + 2 * y + c
    out = []
    for k in range(7):
        to = _peer(x, y, c, k)
        to_idx = 4 * to[0] + 2 * to[1] + to[2]
        for w in range(len(srcs)):
            if gather:
                n = srcs[w].shape[0]
                src = srcs[w]
                dst = lands[w].at[pl.ds(pl.multiple_of((to_idx if arriving else my_idx) * n, 16), n), :]
            else:
                n = lands[w].shape[1]
                src = srcs[w].at[pl.ds(pl.multiple_of(to_idx * n, 16), n), :]
                dst = lands[w].at[k]
            out.append(pltpu.make_async_remote_copy(
                src_ref=src, dst_ref=dst, send_sem=send_sems.at[7 * w + k], recv_sem=recv_sems.at[7 * w + k],
                device_id=to, device_id_type=MESH))
    return out


def _copies_start(name, srcs, lands, after, gather):
    nw = len(srcs)

    def body(*refs):
        send_sems, recv_sems = refs[2 * nw + 1], refs[2 * nw + 2]
        for cp in _split_copies(refs[:nw], refs[nw:2 * nw], send_sems, recv_sems, gather, False):
            cp.start()
        refs[-1][...] = jnp.zeros_like(refs[-1])

    sems = pltpu.SemaphoreType.DMA((7 * nw,))
    thru = [pltpu.HBM(a.shape, a.dtype) for a in list(srcs) + list(lands)]
    res = pl.pallas_call(
        body, name=name, out_shape=(sems, sems, *thru, jax.ShapeDtypeStruct((8, LANES), F32)),
        in_specs=[_HBM] * (2 * nw) + [_ANY], out_specs=(_SEM, _SEM, *[_HBM] * (2 * nw), pl.BlockSpec(memory_space=pltpu.VMEM)),
        input_output_aliases={i: 2 + i for i in range(2 * nw)},
        compiler_params=pltpu.CompilerParams(**_SPLIT_COPY),
    )(*[_in_hbm(a) for a in srcs], *[_in_hbm(a) for a in lands], after)
    return res[0], res[1], list(res[2:2 + nw]), list(res[2 + nw:2 + 2 * nw]), res[-1]


def _copies_wait(name, send_sems, recv_sems, srcs, lands, after, gather):
    nw = len(srcs)

    def body(*refs):
        for cp in _split_copies(refs[:nw], refs[nw:2 * nw], refs[2 * nw], refs[2 * nw + 1], gather, False):
            cp.wait_send()
        for cp in _split_copies(refs[:nw], refs[nw:2 * nw], refs[2 * nw], refs[2 * nw + 1], gather, True):
            cp.wait_recv()

    thru = [pltpu.HBM(a.shape, a.dtype) for a in list(srcs) + list(lands)]
    res = pl.pallas_call(
        body, name=name, out_shape=tuple(thru),
        in_specs=[_HBM] * (2 * nw) + [_SEM, _SEM, _ANY], out_specs=tuple([_HBM] * (2 * nw)),
        input_output_aliases={i: i for i in range(2 * nw)},
        compiler_params=pltpu.CompilerParams(**_SPLIT_COPY),
    )(*srcs, *lands, send_sems, recv_sems, after)
    return list(res[:nw]), list(res[nw:])


def _in_proj(x, tabs, w_in_t, g_mix, b_gate, q_g, k_g, tb, after):
    s, d = x.shape
    n_gate_chunks = d // 256
    q_scale = HEAD_DIM_A ** -0.5 * LOG2_E
    b_scale = HEAD_DIM_B ** -0.5 * LOG2_E

    def body(x_ref, c_ref, s1_ref, s2_ref, w_ref, gmix_ref, bg_ref, qg_ref, kg_ref, after_ref,
             h1_ref, qraw_ref, kraw_ref, qrot_ref, krot_ref, va_ref, *rest):
        qb_refs, kb_refs, vb_refs = rest[0:3], rest[3:6], rest[6:9]
        ga_ref, gb_ref, scr_ref = rest[9:]
        xv = x_ref[...]
        hb = (xv * _rstd(xv) * gmix_ref[...]).astype(BF16)
        h1_ref[...] = hb
        cos, s1, s2 = c_ref[...], s1_ref[...], s2_ref[...]

        def proj(lo, width):
            return _dot_nt(hb, w_ref[lo:lo + width, :])

        def norm_rope(z, g):
            return _rope_fwd(z * _rstd(z) * g, cos, s1, s2)

        for j in range(QA_W // 256):
            z = proj(OFF_QA + 256 * j, 256)
            qraw_ref[:, 256 * j:256 * j + 256] = z
            for hh in range(2):
                lo = 256 * j + 128 * hh
                qrot_ref[:, lo:lo + 128] = (norm_rope(z[:, 128 * hh:128 * hh + 128], qg_ref[...]) * q_scale).astype(BF16)
        z = proj(OFF_KA, 256)
        kraw_ref[...] = z
        for hh in range(2):
            krot_ref[:, 128 * hh:128 * hh + 128] = norm_rope(z[:, 128 * hh:128 * hh + 128], kg_ref[...]).astype(BF16)
        va_ref[...] = proj(OFF_VA, 256).astype(BF16)
        for g, dil in enumerate(DILATIONS):
            _to_residues(proj(OFF_QB + GB_W * g, GB_W) * b_scale, qb_refs[g], scr_ref, dil, BF16)
            _to_residues(proj(OFF_KB + GB_W * g, GB_W), kb_refs[g], scr_ref, dil, BF16)
            _to_residues(proj(OFF_VB + GB_W * g, GB_W), vb_refs[g], scr_ref, dil, BF16)
        for j in range(n_gate_chunks):
            sl = slice(256 * j, 256 * j + 256)
            ga_ref[:, sl] = _sigmoid(proj(OFF_GA + 256 * j, 256) + bg_ref[:, sl])
            gb_ref[:, sl] = _sigmoid(proj(OFF_GA + d + 256 * j, 256) + bg_ref[:, d + 256 * j:d + 256 * j + 256])

    sd = jax.ShapeDtypeStruct
    outs = [sd((s, d), BF16), sd((s, QA_W), F32), sd((s, KA_W), F32), sd((s, QA_W), BF16), sd((s, KA_W), BF16),
            sd((s, KA_W), BF16)] + _dil_shapes(s, BF16) * 3 + [sd((s, d), F32), sd((s, d), F32)]
    out_specs = [_rows(tb, d), _rows(tb, QA_W), _rows(tb, KA_W), _rows(tb, QA_W), _rows(tb, KA_W), _rows(tb, KA_W)
                 ] + _dil_specs(tb) * 3 + [_rows(tb, d), _rows(tb, d)]
    in_specs = [_rows(tb, d), _rows(tb, LANES), _rows(tb, LANES), _rows(tb, LANES), _resident(w_in_t.shape),
                _resident(g_mix.shape), _resident(b_gate.shape), _resident(q_g.shape), _resident(k_g.shape), _ANY]
    res = list(pl.pallas_call(body, name="in_proj", grid=(s // tb,), in_specs=in_specs, out_specs=out_specs,
                              out_shape=outs, scratch_shapes=[pltpu.VMEM((2, tb, LANES), F32)],
                              compiler_params=_cparams(("arbitrary",)))(
        x, *tabs, w_in_t, g_mix, b_gate, q_g, k_g, after))
    return res[:6] + [res[6:9], res[9:12], res[12:15]] + res[15:]


def _attn_a_fwd(qrot, krot, va, tq, tk):
    s = qrot.shape[0]
    n_kv = s // tk
    gw = Q_PER_KV * HEAD_DIM_A

    def body(q_ref, k_ref, v_ref, o_ref, lse_ref):
        q4 = jnp.concatenate([q_ref[:, 128 * h:128 * h + 128] for h in range(Q_PER_KV)], axis=0)

        def step(j, carry):
            m, l, acc = carry
            sl = pl.ds(pl.multiple_of(j * tk, tk), tk)
            kj, vj = k_ref[sl, :], v_ref[sl, :]
            sc = _dot_nt(kj, q4)
            m_new = jnp.maximum(m, jnp.max(sc, axis=0, keepdims=True))
            p = jnp.exp2(sc - m_new)
            alpha = jnp.exp2(m - m_new)
            l = alpha * l + jnp.sum(p, axis=0, keepdims=True)
            acc = alpha * acc + _dot_tn(vj, p.astype(BF16))
            return m_new, l, acc

        rows = Q_PER_KV * tq
        m, l, acc = lax.fori_loop(0, n_kv, step, (jnp.full((1, rows), NEG_INF, F32), jnp.zeros((1, rows), F32),
                                                  jnp.zeros((HEAD_DIM_A, rows), F32)))
        o = (acc / l).T
        lse = m + jnp.log2(l)
        for h in range(Q_PER_KV):
            o_ref[:, 128 * h:128 * h + 128] = o[h * tq:(h + 1) * tq].astype(BF16)
            lse_ref[0, h:h + 1, :] = lse[:, h * tq:(h + 1) * tq]

    return pl.pallas_call(
        body, name="attn_a_fwd", grid=(N_KV_HEADS_A, s // tq),
        in_specs=[pl.BlockSpec((tq, gw), lambda g, i: (i, g)),
                  pl.BlockSpec((s, HEAD_DIM_A), lambda g, i: (0, g)),
                  pl.BlockSpec((s, HEAD_DIM_A), lambda g, i: (0, g))],
        out_specs=[pl.BlockSpec((tq, gw), lambda g, i: (i, g)),
                   pl.BlockSpec((1, Q_PER_KV, tq), lambda g, i: (g, 0, i))],
        out_shape=[jax.ShapeDtypeStruct((s, QA_W), BF16), jax.ShapeDtypeStruct((N_KV_HEADS_A, Q_PER_KV, s), F32)],
        compiler_params=_cparams(("arbitrary", "arbitrary")))(qrot, krot, va)


def _attn_a_bwd(qrot, krot, va, oa, doa, lse, tq, tk, after):
    s = qrot.shape[0]
    n_kv = s // tk
    gw = Q_PER_KV * HEAD_DIM_A

    def body(q_ref, do_ref, o_ref, lse_ref, k_ref, v_ref, after_ref, dq_ref, dk_ref, dv_ref):
        @pl.when(pl.program_id(1) == 0)
        def _():
            dk_ref[...] = jnp.zeros_like(dk_ref)
            dv_ref[...] = jnp.zeros_like(dv_ref)

        def stack(ref):
            return jnp.concatenate([ref[:, 128 * h:128 * h + 128] for h in range(Q_PER_KV)], axis=0)

        q4, do4, o4 = stack(q_ref), stack(do_ref), stack(o_ref)
        q4t, do4t = q4.T, do4.T
        delta = jnp.sum((do4.astype(F32) * o4.astype(F32)).T, axis=0, keepdims=True)
        lse4 = jnp.concatenate([lse_ref[0, h:h + 1, :] for h in range(Q_PER_KV)], axis=1)

        def step(j, dq):
            sl = pl.ds(pl.multiple_of(j * tk, tk), tk)
            kj, vj = k_ref[sl, :], v_ref[sl, :]
            p = jnp.exp2(_dot_nt(kj, q4) - lse4)
            ds = (p * (_dot_nt(vj, do4) - delta)).astype(BF16)
            dk_ref[:, sl] += _dot_nt(q4t, ds)
            dv_ref[:, sl] += _dot_nt(do4t, p.astype(BF16))
            return dq + _dot_tn(kj, ds)

        dq = lax.fori_loop(0, n_kv, step, jnp.zeros((HEAD_DIM_A, Q_PER_KV * tq), F32)).T
        for h in range(Q_PER_KV):
            dq_ref[:, 128 * h:128 * h + 128] = dq[h * tq:(h + 1) * tq]

    qspec = pl.BlockSpec((tq, gw), lambda g, i: (i, g))
    kspec = pl.BlockSpec((s, HEAD_DIM_A), lambda g, i: (0, g))
    ktspec = pl.BlockSpec((HEAD_DIM_A, s), lambda g, i: (g, 0))
    return pl.pallas_call(
        body, name="attn_a_bwd", grid=(N_KV_HEADS_A, s // tq),
        in_specs=[qspec, qspec, qspec, pl.BlockSpec((1, Q_PER_KV, tq), lambda g, i: (g, 0, i)), kspec, kspec, _ANY],
        out_specs=[qspec, ktspec, ktspec],
        out_shape=[jax.ShapeDtypeStruct((s, QA_W), F32), jax.ShapeDtypeStruct((KA_W, s), F32),
                   jax.ShapeDtypeStruct((KA_W, s), F32)],
        compiler_params=_cparams(("arbitrary", "arbitrary")))(qrot, doa, oa, lse, krot, va, after)


BAND_QB = 256
BAND_WIN = BAND_QB + 2 * BAND


def _band_specs(s, cb):
    per = cb // BAND
    last = s // BAND - 1
    cur = pl.BlockSpec((cb, GB_W), lambda i: (i, 0))
    prev = pl.BlockSpec((BAND, GB_W), lambda i: (jnp.maximum(i * per - 1, 0), 0))
    nxt = pl.BlockSpec((BAND, GB_W), lambda i: (jnp.minimum(i * per + per, last), 0))
    return cur, prev, nxt


def _window(prev_ref, cur_ref, next_ref):
    return jnp.concatenate([prev_ref[...], cur_ref[...], next_ref[...]], axis=0)


def _band_mask(base, seg_shift, window_rows):
    shape = (BAND_WIN, BAND_QB) if window_rows else (BAND_QB, BAND_WIN)
    a = lax.broadcasted_iota(jnp.int32, shape, 0)
    b = lax.broadcasted_iota(jnp.int32, shape, 1)
    rq, rk = (base - BAND + a, base + b) if window_rows else (base + a, base - BAND + b)
    same_segment = lax.shift_right_arithmetic(rq, jnp.int32(seg_shift)) == lax.shift_right_arithmetic(rk, jnp.int32(seg_shift))
    return (jnp.abs(rk - rq) <= BAND) & same_segment


def _build_bias(bmap_ref, tab_ref, bias_ref):
    bm = bmap_ref[...]
    acc = [jnp.full(bm.shape, NEG_INF, F32) for _ in range(N_HEADS_PER_DIL)]
    for b in range(N_REL_BUCKETS):
        hit = bm == b
        for h in range(N_HEADS_PER_DIL):
            acc[h] = jnp.where(hit, tab_ref[b, h] * LOG2_E, acc[h])
    for h in range(N_HEADS_PER_DIL):
        bias_ref[h] = acc[h]


def _segment_mask(base, seg_len, seg_shift, window_rows):
    if seg_len % BAND_QB:
        return _band_mask(base, seg_shift, window_rows)
    pos = lax.rem(base, seg_len)
    shape, dim = ((BAND_WIN, 1), 0) if window_rows else ((1, BAND_WIN), 1)
    w = lax.broadcasted_iota(jnp.int32, shape, dim)
    return ((w >= BAND) | (pos != 0)) & ((w < BAND + BAND_QB) | (pos != seg_len - BAND_QB))


def _head_lane_masks():
    lane = lax.broadcasted_iota(jnp.int32, (1, LANES), 1)
    return [lane < HEAD_DIM_B, lane >= HEAD_DIM_B]


def _seg_shift(s, dil):
    seg = s // dil
    assert seg & (seg - 1) == 0, "segment length must be a power of two"
    return seg.bit_length() - 1


def _band_fwd(dil, qb, kb, vb, bmap, tab, cb):
    s = qb.shape[0]
    shift = _seg_shift(s, dil)

    def body(q_ref, kp_ref, kc_ref, kn_ref, vp_ref, vc_ref, vn_ref, bmap_ref, tab_ref, o_ref, lse_ref, bias_ref):
        @pl.when(pl.program_id(0) == 0)
        def _():
            _build_bias(bmap_ref, tab_ref, bias_ref)

        kw, vw = _window(kp_ref, kc_ref, kn_ref), _window(vp_ref, vc_ref, vn_ref)
        hm = _head_lane_masks()
        for jj in range(cb // BAND_QB):
            r0 = BAND_QB * jj
            mask = _segment_mask(pl.program_id(0) * cb + r0, s // dil, shift, False)
            for hp in range(2):
                ls = slice(LANES * hp, LANES * hp + LANES)
                qh = q_ref[r0:r0 + BAND_QB, ls]
                k3, v3 = kw[r0:r0 + BAND_WIN, ls], vw[r0:r0 + BAND_WIN, ls]
                o_half = jnp.zeros((BAND_QB, LANES), F32)
                lse_half = jnp.zeros((BAND_QB, LANES), F32)
                for hh in range(2):
                    sc = _dot_nt(jnp.where(hm[hh], qh, jnp.zeros_like(qh)), k3) + bias_ref[2 * hp + hh]
                    sc = jnp.where(mask, sc, NEG_INF)
                    m = jnp.max(sc, axis=-1, keepdims=True)
                    e = jnp.exp2(sc - m)
                    l = jnp.sum(e, axis=-1, keepdims=True)
                    o_half = o_half + _dot_nn(e.astype(BF16), jnp.where(hm[hh], v3, jnp.zeros_like(v3))) * (1.0 / l)
                    lse_half = jnp.where(hm[hh], m + jnp.log2(l), lse_half)
                o_ref[r0:r0 + BAND_QB, ls] = o_half
                lse_ref[r0:r0 + BAND_QB, ls] = lse_half

    cur, prev, nxt = _band_specs(s, cb)
    return pl.pallas_call(
        body, name=f"band_fwd_d{dil}", grid=(s // cb,),
        in_specs=[cur, prev, cur, nxt, prev, cur, nxt, _resident(bmap.shape), pl.BlockSpec(memory_space=pltpu.SMEM)],
        out_specs=[cur, cur],
        out_shape=[jax.ShapeDtypeStruct(qb.shape, F32), jax.ShapeDtypeStruct(qb.shape, F32)],
        scratch_shapes=[pltpu.VMEM((N_HEADS_PER_DIL, BAND_QB, BAND_WIN), F32)],
        compiler_params=_cparams(("arbitrary",)))(qb, kb, kb, kb, vb, vb, vb, bmap, tab)


def _band_bwd_q(dil, qb, kb, vb, dob, lse, dd, bmap, tab, cb):
    s = qb.shape[0]
    shift = _seg_shift(s, dil)
    n_steps = s // cb

    def body(q_ref, do_ref, lse_ref, dd_ref, kp_ref, kc_ref, kn_ref, vp_ref, vc_ref, vn_ref, bmap_ref, tab_ref,
             dq_ref, dtab_ref, bias_ref, dsum_ref):
        @pl.when(pl.program_id(0) == 0)
        def _():
            _build_bias(bmap_ref, tab_ref, bias_ref)
            dsum_ref[...] = jnp.zeros_like(dsum_ref)

        kw, vw = _window(kp_ref, kc_ref, kn_ref), _window(vp_ref, vc_ref, vn_ref)
        hm = _head_lane_masks()
        for jj in range(cb // BAND_QB):
            r0 = BAND_QB * jj
            mask = _segment_mask(pl.program_id(0) * cb + r0, s // dil, shift, False)
            for hp in range(2):
                ls = slice(LANES * hp, LANES * hp + LANES)
                qh, doh = q_ref[r0:r0 + BAND_QB, ls], do_ref[r0:r0 + BAND_QB, ls]
                k3, v3 = kw[r0:r0 + BAND_WIN, ls], vw[r0:r0 + BAND_WIN, ls]
                dq_half = jnp.zeros((BAND_QB, LANES), F32)
                for hh in range(2):
                    h = 2 * hp + hh
                    col = LANES * hp + HEAD_DIM_B * hh
                    sc = _dot_nt(jnp.where(hm[hh], qh, jnp.zeros_like(qh)), k3) + bias_ref[h]
                    sc = jnp.where(mask, sc, NEG_INF)
                    p = jnp.exp2(sc - lse_ref[r0:r0 + BAND_QB, col:col + 1])
                    dp = _dot_nt(jnp.where(hm[hh], doh, jnp.zeros_like(doh)), v3)
                    ds = p * (dp - dd_ref[r0:r0 + BAND_QB, col:col + 1])
                    dsum_ref[h] += ds
                    dq_half = dq_half + _dot_nn(ds.astype(BF16), jnp.where(hm[hh], k3, jnp.zeros_like(k3)))
                dq_ref[r0:r0 + BAND_QB, ls] = dq_half

        @pl.when(pl.program_id(0) == n_steps - 1)
        def _():
            bm = bmap_ref[...]
            lane = lax.broadcasted_iota(jnp.int32, (1, LANES), 1)
            for b in range(N_REL_BUCKETS):
                hit = bm == b
                row = jnp.zeros((1, LANES), F32)
                for h in range(N_HEADS_PER_DIL):
                    row = jnp.where(lane == h, jnp.sum(jnp.where(hit, dsum_ref[h], 0.0)), row)
                dtab_ref[b:b + 1, :] = row

    cur, prev, nxt = _band_specs(s, cb)
    return pl.pallas_call(
        body, name=f"band_bwd_q_d{dil}", grid=(n_steps,),
        in_specs=[cur, cur, cur, cur, prev, cur, nxt, prev, cur, nxt, _resident(bmap.shape),
                  pl.BlockSpec(memory_space=pltpu.SMEM)],
        out_specs=[cur, _acc_spec((N_REL_BUCKETS, LANES))],
        out_shape=[jax.ShapeDtypeStruct(qb.shape, F32), jax.ShapeDtypeStruct((N_REL_BUCKETS, LANES), F32)],
        scratch_shapes=[pltpu.VMEM((N_HEADS_PER_DIL, BAND_QB, BAND_WIN), F32),
                        pltpu.VMEM((N_HEADS_PER_DIL, BAND_QB, BAND_WIN), F32)],
        compiler_params=_cparams(("arbitrary",)))(qb, dob, lse, dd, kb, kb, kb, vb, vb, vb, bmap, tab)


def _band_bwd_kv(dil, qb, kb, vb, dob, lse, dd, bmap_t, tab, cb):
    s = qb.shape[0]
    shift = _seg_shift(s, dil)

    def body(k_ref, v_ref, qp_ref, qc_ref, qn_ref, dp_ref, dc_ref, dn_ref, lp_ref, lc_ref, ln_ref,
             ep_ref, ec_ref, en_ref, bmap_ref, tab_ref, dk_ref, dv_ref, bias_ref):
        @pl.when(pl.program_id(0) == 0)
        def _():
            _build_bias(bmap_ref, tab_ref, bias_ref)

        qw, dow = _window(qp_ref, qc_ref, qn_ref), _window(dp_ref, dc_ref, dn_ref)
        lw, ew = _window(lp_ref, lc_ref, ln_ref), _window(ep_ref, ec_ref, en_ref)
        hm = _head_lane_masks()
        for jj in range(cb // BAND_QB):
            r0 = BAND_QB * jj
            mask = _segment_mask(pl.program_id(0) * cb + r0, s // dil, shift, True)
            for hp in range(2):
                ls = slice(LANES * hp, LANES * hp + LANES)
                kh, vh = k_ref[r0:r0 + BAND_QB, ls], v_ref[r0:r0 + BAND_QB, ls]
                q3, do3 = qw[r0:r0 + BAND_WIN, ls], dow[r0:r0 + BAND_WIN, ls]
                dk_half = jnp.zeros((BAND_QB, LANES), F32)
                dv_half = jnp.zeros((BAND_QB, LANES), F32)
                for hh in range(2):
                    col = LANES * hp + HEAD_DIM_B * hh
                    q3m = jnp.where(hm[hh], q3, jnp.zeros_like(q3))
                    do3m = jnp.where(hm[hh], do3, jnp.zeros_like(do3))
                    sc = _dot_nt(q3m, kh) + bias_ref[2 * hp + hh]
                    sc = jnp.where(mask, sc, NEG_INF)
                    p = jnp.exp2(sc - lw[r0:r0 + BAND_WIN, col:col + 1])
                    ds = p * (_dot_nt(do3m, vh) - ew[r0:r0 + BAND_WIN, col:col + 1])
                    dk_half = dk_half + _dot_tn(ds.astype(BF16), q3m)
                    dv_half = dv_half + _dot_tn(p.astype(BF16), do3m)
                dk_ref[r0:r0 + BAND_QB, ls] = dk_half
                dv_ref[r0:r0 + BAND_QB, ls] = dv_half

    cur, prev, nxt = _band_specs(s, cb)
    win = [prev, cur, nxt]
    return pl.pallas_call(
        body, name=f"band_bwd_kv_d{dil}", grid=(s // cb,),
        in_specs=[cur, cur] + win * 4 + [_resident(bmap_t.shape), pl.BlockSpec(memory_space=pltpu.SMEM)],
        out_specs=[cur, cur],
        out_shape=[jax.ShapeDtypeStruct(qb.shape, F32), jax.ShapeDtypeStruct(qb.shape, F32)],
        scratch_shapes=[pltpu.VMEM((N_HEADS_PER_DIL, BAND_WIN, BAND_QB), F32)],
        compiler_params=_cparams(("arbitrary",)))(
        kb, vb, qb, qb, qb, dob, dob, dob, lse, lse, lse, dd, dd, dd, bmap_t, tab)


def _t5_bucket(rel):
    nb = N_REL_BUCKETS // 2
    ret = (rel > 0).astype(np.int32) * nb
    n = np.abs(rel)
    max_exact = nb // 2
    large = max_exact + (np.log(np.maximum(n, 1) / max_exact) / math.log(REL_MAX_DIST / max_exact)
                         * (nb - max_exact)).astype(np.int32)
    large = np.minimum(large, nb - 1)
    return ret + np.where(n < max_exact, n, large).astype(np.int32)


def _bucket_maps(dil):
    off_qk = np.arange(BAND_WIN)[None, :] - BAND - np.arange(BAND_QB)[:, None]
    off_kq = np.arange(BAND_QB)[None, :] + BAND - np.arange(BAND_WIN)[:, None]
    return [np.where(np.abs(off) <= BAND, _t5_bucket(off * dil), -1).astype(np.int32) for off in (off_qk, off_kq)]


def _seg_sum(v):
    lane = lax.broadcasted_iota(jnp.int32, (1, v.shape[1]), 1)
    out = jnp.zeros_like(v)
    for h in range(v.shape[1] // HEAD_DIM_B):
        m = (lane >= HEAD_DIM_B * h) & (lane < HEAD_DIM_B * (h + 1))
        out = jnp.where(m, jnp.sum(jnp.where(m, v, 0.0), axis=-1, keepdims=True), out)
    return out


def _mix_out(x, oa, og, lg, ga, gb, w_oa, w_ob_t, w_o, tb):
    s, d = x.shape

    def body(x_ref, oa_ref, og0_ref, og1_ref, og2_ref, lg0_ref, lg1_ref, lg2_ref, ga_ref, gb_ref,
             woa_ref, wob_ref, wo_ref, x2_ref, ob_ref, lse0_ref, lse1_ref, lse2_ref, ya_ref, yb_ref, u_ref, scr_ref):
        og_refs, lg_refs = (og0_ref, og1_ref, og2_ref), (lg0_ref, lg1_ref, lg2_ref)
        l0, l1, l2 = [_from_residues(lg_refs[g], scr_ref, dil) for g, dil in enumerate(DILATIONS)]
        lmax = jnp.maximum(jnp.maximum(l0, l1), l2)
        w0, w1, w2 = jnp.exp2(l0 - lmax), jnp.exp2(l1 - lmax), jnp.exp2(l2 - lmax)
        den = w0 + w1 + w2
        o0, o1, o2 = [_from_residues(og_refs[g], scr_ref, dil) for g, dil in enumerate(DILATIONS)]
        ob = ((w0 * o0 + w1 * o1 + w2 * o2) / den).astype(BF16)
        ob_ref[...] = ob
        lse = lmax + jnp.log2(den)
        for g, (dil, ref) in enumerate(zip(DILATIONS, (lse0_ref, lse1_ref, lse2_ref))):
            _to_residues(lse, ref, scr_ref, dil, F32)
        ya = _dot_nn(oa_ref[...], woa_ref[...])
        yb = _dot_nt(ob, wob_ref[...])
        ya_ref[...] = ya.astype(BF16)
        yb_ref[...] = yb.astype(BF16)
        u = (ga_ref[...] * ya + gb_ref[...] * yb).astype(BF16)
        u_ref[...] = u
        x2_ref[...] = x_ref[...] + _dot_nn(u, wo_ref[...])

    sd = jax.ShapeDtypeStruct
    res = list(pl.pallas_call(
        body, name="mix_out", grid=(s // tb,),
        in_specs=[_rows(tb, d), _rows(tb, QA_W)] + _dil_specs(tb) * 2 + [
            _rows(tb, d), _rows(tb, d), _resident(w_oa.shape), _resident(w_ob_t.shape), _resident(w_o.shape)],
        out_specs=[_rows(tb, d), _rows(tb, GB_W)] + _dil_specs(tb) + [_rows(tb, d), _rows(tb, d), _rows(tb, d)],
        out_shape=[sd((s, d), F32), sd((s, GB_W), BF16)] + _dil_shapes(s, F32) + [
            sd((s, d), BF16), sd((s, d), BF16), sd((s, d), BF16)],
        scratch_shapes=[pltpu.VMEM((2, tb, LANES), F32)],
        compiler_params=_cparams(("arbitrary",)))(x, oa, *og, *lg, ga, gb, w_oa, w_ob_t, w_o))
    return res[:2] + [res[2:5]] + res[5:]


def _mlp_fwd(x2, w1_t, w2, g_mlp, tb, tc):
    s, d = x2.shape
    dff = w1_t.shape[0]

    def body(x_ref, w1_ref, w2_ref, g_ref, x3_ref, r_ref, h_ref):
        xv = x_ref[...]
        hb = (xv * _rstd(xv) * g_ref[...]).astype(BF16)
        h_ref[...] = hb
        x3_ref[...] = xv
        for c in range(dff // tc):
            sl = slice(tc * c, tc * c + tc)
            r = jnp.maximum(_dot_nt(hb, w1_ref[sl, :]), 0.0)
            r_ref[:, sl] = r.astype(BF16)
            x3_ref[...] += _dot_nn((r * r).astype(BF16), w2_ref[sl, :])

    sd = jax.ShapeDtypeStruct
    return pl.pallas_call(
        body, name="mlp_fwd", grid=(s // tb,),
        in_specs=[_rows(tb, d), _resident(w1_t.shape), _resident(w2.shape), _resident(g_mlp.shape)],
        out_specs=[_rows(tb, d), _rows(tb, dff), _rows(tb, d)],
        out_shape=[sd((s, d), F32), sd((s, dff), BF16), sd((s, d), BF16)],
        compiler_params=_cparams(("arbitrary",)))(x2, w1_t, w2, g_mlp)


def _ple_loss(x3, p, target, w_pg, w_p_t, g_ple, g_fin, tb):
    s, d = x3.shape
    dp = p.shape[1]

    def body(x_ref, p_ref, t_ref, wpg_ref, wp_ref, gple_ref, gfin_ref,
             dx3_ref, h3_ref, dpre_ref, dpe_ref, pb_ref, loss_ref, dgfin_ref, dgple_ref):
        @pl.when(pl.program_id(0) == 0)
        def _():
            loss_ref[...] = jnp.zeros_like(loss_ref)
            dgfin_ref[...] = jnp.zeros_like(dgfin_ref)
            dgple_ref[...] = jnp.zeros_like(dgple_ref)

        x3v = x_ref[...]
        r3 = _rstd(x3v)
        n3 = x3v * r3
        h3 = (n3 * gple_ref[...]).astype(BF16)
        h3_ref[...] = h3
        gp = _sigmoid(_dot_nn(h3, wpg_ref[...]))
        pb = p_ref[...].astype(BF16)
        pb_ref[...] = pb
        pe = _dot_nt(pb, wp_ref[...])
        x4 = x3v + gp * pe
        r4 = _rstd(x4)
        n4 = x4 * r4
        err = n4 * gfin_ref[...] - t_ref[...]
        loss_ref[...] += jnp.sum(0.5 * jnp.mean(err * err, axis=-1, keepdims=True), axis=0, keepdims=True)
        dy = err / d
        dgfin_ref[...] += _colsum(dy * n4)
        dx4 = _rms_bwd(dy, n4, r4, gfin_ref[...])
        dpe_ref[...] = (dx4 * gp).astype(BF16)
        dpre = (dx4 * pe * gp * (1.0 - gp)).astype(BF16)
        dpre_ref[...] = dpre
        dh3 = _dot_nt(dpre, wpg_ref[...])
        dgple_ref[...] += _colsum(dh3 * n3)
        dx3_ref[...] = dx4 + _rms_bwd(dh3, n3, r3, gple_ref[...])

    sd = jax.ShapeDtypeStruct
    return pl.pallas_call(
        body, name="ple_loss", grid=(s // tb,),
        in_specs=[_rows(tb, d), _rows(tb, dp), _rows(tb, d), _resident(w_pg.shape), _resident(w_p_t.shape),
                  _resident(g_ple.shape), _resident(g_fin.shape)],
        out_specs=[_rows(tb, d), _rows(tb, d), _rows(tb, d), _rows(tb, d), _rows(tb, dp),
                   _acc_spec((1, LANES)), _acc_spec((1, d)), _acc_spec((1, d))],
        out_shape=[sd((s, d), F32), sd((s, d), BF16), sd((s, d), BF16), sd((s, d), BF16), sd((s, dp), BF16),
                   sd((1, LANES), F32), sd((1, d), F32), sd((1, d), F32)],
        compiler_params=_cparams(("arbitrary",)))(x3, p, target, w_pg, w_p_t, g_ple, g_fin)


def _mlp_bwd(dx3, x2, r, w1_t, w2, g_mlp, tb, tc):
    s, d = x2.shape
    dff = w1_t.shape[0]

    def body(dx3_ref, x_ref, r_ref, w1_ref, w2_ref, g_ref, dx2_ref, df_ref, dg_ref, dh_ref):
        @pl.when(pl.program_id(0) == 0)
        def _():
            dg_ref[...] = jnp.zeros_like(dg_ref)

        dx3v = dx3_ref[...]
        dx3b = dx3v.astype(BF16)
        dh_ref[...] = jnp.zeros_like(dh_ref)
        for c in range(dff // tc):
            sl = slice(tc * c, tc * c + tc)
            df = (_dot_nt(dx3b, w2_ref[sl, :]) * (2.0 * r_ref[:, sl].astype(F32))).astype(BF16)
            df_ref[:, sl] = df
            dh_ref[...] += _dot_nn(df, w1_ref[sl, :])
        xv = x_ref[...]
        r2 = _rstd(xv)
        n2 = xv * r2
        dh = dh_ref[...]
        dg_ref[...] += _colsum(dh * n2)
        dx2_ref[...] = dx3v + _rms_bwd(dh, n2, r2, g_ref[...])

    sd = jax.ShapeDtypeStruct
    return pl.pallas_call(
        body, name="mlp_bwd", grid=(s // tb,),
        in_specs=[_rows(tb, d), _rows(tb, d), _rows(tb, dff), _resident(w1_t.shape), _resident(w2.shape),
                  _resident(g_mlp.shape)],
        out_specs=[_rows(tb, d), _rows(tb, dff), _acc_spec((1, d))],
        out_shape=[sd((s, d), F32), sd((s, dff), BF16), sd((1, d), F32)],
        scratch_shapes=[pltpu.VMEM((tb, d), F32)],
        compiler_params=_cparams(("arbitrary",)))(dx3, x2, r, w1_t, w2, g_mlp)


def _mix_out_bwd(dx2, ya, yb, ga, gb, ob, w_oa, w_ob_t, w_o, tb, after):
    s, d = dx2.shape

    def body(dx_ref, ya_ref, yb_ref, ga_ref, gb_ref, ob_ref, woa_ref, wob_ref, wo_ref, after_ref,
             doa_ref, dob0_ref, dob1_ref, dob2_ref, dd0_ref, dd1_ref, dd2_ref, dga_ref, dgb_ref, dya_ref, dyb_ref,
             dbg_ref, scr_ref):
        @pl.when(pl.program_id(0) == 0)
        def _():
            dbg_ref[...] = jnp.zeros_like(dbg_ref)

        du = _dot_nt(dx_ref[...].astype(BF16), wo_ref[...])
        gav, gbv = ga_ref[...], gb_ref[...]
        dya = (du * gav).astype(BF16)
        dyb = (du * gbv).astype(BF16)
        dya_ref[...] = dya
        dyb_ref[...] = dyb
        dga = du * ya_ref[...].astype(F32) * gav * (1.0 - gav)
        dgb = du * yb_ref[...].astype(F32) * gbv * (1.0 - gbv)
        dga_ref[...] = dga.astype(BF16)
        dgb_ref[...] = dgb.astype(BF16)
        dbg_ref[:, 0:d] += _colsum(dga)
        dbg_ref[:, d:2 * d] += _colsum(dgb)
        doa_ref[...] = _dot_nt(dya, woa_ref[...]).astype(BF16)
        dob = _dot_nn(dyb, wob_ref[...])
        dd = _seg_sum(dob * ob_ref[...].astype(F32))
        for dil, dob_ref, dd_ref in zip(DILATIONS, (dob0_ref, dob1_ref, dob2_ref), (dd0_ref, dd1_ref, dd2_ref)):
            _to_residues(dob, dob_ref, scr_ref, dil, BF16)
            _to_residues(dd, dd_ref, scr_ref, dil, F32)

    sd = jax.ShapeDtypeStruct
    res = list(pl.pallas_call(
        body, name="mix_out_bwd", grid=(s // tb,),
        in_specs=[_rows(tb, d)] * 5 + [_rows(tb, GB_W), _resident(w_oa.shape), _resident(w_ob_t.shape),
                                       _resident(w_o.shape), _ANY],
        out_specs=[_rows(tb, QA_W)] + _dil_specs(tb) * 2 + [_rows(tb, d), _rows(tb, d), _rows(tb, d),
                                                           _rows(tb, d), _acc_spec((1, 2 * d))],
        out_shape=[sd((s, QA_W), BF16)] + _dil_shapes(s, BF16) + _dil_shapes(s, F32) + [
            sd((s, d), BF16), sd((s, d), BF16), sd((s, d), BF16), sd((s, d), BF16), sd((1, 2 * d), F32)],
        scratch_shapes=[pltpu.VMEM((2, tb, LANES), F32)],
        compiler_params=_cparams(("arbitrary",)))(dx2, ya, yb, ga, gb, ob, w_oa, w_ob_t, w_o, after))
    return res[:1] + [res[1:4], res[4:7]] + res[7:]


def _in_proj_bwd(dx2, x, dqrot, dkrot, dva, qraw, kraw, tabs, dqb, dkb, dvb, dga, dgb, w_in_t, g_mix, q_g, k_g, tb):
    s, d = x.shape
    din = w_in_t.shape[0]
    q_scale = HEAD_DIM_A ** -0.5
    b_scale = HEAD_DIM_B ** -0.5
    tc = 256

    def body(dx2_ref, x_ref, dq_ref, dk_ref, dv_ref, qraw_ref, kraw_ref, c_ref, s1_ref, s2_ref, *rest):
        dqb_refs, dkb_refs, dvb_refs = rest[0:3], rest[3:6], rest[6:9]
        (dga_ref, dgb_ref, w_ref, gmix_ref, qg_ref, kg_ref,
         dx_ref, dz_ref, dgmix_ref, dqg_ref, dkg_ref, dh_ref, scr_ref) = rest[9:]

        @pl.when(pl.program_id(0) == 0)
        def _():
            dgmix_ref[...] = jnp.zeros_like(dgmix_ref)
            dqg_ref[...] = jnp.zeros_like(dqg_ref)
            dkg_ref[...] = jnp.zeros_like(dkg_ref)

        cos, s1, s2 = c_ref[...], s1_ref[...], s2_ref[...]

        def head_bwd(drot, z, g_ref, acc_ref):
            dn = _rope_bwd(drot, cos, s1, s2)
            rr = _rstd(z)
            nn = z * rr
            acc_ref[...] += _colsum(dn * nn)
            return _rms_bwd(dn, nn, rr, g_ref[...])

        for h in range(N_Q_HEADS_A):
            sl = slice(128 * h, 128 * h + 128)
            dz_ref[:, OFF_QA + 128 * h:OFF_QA + 128 * h + 128] = head_bwd(
                dq_ref[:, sl] * q_scale, qraw_ref[:, sl], qg_ref, dqg_ref).astype(BF16)
        for h in range(N_KV_HEADS_A):
            sl = slice(128 * h, 128 * h + 128)
            dz_ref[:, OFF_KA + 128 * h:OFF_KA + 128 * h + 128] = head_bwd(
                dk_ref[sl, :].T * LN_2, kraw_ref[:, sl], kg_ref, dkg_ref).astype(BF16)
        dz_ref[:, OFF_VA:OFF_VA + KA_W] = dv_ref[...].T.astype(BF16)
        for g, dil in enumerate(DILATIONS):
            dz_ref[:, OFF_QB + GB_W * g:OFF_QB + GB_W * (g + 1)] = (
                _from_residues(dqb_refs[g], scr_ref, dil) * b_scale).astype(BF16)
            dz_ref[:, OFF_KB + GB_W * g:OFF_KB + GB_W * (g + 1)] = (
                _from_residues(dkb_refs[g], scr_ref, dil) * LN_2).astype(BF16)
            dz_ref[:, OFF_VB + GB_W * g:OFF_VB + GB_W * (g + 1)] = _from_residues(dvb_refs[g], scr_ref, dil).astype(BF16)
        dz_ref[:, OFF_GA:OFF_GA + d] = dga_ref[...]
        dz_ref[:, OFF_GA + d:OFF_GA + 2 * d] = dgb_ref[...]
        dh_ref[...] = jnp.zeros_like(dh_ref)
        for c in range(din // tc):
            sl = slice(tc * c, tc * c + tc)
            dh_ref[...] += _dot_nn(dz_ref[:, sl], w_ref[sl, :])
        xv = x_ref[...]
        r1 = _rstd(xv)
        n1 = xv * r1
        dh = dh_ref[...]
        dgmix_ref[...] += _colsum(dh * n1)
        dx_ref[...] = dx2_ref[...] + _rms_bwd(dh, n1, r1, gmix_ref[...])

    sd = jax.ShapeDtypeStruct
    return pl.pallas_call(
        body, name="in_proj_bwd", grid=(s // tb,),
        in_specs=[_rows(tb, d), _rows(tb, d), _rows(tb, QA_W), pl.BlockSpec((KA_W, tb), lambda i: (0, i)),
                  pl.BlockSpec((KA_W, tb), lambda i: (0, i)), _rows(tb, QA_W),
                  _rows(tb, KA_W), _rows(tb, LANES), _rows(tb, LANES), _rows(tb, LANES),
                  ] + _dil_specs(tb) * 3 + [_rows(tb, d), _rows(tb, d),
                  _resident(w_in_t.shape), _resident(g_mix.shape), _resident(q_g.shape), _resident(k_g.shape)],
        out_specs=[_rows(tb, d), _rows(tb, din), _acc_spec((1, d)), _acc_spec((1, HEAD_DIM_A)),
                   _acc_spec((1, HEAD_DIM_A))],
        out_shape=[sd((s, d), F32), sd((s, din), BF16), sd((1, d), F32), sd((1, HEAD_DIM_A), F32),
                   sd((1, HEAD_DIM_A), F32)],
        scratch_shapes=[pltpu.VMEM((tb, d), F32), pltpu.VMEM((2, tb, LANES), F32)],
        compiler_params=_cparams(("arbitrary",)))(
        dx2, x, dqrot, dkrot, dva, qraw, kraw, *tabs, *dqb, *dkb, *dvb, dga, dgb, w_in_t, g_mix, q_g, k_g)


def _identity(v):
    return v


def _to_bf16(v):
    return v.astype(BF16)


def _square_bf16(v):
    vf = v.astype(F32)
    return (vf * vf).astype(BF16)


def _weight_grad(name, a, b, ti, tj, tk, a_fn=_identity, b_fn=_identity, col0=0, n=None, after=None):
    t, m = a.shape
    n = b.shape[1] if n is None else n
    n_k = t // tk
    after = a if after is None else after

    def body(a_ref, b_ref, after_ref, o_ref, acc_ref):
        k = pl.program_id(2)

        @pl.when(k == 0)
        def _():
            acc_ref[...] = jnp.zeros_like(acc_ref)

        acc_ref[...] += _dot_tn(a_fn(a_ref[...]), b_fn(b_ref[...]))

        @pl.when(k == n_k - 1)
        def _():
            o_ref[...] = acc_ref[...].astype(BF16)

    return pl.pallas_call(
        body, name=name, grid=(m // ti, n // tj, n_k),
        in_specs=[pl.BlockSpec((tk, ti), lambda i, j, k: (k, i)),
                  pl.BlockSpec((tk, tj), lambda i, j, k: (k, j + col0 // tj)), _ANY],
        out_specs=pl.BlockSpec((ti, tj), lambda i, j, k: (i, j)),
        out_shape=jax.ShapeDtypeStruct((m, n), BF16),
        scratch_shapes=[pltpu.VMEM((ti, tj), F32)],
        compiler_params=_cparams(("arbitrary", "arbitrary", "arbitrary")))(a, b, after)


def _sum_slots(name, recv, own):
    m, n, k = recv.shape
    tc = min(k, 256)

    def body(own_ref, r_ref, o_ref):
        acc = own_ref[...].astype(F32)
        for i in range(m):
            acc = acc + r_ref[i].astype(F32)
        o_ref[...] = acc

    return pl.pallas_call(
        body, name=name, grid=(k // tc,),
        in_specs=[pl.BlockSpec((n, tc), lambda j: (0, j)), pl.BlockSpec((m, n, tc), lambda j: (0, 0, j))],
        out_specs=pl.BlockSpec((n, tc), lambda j: (0, j)),
        out_shape=jax.ShapeDtypeStruct((n, k), F32),
        compiler_params=_cparams(("arbitrary",)))(own, recv)


def _adamw_math(w, g, m, v):
    m = ADAM_B1 * m + (1.0 - ADAM_B1) * g
    v = ADAM_B2 * v + (1.0 - ADAM_B2) * (g * g)
    m_hat = m / (1.0 - ADAM_B1 ** ADAM_STEP)
    v_hat = v / (1.0 - ADAM_B2 ** ADAM_STEP)
    delta = -ADAM_LR * (m_hat / (jnp.sqrt(v_hat) + ADAM_EPS) + ADAM_WD * w)
    return delta, m, v


def _adamw(name, w, g, m, v):
    r, c = w.shape
    tr = min(r, 256)

    def body(w_ref, g_ref, m_ref, v_ref, d_ref, mo_ref, vo_ref):
        d_ref[...], mo_ref[...], vo_ref[...] = _adamw_math(w_ref[...], g_ref[...], m_ref[...], v_ref[...])

    spec = pl.BlockSpec((tr, c), lambda i: (i, 0))
    return pl.pallas_call(
        body, name=name, grid=(r // tr,), in_specs=[spec] * 4, out_specs=[spec] * 3,
        out_shape=[jax.ShapeDtypeStruct((r, c), F32)] * 3,
        compiler_params=_cparams(("arbitrary",)))(w, g, m, v)


def _small_update(parts, w, m, v):
    def body(p_ref, w_ref, m_ref, v_ref, g_ref, d_ref, mo_ref, vo_ref):
        g = p_ref[0]
        for i in range(1, N_DEV):
            g = g + p_ref[i]
        g_ref[...] = g
        d_ref[...], mo_ref[...], vo_ref[...] = _adamw_math(w_ref[...], g, m_ref[...], v_ref[...])

    return pl.pallas_call(body, name="small_update", out_shape=[jax.ShapeDtypeStruct(w.shape, F32)] * 4)(
        parts, w, m, v)


def _pack_rows(vectors, n_rows):
    flat = jnp.concatenate([v.reshape(-1).astype(F32) for v in vectors])
    flat = jnp.pad(flat, (0, n_rows * LANES - flat.shape[0]))
    return flat.reshape(n_rows, LANES)


def _pick_tile(n, prefs):
    for t in prefs:
        if n % t == 0:
            return t
    return n


def kernel(x, p, norm_mix_g, w_in, b_gate, q_norm_g, k_norm_g, rel_bias, w_out_a, w_out_b, w_out, norm_mlp_g, w_ff1, w_ff2, norm_ple_g, w_ple_gate, w_ple, final_norm_g, loss_target, m_norm_mix_g, m_w_in, m_b_gate, m_q_norm_g, m_k_norm_g, m_rel_bias, m_w_out_a, m_w_out_b, m_w_out, m_norm_mlp_g, m_w_ff1, m_w_ff2, m_norm_ple_g, m_w_ple_gate, m_w_ple, m_final_norm_g, v_norm_mix_g, v_w_in, v_b_gate, v_q_norm_g, v_k_norm_g, v_rel_bias, v_w_out_a, v_w_out_b, v_w_out, v_norm_mlp_g, v_w_ff1, v_w_ff2, v_norm_ple_g, v_w_ple_gate, v_w_ple, v_final_norm_g):
    s, d = x.shape[1], x.shape[2]
    xs, ps, ts = x[0], p[0, 0], loss_target[0]
    tb = _pick_tile(s, (512, 256))
    tq = _pick_tile(s, (256,))
    tk = _pick_tile(s, (1024, 512))
    cb = _pick_tile(s, (512,))
    fin_g = final_norm_g.reshape(1, d)

    col_sharded = {"w_in": w_in[0], "w_out_b": w_out_b[0], "w_ff1": w_ff1[0], "w_ple": w_ple[0]}
    row_sharded = {"w_out_a": w_out_a[0], "w_out": w_out[0], "w_ff2": w_ff2[0], "w_ple_gate": w_ple_gate[0]}
    order = ["w_in", "w_out_a", "w_out_b", "w_out", "w_ff1", "w_ff2", "w_ple_gate", "w_ple"]
    shards = [(col_sharded[n].T if n in col_sharded else row_sharded[n]).astype(BF16) for n in order]
    my_idx = 4 * lax.axis_index("x") + 2 * lax.axis_index("y") + lax.axis_index("c")
    (w_in_t,) = _all_gather(shards[:1])
    zones = [lax.dynamic_update_slice(lax.empty((N_DEV * sh.shape[0], sh.shape[1]), BF16), sh,
                                      (my_idx * sh.shape[0], 0)) for sh in shards[1:]]
    ag = _copies_start("weights_gather_start", shards[1:], zones, w_in_t, True)

    tabs = _rope_tables(s)
    (h1, qraw, kraw, qrot, krot, va, qb, kb, vb, ga, gb) = _in_proj(
        xs, tabs, w_in_t, norm_mix_g, b_gate, q_norm_g, k_norm_g, tb, ag[4])
    oa, lse_a = _attn_a_fwd(qrot, krot, va, tq, tk)
    _, (w_oa, w_ob_t, w_o, w_ff1_t, w_ff2_f, w_pg, w_p_t) = _copies_wait(
        "weights_gather_wait", ag[0], ag[1], ag[2], ag[3], lse_a, True)
    flat = lambda arrs: [a.reshape(s, GB_W) for a in arrs]
    split = lambda arrs: [a.reshape(dil, s // dil, GB_W) for a, dil in zip(arrs, DILATIONS)]
    qb_r, kb_r, vb_r = flat(qb), flat(kb), flat(vb)
    bmaps = [[jnp.asarray(m) for m in _bucket_maps(dil)] for dil in DILATIONS]
    bias_tabs = [rel_bias[:, N_HEADS_PER_DIL * g:N_HEADS_PER_DIL * (g + 1)] for g in range(3)]
    band_out = [_band_fwd(dil, qb_r[g], kb_r[g], vb_r[g], bmaps[g][0], bias_tabs[g], cb)
                for g, dil in enumerate(DILATIONS)]
    og, lg = split([o for o, _ in band_out]), split([l for _, l in band_out])
    x2, ob, lse_b, ya, yb, u = _mix_out(xs, oa, og, lg, ga, gb, w_oa, w_ob_t, w_o, tb)
    tc = _pick_tile(w_ff1_t.shape[0], (512,))
    x3, r_act, h2 = _mlp_fwd(x2, w_ff1_t, w_ff2_f, norm_mlp_g, tb, tc)

    dx3, h3, dpre, dpe, pb, loss_part, dg_fin, dg_ple = _ple_loss(
        x3, ps, ts, w_pg, w_p_t, norm_ple_g, fin_g, tb)
    dx2, df, dg_mlp = _mlp_bwd(dx3, x2, r_act, w_ff1_t, w_ff2_f, norm_mlp_g, tb, tc)

    tkk = _pick_tile(s, (1024, 512))
    dff = w_ff1_t.shape[0]
    t1k = lambda n: _pick_tile(n, (1024, 512, 256))
    slots = lambda parts: [lax.empty((7, a.shape[0] // N_DEV, a.shape[1]), BF16) for a in parts]
    part1 = [_weight_grad("grad_w_ff1", df, h2, t1k(dff), t1k(d), tkk),
             _weight_grad("grad_w_ff2", r_act, dx3, t1k(dff), t1k(d), tkk, a_fn=_square_bf16, b_fn=_to_bf16),
             _weight_grad("grad_w_ple_gate", h3, dpre, t1k(d), t1k(d), tkk),
             _weight_grad("grad_w_ple", dpe, pb, t1k(d), ps.shape[1], tkk)]
    rs1 = _copies_start("grads1_start", part1, slots(part1), dx2, False)
    doa, dob, dd, dga, dgb, dya, dyb, dbg = _mix_out_bwd(dx2, ya, yb, ga, gb, ob, w_oa, w_ob_t, w_o, tb, rs1[4])
    part2 = [_weight_grad("grad_w_out_a", oa, dya, t1k(QA_W), t1k(d), tkk),
             _weight_grad("grad_w_out_b", dyb, ob, t1k(d), GB_W, tkk),
             _weight_grad("grad_w_out", u, dx2, t1k(d), t1k(d), tkk, b_fn=_to_bf16)]
    rs2 = _copies_start("grads2_start", part2, slots(part2), doa, False)
    dqrot, dkrot, dva = _attn_a_bwd(qrot, krot, va, oa, doa, lse_a, tq, tk, rs2[4])
    dob_r, lse_r, dd_r = flat(dob), flat(lse_b), flat(dd)
    bwd_q = [_band_bwd_q(dil, qb_r[g], kb_r[g], vb_r[g], dob_r[g], lse_r[g], dd_r[g], bmaps[g][0], bias_tabs[g], cb)
             for g, dil in enumerate(DILATIONS)]
    bwd_kv = [_band_bwd_kv(dil, qb_r[g], kb_r[g], vb_r[g], dob_r[g], lse_r[g], dd_r[g], bmaps[g][1], bias_tabs[g], cb)
              for g, dil in enumerate(DILATIONS)]
    dqb, dkb, dvb = split([r[0] for r in bwd_q]), split([r[0] for r in bwd_kv]), split([r[1] for r in bwd_kv])
    grad_x, dz, dg_mix, dg_q, dg_k = _in_proj_bwd(
        dx2, xs, dqrot, dkrot, dva, qraw, kraw, tabs, dqb, dkb, dvb, dga, dgb, w_in_t, norm_mix_g,
        q_norm_g, k_norm_g, _pick_tile(s, (256,)))
    d_rel = jnp.concatenate([r[1][:, :N_HEADS_PER_DIL] for r in bwd_q], axis=1)

    din = w_in_t.shape[0]
    ti_in = _pick_tile(din, (din // 2,)) if (din // 2) % LANES == 0 else din
    hd_ = d // 2
    part3 = [_weight_grad("grad_w_in_lo", dz, h1, ti_in, t1k(hd_), tkk, n=hd_)]
    rs3 = _copies_start("grads3_start", part3, slots(part3), grad_x, False)
    part4 = [_weight_grad("grad_w_in_hi", dz, h1, ti_in, t1k(hd_), tkk, col0=hd_, n=hd_, after=rs3[4])]
    rs4 = _copies_start("grads4_start", part4, slots(part4), rs3[4], False)

    def own_rows(a):
        n = a.shape[0] // N_DEV
        return lax.dynamic_slice(a, (my_idx * n, 0), (n, a.shape[1]))

    sums = {}
    src1, got1 = _copies_wait("grads1_wait", rs1[0], rs1[1], rs1[2], rs1[3], rs4[4], False)
    src2, got2 = _copies_wait("grads2_wait", rs2[0], rs2[1], rs2[2], rs2[3], rs4[4], False)
    for n, a, r in zip(["w_ff1", "w_ff2", "w_ple_gate", "w_ple", "w_out_a", "w_out_b", "w_out"],
                       src1 + src2, got1 + got2):
        sums[n] = _sum_slots("sum_" + n, r, own_rows(a))
    given_w = dict(w_in=w_in, w_out_a=w_out_a, w_out_b=w_out_b, w_out=w_out, w_ff1=w_ff1, w_ff2=w_ff2,
                   w_ple_gate=w_ple_gate, w_ple=w_ple)
    given_m = dict(w_in=m_w_in, w_out_a=m_w_out_a, w_out_b=m_w_out_b, w_out=m_w_out, w_ff1=m_w_ff1, w_ff2=m_w_ff2,
                   w_ple_gate=m_w_ple_gate, w_ple=m_w_ple)
    given_v = dict(w_in=v_w_in, w_out_a=v_w_out_a, w_out_b=v_w_out_b, w_out=v_w_out, w_ff1=v_w_ff1, w_ff2=v_w_ff2,
                   w_ple_gate=v_w_ple_gate, w_ple=v_w_ple)
    big = {}

    def update(n):
        g = sums[n].T if n in col_sharded else sums[n]
        delta, new_m, new_v = _adamw("adamw_" + n, given_w[n][0], g, given_m[n][0], given_v[n][0])
        big[n] = tuple(a[None] for a in (g, delta, new_m, new_v))

    for n in order[1:]:
        update(n)
    src3, got3 = _copies_wait("grads3_wait", rs3[0], rs3[1], rs3[2], rs3[3], big["w_ple"][1], False)
    src4, got4 = _copies_wait("grads4_wait", rs4[0], rs4[1], rs4[2], rs4[3], big["w_ple"][1], False)
    sums["w_in"] = jnp.concatenate([_sum_slots("sum_w_in_lo", got3[0], own_rows(src3[0])),
                                    _sum_slots("sum_w_in_hi", got4[0], own_rows(src4[0]))], axis=1)
    update("w_in")

    small_names = ["norm_mix_g", "b_gate", "q_norm_g", "k_norm_g", "rel_bias", "norm_mlp_g", "norm_ple_g",
                   "final_norm_g"]
    small_w = [norm_mix_g, b_gate, q_norm_g, k_norm_g, rel_bias, norm_mlp_g, norm_ple_g, final_norm_g]
    small_m = [m_norm_mix_g, m_b_gate, m_q_norm_g, m_k_norm_g, m_rel_bias, m_norm_mlp_g, m_norm_ple_g,
               m_final_norm_g]
    small_v = [v_norm_mix_g, v_b_gate, v_q_norm_g, v_k_norm_g, v_rel_bias, v_norm_mlp_g, v_norm_ple_g,
               v_final_norm_g]
    small_g = [dg_mix, dbg, dg_q, dg_k, d_rel, dg_mlp, dg_ple, dg_fin]
    sizes = [int(np.prod(w.shape)) for w in small_w]
    n_rows = -(-(sum(-(-sz // LANES) for sz in sizes) + 1) // 8) * 8
    pad = lambda v: jnp.pad(v.reshape(-1).astype(F32), (0, -v.size % LANES))
    pack = lambda vs, last: _pack_rows([pad(v) for v in vs] + [last], n_rows)
    zero_row = jnp.zeros((LANES,), F32)
    parts = _small_all_gather(pack(small_g, loss_part.reshape(-1) * (jnp.arange(LANES) == 0)))
    g_all, d_all, m_all, v_all = _small_update(parts, pack(small_w, zero_row), pack(small_m, zero_row),
                                               pack(small_v, zero_row))
    small = {}
    row = 0
    for n, w, sz in zip(small_names, small_w, sizes):
        nr = -(-sz // LANES)
        small[n] = tuple(a[row:row + nr].reshape(-1)[:sz].reshape(w.shape) for a in (g_all, d_all, m_all, v_all))
        row += nr
    loss = g_all[row, 0]

    names = ["norm_mix_g", "w_in", "b_gate", "q_norm_g", "k_norm_g", "rel_bias", "w_out_a", "w_out_b", "w_out",
             "norm_mlp_g", "w_ff1", "w_ff2", "norm_ple_g", "w_ple_gate", "w_ple", "final_norm_g"]
    res = {n: (big[n] if n in big else small[n]) for n in names}
    return (loss, grad_x[None], *[res[n][0] for n in names], *[res[n][1] for n in names],
            *[res[n][2] for n in names], *[res[n][3] for n in names])
```

```python
import functools
import math

import numpy as np
import jax
import jax.numpy as jnp
from jax import lax
from jax.experimental import pallas as pl
from jax.experimental.pallas import tpu as pltpu

F32 = jnp.float32
BF16 = jnp.bfloat16
MESH = pl.DeviceIdType.MESH

NORM_EPS = 1e-6
NEG_INF = -1e30
LOG2_E = math.log2(math.e)
LN_2 = math.log(2.0)
GRID_W = 64
ROPE_THETA = 10000.0
HEAD_DIM_A = 128
N_Q_HEADS_A = 8
N_KV_HEADS_A = 2
Q_PER_KV = N_Q_HEADS_A // N_KV_HEADS_A
HEAD_DIM_B = 64
N_HEADS_PER_DIL = 4
DILATIONS = (1, 4, 16)
BAND = 64
N_REL_BUCKETS = 32
REL_MAX_DIST = 1024
QA_W = N_Q_HEADS_A * HEAD_DIM_A
KA_W = N_KV_HEADS_A * HEAD_DIM_A
GB_W = N_HEADS_PER_DIL * HEAD_DIM_B
QB_W = GB_W * len(DILATIONS)
OFF_QA, OFF_KA, OFF_VA = 0, QA_W, QA_W + KA_W
OFF_QB = QA_W + 2 * KA_W
OFF_KB = OFF_QB + QB_W
OFF_VB = OFF_KB + QB_W
OFF_GA = OFF_VB + QB_W
N_DEV = 8
LANES = 128
VMEM_LIMIT = 56 * 2 ** 20

ADAM_LR, ADAM_B1, ADAM_B2, ADAM_EPS, ADAM_WD, ADAM_STEP = 0.001, 0.9, 0.999, 1e-08, 0.01, 10


def _cparams(sem):
    return pltpu.CompilerParams(dimension_semantics=sem, vmem_limit_bytes=VMEM_LIMIT)


def _resident(shape):
    nd = len(shape)
    return pl.BlockSpec(shape, lambda *_: (0,) * nd, pipeline_mode=pl.Buffered(1))


def _acc_spec(shape):
    nd = len(shape)
    return pl.BlockSpec(shape, lambda *_: (0,) * nd)


def _rows(tb, c):
    return pl.BlockSpec((tb, c), lambda i: (i, 0))


def _dil_shapes(s, dtype):
    return [jax.ShapeDtypeStruct((dil, s // dil, GB_W), dtype) for dil in DILATIONS]


def _dil_specs(tb):
    return [pl.BlockSpec((dil, tb // dil, GB_W), lambda i: (0, i, 0)) for dil in DILATIONS]


def _to_residues(val, out_ref, scr_ref, dil, dtype):
    if dil == 1:
        out_ref[0] = val.astype(dtype)
        return
    n = val.shape[0] // dil
    scr_ref[0] = val[:, :LANES]
    scr_ref[1] = val[:, LANES:]
    for r in range(dil):
        out_ref[r] = jnp.concatenate([scr_ref[0, pl.ds(r, n, stride=dil), :],
                                      scr_ref[1, pl.ds(r, n, stride=dil), :]], axis=1).astype(dtype)


def _from_residues(in_ref, scr_ref, dil):
    if dil == 1:
        return in_ref[0]
    n = in_ref.shape[1]
    for r in range(dil):
        v = in_ref[r]
        scr_ref[0, pl.ds(r, n, stride=dil), :] = v[:, :LANES]
        scr_ref[1, pl.ds(r, n, stride=dil), :] = v[:, LANES:]
    return jnp.concatenate([scr_ref[0], scr_ref[1]], axis=1)


def _dot_nt(a, b):
    return lax.dot_general(a, b, (((1,), (1,)), ((), ())), preferred_element_type=F32)


def _dot_nn(a, b):
    return lax.dot_general(a, b, (((1,), (0,)), ((), ())), preferred_element_type=F32)


def _dot_tn(a, b):
    return lax.dot_general(a, b, (((0,), (0,)), ((), ())), preferred_element_type=F32)


def _rstd(x):
    return lax.rsqrt(jnp.mean(x * x, axis=-1, keepdims=True) + NORM_EPS)


def _rms_bwd(dy, n, r, g):
    dn = dy * g
    return r * (dn - n * jnp.mean(dn * n, axis=-1, keepdims=True))


def _colsum(v):
    return jnp.sum(v, axis=0, keepdims=True)


def _sigmoid(v):
    return 1.0 / (1.0 + jnp.exp(-v))


def _rope_fwd(n, c, s1, s2):
    return n * c + pltpu.roll(n, 32, 1) * s1 + pltpu.roll(n, 96, 1) * s2


def _rope_bwd(d, c, s1, s2):
    return d * c + pltpu.roll(d * s1, 96, 1) + pltpu.roll(d * s2, 32, 1)


def _rope_tables(s):
    half = HEAD_DIM_A // 2
    inv = jnp.power(ROPE_THETA, -jnp.arange(0, half, 2, dtype=F32) / half)
    n_rows = s // GRID_W
    ang_r = jnp.arange(n_rows, dtype=F32)[:, None] * inv[None, :]
    ang_c = jnp.arange(GRID_W, dtype=F32)[:, None] * inv[None, :]
    cr, sr = jnp.repeat(jnp.cos(ang_r), GRID_W, axis=0), jnp.repeat(jnp.sin(ang_r), GRID_W, axis=0)
    cc, sc = jnp.tile(jnp.cos(ang_c), (n_rows, 1)), jnp.tile(jnp.sin(ang_c), (n_rows, 1))
    z = jnp.zeros_like(sr)
    cos = jnp.concatenate([cr, cr, cc, cc], axis=1)
    s1 = jnp.concatenate([z, sr, z, sc], axis=1)
    s2 = jnp.concatenate([-sr, z, -sc, z], axis=1)
    return cos, s1, s2


def _my_place():
    return lax.axis_index("x"), lax.axis_index("y"), lax.axis_index("c")


def _all_gather(shards):
    nw = len(shards)

    def body(*refs):
        ins, outs = refs[:nw], refs[nw:2 * nw]
        send_sems, recv_sems, local_sems = refs[2 * nw:]
        x, y, c = _my_place()
        me, sibling = (x, y, c), (x, y, 1 - c)
        chips = [(1 - x, y), (x, 1 - y), (1 - x, 1 - y)]

        def rows(w, px, py, pc):
            n = ins[w].shape[0]
            return outs[w].at[pl.ds(pl.multiple_of((4 * px + 2 * py + pc) * n, 16), n), :]

        def copy(w, k, block, to, src=None):
            return pltpu.make_async_remote_copy(
                src_ref=rows(w, *block) if src is None else src, dst_ref=rows(w, *block),
                send_sem=send_sems.at[w, k], recv_sem=recv_sems.at[w, k], device_id=to, device_id_type=MESH)

        mine = [pltpu.make_async_copy(ins[w], rows(w, *me), local_sems.at[w]) for w in range(nw)]
        for cp in mine:
            cp.start()
        first = []
        for w in range(nw):
            first.append(copy(w, 0, me, sibling, src=ins[w]))
            first += [copy(w, 1 + j, me, (*chip, c), src=ins[w]) for j, chip in enumerate(chips)]
        for cp in first:
            cp.start()
        passed = []
        for j, chip in enumerate(chips):
            for w in range(nw):
                copy(w, 1 + j, (*chip, c), me).wait_recv()
                fwd = copy(w, 4 + j, (*chip, c), sibling)
                fwd.start()
                passed.append(fwd)
        for w in range(nw):
            copy(w, 0, sibling, me).wait_recv()
        for j, chip in enumerate(chips):
            for w in range(nw):
                copy(w, 4 + j, (*chip, 1 - c), me).wait_recv()
        for cp in first + passed:
            cp.wait_send()
        for cp in mine:
            cp.wait()

    any_spec = pl.BlockSpec(memory_space=pl.ANY)
    return pl.pallas_call(
        body, name="weights_all_gather",
        out_shape=[jax.ShapeDtypeStruct((N_DEV * s.shape[0], s.shape[1]), s.dtype) for s in shards],
        in_specs=[any_spec] * nw, out_specs=[any_spec] * nw,
        scratch_shapes=[pltpu.SemaphoreType.DMA((nw, 7)), pltpu.SemaphoreType.DMA((nw, 7)),
                        pltpu.SemaphoreType.DMA((nw,))],
    )(*shards)


_FLIPS = [(fx, fy, fc) for fx in (0, 1) for fy in (0, 1) for fc in (0, 1)][1:]


def _small_all_gather(v):
    def body(v_ref, out_ref, send_sems, recv_sems):
        x, y, c = _my_place()
        my_idx = 4 * x + 2 * y + c
        out_ref[my_idx] = v_ref[...]
        sends = []
        for k, (fx, fy, fc) in enumerate(_FLIPS):
            to = (1 - x if fx else x, 1 - y if fy else y, 1 - c if fc else c)
            sends.append(pltpu.make_async_remote_copy(
                src_ref=v_ref, dst_ref=out_ref.at[my_idx], send_sem=send_sems.at[k], recv_sem=recv_sems.at[k],
                device_id=to, device_id_type=MESH))
        for cp in sends:
            cp.start()
        for k, (fx, fy, fc) in enumerate(_FLIPS):
            frm_idx = 4 * (1 - x if fx else x) + 2 * (1 - y if fy else y) + (1 - c if fc else c)
            pltpu.make_async_remote_copy(
                src_ref=v_ref, dst_ref=out_ref.at[frm_idx], send_sem=send_sems.at[k], recv_sem=recv_sems.at[k],
                device_id=(x, y, c), device_id_type=MESH).wait_recv()
        for cp in sends:
            cp.wait_send()

    vm = pl.BlockSpec(memory_space=pltpu.VMEM)
    return pl.pallas_call(
        body, name="small_all_gather", out_shape=jax.ShapeDtypeStruct((N_DEV,) + v.shape, v.dtype),
        in_specs=[vm], out_specs=vm,
        scratch_shapes=[pltpu.SemaphoreType.DMA((7,)), pltpu.SemaphoreType.DMA((7,))],
    )(v)


_HBM = pl.BlockSpec(memory_space=pltpu.HBM)
_SEM = pl.BlockSpec(memory_space=pltpu.SEMAPHORE)
_ANY = pl.BlockSpec(memory_space=pl.ANY)
_SPLIT_COPY = dict(has_side_effects=pltpu.SideEffectType.DATAFLOW_SIDE_EFFECTING)


def _peer(x, y, c, k):
    fx, fy, fc = _FLIPS[k]
    return (1 - x if fx else x, 1 - y if fy else y, 1 - c if fc else c)


def _in_hbm(a):
    return pltpu.with_memory_space_constraint(a, pltpu.HBM)


def _split_copies(srcs, lands, send_sems, recv_sems, gather, arriving):
    x, y, c = _my_place()
    my_idx = 4 * x + 2 * y + c
    out = []
    for k in range(7):
        to = _peer(x, y, c, k)
        to_idx = 4 * to[0] + 2 * to[1] + to[2]
        for w in range(len(srcs)):
            if gather:
                n = srcs[w].shape[0]
                src = srcs[w]
                dst = lands[w].at[pl.ds(pl.multiple_of((to_idx if arriving else my_idx) * n, 16), n), :]
            else:
                n = lands[w].shape[1]
                src = srcs[w].at[pl.ds(pl.multiple_of(to_idx * n, 16), n), :]
                dst = lands[w].at[k]
            out.append(pltpu.make_async_remote_copy(
                src_ref=src, dst_ref=dst, send_sem=send_sems.at[7 * w + k], recv_sem=recv_sems.at[7 * w + k],
                device_id=to, device_id_type=MESH))
    return out


def _copies_start(name, srcs, lands, after, gather):
    nw = len(srcs)

    def body(*refs):
        send_sems, recv_sems = refs[2 * nw + 1], refs[2 * nw + 2]
        for cp in _split_copies(refs[:nw], refs[nw:2 * nw], send_sems, recv_sems, gather, False):
            cp.start()
        refs[-1][...] = jnp.zeros_like(refs[-1])

    sems = pltpu.SemaphoreType.DMA((7 * nw,))
    thru = [pltpu.HBM(a.shape, a.dtype) for a in list(srcs) + list(lands)]
    res = pl.pallas_call(
        body, name=name, out_shape=(sems, sems, *thru, jax.ShapeDtypeStruct((8, LANES), F32)),
        in_specs=[_HBM] * (2 * nw) + [_ANY], out_specs=(_SEM, _SEM, *[_HBM] * (2 * nw), pl.BlockSpec(memory_space=pltpu.VMEM)),
        input_output_aliases={i: 2 + i for i in range(2 * nw)},
        compiler_params=pltpu.CompilerParams(**_SPLIT_COPY),
    )(*[_in_hbm(a) for a in srcs], *[_in_hbm(a) for a in lands], after)
    return res[0], res[1], list(res[2:2 + nw]), list(res[2 + nw:2 + 2 * nw]), res[-1]


def _copies_wait(name, send_sems, recv_sems, srcs, lands, after, gather):
    nw = len(srcs)

    def body(*refs):
        for cp in _split_copies(refs[:nw], refs[nw:2 * nw], refs[2 * nw], refs[2 * nw + 1], gather, False):
            cp.wait_send()
        for cp in _split_copies(refs[:nw], refs[nw:2 * nw], refs[2 * nw], refs[2 * nw + 1], gather, True):
            cp.wait_recv()

    thru = [pltpu.HBM(a.shape, a.dtype) for a in list(srcs) + list(lands)]
    res = pl.pallas_call(
        body, name=name, out_shape=tuple(thru),
        in_specs=[_HBM] * (2 * nw) + [_SEM, _SEM, _ANY], out_specs=tuple([_HBM] * (2 * nw)),
        input_output_aliases={i: i for i in range(2 * nw)},
        compiler_params=pltpu.CompilerParams(**_SPLIT_COPY),
    )(*srcs, *lands, send_sems, recv_sems, after)
    return list(res[:nw]), list(res[nw:])


def _in_proj(x, tabs, w_in_t, g_mix, b_gate, q_g, k_g, tb, after):
    s, d = x.shape
    n_gate_chunks = d // 256
    q_scale = HEAD_DIM_A ** -0.5 * LOG2_E
    b_scale = HEAD_DIM_B ** -0.5 * LOG2_E

    def body(x_ref, c_ref, s1_ref, s2_ref, w_ref, gmix_ref, bg_ref, qg_ref, kg_ref, after_ref,
             h1_ref, qraw_ref, kraw_ref, qrot_ref, krot_ref, va_ref, *rest):
        qb_refs, kb_refs, vb_refs = rest[0:3], rest[3:6], rest[6:9]
        ga_ref, gb_ref, scr_ref = rest[9:]
        xv = x_ref[...]
        hb = (xv * _rstd(xv) * gmix_ref[...]).astype(BF16)
        h1_ref[...] = hb
        cos, s1, s2 = c_ref[...], s1_ref[...], s2_ref[...]

        def proj(lo, width):
            return _dot_nt(hb, w_ref[lo:lo + width, :])

        def norm_rope(z, g):
            return _rope_fwd(z * _rstd(z) * g, cos, s1, s2)

        for j in range(QA_W // 256):
            z = proj(OFF_QA + 256 * j, 256)
            qraw_ref[:, 256 * j:256 * j + 256] = z
            for hh in range(2):
                lo = 256 * j + 128 * hh
                qrot_ref[:, lo:lo + 128] = (norm_rope(z[:, 128 * hh:128 * hh + 128], qg_ref[...]) * q_scale).astype(BF16)
        z = proj(OFF_KA, 256)
        kraw_ref[...] = z
        for hh in range(2):
            krot_ref[:, 128 * hh:128 * hh + 128] = norm_rope(z[:, 128 * hh:128 * hh + 128], kg_ref[...]).astype(BF16)
        va_ref[...] = proj(OFF_VA, 256).astype(BF16)
        for g, dil in enumerate(DILATIONS):
            _to_residues(proj(OFF_QB + GB_W * g, GB_W) * b_scale, qb_refs[g], scr_ref, dil, BF16)
            _to_residues(proj(OFF_KB + GB_W * g, GB_W), kb_refs[g], scr_ref, dil, BF16)
            _to_residues(proj(OFF_VB + GB_W * g, GB_W), vb_refs[g], scr_ref, dil, BF16)
        for j in range(n_gate_chunks):
            sl = slice(256 * j, 256 * j + 256)
            ga_ref[:, sl] = _sigmoid(proj(OFF_GA + 256 * j, 256) + bg_ref[:, sl])
            gb_ref[:, sl] = _sigmoid(proj(OFF_GA + d + 256 * j, 256) + bg_ref[:, d + 256 * j:d + 256 * j + 256])

    sd = jax.ShapeDtypeStruct
    outs = [sd((s, d), BF16), sd((s, QA_W), F32), sd((s, KA_W), F32), sd((s, QA_W), BF16), sd((s, KA_W), BF16),
            sd((s, KA_W), BF16)] + _dil_shapes(s, BF16) * 3 + [sd((s, d), F32), sd((s, d), F32)]
    out_specs = [_rows(tb, d), _rows(tb, QA_W), _rows(tb, KA_W), _rows(tb, QA_W), _rows(tb, KA_W), _rows(tb, KA_W)
                 ] + _dil_specs(tb) * 3 + [_rows(tb, d), _rows(tb, d)]
    in_specs = [_rows(tb, d), _rows(tb, LANES), _rows(tb, LANES), _rows(tb, LANES), _resident(w_in_t.shape),
                _resident(g_mix.shape), _resident(b_gate.shape), _resident(q_g.shape), _resident(k_g.shape), _ANY]
    res = list(pl.pallas_call(body, name="in_proj", grid=(s // tb,), in_specs=in_specs, out_specs=out_specs,
                              out_shape=outs, scratch_shapes=[pltpu.VMEM((2, tb, LANES), F32)],
                              compiler_params=_cparams(("arbitrary",)))(
        x, *tabs, w_in_t, g_mix, b_gate, q_g, k_g, after))
    return res[:6] + [res[6:9], res[9:12], res[12:15]] + res[15:]


def _attn_a_fwd(qrot, krot, va, tq, tk):
    s = qrot.shape[0]
    n_kv = s // tk
    gw = Q_PER_KV * HEAD_DIM_A

    def body(q_ref, k_ref, v_ref, o_ref, lse_ref):
        q4 = jnp.concatenate([q_ref[:, 128 * h:128 * h + 128] for h in range(Q_PER_KV)], axis=0)

        def step(j, carry):
            m, l, acc = carry
            sl = pl.ds(pl.multiple_of(j * tk, tk), tk)
            kj, vj = k_ref[sl, :], v_ref[sl, :]
            sc = _dot_nt(kj, q4)
            m_new = jnp.maximum(m, jnp.max(sc, axis=0, keepdims=True))
            p = jnp.exp2(sc - m_new)
            alpha = jnp.exp2(m - m_new)
            l = alpha * l + jnp.sum(p, axis=0, keepdims=True)
            acc = alpha * acc + _dot_tn(vj, p.astype(BF16))
            return m_new, l, acc

        rows = Q_PER_KV * tq
        m, l, acc = lax.fori_loop(0, n_kv, step, (jnp.full((1, rows), NEG_INF, F32), jnp.zeros((1, rows), F32),
                                                  jnp.zeros((HEAD_DIM_A, rows), F32)))
        o = (acc / l).T
        lse = m + jnp.log2(l)
        for h in range(Q_PER_KV):
            o_ref[:, 128 * h:128 * h + 128] = o[h * tq:(h + 1) * tq].astype(BF16)
            lse_ref[0, h:h + 1, :] = lse[:, h * tq:(h + 1) * tq]

    return pl.pallas_call(
        body, name="attn_a_fwd", grid=(N_KV_HEADS_A, s // tq),
        in_specs=[pl.BlockSpec((tq, gw), lambda g, i: (i, g)),
                  pl.BlockSpec((s, HEAD_DIM_A), lambda g, i: (0, g)),
                  pl.BlockSpec((s, HEAD_DIM_A), lambda g, i: (0, g))],
        out_specs=[pl.BlockSpec((tq, gw), lambda g, i: (i, g)),
                   pl.BlockSpec((1, Q_PER_KV, tq), lambda g, i: (g, 0, i))],
        out_shape=[jax.ShapeDtypeStruct((s, QA_W), BF16), jax.ShapeDtypeStruct((N_KV_HEADS_A, Q_PER_KV, s), F32)],
        compiler_params=_cparams(("arbitrary", "arbitrary")))(qrot, krot, va)


def _attn_a_bwd(qrot, krot, va, oa, doa, lse, tq, tk, after):
    s = qrot.shape[0]
    n_kv = s // tk
    gw = Q_PER_KV * HEAD_DIM_A

    def body(q_ref, do_ref, o_ref, lse_ref, k_ref, v_ref, after_ref, dq_ref, dk_ref, dv_ref):
        @pl.when(pl.program_id(1) == 0)
        def _():
            dk_ref[...] = jnp.zeros_like(dk_ref)
            dv_ref[...] = jnp.zeros_like(dv_ref)

        def stack(ref):
            return jnp.concatenate([ref[:, 128 * h:128 * h + 128] for h in range(Q_PER_KV)], axis=0)

        q4, do4, o4 = stack(q_ref), stack(do_ref), stack(o_ref)
        q4t, do4t = q4.T, do4.T
        delta = jnp.sum((do4.astype(F32) * o4.astype(F32)).T, axis=0, keepdims=True)
        lse4 = jnp.concatenate([lse_ref[0, h:h + 1, :] for h in range(Q_PER_KV)], axis=1)

        def step(j, dq):
            sl = pl.ds(pl.multiple_of(j * tk, tk), tk)
            kj, vj = k_ref[sl, :], v_ref[sl, :]
            p = jnp.exp2(_dot_nt(kj, q4) - lse4)
            ds = (p * (_dot_nt(vj, do4) - delta)).astype(BF16)
            dk_ref[:, sl] += _dot_nt(q4t, ds)
            dv_ref[:, sl] += _dot_nt(do4t, p.astype(BF16))
            return dq + _dot_tn(kj, ds)

        dq = lax.fori_loop(0, n_kv, step, jnp.zeros((HEAD_DIM_A, Q_PER_KV * tq), F32)).T
        for h in range(Q_PER_KV):
            dq_ref[:, 128 * h:128 * h + 128] = dq[h * tq:(h + 1) * tq]

    qspec = pl.BlockSpec((tq, gw), lambda g, i: (i, g))
    kspec = pl.BlockSpec((s, HEAD_DIM_A), lambda g, i: (0, g))
    ktspec = pl.BlockSpec((HEAD_DIM_A, s), lambda g, i: (g, 0))
    return pl.pallas_call(
        body, name="attn_a_bwd", grid=(N_KV_HEADS_A, s // tq),
        in_specs=[qspec, qspec, qspec, pl.BlockSpec((1, Q_PER_KV, tq), lambda g, i: (g, 0, i)), kspec, kspec, _ANY],
        out_specs=[qspec, ktspec, ktspec],
        out_shape=[jax.ShapeDtypeStruct((s, QA_W), F32), jax.ShapeDtypeStruct((KA_W, s), F32),
                   jax.ShapeDtypeStruct((KA_W, s), F32)],
        compiler_params=_cparams(("arbitrary", "arbitrary")))(qrot, doa, oa, lse, krot, va, after)


BAND_QB = 256
BAND_WIN = BAND_QB + 2 * BAND


def _band_specs(s, cb):
    per = cb // BAND
    last = s // BAND - 1
    cur = pl.BlockSpec((cb, GB_W), lambda i: (i, 0))
    prev = pl.BlockSpec((BAND, GB_W), lambda i: (jnp.maximum(i * per - 1, 0), 0))
    nxt = pl.BlockSpec((BAND, GB_W), lambda i: (jnp.minimum(i * per + per, last), 0))
    return cur, prev, nxt


def _window(prev_ref, cur_ref, next_ref):
    return jnp.concatenate([prev_ref[...], cur_ref[...], next_ref[...]], axis=0)


def _band_mask(base, seg_shift, window_rows):
    shape = (BAND_WIN, BAND_QB) if window_rows else (BAND_QB, BAND_WIN)
    a = lax.broadcasted_iota(jnp.int32, shape, 0)
    b = lax.broadcasted_iota(jnp.int32, shape, 1)
    rq, rk = (base - BAND + a, base + b) if window_rows else (base + a, base - BAND + b)
    same_segment = lax.shift_right_arithmetic(rq, jnp.int32(seg_shift)) == lax.shift_right_arithmetic(rk, jnp.int32(seg_shift))
    return (jnp.abs(rk - rq) <= BAND) & same_segment


def _build_bias(bmap_ref, tab_ref, bias_ref):
    bm = bmap_ref[...]
    acc = [jnp.full(bm.shape, NEG_INF, F32) for _ in range(N_HEADS_PER_DIL)]
    for b in range(N_REL_BUCKETS):
        hit = bm == b
        for h in range(N_HEADS_PER_DIL):
            acc[h] = jnp.where(hit, tab_ref[b, h] * LOG2_E, acc[h])
    rows = bm.shape[0]
    for h in range(N_HEADS_PER_DIL):
        bias_ref[h * rows:(h + 1) * rows, :] = acc[h]


def _segment_mask(base, seg_len, seg_shift, window_rows):
    if seg_len % BAND_QB:
        return _band_mask(base, seg_shift, window_rows)
    pos = lax.rem(base, seg_len)
    shape, dim = ((BAND_WIN, 1), 0) if window_rows else ((1, BAND_WIN), 1)
    w = lax.broadcasted_iota(jnp.int32, shape, dim)
    return ((w >= BAND) | (pos != 0)) & ((w < BAND + BAND_QB) | (pos != seg_len - BAND_QB))


def _head_lane_masks():
    lane = lax.broadcasted_iota(jnp.int32, (1, LANES), 1)
    return [lane < HEAD_DIM_B, lane >= HEAD_DIM_B]


def _rows4(mask):
    return mask if mask.shape[0] == 1 else jnp.concatenate([mask] * N_HEADS_PER_DIL, axis=0)


def _head_scores(a, b):
    hm = _head_lane_masks()
    out = []
    for hp in range(2):
        ls = slice(LANES * hp, LANES * hp + LANES)
        ah = a[:, ls]
        both = jnp.concatenate([jnp.where(hm[0], ah, jnp.zeros_like(ah)), jnp.where(hm[1], ah, jnp.zeros_like(ah))],
                               axis=0)
        out.append(_dot_nt(both, b[:, ls]))
    return jnp.concatenate(out, axis=0)


def _head_combine(p, v, scale=None, transposed=False):
    hm = _head_lane_masks()
    rows = p.shape[0] // N_HEADS_PER_DIL
    halves = []
    for hp in range(2):
        vh = v[:, LANES * hp:LANES * hp + LANES]
        acc = None
        for hh in range(2):
            h = 2 * hp + hh
            ph = p[h * rows:(h + 1) * rows]
            vm = jnp.where(hm[hh], vh, jnp.zeros_like(vh))
            t = _dot_tn(ph, vm) if transposed else _dot_nn(ph, vm)
            if scale is not None:
                t = t * scale[h * rows:(h + 1) * rows]
            acc = t if acc is None else acc + t
        halves.append(acc)
    return jnp.concatenate(halves, axis=1)


def _head_spread(col):
    rows = col.shape[0] // N_HEADS_PER_DIL
    lane = lax.broadcasted_iota(jnp.int32, (1, GB_W), 1)
    out = jnp.zeros((rows, GB_W), F32)
    for h in range(N_HEADS_PER_DIL):
        out = jnp.where((lane >= HEAD_DIM_B * h) & (lane < HEAD_DIM_B * (h + 1)), col[h * rows:(h + 1) * rows], out)
    return out


def _head_cols(v):
    return jnp.concatenate([v[:, HEAD_DIM_B * h:HEAD_DIM_B * h + 1] for h in range(N_HEADS_PER_DIL)], axis=0)


def _seg_shift(s, dil):
    seg = s // dil
    assert seg & (seg - 1) == 0, "segment length must be a power of two"
    return seg.bit_length() - 1


def _band_fwd(dil, qb, kb, vb, bmap, tab, cb):
    s = qb.shape[0]
    shift = _seg_shift(s, dil)

    def body(q_ref, kp_ref, kc_ref, kn_ref, vp_ref, vc_ref, vn_ref, bmap_ref, tab_ref, o_ref, lse_ref, bias_ref):
        @pl.when(pl.program_id(0) == 0)
        def _():
            _build_bias(bmap_ref, tab_ref, bias_ref)

        kw, vw = _window(kp_ref, kc_ref, kn_ref), _window(vp_ref, vc_ref, vn_ref)
        for jj in range(cb // BAND_QB):
            r0 = BAND_QB * jj
            mask = _rows4(_segment_mask(pl.program_id(0) * cb + r0, s // dil, shift, False))
            sc = _head_scores(q_ref[r0:r0 + BAND_QB, :], kw[r0:r0 + BAND_WIN, :]) + bias_ref[...]
            sc = jnp.where(mask, sc, NEG_INF)
            m = jnp.max(sc, axis=-1, keepdims=True)
            e = jnp.exp2(sc - m)
            l = jnp.sum(e, axis=-1, keepdims=True)
            o = _head_combine(e.astype(BF16), vw[r0:r0 + BAND_WIN, :], 1.0 / l)
            o_ref[r0:r0 + BAND_QB, :] = o
            lse_ref[r0:r0 + BAND_QB, :] = _head_spread(m + jnp.log2(l))

    cur, prev, nxt = _band_specs(s, cb)
    return pl.pallas_call(
        body, name=f"band_fwd_d{dil}", grid=(s // cb,),
        in_specs=[cur, prev, cur, nxt, prev, cur, nxt, _resident(bmap.shape), pl.BlockSpec(memory_space=pltpu.SMEM)],
        out_specs=[cur, cur],
        out_shape=[jax.ShapeDtypeStruct(qb.shape, F32), jax.ShapeDtypeStruct(qb.shape, F32)],
        scratch_shapes=[pltpu.VMEM((N_HEADS_PER_DIL * BAND_QB, BAND_WIN), F32)],
        compiler_params=_cparams(("arbitrary",)))(qb, kb, kb, kb, vb, vb, vb, bmap, tab)


def _band_bwd_q(dil, qb, kb, vb, dob, lse, dd, bmap, tab, cb):
    s = qb.shape[0]
    shift = _seg_shift(s, dil)
    n_steps = s // cb

    def body(q_ref, do_ref, lse_ref, dd_ref, kp_ref, kc_ref, kn_ref, vp_ref, vc_ref, vn_ref, bmap_ref, tab_ref,
             dq_ref, dtab_ref, bias_ref, dsum_ref):
        @pl.when(pl.program_id(0) == 0)
        def _():
            _build_bias(bmap_ref, tab_ref, bias_ref)
            dsum_ref[...] = jnp.zeros_like(dsum_ref)

        kw, vw = _window(kp_ref, kc_ref, kn_ref), _window(vp_ref, vc_ref, vn_ref)
        for jj in range(cb // BAND_QB):
            r0 = BAND_QB * jj
            mask = _rows4(_segment_mask(pl.program_id(0) * cb + r0, s // dil, shift, False))
            k3, v3 = kw[r0:r0 + BAND_WIN, :], vw[r0:r0 + BAND_WIN, :]
            sc = _head_scores(q_ref[r0:r0 + BAND_QB, :], k3) + bias_ref[...]
            sc = jnp.where(mask, sc, NEG_INF)
            p = jnp.exp2(sc - _head_cols(lse_ref[r0:r0 + BAND_QB, :]))
            dp = _head_scores(do_ref[r0:r0 + BAND_QB, :], v3)
            ds = p * (dp - _head_cols(dd_ref[r0:r0 + BAND_QB, :]))
            dsum_ref[...] += ds
            dq_ref[r0:r0 + BAND_QB, :] = _head_combine(ds.astype(BF16), k3)

        @pl.when(pl.program_id(0) == n_steps - 1)
        def _():
            bm = bmap_ref[...]
            lane = lax.broadcasted_iota(jnp.int32, (1, LANES), 1)
            for b in range(N_REL_BUCKETS):
                hit = bm == b
                row = jnp.zeros((1, LANES), F32)
                for h in range(N_HEADS_PER_DIL):
                    part = dsum_ref[h * BAND_QB:(h + 1) * BAND_QB, :]
                    row = jnp.where(lane == h, jnp.sum(jnp.where(hit, part, 0.0)), row)
                dtab_ref[b:b + 1, :] = row

    cur, prev, nxt = _band_specs(s, cb)
    return pl.pallas_call(
        body, name=f"band_bwd_q_d{dil}", grid=(n_steps,),
        in_specs=[cur, cur, cur, cur, prev, cur, nxt, prev, cur, nxt, _resident(bmap.shape),
                  pl.BlockSpec(memory_space=pltpu.SMEM)],
        out_specs=[cur, _acc_spec((N_REL_BUCKETS, LANES))],
        out_shape=[jax.ShapeDtypeStruct(qb.shape, F32), jax.ShapeDtypeStruct((N_REL_BUCKETS, LANES), F32)],
        scratch_shapes=[pltpu.VMEM((N_HEADS_PER_DIL * BAND_QB, BAND_WIN), F32),
                        pltpu.VMEM((N_HEADS_PER_DIL * BAND_QB, BAND_WIN), F32)],
        compiler_params=_cparams(("arbitrary",)))(qb, dob, lse, dd, kb, kb, kb, vb, vb, vb, bmap, tab)


def _band_bwd_kv(dil, qb, kb, vb, dob, lse, dd, bmap_t, tab, cb):
    s = qb.shape[0]
    shift = _seg_shift(s, dil)

    def body(k_ref, v_ref, qp_ref, qc_ref, qn_ref, dp_ref, dc_ref, dn_ref, lp_ref, lc_ref, ln_ref,
             ep_ref, ec_ref, en_ref, bmap_ref, tab_ref, dk_ref, dv_ref, bias_ref):
        @pl.when(pl.program_id(0) == 0)
        def _():
            _build_bias(bmap_ref, tab_ref, bias_ref)

        qw, dow = _window(qp_ref, qc_ref, qn_ref), _window(dp_ref, dc_ref, dn_ref)
        lw, ew = _window(lp_ref, lc_ref, ln_ref), _window(ep_ref, ec_ref, en_ref)
        for jj in range(cb // BAND_QB):
            r0 = BAND_QB * jj
            mask = _rows4(_segment_mask(pl.program_id(0) * cb + r0, s // dil, shift, True))
            q3, do3 = qw[r0:r0 + BAND_WIN, :], dow[r0:r0 + BAND_WIN, :]
            sc = _head_scores(q3, k_ref[r0:r0 + BAND_QB, :]) + bias_ref[...]
            sc = jnp.where(mask, sc, NEG_INF)
            p = jnp.exp2(sc - _head_cols(lw[r0:r0 + BAND_WIN, :]))
            ds = p * (_head_scores(do3, v_ref[r0:r0 + BAND_QB, :]) - _head_cols(ew[r0:r0 + BAND_WIN, :]))
            dk_ref[r0:r0 + BAND_QB, :] = _head_combine(ds.astype(BF16), q3, transposed=True)
            dv_ref[r0:r0 + BAND_QB, :] = _head_combine(p.astype(BF16), do3, transposed=True)

    cur, prev, nxt = _band_specs(s, cb)
    win = [prev, cur, nxt]
    return pl.pallas_call(
        body, name=f"band_bwd_kv_d{dil}", grid=(s // cb,),
        in_specs=[cur, cur] + win * 4 + [_resident(bmap_t.shape), pl.BlockSpec(memory_space=pltpu.SMEM)],
        out_specs=[cur, cur],
        out_shape=[jax.ShapeDtypeStruct(qb.shape, F32), jax.ShapeDtypeStruct(qb.shape, F32)],
        scratch_shapes=[pltpu.VMEM((N_HEADS_PER_DIL * BAND_WIN, BAND_QB), F32)],
        compiler_params=_cparams(("arbitrary",)))(
        kb, vb, qb, qb, qb, dob, dob, dob, lse, lse, lse, dd, dd, dd, bmap_t, tab)


def _t5_bucket(rel):
    nb = N_REL_BUCKETS // 2
    ret = (rel > 0).astype(np.int32) * nb
    n = np.abs(rel)
    max_exact = nb // 2
    large = max_exact + (np.log(np.maximum(n, 1) / max_exact) / math.log(REL_MAX_DIST / max_exact)
                         * (nb - max_exact)).astype(np.int32)
    large = np.minimum(large, nb - 1)
    return ret + np.where(n < max_exact, n, large).astype(np.int32)


def _bucket_maps(dil):
    off_qk = np.arange(BAND_WIN)[None, :] - BAND - np.arange(BAND_QB)[:, None]
    off_kq = np.arange(BAND_QB)[None, :] + BAND - np.arange(BAND_WIN)[:, None]
    return [np.where(np.abs(off) <= BAND, _t5_bucket(off * dil), -1).astype(np.int32) for off in (off_qk, off_kq)]


def _seg_sum(v):
    lane = lax.broadcasted_iota(jnp.int32, (1, v.shape[1]), 1)
    out = jnp.zeros_like(v)
    for h in range(v.shape[1] // HEAD_DIM_B):
        m = (lane >= HEAD_DIM_B * h) & (lane < HEAD_DIM_B * (h + 1))
        out = jnp.where(m, jnp.sum(jnp.where(m, v, 0.0), axis=-1, keepdims=True), out)
    return out


def _mix_out(x, oa, og, lg, ga, gb, w_oa, w_ob_t, w_o, tb):
    s, d = x.shape

    def body(x_ref, oa_ref, og0_ref, og1_ref, og2_ref, lg0_ref, lg1_ref, lg2_ref, ga_ref, gb_ref,
             woa_ref, wob_ref, wo_ref, x2_ref, ob_ref, lse0_ref, lse1_ref, lse2_ref, ya_ref, yb_ref, u_ref, scr_ref):
        og_refs, lg_refs = (og0_ref, og1_ref, og2_ref), (lg0_ref, lg1_ref, lg2_ref)
        l0, l1, l2 = [_from_residues(lg_refs[g], scr_ref, dil) for g, dil in enumerate(DILATIONS)]
        lmax = jnp.maximum(jnp.maximum(l0, l1), l2)
        w0, w1, w2 = jnp.exp2(l0 - lmax), jnp.exp2(l1 - lmax), jnp.exp2(l2 - lmax)
        den = w0 + w1 + w2
        o0, o1, o2 = [_from_residues(og_refs[g], scr_ref, dil) for g, dil in enumerate(DILATIONS)]
        ob = ((w0 * o0 + w1 * o1 + w2 * o2) / den).astype(BF16)
        ob_ref[...] = ob
        lse = lmax + jnp.log2(den)
        for g, (dil, ref) in enumerate(zip(DILATIONS, (lse0_ref, lse1_ref, lse2_ref))):
            _to_residues(lse, ref, scr_ref, dil, F32)
        ya = _dot_nn(oa_ref[...], woa_ref[...])
        yb = _dot_nt(ob, wob_ref[...])
        ya_ref[...] = ya.astype(BF16)
        yb_ref[...] = yb.astype(BF16)
        u = (ga_ref[...] * ya + gb_ref[...] * yb).astype(BF16)
        u_ref[...] = u
        x2_ref[...] = x_ref[...] + _dot_nn(u, wo_ref[...])

    sd = jax.ShapeDtypeStruct
    res = list(pl.pallas_call(
        body, name="mix_out", grid=(s // tb,),
        in_specs=[_rows(tb, d), _rows(tb, QA_W)] + _dil_specs(tb) * 2 + [
            _rows(tb, d), _rows(tb, d), _resident(w_oa.shape), _resident(w_ob_t.shape), _resident(w_o.shape)],
        out_specs=[_rows(tb, d), _rows(tb, GB_W)] + _dil_specs(tb) + [_rows(tb, d), _rows(tb, d), _rows(tb, d)],
        out_shape=[sd((s, d), F32), sd((s, GB_W), BF16)] + _dil_shapes(s, F32) + [
            sd((s, d), BF16), sd((s, d), BF16), sd((s, d), BF16)],
        scratch_shapes=[pltpu.VMEM((2, tb, LANES), F32)],
        compiler_params=_cparams(("arbitrary",)))(x, oa, *og, *lg, ga, gb, w_oa, w_ob_t, w_o))
    return res[:2] + [res[2:5]] + res[5:]


def _mlp_fwd(x2, w1_t, w2, g_mlp, tb, tc):
    s, d = x2.shape
    dff = w1_t.shape[0]

    def body(x_ref, w1_ref, w2_ref, g_ref, x3_ref, r_ref, h_ref):
        xv = x_ref[...]
        hb = (xv * _rstd(xv) * g_ref[...]).astype(BF16)
        h_ref[...] = hb
        x3_ref[...] = xv
        for c in range(dff // tc):
            sl = slice(tc * c, tc * c + tc)
            r = jnp.maximum(_dot_nt(hb, w1_ref[sl, :]), 0.0)
            r_ref[:, sl] = r.astype(BF16)
            x3_ref[...] += _dot_nn((r * r).astype(BF16), w2_ref[sl, :])

    sd = jax.ShapeDtypeStruct
    return pl.pallas_call(
        body, name="mlp_fwd", grid=(s // tb,),
        in_specs=[_rows(tb, d), _resident(w1_t.shape), _resident(w2.shape), _resident(g_mlp.shape)],
        out_specs=[_rows(tb, d), _rows(tb, dff), _rows(tb, d)],
        out_shape=[sd((s, d), F32), sd((s, dff), BF16), sd((s, d), BF16)],
        compiler_params=_cparams(("arbitrary",)))(x2, w1_t, w2, g_mlp)


def _ple_loss(x3, p, target, w_pg, w_p_t, g_ple, g_fin, tb):
    s, d = x3.shape
    dp = p.shape[1]

    def body(x_ref, p_ref, t_ref, wpg_ref, wp_ref, gple_ref, gfin_ref,
             dx3_ref, h3_ref, dpre_ref, dpe_ref, pb_ref, loss_ref, dgfin_ref, dgple_ref):
        @pl.when(pl.program_id(0) == 0)
        def _():
            loss_ref[...] = jnp.zeros_like(loss_ref)
            dgfin_ref[...] = jnp.zeros_like(dgfin_ref)
            dgple_ref[...] = jnp.zeros_like(dgple_ref)

        x3v = x_ref[...]
        r3 = _rstd(x3v)
        n3 = x3v * r3
        h3 = (n3 * gple_ref[...]).astype(BF16)
        h3_ref[...] = h3
        gp = _sigmoid(_dot_nn(h3, wpg_ref[...]))
        pb = p_ref[...].astype(BF16)
        pb_ref[...] = pb
        pe = _dot_nt(pb, wp_ref[...])
        x4 = x3v + gp * pe
        r4 = _rstd(x4)
        n4 = x4 * r4
        err = n4 * gfin_ref[...] - t_ref[...]
        loss_ref[...] += jnp.sum(0.5 * jnp.mean(err * err, axis=-1, keepdims=True), axis=0, keepdims=True)
        dy = err / d
        dgfin_ref[...] += _colsum(dy * n4)
        dx4 = _rms_bwd(dy, n4, r4, gfin_ref[...])
        dpe_ref[...] = (dx4 * gp).astype(BF16)
        dpre = (dx4 * pe * gp * (1.0 - gp)).astype(BF16)
        dpre_ref[...] = dpre
        dh3 = _dot_nt(dpre, wpg_ref[...])
        dgple_ref[...] += _colsum(dh3 * n3)
        dx3_ref[...] = dx4 + _rms_bwd(dh3, n3, r3, gple_ref[...])

    sd = jax.ShapeDtypeStruct
    return pl.pallas_call(
        body, name="ple_loss", grid=(s // tb,),
        in_specs=[_rows(tb, d), _rows(tb, dp), _rows(tb, d), _resident(w_pg.shape), _resident(w_p_t.shape),
                  _resident(g_ple.shape), _resident(g_fin.shape)],
        out_specs=[_rows(tb, d), _rows(tb, d), _rows(tb, d), _rows(tb, d), _rows(tb, dp),
                   _acc_spec((1, LANES)), _acc_spec((1, d)), _acc_spec((1, d))],
        out_shape=[sd((s, d), F32), sd((s, d), BF16), sd((s, d), BF16), sd((s, d), BF16), sd((s, dp), BF16),
                   sd((1, LANES), F32), sd((1, d), F32), sd((1, d), F32)],
        compiler_params=_cparams(("arbitrary",)))(x3, p, target, w_pg, w_p_t, g_ple, g_fin)


def _mlp_bwd(dx3, x2, r, w1_t, w2, g_mlp, tb, tc):
    s, d = x2.shape
    dff = w1_t.shape[0]

    def body(dx3_ref, x_ref, r_ref, w1_ref, w2_ref, g_ref, dx2_ref, df_ref, dg_ref, dh_ref):
        @pl.when(pl.program_id(0) == 0)
        def _():
            dg_ref[...] = jnp.zeros_like(dg_ref)

        dx3v = dx3_ref[...]
        dx3b = dx3v.astype(BF16)
        dh_ref[...] = jnp.zeros_like(dh_ref)
        for c in range(dff // tc):
            sl = slice(tc * c, tc * c + tc)
            df = (_dot_nt(dx3b, w2_ref[sl, :]) * (2.0 * r_ref[:, sl].astype(F32))).astype(BF16)
            df_ref[:, sl] = df
            dh_ref[...] += _dot_nn(df, w1_ref[sl, :])
        xv = x_ref[...]
        r2 = _rstd(xv)
        n2 = xv * r2
        dh = dh_ref[...]
        dg_ref[...] += _colsum(dh * n2)
        dx2_ref[...] = dx3v + _rms_bwd(dh, n2, r2, g_ref[...])

    sd = jax.ShapeDtypeStruct
    return pl.pallas_call(
        body, name="mlp_bwd", grid=(s // tb,),
        in_specs=[_rows(tb, d), _rows(tb, d), _rows(tb, dff), _resident(w1_t.shape), _resident(w2.shape),
                  _resident(g_mlp.shape)],
        out_specs=[_rows(tb, d), _rows(tb, dff), _acc_spec((1, d))],
        out_shape=[sd((s, d), F32), sd((s, dff), BF16), sd((1, d), F32)],
        scratch_shapes=[pltpu.VMEM((tb, d), F32)],
        compiler_params=_cparams(("arbitrary",)))(dx3, x2, r, w1_t, w2, g_mlp)


def _mix_out_bwd(dx2, ya, yb, ga, gb, ob, w_oa, w_ob_t, w_o, tb, after):
    s, d = dx2.shape

    def body(dx_ref, ya_ref, yb_ref, ga_ref, gb_ref, ob_ref, woa_ref, wob_ref, wo_ref, after_ref,
             doa_ref, dob0_ref, dob1_ref, dob2_ref, dd0_ref, dd1_ref, dd2_ref, dga_ref, dgb_ref, dya_ref, dyb_ref,
             dbg_ref, scr_ref):
        @pl.when(pl.program_id(0) == 0)
        def _():
            dbg_ref[...] = jnp.zeros_like(dbg_ref)

        du = _dot_nt(dx_ref[...].astype(BF16), wo_ref[...])
        gav, gbv = ga_ref[...], gb_ref[...]
        dya = (du * gav).astype(BF16)
        dyb = (du * gbv).astype(BF16)
        dya_ref[...] = dya
        dyb_ref[...] = dyb
        dga = du * ya_ref[...].astype(F32) * gav * (1.0 - gav)
        dgb = du * yb_ref[...].astype(F32) * gbv * (1.0 - gbv)
        dga_ref[...] = dga.astype(BF16)
        dgb_ref[...] = dgb.astype(BF16)
        dbg_ref[:, 0:d] += _colsum(dga)
        dbg_ref[:, d:2 * d] += _colsum(dgb)
        doa_ref[...] = _dot_nt(dya, woa_ref[...]).astype(BF16)
        dob = _dot_nn(dyb, wob_ref[...])
        dd = _seg_sum(dob * ob_ref[...].astype(F32))
        for dil, dob_ref, dd_ref in zip(DILATIONS, (dob0_ref, dob1_ref, dob2_ref), (dd0_ref, dd1_ref, dd2_ref)):
            _to_residues(dob, dob_ref, scr_ref, dil, BF16)
            _to_residues(dd, dd_ref, scr_ref, dil, F32)

    sd = jax.ShapeDtypeStruct
    res = list(pl.pallas_call(
        body, name="mix_out_bwd", grid=(s // tb,),
        in_specs=[_rows(tb, d)] * 5 + [_rows(tb, GB_W), _resident(w_oa.shape), _resident(w_ob_t.shape),
                                       _resident(w_o.shape), _ANY],
        out_specs=[_rows(tb, QA_W)] + _dil_specs(tb) * 2 + [_rows(tb, d), _rows(tb, d), _rows(tb, d),
                                                           _rows(tb, d), _acc_spec((1, 2 * d))],
        out_shape=[sd((s, QA_W), BF16)] + _dil_shapes(s, BF16) + _dil_shapes(s, F32) + [
            sd((s, d), BF16), sd((s, d), BF16), sd((s, d), BF16), sd((s, d), BF16), sd((1, 2 * d), F32)],
        scratch_shapes=[pltpu.VMEM((2, tb, LANES), F32)],
        compiler_params=_cparams(("arbitrary",)))(dx2, ya, yb, ga, gb, ob, w_oa, w_ob_t, w_o, after))
    return res[:1] + [res[1:4], res[4:7]] + res[7:]


def _in_proj_bwd(dx2, x, dqrot, dkrot, dva, qraw, kraw, tabs, dqb, dkb, dvb, dga, dgb, w_in_t, g_mix, q_g, k_g, tb):
    s, d = x.shape
    din = w_in_t.shape[0]
    q_scale = HEAD_DIM_A ** -0.5
    b_scale = HEAD_DIM_B ** -0.5
    tc = 256

    def body(dx2_ref, x_ref, dq_ref, dk_ref, dv_ref, qraw_ref, kraw_ref, c_ref, s1_ref, s2_ref, *rest):
        dqb_refs, dkb_refs, dvb_refs = rest[0:3], rest[3:6], rest[6:9]
        (dga_ref, dgb_ref, w_ref, gmix_ref, qg_ref, kg_ref,
         dx_ref, dz_ref, dgmix_ref, dqg_ref, dkg_ref, dh_ref, scr_ref) = rest[9:]

        @pl.when(pl.program_id(0) == 0)
        def _():
            dgmix_ref[...] = jnp.zeros_like(dgmix_ref)
            dqg_ref[...] = jnp.zeros_like(dqg_ref)
            dkg_ref[...] = jnp.zeros_like(dkg_ref)

        cos, s1, s2 = c_ref[...], s1_ref[...], s2_ref[...]

        def head_bwd(drot, z, g_ref, acc_ref):
            dn = _rope_bwd(drot, cos, s1, s2)
            rr = _rstd(z)
            nn = z * rr
            acc_ref[...] += _colsum(dn * nn)
            return _rms_bwd(dn, nn, rr, g_ref[...])

        for h in range(N_Q_HEADS_A):
            sl = slice(128 * h, 128 * h + 128)
            dz_ref[:, OFF_QA + 128 * h:OFF_QA + 128 * h + 128] = head_bwd(
                dq_ref[:, sl] * q_scale, qraw_ref[:, sl], qg_ref, dqg_ref).astype(BF16)
        for h in range(N_KV_HEADS_A):
            sl = slice(128 * h, 128 * h + 128)
            dz_ref[:, OFF_KA + 128 * h:OFF_KA + 128 * h + 128] = head_bwd(
                dk_ref[sl, :].T * LN_2, kraw_ref[:, sl], kg_ref, dkg_ref).astype(BF16)
        dz_ref[:, OFF_VA:OFF_VA + KA_W] = dv_ref[...].T.astype(BF16)
        for g, dil in enumerate(DILATIONS):
            dz_ref[:, OFF_QB + GB_W * g:OFF_QB + GB_W * (g + 1)] = (
                _from_residues(dqb_refs[g], scr_ref, dil) * b_scale).astype(BF16)
            dz_ref[:, OFF_KB + GB_W * g:OFF_KB + GB_W * (g + 1)] = (
                _from_residues(dkb_refs[g], scr_ref, dil) * LN_2).astype(BF16)
            dz_ref[:, OFF_VB + GB_W * g:OFF_VB + GB_W * (g + 1)] = _from_residues(dvb_refs[g], scr_ref, dil).astype(BF16)
        dz_ref[:, OFF_GA:OFF_GA + d] = dga_ref[...]
        dz_ref[:, OFF_GA + d:OFF_GA + 2 * d] = dgb_ref[...]
        dh_ref[...] = jnp.zeros_like(dh_ref)
        for c in range(din // tc):
            sl = slice(tc * c, tc * c + tc)
            dh_ref[...] += _dot_nn(dz_ref[:, sl], w_ref[sl, :])
        xv = x_ref[...]
        r1 = _rstd(xv)
        n1 = xv * r1
        dh = dh_ref[...]
        dgmix_ref[...] += _colsum(dh * n1)
        dx_ref[...] = dx2_ref[...] + _rms_bwd(dh, n1, r1, gmix_ref[...])

    sd = jax.ShapeDtypeStruct
    return pl.pallas_call(
        body, name="in_proj_bwd", grid=(s // tb,),
        in_specs=[_rows(tb, d), _rows(tb, d), _rows(tb, QA_W), pl.BlockSpec((KA_W, tb), lambda i: (0, i)),
                  pl.BlockSpec((KA_W, tb), lambda i: (0, i)), _rows(tb, QA_W),
                  _rows(tb, KA_W), _rows(tb, LANES), _rows(tb, LANES), _rows(tb, LANES),
                  ] + _dil_specs(tb) * 3 + [_rows(tb, d), _rows(tb, d),
                  _resident(w_in_t.shape), _resident(g_mix.shape), _resident(q_g.shape), _resident(k_g.shape)],
        out_specs=[_rows(tb, d), _rows(tb, din), _acc_spec((1, d)), _acc_spec((1, HEAD_DIM_A)),
                   _acc_spec((1, HEAD_DIM_A))],
        out_shape=[sd((s, d), F32), sd((s, din), BF16), sd((1, d), F32), sd((1, HEAD_DIM_A), F32),
                   sd((1, HEAD_DIM_A), F32)],
        scratch_shapes=[pltpu.VMEM((tb, d), F32), pltpu.VMEM((2, tb, LANES), F32)],
        compiler_params=_cparams(("arbitrary",)))(
        dx2, x, dqrot, dkrot, dva, qraw, kraw, *tabs, *dqb, *dkb, *dvb, dga, dgb, w_in_t, g_mix, q_g, k_g)


def _identity(v):
    return v


def _to_bf16(v):
    return v.astype(BF16)


def _square_bf16(v):
    vf = v.astype(F32)
    return (vf * vf).astype(BF16)


def _weight_grad(name, a, b, ti, tj, tk, a_fn=_identity, b_fn=_identity, col0=0, n=None, after=None):
    t, m = a.shape
    n = b.shape[1] if n is None else n
    n_k = t // tk
    after = a if after is None else after

    def body(a_ref, b_ref, after_ref, o_ref, acc_ref):
        k = pl.program_id(2)

        @pl.when(k == 0)
        def _():
            acc_ref[...] = jnp.zeros_like(acc_ref)

        acc_ref[...] += _dot_tn(a_fn(a_ref[...]), b_fn(b_ref[...]))

        @pl.when(k == n_k - 1)
        def _():
            o_ref[...] = acc_ref[...].astype(BF16)

    return pl.pallas_call(
        body, name=name, grid=(m // ti, n // tj, n_k),
        in_specs=[pl.BlockSpec((tk, ti), lambda i, j, k: (k, i)),
                  pl.BlockSpec((tk, tj), lambda i, j, k: (k, j + col0 // tj)), _ANY],
        out_specs=pl.BlockSpec((ti, tj), lambda i, j, k: (i, j)),
        out_shape=jax.ShapeDtypeStruct((m, n), BF16),
        scratch_shapes=[pltpu.VMEM((ti, tj), F32)],
        compiler_params=_cparams(("arbitrary", "arbitrary", "arbitrary")))(a, b, after)


def _sum_slots(name, recv, own):
    m, n, k = recv.shape
    tc = min(k, 256)

    def body(own_ref, r_ref, o_ref):
        acc = own_ref[...].astype(F32)
        for i in range(m):
            acc = acc + r_ref[i].astype(F32)
        o_ref[...] = acc

    return pl.pallas_call(
        body, name=name, grid=(k // tc,),
        in_specs=[pl.BlockSpec((n, tc), lambda j: (0, j)), pl.BlockSpec((m, n, tc), lambda j: (0, 0, j))],
        out_specs=pl.BlockSpec((n, tc), lambda j: (0, j)),
        out_shape=jax.ShapeDtypeStruct((n, k), F32),
        compiler_params=_cparams(("arbitrary",)))(own, recv)


def _adamw_math(w, g, m, v):
    m = ADAM_B1 * m + (1.0 - ADAM_B1) * g
    v = ADAM_B2 * v + (1.0 - ADAM_B2) * (g * g)
    m_hat = m / (1.0 - ADAM_B1 ** ADAM_STEP)
    v_hat = v / (1.0 - ADAM_B2 ** ADAM_STEP)
    delta = -ADAM_LR * (m_hat / (jnp.sqrt(v_hat) + ADAM_EPS) + ADAM_WD * w)
    return delta, m, v


def _adamw(name, w, g, m, v):
    r, c = w.shape
    tr = min(r, 256)

    def body(w_ref, g_ref, m_ref, v_ref, d_ref, mo_ref, vo_ref):
        d_ref[...], mo_ref[...], vo_ref[...] = _adamw_math(w_ref[...], g_ref[...], m_ref[...], v_ref[...])

    spec = pl.BlockSpec((tr, c), lambda i: (i, 0))
    return pl.pallas_call(
        body, name=name, grid=(r // tr,), in_specs=[spec] * 4, out_specs=[spec] * 3,
        out_shape=[jax.ShapeDtypeStruct((r, c), F32)] * 3,
        compiler_params=_cparams(("arbitrary",)))(w, g, m, v)


def _small_update(parts, w, m, v):
    def body(p_ref, w_ref, m_ref, v_ref, g_ref, d_ref, mo_ref, vo_ref):
        g = p_ref[0]
        for i in range(1, N_DEV):
            g = g + p_ref[i]
        g_ref[...] = g
        d_ref[...], mo_ref[...], vo_ref[...] = _adamw_math(w_ref[...], g, m_ref[...], v_ref[...])

    return pl.pallas_call(body, name="small_update", out_shape=[jax.ShapeDtypeStruct(w.shape, F32)] * 4)(
        parts, w, m, v)


def _pack_rows(vectors, n_rows):
    flat = jnp.concatenate([v.reshape(-1).astype(F32) for v in vectors])
    flat = jnp.pad(flat, (0, n_rows * LANES - flat.shape[0]))
    return flat.reshape(n_rows, LANES)


def _pick_tile(n, prefs):
    for t in prefs:
        if n % t == 0:
            return t
    return n


def kernel(x, p, norm_mix_g, w_in, b_gate, q_norm_g, k_norm_g, rel_bias, w_out_a, w_out_b, w_out, norm_mlp_g, w_ff1, w_ff2, norm_ple_g, w_ple_gate, w_ple, final_norm_g, loss_target, m_norm_mix_g, m_w_in, m_b_gate, m_q_norm_g, m_k_norm_g, m_rel_bias, m_w_out_a, m_w_out_b, m_w_out, m_norm_mlp_g, m_w_ff1, m_w_ff2, m_norm_ple_g, m_w_ple_gate, m_w_ple, m_final_norm_g, v_norm_mix_g, v_w_in, v_b_gate, v_q_norm_g, v_k_norm_g, v_rel_bias, v_w_out_a, v_w_out_b, v_w_out, v_norm_mlp_g, v_w_ff1, v_w_ff2, v_norm_ple_g, v_w_ple_gate, v_w_ple, v_final_norm_g):
    s, d = x.shape[1], x.shape[2]
    xs, ps, ts = x[0], p[0, 0], loss_target[0]
    tb = _pick_tile(s, (512, 256))
    tq = _pick_tile(s, (256,))
    tk = _pick_tile(s, (1024, 512))
    cb = _pick_tile(s, (512,))
    fin_g = final_norm_g.reshape(1, d)

    col_sharded = {"w_in": w_in[0], "w_out_b": w_out_b[0], "w_ff1": w_ff1[0], "w_ple": w_ple[0]}
    row_sharded = {"w_out_a": w_out_a[0], "w_out": w_out[0], "w_ff2": w_ff2[0], "w_ple_gate": w_ple_gate[0]}
    order = ["w_in", "w_out_a", "w_out_b", "w_out", "w_ff1", "w_ff2", "w_ple_gate", "w_ple"]
    shards = [(col_sharded[n].T if n in col_sharded else row_sharded[n]).astype(BF16) for n in order]
    my_idx = 4 * lax.axis_index("x") + 2 * lax.axis_index("y") + lax.axis_index("c")
    (w_in_t,) = _all_gather(shards[:1])
    zones = [lax.dynamic_update_slice(lax.empty((N_DEV * sh.shape[0], sh.shape[1]), BF16), sh,
                                      (my_idx * sh.shape[0], 0)) for sh in shards[1:]]
    ag = _copies_start("weights_gather_start", shards[1:], zones, w_in_t, True)

    tabs = _rope_tables(s)
    (h1, qraw, kraw, qrot, krot, va, qb, kb, vb, ga, gb) = _in_proj(
        xs, tabs, w_in_t, norm_mix_g, b_gate, q_norm_g, k_norm_g, tb, ag[4])
    oa, lse_a = _attn_a_fwd(qrot, krot, va, tq, tk)
    _, (w_oa, w_ob_t, w_o, w_ff1_t, w_ff2_f, w_pg, w_p_t) = _copies_wait(
        "weights_gather_wait", ag[0], ag[1], ag[2], ag[3], lse_a, True)
    flat = lambda arrs: [a.reshape(s, GB_W) for a in arrs]
    split = lambda arrs: [a.reshape(dil, s // dil, GB_W) for a, dil in zip(arrs, DILATIONS)]
    qb_r, kb_r, vb_r = flat(qb), flat(kb), flat(vb)
    bmaps = [[jnp.asarray(m) for m in _bucket_maps(dil)] for dil in DILATIONS]
    bias_tabs = [rel_bias[:, N_HEADS_PER_DIL * g:N_HEADS_PER_DIL * (g + 1)] for g in range(3)]
    band_out = [_band_fwd(dil, qb_r[g], kb_r[g], vb_r[g], bmaps[g][0], bias_tabs[g], cb)
                for g, dil in enumerate(DILATIONS)]
    og, lg = split([o for o, _ in band_out]), split([l for _, l in band_out])
    x2, ob, lse_b, ya, yb, u = _mix_out(xs, oa, og, lg, ga, gb, w_oa, w_ob_t, w_o, tb)
    tc = _pick_tile(w_ff1_t.shape[0], (512,))
    x3, r_act, h2 = _mlp_fwd(x2, w_ff1_t, w_ff2_f, norm_mlp_g, tb, tc)

    dx3, h3, dpre, dpe, pb, loss_part, dg_fin, dg_ple = _ple_loss(
        x3, ps, ts, w_pg, w_p_t, norm_ple_g, fin_g, tb)
    dx2, df, dg_mlp = _mlp_bwd(dx3, x2, r_act, w_ff1_t, w_ff2_f, norm_mlp_g, tb, tc)

    tkk = _pick_tile(s, (1024, 512))
    dff = w_ff1_t.shape[0]
    t1k = lambda n: _pick_tile(n, (1024, 512, 256))
    slots = lambda parts: [lax.empty((7, a.shape[0] // N_DEV, a.shape[1]), BF16) for a in parts]
    part1 = [_weight_grad("grad_w_ff1", df, h2, t1k(dff), t1k(d), tkk),
             _weight_grad("grad_w_ff2", r_act, dx3, t1k(dff), t1k(d), tkk, a_fn=_square_bf16, b_fn=_to_bf16),
             _weight_grad("grad_w_ple_gate", h3, dpre, t1k(d), t1k(d), tkk),
             _weight_grad("grad_w_ple", dpe, pb, t1k(d), ps.shape[1], tkk)]
    rs1 = _copies_start("grads1_start", part1, slots(part1), dx2, False)
    doa, dob, dd, dga, dgb, dya, dyb, dbg = _mix_out_bwd(dx2, ya, yb, ga, gb, ob, w_oa, w_ob_t, w_o, tb, rs1[4])
    part2 = [_weight_grad("grad_w_out_a", oa, dya, t1k(QA_W), t1k(d), tkk),
             _weight_grad("grad_w_out_b", dyb, ob, t1k(d), GB_W, tkk),
             _weight_grad("grad_w_out", u, dx2, t1k(d), t1k(d), tkk, b_fn=_to_bf16)]
    rs2 = _copies_start("grads2_start", part2, slots(part2), doa, False)
    dqrot, dkrot, dva = _attn_a_bwd(qrot, krot, va, oa, doa, lse_a, tq, tk, rs2[4])
    dob_r, lse_r, dd_r = flat(dob), flat(lse_b), flat(dd)
    bwd_q = [_band_bwd_q(dil, qb_r[g], kb_r[g], vb_r[g], dob_r[g], lse_r[g], dd_r[g], bmaps[g][0], bias_tabs[g], cb)
             for g, dil in enumerate(DILATIONS)]
    bwd_kv = [_band_bwd_kv(dil, qb_r[g], kb_r[g], vb_r[g], dob_r[g], lse_r[g], dd_r[g], bmaps[g][1], bias_tabs[g], cb)
              for g, dil in enumerate(DILATIONS)]
    dqb, dkb, dvb = split([r[0] for r in bwd_q]), split([r[0] for r in bwd_kv]), split([r[1] for r in bwd_kv])
    grad_x, dz, dg_mix, dg_q, dg_k = _in_proj_bwd(
        dx2, xs, dqrot, dkrot, dva, qraw, kraw, tabs, dqb, dkb, dvb, dga, dgb, w_in_t, norm_mix_g,
        q_norm_g, k_norm_g, _pick_tile(s, (256,)))
    d_rel = jnp.concatenate([r[1][:, :N_HEADS_PER_DIL] for r in bwd_q], axis=1)

    din = w_in_t.shape[0]
    ti_in = _pick_tile(din, (din // 2,)) if (din // 2) % LANES == 0 else din
    hd_ = d // 2
    part3 = [_weight_grad("grad_w_in_lo", dz, h1, ti_in, t1k(hd_), tkk, n=hd_)]
    rs3 = _copies_start("grads3_start", part3, slots(part3), grad_x, False)
    part4 = [_weight_grad("grad_w_in_hi", dz, h1, ti_in, t1k(hd_), tkk, col0=hd_, n=hd_, after=rs3[4])]
    rs4 = _copies_start("grads4_start", part4, slots(part4), rs3[4], False)

    def own_rows(a):
        n = a.shape[0] // N_DEV
        return lax.dynamic_slice(a, (my_idx * n, 0), (n, a.shape[1]))

    sums = {}
    src1, got1 = _copies_wait("grads1_wait", rs1[0], rs1[1], rs1[2], rs1[3], rs4[4], False)
    src2, got2 = _copies_wait("grads2_wait", rs2[0], rs2[1], rs2[2], rs2[3], rs4[4], False)
    for n, a, r in zip(["w_ff1", "w_ff2", "w_ple_gate", "w_ple", "w_out_a", "w_out_b", "w_out"],
                       src1 + src2, got1 + got2):
        sums[n] = _sum_slots("sum_" + n, r, own_rows(a))
    given_w = dict(w_in=w_in, w_out_a=w_out_a, w_out_b=w_out_b, w_out=w_out, w_ff1=w_ff1, w_ff2=w_ff2,
                   w_ple_gate=w_ple_gate, w_ple=w_ple)
    given_m = dict(w_in=m_w_in, w_out_a=m_w_out_a, w_out_b=m_w_out_b, w_out=m_w_out, w_ff1=m_w_ff1, w_ff2=m_w_ff2,
                   w_ple_gate=m_w_ple_gate, w_ple=m_w_ple)
    given_v = dict(w_in=v_w_in, w_out_a=v_w_out_a, w_out_b=v_w_out_b, w_out=v_w_out, w_ff1=v_w_ff1, w_ff2=v_w_ff2,
                   w_ple_gate=v_w_ple_gate, w_ple=v_w_ple)
    big = {}

    def update(n):
        g = sums[n].T if n in col_sharded else sums[n]
        delta, new_m, new_v = _adamw("adamw_" + n, given_w[n][0], g, given_m[n][0], given_v[n][0])
        big[n] = tuple(a[None] for a in (g, delta, new_m, new_v))

    for n in order[1:]:
        update(n)
    src3, got3 = _copies_wait("grads3_wait", rs3[0], rs3[1], rs3[2], rs3[3], big["w_ple"][1], False)
    src4, got4 = _copies_wait("grads4_wait", rs4[0], rs4[1], rs4[2], rs4[3], big["w_ple"][1], False)
    sums["w_in"] = jnp.concatenate([_sum_slots("sum_w_in_lo", got3[0], own_rows(src3[0])),
                                    _sum_slots("sum_w_in_hi", got4[0], own_rows(src4[0]))], axis=1)
    update("w_in")

    small_names = ["norm_mix_g", "b_gate", "q_norm_g", "k_norm_g", "rel_bias", "norm_mlp_g", "norm_ple_g",
                   "final_norm_g"]
    small_w = [norm_mix_g, b_gate, q_norm_g, k_norm_g, rel_bias, norm_mlp_g, norm_ple_g, final_norm_g]
    small_m = [m_norm_mix_g, m_b_gate, m_q_norm_g, m_k_norm_g, m_rel_bias, m_norm_mlp_g, m_norm_ple_g,
               m_final_norm_g]
    small_v = [v_norm_mix_g, v_b_gate, v_q_norm_g, v_k_norm_g, v_rel_bias, v_norm_mlp_g, v_norm_ple_g,
               v_final_norm_g]
    small_g = [dg_mix, dbg, dg_q, dg_k, d_rel, dg_mlp, dg_ple, dg_fin]
    sizes = [int(np.prod(w.shape)) for w in small_w]
    n_rows = -(-(sum(-(-sz // LANES) for sz in sizes) + 1) // 8) * 8
    pad = lambda v: jnp.pad(v.reshape(-1).astype(F32), (0, -v.size % LANES))
    pack = lambda vs, last: _pack_rows([pad(v) for v in vs] + [last], n_rows)
    zero_row = jnp.zeros((LANES,), F32)
    parts = _small_all_gather(pack(small_g, loss_part.reshape(-1) * (jnp.arange(LANES) == 0)))
    g_all, d_all, m_all, v_all = _small_update(parts, pack(small_w, zero_row), pack(small_m, zero_row),
                                               pack(small_v, zero_row))
    small = {}
    row = 0
    for n, w, sz in zip(small_names, small_w, sizes):
        nr = -(-sz // LANES)
        small[n] = tuple(a[row:row + nr].reshape(-1)[:sz].reshape(w.shape) for a in (g_all, d_all, m_all, v_all))
        row += nr
    loss = g_all[row, 0]

    names = ["norm_mix_g", "w_in", "b_gate", "q_norm_g", "k_norm_g", "rel_bias", "w_out_a", "w_out_b", "w_out",
             "norm_mlp_g", "w_ff1", "w_ff2", "norm_ple_g", "w_ple_gate", "w_ple", "final_norm_g"]
    res = {n: (big[n] if n in big else small[n]) for n in names}
    return (loss, grad_x[None], *[res[n][0] for n in names], *[res[n][1] for n in names],
            *[res[n][2] for n in names], *[res[n][3] for n in names])
```

```python
import functools
import math

import numpy as np
import jax
import jax.numpy as jnp
from jax import lax
from jax.experimental import pallas as pl
from jax.experimental.pallas import tpu as pltpu

F32 = jnp.float32
BF16 = jnp.bfloat16
MESH = pl.DeviceIdType.MESH

NORM_EPS = 1e-6
NEG_INF = -1e30
LOG2_E = math.log2(math.e)
LN_2 = math.log(2.0)
GRID_W = 64
ROPE_THETA = 10000.0
HEAD_DIM_A = 128
N_Q_HEADS_A = 8
N_KV_HEADS_A = 2
Q_PER_KV = N_Q_HEADS_A // N_KV_HEADS_A
HEAD_DIM_B = 64
N_HEADS_PER_DIL = 4
DILATIONS = (1, 4, 16)
BAND = 64
N_REL_BUCKETS = 32
REL_MAX_DIST = 1024
QA_W = N_Q_HEADS_A * HEAD_DIM_A
KA_W = N_KV_HEADS_A * HEAD_DIM_A
GB_W = N_HEADS_PER_DIL * HEAD_DIM_B
QB_W = GB_W * len(DILATIONS)
OFF_QA, OFF_KA, OFF_VA = 0, QA_W, QA_W + KA_W
OFF_QB = QA_W + 2 * KA_W
OFF_KB = OFF_QB + QB_W
OFF_VB = OFF_KB + QB_W
OFF_GA = OFF_VB + QB_W
N_DEV = 8
LANES = 128
VMEM_LIMIT = 56 * 2 ** 20

ADAM_LR, ADAM_B1, ADAM_B2, ADAM_EPS, ADAM_WD, ADAM_STEP = 0.001, 0.9, 0.999, 1e-08, 0.01, 10


def _cparams(sem):
    return pltpu.CompilerParams(dimension_semantics=sem, vmem_limit_bytes=VMEM_LIMIT)


def _resident(shape):
    nd = len(shape)
    return pl.BlockSpec(shape, lambda *_: (0,) * nd, pipeline_mode=pl.Buffered(1))


def _acc_spec(shape):
    nd = len(shape)
    return pl.BlockSpec(shape, lambda *_: (0,) * nd)


def _rows(tb, c):
    return pl.BlockSpec((tb, c), lambda i: (i, 0))


def _dil_shapes(s, dtype):
    return [jax.ShapeDtypeStruct((dil, s // dil, GB_W), dtype) for dil in DILATIONS]


def _dil_specs(tb):
    return [pl.BlockSpec((dil, tb // dil, GB_W), lambda i: (0, i, 0)) for dil in DILATIONS]


def _to_residues(val, out_ref, scr_ref, dil, dtype):
    if dil == 1:
        out_ref[0] = val.astype(dtype)
        return
    n = val.shape[0] // dil
    scr_ref[0] = val[:, :LANES]
    scr_ref[1] = val[:, LANES:]
    for r in range(dil):
        out_ref[r] = jnp.concatenate([scr_ref[0, pl.ds(r, n, stride=dil), :],
                                      scr_ref[1, pl.ds(r, n, stride=dil), :]], axis=1).astype(dtype)


def _from_residues(in_ref, scr_ref, dil):
    if dil == 1:
        return in_ref[0]
    n = in_ref.shape[1]
    for r in range(dil):
        v = in_ref[r]
        scr_ref[0, pl.ds(r, n, stride=dil), :] = v[:, :LANES]
        scr_ref[1, pl.ds(r, n, stride=dil), :] = v[:, LANES:]
    return jnp.concatenate([scr_ref[0], scr_ref[1]], axis=1)


def _dot_nt(a, b):
    return lax.dot_general(a, b, (((1,), (1,)), ((), ())), preferred_element_type=F32)


def _dot_nn(a, b):
    return lax.dot_general(a, b, (((1,), (0,)), ((), ())), preferred_element_type=F32)


def _dot_tn(a, b):
    return lax.dot_general(a, b, (((0,), (0,)), ((), ())), preferred_element_type=F32)


def _rstd(x):
    return lax.rsqrt(jnp.mean(x * x, axis=-1, keepdims=True) + NORM_EPS)


def _rms_bwd(dy, n, r, g):
    dn = dy * g
    return r * (dn - n * jnp.mean(dn * n, axis=-1, keepdims=True))


def _colsum(v):
    return jnp.sum(v, axis=0, keepdims=True)


def _sigmoid(v):
    return 1.0 / (1.0 + jnp.exp(-v))


def _rope_fwd(n, c, s1, s2):
    return n * c + pltpu.roll(n, 32, 1) * s1 + pltpu.roll(n, 96, 1) * s2


def _rope_bwd(d, c, s1, s2):
    return d * c + pltpu.roll(d * s1, 96, 1) + pltpu.roll(d * s2, 32, 1)


def _rope_tables(s):
    half = HEAD_DIM_A // 2
    inv = jnp.power(ROPE_THETA, -jnp.arange(0, half, 2, dtype=F32) / half)
    n_rows = s // GRID_W
    ang_r = jnp.arange(n_rows, dtype=F32)[:, None] * inv[None, :]
    ang_c = jnp.arange(GRID_W, dtype=F32)[:, None] * inv[None, :]
    cr, sr = jnp.repeat(jnp.cos(ang_r), GRID_W, axis=0), jnp.repeat(jnp.sin(ang_r), GRID_W, axis=0)
    cc, sc = jnp.tile(jnp.cos(ang_c), (n_rows, 1)), jnp.tile(jnp.sin(ang_c), (n_rows, 1))
    z = jnp.zeros_like(sr)
    cos = jnp.concatenate([cr, cr, cc, cc], axis=1)
    s1 = jnp.concatenate([z, sr, z, sc], axis=1)
    s2 = jnp.concatenate([-sr, z, -sc, z], axis=1)
    return cos, s1, s2


def _my_place():
    return lax.axis_index("x"), lax.axis_index("y"), lax.axis_index("c")


def _all_gather(shards):
    nw = len(shards)

    def body(*refs):
        ins, outs = refs[:nw], refs[nw:2 * nw]
        send_sems, recv_sems, local_sems = refs[2 * nw:]
        x, y, c = _my_place()
        me, sibling = (x, y, c), (x, y, 1 - c)
        chips = [(1 - x, y), (x, 1 - y), (1 - x, 1 - y)]

        def rows(w, px, py, pc):
            n = ins[w].shape[0]
            return outs[w].at[pl.ds(pl.multiple_of((4 * px + 2 * py + pc) * n, 16), n), :]

        def copy(w, k, block, to, src=None):
            return pltpu.make_async_remote_copy(
                src_ref=rows(w, *block) if src is None else src, dst_ref=rows(w, *block),
                send_sem=send_sems.at[w, k], recv_sem=recv_sems.at[w, k], device_id=to, device_id_type=MESH)

        mine = [pltpu.make_async_copy(ins[w], rows(w, *me), local_sems.at[w]) for w in range(nw)]
        for cp in mine:
            cp.start()
        first = []
        for w in range(nw):
            first.append(copy(w, 0, me, sibling, src=ins[w]))
            first += [copy(w, 1 + j, me, (*chip, c), src=ins[w]) for j, chip in enumerate(chips)]
        for cp in first:
            cp.start()
        passed = []
        for j, chip in enumerate(chips):
            for w in range(nw):
                copy(w, 1 + j, (*chip, c), me).wait_recv()
                fwd = copy(w, 4 + j, (*chip, c), sibling)
                fwd.start()
                passed.append(fwd)
        for w in range(nw):
            copy(w, 0, sibling, me).wait_recv()
        for j, chip in enumerate(chips):
            for w in range(nw):
                copy(w, 4 + j, (*chip, 1 - c), me).wait_recv()
        for cp in first + passed:
            cp.wait_send()
        for cp in mine:
            cp.wait()

    any_spec = pl.BlockSpec(memory_space=pl.ANY)
    return pl.pallas_call(
        body, name="weights_all_gather",
        out_shape=[jax.ShapeDtypeStruct((N_DEV * s.shape[0], s.shape[1]), s.dtype) for s in shards],
        in_specs=[any_spec] * nw, out_specs=[any_spec] * nw,
        scratch_shapes=[pltpu.SemaphoreType.DMA((nw, 7)), pltpu.SemaphoreType.DMA((nw, 7)),
                        pltpu.SemaphoreType.DMA((nw,))],
    )(*shards)


_FLIPS = [(fx, fy, fc) for fx in (0, 1) for fy in (0, 1) for fc in (0, 1)][1:]


def _small_all_gather(v):
    def body(v_ref, out_ref, send_sems, recv_sems):
        x, y, c = _my_place()
        my_idx = 4 * x + 2 * y + c
        out_ref[my_idx] = v_ref[...]
        sends = []
        for k, (fx, fy, fc) in enumerate(_FLIPS):
            to = (1 - x if fx else x, 1 - y if fy else y, 1 - c if fc else c)
            sends.append(pltpu.make_async_remote_copy(
                src_ref=v_ref, dst_ref=out_ref.at[my_idx], send_sem=send_sems.at[k], recv_sem=recv_sems.at[k],
                device_id=to, device_id_type=MESH))
        for cp in sends:
            cp.start()
        for k, (fx, fy, fc) in enumerate(_FLIPS):
            frm_idx = 4 * (1 - x if fx else x) + 2 * (1 - y if fy else y) + (1 - c if fc else c)
            pltpu.make_async_remote_copy(
                src_ref=v_ref, dst_ref=out_ref.at[frm_idx], send_sem=send_sems.at[k], recv_sem=recv_sems.at[k],
                device_id=(x, y, c), device_id_type=MESH).wait_recv()
        for cp in sends:
            cp.wait_send()

    vm = pl.BlockSpec(memory_space=pltpu.VMEM)
    return pl.pallas_call(
        body, name="small_all_gather", out_shape=jax.ShapeDtypeStruct((N_DEV,) + v.shape, v.dtype),
        in_specs=[vm], out_specs=vm,
        scratch_shapes=[pltpu.SemaphoreType.DMA((7,)), pltpu.SemaphoreType.DMA((7,))],
    )(v)


_HBM = pl.BlockSpec(memory_space=pltpu.HBM)
_SEM = pl.BlockSpec(memory_space=pltpu.SEMAPHORE)
_ANY = pl.BlockSpec(memory_space=pl.ANY)
_SPLIT_COPY = dict(has_side_effects=pltpu.SideEffectType.DATAFLOW_SIDE_EFFECTING)


def _peer(x, y, c, k):
    fx, fy, fc = _FLIPS[k]
    return (1 - x if fx else x, 1 - y if fy else y, 1 - c if fc else c)


def _in_hbm(a):
    return pltpu.with_memory_space_constraint(a, pltpu.HBM)


def _split_copies(srcs, lands, send_sems, recv_sems, gather, arriving):
    x, y, c = _my_place()
    my_idx = 4 * x + 2 * y + c
    out = []
    for k in range(7):
        to = _peer(x, y, c, k)
        to_idx = 4 * to[0] + 2 * to[1] + to[2]
        for w in range(len(srcs)):
            if gather:
                n = srcs[w].shape[0]
                src = srcs[w]
                dst = lands[w].at[pl.ds(pl.multiple_of((to_idx if arriving else my_idx) * n, 16), n), :]
            else:
                n = lands[w].shape[1]
                src = srcs[w].at[pl.ds(pl.multiple_of(to_idx * n, 16), n), :]
                dst = lands[w].at[k]
            out.append(pltpu.make_async_remote_copy(
                src_ref=src, dst_ref=dst, send_sem=send_sems.at[7 * w + k], recv_sem=recv_sems.at[7 * w + k],
                device_id=to, device_id_type=MESH))
    return out


def _copies_start(name, srcs, lands, after, gather):
    nw = len(srcs)

    def body(*refs):
        send_sems, recv_sems = refs[2 * nw + 1], refs[2 * nw + 2]
        for cp in _split_copies(refs[:nw], refs[nw:2 * nw], send_sems, recv_sems, gather, False):
            cp.start()
        refs[-1][...] = jnp.zeros_like(refs[-1])

    sems = pltpu.SemaphoreType.DMA((7 * nw,))
    thru = [pltpu.HBM(a.shape, a.dtype) for a in list(srcs) + list(lands)]
    res = pl.pallas_call(
        body, name=name, out_shape=(sems, sems, *thru, jax.ShapeDtypeStruct((8, LANES), F32)),
        in_specs=[_HBM] * (2 * nw) + [_ANY], out_specs=(_SEM, _SEM, *[_HBM] * (2 * nw), pl.BlockSpec(memory_space=pltpu.VMEM)),
        input_output_aliases={i: 2 + i for i in range(2 * nw)},
        compiler_params=pltpu.CompilerParams(**_SPLIT_COPY),
    )(*[_in_hbm(a) for a in srcs], *[_in_hbm(a) for a in lands], after)
    return res[0], res[1], list(res[2:2 + nw]), list(res[2 + nw:2 + 2 * nw]), res[-1]


def _copies_wait(name, send_sems, recv_sems, srcs, lands, after, gather):
    nw = len(srcs)

    def body(*refs):
        for cp in _split_copies(refs[:nw], refs[nw:2 * nw], refs[2 * nw], refs[2 * nw + 1], gather, False):
            cp.wait_send()
        for cp in _split_copies(refs[:nw], refs[nw:2 * nw], refs[2 * nw], refs[2 * nw + 1], gather, True):
            cp.wait_recv()

    thru = [pltpu.HBM(a.shape, a.dtype) for a in list(srcs) + list(lands)]
    res = pl.pallas_call(
        body, name=name, out_shape=tuple(thru),
        in_specs=[_HBM] * (2 * nw) + [_SEM, _SEM, _ANY], out_specs=tuple([_HBM] * (2 * nw)),
        input_output_aliases={i: i for i in range(2 * nw)},
        compiler_params=pltpu.CompilerParams(**_SPLIT_COPY),
    )(*srcs, *lands, send_sems, recv_sems, after)
    return list(res[:nw]), list(res[nw:])


def _in_proj(x, tabs, w_in_t, g_mix, b_gate, q_g, k_g, tb, after):
    s, d = x.shape
    n_gate_chunks = d // 256
    q_scale = HEAD_DIM_A ** -0.5 * LOG2_E
    b_scale = HEAD_DIM_B ** -0.5 * LOG2_E

    def body(x_ref, c_ref, s1_ref, s2_ref, w_ref, gmix_ref, bg_ref, qg_ref, kg_ref, after_ref,
             h1_ref, qraw_ref, kraw_ref, qrot_ref, krot_ref, va_ref, *rest):
        qb_refs, kb_refs, vb_refs = rest[0:3], rest[3:6], rest[6:9]
        ga_ref, gb_ref, scr_ref = rest[9:]
        xv = x_ref[...]
        hb = (xv * _rstd(xv) * gmix_ref[...]).astype(BF16)
        h1_ref[...] = hb
        cos, s1, s2 = c_ref[...], s1_ref[...], s2_ref[...]

        def proj(lo, width):
            return _dot_nt(hb, w_ref[lo:lo + width, :])

        def norm_rope(z, g):
            return _rope_fwd(z * _rstd(z) * g, cos, s1, s2)

        for j in range(QA_W // 256):
            z = proj(OFF_QA + 256 * j, 256)
            qraw_ref[:, 256 * j:256 * j + 256] = z
            for hh in range(2):
                lo = 256 * j + 128 * hh
                qrot_ref[:, lo:lo + 128] = (norm_rope(z[:, 128 * hh:128 * hh + 128], qg_ref[...]) * q_scale).astype(BF16)
        z = proj(OFF_KA, 256)
        kraw_ref[...] = z
        for hh in range(2):
            krot_ref[:, 128 * hh:128 * hh + 128] = norm_rope(z[:, 128 * hh:128 * hh + 128], kg_ref[...]).astype(BF16)
        va_ref[...] = proj(OFF_VA, 256).astype(BF16)
        for g, dil in enumerate(DILATIONS):
            _to_residues(proj(OFF_QB + GB_W * g, GB_W) * b_scale, qb_refs[g], scr_ref, dil, BF16)
            _to_residues(proj(OFF_KB + GB_W * g, GB_W), kb_refs[g], scr_ref, dil, BF16)
            _to_residues(proj(OFF_VB + GB_W * g, GB_W), vb_refs[g], scr_ref, dil, BF16)
        for j in range(n_gate_chunks):
            sl = slice(256 * j, 256 * j + 256)
            ga_ref[:, sl] = _sigmoid(proj(OFF_GA + 256 * j, 256) + bg_ref[:, sl])
            gb_ref[:, sl] = _sigmoid(proj(OFF_GA + d + 256 * j, 256) + bg_ref[:, d + 256 * j:d + 256 * j + 256])

    sd = jax.ShapeDtypeStruct
    outs = [sd((s, d), BF16), sd((s, QA_W), F32), sd((s, KA_W), F32), sd((s, QA_W), BF16), sd((s, KA_W), BF16),
            sd((s, KA_W), BF16)] + _dil_shapes(s, BF16) * 3 + [sd((s, d), F32), sd((s, d), F32)]
    out_specs = [_rows(tb, d), _rows(tb, QA_W), _rows(tb, KA_W), _rows(tb, QA_W), _rows(tb, KA_W), _rows(tb, KA_W)
                 ] + _dil_specs(tb) * 3 + [_rows(tb, d), _rows(tb, d)]
    in_specs = [_rows(tb, d), _rows(tb, LANES), _rows(tb, LANES), _rows(tb, LANES), _resident(w_in_t.shape),
                _resident(g_mix.shape), _resident(b_gate.shape), _resident(q_g.shape), _resident(k_g.shape), _ANY]
    res = list(pl.pallas_call(body, name="in_proj", grid=(s // tb,), in_specs=in_specs, out_specs=out_specs,
                              out_shape=outs, scratch_shapes=[pltpu.VMEM((2, tb, LANES), F32)],
                              compiler_params=_cparams(("arbitrary",)))(
        x, *tabs, w_in_t, g_mix, b_gate, q_g, k_g, after))
    return res[:6] + [res[6:9], res[9:12], res[12:15]] + res[15:]


def _attn_a_fwd(qrot, krot, va, tq, tk):
    s = qrot.shape[0]
    n_kv = s // tk
    gw = Q_PER_KV * HEAD_DIM_A

    def body(q_ref, k_ref, v_ref, o_ref, lse_ref):
        q4 = jnp.concatenate([q_ref[:, 128 * h:128 * h + 128] for h in range(Q_PER_KV)], axis=0)

        def step(j, carry):
            m, l, acc = carry
            sl = pl.ds(pl.multiple_of(j * tk, tk), tk)
            kj, vj = k_ref[sl, :], v_ref[sl, :]
            sc = _dot_nt(kj, q4)
            m_new = jnp.maximum(m, jnp.max(sc, axis=0, keepdims=True))
            p = jnp.exp2(sc - m_new)
            alpha = jnp.exp2(m - m_new)
            l = alpha * l + jnp.sum(p, axis=0, keepdims=True)
            acc = alpha * acc + _dot_tn(vj, p.astype(BF16))
            return m_new, l, acc

        rows = Q_PER_KV * tq
        m, l, acc = lax.fori_loop(0, n_kv, step, (jnp.full((1, rows), NEG_INF, F32), jnp.zeros((1, rows), F32),
                                                  jnp.zeros((HEAD_DIM_A, rows), F32)))
        o = (acc / l).T
        lse = m + jnp.log2(l)
        for h in range(Q_PER_KV):
            o_ref[:, 128 * h:128 * h + 128] = o[h * tq:(h + 1) * tq].astype(BF16)
            lse_ref[0, h:h + 1, :] = lse[:, h * tq:(h + 1) * tq]

    return pl.pallas_call(
        body, name="attn_a_fwd", grid=(N_KV_HEADS_A, s // tq),
        in_specs=[pl.BlockSpec((tq, gw), lambda g, i: (i, g)),
                  pl.BlockSpec((s, HEAD_DIM_A), lambda g, i: (0, g)),
                  pl.BlockSpec((s, HEAD_DIM_A), lambda g, i: (0, g))],
        out_specs=[pl.BlockSpec((tq, gw), lambda g, i: (i, g)),
                   pl.BlockSpec((1, Q_PER_KV, tq), lambda g, i: (g, 0, i))],
        out_shape=[jax.ShapeDtypeStruct((s, QA_W), BF16), jax.ShapeDtypeStruct((N_KV_HEADS_A, Q_PER_KV, s), F32)],
        compiler_params=_cparams(("arbitrary", "arbitrary")))(qrot, krot, va)


def _attn_a_bwd(qrot, krot, va, oa, doa, lse, tq, tk, after):
    s = qrot.shape[0]
    n_kv = s // tk
    gw = Q_PER_KV * HEAD_DIM_A

    def body(q_ref, do_ref, o_ref, lse_ref, k_ref, v_ref, after_ref, dq_ref, dk_ref, dv_ref):
        @pl.when(pl.program_id(1) == 0)
        def _():
            dk_ref[...] = jnp.zeros_like(dk_ref)
            dv_ref[...] = jnp.zeros_like(dv_ref)

        def stack(ref):
            return jnp.concatenate([ref[:, 128 * h:128 * h + 128] for h in range(Q_PER_KV)], axis=0)

        q4, do4, o4 = stack(q_ref), stack(do_ref), stack(o_ref)
        q4t, do4t = q4.T, do4.T
        delta = jnp.sum((do4.astype(F32) * o4.astype(F32)).T, axis=0, keepdims=True)
        lse4 = jnp.concatenate([lse_ref[0, h:h + 1, :] for h in range(Q_PER_KV)], axis=1)

        def step(j, dq):
            sl = pl.ds(pl.multiple_of(j * tk, tk), tk)
            kj, vj = k_ref[sl, :], v_ref[sl, :]
            p = jnp.exp2(_dot_nt(kj, q4) - lse4)
            ds = (p * (_dot_nt(vj, do4) - delta)).astype(BF16)
            dk_ref[:, sl] += _dot_nt(q4t, ds)
            dv_ref[:, sl] += _dot_nt(do4t, p.astype(BF16))
            return dq + _dot_tn(kj, ds)

        dq = lax.fori_loop(0, n_kv, step, jnp.zeros((HEAD_DIM_A, Q_PER_KV * tq), F32)).T
        for h in range(Q_PER_KV):
            dq_ref[:, 128 * h:128 * h + 128] = dq[h * tq:(h + 1) * tq]

    qspec = pl.BlockSpec((tq, gw), lambda g, i: (i, g))
    kspec = pl.BlockSpec((s, HEAD_DIM_A), lambda g, i: (0, g))
    ktspec = pl.BlockSpec((HEAD_DIM_A, s), lambda g, i: (g, 0))
    return pl.pallas_call(
        body, name="attn_a_bwd", grid=(N_KV_HEADS_A, s // tq),
        in_specs=[qspec, qspec, qspec, pl.BlockSpec((1, Q_PER_KV, tq), lambda g, i: (g, 0, i)), kspec, kspec, _ANY],
        out_specs=[qspec, ktspec, ktspec],
        out_shape=[jax.ShapeDtypeStruct((s, QA_W), F32), jax.ShapeDtypeStruct((KA_W, s), F32),
                   jax.ShapeDtypeStruct((KA_W, s), F32)],
        compiler_params=_cparams(("arbitrary", "arbitrary")))(qrot, doa, oa, lse, krot, va, after)


BAND_QB = 256
BAND_WIN = BAND_QB + 2 * BAND


def _band_specs(s, cb):
    per = cb // BAND
    last = s // BAND - 1
    cur = pl.BlockSpec((cb, GB_W), lambda i: (i, 0))
    prev = pl.BlockSpec((BAND, GB_W), lambda i: (jnp.maximum(i * per - 1, 0), 0))
    nxt = pl.BlockSpec((BAND, GB_W), lambda i: (jnp.minimum(i * per + per, last), 0))
    return cur, prev, nxt


def _window(prev_ref, cur_ref, next_ref):
    return jnp.concatenate([prev_ref[...], cur_ref[...], next_ref[...]], axis=0)


def _band_mask(base, seg_shift, window_rows):
    shape = (BAND_WIN, BAND_QB) if window_rows else (BAND_QB, BAND_WIN)
    a = lax.broadcasted_iota(jnp.int32, shape, 0)
    b = lax.broadcasted_iota(jnp.int32, shape, 1)
    rq, rk = (base - BAND + a, base + b) if window_rows else (base + a, base - BAND + b)
    same_segment = lax.shift_right_arithmetic(rq, jnp.int32(seg_shift)) == lax.shift_right_arithmetic(rk, jnp.int32(seg_shift))
    return (jnp.abs(rk - rq) <= BAND) & same_segment


def _build_bias(bmap_ref, tab_ref, bias_ref):
    bm = bmap_ref[...]
    acc = [jnp.full(bm.shape, NEG_INF, F32) for _ in range(N_HEADS_PER_DIL)]
    for b in range(N_REL_BUCKETS):
        hit = bm == b
        for h in range(N_HEADS_PER_DIL):
            acc[h] = jnp.where(hit, tab_ref[b, h] * LOG2_E, acc[h])
    rows = bm.shape[0]
    for h in range(N_HEADS_PER_DIL):
        bias_ref[h * rows:(h + 1) * rows, :] = acc[h]


def _segment_mask(base, seg_len, seg_shift, window_rows):
    if seg_len % BAND_QB:
        return _band_mask(base, seg_shift, window_rows)
    pos = lax.rem(base, seg_len)
    shape, dim = ((BAND_WIN, 1), 0) if window_rows else ((1, BAND_WIN), 1)
    w = lax.broadcasted_iota(jnp.int32, shape, dim)
    return ((w >= BAND) | (pos != 0)) & ((w < BAND + BAND_QB) | (pos != seg_len - BAND_QB))


def _head_lane_masks():
    lane = lax.broadcasted_iota(jnp.int32, (1, LANES), 1)
    return [lane < HEAD_DIM_B, lane >= HEAD_DIM_B]


def _rows4(mask):
    return mask if mask.shape[0] == 1 else jnp.concatenate([mask] * N_HEADS_PER_DIL, axis=0)


def _head_scores(a, b):
    hm = _head_lane_masks()
    out = []
    for hp in range(2):
        ls = slice(LANES * hp, LANES * hp + LANES)
        ah = a[:, ls]
        both = jnp.concatenate([jnp.where(hm[0], ah, jnp.zeros_like(ah)), jnp.where(hm[1], ah, jnp.zeros_like(ah))],
                               axis=0)
        out.append(_dot_nt(both, b[:, ls]))
    return jnp.concatenate(out, axis=0)


def _head_combine(p, v, scale=None, transposed=False):
    hm = _head_lane_masks()
    rows = p.shape[0] // N_HEADS_PER_DIL
    halves = []
    for hp in range(2):
        vh = v[:, LANES * hp:LANES * hp + LANES]
        acc = None
        for hh in range(2):
            h = 2 * hp + hh
            ph = p[h * rows:(h + 1) * rows]
            vm = jnp.where(hm[hh], vh, jnp.zeros_like(vh))
            t = _dot_tn(ph, vm) if transposed else _dot_nn(ph, vm)
            if scale is not None:
                t = t * scale[h * rows:(h + 1) * rows]
            acc = t if acc is None else acc + t
        halves.append(acc)
    return jnp.concatenate(halves, axis=1)


def _head_spread(col):
    rows = col.shape[0] // N_HEADS_PER_DIL
    lane = lax.broadcasted_iota(jnp.int32, (1, GB_W), 1)
    out = jnp.zeros((rows, GB_W), F32)
    for h in range(N_HEADS_PER_DIL):
        out = jnp.where((lane >= HEAD_DIM_B * h) & (lane < HEAD_DIM_B * (h + 1)), col[h * rows:(h + 1) * rows], out)
    return out


def _head_cols(v):
    return jnp.concatenate([v[:, HEAD_DIM_B * h:HEAD_DIM_B * h + 1] for h in range(N_HEADS_PER_DIL)], axis=0)


def _seg_shift(s, dil):
    seg = s // dil
    assert seg & (seg - 1) == 0, "segment length must be a power of two"
    return seg.bit_length() - 1


def _band_fwd(dil, qb, kb, vb, bmap, tab, cb):
    s = qb.shape[0]
    shift = _seg_shift(s, dil)

    def body(q_ref, kp_ref, kc_ref, kn_ref, vp_ref, vc_ref, vn_ref, bmap_ref, tab_ref, o_ref, lse_ref, bias_ref):
        @pl.when(pl.program_id(0) == 0)
        def _():
            _build_bias(bmap_ref, tab_ref, bias_ref)

        kw, vw = _window(kp_ref, kc_ref, kn_ref), _window(vp_ref, vc_ref, vn_ref)
        for jj in range(cb // BAND_QB):
            r0 = BAND_QB * jj
            mask = _rows4(_segment_mask(pl.program_id(0) * cb + r0, s // dil, shift, False))
            sc = _head_scores(q_ref[r0:r0 + BAND_QB, :], kw[r0:r0 + BAND_WIN, :]) + bias_ref[...]
            sc = jnp.where(mask, sc, NEG_INF)
            m = jnp.max(sc, axis=-1, keepdims=True)
            e = jnp.exp2(sc - m)
            l = jnp.sum(e, axis=-1, keepdims=True)
            o = _head_combine(e.astype(BF16), vw[r0:r0 + BAND_WIN, :], 1.0 / l)
            o_ref[r0:r0 + BAND_QB, :] = o
            lse_ref[r0:r0 + BAND_QB, :] = _head_spread(m + jnp.log2(l))

    cur, prev, nxt = _band_specs(s, cb)
    return pl.pallas_call(
        body, name=f"band_fwd_d{dil}", grid=(s // cb,),
        in_specs=[cur, prev, cur, nxt, prev, cur, nxt, _resident(bmap.shape), pl.BlockSpec(memory_space=pltpu.SMEM)],
        out_specs=[cur, cur],
        out_shape=[jax.ShapeDtypeStruct(qb.shape, F32), jax.ShapeDtypeStruct(qb.shape, F32)],
        scratch_shapes=[pltpu.VMEM((N_HEADS_PER_DIL * BAND_QB, BAND_WIN), F32)],
        compiler_params=_cparams(("arbitrary",)))(qb, kb, kb, kb, vb, vb, vb, bmap, tab)


def _band_bwd_q(dil, qb, kb, vb, dob, lse, dd, bmap, tab, cb):
    s = qb.shape[0]
    shift = _seg_shift(s, dil)
    n_steps = s // cb

    def body(q_ref, do_ref, lse_ref, dd_ref, kp_ref, kc_ref, kn_ref, vp_ref, vc_ref, vn_ref, bmap_ref, tab_ref,
             dq_ref, dtab_ref, bias_ref, dsum_ref):
        @pl.when(pl.program_id(0) == 0)
        def _():
            _build_bias(bmap_ref, tab_ref, bias_ref)
            dsum_ref[...] = jnp.zeros_like(dsum_ref)

        kw, vw = _window(kp_ref, kc_ref, kn_ref), _window(vp_ref, vc_ref, vn_ref)
        for jj in range(cb // BAND_QB):
            r0 = BAND_QB * jj
            mask = _rows4(_segment_mask(pl.program_id(0) * cb + r0, s // dil, shift, False))
            k3, v3 = kw[r0:r0 + BAND_WIN, :], vw[r0:r0 + BAND_WIN, :]
            sc = _head_scores(q_ref[r0:r0 + BAND_QB, :], k3) + bias_ref[...]
            sc = jnp.where(mask, sc, NEG_INF)
            p = jnp.exp2(sc - _head_cols(lse_ref[r0:r0 + BAND_QB, :]))
            dp = _head_scores(do_ref[r0:r0 + BAND_QB, :], v3)
            ds = p * (dp - _head_cols(dd_ref[r0:r0 + BAND_QB, :]))
            dsum_ref[...] += ds
            dq_ref[r0:r0 + BAND_QB, :] = _head_combine(ds.astype(BF16), k3)

        @pl.when(pl.program_id(0) == n_steps - 1)
        def _():
            bm = bmap_ref[...]
            lane = lax.broadcasted_iota(jnp.int32, (1, LANES), 1)
            for b in range(N_REL_BUCKETS):
                hit = bm == b
                row = jnp.zeros((1, LANES), F32)
                for h in range(N_HEADS_PER_DIL):
                    part = dsum_ref[h * BAND_QB:(h + 1) * BAND_QB, :]
                    row = jnp.where(lane == h, jnp.sum(jnp.where(hit, part, 0.0)), row)
                dtab_ref[b:b + 1, :] = row

    cur, prev, nxt = _band_specs(s, cb)
    return pl.pallas_call(
        body, name=f"band_bwd_q_d{dil}", grid=(n_steps,),
        in_specs=[cur, cur, cur, cur, prev, cur, nxt, prev, cur, nxt, _resident(bmap.shape),
                  pl.BlockSpec(memory_space=pltpu.SMEM)],
        out_specs=[cur, _acc_spec((N_REL_BUCKETS, LANES))],
        out_shape=[jax.ShapeDtypeStruct(qb.shape, F32), jax.ShapeDtypeStruct((N_REL_BUCKETS, LANES), F32)],
        scratch_shapes=[pltpu.VMEM((N_HEADS_PER_DIL * BAND_QB, BAND_WIN), F32),
                        pltpu.VMEM((N_HEADS_PER_DIL * BAND_QB, BAND_WIN), F32)],
        compiler_params=_cparams(("arbitrary",)))(qb, dob, lse, dd, kb, kb, kb, vb, vb, vb, bmap, tab)


def _band_bwd_kv(dil, qb, kb, vb, dob, lse, dd, bmap_t, tab, cb):
    s = qb.shape[0]
    shift = _seg_shift(s, dil)

    def body(k_ref, v_ref, qp_ref, qc_ref, qn_ref, dp_ref, dc_ref, dn_ref, lp_ref, lc_ref, ln_ref,
             ep_ref, ec_ref, en_ref, bmap_ref, tab_ref, dk_ref, dv_ref, bias_ref):
        @pl.when(pl.program_id(0) == 0)
        def _():
            _build_bias(bmap_ref, tab_ref, bias_ref)

        qw, dow = _window(qp_ref, qc_ref, qn_ref), _window(dp_ref, dc_ref, dn_ref)
        lw, ew = _window(lp_ref, lc_ref, ln_ref), _window(ep_ref, ec_ref, en_ref)
        for jj in range(cb // BAND_QB):
            r0 = BAND_QB * jj
            mask = _rows4(_segment_mask(pl.program_id(0) * cb + r0, s // dil, shift, True))
            q3, do3 = qw[r0:r0 + BAND_WIN, :], dow[r0:r0 + BAND_WIN, :]
            sc = _head_scores(q3, k_ref[r0:r0 + BAND_QB, :]) + bias_ref[...]
            sc = jnp.where(mask, sc, NEG_INF)
            p = jnp.exp2(sc - _head_cols(lw[r0:r0 + BAND_WIN, :]))
            ds = p * (_head_scores(do3, v_ref[r0:r0 + BAND_QB, :]) - _head_cols(ew[r0:r0 + BAND_WIN, :]))
            dk_ref[r0:r0 + BAND_QB, :] = _head_combine(ds.astype(BF16), q3, transposed=True)
            dv_ref[r0:r0 + BAND_QB, :] = _head_combine(p.astype(BF16), do3, transposed=True)

    cur, prev, nxt = _band_specs(s, cb)
    win = [prev, cur, nxt]
    return pl.pallas_call(
        body, name=f"band_bwd_kv_d{dil}", grid=(s // cb,),
        in_specs=[cur, cur] + win * 4 + [_resident(bmap_t.shape), pl.BlockSpec(memory_space=pltpu.SMEM)],
        out_specs=[cur, cur],
        out_shape=[jax.ShapeDtypeStruct(qb.shape, F32), jax.ShapeDtypeStruct(qb.shape, F32)],
        scratch_shapes=[pltpu.VMEM((N_HEADS_PER_DIL * BAND_WIN, BAND_QB), F32)],
        compiler_params=_cparams(("arbitrary",)))(
        kb, vb, qb, qb, qb, dob, dob, dob, lse, lse, lse, dd, dd, dd, bmap_t, tab)


def _t5_bucket(rel):
    nb = N_REL_BUCKETS // 2
    ret = (rel > 0).astype(np.int32) * nb
    n = np.abs(rel)
    max_exact = nb // 2
    large = max_exact + (np.log(np.maximum(n, 1) / max_exact) / math.log(REL_MAX_DIST / max_exact)
                         * (nb - max_exact)).astype(np.int32)
    large = np.minimum(large, nb - 1)
    return ret + np.where(n < max_exact, n, large).astype(np.int32)


def _bucket_maps(dil):
    off_qk = np.arange(BAND_WIN)[None, :] - BAND - np.arange(BAND_QB)[:, None]
    off_kq = np.arange(BAND_QB)[None, :] + BAND - np.arange(BAND_WIN)[:, None]
    return [np.where(np.abs(off) <= BAND, _t5_bucket(off * dil), -1).astype(np.int32) for off in (off_qk, off_kq)]


def _seg_sum(v):
    lane = lax.broadcasted_iota(jnp.int32, (1, v.shape[1]), 1)
    out = jnp.zeros_like(v)
    for h in range(v.shape[1] // HEAD_DIM_B):
        m = (lane >= HEAD_DIM_B * h) & (lane < HEAD_DIM_B * (h + 1))
        out = jnp.where(m, jnp.sum(jnp.where(m, v, 0.0), axis=-1, keepdims=True), out)
    return out


def _mix_out(x, oa, og, lg, ga, gb, w_oa, w_ob_t, w_o, tb):
    s, d = x.shape

    def body(x_ref, oa_ref, og0_ref, og1_ref, og2_ref, lg0_ref, lg1_ref, lg2_ref, ga_ref, gb_ref,
             woa_ref, wob_ref, wo_ref, x2_ref, ob_ref, lse0_ref, lse1_ref, lse2_ref, ya_ref, yb_ref, u_ref, scr_ref):
        og_refs, lg_refs = (og0_ref, og1_ref, og2_ref), (lg0_ref, lg1_ref, lg2_ref)
        l0, l1, l2 = [_from_residues(lg_refs[g], scr_ref, dil) for g, dil in enumerate(DILATIONS)]
        lmax = jnp.maximum(jnp.maximum(l0, l1), l2)
        w0, w1, w2 = jnp.exp2(l0 - lmax), jnp.exp2(l1 - lmax), jnp.exp2(l2 - lmax)
        den = w0 + w1 + w2
        o0, o1, o2 = [_from_residues(og_refs[g], scr_ref, dil) for g, dil in enumerate(DILATIONS)]
        ob = ((w0 * o0 + w1 * o1 + w2 * o2) / den).astype(BF16)
        ob_ref[...] = ob
        lse = lmax + jnp.log2(den)
        for g, (dil, ref) in enumerate(zip(DILATIONS, (lse0_ref, lse1_ref, lse2_ref))):
            _to_residues(lse, ref, scr_ref, dil, F32)
        ya = _dot_nn(oa_ref[...], woa_ref[...])
        yb = _dot_nt(ob, wob_ref[...])
        ya_ref[...] = ya.astype(BF16)
        yb_ref[...] = yb.astype(BF16)
        u = (ga_ref[...] * ya + gb_ref[...] * yb).astype(BF16)
        u_ref[...] = u
        x2_ref[...] = x_ref[...] + _dot_nn(u, wo_ref[...])

    sd = jax.ShapeDtypeStruct
    res = list(pl.pallas_call(
        body, name="mix_out", grid=(s // tb,),
        in_specs=[_rows(tb, d), _rows(tb, QA_W)] + _dil_specs(tb) * 2 + [
            _rows(tb, d), _rows(tb, d), _resident(w_oa.shape), _resident(w_ob_t.shape), _resident(w_o.shape)],
        out_specs=[_rows(tb, d), _rows(tb, GB_W)] + _dil_specs(tb) + [_rows(tb, d), _rows(tb, d), _rows(tb, d)],
        out_shape=[sd((s, d), F32), sd((s, GB_W), BF16)] + _dil_shapes(s, F32) + [
            sd((s, d), BF16), sd((s, d), BF16), sd((s, d), BF16)],
        scratch_shapes=[pltpu.VMEM((2, tb, LANES), F32)],
        compiler_params=_cparams(("arbitrary",)))(x, oa, *og, *lg, ga, gb, w_oa, w_ob_t, w_o))
    return res[:2] + [res[2:5]] + res[5:]


def _mlp_fwd(x2, w1_t, w2, g_mlp, tb, tc):
    s, d = x2.shape
    dff = w1_t.shape[0]

    def body(x_ref, w1_ref, w2_ref, g_ref, x3_ref, r_ref, h_ref):
        xv = x_ref[...]
        hb = (xv * _rstd(xv) * g_ref[...]).astype(BF16)
        h_ref[...] = hb
        x3_ref[...] = xv
        for c in range(dff // tc):
            sl = slice(tc * c, tc * c + tc)
            r = jnp.maximum(_dot_nt(hb, w1_ref[sl, :]), 0.0)
            r_ref[:, sl] = r.astype(BF16)
            x3_ref[...] += _dot_nn((r * r).astype(BF16), w2_ref[sl, :])

    sd = jax.ShapeDtypeStruct
    return pl.pallas_call(
        body, name="mlp_fwd", grid=(s // tb,),
        in_specs=[_rows(tb, d), _resident(w1_t.shape), _resident(w2.shape), _resident(g_mlp.shape)],
        out_specs=[_rows(tb, d), _rows(tb, dff), _rows(tb, d)],
        out_shape=[sd((s, d), F32), sd((s, dff), BF16), sd((s, d), BF16)],
        compiler_params=_cparams(("arbitrary",)))(x2, w1_t, w2, g_mlp)


def _ple_loss(x3, p, target, w_pg, w_p_t, g_ple, g_fin, tb):
    s, d = x3.shape
    dp = p.shape[1]

    def body(x_ref, p_ref, t_ref, wpg_ref, wp_ref, gple_ref, gfin_ref,
             dx3_ref, h3_ref, dpre_ref, dpe_ref, pb_ref, loss_ref, dgfin_ref, dgple_ref):
        @pl.when(pl.program_id(0) == 0)
        def _():
            loss_ref[...] = jnp.zeros_like(loss_ref)
            dgfin_ref[...] = jnp.zeros_like(dgfin_ref)
            dgple_ref[...] = jnp.zeros_like(dgple_ref)

        x3v = x_ref[...]
        r3 = _rstd(x3v)
        n3 = x3v * r3
        h3 = (n3 * gple_ref[...]).astype(BF16)
        h3_ref[...] = h3
        gp = _sigmoid(_dot_nn(h3, wpg_ref[...]))
        pb = p_ref[...].astype(BF16)
        pb_ref[...] = pb
        pe = _dot_nt(pb, wp_ref[...])
        x4 = x3v + gp * pe
        r4 = _rstd(x4)
        n4 = x4 * r4
        err = n4 * gfin_ref[...] - t_ref[...]
        loss_ref[...] += jnp.sum(0.5 * jnp.mean(err * err, axis=-1, keepdims=True), axis=0, keepdims=True)
        dy = err / d
        dgfin_ref[...] += _colsum(dy * n4)
        dx4 = _rms_bwd(dy, n4, r4, gfin_ref[...])
        dpe_ref[...] = (dx4 * gp).astype(BF16)
        dpre = (dx4 * pe * gp * (1.0 - gp)).astype(BF16)
        dpre_ref[...] = dpre
        dh3 = _dot_nt(dpre, wpg_ref[...])
        dgple_ref[...] += _colsum(dh3 * n3)
        dx3_ref[...] = dx4 + _rms_bwd(dh3, n3, r3, gple_ref[...])

    sd = jax.ShapeDtypeStruct
    return pl.pallas_call(
        body, name="ple_loss", grid=(s // tb,),
        in_specs=[_rows(tb, d), _rows(tb, dp), _rows(tb, d), _resident(w_pg.shape), _resident(w_p_t.shape),
                  _resident(g_ple.shape), _resident(g_fin.shape)],
        out_specs=[_rows(tb, d), _rows(tb, d), _rows(tb, d), _rows(tb, d), _rows(tb, dp),
                   _acc_spec((1, LANES)), _acc_spec((1, d)), _acc_spec((1, d))],
        out_shape=[sd((s, d), F32), sd((s, d), BF16), sd((s, d), BF16), sd((s, d), BF16), sd((s, dp), BF16),
                   sd((1, LANES), F32), sd((1, d), F32), sd((1, d), F32)],
        compiler_params=_cparams(("arbitrary",)))(x3, p, target, w_pg, w_p_t, g_ple, g_fin)


def _mlp_bwd(dx3, x2, r, w1_t, w2, g_mlp, tb, tc):
    s, d = x2.shape
    dff = w1_t.shape[0]

    def body(dx3_ref, x_ref, r_ref, w1_ref, w2_ref, g_ref, dx2_ref, df_ref, dg_ref, dh_ref):
        @pl.when(pl.program_id(0) == 0)
        def _():
            dg_ref[...] = jnp.zeros_like(dg_ref)

        dx3v = dx3_ref[...]
        dx3b = dx3v.astype(BF16)
        dh_ref[...] = jnp.zeros_like(dh_ref)
        for c in range(dff // tc):
            sl = slice(tc * c, tc * c + tc)
            df = (_dot_nt(dx3b, w2_ref[sl, :]) * (2.0 * r_ref[:, sl].astype(F32))).astype(BF16)
            df_ref[:, sl] = df
            dh_ref[...] += _dot_nn(df, w1_ref[sl, :])
        xv = x_ref[...]
        r2 = _rstd(xv)
        n2 = xv * r2
        dh = dh_ref[...]
        dg_ref[...] += _colsum(dh * n2)
        dx2_ref[...] = dx3v + _rms_bwd(dh, n2, r2, g_ref[...])

    sd = jax.ShapeDtypeStruct
    return pl.pallas_call(
        body, name="mlp_bwd", grid=(s // tb,),
        in_specs=[_rows(tb, d), _rows(tb, d), _rows(tb, dff), _resident(w1_t.shape), _resident(w2.shape),
                  _resident(g_mlp.shape)],
        out_specs=[_rows(tb, d), _rows(tb, dff), _acc_spec((1, d))],
        out_shape=[sd((s, d), F32), sd((s, dff), BF16), sd((1, d), F32)],
        scratch_shapes=[pltpu.VMEM((tb, d), F32)],
        compiler_params=_cparams(("arbitrary",)))(dx3, x2, r, w1_t, w2, g_mlp)


def _mix_out_bwd(dx2, ya, yb, ga, gb, ob, w_oa, w_ob_t, w_o, tb, after):
    s, d = dx2.shape

    def body(dx_ref, ya_ref, yb_ref, ga_ref, gb_ref, ob_ref, woa_ref, wob_ref, wo_ref, after_ref,
             doa_ref, dob0_ref, dob1_ref, dob2_ref, dd0_ref, dd1_ref, dd2_ref, dga_ref, dgb_ref, dya_ref, dyb_ref,
             dbg_ref, scr_ref):
        @pl.when(pl.program_id(0) == 0)
        def _():
            dbg_ref[...] = jnp.zeros_like(dbg_ref)

        du = _dot_nt(dx_ref[...].astype(BF16), wo_ref[...])
        gav, gbv = ga_ref[...], gb_ref[...]
        dya = (du * gav).astype(BF16)
        dyb = (du * gbv).astype(BF16)
        dya_ref[...] = dya
        dyb_ref[...] = dyb
        dga = du * ya_ref[...].astype(F32) * gav * (1.0 - gav)
        dgb = du * yb_ref[...].astype(F32) * gbv * (1.0 - gbv)
        dga_ref[...] = dga.astype(BF16)
        dgb_ref[...] = dgb.astype(BF16)
        dbg_ref[:, 0:d] += _colsum(dga)
        dbg_ref[:, d:2 * d] += _colsum(dgb)
        doa_ref[...] = _dot_nt(dya, woa_ref[...]).astype(BF16)
        dob = _dot_nn(dyb, wob_ref[...])
        dd = _seg_sum(dob * ob_ref[...].astype(F32))
        for dil, dob_ref, dd_ref in zip(DILATIONS, (dob0_ref, dob1_ref, dob2_ref), (dd0_ref, dd1_ref, dd2_ref)):
            _to_residues(dob, dob_ref, scr_ref, dil, BF16)
            _to_residues(dd, dd_ref, scr_ref, dil, F32)

    sd = jax.ShapeDtypeStruct
    res = list(pl.pallas_call(
        body, name="mix_out_bwd", grid=(s // tb,),
        in_specs=[_rows(tb, d)] * 5 + [_rows(tb, GB_W), _resident(w_oa.shape), _resident(w_ob_t.shape),
                                       _resident(w_o.shape), _ANY],
        out_specs=[_rows(tb, QA_W)] + _dil_specs(tb) * 2 + [_rows(tb, d), _rows(tb, d), _rows(tb, d),
                                                           _rows(tb, d), _acc_spec((1, 2 * d))],
        out_shape=[sd((s, QA_W), BF16)] + _dil_shapes(s, BF16) + _dil_shapes(s, F32) + [
            sd((s, d), BF16), sd((s, d), BF16), sd((s, d), BF16), sd((s, d), BF16), sd((1, 2 * d), F32)],
        scratch_shapes=[pltpu.VMEM((2, tb, LANES), F32)],
        compiler_params=_cparams(("arbitrary",)))(dx2, ya, yb, ga, gb, ob, w_oa, w_ob_t, w_o, after))
    return res[:1] + [res[1:4], res[4:7]] + res[7:]


def _in_proj_bwd(dx2, x, dqrot, dkrot, dva, qraw, kraw, tabs, dqb, dkb, dvb, dga, dgb, w_in_t, g_mix, q_g, k_g, tb):
    s, d = x.shape
    din = w_in_t.shape[0]
    q_scale = HEAD_DIM_A ** -0.5
    b_scale = HEAD_DIM_B ** -0.5
    tc = 256

    def body(dx2_ref, x_ref, dq_ref, dk_ref, dv_ref, qraw_ref, kraw_ref, c_ref, s1_ref, s2_ref, *rest):
        dqb_refs, dkb_refs, dvb_refs = rest[0:3], rest[3:6], rest[6:9]
        (dga_ref, dgb_ref, w_ref, gmix_ref, qg_ref, kg_ref,
         dx_ref, dz_ref, dgmix_ref, dqg_ref, dkg_ref, dh_ref, scr_ref) = rest[9:]

        @pl.when(pl.program_id(0) == 0)
        def _():
            dgmix_ref[...] = jnp.zeros_like(dgmix_ref)
            dqg_ref[...] = jnp.zeros_like(dqg_ref)
            dkg_ref[...] = jnp.zeros_like(dkg_ref)

        cos, s1, s2 = c_ref[...][None], s1_ref[...][None], s2_ref[...][None]

        def heads_bwd(drot, z, g_ref, acc_ref):
            dn = drot * cos + pltpu.roll(drot * s1, 96, 2) + pltpu.roll(drot * s2, 32, 2)
            rr = _rstd(z)
            nn = z * rr
            acc_ref[...] += jnp.sum(jnp.sum(dn * nn, axis=0), axis=0, keepdims=True)
            return _rms_bwd(dn, nn, rr, g_ref[...][None]).astype(BF16)

        dh_ref[...] = jnp.zeros_like(dh_ref)

        def emit(off, piece):
            dz_ref[:, off:off + tc] = piece
            dh_ref[...] += _dot_nn(piece, w_ref[off:off + tc, :])

        for j in range(d // tc):
            emit(OFF_GA + tc * j, dga_ref[:, tc * j:tc * j + tc])
            emit(OFF_GA + d + tc * j, dgb_ref[:, tc * j:tc * j + tc])
        emit(OFF_VA, dv_ref[...].T.astype(BF16))
        for g, dil in enumerate(DILATIONS):
            emit(OFF_QB + GB_W * g, (_from_residues(dqb_refs[g], scr_ref, dil) * b_scale).astype(BF16))
            emit(OFF_KB + GB_W * g, (_from_residues(dkb_refs[g], scr_ref, dil) * LN_2).astype(BF16))
            emit(OFF_VB + GB_W * g, _from_residues(dvb_refs[g], scr_ref, dil).astype(BF16))
        stack = lambda ref, n: jnp.stack([ref[:, 128 * h:128 * h + 128] for h in range(n)], axis=0)
        dzq = heads_bwd(stack(dq_ref, N_Q_HEADS_A) * q_scale, stack(qraw_ref, N_Q_HEADS_A), qg_ref, dqg_ref)
        dkt = jnp.stack([dk_ref[128 * h:128 * h + 128, :].T for h in range(N_KV_HEADS_A)], axis=0) * LN_2
        dzk = heads_bwd(dkt, stack(kraw_ref, N_KV_HEADS_A), kg_ref, dkg_ref)
        for j in range(N_Q_HEADS_A // 2):
            emit(OFF_QA + tc * j, jnp.concatenate([dzq[2 * j], dzq[2 * j + 1]], axis=1))
        emit(OFF_KA, jnp.concatenate([dzk[0], dzk[1]], axis=1))
        xv = x_ref[...]
        r1 = _rstd(xv)
        n1 = xv * r1
        dh = dh_ref[...]
        dgmix_ref[...] += _colsum(dh * n1)
        dx_ref[...] = dx2_ref[...] + _rms_bwd(dh, n1, r1, gmix_ref[...])

    sd = jax.ShapeDtypeStruct
    return pl.pallas_call(
        body, name="in_proj_bwd", grid=(s // tb,),
        in_specs=[_rows(tb, d), _rows(tb, d), _rows(tb, QA_W), pl.BlockSpec((KA_W, tb), lambda i: (0, i)),
                  pl.BlockSpec((KA_W, tb), lambda i: (0, i)), _rows(tb, QA_W),
                  _rows(tb, KA_W), _rows(tb, LANES), _rows(tb, LANES), _rows(tb, LANES),
                  ] + _dil_specs(tb) * 3 + [_rows(tb, d), _rows(tb, d),
                  _resident(w_in_t.shape), _resident(g_mix.shape), _resident(q_g.shape), _resident(k_g.shape)],
        out_specs=[_rows(tb, d), _rows(tb, din), _acc_spec((1, d)), _acc_spec((1, HEAD_DIM_A)),
                   _acc_spec((1, HEAD_DIM_A))],
        out_shape=[sd((s, d), F32), sd((s, din), BF16), sd((1, d), F32), sd((1, HEAD_DIM_A), F32),
                   sd((1, HEAD_DIM_A), F32)],
        scratch_shapes=[pltpu.VMEM((tb, d), F32), pltpu.VMEM((2, tb, LANES), F32)],
        compiler_params=_cparams(("arbitrary",)))(
        dx2, x, dqrot, dkrot, dva, qraw, kraw, *tabs, *dqb, *dkb, *dvb, dga, dgb, w_in_t, g_mix, q_g, k_g)


def _identity(v):
    return v


def _to_bf16(v):
    return v.astype(BF16)


def _square_bf16(v):
    vf = v.astype(F32)
    return (vf * vf).astype(BF16)


def _weight_grad(name, a, b, ti, tj, tk, a_fn=_identity, b_fn=_identity, col0=0, n=None, after=None):
    t, m = a.shape
    n = b.shape[1] if n is None else n
    n_k = t // tk
    after = a if after is None else after

    def body(a_ref, b_ref, after_ref, o_ref, acc_ref):
        k = pl.program_id(2)

        @pl.when(k == 0)
        def _():
            acc_ref[...] = jnp.zeros_like(acc_ref)

        acc_ref[...] += _dot_tn(a_fn(a_ref[...]), b_fn(b_ref[...]))

        @pl.when(k == n_k - 1)
        def _():
            o_ref[...] = acc_ref[...].astype(BF16)

    return pl.pallas_call(
        body, name=name, grid=(m // ti, n // tj, n_k),
        in_specs=[pl.BlockSpec((tk, ti), lambda i, j, k: (k, i)),
                  pl.BlockSpec((tk, tj), lambda i, j, k: (k, j + col0 // tj)), _ANY],
        out_specs=pl.BlockSpec((ti, tj), lambda i, j, k: (i, j)),
        out_shape=jax.ShapeDtypeStruct((m, n), BF16),
        scratch_shapes=[pltpu.VMEM((ti, tj), F32)],
        compiler_params=_cparams(("arbitrary", "arbitrary", "arbitrary")))(a, b, after)


def _sum_slots(name, recv, own):
    m, n, k = recv.shape
    tc = min(k, 256)

    def body(own_ref, r_ref, o_ref):
        acc = own_ref[...].astype(F32)
        for i in range(m):
            acc = acc + r_ref[i].astype(F32)
        o_ref[...] = acc

    return pl.pallas_call(
        body, name=name, grid=(k // tc,),
        in_specs=[pl.BlockSpec((n, tc), lambda j: (0, j)), pl.BlockSpec((m, n, tc), lambda j: (0, 0, j))],
        out_specs=pl.BlockSpec((n, tc), lambda j: (0, j)),
        out_shape=jax.ShapeDtypeStruct((n, k), F32),
        compiler_params=_cparams(("arbitrary",)))(own, recv)


def _adamw_math(w, g, m, v):
    m = ADAM_B1 * m + (1.0 - ADAM_B1) * g
    v = ADAM_B2 * v + (1.0 - ADAM_B2) * (g * g)
    m_hat = m / (1.0 - ADAM_B1 ** ADAM_STEP)
    v_hat = v / (1.0 - ADAM_B2 ** ADAM_STEP)
    delta = -ADAM_LR * (m_hat / (jnp.sqrt(v_hat) + ADAM_EPS) + ADAM_WD * w)
    return delta, m, v


def _adamw(name, w, g, m, v):
    r, c = w.shape
    tr = min(r, 256)

    def body(w_ref, g_ref, m_ref, v_ref, d_ref, mo_ref, vo_ref):
        d_ref[...], mo_ref[...], vo_ref[...] = _adamw_math(w_ref[...], g_ref[...], m_ref[...], v_ref[...])

    spec = pl.BlockSpec((tr, c), lambda i: (i, 0))
    return pl.pallas_call(
        body, name=name, grid=(r // tr,), in_specs=[spec] * 4, out_specs=[spec] * 3,
        out_shape=[jax.ShapeDtypeStruct((r, c), F32)] * 3,
        compiler_params=_cparams(("arbitrary",)))(w, g, m, v)


def _small_update(parts, w, m, v):
    def body(p_ref, w_ref, m_ref, v_ref, g_ref, d_ref, mo_ref, vo_ref):
        g = p_ref[0]
        for i in range(1, N_DEV):
            g = g + p_ref[i]
        g_ref[...] = g
        d_ref[...], mo_ref[...], vo_ref[...] = _adamw_math(w_ref[...], g, m_ref[...], v_ref[...])

    return pl.pallas_call(body, name="small_update", out_shape=[jax.ShapeDtypeStruct(w.shape, F32)] * 4)(
        parts, w, m, v)


def _pack_rows(vectors, n_rows):
    flat = jnp.concatenate([v.reshape(-1).astype(F32) for v in vectors])
    flat = jnp.pad(flat, (0, n_rows * LANES - flat.shape[0]))
    return flat.reshape(n_rows, LANES)


def _pick_tile(n, prefs):
    for t in prefs:
        if n % t == 0:
            return t
    return n


def kernel(x, p, norm_mix_g, w_in, b_gate, q_norm_g, k_norm_g, rel_bias, w_out_a, w_out_b, w_out, norm_mlp_g, w_ff1, w_ff2, norm_ple_g, w_ple_gate, w_ple, final_norm_g, loss_target, m_norm_mix_g, m_w_in, m_b_gate, m_q_norm_g, m_k_norm_g, m_rel_bias, m_w_out_a, m_w_out_b, m_w_out, m_norm_mlp_g, m_w_ff1, m_w_ff2, m_norm_ple_g, m_w_ple_gate, m_w_ple, m_final_norm_g, v_norm_mix_g, v_w_in, v_b_gate, v_q_norm_g, v_k_norm_g, v_rel_bias, v_w_out_a, v_w_out_b, v_w_out, v_norm_mlp_g, v_w_ff1, v_w_ff2, v_norm_ple_g, v_w_ple_gate, v_w_ple, v_final_norm_g):
    s, d = x.shape[1], x.shape[2]
    xs, ps, ts = x[0], p[0, 0], loss_target[0]
    tb = _pick_tile(s, (512, 256))
    tq = _pick_tile(s, (256,))
    tk = _pick_tile(s, (1024, 512))
    cb = _pick_tile(s, (512,))
    fin_g = final_norm_g.reshape(1, d)

    col_sharded = {"w_in": w_in[0], "w_out_b": w_out_b[0], "w_ff1": w_ff1[0], "w_ple": w_ple[0]}
    row_sharded = {"w_out_a": w_out_a[0], "w_out": w_out[0], "w_ff2": w_ff2[0], "w_ple_gate": w_ple_gate[0]}
    order = ["w_in", "w_out_a", "w_out_b", "w_out", "w_ff1", "w_ff2", "w_ple_gate", "w_ple"]
    shards = [(col_sharded[n].T if n in col_sharded else row_sharded[n]).astype(BF16) for n in order]
    my_idx = 4 * lax.axis_index("x") + 2 * lax.axis_index("y") + lax.axis_index("c")
    (w_in_t,) = _all_gather(shards[:1])
    zones = [lax.dynamic_update_slice(lax.empty((N_DEV * sh.shape[0], sh.shape[1]), BF16), sh,
                                      (my_idx * sh.shape[0], 0)) for sh in shards[1:]]
    ag = _copies_start("weights_gather_start", shards[1:], zones, w_in_t, True)

    tabs = _rope_tables(s)
    (h1, qraw, kraw, qrot, krot, va, qb, kb, vb, ga, gb) = _in_proj(
        xs, tabs, w_in_t, norm_mix_g, b_gate, q_norm_g, k_norm_g, tb, ag[4])
    oa, lse_a = _attn_a_fwd(qrot, krot, va, tq, tk)
    _, (w_oa, w_ob_t, w_o, w_ff1_t, w_ff2_f, w_pg, w_p_t) = _copies_wait(
        "weights_gather_wait", ag[0], ag[1], ag[2], ag[3], lse_a, True)
    flat = lambda arrs: [a.reshape(s, GB_W) for a in arrs]
    split = lambda arrs: [a.reshape(dil, s // dil, GB_W) for a, dil in zip(arrs, DILATIONS)]
    qb_r, kb_r, vb_r = flat(qb), flat(kb), flat(vb)
    bmaps = [[jnp.asarray(m) for m in _bucket_maps(dil)] for dil in DILATIONS]
    bias_tabs = [rel_bias[:, N_HEADS_PER_DIL * g:N_HEADS_PER_DIL * (g + 1)] for g in range(3)]
    band_out = [_band_fwd(dil, qb_r[g], kb_r[g], vb_r[g], bmaps[g][0], bias_tabs[g], cb)
                for g, dil in enumerate(DILATIONS)]
    og, lg = split([o for o, _ in band_out]), split([l for _, l in band_out])
    x2, ob, lse_b, ya, yb, u = _mix_out(xs, oa, og, lg, ga, gb, w_oa, w_ob_t, w_o, tb)
    tc = _pick_tile(w_ff1_t.shape[0], (512,))
    x3, r_act, h2 = _mlp_fwd(x2, w_ff1_t, w_ff2_f, norm_mlp_g, tb, tc)

    dx3, h3, dpre, dpe, pb, loss_part, dg_fin, dg_ple = _ple_loss(
        x3, ps, ts, w_pg, w_p_t, norm_ple_g, fin_g, tb)
    dx2, df, dg_mlp = _mlp_bwd(dx3, x2, r_act, w_ff1_t, w_ff2_f, norm_mlp_g, tb, tc)

    tkk = _pick_tile(s, (1024, 512))
    dff = w_ff1_t.shape[0]
    t1k = lambda n: _pick_tile(n, (1024, 512, 256))
    slots = lambda parts: [lax.empty((7, a.shape[0] // N_DEV, a.shape[1]), BF16) for a in parts]
    part1 = [_weight_grad("grad_w_ff1", df, h2, t1k(dff), t1k(d), tkk),
             _weight_grad("grad_w_ff2", r_act, dx3, t1k(dff), t1k(d), tkk, a_fn=_square_bf16, b_fn=_to_bf16),
             _weight_grad("grad_w_ple_gate", h3, dpre, t1k(d), t1k(d), tkk),
             _weight_grad("grad_w_ple", dpe, pb, t1k(d), ps.shape[1], tkk)]
    rs1 = _copies_start("grads1_start", part1, slots(part1), dx2, False)
    doa, dob, dd, dga, dgb, dya, dyb, dbg = _mix_out_bwd(dx2, ya, yb, ga, gb, ob, w_oa, w_ob_t, w_o, tb, rs1[4])
    part2 = [_weight_grad("grad_w_out_a", oa, dya, t1k(QA_W), t1k(d), tkk),
             _weight_grad("grad_w_out_b", dyb, ob, t1k(d), GB_W, tkk),
             _weight_grad("grad_w_out", u, dx2, t1k(d), t1k(d), tkk, b_fn=_to_bf16)]
    rs2 = _copies_start("grads2_start", part2, slots(part2), doa, False)
    dqrot, dkrot, dva = _attn_a_bwd(qrot, krot, va, oa, doa, lse_a, tq, tk, rs2[4])
    dob_r, lse_r, dd_r = flat(dob), flat(lse_b), flat(dd)
    bwd_q = [_band_bwd_q(dil, qb_r[g], kb_r[g], vb_r[g], dob_r[g], lse_r[g], dd_r[g], bmaps[g][0], bias_tabs[g], cb)
             for g, dil in enumerate(DILATIONS)]
    bwd_kv = [_band_bwd_kv(dil, qb_r[g], kb_r[g], vb_r[g], dob_r[g], lse_r[g], dd_r[g], bmaps[g][1], bias_tabs[g], cb)
              for g, dil in enumerate(DILATIONS)]
    dqb, dkb, dvb = split([r[0] for r in bwd_q]), split([r[0] for r in bwd_kv]), split([r[1] for r in bwd_kv])
    grad_x, dz, dg_mix, dg_q, dg_k = _in_proj_bwd(
        dx2, xs, dqrot, dkrot, dva, qraw, kraw, tabs, dqb, dkb, dvb, dga, dgb, w_in_t, norm_mix_g,
        q_norm_g, k_norm_g, _pick_tile(s, (256,)))
    d_rel = jnp.concatenate([r[1][:, :N_HEADS_PER_DIL] for r in bwd_q], axis=1)

    din = w_in_t.shape[0]
    ti_in = _pick_tile(din, (din // 2,)) if (din // 2) % LANES == 0 else din
    hd_ = d // 2
    part3 = [_weight_grad("grad_w_in_lo", dz, h1, ti_in, t1k(hd_), tkk, n=hd_)]
    rs3 = _copies_start("grads3_start", part3, slots(part3), grad_x, False)
    part4 = [_weight_grad("grad_w_in_hi", dz, h1, ti_in, t1k(hd_), tkk, col0=hd_, n=hd_, after=rs3[4])]
    rs4 = _copies_start("grads4_start", part4, slots(part4), rs3[4], False)

    def own_rows(a):
        n = a.shape[0] // N_DEV
        return lax.dynamic_slice(a, (my_idx * n, 0), (n, a.shape[1]))

    sums = {}
    src1, got1 = _copies_wait("grads1_wait", rs1[0], rs1[1], rs1[2], rs1[3], rs4[4], False)
    src2, got2 = _copies_wait("grads2_wait", rs2[0], rs2[1], rs2[2], rs2[3], rs4[4], False)
    for n, a, r in zip(["w_ff1", "w_ff2", "w_ple_gate", "w_ple", "w_out_a", "w_out_b", "w_out"],
                       src1 + src2, got1 + got2):
        sums[n] = _sum_slots("sum_" + n, r, own_rows(a))
    given_w = dict(w_in=w_in, w_out_a=w_out_a, w_out_b=w_out_b, w_out=w_out, w_ff1=w_ff1, w_ff2=w_ff2,
                   w_ple_gate=w_ple_gate, w_ple=w_ple)
    given_m = dict(w_in=m_w_in, w_out_a=m_w_out_a, w_out_b=m_w_out_b, w_out=m_w_out, w_ff1=m_w_ff1, w_ff2=m_w_ff2,
                   w_ple_gate=m_w_ple_gate, w_ple=m_w_ple)
    given_v = dict(w_in=v_w_in, w_out_a=v_w_out_a, w_out_b=v_w_out_b, w_out=v_w_out, w_ff1=v_w_ff1, w_ff2=v_w_ff2,
                   w_ple_gate=v_w_ple_gate, w_ple=v_w_ple)
    big = {}

    def update(n):
        g = sums[n].T if n in col_sharded else sums[n]
        delta, new_m, new_v = _adamw("adamw_" + n, given_w[n][0], g, given_m[n][0], given_v[n][0])
        big[n] = tuple(a[None] for a in (g, delta, new_m, new_v))

    for n in order[1:]:
        update(n)
    src3, got3 = _copies_wait("grads3_wait", rs3[0], rs3[1], rs3[2], rs3[3], big["w_ple"][1], False)
    src4, got4 = _copies_wait("grads4_wait", rs4[0], rs4[1], rs4[2], rs4[3], big["w_ple"][1], False)
    sums["w_in"] = jnp.concatenate([_sum_slots("sum_w_in_lo", got3[0], own_rows(src3[0])),
                                    _sum_slots("sum_w_in_hi", got4[0], own_rows(src4[0]))], axis=1)
    update("w_in")

    small_names = ["norm_mix_g", "b_gate", "q_norm_g", "k_norm_g", "rel_bias", "norm_mlp_g", "norm_ple_g",
                   "final_norm_g"]
    small_w = [norm_mix_g, b_gate, q_norm_g, k_norm_g, rel_bias, norm_mlp_g, norm_ple_g, final_norm_g]
    small_m = [m_norm_mix_g, m_b_gate, m_q_norm_g, m_k_norm_g, m_rel_bias, m_norm_mlp_g, m_norm_ple_g,
               m_final_norm_g]
    small_v = [v_norm_mix_g, v_b_gate, v_q_norm_g, v_k_norm_g, v_rel_bias, v_norm_mlp_g, v_norm_ple_g,
               v_final_norm_g]
    small_g = [dg_mix, dbg, dg_q, dg_k, d_rel, dg_mlp, dg_ple, dg_fin]
    sizes = [int(np.prod(w.shape)) for w in small_w]
    n_rows = -(-(sum(-(-sz // LANES) for sz in sizes) + 1) // 8) * 8
    pad = lambda v: jnp.pad(v.reshape(-1).astype(F32), (0, -v.size % LANES))
    pack = lambda vs, last: _pack_rows([pad(v) for v in vs] + [last], n_rows)
    zero_row = jnp.zeros((LANES,), F32)
    parts = _small_all_gather(pack(small_g, loss_part.reshape(-1) * (jnp.arange(LANES) == 0)))
    g_all, d_all, m_all, v_all = _small_update(parts, pack(small_w, zero_row), pack(small_m, zero_row),
                                               pack(small_v, zero_row))
    small = {}
    row = 0
    for n, w, sz in zip(small_names, small_w, sizes):
        nr = -(-sz // LANES)
        small[n] = tuple(a[row:row + nr].reshape(-1)[:sz].reshape(w.shape) for a in (g_all, d_all, m_all, v_all))
        row += nr
    loss = g_all[row, 0]

    names = ["norm_mix_g", "w_in", "b_gate", "q_norm_g", "k_norm_g", "rel_bias", "w_out_a", "w_out_b", "w_out",
             "norm_mlp_g", "w_ff1", "w_ff2", "norm_ple_g", "w_ple_gate", "w_ple", "final_norm_g"]
    res = {n: (big[n] if n in big else small[n]) for n in names}
    return (loss, grad_x[None], *[res[n][0] for n in names], *[res[n][1] for n in names],
            *[res[n][2] for n in names], *[res[n][3] for n in names])
```

```python
import functools
import math

import numpy as np
import jax
import jax.numpy as jnp
from jax import lax
from jax.experimental import pallas as pl
from jax.experimental.pallas import tpu as pltpu

F32 = jnp.float32
BF16 = jnp.bfloat16
MESH = pl.DeviceIdType.MESH

NORM_EPS = 1e-6
NEG_INF = -1e30
LOG2_E = math.log2(math.e)
LN_2 = math.log(2.0)
GRID_W = 64
ROPE_THETA = 10000.0
HEAD_DIM_A = 128
N_Q_HEADS_A = 8
N_KV_HEADS_A = 2
Q_PER_KV = N_Q_HEADS_A // N_KV_HEADS_A
HEAD_DIM_B = 64
N_HEADS_PER_DIL = 4
DILATIONS = (1, 4, 16)
BAND = 64
N_REL_BUCKETS = 32
REL_MAX_DIST = 1024
QA_W = N_Q_HEADS_A * HEAD_DIM_A
KA_W = N_KV_HEADS_A * HEAD_DIM_A
GB_W = N_HEADS_PER_DIL * HEAD_DIM_B
QB_W = GB_W * len(DILATIONS)
OFF_QA, OFF_KA, OFF_VA = 0, QA_W, QA_W + KA_W
OFF_QB = QA_W + 2 * KA_W
OFF_KB = OFF_QB + QB_W
OFF_VB = OFF_KB + QB_W
OFF_GA = OFF_VB + QB_W
N_DEV = 8
LANES = 128
VMEM_LIMIT = 56 * 2 ** 20

ADAM_LR, ADAM_B1, ADAM_B2, ADAM_EPS, ADAM_WD, ADAM_STEP = 0.001, 0.9, 0.999, 1e-08, 0.01, 10


def _cparams(sem):
    return pltpu.CompilerParams(dimension_semantics=sem, vmem_limit_bytes=VMEM_LIMIT)


def _resident(shape):
    nd = len(shape)
    return pl.BlockSpec(shape, lambda *_: (0,) * nd, pipeline_mode=pl.Buffered(1))


def _acc_spec(shape):
    nd = len(shape)
    return pl.BlockSpec(shape, lambda *_: (0,) * nd)


def _rows(tb, c):
    return pl.BlockSpec((tb, c), lambda i: (i, 0))


def _dil_shapes(s, dtype):
    return [jax.ShapeDtypeStruct((dil, s // dil, GB_W), dtype) for dil in DILATIONS]


def _dil_specs(tb):
    return [pl.BlockSpec((dil, tb // dil, GB_W), lambda i: (0, i, 0)) for dil in DILATIONS]


def _to_residues(val, out_ref, scr_ref, dil, dtype):
    if dil == 1:
        out_ref[0] = val.astype(dtype)
        return
    n = val.shape[0] // dil
    scr_ref[0] = val[:, :LANES]
    scr_ref[1] = val[:, LANES:]
    for r in range(dil):
        out_ref[r] = jnp.concatenate([scr_ref[0, pl.ds(r, n, stride=dil), :],
                                      scr_ref[1, pl.ds(r, n, stride=dil), :]], axis=1).astype(dtype)


def _from_residues(in_ref, scr_ref, dil):
    if dil == 1:
        return in_ref[0]
    n = in_ref.shape[1]
    for r in range(dil):
        v = in_ref[r]
        scr_ref[0, pl.ds(r, n, stride=dil), :] = v[:, :LANES]
        scr_ref[1, pl.ds(r, n, stride=dil), :] = v[:, LANES:]
    return jnp.concatenate([scr_ref[0], scr_ref[1]], axis=1)


def _dot_nt(a, b):
    return lax.dot_general(a, b, (((1,), (1,)), ((), ())), preferred_element_type=F32)


def _dot_nn(a, b):
    return lax.dot_general(a, b, (((1,), (0,)), ((), ())), preferred_element_type=F32)


def _dot_tn(a, b):
    return lax.dot_general(a, b, (((0,), (0,)), ((), ())), preferred_element_type=F32)


def _rstd(x):
    return lax.rsqrt(jnp.mean(x * x, axis=-1, keepdims=True) + NORM_EPS)


def _rms_bwd(dy, n, r, g):
    dn = dy * g
    return r * (dn - n * jnp.mean(dn * n, axis=-1, keepdims=True))


def _colsum(v):
    return jnp.sum(v, axis=0, keepdims=True)


def _sigmoid(v):
    return 1.0 / (1.0 + jnp.exp(-v))


def _rope_tables(s):
    half = HEAD_DIM_A // 2
    inv = jnp.power(ROPE_THETA, -jnp.arange(0, half, 2, dtype=F32) / half)
    n_rows = s // GRID_W
    ang_r = jnp.arange(n_rows, dtype=F32)[:, None] * inv[None, :]
    ang_c = jnp.arange(GRID_W, dtype=F32)[:, None] * inv[None, :]
    cr, sr = jnp.repeat(jnp.cos(ang_r), GRID_W, axis=0), jnp.repeat(jnp.sin(ang_r), GRID_W, axis=0)
    cc, sc = jnp.tile(jnp.cos(ang_c), (n_rows, 1)), jnp.tile(jnp.sin(ang_c), (n_rows, 1))
    z = jnp.zeros_like(sr)
    cos = jnp.concatenate([cr, cr, cc, cc], axis=1)
    s1 = jnp.concatenate([z, sr, z, sc], axis=1)
    s2 = jnp.concatenate([-sr, z, -sc, z], axis=1)
    return cos, s1, s2


def _my_place():
    return lax.axis_index("x"), lax.axis_index("y"), lax.axis_index("c")


def _all_gather(shards):
    nw = len(shards)

    def body(*refs):
        ins, outs = refs[:nw], refs[nw:2 * nw]
        send_sems, recv_sems, local_sems = refs[2 * nw:]
        x, y, c = _my_place()
        me, sibling = (x, y, c), (x, y, 1 - c)
        chips = [(1 - x, y), (x, 1 - y), (1 - x, 1 - y)]

        def rows(w, px, py, pc):
            n = ins[w].shape[0]
            return outs[w].at[pl.ds(pl.multiple_of((4 * px + 2 * py + pc) * n, 16), n), :]

        def copy(w, k, block, to, src=None):
            return pltpu.make_async_remote_copy(
                src_ref=rows(w, *block) if src is None else src, dst_ref=rows(w, *block),
                send_sem=send_sems.at[w, k], recv_sem=recv_sems.at[w, k], device_id=to, device_id_type=MESH)

        mine = [pltpu.make_async_copy(ins[w], rows(w, *me), local_sems.at[w]) for w in range(nw)]
        for cp in mine:
            cp.start()
        first = []
        for w in range(nw):
            first.append(copy(w, 0, me, sibling, src=ins[w]))
            first += [copy(w, 1 + j, me, (*chip, c), src=ins[w]) for j, chip in enumerate(chips)]
        for cp in first:
            cp.start()
        passed = []
        for j, chip in enumerate(chips):
            for w in range(nw):
                copy(w, 1 + j, (*chip, c), me).wait_recv()
                fwd = copy(w, 4 + j, (*chip, c), sibling)
                fwd.start()
                passed.append(fwd)
        for w in range(nw):
            copy(w, 0, sibling, me).wait_recv()
        for j, chip in enumerate(chips):
            for w in range(nw):
                copy(w, 4 + j, (*chip, 1 - c), me).wait_recv()
        for cp in first + passed:
            cp.wait_send()
        for cp in mine:
            cp.wait()

    any_spec = pl.BlockSpec(memory_space=pl.ANY)
    return pl.pallas_call(
        body, name="weights_all_gather",
        out_shape=[jax.ShapeDtypeStruct((N_DEV * s.shape[0], s.shape[1]), s.dtype) for s in shards],
        in_specs=[any_spec] * nw, out_specs=[any_spec] * nw,
        scratch_shapes=[pltpu.SemaphoreType.DMA((nw, 7)), pltpu.SemaphoreType.DMA((nw, 7)),
                        pltpu.SemaphoreType.DMA((nw,))],
    )(*shards)


_FLIPS = [(fx, fy, fc) for fx in (0, 1) for fy in (0, 1) for fc in (0, 1)][1:]


def _small_all_gather(v):
    def body(v_ref, out_ref, send_sems, recv_sems):
        x, y, c = _my_place()
        my_idx = 4 * x + 2 * y + c
        out_ref[my_idx] = v_ref[...]
        sends = []
        for k, (fx, fy, fc) in enumerate(_FLIPS):
            to = (1 - x if fx else x, 1 - y if fy else y, 1 - c if fc else c)
            sends.append(pltpu.make_async_remote_copy(
                src_ref=v_ref, dst_ref=out_ref.at[my_idx], send_sem=send_sems.at[k], recv_sem=recv_sems.at[k],
                device_id=to, device_id_type=MESH))
        for cp in sends:
            cp.start()
        for k, (fx, fy, fc) in enumerate(_FLIPS):
            frm_idx = 4 * (1 - x if fx else x) + 2 * (1 - y if fy else y) + (1 - c if fc else c)
            pltpu.make_async_remote_copy(
                src_ref=v_ref, dst_ref=out_ref.at[frm_idx], send_sem=send_sems.at[k], recv_sem=recv_sems.at[k],
                device_id=(x, y, c), device_id_type=MESH).wait_recv()
        for cp in sends:
            cp.wait_send()

    vm = pl.BlockSpec(memory_space=pltpu.VMEM)
    return pl.pallas_call(
        body, name="small_all_gather", out_shape=jax.ShapeDtypeStruct((N_DEV,) + v.shape, v.dtype),
        in_specs=[vm], out_specs=vm,
        scratch_shapes=[pltpu.SemaphoreType.DMA((7,)), pltpu.SemaphoreType.DMA((7,))],
    )(v)


_HBM = pl.BlockSpec(memory_space=pltpu.HBM)
_SEM = pl.BlockSpec(memory_space=pltpu.SEMAPHORE)
_ANY = pl.BlockSpec(memory_space=pl.ANY)
_SPLIT_COPY = dict(has_side_effects=pltpu.SideEffectType.DATAFLOW_SIDE_EFFECTING)


def _peer(x, y, c, k):
    fx, fy, fc = _FLIPS[k]
    return (1 - x if fx else x, 1 - y if fy else y, 1 - c if fc else c)


def _in_hbm(a):
    return pltpu.with_memory_space_constraint(a, pltpu.HBM)


def _split_copies(srcs, lands, send_sems, recv_sems, gather, arriving):
    x, y, c = _my_place()
    my_idx = 4 * x + 2 * y + c
    out = []
    for k in range(7):
        to = _peer(x, y, c, k)
        to_idx = 4 * to[0] + 2 * to[1] + to[2]
        for w in range(len(srcs)):
            if gather:
                n = srcs[w].shape[0]
                src = srcs[w]
                dst = lands[w].at[pl.ds(pl.multiple_of((to_idx if arriving else my_idx) * n, 16), n), :]
            else:
                n = lands[w].shape[1]
                src = srcs[w].at[pl.ds(pl.multiple_of(to_idx * n, 16), n), :]
                dst = lands[w].at[k]
            out.append(pltpu.make_async_remote_copy(
                src_ref=src, dst_ref=dst, send_sem=send_sems.at[7 * w + k], recv_sem=recv_sems.at[7 * w + k],
                device_id=to, device_id_type=MESH))
    return out


def _copies_start(name, srcs, lands, after, gather):
    nw = len(srcs)

    def body(*refs):
        send_sems, recv_sems = refs[2 * nw + 1], refs[2 * nw + 2]
        for cp in _split_copies(refs[:nw], refs[nw:2 * nw], send_sems, recv_sems, gather, False):
            cp.start()
        refs[-1][...] = jnp.zeros_like(refs[-1])

    sems = pltpu.SemaphoreType.DMA((7 * nw,))
    thru = [pltpu.HBM(a.shape, a.dtype) for a in list(srcs) + list(lands)]
    res = pl.pallas_call(
        body, name=name, out_shape=(sems, sems, *thru, jax.ShapeDtypeStruct((8, LANES), F32)),
        in_specs=[_HBM] * (2 * nw) + [_ANY], out_specs=(_SEM, _SEM, *[_HBM] * (2 * nw), pl.BlockSpec(memory_space=pltpu.VMEM)),
        input_output_aliases={i: 2 + i for i in range(2 * nw)},
        compiler_params=pltpu.CompilerParams(**_SPLIT_COPY),
    )(*[_in_hbm(a) for a in srcs], *[_in_hbm(a) for a in lands], after)
    return res[0], res[1], list(res[2:2 + nw]), list(res[2 + nw:2 + 2 * nw]), res[-1]


def _copies_wait(name, send_sems, recv_sems, srcs, lands, after, gather):
    nw = len(srcs)

    def body(*refs):
        for cp in _split_copies(refs[:nw], refs[nw:2 * nw], refs[2 * nw], refs[2 * nw + 1], gather, False):
            cp.wait_send()
        for cp in _split_copies(refs[:nw], refs[nw:2 * nw], refs[2 * nw], refs[2 * nw + 1], gather, True):
            cp.wait_recv()

    thru = [pltpu.HBM(a.shape, a.dtype) for a in list(srcs) + list(lands)]
    res = pl.pallas_call(
        body, name=name, out_shape=tuple(thru),
        in_specs=[_HBM] * (2 * nw) + [_SEM, _SEM, _ANY], out_specs=tuple([_HBM] * (2 * nw)),
        input_output_aliases={i: i for i in range(2 * nw)},
        compiler_params=pltpu.CompilerParams(**_SPLIT_COPY),
    )(*srcs, *lands, send_sems, recv_sems, after)
    return list(res[:nw]), list(res[nw:])


def _in_proj(x, tabs, w_in_t, g_mix, b_gate, q_g, k_g, tb, after):
    s, d = x.shape
    n_gate_chunks = d // 256
    q_scale = HEAD_DIM_A ** -0.5 * LOG2_E
    b_scale = HEAD_DIM_B ** -0.5 * LOG2_E

    def body(x_ref, c_ref, s1_ref, s2_ref, w_ref, gmix_ref, bg_ref, qg_ref, kg_ref, after_ref,
             h1_ref, qraw_ref, kraw_ref, qrot_ref, krot_ref, va_ref, *rest):
        qb_refs, kb_refs, vb_refs = rest[0:3], rest[3:6], rest[6:9]
        ga_ref, gb_ref, scr_ref = rest[9:]
        xv = x_ref[...]
        hb = (xv * _rstd(xv) * gmix_ref[...]).astype(BF16)
        h1_ref[...] = hb
        cos, s1, s2 = c_ref[...][None], s1_ref[...][None], s2_ref[...][None]

        def proj(lo, width):
            return _dot_nt(hb, w_ref[lo:lo + width, :])

        def norm_rope(zs, g_ref, scale):
            z = jnp.stack([zz[:, 128 * hh:128 * hh + 128] for zz in zs for hh in range(2)], axis=0)
            n = z * _rstd(z) * g_ref[...][None]
            return ((n * cos + pltpu.roll(n, 32, 2) * s1 + pltpu.roll(n, 96, 2) * s2) * scale).astype(BF16)

        zq = [proj(OFF_QA + 256 * j, 256) for j in range(QA_W // 256)]
        for j, z in enumerate(zq):
            qraw_ref[:, 256 * j:256 * j + 256] = z
        zk = proj(OFF_KA, 256)
        kraw_ref[...] = zk
        va_ref[...] = proj(OFF_VA, 256).astype(BF16)
        q_rot, k_rot = norm_rope(zq, qg_ref, q_scale), norm_rope([zk], kg_ref, 1.0)
        for h in range(N_Q_HEADS_A):
            qrot_ref[:, 128 * h:128 * h + 128] = q_rot[h]
        for h in range(N_KV_HEADS_A):
            krot_ref[:, 128 * h:128 * h + 128] = k_rot[h]
        for g, dil in enumerate(DILATIONS):
            _to_residues(proj(OFF_QB + GB_W * g, GB_W) * b_scale, qb_refs[g], scr_ref, dil, BF16)
            _to_residues(proj(OFF_KB + GB_W * g, GB_W), kb_refs[g], scr_ref, dil, BF16)
            _to_residues(proj(OFF_VB + GB_W * g, GB_W), vb_refs[g], scr_ref, dil, BF16)
        for j in range(n_gate_chunks):
            sl = slice(256 * j, 256 * j + 256)
            ga_ref[:, sl] = _sigmoid(proj(OFF_GA + 256 * j, 256) + bg_ref[:, sl])
            gb_ref[:, sl] = _sigmoid(proj(OFF_GA + d + 256 * j, 256) + bg_ref[:, d + 256 * j:d + 256 * j + 256])

    sd = jax.ShapeDtypeStruct
    outs = [sd((s, d), BF16), sd((s, QA_W), F32), sd((s, KA_W), F32), sd((s, QA_W), BF16), sd((s, KA_W), BF16),
            sd((s, KA_W), BF16)] + _dil_shapes(s, BF16) * 3 + [sd((s, d), F32), sd((s, d), F32)]
    out_specs = [_rows(tb, d), _rows(tb, QA_W), _rows(tb, KA_W), _rows(tb, QA_W), _rows(tb, KA_W), _rows(tb, KA_W)
                 ] + _dil_specs(tb) * 3 + [_rows(tb, d), _rows(tb, d)]
    in_specs = [_rows(tb, d), _rows(tb, LANES), _rows(tb, LANES), _rows(tb, LANES), _resident(w_in_t.shape),
                _resident(g_mix.shape), _resident(b_gate.shape), _resident(q_g.shape), _resident(k_g.shape), _ANY]
    res = list(pl.pallas_call(body, name="in_proj", grid=(s // tb,), in_specs=in_specs, out_specs=out_specs,
                              out_shape=outs, scratch_shapes=[pltpu.VMEM((2, tb, LANES), F32)],
                              compiler_params=_cparams(("arbitrary",)))(
        x, *tabs, w_in_t, g_mix, b_gate, q_g, k_g, after))
    return res[:6] + [res[6:9], res[9:12], res[12:15]] + res[15:]


def _attn_a_fwd(qrot, krot, va, tq, tk):
    s = qrot.shape[0]
    n_kv = s // tk
    gw = Q_PER_KV * HEAD_DIM_A

    def body(q_ref, k_ref, v_ref, o_ref, lse_ref):
        q4 = jnp.concatenate([q_ref[:, 128 * h:128 * h + 128] for h in range(Q_PER_KV)], axis=0)

        def step(j, carry):
            m, l, acc = carry
            sl = pl.ds(pl.multiple_of(j * tk, tk), tk)
            kj, vj = k_ref[sl, :], v_ref[sl, :]
            sc = _dot_nt(kj, q4)
            m_new = jnp.maximum(m, jnp.max(sc, axis=0, keepdims=True))
            p = jnp.exp2(sc - m_new)
            alpha = jnp.exp2(m - m_new)
            l = alpha * l + jnp.sum(p, axis=0, keepdims=True)
            acc = alpha * acc + _dot_tn(vj, p.astype(BF16))
            return m_new, l, acc

        rows = Q_PER_KV * tq
        m, l, acc = lax.fori_loop(0, n_kv, step, (jnp.full((1, rows), NEG_INF, F32), jnp.zeros((1, rows), F32),
                                                  jnp.zeros((HEAD_DIM_A, rows), F32)))
        o = (acc / l).T
        lse = m + jnp.log2(l)
        for h in range(Q_PER_KV):
            o_ref[:, 128 * h:128 * h + 128] = o[h * tq:(h + 1) * tq].astype(BF16)
            lse_ref[0, h:h + 1, :] = lse[:, h * tq:(h + 1) * tq]

    return pl.pallas_call(
        body, name="attn_a_fwd", grid=(N_KV_HEADS_A, s // tq),
        in_specs=[pl.BlockSpec((tq, gw), lambda g, i: (i, g)),
                  pl.BlockSpec((s, HEAD_DIM_A), lambda g, i: (0, g)),
                  pl.BlockSpec((s, HEAD_DIM_A), lambda g, i: (0, g))],
        out_specs=[pl.BlockSpec((tq, gw), lambda g, i: (i, g)),
                   pl.BlockSpec((1, Q_PER_KV, tq), lambda g, i: (g, 0, i))],
        out_shape=[jax.ShapeDtypeStruct((s, QA_W), BF16), jax.ShapeDtypeStruct((N_KV_HEADS_A, Q_PER_KV, s), F32)],
        compiler_params=_cparams(("arbitrary", "arbitrary")))(qrot, krot, va)


def _attn_a_bwd(qrot, krot, va, oa, doa, lse, tq, tk, after):
    s = qrot.shape[0]
    n_kv = s // tk
    gw = Q_PER_KV * HEAD_DIM_A

    def body(q_ref, do_ref, o_ref, lse_ref, k_ref, v_ref, after_ref, dq_ref, dk_ref, dv_ref):
        @pl.when(pl.program_id(1) == 0)
        def _():
            dk_ref[...] = jnp.zeros_like(dk_ref)
            dv_ref[...] = jnp.zeros_like(dv_ref)

        def stack(ref):
            return jnp.concatenate([ref[:, 128 * h:128 * h + 128] for h in range(Q_PER_KV)], axis=0)

        q4, do4, o4 = stack(q_ref), stack(do_ref), stack(o_ref)
        q4t, do4t = q4.T, do4.T
        delta = jnp.sum((do4.astype(F32) * o4.astype(F32)).T, axis=0, keepdims=True)
        lse4 = jnp.concatenate([lse_ref[0, h:h + 1, :] for h in range(Q_PER_KV)], axis=1)

        def step(j, dq):
            sl = pl.ds(pl.multiple_of(j * tk, tk), tk)
            kj, vj = k_ref[sl, :], v_ref[sl, :]
            p = jnp.exp2(_dot_nt(kj, q4) - lse4)
            ds = (p * (_dot_nt(vj, do4) - delta)).astype(BF16)
            dk_ref[:, sl] += _dot_nt(q4t, ds)
            dv_ref[:, sl] += _dot_nt(do4t, p.astype(BF16))
            return dq + _dot_tn(kj, ds)

        dq = lax.fori_loop(0, n_kv, step, jnp.zeros((HEAD_DIM_A, Q_PER_KV * tq), F32)).T
        for h in range(Q_PER_KV):
            dq_ref[:, 128 * h:128 * h + 128] = dq[h * tq:(h + 1) * tq]

    qspec = pl.BlockSpec((tq, gw), lambda g, i: (i, g))
    kspec = pl.BlockSpec((s, HEAD_DIM_A), lambda g, i: (0, g))
    ktspec = pl.BlockSpec((HEAD_DIM_A, s), lambda g, i: (g, 0))
    return pl.pallas_call(
        body, name="attn_a_bwd", grid=(N_KV_HEADS_A, s // tq),
        in_specs=[qspec, qspec, qspec, pl.BlockSpec((1, Q_PER_KV, tq), lambda g, i: (g, 0, i)), kspec, kspec, _ANY],
        out_specs=[qspec, ktspec, ktspec],
        out_shape=[jax.ShapeDtypeStruct((s, QA_W), F32), jax.ShapeDtypeStruct((KA_W, s), F32),
                   jax.ShapeDtypeStruct((KA_W, s), F32)],
        compiler_params=_cparams(("arbitrary", "arbitrary")))(qrot, doa, oa, lse, krot, va, after)


BAND_QB = 128
BAND_WIN = BAND_QB + 2 * BAND


def _band_specs(s, cb):
    per = cb // BAND
    last = s // BAND - 1
    cur = pl.BlockSpec((cb, GB_W), lambda i: (i, 0))
    prev = pl.BlockSpec((BAND, GB_W), lambda i: (jnp.maximum(i * per - 1, 0), 0))
    nxt = pl.BlockSpec((BAND, GB_W), lambda i: (jnp.minimum(i * per + per, last), 0))
    return cur, prev, nxt


def _window(prev_ref, cur_ref, next_ref):
    return jnp.concatenate([prev_ref[...], cur_ref[...], next_ref[...]], axis=0)


def _band_mask(base, seg_shift, window_rows):
    shape = (BAND_WIN, BAND_QB) if window_rows else (BAND_QB, BAND_WIN)
    a = lax.broadcasted_iota(jnp.int32, shape, 0)
    b = lax.broadcasted_iota(jnp.int32, shape, 1)
    rq, rk = (base - BAND + a, base + b) if window_rows else (base + a, base - BAND + b)
    same_segment = lax.shift_right_arithmetic(rq, jnp.int32(seg_shift)) == lax.shift_right_arithmetic(rk, jnp.int32(seg_shift))
    return (jnp.abs(rk - rq) <= BAND) & same_segment


def _build_bias(bmap_ref, tab_ref, bias_ref):
    bm = bmap_ref[...]
    acc = [jnp.full(bm.shape, NEG_INF, F32) for _ in range(N_HEADS_PER_DIL)]
    for b in range(N_REL_BUCKETS):
        hit = bm == b
        for h in range(N_HEADS_PER_DIL):
            acc[h] = jnp.where(hit, tab_ref[b, h] * LOG2_E, acc[h])
    rows = bm.shape[0]
    for h in range(N_HEADS_PER_DIL):
        bias_ref[h * rows:(h + 1) * rows, :] = acc[h]


def _segment_mask(base, seg_len, seg_shift, window_rows):
    if seg_len % BAND_QB:
        return _band_mask(base, seg_shift, window_rows)
    pos = lax.rem(base, seg_len)
    shape, dim = ((BAND_WIN, 1), 0) if window_rows else ((1, BAND_WIN), 1)
    w = lax.broadcasted_iota(jnp.int32, shape, dim)
    return ((w >= BAND) | (pos != 0)) & ((w < BAND + BAND_QB) | (pos != seg_len - BAND_QB))


def _head_lane_masks():
    lane = lax.broadcasted_iota(jnp.int32, (1, LANES), 1)
    return [lane < HEAD_DIM_B, lane >= HEAD_DIM_B]


def _rows4(mask):
    return mask if mask.shape[0] == 1 else jnp.concatenate([mask] * N_HEADS_PER_DIL, axis=0)


def _head_scores(a, b):
    hm = _head_lane_masks()
    out = []
    for hp in range(2):
        ls = slice(LANES * hp, LANES * hp + LANES)
        ah = a[:, ls]
        both = jnp.concatenate([jnp.where(hm[0], ah, jnp.zeros_like(ah)), jnp.where(hm[1], ah, jnp.zeros_like(ah))],
                               axis=0)
        out.append(_dot_nt(both, b[:, ls]))
    return jnp.concatenate(out, axis=0)


def _head_combine(p, v, scale=None, transposed=False):
    hm = _head_lane_masks()
    rows = p.shape[0] // N_HEADS_PER_DIL
    halves = []
    for hp in range(2):
        vh = v[:, LANES * hp:LANES * hp + LANES]
        acc = None
        for hh in range(2):
            h = 2 * hp + hh
            ph = p[h * rows:(h + 1) * rows]
            vm = jnp.where(hm[hh], vh, jnp.zeros_like(vh))
            t = _dot_tn(ph, vm) if transposed else _dot_nn(ph, vm)
            if scale is not None:
                t = t * scale[h * rows:(h + 1) * rows]
            acc = t if acc is None else acc + t
        halves.append(acc)
    return jnp.concatenate(halves, axis=1)


def _head_spread(col):
    rows = col.shape[0] // N_HEADS_PER_DIL
    lane = lax.broadcasted_iota(jnp.int32, (1, GB_W), 1)
    out = jnp.zeros((rows, GB_W), F32)
    for h in range(N_HEADS_PER_DIL):
        out = jnp.where((lane >= HEAD_DIM_B * h) & (lane < HEAD_DIM_B * (h + 1)), col[h * rows:(h + 1) * rows], out)
    return out


def _head_cols(v):
    return jnp.concatenate([v[:, HEAD_DIM_B * h:HEAD_DIM_B * h + 1] for h in range(N_HEADS_PER_DIL)], axis=0)


def _seg_shift(s, dil):
    seg = s // dil
    assert seg & (seg - 1) == 0, "segment length must be a power of two"
    return seg.bit_length() - 1


def _band_fwd(dil, qb, kb, vb, bmap, tab, cb):
    s = qb.shape[0]
    shift = _seg_shift(s, dil)

    def body(q_ref, kp_ref, kc_ref, kn_ref, vp_ref, vc_ref, vn_ref, bmap_ref, tab_ref, o_ref, lse_ref, bias_ref):
        @pl.when(pl.program_id(0) == 0)
        def _():
            _build_bias(bmap_ref, tab_ref, bias_ref)

        kw, vw = _window(kp_ref, kc_ref, kn_ref), _window(vp_ref, vc_ref, vn_ref)
        for jj in range(cb // BAND_QB):
            r0 = BAND_QB * jj
            mask = _rows4(_segment_mask(pl.program_id(0) * cb + r0, s // dil, shift, False))
            sc = _head_scores(q_ref[r0:r0 + BAND_QB, :], kw[r0:r0 + BAND_WIN, :]) + bias_ref[...]
            sc = jnp.where(mask, sc, NEG_INF)
            m = jnp.max(sc, axis=-1, keepdims=True)
            e = jnp.exp2(sc - m)
            l = jnp.sum(e, axis=-1, keepdims=True)
            o = _head_combine(e.astype(BF16), vw[r0:r0 + BAND_WIN, :], 1.0 / l)
            o_ref[r0:r0 + BAND_QB, :] = o
            lse_ref[r0:r0 + BAND_QB, :] = _head_spread(m + jnp.log2(l))

    cur, prev, nxt = _band_specs(s, cb)
    return pl.pallas_call(
        body, name=f"band_fwd_d{dil}", grid=(s // cb,),
        in_specs=[cur, prev, cur, nxt, prev, cur, nxt, _resident(bmap.shape), pl.BlockSpec(memory_space=pltpu.SMEM)],
        out_specs=[cur, cur],
        out_shape=[jax.ShapeDtypeStruct(qb.shape, F32), jax.ShapeDtypeStruct(qb.shape, F32)],
        scratch_shapes=[pltpu.VMEM((N_HEADS_PER_DIL * BAND_QB, BAND_WIN), F32)],
        compiler_params=_cparams(("arbitrary",)))(qb, kb, kb, kb, vb, vb, vb, bmap, tab)


def _band_bwd_q(dil, qb, kb, vb, dob, lse, dd, bmap, tab, cb):
    s = qb.shape[0]
    shift = _seg_shift(s, dil)
    n_steps = s // cb

    def body(q_ref, do_ref, lse_ref, dd_ref, kp_ref, kc_ref, kn_ref, vp_ref, vc_ref, vn_ref, bmap_ref, tab_ref,
             dq_ref, dtab_ref, bias_ref, dsum_ref):
        @pl.when(pl.program_id(0) == 0)
        def _():
            _build_bias(bmap_ref, tab_ref, bias_ref)
            dsum_ref[...] = jnp.zeros_like(dsum_ref)

        kw, vw = _window(kp_ref, kc_ref, kn_ref), _window(vp_ref, vc_ref, vn_ref)
        for jj in range(cb // BAND_QB):
            r0 = BAND_QB * jj
            mask = _rows4(_segment_mask(pl.program_id(0) * cb + r0, s // dil, shift, False))
            k3, v3 = kw[r0:r0 + BAND_WIN, :], vw[r0:r0 + BAND_WIN, :]
            sc = _head_scores(q_ref[r0:r0 + BAND_QB, :], k3) + bias_ref[...]
            sc = jnp.where(mask, sc, NEG_INF)
            p = jnp.exp2(sc - _head_cols(lse_ref[r0:r0 + BAND_QB, :]))
            dp = _head_scores(do_ref[r0:r0 + BAND_QB, :], v3)
            ds = p * (dp - _head_cols(dd_ref[r0:r0 + BAND_QB, :]))
            dsum_ref[...] += ds
            dq_ref[r0:r0 + BAND_QB, :] = _head_combine(ds.astype(BF16), k3)

        @pl.when(pl.program_id(0) == n_steps - 1)
        def _():
            bm = bmap_ref[...]
            lane = lax.broadcasted_iota(jnp.int32, (1, LANES), 1)
            for b in range(N_REL_BUCKETS):
                hit = bm == b
                row = jnp.zeros((1, LANES), F32)
                for h in range(N_HEADS_PER_DIL):
                    part = dsum_ref[h * BAND_QB:(h + 1) * BAND_QB, :]
                    row = jnp.where(lane == h, jnp.sum(jnp.where(hit, part, 0.0)), row)
                dtab_ref[b:b + 1, :] = row

    cur, prev, nxt = _band_specs(s, cb)
    return pl.pallas_call(
        body, name=f"band_bwd_q_d{dil}", grid=(n_steps,),
        in_specs=[cur, cur, cur, cur, prev, cur, nxt, prev, cur, nxt, _resident(bmap.shape),
                  pl.BlockSpec(memory_space=pltpu.SMEM)],
        out_specs=[cur, _acc_spec((N_REL_BUCKETS, LANES))],
        out_shape=[jax.ShapeDtypeStruct(qb.shape, F32), jax.ShapeDtypeStruct((N_REL_BUCKETS, LANES), F32)],
        scratch_shapes=[pltpu.VMEM((N_HEADS_PER_DIL * BAND_QB, BAND_WIN), F32),
                        pltpu.VMEM((N_HEADS_PER_DIL * BAND_QB, BAND_WIN), F32)],
        compiler_params=_cparams(("arbitrary",)))(qb, dob, lse, dd, kb, kb, kb, vb, vb, vb, bmap, tab)


def _band_bwd_kv(dil, qb, kb, vb, dob, lse, dd, bmap_t, tab, cb):
    s = qb.shape[0]
    shift = _seg_shift(s, dil)

    def body(k_ref, v_ref, qp_ref, qc_ref, qn_ref, dp_ref, dc_ref, dn_ref, lp_ref, lc_ref, ln_ref,
             ep_ref, ec_ref, en_ref, bmap_ref, tab_ref, dk_ref, dv_ref, bias_ref):
        @pl.when(pl.program_id(0) == 0)
        def _():
            _build_bias(bmap_ref, tab_ref, bias_ref)

        qw, dow = _window(qp_ref, qc_ref, qn_ref), _window(dp_ref, dc_ref, dn_ref)
        lw, ew = _window(lp_ref, lc_ref, ln_ref), _window(ep_ref, ec_ref, en_ref)
        for jj in range(cb // BAND_QB):
            r0 = BAND_QB * jj
            mask = _rows4(_segment_mask(pl.program_id(0) * cb + r0, s // dil, shift, True))
            q3, do3 = qw[r0:r0 + BAND_WIN, :], dow[r0:r0 + BAND_WIN, :]
            sc = _head_scores(q3, k_ref[r0:r0 + BAND_QB, :]) + bias_ref[...]
            sc = jnp.where(mask, sc, NEG_INF)
            p = jnp.exp2(sc - _head_cols(lw[r0:r0 + BAND_WIN, :]))
            ds = p * (_head_scores(do3, v_ref[r0:r0 + BAND_QB, :]) - _head_cols(ew[r0:r0 + BAND_WIN, :]))
            dk_ref[r0:r0 + BAND_QB, :] = _head_combine(ds.astype(BF16), q3, transposed=True)
            dv_ref[r0:r0 + BAND_QB, :] = _head_combine(p.astype(BF16), do3, transposed=True)

    cur, prev, nxt = _band_specs(s, cb)
    win = [prev, cur, nxt]
    return pl.pallas_call(
        body, name=f"band_bwd_kv_d{dil}", grid=(s // cb,),
        in_specs=[cur, cur] + win * 4 + [_resident(bmap_t.shape), pl.BlockSpec(memory_space=pltpu.SMEM)],
        out_specs=[cur, cur],
        out_shape=[jax.ShapeDtypeStruct(qb.shape, F32), jax.ShapeDtypeStruct(qb.shape, F32)],
        scratch_shapes=[pltpu.VMEM((N_HEADS_PER_DIL * BAND_WIN, BAND_QB), F32)],
        compiler_params=_cparams(("arbitrary",)))(
        kb, vb, qb, qb, qb, dob, dob, dob, lse, lse, lse, dd, dd, dd, bmap_t, tab)


def _t5_bucket(rel):
    nb = N_REL_BUCKETS // 2
    ret = (rel > 0).astype(np.int32) * nb
    n = np.abs(rel)
    max_exact = nb // 2
    large = max_exact + (np.log(np.maximum(n, 1) / max_exact) / math.log(REL_MAX_DIST / max_exact)
                         * (nb - max_exact)).astype(np.int32)
    large = np.minimum(large, nb - 1)
    return ret + np.where(n < max_exact, n, large).astype(np.int32)


def _bucket_maps(dil):
    off_qk = np.arange(BAND_WIN)[None, :] - BAND - np.arange(BAND_QB)[:, None]
    off_kq = np.arange(BAND_QB)[None, :] + BAND - np.arange(BAND_WIN)[:, None]
    return [np.where(np.abs(off) <= BAND, _t5_bucket(off * dil), -1).astype(np.int32) for off in (off_qk, off_kq)]


def _seg_sum(v):
    lane = lax.broadcasted_iota(jnp.int32, (1, v.shape[1]), 1)
    out = jnp.zeros_like(v)
    for h in range(v.shape[1] // HEAD_DIM_B):
        m = (lane >= HEAD_DIM_B * h) & (lane < HEAD_DIM_B * (h + 1))
        out = jnp.where(m, jnp.sum(jnp.where(m, v, 0.0), axis=-1, keepdims=True), out)
    return out


def _mix_out(x, oa, og, lg, ga, gb, w_oa, w_ob_t, w_o, tb):
    s, d = x.shape

    def body(x_ref, oa_ref, og0_ref, og1_ref, og2_ref, lg0_ref, lg1_ref, lg2_ref, ga_ref, gb_ref,
             woa_ref, wob_ref, wo_ref, x2_ref, ob_ref, lse0_ref, lse1_ref, lse2_ref, ya_ref, yb_ref, u_ref, scr_ref):
        og_refs, lg_refs = (og0_ref, og1_ref, og2_ref), (lg0_ref, lg1_ref, lg2_ref)
        l0, l1, l2 = [_from_residues(lg_refs[g], scr_ref, dil) for g, dil in enumerate(DILATIONS)]
        lmax = jnp.maximum(jnp.maximum(l0, l1), l2)
        w0, w1, w2 = jnp.exp2(l0 - lmax), jnp.exp2(l1 - lmax), jnp.exp2(l2 - lmax)
        den = w0 + w1 + w2
        o0, o1, o2 = [_from_residues(og_refs[g], scr_ref, dil) for g, dil in enumerate(DILATIONS)]
        ob = ((w0 * o0 + w1 * o1 + w2 * o2) / den).astype(BF16)
        ob_ref[...] = ob
        lse = lmax + jnp.log2(den)
        for g, (dil, ref) in enumerate(zip(DILATIONS, (lse0_ref, lse1_ref, lse2_ref))):
            _to_residues(lse, ref, scr_ref, dil, F32)
        ya = _dot_nn(oa_ref[...], woa_ref[...])
        yb = _dot_nt(ob, wob_ref[...])
        ya_ref[...] = ya.astype(BF16)
        yb_ref[...] = yb.astype(BF16)
        u = (ga_ref[...] * ya + gb_ref[...] * yb).astype(BF16)
        u_ref[...] = u
        x2_ref[...] = x_ref[...] + _dot_nn(u, wo_ref[...])

    sd = jax.ShapeDtypeStruct
    res = list(pl.pallas_call(
        body, name="mix_out", grid=(s // tb,),
        in_specs=[_rows(tb, d), _rows(tb, QA_W)] + _dil_specs(tb) * 2 + [
            _rows(tb, d), _rows(tb, d), _resident(w_oa.shape), _resident(w_ob_t.shape), _resident(w_o.shape)],
        out_specs=[_rows(tb, d), _rows(tb, GB_W)] + _dil_specs(tb) + [_rows(tb, d), _rows(tb, d), _rows(tb, d)],
        out_shape=[sd((s, d), F32), sd((s, GB_W), BF16)] + _dil_shapes(s, F32) + [
            sd((s, d), BF16), sd((s, d), BF16), sd((s, d), BF16)],
        scratch_shapes=[pltpu.VMEM((2, tb, LANES), F32)],
        compiler_params=_cparams(("arbitrary",)))(x, oa, *og, *lg, ga, gb, w_oa, w_ob_t, w_o))
    return res[:2] + [res[2:5]] + res[5:]


def _mlp_fwd(x2, w1_t, w2, g_mlp, tb, tc):
    s, d = x2.shape
    dff = w1_t.shape[0]

    def body(x_ref, w1_ref, w2_ref, g_ref, x3_ref, r_ref, h_ref):
        xv = x_ref[...]
        hb = (xv * _rstd(xv) * g_ref[...]).astype(BF16)
        h_ref[...] = hb
        x3_ref[...] = xv
        for c in range(dff // tc):
            sl = slice(tc * c, tc * c + tc)
            r = jnp.maximum(_dot_nt(hb, w1_ref[sl, :]), 0.0)
            r_ref[:, sl] = r.astype(BF16)
            x3_ref[...] += _dot_nn((r * r).astype(BF16), w2_ref[sl, :])

    sd = jax.ShapeDtypeStruct
    return pl.pallas_call(
        body, name="mlp_fwd", grid=(s // tb,),
        in_specs=[_rows(tb, d), _resident(w1_t.shape), _resident(w2.shape), _resident(g_mlp.shape)],
        out_specs=[_rows(tb, d), _rows(tb, dff), _rows(tb, d)],
        out_shape=[sd((s, d), F32), sd((s, dff), BF16), sd((s, d), BF16)],
        compiler_params=_cparams(("arbitrary",)))(x2, w1_t, w2, g_mlp)


def _ple_loss(x3, p, target, w_pg, w_p_t, g_ple, g_fin, tb):
    s, d = x3.shape
    dp = p.shape[1]

    def body(x_ref, p_ref, t_ref, wpg_ref, wp_ref, gple_ref, gfin_ref,
             dx3_ref, h3_ref, dpre_ref, dpe_ref, pb_ref, loss_ref, dgfin_ref, dgple_ref):
        @pl.when(pl.program_id(0) == 0)
        def _():
            loss_ref[...] = jnp.zeros_like(loss_ref)
            dgfin_ref[...] = jnp.zeros_like(dgfin_ref)
            dgple_ref[...] = jnp.zeros_like(dgple_ref)

        x3v = x_ref[...]
        r3 = _rstd(x3v)
        n3 = x3v * r3
        h3 = (n3 * gple_ref[...]).astype(BF16)
        h3_ref[...] = h3
        gp = _sigmoid(_dot_nn(h3, wpg_ref[...]))
        pb = p_ref[...].astype(BF16)
        pb_ref[...] = pb
        pe = _dot_nt(pb, wp_ref[...])
        x4 = x3v + gp * pe
        r4 = _rstd(x4)
        n4 = x4 * r4
        err = n4 * gfin_ref[...] - t_ref[...]
        loss_ref[...] += jnp.sum(0.5 * jnp.mean(err * err, axis=-1, keepdims=True), axis=0, keepdims=True)
        dy = err / d
        dgfin_ref[...] += _colsum(dy * n4)
        dx4 = _rms_bwd(dy, n4, r4, gfin_ref[...])
        dpe_ref[...] = (dx4 * gp).astype(BF16)
        dpre = (dx4 * pe * gp * (1.0 - gp)).astype(BF16)
        dpre_ref[...] = dpre
        dh3 = _dot_nt(dpre, wpg_ref[...])
        dgple_ref[...] += _colsum(dh3 * n3)
        dx3_ref[...] = dx4 + _rms_bwd(dh3, n3, r3, gple_ref[...])

    sd = jax.ShapeDtypeStruct
    return pl.pallas_call(
        body, name="ple_loss", grid=(s // tb,),
        in_specs=[_rows(tb, d), _rows(tb, dp), _rows(tb, d), _resident(w_pg.shape), _resident(w_p_t.shape),
                  _resident(g_ple.shape), _resident(g_fin.shape)],
        out_specs=[_rows(tb, d), _rows(tb, d), _rows(tb, d), _rows(tb, d), _rows(tb, dp),
                   _acc_spec((1, LANES)), _acc_spec((1, d)), _acc_spec((1, d))],
        out_shape=[sd((s, d), F32), sd((s, d), BF16), sd((s, d), BF16), sd((s, d), BF16), sd((s, dp), BF16),
                   sd((1, LANES), F32), sd((1, d), F32), sd((1, d), F32)],
        compiler_params=_cparams(("arbitrary",)))(x3, p, target, w_pg, w_p_t, g_ple, g_fin)


def _mlp_bwd(dx3, x2, r, w1_t, w2, g_mlp, tb, tc):
    s, d = x2.shape
    dff = w1_t.shape[0]

    def body(dx3_ref, x_ref, r_ref, w1_ref, w2_ref, g_ref, dx2_ref, df_ref, dg_ref, dh_ref):
        @pl.when(pl.program_id(0) == 0)
        def _():
            dg_ref[...] = jnp.zeros_like(dg_ref)

        dx3v = dx3_ref[...]
        dx3b = dx3v.astype(BF16)
        dh_ref[...] = jnp.zeros_like(dh_ref)
        for c in range(dff // tc):
            sl = slice(tc * c, tc * c + tc)
            df = (_dot_nt(dx3b, w2_ref[sl, :]) * (2.0 * r_ref[:, sl].astype(F32))).astype(BF16)
            df_ref[:, sl] = df
            dh_ref[...] += _dot_nn(df, w1_ref[sl, :])
        xv = x_ref[...]
        r2 = _rstd(xv)
        n2 = xv * r2
        dh = dh_ref[...]
        dg_ref[...] += _colsum(dh * n2)
        dx2_ref[...] = dx3v + _rms_bwd(dh, n2, r2, g_ref[...])

    sd = jax.ShapeDtypeStruct
    return pl.pallas_call(
        body, name="mlp_bwd", grid=(s // tb,),
        in_specs=[_rows(tb, d), _rows(tb, d), _rows(tb, dff), _resident(w1_t.shape), _resident(w2.shape),
                  _resident(g_mlp.shape)],
        out_specs=[_rows(tb, d), _rows(tb, dff), _acc_spec((1, d))],
        out_shape=[sd((s, d), F32), sd((s, dff), BF16), sd((1, d), F32)],
        scratch_shapes=[pltpu.VMEM((tb, d), F32)],
        compiler_params=_cparams(("arbitrary",)))(dx3, x2, r, w1_t, w2, g_mlp)


def _mix_out_bwd(dx2, ya, yb, ga, gb, ob, w_oa, w_ob_t, w_o, tb, after):
    s, d = dx2.shape

    def body(dx_ref, ya_ref, yb_ref, ga_ref, gb_ref, ob_ref, woa_ref, wob_ref, wo_ref, after_ref,
             doa_ref, dob0_ref, dob1_ref, dob2_ref, dd0_ref, dd1_ref, dd2_ref, dga_ref, dgb_ref, dya_ref, dyb_ref,
             dbg_ref, scr_ref):
        @pl.when(pl.program_id(0) == 0)
        def _():
            dbg_ref[...] = jnp.zeros_like(dbg_ref)

        du = _dot_nt(dx_ref[...].astype(BF16), wo_ref[...])
        gav, gbv = ga_ref[...], gb_ref[...]
        dya = (du * gav).astype(BF16)
        dyb = (du * gbv).astype(BF16)
        dya_ref[...] = dya
        dyb_ref[...] = dyb
        dga = du * ya_ref[...].astype(F32) * gav * (1.0 - gav)
        dgb = du * yb_ref[...].astype(F32) * gbv * (1.0 - gbv)
        dga_ref[...] = dga.astype(BF16)
        dgb_ref[...] = dgb.astype(BF16)
        dbg_ref[:, 0:d] += _colsum(dga)
        dbg_ref[:, d:2 * d] += _colsum(dgb)
        doa_ref[...] = _dot_nt(dya, woa_ref[...]).astype(BF16)
        dob = _dot_nn(dyb, wob_ref[...])
        dd = _seg_sum(dob * ob_ref[...].astype(F32))
        for dil, dob_ref, dd_ref in zip(DILATIONS, (dob0_ref, dob1_ref, dob2_ref), (dd0_ref, dd1_ref, dd2_ref)):
            _to_residues(dob, dob_ref, scr_ref, dil, BF16)
            _to_residues(dd, dd_ref, scr_ref, dil, F32)

    sd = jax.ShapeDtypeStruct
    res = list(pl.pallas_call(
        body, name="mix_out_bwd", grid=(s // tb,),
        in_specs=[_rows(tb, d)] * 5 + [_rows(tb, GB_W), _resident(w_oa.shape), _resident(w_ob_t.shape),
                                       _resident(w_o.shape), _ANY],
        out_specs=[_rows(tb, QA_W)] + _dil_specs(tb) * 2 + [_rows(tb, d), _rows(tb, d), _rows(tb, d),
                                                           _rows(tb, d), _acc_spec((1, 2 * d))],
        out_shape=[sd((s, QA_W), BF16)] + _dil_shapes(s, BF16) + _dil_shapes(s, F32) + [
            sd((s, d), BF16), sd((s, d), BF16), sd((s, d), BF16), sd((s, d), BF16), sd((1, 2 * d), F32)],
        scratch_shapes=[pltpu.VMEM((2, tb, LANES), F32)],
        compiler_params=_cparams(("arbitrary",)))(dx2, ya, yb, ga, gb, ob, w_oa, w_ob_t, w_o, after))
    return res[:1] + [res[1:4], res[4:7]] + res[7:]


def _in_proj_bwd(dx2, x, dqrot, dkrot, dva, qraw, kraw, tabs, dqb, dkb, dvb, dga, dgb, w_in_t, g_mix, q_g, k_g, tb):
    s, d = x.shape
    din = w_in_t.shape[0]
    q_scale = HEAD_DIM_A ** -0.5
    b_scale = HEAD_DIM_B ** -0.5
    tc = 256

    def body(dx2_ref, x_ref, dq_ref, dk_ref, dv_ref, qraw_ref, kraw_ref, c_ref, s1_ref, s2_ref, *rest):
        dqb_refs, dkb_refs, dvb_refs = rest[0:3], rest[3:6], rest[6:9]
        (dga_ref, dgb_ref, w_ref, gmix_ref, qg_ref, kg_ref,
         dx_ref, dz_ref, dgmix_ref, dqg_ref, dkg_ref, dh_ref, scr_ref) = rest[9:]

        @pl.when(pl.program_id(0) == 0)
        def _():
            dgmix_ref[...] = jnp.zeros_like(dgmix_ref)
            dqg_ref[...] = jnp.zeros_like(dqg_ref)
            dkg_ref[...] = jnp.zeros_like(dkg_ref)

        cos, s1, s2 = c_ref[...][None], s1_ref[...][None], s2_ref[...][None]

        def heads_bwd(drot, z, g_ref, acc_ref):
            dn = drot * cos + pltpu.roll(drot * s1, 96, 2) + pltpu.roll(drot * s2, 32, 2)
            rr = _rstd(z)
            nn = z * rr
            acc_ref[...] += jnp.sum(jnp.sum(dn * nn, axis=0), axis=0, keepdims=True)
            return _rms_bwd(dn, nn, rr, g_ref[...][None]).astype(BF16)

        dh_ref[...] = jnp.zeros_like(dh_ref)

        def emit(off, piece):
            dz_ref[:, off:off + tc] = piece
            dh_ref[...] += _dot_nn(piece, w_ref[off:off + tc, :])

        for j in range(d // tc):
            emit(OFF_GA + tc * j, dga_ref[:, tc * j:tc * j + tc])
            emit(OFF_GA + d + tc * j, dgb_ref[:, tc * j:tc * j + tc])
        emit(OFF_VA, dv_ref[...].T.astype(BF16))
        for g, dil in enumerate(DILATIONS):
            emit(OFF_QB + GB_W * g, (_from_residues(dqb_refs[g], scr_ref, dil) * b_scale).astype(BF16))
            emit(OFF_KB + GB_W * g, (_from_residues(dkb_refs[g], scr_ref, dil) * LN_2).astype(BF16))
            emit(OFF_VB + GB_W * g, _from_residues(dvb_refs[g], scr_ref, dil).astype(BF16))
        stack = lambda ref, n: jnp.stack([ref[:, 128 * h:128 * h + 128] for h in range(n)], axis=0)
        dzq = heads_bwd(stack(dq_ref, N_Q_HEADS_A) * q_scale, stack(qraw_ref, N_Q_HEADS_A), qg_ref, dqg_ref)
        dkt = jnp.stack([dk_ref[128 * h:128 * h + 128, :].T for h in range(N_KV_HEADS_A)], axis=0) * LN_2
        dzk = heads_bwd(dkt, stack(kraw_ref, N_KV_HEADS_A), kg_ref, dkg_ref)
        for j in range(N_Q_HEADS_A // 2):
            emit(OFF_QA + tc * j, jnp.concatenate([dzq[2 * j], dzq[2 * j + 1]], axis=1))
        emit(OFF_KA, jnp.concatenate([dzk[0], dzk[1]], axis=1))
        xv = x_ref[...]
        r1 = _rstd(xv)
        n1 = xv * r1
        dh = dh_ref[...]
        dgmix_ref[...] += _colsum(dh * n1)
        dx_ref[...] = dx2_ref[...] + _rms_bwd(dh, n1, r1, gmix_ref[...])

    sd = jax.ShapeDtypeStruct
    return pl.pallas_call(
        body, name="in_proj_bwd", grid=(s // tb,),
        in_specs=[_rows(tb, d), _rows(tb, d), _rows(tb, QA_W), pl.BlockSpec((KA_W, tb), lambda i: (0, i)),
                  pl.BlockSpec((KA_W, tb), lambda i: (0, i)), _rows(tb, QA_W),
                  _rows(tb, KA_W), _rows(tb, LANES), _rows(tb, LANES), _rows(tb, LANES),
                  ] + _dil_specs(tb) * 3 + [_rows(tb, d), _rows(tb, d),
                  _resident(w_in_t.shape), _resident(g_mix.shape), _resident(q_g.shape), _resident(k_g.shape)],
        out_specs=[_rows(tb, d), _rows(tb, din), _acc_spec((1, d)), _acc_spec((1, HEAD_DIM_A)),
                   _acc_spec((1, HEAD_DIM_A))],
        out_shape=[sd((s, d), F32), sd((s, din), BF16), sd((1, d), F32), sd((1, HEAD_DIM_A), F32),
                   sd((1, HEAD_DIM_A), F32)],
        scratch_shapes=[pltpu.VMEM((tb, d), F32), pltpu.VMEM((2, tb, LANES), F32)],
        compiler_params=_cparams(("arbitrary",)))(
        dx2, x, dqrot, dkrot, dva, qraw, kraw, *tabs, *dqb, *dkb, *dvb, dga, dgb, w_in_t, g_mix, q_g, k_g)


def _identity(v):
    return v


def _to_bf16(v):
    return v.astype(BF16)


def _square_bf16(v):
    vf = v.astype(F32)
    return (vf * vf).astype(BF16)


def _weight_grad(name, a, b, ti, tj, tk, a_fn=_identity, b_fn=_identity, col0=0, n=None, after=None):
    t, m = a.shape
    n = b.shape[1] if n is None else n
    n_k = t // tk
    after = a if after is None else after

    def body(a_ref, b_ref, after_ref, o_ref, acc_ref):
        k = pl.program_id(2)

        @pl.when(k == 0)
        def _():
            acc_ref[...] = jnp.zeros_like(acc_ref)

        acc_ref[...] += _dot_tn(a_fn(a_ref[...]), b_fn(b_ref[...]))

        @pl.when(k == n_k - 1)
        def _():
            o_ref[...] = acc_ref[...].astype(BF16)

    return pl.pallas_call(
        body, name=name, grid=(m // ti, n // tj, n_k),
        in_specs=[pl.BlockSpec((tk, ti), lambda i, j, k: (k, i)),
                  pl.BlockSpec((tk, tj), lambda i, j, k: (k, j + col0 // tj)), _ANY],
        out_specs=pl.BlockSpec((ti, tj), lambda i, j, k: (i, j)),
        out_shape=jax.ShapeDtypeStruct((m, n), BF16),
        scratch_shapes=[pltpu.VMEM((ti, tj), F32)],
        compiler_params=_cparams(("arbitrary", "arbitrary", "arbitrary")))(a, b, after)


def _sum_slots(name, recv, own):
    m, n, k = recv.shape
    tc = min(k, 256)

    def body(own_ref, r_ref, o_ref):
        acc = own_ref[...].astype(F32)
        for i in range(m):
            acc = acc + r_ref[i].astype(F32)
        o_ref[...] = acc

    return pl.pallas_call(
        body, name=name, grid=(k // tc,),
        in_specs=[pl.BlockSpec((n, tc), lambda j: (0, j)), pl.BlockSpec((m, n, tc), lambda j: (0, 0, j))],
        out_specs=pl.BlockSpec((n, tc), lambda j: (0, j)),
        out_shape=jax.ShapeDtypeStruct((n, k), F32),
        compiler_params=_cparams(("arbitrary",)))(own, recv)


def _adamw_math(w, g, m, v):
    m = ADAM_B1 * m + (1.0 - ADAM_B1) * g
    v = ADAM_B2 * v + (1.0 - ADAM_B2) * (g * g)
    m_hat = m / (1.0 - ADAM_B1 ** ADAM_STEP)
    v_hat = v / (1.0 - ADAM_B2 ** ADAM_STEP)
    delta = -ADAM_LR * (m_hat / (jnp.sqrt(v_hat) + ADAM_EPS) + ADAM_WD * w)
    return delta, m, v


def _adamw(name, w, g, m, v):
    r, c = w.shape
    tr = min(r, 256)

    def body(w_ref, g_ref, m_ref, v_ref, d_ref, mo_ref, vo_ref):
        d_ref[...], mo_ref[...], vo_ref[...] = _adamw_math(w_ref[...], g_ref[...], m_ref[...], v_ref[...])

    spec = pl.BlockSpec((tr, c), lambda i: (i, 0))
    return pl.pallas_call(
        body, name=name, grid=(r // tr,), in_specs=[spec] * 4, out_specs=[spec] * 3,
        out_shape=[jax.ShapeDtypeStruct((r, c), F32)] * 3,
        compiler_params=_cparams(("arbitrary",)))(w, g, m, v)


def _small_update(parts, w, m, v):
    def body(p_ref, w_ref, m_ref, v_ref, g_ref, d_ref, mo_ref, vo_ref):
        g = p_ref[0]
        for i in range(1, N_DEV):
            g = g + p_ref[i]
        g_ref[...] = g
        d_ref[...], mo_ref[...], vo_ref[...] = _adamw_math(w_ref[...], g, m_ref[...], v_ref[...])

    return pl.pallas_call(body, name="small_update", out_shape=[jax.ShapeDtypeStruct(w.shape, F32)] * 4)(
        parts, w, m, v)


def _pack_rows(vectors, n_rows):
    flat = jnp.concatenate([v.reshape(-1).astype(F32) for v in vectors])
    flat = jnp.pad(flat, (0, n_rows * LANES - flat.shape[0]))
    return flat.reshape(n_rows, LANES)


def _pick_tile(n, prefs):
    for t in prefs:
        if n % t == 0:
            return t
    return n


def kernel(x, p, norm_mix_g, w_in, b_gate, q_norm_g, k_norm_g, rel_bias, w_out_a, w_out_b, w_out, norm_mlp_g, w_ff1, w_ff2, norm_ple_g, w_ple_gate, w_ple, final_norm_g, loss_target, m_norm_mix_g, m_w_in, m_b_gate, m_q_norm_g, m_k_norm_g, m_rel_bias, m_w_out_a, m_w_out_b, m_w_out, m_norm_mlp_g, m_w_ff1, m_w_ff2, m_norm_ple_g, m_w_ple_gate, m_w_ple, m_final_norm_g, v_norm_mix_g, v_w_in, v_b_gate, v_q_norm_g, v_k_norm_g, v_rel_bias, v_w_out_a, v_w_out_b, v_w_out, v_norm_mlp_g, v_w_ff1, v_w_ff2, v_norm_ple_g, v_w_ple_gate, v_w_ple, v_final_norm_g):
    s, d = x.shape[1], x.shape[2]
    xs, ps, ts = x[0], p[0, 0], loss_target[0]
    tb = _pick_tile(s, (512, 256))
    tq = _pick_tile(s, (256,))
    tk = _pick_tile(s, (1024, 512))
    cb = _pick_tile(s, (512,))
    fin_g = final_norm_g.reshape(1, d)

    col_sharded = {"w_in": w_in[0], "w_out_b": w_out_b[0], "w_ff1": w_ff1[0], "w_ple": w_ple[0]}
    row_sharded = {"w_out_a": w_out_a[0], "w_out": w_out[0], "w_ff2": w_ff2[0], "w_ple_gate": w_ple_gate[0]}
    order = ["w_in", "w_out_a", "w_out_b", "w_out", "w_ff1", "w_ff2", "w_ple_gate", "w_ple"]
    shards = [(col_sharded[n].T if n in col_sharded else row_sharded[n]).astype(BF16) for n in order]
    my_idx = 4 * lax.axis_index("x") + 2 * lax.axis_index("y") + lax.axis_index("c")
    (w_in_t,) = _all_gather(shards[:1])
    zones = [lax.dynamic_update_slice(lax.empty((N_DEV * sh.shape[0], sh.shape[1]), BF16), sh,
                                      (my_idx * sh.shape[0], 0)) for sh in shards[1:]]
    ag = _copies_start("weights_gather_start", shards[1:], zones, w_in_t, True)

    tabs = _rope_tables(s)
    (h1, qraw, kraw, qrot, krot, va, qb, kb, vb, ga, gb) = _in_proj(
        xs, tabs, w_in_t, norm_mix_g, b_gate, q_norm_g, k_norm_g, tb, ag[4])
    oa, lse_a = _attn_a_fwd(qrot, krot, va, tq, tk)
    _, (w_oa, w_ob_t, w_o, w_ff1_t, w_ff2_f, w_pg, w_p_t) = _copies_wait(
        "weights_gather_wait", ag[0], ag[1], ag[2], ag[3], lse_a, True)
    flat = lambda arrs: [a.reshape(s, GB_W) for a in arrs]
    split = lambda arrs: [a.reshape(dil, s // dil, GB_W) for a, dil in zip(arrs, DILATIONS)]
    qb_r, kb_r, vb_r = flat(qb), flat(kb), flat(vb)
    bmaps = [[jnp.asarray(m) for m in _bucket_maps(dil)] for dil in DILATIONS]
    bias_tabs = [rel_bias[:, N_HEADS_PER_DIL * g:N_HEADS_PER_DIL * (g + 1)] for g in range(3)]
    band_out = [_band_fwd(dil, qb_r[g], kb_r[g], vb_r[g], bmaps[g][0], bias_tabs[g], cb)
                for g, dil in enumerate(DILATIONS)]
    og, lg = split([o for o, _ in band_out]), split([l for _, l in band_out])
    x2, ob, lse_b, ya, yb, u = _mix_out(xs, oa, og, lg, ga, gb, w_oa, w_ob_t, w_o, tb)
    tc = _pick_tile(w_ff1_t.shape[0], (512,))
    x3, r_act, h2 = _mlp_fwd(x2, w_ff1_t, w_ff2_f, norm_mlp_g, tb, tc)

    dx3, h3, dpre, dpe, pb, loss_part, dg_fin, dg_ple = _ple_loss(
        x3, ps, ts, w_pg, w_p_t, norm_ple_g, fin_g, tb)
    dx2, df, dg_mlp = _mlp_bwd(dx3, x2, r_act, w_ff1_t, w_ff2_f, norm_mlp_g, tb, tc)

    tkk = _pick_tile(s, (1024, 512))
    dff = w_ff1_t.shape[0]
    t1k = lambda n: _pick_tile(n, (1024, 512, 256))
    slots = lambda parts: [lax.empty((7, a.shape[0] // N_DEV, a.shape[1]), BF16) for a in parts]
    part1 = [_weight_grad("grad_w_ff1", df, h2, t1k(dff), t1k(d), tkk),
             _weight_grad("grad_w_ff2", r_act, dx3, t1k(dff), t1k(d), tkk, a_fn=_square_bf16, b_fn=_to_bf16),
             _weight_grad("grad_w_ple_gate", h3, dpre, t1k(d), t1k(d), tkk),
             _weight_grad("grad_w_ple", dpe, pb, t1k(d), ps.shape[1], tkk)]
    doa, dob, dd, dga, dgb, dya, dyb, dbg = _mix_out_bwd(dx2, ya, yb, ga, gb, ob, w_oa, w_ob_t, w_o, tb, dx2)
    part1 += [_weight_grad("grad_w_out_a", oa, dya, t1k(QA_W), t1k(d), tkk),
              _weight_grad("grad_w_out_b", dyb, ob, t1k(d), GB_W, tkk),
              _weight_grad("grad_w_out", u, dx2, t1k(d), t1k(d), tkk, b_fn=_to_bf16)]
    rs1 = _copies_start("grads1_start", part1, slots(part1), doa, False)
    dqrot, dkrot, dva = _attn_a_bwd(qrot, krot, va, oa, doa, lse_a, tq, tk, rs1[4])
    dob_r, lse_r, dd_r = flat(dob), flat(lse_b), flat(dd)
    bwd_q = [_band_bwd_q(dil, qb_r[g], kb_r[g], vb_r[g], dob_r[g], lse_r[g], dd_r[g], bmaps[g][0], bias_tabs[g], cb)
             for g, dil in enumerate(DILATIONS)]
    bwd_kv = [_band_bwd_kv(dil, qb_r[g], kb_r[g], vb_r[g], dob_r[g], lse_r[g], dd_r[g], bmaps[g][1], bias_tabs[g], cb)
              for g, dil in enumerate(DILATIONS)]
    dqb, dkb, dvb = split([r[0] for r in bwd_q]), split([r[0] for r in bwd_kv]), split([r[1] for r in bwd_kv])
    grad_x, dz, dg_mix, dg_q, dg_k = _in_proj_bwd(
        dx2, xs, dqrot, dkrot, dva, qraw, kraw, tabs, dqb, dkb, dvb, dga, dgb, w_in_t, norm_mix_g,
        q_norm_g, k_norm_g, _pick_tile(s, (256,)))
    d_rel = jnp.concatenate([r[1][:, :N_HEADS_PER_DIL] for r in bwd_q], axis=1)

    din = w_in_t.shape[0]
    ti_in = _pick_tile(din, (din // 2,)) if (din // 2) % LANES == 0 else din
    hd_ = d // 2
    part3 = [_weight_grad("grad_w_in_lo", dz, h1, ti_in, t1k(hd_), tkk, n=hd_)]
    rs3 = _copies_start("grads3_start", part3, slots(part3), grad_x, False)
    part4 = [_weight_grad("grad_w_in_hi", dz, h1, ti_in, t1k(hd_), tkk, col0=hd_, n=hd_, after=rs3[4])]
    rs4 = _copies_start("grads4_start", part4, slots(part4), rs3[4], False)

    def own_rows(a):
        n = a.shape[0] // N_DEV
        return lax.dynamic_slice(a, (my_idx * n, 0), (n, a.shape[1]))

    sums = {}
    src1, got1 = _copies_wait("grads1_wait", rs1[0], rs1[1], rs1[2], rs1[3], rs4[4], False)
    for n, a, r in zip(["w_ff1", "w_ff2", "w_ple_gate", "w_ple", "w_out_a", "w_out_b", "w_out"], src1, got1):
        sums[n] = _sum_slots("sum_" + n, r, own_rows(a))
    given_w = dict(w_in=w_in, w_out_a=w_out_a, w_out_b=w_out_b, w_out=w_out, w_ff1=w_ff1, w_ff2=w_ff2,
                   w_ple_gate=w_ple_gate, w_ple=w_ple)
    given_m = dict(w_in=m_w_in, w_out_a=m_w_out_a, w_out_b=m_w_out_b, w_out=m_w_out, w_ff1=m_w_ff1, w_ff2=m_w_ff2,
                   w_ple_gate=m_w_ple_gate, w_ple=m_w_ple)
    given_v = dict(w_in=v_w_in, w_out_a=v_w_out_a, w_out_b=v_w_out_b, w_out=v_w_out, w_ff1=v_w_ff1, w_ff2=v_w_ff2,
                   w_ple_gate=v_w_ple_gate, w_ple=v_w_ple)
    big = {}

    def update(n):
        g = sums[n].T if n in col_sharded else sums[n]
        delta, new_m, new_v = _adamw("adamw_" + n, given_w[n][0], g, given_m[n][0], given_v[n][0])
        big[n] = tuple(a[None] for a in (g, delta, new_m, new_v))

    for n in order[1:]:
        update(n)
    src3, got3 = _copies_wait("grads3_wait", rs3[0], rs3[1], rs3[2], rs3[3], big["w_ple"][1], False)
    src4, got4 = _copies_wait("grads4_wait", rs4[0], rs4[1], rs4[2], rs4[3], big["w_ple"][1], False)
    sums["w_in"] = jnp.concatenate([_sum_slots("sum_w_in_lo", got3[0], own_rows(src3[0])),
                                    _sum_slots("sum_w_in_hi", got4[0], own_rows(src4[0]))], axis=1)
    update("w_in")

    small_names = ["norm_mix_g", "b_gate", "q_norm_g", "k_norm_g", "rel_bias", "norm_mlp_g", "norm_ple_g",
                   "final_norm_g"]
    small_w = [norm_mix_g, b_gate, q_norm_g, k_norm_g, rel_bias, norm_mlp_g, norm_ple_g, final_norm_g]
    small_m = [m_norm_mix_g, m_b_gate, m_q_norm_g, m_k_norm_g, m_rel_bias, m_norm_mlp_g, m_norm_ple_g,
               m_final_norm_g]
    small_v = [v_norm_mix_g, v_b_gate, v_q_norm_g, v_k_norm_g, v_rel_bias, v_norm_mlp_g, v_norm_ple_g,
               v_final_norm_g]
    small_g = [dg_mix, dbg, dg_q, dg_k, d_rel, dg_mlp, dg_ple, dg_fin]
    sizes = [int(np.prod(w.shape)) for w in small_w]
    n_rows = -(-(sum(-(-sz // LANES) for sz in sizes) + 1) // 8) * 8
    pad = lambda v: jnp.pad(v.reshape(-1).astype(F32), (0, -v.size % LANES))
    pack = lambda vs, last: _pack_rows([pad(v) for v in vs] + [last], n_rows)
    zero_row = jnp.zeros((LANES,), F32)
    parts = _small_all_gather(pack(small_g, loss_part.reshape(-1) * (jnp.arange(LANES) == 0)))
    g_all, d_all, m_all, v_all = _small_update(parts, pack(small_w, zero_row), pack(small_m, zero_row),
                                               pack(small_v, zero_row))
    small = {}
    row = 0
    for n, w, sz in zip(small_names, small_w, sizes):
        nr = -(-sz // LANES)
        small[n] = tuple(a[row:row + nr].reshape(-1)[:sz].reshape(w.shape) for a in (g_all, d_all, m_all, v_all))
        row += nr
    loss = g_all[row, 0]

    names = ["norm_mix_g", "w_in", "b_gate", "q_norm_g", "k_norm_g", "rel_bias", "w_out_a", "w_out_b", "w_out",
             "norm_mlp_g", "w_ff1", "w_ff2", "norm_ple_g", "w_ple_gate", "w_ple", "final_norm_g"]
    res = {n: (big[n] if n in big else small[n]) for n in names}
    return (loss, grad_x[None], *[res[n][0] for n in names], *[res[n][1] for n in names],
            *[res[n][2] for n in names], *[res[n][3] for n in names])
```

```python
import functools
import math

import numpy as np
import jax
import jax.numpy as jnp
from jax import lax
from jax.experimental import pallas as pl
from jax.experimental.pallas import tpu as pltpu

F32 = jnp.float32
BF16 = jnp.bfloat16
MESH = pl.DeviceIdType.MESH

NORM_EPS = 1e-6
NEG_INF = -1e30
LOG2_E = math.log2(math.e)
LN_2 = math.log(2.0)
GRID_W = 64
ROPE_THETA = 10000.0
HEAD_DIM_A = 128
N_Q_HEADS_A = 8
N_KV_HEADS_A = 2
Q_PER_KV = N_Q_HEADS_A // N_KV_HEADS_A
HEAD_DIM_B = 64
N_HEADS_PER_DIL = 4
DILATIONS = (1, 4, 16)
BAND = 64
N_REL_BUCKETS = 32
REL_MAX_DIST = 1024
QA_W = N_Q_HEADS_A * HEAD_DIM_A
KA_W = N_KV_HEADS_A * HEAD_DIM_A
GB_W = N_HEADS_PER_DIL * HEAD_DIM_B
QB_W = GB_W * len(DILATIONS)
OFF_QA, OFF_KA, OFF_VA = 0, QA_W, QA_W + KA_W
OFF_QB = QA_W + 2 * KA_W
OFF_KB = OFF_QB + QB_W
OFF_VB = OFF_KB + QB_W
OFF_GA = OFF_VB + QB_W
N_DEV = 8
LANES = 128
VMEM_LIMIT = 56 * 2 ** 20

ADAM_LR, ADAM_B1, ADAM_B2, ADAM_EPS, ADAM_WD, ADAM_STEP = 0.001, 0.9, 0.999, 1e-08, 0.01, 10


def _cparams(sem):
    return pltpu.CompilerParams(dimension_semantics=sem, vmem_limit_bytes=VMEM_LIMIT)


def _resident(shape):
    nd = len(shape)
    return pl.BlockSpec(shape, lambda *_: (0,) * nd, pipeline_mode=pl.Buffered(1))


def _acc_spec(shape):
    nd = len(shape)
    return pl.BlockSpec(shape, lambda *_: (0,) * nd)


def _rows(tb, c):
    return pl.BlockSpec((tb, c), lambda i: (i, 0))


def _dil_shapes(s, dtype):
    return [jax.ShapeDtypeStruct((dil, s // dil, GB_W), dtype) for dil in DILATIONS]


def _dil_specs(tb):
    return [pl.BlockSpec((dil, tb // dil, GB_W), lambda i: (0, i, 0)) for dil in DILATIONS]


def _to_residues(val, out_ref, scr_ref, dil, dtype):
    if dil == 1:
        out_ref[0] = val.astype(dtype)
        return
    n = val.shape[0] // dil
    scr_ref[0] = val[:, :LANES]
    scr_ref[1] = val[:, LANES:]
    for r in range(dil):
        out_ref[r] = jnp.concatenate([scr_ref[0, pl.ds(r, n, stride=dil), :],
                                      scr_ref[1, pl.ds(r, n, stride=dil), :]], axis=1).astype(dtype)


def _from_residues(in_ref, scr_ref, dil):
    if dil == 1:
        return in_ref[0]
    n = in_ref.shape[1]
    for r in range(dil):
        v = in_ref[r]
        scr_ref[0, pl.ds(r, n, stride=dil), :] = v[:, :LANES]
        scr_ref[1, pl.ds(r, n, stride=dil), :] = v[:, LANES:]
    return jnp.concatenate([scr_ref[0], scr_ref[1]], axis=1)


def _dot_nt(a, b):
    return lax.dot_general(a, b, (((1,), (1,)), ((), ())), preferred_element_type=F32)


def _dot_nn(a, b):
    return lax.dot_general(a, b, (((1,), (0,)), ((), ())), preferred_element_type=F32)


def _dot_tn(a, b):
    return lax.dot_general(a, b, (((0,), (0,)), ((), ())), preferred_element_type=F32)


def _rstd(x):
    return lax.rsqrt(jnp.mean(x * x, axis=-1, keepdims=True) + NORM_EPS)


def _rms_bwd(dy, n, r, g):
    dn = dy * g
    return r * (dn - n * jnp.mean(dn * n, axis=-1, keepdims=True))


def _colsum(v):
    return jnp.sum(v, axis=0, keepdims=True)


def _sigmoid(v):
    return 1.0 / (1.0 + jnp.exp(-v))


def _rope_tables(s):
    half = HEAD_DIM_A // 2
    inv = np.power(np.float32(ROPE_THETA), -np.arange(0, half, 2, dtype=np.float32) / np.float32(half))
    t = np.arange(s)
    ang_r = (t // GRID_W).astype(np.float32)[:, None] * inv[None, :]
    ang_c = (t % GRID_W).astype(np.float32)[:, None] * inv[None, :]
    cr, sr, cc, sc = np.cos(ang_r), np.sin(ang_r), np.cos(ang_c), np.sin(ang_c)
    z = np.zeros_like(sr)
    cos = np.concatenate([cr, cr, cc, cc], axis=1)
    s1 = np.concatenate([z, sr, z, sc], axis=1)
    s2 = np.concatenate([-sr, z, -sc, z], axis=1)
    return [jnp.asarray(a, F32) for a in (cos, s1, s2)]


def _my_place():
    return lax.axis_index("x"), lax.axis_index("y"), lax.axis_index("c")


def _all_gather(shards, n_gather):
    n_all = len(shards)
    nw = n_gather

    def body(*refs):
        ins, outs = refs[:n_all], refs[n_all:2 * n_all]
        send_sems, recv_sems, local_sems = refs[2 * n_all:]
        x, y, c = _my_place()
        me, sibling = (x, y, c), (x, y, 1 - c)
        chips = [(1 - x, y), (x, 1 - y), (1 - x, 1 - y)]

        def rows(w, px, py, pc):
            n = ins[w].shape[0]
            return outs[w].at[pl.ds(pl.multiple_of((4 * px + 2 * py + pc) * n, 16), n), :]

        def copy(w, k, block, to, src=None):
            return pltpu.make_async_remote_copy(
                src_ref=rows(w, *block) if src is None else src, dst_ref=rows(w, *block),
                send_sem=send_sems.at[w, k], recv_sem=recv_sems.at[w, k], device_id=to, device_id_type=MESH)

        mine = [pltpu.make_async_copy(ins[w], rows(w, *me), local_sems.at[w]) for w in range(n_all)]
        for cp in mine:
            cp.start()
        first = []
        for w in range(nw):
            first.append(copy(w, 0, me, sibling, src=ins[w]))
            first += [copy(w, 1 + j, me, (*chip, c), src=ins[w]) for j, chip in enumerate(chips)]
        for cp in first:
            cp.start()
        passed = []
        for j, chip in enumerate(chips):
            for w in range(nw):
                copy(w, 1 + j, (*chip, c), me).wait_recv()
                fwd = copy(w, 4 + j, (*chip, c), sibling)
                fwd.start()
                passed.append(fwd)
        for w in range(nw):
            copy(w, 0, sibling, me).wait_recv()
        for j, chip in enumerate(chips):
            for w in range(nw):
                copy(w, 4 + j, (*chip, 1 - c), me).wait_recv()
        for cp in first + passed:
            cp.wait_send()
        for cp in mine:
            cp.wait()

    any_spec = pl.BlockSpec(memory_space=pl.ANY)
    return pl.pallas_call(
        body, name="weights_all_gather",
        out_shape=[jax.ShapeDtypeStruct((N_DEV * s.shape[0], s.shape[1]), s.dtype) for s in shards],
        in_specs=[any_spec] * n_all, out_specs=[any_spec] * n_all,
        scratch_shapes=[pltpu.SemaphoreType.DMA((nw, 7)), pltpu.SemaphoreType.DMA((nw, 7)),
                        pltpu.SemaphoreType.DMA((n_all,))],
    )(*shards)


_FLIPS = [(fx, fy, fc) for fx in (0, 1) for fy in (0, 1) for fc in (0, 1)][1:]


def _small_all_gather(v):
    def body(v_ref, out_ref, send_sems, recv_sems):
        x, y, c = _my_place()
        my_idx = 4 * x + 2 * y + c
        out_ref[my_idx] = v_ref[...]
        sends = []
        for k, (fx, fy, fc) in enumerate(_FLIPS):
            to = (1 - x if fx else x, 1 - y if fy else y, 1 - c if fc else c)
            sends.append(pltpu.make_async_remote_copy(
                src_ref=v_ref, dst_ref=out_ref.at[my_idx], send_sem=send_sems.at[k], recv_sem=recv_sems.at[k],
                device_id=to, device_id_type=MESH))
        for cp in sends:
            cp.start()
        for k, (fx, fy, fc) in enumerate(_FLIPS):
            frm_idx = 4 * (1 - x if fx else x) + 2 * (1 - y if fy else y) + (1 - c if fc else c)
            pltpu.make_async_remote_copy(
                src_ref=v_ref, dst_ref=out_ref.at[frm_idx], send_sem=send_sems.at[k], recv_sem=recv_sems.at[k],
                device_id=(x, y, c), device_id_type=MESH).wait_recv()
        for cp in sends:
            cp.wait_send()

    vm = pl.BlockSpec(memory_space=pltpu.VMEM)
    return pl.pallas_call(
        body, name="small_all_gather", out_shape=jax.ShapeDtypeStruct((N_DEV,) + v.shape, v.dtype),
        in_specs=[vm], out_specs=vm,
        scratch_shapes=[pltpu.SemaphoreType.DMA((7,)), pltpu.SemaphoreType.DMA((7,))],
    )(v)


_HBM = pl.BlockSpec(memory_space=pltpu.HBM)
_SEM = pl.BlockSpec(memory_space=pltpu.SEMAPHORE)
_ANY = pl.BlockSpec(memory_space=pl.ANY)
_SPLIT_COPY = dict(has_side_effects=pltpu.SideEffectType.DATAFLOW_SIDE_EFFECTING)


def _peer(x, y, c, k):
    fx, fy, fc = _FLIPS[k]
    return (1 - x if fx else x, 1 - y if fy else y, 1 - c if fc else c)


def _in_hbm(a):
    return pltpu.with_memory_space_constraint(a, pltpu.HBM)


def _split_copies(srcs, lands, send_sems, recv_sems, gather, arriving):
    x, y, c = _my_place()
    my_idx = 4 * x + 2 * y + c
    out = []
    for k in range(7):
        to = _peer(x, y, c, k)
        to_idx = 4 * to[0] + 2 * to[1] + to[2]
        for w in range(len(srcs)):
            if gather:
                n = srcs[w].shape[0]
                src = srcs[w]
                dst = lands[w].at[pl.ds(pl.multiple_of((to_idx if arriving else my_idx) * n, 16), n), :]
            else:
                n = lands[w].shape[1]
                src = srcs[w].at[pl.ds(pl.multiple_of(to_idx * n, 16), n), :]
                dst = lands[w].at[k]
            out.append(pltpu.make_async_remote_copy(
                src_ref=src, dst_ref=dst, send_sem=send_sems.at[7 * w + k], recv_sem=recv_sems.at[7 * w + k],
                device_id=to, device_id_type=MESH))
    return out


def _copies_start(name, srcs, lands, after, gather):
    nw = len(srcs)

    def body(*refs):
        send_sems, recv_sems = refs[2 * nw + 1], refs[2 * nw + 2]
        for cp in _split_copies(refs[:nw], refs[nw:2 * nw], send_sems, recv_sems, gather, False):
            cp.start()
        refs[-1][...] = jnp.zeros_like(refs[-1])

    sems = pltpu.SemaphoreType.DMA((7 * nw,))
    thru = [pltpu.HBM(a.shape, a.dtype) for a in list(srcs) + list(lands)]
    res = pl.pallas_call(
        body, name=name, out_shape=(sems, sems, *thru, jax.ShapeDtypeStruct((8, LANES), F32)),
        in_specs=[_HBM] * (2 * nw) + [_ANY], out_specs=(_SEM, _SEM, *[_HBM] * (2 * nw), pl.BlockSpec(memory_space=pltpu.VMEM)),
        input_output_aliases={i: 2 + i for i in range(2 * nw)},
        compiler_params=pltpu.CompilerParams(**_SPLIT_COPY),
    )(*[_in_hbm(a) for a in srcs], *[_in_hbm(a) for a in lands], after)
    return res[0], res[1], list(res[2:2 + nw]), list(res[2 + nw:2 + 2 * nw]), res[-1]


def _copies_wait(name, send_sems, recv_sems, srcs, lands, after, gather):
    nw = len(srcs)

    def body(*refs):
        for cp in _split_copies(refs[:nw], refs[nw:2 * nw], refs[2 * nw], refs[2 * nw + 1], gather, False):
            cp.wait_send()
        for cp in _split_copies(refs[:nw], refs[nw:2 * nw], refs[2 * nw], refs[2 * nw + 1], gather, True):
            cp.wait_recv()

    thru = [pltpu.HBM(a.shape, a.dtype) for a in list(srcs) + list(lands)]
    res = pl.pallas_call(
        body, name=name, out_shape=tuple(thru),
        in_specs=[_HBM] * (2 * nw) + [_SEM, _SEM, _ANY], out_specs=tuple([_HBM] * (2 * nw)),
        input_output_aliases={i: i for i in range(2 * nw)},
        compiler_params=pltpu.CompilerParams(**_SPLIT_COPY),
    )(*srcs, *lands, send_sems, recv_sems, after)
    return list(res[:nw]), list(res[nw:])


def _in_proj(x, tabs, w_in_t, g_mix, b_gate, q_g, k_g, tb, after):
    s, d = x.shape
    n_gate_chunks = d // 256
    q_scale = HEAD_DIM_A ** -0.5 * LOG2_E
    b_scale = HEAD_DIM_B ** -0.5 * LOG2_E

    def body(x_ref, c_ref, s1_ref, s2_ref, w_ref, gmix_ref, bg_ref, qg_ref, kg_ref, after_ref,
             h1_ref, qraw_ref, kraw_ref, qrot_ref, krot_ref, va_ref, *rest):
        qb_refs, kb_refs, vb_refs = rest[0:3], rest[3:6], rest[6:9]
        ga_ref, gb_ref, scr_ref = rest[9:]
        xv = x_ref[...]
        hb = (xv * _rstd(xv) * gmix_ref[...]).astype(BF16)
        h1_ref[...] = hb
        cos, s1, s2 = c_ref[...], s1_ref[...], s2_ref[...]

        def proj(lo, width):
            return _dot_nt(hb, w_ref[lo:lo + width, :])

        def norm_rope(z, g):
            n = z * _rstd(z) * g
            return n * cos + pltpu.roll(n, 32, 1) * s1 + pltpu.roll(n, 96, 1) * s2

        for j in range(QA_W // 256):
            z = proj(OFF_QA + 256 * j, 256)
            qraw_ref[:, 256 * j:256 * j + 256] = z
            for hh in range(2):
                lo = 256 * j + 128 * hh
                qrot_ref[:, lo:lo + 128] = (norm_rope(z[:, 128 * hh:128 * hh + 128], qg_ref[...]) * q_scale).astype(BF16)
        z = proj(OFF_KA, 256)
        kraw_ref[...] = z
        for hh in range(2):
            krot_ref[:, 128 * hh:128 * hh + 128] = norm_rope(z[:, 128 * hh:128 * hh + 128], kg_ref[...]).astype(BF16)
        va_ref[...] = proj(OFF_VA, 256).astype(BF16)
        for g, dil in enumerate(DILATIONS):
            _to_residues(proj(OFF_QB + GB_W * g, GB_W) * b_scale, qb_refs[g], scr_ref, dil, BF16)
            _to_residues(proj(OFF_KB + GB_W * g, GB_W), kb_refs[g], scr_ref, dil, BF16)
            _to_residues(proj(OFF_VB + GB_W * g, GB_W), vb_refs[g], scr_ref, dil, BF16)
        for j in range(n_gate_chunks):
            sl = slice(256 * j, 256 * j + 256)
            ga_ref[:, sl] = _sigmoid(proj(OFF_GA + 256 * j, 256) + bg_ref[:, sl]).astype(BF16)
            gb_ref[:, sl] = _sigmoid(
                proj(OFF_GA + d + 256 * j, 256) + bg_ref[:, d + 256 * j:d + 256 * j + 256]).astype(BF16)

    sd = jax.ShapeDtypeStruct
    outs = [sd((s, d), BF16), sd((s, QA_W), F32), sd((s, KA_W), F32), sd((s, QA_W), BF16), sd((s, KA_W), BF16),
            sd((s, KA_W), BF16)] + _dil_shapes(s, BF16) * 3 + [sd((s, d), BF16), sd((s, d), BF16)]
    out_specs = [_rows(tb, d), _rows(tb, QA_W), _rows(tb, KA_W), _rows(tb, QA_W), _rows(tb, KA_W), _rows(tb, KA_W)
                 ] + _dil_specs(tb) * 3 + [_rows(tb, d), _rows(tb, d)]
    in_specs = [_rows(tb, d), _rows(tb, LANES), _rows(tb, LANES), _rows(tb, LANES), _resident(w_in_t.shape),
                _resident(g_mix.shape), _resident(b_gate.shape), _resident(q_g.shape), _resident(k_g.shape), _ANY]
    res = list(pl.pallas_call(body, name="in_proj", grid=(s // tb,), in_specs=in_specs, out_specs=out_specs,
                              out_shape=outs, scratch_shapes=[pltpu.VMEM((2, tb, LANES), F32)],
                              compiler_params=_cparams(("arbitrary",)))(
        x, *tabs, w_in_t, g_mix, b_gate, q_g, k_g, after))
    return res[:6] + [res[6:9], res[9:12], res[12:15]] + res[15:]


def _attn_a_fwd(qrot, krot, va, tq, tk):
    s = qrot.shape[0]
    n_kv = s // tk
    gw = Q_PER_KV * HEAD_DIM_A

    def body(q_ref, k_ref, v_ref, o_ref, lse_ref):
        q4 = jnp.concatenate([q_ref[:, 128 * h:128 * h + 128] for h in range(Q_PER_KV)], axis=0)

        def step(j, carry):
            m, l, acc = carry
            sl = pl.ds(pl.multiple_of(j * tk, tk), tk)
            kj, vj = k_ref[sl, :], v_ref[sl, :]
            sc = _dot_nt(kj, q4)
            m_new = jnp.maximum(m, jnp.max(sc, axis=0, keepdims=True))
            p = jnp.exp2(sc - m_new)
            alpha = jnp.exp2(m - m_new)
            l = alpha * l + jnp.sum(p, axis=0, keepdims=True)
            acc = alpha * acc + _dot_tn(vj, p.astype(BF16))
            return m_new, l, acc

        rows = Q_PER_KV * tq
        m, l, acc = lax.fori_loop(0, n_kv, step, (jnp.full((1, rows), NEG_INF, F32), jnp.zeros((1, rows), F32),
                                                  jnp.zeros((HEAD_DIM_A, rows), F32)))
        o = (acc / l).T
        lse = m + jnp.log2(l)
        for h in range(Q_PER_KV):
            o_ref[:, 128 * h:128 * h + 128] = o[h * tq:(h + 1) * tq].astype(BF16)
            lse_ref[0, h:h + 1, :] = lse[:, h * tq:(h + 1) * tq]

    return pl.pallas_call(
        body, name="attn_a_fwd", grid=(N_KV_HEADS_A, s // tq),
        in_specs=[pl.BlockSpec((tq, gw), lambda g, i: (i, g)),
                  pl.BlockSpec((s, HEAD_DIM_A), lambda g, i: (0, g)),
                  pl.BlockSpec((s, HEAD_DIM_A), lambda g, i: (0, g))],
        out_specs=[pl.BlockSpec((tq, gw), lambda g, i: (i, g)),
                   pl.BlockSpec((1, Q_PER_KV, tq), lambda g, i: (g, 0, i))],
        out_shape=[jax.ShapeDtypeStruct((s, QA_W), BF16), jax.ShapeDtypeStruct((N_KV_HEADS_A, Q_PER_KV, s), F32)],
        compiler_params=_cparams(("arbitrary", "arbitrary")))(qrot, krot, va)


def _attn_a_bwd(qrot, krot, va, oa, doa, lse, tq, tk, after):
    s = qrot.shape[0]
    n_kv = s // tk
    gw = Q_PER_KV * HEAD_DIM_A

    def body(q_ref, do_ref, o_ref, lse_ref, k_ref, v_ref, after_ref, dq_ref, dk_ref, dv_ref):
        @pl.when(pl.program_id(1) == 0)
        def _():
            dk_ref[...] = jnp.zeros_like(dk_ref)
            dv_ref[...] = jnp.zeros_like(dv_ref)

        def stack(ref):
            return jnp.concatenate([ref[:, 128 * h:128 * h + 128] for h in range(Q_PER_KV)], axis=0)

        q4, do4, o4 = stack(q_ref), stack(do_ref), stack(o_ref)
        q4t, do4t = q4.T, do4.T
        delta = jnp.sum((do4.astype(F32) * o4.astype(F32)).T, axis=0, keepdims=True)
        lse4 = jnp.concatenate([lse_ref[0, h:h + 1, :] for h in range(Q_PER_KV)], axis=1)

        def step(j, dq):
            sl = pl.ds(pl.multiple_of(j * tk, tk), tk)
            kj, vj = k_ref[sl, :], v_ref[sl, :]
            p = jnp.exp2(_dot_nt(kj, q4) - lse4)
            ds = (p * (_dot_nt(vj, do4) - delta)).astype(BF16)
            dk_ref[:, sl] += _dot_nt(q4t, ds)
            dv_ref[:, sl] += _dot_nt(do4t, p.astype(BF16))
            return dq + _dot_tn(kj, ds)

        dq = lax.fori_loop(0, n_kv, step, jnp.zeros((HEAD_DIM_A, Q_PER_KV * tq), F32)).T
        for h in range(Q_PER_KV):
            dq_ref[:, 128 * h:128 * h + 128] = dq[h * tq:(h + 1) * tq]

    qspec = pl.BlockSpec((tq, gw), lambda g, i: (i, g))
    kspec = pl.BlockSpec((s, HEAD_DIM_A), lambda g, i: (0, g))
    ktspec = pl.BlockSpec((HEAD_DIM_A, s), lambda g, i: (g, 0))
    return pl.pallas_call(
        body, name="attn_a_bwd", grid=(N_KV_HEADS_A, s // tq),
        in_specs=[qspec, qspec, qspec, pl.BlockSpec((1, Q_PER_KV, tq), lambda g, i: (g, 0, i)), kspec, kspec, _ANY],
        out_specs=[qspec, ktspec, ktspec],
        out_shape=[jax.ShapeDtypeStruct((s, QA_W), F32), jax.ShapeDtypeStruct((KA_W, s), F32),
                   jax.ShapeDtypeStruct((KA_W, s), F32)],
        compiler_params=_cparams(("arbitrary", "arbitrary")))(qrot, doa, oa, lse, krot, va, after)


BAND_QB = 128
BAND_WIN = BAND_QB + 2 * BAND


def _band_specs(s, cb):
    per = cb // BAND
    last = s // BAND - 1
    cur = pl.BlockSpec((cb, GB_W), lambda i: (i, 0))
    prev = pl.BlockSpec((BAND, GB_W), lambda i: (jnp.maximum(i * per - 1, 0), 0))
    nxt = pl.BlockSpec((BAND, GB_W), lambda i: (jnp.minimum(i * per + per, last), 0))
    return cur, prev, nxt


def _window(prev_ref, cur_ref, next_ref):
    return jnp.concatenate([prev_ref[...], cur_ref[...], next_ref[...]], axis=0)


def _band_mask(base, seg_shift, window_rows):
    shape = (BAND_WIN, BAND_QB) if window_rows else (BAND_QB, BAND_WIN)
    a = lax.broadcasted_iota(jnp.int32, shape, 0)
    b = lax.broadcasted_iota(jnp.int32, shape, 1)
    rq, rk = (base - BAND + a, base + b) if window_rows else (base + a, base - BAND + b)
    same_segment = lax.shift_right_arithmetic(rq, jnp.int32(seg_shift)) == lax.shift_right_arithmetic(rk, jnp.int32(seg_shift))
    return (jnp.abs(rk - rq) <= BAND) & same_segment


def _build_bias(bmap_ref, tab_ref, bias_ref):
    bm = bmap_ref[...]
    acc = [jnp.full(bm.shape, NEG_INF, F32) for _ in range(N_HEADS_PER_DIL)]
    for b in range(N_REL_BUCKETS):
        hit = bm == b
        for h in range(N_HEADS_PER_DIL):
            acc[h] = jnp.where(hit, tab_ref[b, h] * LOG2_E, acc[h])
    rows = bm.shape[0]
    for h in range(N_HEADS_PER_DIL):
        bias_ref[h * rows:(h + 1) * rows, :] = acc[h]


def _segment_mask(base, seg_len, seg_shift, window_rows):
    if seg_len % BAND_QB:
        return _band_mask(base, seg_shift, window_rows)
    pos = lax.rem(base, seg_len)
    shape, dim = ((BAND_WIN, 1), 0) if window_rows else ((1, BAND_WIN), 1)
    w = lax.broadcasted_iota(jnp.int32, shape, dim)
    return ((w >= BAND) | (pos != 0)) & ((w < BAND + BAND_QB) | (pos != seg_len - BAND_QB))


def _head_lane_masks():
    lane = lax.broadcasted_iota(jnp.int32, (1, LANES), 1)
    return [lane < HEAD_DIM_B, lane >= HEAD_DIM_B]


def _rows4(mask):
    return mask if mask.shape[0] == 1 else jnp.concatenate([mask] * N_HEADS_PER_DIL, axis=0)


def _head_scores(a, b):
    hm = _head_lane_masks()
    out = []
    for hp in range(2):
        ls = slice(LANES * hp, LANES * hp + LANES)
        ah = a[:, ls]
        both = jnp.concatenate([jnp.where(hm[0], ah, jnp.zeros_like(ah)), jnp.where(hm[1], ah, jnp.zeros_like(ah))],
                               axis=0)
        out.append(_dot_nt(both, b[:, ls]))
    return jnp.concatenate(out, axis=0)


def _head_combine(p, v, scale=None, transposed=False):
    hm = _head_lane_masks()
    rows = p.shape[0] // N_HEADS_PER_DIL
    halves = []
    for hp in range(2):
        vh = v[:, LANES * hp:LANES * hp + LANES]
        acc = None
        for hh in range(2):
            h = 2 * hp + hh
            ph = p[h * rows:(h + 1) * rows]
            vm = jnp.where(hm[hh], vh, jnp.zeros_like(vh))
            t = _dot_tn(ph, vm) if transposed else _dot_nn(ph, vm)
            if scale is not None:
                t = t * scale[h * rows:(h + 1) * rows]
            acc = t if acc is None else acc + t
        halves.append(acc)
    return jnp.concatenate(halves, axis=1)


def _head_spread(col):
    rows = col.shape[0] // N_HEADS_PER_DIL
    lane = lax.broadcasted_iota(jnp.int32, (1, GB_W), 1)
    out = jnp.zeros((rows, GB_W), F32)
    for h in range(N_HEADS_PER_DIL):
        out = jnp.where((lane >= HEAD_DIM_B * h) & (lane < HEAD_DIM_B * (h + 1)), col[h * rows:(h + 1) * rows], out)
    return out


def _head_cols(v):
    return jnp.concatenate([v[:, HEAD_DIM_B * h:HEAD_DIM_B * h + 1] for h in range(N_HEADS_PER_DIL)], axis=0)


def _seg_shift(s, dil):
    seg = s // dil
    assert seg & (seg - 1) == 0, "segment length must be a power of two"
    return seg.bit_length() - 1


def _band_fwd(dil, qb, kb, vb, bmap, tab, cb):
    s = qb.shape[0]
    shift = _seg_shift(s, dil)

    def body(q_ref, kp_ref, kc_ref, kn_ref, vp_ref, vc_ref, vn_ref, bmap_ref, tab_ref, o_ref, lse_ref, bias_ref):
        @pl.when(pl.program_id(0) == 0)
        def _():
            _build_bias(bmap_ref, tab_ref, bias_ref)

        kw, vw = _window(kp_ref, kc_ref, kn_ref), _window(vp_ref, vc_ref, vn_ref)
        for jj in range(cb // BAND_QB):
            r0 = BAND_QB * jj
            mask = _rows4(_segment_mask(pl.program_id(0) * cb + r0, s // dil, shift, False))
            sc = _head_scores(q_ref[r0:r0 + BAND_QB, :], kw[r0:r0 + BAND_WIN, :]) + bias_ref[...]
            sc = jnp.where(mask, sc, NEG_INF)
            m = jnp.max(sc, axis=-1, keepdims=True)
            e = jnp.exp2(sc - m)
            l = jnp.sum(e, axis=-1, keepdims=True)
            o = _head_combine(e.astype(BF16), vw[r0:r0 + BAND_WIN, :], 1.0 / l)
            o_ref[r0:r0 + BAND_QB, :] = o
            lse_ref[r0:r0 + BAND_QB, :] = _head_spread(m + jnp.log2(l))

    cur, prev, nxt = _band_specs(s, cb)
    return pl.pallas_call(
        body, name=f"band_fwd_d{dil}", grid=(s // cb,),
        in_specs=[cur, prev, cur, nxt, prev, cur, nxt, _resident(bmap.shape), pl.BlockSpec(memory_space=pltpu.SMEM)],
        out_specs=[cur, cur],
        out_shape=[jax.ShapeDtypeStruct(qb.shape, F32), jax.ShapeDtypeStruct(qb.shape, F32)],
        scratch_shapes=[pltpu.VMEM((N_HEADS_PER_DIL * BAND_QB, BAND_WIN), F32)],
        compiler_params=_cparams(("arbitrary",)))(qb, kb, kb, kb, vb, vb, vb, bmap, tab)


def _band_bwd_q(dil, qb, kb, vb, dob, lse, dd, bmap, tab, cb):
    s = qb.shape[0]
    shift = _seg_shift(s, dil)
    n_steps = s // cb

    def body(q_ref, do_ref, lse_ref, dd_ref, kp_ref, kc_ref, kn_ref, vp_ref, vc_ref, vn_ref, bmap_ref, tab_ref,
             dq_ref, dtab_ref, bias_ref, dsum_ref):
        @pl.when(pl.program_id(0) == 0)
        def _():
            _build_bias(bmap_ref, tab_ref, bias_ref)
            dsum_ref[...] = jnp.zeros_like(dsum_ref)

        kw, vw = _window(kp_ref, kc_ref, kn_ref), _window(vp_ref, vc_ref, vn_ref)
        for jj in range(cb // BAND_QB):
            r0 = BAND_QB * jj
            mask = _rows4(_segment_mask(pl.program_id(0) * cb + r0, s // dil, shift, False))
            k3, v3 = kw[r0:r0 + BAND_WIN, :], vw[r0:r0 + BAND_WIN, :]
            sc = _head_scores(q_ref[r0:r0 + BAND_QB, :], k3) + bias_ref[...]
            sc = jnp.where(mask, sc, NEG_INF)
            p = jnp.exp2(sc - _head_cols(lse_ref[r0:r0 + BAND_QB, :]))
            dp = _head_scores(do_ref[r0:r0 + BAND_QB, :], v3)
            ds = p * (dp - _head_cols(dd_ref[r0:r0 + BAND_QB, :]))
            dsum_ref[...] += ds
            dq_ref[r0:r0 + BAND_QB, :] = _head_combine(ds.astype(BF16), k3)

        @pl.when(pl.program_id(0) == n_steps - 1)
        def _():
            bm = bmap_ref[...]
            lane = lax.broadcasted_iota(jnp.int32, (1, LANES), 1)
            for b in range(N_REL_BUCKETS):
                hit = bm == b
                row = jnp.zeros((1, LANES), F32)
                for h in range(N_HEADS_PER_DIL):
                    part = dsum_ref[h * BAND_QB:(h + 1) * BAND_QB, :]
                    row = jnp.where(lane == h, jnp.sum(jnp.where(hit, part, 0.0)), row)
                dtab_ref[b:b + 1, :] = row

    cur, prev, nxt = _band_specs(s, cb)
    return pl.pallas_call(
        body, name=f"band_bwd_q_d{dil}", grid=(n_steps,),
        in_specs=[cur, cur, cur, cur, prev, cur, nxt, prev, cur, nxt, _resident(bmap.shape),
                  pl.BlockSpec(memory_space=pltpu.SMEM)],
        out_specs=[cur, _acc_spec((N_REL_BUCKETS, LANES))],
        out_shape=[jax.ShapeDtypeStruct(qb.shape, F32), jax.ShapeDtypeStruct((N_REL_BUCKETS, LANES), F32)],
        scratch_shapes=[pltpu.VMEM((N_HEADS_PER_DIL * BAND_QB, BAND_WIN), F32),
                        pltpu.VMEM((N_HEADS_PER_DIL * BAND_QB, BAND_WIN), F32)],
        compiler_params=_cparams(("arbitrary",)))(qb, dob, lse, dd, kb, kb, kb, vb, vb, vb, bmap, tab)


def _band_bwd_kv(dil, qb, kb, vb, dob, lse, dd, bmap_t, tab, cb):
    s = qb.shape[0]
    shift = _seg_shift(s, dil)

    def body(k_ref, v_ref, qp_ref, qc_ref, qn_ref, dp_ref, dc_ref, dn_ref, lp_ref, lc_ref, ln_ref,
             ep_ref, ec_ref, en_ref, bmap_ref, tab_ref, dk_ref, dv_ref, bias_ref):
        @pl.when(pl.program_id(0) == 0)
        def _():
            _build_bias(bmap_ref, tab_ref, bias_ref)

        qw, dow = _window(qp_ref, qc_ref, qn_ref), _window(dp_ref, dc_ref, dn_ref)
        lw, ew = _window(lp_ref, lc_ref, ln_ref), _window(ep_ref, ec_ref, en_ref)
        for jj in range(cb // BAND_QB):
            r0 = BAND_QB * jj
            mask = _rows4(_segment_mask(pl.program_id(0) * cb + r0, s // dil, shift, True))
            q3, do3 = qw[r0:r0 + BAND_WIN, :], dow[r0:r0 + BAND_WIN, :]
            sc = _head_scores(q3, k_ref[r0:r0 + BAND_QB, :]) + bias_ref[...]
            sc = jnp.where(mask, sc, NEG_INF)
            p = jnp.exp2(sc - _head_cols(lw[r0:r0 + BAND_WIN, :]))
            ds = p * (_head_scores(do3, v_ref[r0:r0 + BAND_QB, :]) - _head_cols(ew[r0:r0 + BAND_WIN, :]))
            dk_ref[r0:r0 + BAND_QB, :] = _head_combine(ds.astype(BF16), q3, transposed=True)
            dv_ref[r0:r0 + BAND_QB, :] = _head_combine(p.astype(BF16), do3, transposed=True)

    cur, prev, nxt = _band_specs(s, cb)
    win = [prev, cur, nxt]
    return pl.pallas_call(
        body, name=f"band_bwd_kv_d{dil}", grid=(s // cb,),
        in_specs=[cur, cur] + win * 4 + [_resident(bmap_t.shape), pl.BlockSpec(memory_space=pltpu.SMEM)],
        out_specs=[cur, cur],
        out_shape=[jax.ShapeDtypeStruct(qb.shape, F32), jax.ShapeDtypeStruct(qb.shape, F32)],
        scratch_shapes=[pltpu.VMEM((N_HEADS_PER_DIL * BAND_WIN, BAND_QB), F32)],
        compiler_params=_cparams(("arbitrary",)))(
        kb, vb, qb, qb, qb, dob, dob, dob, lse, lse, lse, dd, dd, dd, bmap_t, tab)


def _t5_bucket(rel):
    nb = N_REL_BUCKETS // 2
    ret = (rel > 0).astype(np.int32) * nb
    n = np.abs(rel)
    max_exact = nb // 2
    large = max_exact + (np.log(np.maximum(n, 1) / max_exact) / math.log(REL_MAX_DIST / max_exact)
                         * (nb - max_exact)).astype(np.int32)
    large = np.minimum(large, nb - 1)
    return ret + np.where(n < max_exact, n, large).astype(np.int32)


def _bucket_maps(dil):
    off_qk = np.arange(BAND_WIN)[None, :] - BAND - np.arange(BAND_QB)[:, None]
    off_kq = np.arange(BAND_QB)[None, :] + BAND - np.arange(BAND_WIN)[:, None]
    return [np.where(np.abs(off) <= BAND, _t5_bucket(off * dil), -1).astype(np.int32) for off in (off_qk, off_kq)]


def _seg_sum(v):
    lane = lax.broadcasted_iota(jnp.int32, (1, v.shape[1]), 1)
    out = jnp.zeros_like(v)
    for h in range(v.shape[1] // HEAD_DIM_B):
        m = (lane >= HEAD_DIM_B * h) & (lane < HEAD_DIM_B * (h + 1))
        out = jnp.where(m, jnp.sum(jnp.where(m, v, 0.0), axis=-1, keepdims=True), out)
    return out


def _mix_out(x, oa, og, lg, ga, gb, w_oa, w_ob_t, w_o, tb):
    s, d = x.shape

    def body(x_ref, oa_ref, og0_ref, og1_ref, og2_ref, lg0_ref, lg1_ref, lg2_ref, ga_ref, gb_ref,
             woa_ref, wob_ref, wo_ref, x2_ref, ob_ref, lse0_ref, lse1_ref, lse2_ref, ya_ref, yb_ref, u_ref, scr_ref):
        og_refs, lg_refs = (og0_ref, og1_ref, og2_ref), (lg0_ref, lg1_ref, lg2_ref)
        l0, l1, l2 = [_from_residues(lg_refs[g], scr_ref, dil) for g, dil in enumerate(DILATIONS)]
        lmax = jnp.maximum(jnp.maximum(l0, l1), l2)
        w0, w1, w2 = jnp.exp2(l0 - lmax), jnp.exp2(l1 - lmax), jnp.exp2(l2 - lmax)
        den = w0 + w1 + w2
        o0, o1, o2 = [_from_residues(og_refs[g], scr_ref, dil) for g, dil in enumerate(DILATIONS)]
        ob = ((w0 * o0 + w1 * o1 + w2 * o2) / den).astype(BF16)
        ob_ref[...] = ob
        lse = lmax + jnp.log2(den)
        for g, (dil, ref) in enumerate(zip(DILATIONS, (lse0_ref, lse1_ref, lse2_ref))):
            _to_residues(lse, ref, scr_ref, dil, F32)
        ya = _dot_nn(oa_ref[...], woa_ref[...])
        yb = _dot_nt(ob, wob_ref[...])
        ya_ref[...] = ya.astype(BF16)
        yb_ref[...] = yb.astype(BF16)
        u = (ga_ref[...].astype(F32) * ya + gb_ref[...].astype(F32) * yb).astype(BF16)
        u_ref[...] = u
        x2_ref[...] = x_ref[...] + _dot_nn(u, wo_ref[...])

    sd = jax.ShapeDtypeStruct
    res = list(pl.pallas_call(
        body, name="mix_out", grid=(s // tb,),
        in_specs=[_rows(tb, d), _rows(tb, QA_W)] + _dil_specs(tb) * 2 + [
            _rows(tb, d), _rows(tb, d), _resident(w_oa.shape), _resident(w_ob_t.shape), _resident(w_o.shape)],
        out_specs=[_rows(tb, d), _rows(tb, GB_W)] + _dil_specs(tb) + [_rows(tb, d), _rows(tb, d), _rows(tb, d)],
        out_shape=[sd((s, d), F32), sd((s, GB_W), BF16)] + _dil_shapes(s, F32) + [
            sd((s, d), BF16), sd((s, d), BF16), sd((s, d), BF16)],
        scratch_shapes=[pltpu.VMEM((2, tb, LANES), F32)],
        compiler_params=_cparams(("arbitrary",)))(x, oa, *og, *lg, ga, gb, w_oa, w_ob_t, w_o))
    return res[:2] + [res[2:5]] + res[5:]


def _mlp_fwd(x2, w1_t, w2, g_mlp, tb, tc):
    s, d = x2.shape
    dff = w1_t.shape[0]

    def body(x_ref, w1_ref, w2_ref, g_ref, x3_ref, r_ref, h_ref):
        xv = x_ref[...]
        hb = (xv * _rstd(xv) * g_ref[...]).astype(BF16)
        h_ref[...] = hb
        x3_ref[...] = xv
        for c in range(dff // tc):
            sl = slice(tc * c, tc * c + tc)
            r = jnp.maximum(_dot_nt(hb, w1_ref[sl, :]), 0.0)
            r_ref[:, sl] = r.astype(BF16)
            x3_ref[...] += _dot_nn((r * r).astype(BF16), w2_ref[sl, :])

    sd = jax.ShapeDtypeStruct
    return pl.pallas_call(
        body, name="mlp_fwd", grid=(s // tb,),
        in_specs=[_rows(tb, d), _resident(w1_t.shape), _resident(w2.shape), _resident(g_mlp.shape)],
        out_specs=[_rows(tb, d), _rows(tb, dff), _rows(tb, d)],
        out_shape=[sd((s, d), F32), sd((s, dff), BF16), sd((s, d), BF16)],
        compiler_params=_cparams(("arbitrary",)))(x2, w1_t, w2, g_mlp)


def _ple_loss(x3, p, target, w_pg, w_p_t, g_ple, g_fin, tb):
    s, d = x3.shape
    dp = p.shape[1]

    def body(x_ref, p_ref, t_ref, wpg_ref, wp_ref, gple_ref, gfin_ref,
             dx3_ref, h3_ref, dpre_ref, dpe_ref, pb_ref, loss_ref, dgfin_ref, dgple_ref):
        @pl.when(pl.program_id(0) == 0)
        def _():
            loss_ref[...] = jnp.zeros_like(loss_ref)
            dgfin_ref[...] = jnp.zeros_like(dgfin_ref)
            dgple_ref[...] = jnp.zeros_like(dgple_ref)

        x3v = x_ref[...]
        r3 = _rstd(x3v)
        n3 = x3v * r3
        h3 = (n3 * gple_ref[...]).astype(BF16)
        h3_ref[...] = h3
        gp = _sigmoid(_dot_nn(h3, wpg_ref[...]))
        pb = p_ref[...].astype(BF16)
        pb_ref[...] = pb
        pe = _dot_nt(pb, wp_ref[...])
        x4 = x3v + gp * pe
        r4 = _rstd(x4)
        n4 = x4 * r4
        err = n4 * gfin_ref[...] - t_ref[...]
        loss_ref[...] += jnp.sum(0.5 * jnp.mean(err * err, axis=-1, keepdims=True), axis=0, keepdims=True)
        dy = err / d
        dgfin_ref[...] += _colsum(dy * n4)
        dx4 = _rms_bwd(dy, n4, r4, gfin_ref[...])
        dpe_ref[...] = (dx4 * gp).astype(BF16)
        dpre = (dx4 * pe * gp * (1.0 - gp)).astype(BF16)
        dpre_ref[...] = dpre
        dh3 = _dot_nt(dpre, wpg_ref[...])
        dgple_ref[...] += _colsum(dh3 * n3)
        dx3_ref[...] = dx4 + _rms_bwd(dh3, n3, r3, gple_ref[...])

    sd = jax.ShapeDtypeStruct
    return pl.pallas_call(
        body, name="ple_loss", grid=(s // tb,),
        in_specs=[_rows(tb, d), _rows(tb, dp), _rows(tb, d), _resident(w_pg.shape), _resident(w_p_t.shape),
                  _resident(g_ple.shape), _resident(g_fin.shape)],
        out_specs=[_rows(tb, d), _rows(tb, d), _rows(tb, d), _rows(tb, d), _rows(tb, dp),
                   _acc_spec((1, LANES)), _acc_spec((1, d)), _acc_spec((1, d))],
        out_shape=[sd((s, d), F32), sd((s, d), BF16), sd((s, d), BF16), sd((s, d), BF16), sd((s, dp), BF16),
                   sd((1, LANES), F32), sd((1, d), F32), sd((1, d), F32)],
        compiler_params=_cparams(("arbitrary",)))(x3, p, target, w_pg, w_p_t, g_ple, g_fin)


def _mlp_bwd(dx3, x2, r, w1_t, w2, g_mlp, tb, tc):
    s, d = x2.shape
    dff = w1_t.shape[0]

    def body(dx3_ref, x_ref, r_ref, w1_ref, w2_ref, g_ref, dx2_ref, df_ref, dg_ref, dh_ref):
        @pl.when(pl.program_id(0) == 0)
        def _():
            dg_ref[...] = jnp.zeros_like(dg_ref)

        dx3v = dx3_ref[...]
        dx3b = dx3v.astype(BF16)
        dh_ref[...] = jnp.zeros_like(dh_ref)
        for c in range(dff // tc):
            sl = slice(tc * c, tc * c + tc)
            df = (_dot_nt(dx3b, w2_ref[sl, :]) * (2.0 * r_ref[:, sl].astype(F32))).astype(BF16)
            df_ref[:, sl] = df
            dh_ref[...] += _dot_nn(df, w1_ref[sl, :])
        xv = x_ref[...]
        r2 = _rstd(xv)
        n2 = xv * r2
        dh = dh_ref[...]
        dg_ref[...] += _colsum(dh * n2)
        dx2_ref[...] = dx3v + _rms_bwd(dh, n2, r2, g_ref[...])

    sd = jax.ShapeDtypeStruct
    return pl.pallas_call(
        body, name="mlp_bwd", grid=(s // tb,),
        in_specs=[_rows(tb, d), _rows(tb, d), _rows(tb, dff), _resident(w1_t.shape), _resident(w2.shape),
                  _resident(g_mlp.shape)],
        out_specs=[_rows(tb, d), _rows(tb, dff), _acc_spec((1, d))],
        out_shape=[sd((s, d), F32), sd((s, dff), BF16), sd((1, d), F32)],
        scratch_shapes=[pltpu.VMEM((tb, d), F32)],
        compiler_params=_cparams(("arbitrary",)))(dx3, x2, r, w1_t, w2, g_mlp)


def _mix_out_bwd(dx2, ya, yb, ga, gb, ob, w_oa, w_ob_t, w_o, tb, after):
    s, d = dx2.shape

    def body(dx_ref, ya_ref, yb_ref, ga_ref, gb_ref, ob_ref, woa_ref, wob_ref, wo_ref, after_ref,
             doa_ref, dob0_ref, dob1_ref, dob2_ref, dd0_ref, dd1_ref, dd2_ref, dga_ref, dgb_ref, dya_ref, dyb_ref,
             dbg_ref, scr_ref):
        @pl.when(pl.program_id(0) == 0)
        def _():
            dbg_ref[...] = jnp.zeros_like(dbg_ref)

        du = _dot_nt(dx_ref[...].astype(BF16), wo_ref[...])
        gav, gbv = ga_ref[...].astype(F32), gb_ref[...].astype(F32)
        dya = (du * gav).astype(BF16)
        dyb = (du * gbv).astype(BF16)
        dya_ref[...] = dya
        dyb_ref[...] = dyb
        dga = du * ya_ref[...].astype(F32) * gav * (1.0 - gav)
        dgb = du * yb_ref[...].astype(F32) * gbv * (1.0 - gbv)
        dga_ref[...] = dga.astype(BF16)
        dgb_ref[...] = dgb.astype(BF16)
        dbg_ref[:, 0:d] += _colsum(dga)
        dbg_ref[:, d:2 * d] += _colsum(dgb)
        doa_ref[...] = _dot_nt(dya, woa_ref[...]).astype(BF16)
        dob = _dot_nn(dyb, wob_ref[...])
        dd = _seg_sum(dob * ob_ref[...].astype(F32))
        for dil, dob_ref, dd_ref in zip(DILATIONS, (dob0_ref, dob1_ref, dob2_ref), (dd0_ref, dd1_ref, dd2_ref)):
            _to_residues(dob, dob_ref, scr_ref, dil, BF16)
            _to_residues(dd, dd_ref, scr_ref, dil, F32)

    sd = jax.ShapeDtypeStruct
    res = list(pl.pallas_call(
        body, name="mix_out_bwd", grid=(s // tb,),
        in_specs=[_rows(tb, d)] * 5 + [_rows(tb, GB_W), _resident(w_oa.shape), _resident(w_ob_t.shape),
                                       _resident(w_o.shape), _ANY],
        out_specs=[_rows(tb, QA_W)] + _dil_specs(tb) * 2 + [_rows(tb, d), _rows(tb, d), _rows(tb, d),
                                                           _rows(tb, d), _acc_spec((1, 2 * d))],
        out_shape=[sd((s, QA_W), BF16)] + _dil_shapes(s, BF16) + _dil_shapes(s, F32) + [
            sd((s, d), BF16), sd((s, d), BF16), sd((s, d), BF16), sd((s, d), BF16), sd((1, 2 * d), F32)],
        scratch_shapes=[pltpu.VMEM((2, tb, LANES), F32)],
        compiler_params=_cparams(("arbitrary",)))(dx2, ya, yb, ga, gb, ob, w_oa, w_ob_t, w_o, after))
    return res[:1] + [res[1:4], res[4:7]] + res[7:]


def _in_proj_bwd(dx2, x, dqrot, dkrot, dva, qraw, kraw, tabs, dqb, dkb, dvb, dga, dgb, w_in_t, g_mix, q_g, k_g, tb):
    s, d = x.shape
    din = w_in_t.shape[0]
    q_scale = HEAD_DIM_A ** -0.5
    b_scale = HEAD_DIM_B ** -0.5
    tc = 256

    def body(dx2_ref, x_ref, dq_ref, dk_ref, dv_ref, qraw_ref, kraw_ref, c_ref, s1_ref, s2_ref, *rest):
        dqb_refs, dkb_refs, dvb_refs = rest[0:3], rest[3:6], rest[6:9]
        (dga_ref, dgb_ref, w_ref, gmix_ref, qg_ref, kg_ref,
         dx_ref, dz_ref, dgmix_ref, dqg_ref, dkg_ref, dh_ref, scr_ref) = rest[9:]

        @pl.when(pl.program_id(0) == 0)
        def _():
            dgmix_ref[...] = jnp.zeros_like(dgmix_ref)
            dqg_ref[...] = jnp.zeros_like(dqg_ref)
            dkg_ref[...] = jnp.zeros_like(dkg_ref)

        cos, s1, s2 = c_ref[...][None], s1_ref[...][None], s2_ref[...][None]

        def heads_bwd(drot, z, g_ref, acc_ref):
            dn = drot * cos + pltpu.roll(drot * s1, 96, 2) + pltpu.roll(drot * s2, 32, 2)
            rr = _rstd(z)
            nn = z * rr
            acc_ref[...] += jnp.sum(jnp.sum(dn * nn, axis=0), axis=0, keepdims=True)
            return _rms_bwd(dn, nn, rr, g_ref[...][None]).astype(BF16)

        dh_ref[...] = jnp.zeros_like(dh_ref)

        def emit(off, piece):
            dz_ref[:, off:off + tc] = piece
            dh_ref[...] += _dot_nn(piece, w_ref[off:off + tc, :])

        for j in range(d // tc):
            emit(OFF_GA + tc * j, dga_ref[:, tc * j:tc * j + tc])
            emit(OFF_GA + d + tc * j, dgb_ref[:, tc * j:tc * j + tc])
        emit(OFF_VA, dv_ref[...].T.astype(BF16))
        for g, dil in enumerate(DILATIONS):
            emit(OFF_QB + GB_W * g, (_from_residues(dqb_refs[g], scr_ref, dil) * b_scale).astype(BF16))
            emit(OFF_KB + GB_W * g, (_from_residues(dkb_refs[g], scr_ref, dil) * LN_2).astype(BF16))
            emit(OFF_VB + GB_W * g, _from_residues(dvb_refs[g], scr_ref, dil).astype(BF16))
        stack = lambda ref, n: jnp.stack([ref[:, 128 * h:128 * h + 128] for h in range(n)], axis=0)
        dzq = heads_bwd(stack(dq_ref, N_Q_HEADS_A) * q_scale, stack(qraw_ref, N_Q_HEADS_A), qg_ref, dqg_ref)
        dkt = jnp.stack([dk_ref[128 * h:128 * h + 128, :].T for h in range(N_KV_HEADS_A)], axis=0) * LN_2
        dzk = heads_bwd(dkt, stack(kraw_ref, N_KV_HEADS_A), kg_ref, dkg_ref)
        for j in range(N_Q_HEADS_A // 2):
            emit(OFF_QA + tc * j, jnp.concatenate([dzq[2 * j], dzq[2 * j + 1]], axis=1))
        emit(OFF_KA, jnp.concatenate([dzk[0], dzk[1]], axis=1))
        xv = x_ref[...]
        r1 = _rstd(xv)
        n1 = xv * r1
        dh = dh_ref[...]
        dgmix_ref[...] += _colsum(dh * n1)
        dx_ref[...] = dx2_ref[...] + _rms_bwd(dh, n1, r1, gmix_ref[...])

    sd = jax.ShapeDtypeStruct
    return pl.pallas_call(
        body, name="in_proj_bwd", grid=(s // tb,),
        in_specs=[_rows(tb, d), _rows(tb, d), _rows(tb, QA_W), pl.BlockSpec((KA_W, tb), lambda i: (0, i)),
                  pl.BlockSpec((KA_W, tb), lambda i: (0, i)), _rows(tb, QA_W),
                  _rows(tb, KA_W), _rows(tb, LANES), _rows(tb, LANES), _rows(tb, LANES),
                  ] + _dil_specs(tb) * 3 + [_rows(tb, d), _rows(tb, d),
                  _resident(w_in_t.shape), _resident(g_mix.shape), _resident(q_g.shape), _resident(k_g.shape)],
        out_specs=[_rows(tb, d), _rows(tb, din), _acc_spec((1, d)), _acc_spec((1, HEAD_DIM_A)),
                   _acc_spec((1, HEAD_DIM_A))],
        out_shape=[sd((s, d), F32), sd((s, din), BF16), sd((1, d), F32), sd((1, HEAD_DIM_A), F32),
                   sd((1, HEAD_DIM_A), F32)],
        scratch_shapes=[pltpu.VMEM((tb, d), F32), pltpu.VMEM((2, tb, LANES), F32)],
        compiler_params=_cparams(("arbitrary",)))(
        dx2, x, dqrot, dkrot, dva, qraw, kraw, *tabs, *dqb, *dkb, *dvb, dga, dgb, w_in_t, g_mix, q_g, k_g)


def _identity(v):
    return v


def _to_bf16(v):
    return v.astype(BF16)


def _square_bf16(v):
    vf = v.astype(F32)
    return (vf * vf).astype(BF16)


def _weight_grad(name, a, b, ti, tj, tk, a_fn=_identity, b_fn=_identity, col0=0, n=None, after=None):
    t, m = a.shape
    n = b.shape[1] if n is None else n
    n_k = t // tk
    after = a if after is None else after

    def body(a_ref, b_ref, after_ref, o_ref, acc_ref):
        k = pl.program_id(2)

        @pl.when(k == 0)
        def _():
            acc_ref[...] = jnp.zeros_like(acc_ref)

        acc_ref[...] += _dot_tn(a_fn(a_ref[...]), b_fn(b_ref[...]))

        @pl.when(k == n_k - 1)
        def _():
            o_ref[...] = acc_ref[...].astype(BF16)

    return pl.pallas_call(
        body, name=name, grid=(m // ti, n // tj, n_k),
        in_specs=[pl.BlockSpec((tk, ti), lambda i, j, k: (k, i)),
                  pl.BlockSpec((tk, tj), lambda i, j, k: (k, j + col0 // tj)), _ANY],
        out_specs=pl.BlockSpec((ti, tj), lambda i, j, k: (i, j)),
        out_shape=jax.ShapeDtypeStruct((m, n), BF16),
        scratch_shapes=[pltpu.VMEM((ti, tj), F32)],
        compiler_params=_cparams(("arbitrary", "arbitrary", "arbitrary")))(a, b, after)


def _sum_slots(name, recv, own):
    m, n, k = recv.shape
    tc = min(k, 256)

    def body(own_ref, r_ref, o_ref):
        acc = own_ref[...].astype(F32)
        for i in range(m):
            acc = acc + r_ref[i].astype(F32)
        o_ref[...] = acc

    return pl.pallas_call(
        body, name=name, grid=(k // tc,),
        in_specs=[pl.BlockSpec((n, tc), lambda j: (0, j)), pl.BlockSpec((m, n, tc), lambda j: (0, 0, j))],
        out_specs=pl.BlockSpec((n, tc), lambda j: (0, j)),
        out_shape=jax.ShapeDtypeStruct((n, k), F32),
        compiler_params=_cparams(("arbitrary",)))(own, recv)


def _adamw_math(w, g, m, v):
    m = ADAM_B1 * m + (1.0 - ADAM_B1) * g
    v = ADAM_B2 * v + (1.0 - ADAM_B2) * (g * g)
    m_hat = m / (1.0 - ADAM_B1 ** ADAM_STEP)
    v_hat = v / (1.0 - ADAM_B2 ** ADAM_STEP)
    delta = -ADAM_LR * (m_hat / (jnp.sqrt(v_hat) + ADAM_EPS) + ADAM_WD * w)
    return delta, m, v


def _adamw(name, w, g, m, v):
    r, c = w.shape
    tr = min(r, 256)

    def body(w_ref, g_ref, m_ref, v_ref, d_ref, mo_ref, vo_ref):
        d_ref[...], mo_ref[...], vo_ref[...] = _adamw_math(w_ref[...], g_ref[...], m_ref[...], v_ref[...])

    spec = pl.BlockSpec((tr, c), lambda i: (i, 0))
    return pl.pallas_call(
        body, name=name, grid=(r // tr,), in_specs=[spec] * 4, out_specs=[spec] * 3,
        out_shape=[jax.ShapeDtypeStruct((r, c), F32)] * 3,
        compiler_params=_cparams(("arbitrary",)))(w, g, m, v)


def _small_update(parts, w, m, v):
    def body(p_ref, w_ref, m_ref, v_ref, g_ref, d_ref, mo_ref, vo_ref):
        g = p_ref[0]
        for i in range(1, N_DEV):
            g = g + p_ref[i]
        g_ref[...] = g
        d_ref[...], mo_ref[...], vo_ref[...] = _adamw_math(w_ref[...], g, m_ref[...], v_ref[...])

    return pl.pallas_call(body, name="small_update", out_shape=[jax.ShapeDtypeStruct(w.shape, F32)] * 4)(
        parts, w, m, v)


def _pack_rows(vectors, n_rows):
    flat = jnp.concatenate([v.reshape(-1).astype(F32) for v in vectors])
    flat = jnp.pad(flat, (0, n_rows * LANES - flat.shape[0]))
    return flat.reshape(n_rows, LANES)


def _pick_tile(n, prefs):
    for t in prefs:
        if n % t == 0:
            return t
    return n


def kernel(x, p, norm_mix_g, w_in, b_gate, q_norm_g, k_norm_g, rel_bias, w_out_a, w_out_b, w_out, norm_mlp_g, w_ff1, w_ff2, norm_ple_g, w_ple_gate, w_ple, final_norm_g, loss_target, m_norm_mix_g, m_w_in, m_b_gate, m_q_norm_g, m_k_norm_g, m_rel_bias, m_w_out_a, m_w_out_b, m_w_out, m_norm_mlp_g, m_w_ff1, m_w_ff2, m_norm_ple_g, m_w_ple_gate, m_w_ple, m_final_norm_g, v_norm_mix_g, v_w_in, v_b_gate, v_q_norm_g, v_k_norm_g, v_rel_bias, v_w_out_a, v_w_out_b, v_w_out, v_norm_mlp_g, v_w_ff1, v_w_ff2, v_norm_ple_g, v_w_ple_gate, v_w_ple, v_final_norm_g):
    s, d = x.shape[1], x.shape[2]
    xs, ps, ts = x[0], p[0, 0], loss_target[0]
    tb = _pick_tile(s, (512, 256))
    tq = _pick_tile(s, (256,))
    tk = _pick_tile(s, (1024, 512))
    cb = _pick_tile(s, (1024, 512))
    fin_g = final_norm_g.reshape(1, d)

    col_sharded = {"w_in": w_in[0], "w_out_b": w_out_b[0], "w_ff1": w_ff1[0], "w_ple": w_ple[0]}
    row_sharded = {"w_out_a": w_out_a[0], "w_out": w_out[0], "w_ff2": w_ff2[0], "w_ple_gate": w_ple_gate[0]}
    order = ["w_in", "w_out_a", "w_out_b", "w_out", "w_ff1", "w_ff2", "w_ple_gate", "w_ple"]
    shards = [(col_sharded[n].T if n in col_sharded else row_sharded[n]).astype(BF16) for n in order]
    my_idx = 4 * lax.axis_index("x") + 2 * lax.axis_index("y") + lax.axis_index("c")
    w_in_t, *zones = _all_gather(shards, 1)
    ag = _copies_start("weights_gather_start", shards[1:], zones, w_in_t, True)

    tabs = _rope_tables(s)
    (h1, qraw, kraw, qrot, krot, va, qb, kb, vb, ga, gb) = _in_proj(
        xs, tabs, w_in_t, norm_mix_g, b_gate, q_norm_g, k_norm_g, tb, ag[4])
    oa, lse_a = _attn_a_fwd(qrot, krot, va, tq, tk)
    _, (w_oa, w_ob_t, w_o, w_ff1_t, w_ff2_f, w_pg, w_p_t) = _copies_wait(
        "weights_gather_wait", ag[0], ag[1], ag[2], ag[3], lse_a, True)
    flat = lambda arrs: [a.reshape(s, GB_W) for a in arrs]
    split = lambda arrs: [a.reshape(dil, s // dil, GB_W) for a, dil in zip(arrs, DILATIONS)]
    qb_r, kb_r, vb_r = flat(qb), flat(kb), flat(vb)
    bmaps = [[jnp.asarray(m) for m in _bucket_maps(dil)] for dil in DILATIONS]
    bias_tabs = [rel_bias[:, N_HEADS_PER_DIL * g:N_HEADS_PER_DIL * (g + 1)] for g in range(3)]
    band_out = [_band_fwd(dil, qb_r[g], kb_r[g], vb_r[g], bmaps[g][0], bias_tabs[g], cb)
                for g, dil in enumerate(DILATIONS)]
    og, lg = split([o for o, _ in band_out]), split([l for _, l in band_out])
    x2, ob, lse_b, ya, yb, u = _mix_out(xs, oa, og, lg, ga, gb, w_oa, w_ob_t, w_o, tb)
    tc = _pick_tile(w_ff1_t.shape[0], (512,))
    x3, r_act, h2 = _mlp_fwd(x2, w_ff1_t, w_ff2_f, norm_mlp_g, tb, tc)

    dx3, h3, dpre, dpe, pb, loss_part, dg_fin, dg_ple = _ple_loss(
        x3, ps, ts, w_pg, w_p_t, norm_ple_g, fin_g, tb)
    dx2, df, dg_mlp = _mlp_bwd(dx3, x2, r_act, w_ff1_t, w_ff2_f, norm_mlp_g, tb, tc)

    tkk = _pick_tile(s, (1024, 512))
    dff = w_ff1_t.shape[0]
    t1k = lambda n: _pick_tile(n, (1024, 512, 256))
    slots = lambda parts: [lax.empty((7, a.shape[0] // N_DEV, a.shape[1]), BF16) for a in parts]
    part1 = [_weight_grad("grad_w_ff1", df, h2, t1k(dff), t1k(d), tkk),
             _weight_grad("grad_w_ff2", r_act, dx3, t1k(dff), t1k(d), tkk, a_fn=_square_bf16, b_fn=_to_bf16),
             _weight_grad("grad_w_ple_gate", h3, dpre, t1k(d), t1k(d), tkk),
             _weight_grad("grad_w_ple", dpe, pb, t1k(d), ps.shape[1], tkk)]
    doa, dob, dd, dga, dgb, dya, dyb, dbg = _mix_out_bwd(dx2, ya, yb, ga, gb, ob, w_oa, w_ob_t, w_o, tb, dx2)
    part1 += [_weight_grad("grad_w_out_a", oa, dya, t1k(QA_W), t1k(d), tkk),
              _weight_grad("grad_w_out_b", dyb, ob, t1k(d), GB_W, tkk),
              _weight_grad("grad_w_out", u, dx2, t1k(d), t1k(d), tkk, b_fn=_to_bf16)]
    rs1 = _copies_start("grads1_start", part1, slots(part1), doa, False)
    dqrot, dkrot, dva = _attn_a_bwd(qrot, krot, va, oa, doa, lse_a, tq, tk, rs1[4])
    dob_r, lse_r, dd_r = flat(dob), flat(lse_b), flat(dd)
    bwd_q = [_band_bwd_q(dil, qb_r[g], kb_r[g], vb_r[g], dob_r[g], lse_r[g], dd_r[g], bmaps[g][0], bias_tabs[g], cb)
             for g, dil in enumerate(DILATIONS)]
    bwd_kv = [_band_bwd_kv(dil, qb_r[g], kb_r[g], vb_r[g], dob_r[g], lse_r[g], dd_r[g], bmaps[g][1], bias_tabs[g], cb)
              for g, dil in enumerate(DILATIONS)]
    dqb, dkb, dvb = split([r[0] for r in bwd_q]), split([r[0] for r in bwd_kv]), split([r[1] for r in bwd_kv])
    grad_x, dz, dg_mix, dg_q, dg_k = _in_proj_bwd(
        dx2, xs, dqrot, dkrot, dva, qraw, kraw, tabs, dqb, dkb, dvb, dga, dgb, w_in_t, norm_mix_g,
        q_norm_g, k_norm_g, _pick_tile(s, (256,)))
    d_rel = jnp.concatenate([r[1][:, :N_HEADS_PER_DIL] for r in bwd_q], axis=1)

    din = w_in_t.shape[0]
    ti_in = _pick_tile(din, (din // 2,)) if (din // 2) % LANES == 0 else din
    hd_ = d // 2
    part3 = [_weight_grad("grad_w_in_lo", dz, h1, ti_in, t1k(hd_), tkk, n=hd_)]
    rs3 = _copies_start("grads3_start", part3, slots(part3), grad_x, False)
    part4 = [_weight_grad("grad_w_in_hi", dz, h1, ti_in, t1k(hd_), tkk, col0=hd_, n=hd_, after=rs3[4])]
    rs4 = _copies_start("grads4_start", part4, slots(part4), rs3[4], False)

    def own_rows(a):
        n = a.shape[0] // N_DEV
        return lax.dynamic_slice(a, (my_idx * n, 0), (n, a.shape[1]))

    sums = {}
    src1, got1 = _copies_wait("grads1_wait", rs1[0], rs1[1], rs1[2], rs1[3], rs4[4], False)
    for n, a, r in zip(["w_ff1", "w_ff2", "w_ple_gate", "w_ple", "w_out_a", "w_out_b", "w_out"], src1, got1):
        sums[n] = _sum_slots("sum_" + n, r, own_rows(a))
    given_w = dict(w_in=w_in, w_out_a=w_out_a, w_out_b=w_out_b, w_out=w_out, w_ff1=w_ff1, w_ff2=w_ff2,
                   w_ple_gate=w_ple_gate, w_ple=w_ple)
    given_m = dict(w_in=m_w_in, w_out_a=m_w_out_a, w_out_b=m_w_out_b, w_out=m_w_out, w_ff1=m_w_ff1, w_ff2=m_w_ff2,
                   w_ple_gate=m_w_ple_gate, w_ple=m_w_ple)
    given_v = dict(w_in=v_w_in, w_out_a=v_w_out_a, w_out_b=v_w_out_b, w_out=v_w_out, w_ff1=v_w_ff1, w_ff2=v_w_ff2,
                   w_ple_gate=v_w_ple_gate, w_ple=v_w_ple)
    big = {}

    def update(n):
        g = sums[n].T if n in col_sharded else sums[n]
        delta, new_m, new_v = _adamw("adamw_" + n, given_w[n][0], g, given_m[n][0], given_v[n][0])
        big[n] = tuple(a[None] for a in (g, delta, new_m, new_v))

    for n in order[1:]:
        update(n)
    src3, got3 = _copies_wait("grads3_wait", rs3[0], rs3[1], rs3[2], rs3[3], big["w_ple"][1], False)
    src4, got4 = _copies_wait("grads4_wait", rs4[0], rs4[1], rs4[2], rs4[3], big["w_ple"][1], False)
    sums["w_in"] = jnp.concatenate([_sum_slots("sum_w_in_lo", got3[0], own_rows(src3[0])),
                                    _sum_slots("sum_w_in_hi", got4[0], own_rows(src4[0]))], axis=1)
    update("w_in")

    small_names = ["norm_mix_g", "b_gate", "q_norm_g", "k_norm_g", "rel_bias", "norm_mlp_g", "norm_ple_g",
                   "final_norm_g"]
    small_w = [norm_mix_g, b_gate, q_norm_g, k_norm_g, rel_bias, norm_mlp_g, norm_ple_g, final_norm_g]
    small_m = [m_norm_mix_g, m_b_gate, m_q_norm_g, m_k_norm_g, m_rel_bias, m_norm_mlp_g, m_norm_ple_g,
               m_final_norm_g]
    small_v = [v_norm_mix_g, v_b_gate, v_q_norm_g, v_k_norm_g, v_rel_bias, v_norm_mlp_g, v_norm_ple_g,
               v_final_norm_g]
    small_g = [dg_mix, dbg, dg_q, dg_k, d_rel, dg_mlp, dg_ple, dg_fin]
    sizes = [int(np.prod(w.shape)) for w in small_w]
    n_rows = -(-(sum(-(-sz // LANES) for sz in sizes) + 1) // 8) * 8
    pad = lambda v: jnp.pad(v.reshape(-1).astype(F32), (0, -v.size % LANES))
    pack = lambda vs, last: _pack_rows([pad(v) for v in vs] + [last], n_rows)
    zero_row = jnp.zeros((LANES,), F32)
    parts = _small_all_gather(pack(small_g, loss_part.reshape(-1) * (jnp.arange(LANES) == 0)))
    g_all, d_all, m_all, v_all = _small_update(parts, pack(small_w, zero_row), pack(small_m, zero_row),
                                               pack(small_v, zero_row))
    small = {}
    row = 0
    for n, w, sz in zip(small_names, small_w, sizes):
        nr = -(-sz // LANES)
        small[n] = tuple(a[row:row + nr].reshape(-1)[:sz].reshape(w.shape) for a in (g_all, d_all, m_all, v_all))
        row += nr
    loss = g_all[row, 0]

    names = ["norm_mix_g", "w_in", "b_gate", "q_norm_g", "k_norm_g", "rel_bias", "w_out_a", "w_out_b", "w_out",
             "norm_mlp_g", "w_ff1", "w_ff2", "norm_ple_g", "w_ple_gate", "w_ple", "final_norm_g"]
    res = {n: (big[n] if n in big else small[n]) for n in names}
    return (loss, grad_x[None], *[res[n][0] for n in names], *[res[n][1] for n in names],
            *[res[n][2] for n in names], *[res[n][3] for n in names])
```

```python
import functools
import math

import numpy as np
import jax
import jax.numpy as jnp
from jax import lax
from jax.experimental import pallas as pl
from jax.experimental.pallas import tpu as pltpu

F32 = jnp.float32
BF16 = jnp.bfloat16
MESH = pl.DeviceIdType.MESH

NORM_EPS = 1e-6
NEG_INF = -1e30
LOG2_E = math.log2(math.e)
LN_2 = math.log(2.0)
GRID_W = 64
ROPE_THETA = 10000.0
HEAD_DIM_A = 128
N_Q_HEADS_A = 8
N_KV_HEADS_A = 2
Q_PER_KV = N_Q_HEADS_A // N_KV_HEADS_A
HEAD_DIM_B = 64
N_HEADS_PER_DIL = 4
DILATIONS = (1, 4, 16)
BAND = 64
N_REL_BUCKETS = 32
REL_MAX_DIST = 1024
QA_W = N_Q_HEADS_A * HEAD_DIM_A
KA_W = N_KV_HEADS_A * HEAD_DIM_A
GB_W = N_HEADS_PER_DIL * HEAD_DIM_B
QB_W = GB_W * len(DILATIONS)
OFF_QA, OFF_KA, OFF_VA = 0, QA_W, QA_W + KA_W
OFF_QB = QA_W + 2 * KA_W
OFF_KB = OFF_QB + QB_W
OFF_VB = OFF_KB + QB_W
OFF_GA = OFF_VB + QB_W
N_DEV = 8
LANES = 128
VMEM_LIMIT = 56 * 2 ** 20

ADAM_LR, ADAM_B1, ADAM_B2, ADAM_EPS, ADAM_WD, ADAM_STEP = 0.001, 0.9, 0.999, 1e-08, 0.01, 10


def _cparams(sem):
    return pltpu.CompilerParams(dimension_semantics=sem, vmem_limit_bytes=VMEM_LIMIT)


def _resident(shape):
    nd = len(shape)
    return pl.BlockSpec(shape, lambda *_: (0,) * nd, pipeline_mode=pl.Buffered(1))


def _acc_spec(shape):
    nd = len(shape)
    return pl.BlockSpec(shape, lambda *_: (0,) * nd)


def _rows(tb, c):
    return pl.BlockSpec((tb, c), lambda i: (i, 0))


def _dil_shapes(s, dtype):
    return [jax.ShapeDtypeStruct((dil, s // dil, GB_W), dtype) for dil in DILATIONS]


def _dil_specs(tb):
    return [pl.BlockSpec((dil, tb // dil, GB_W), lambda i: (0, i, 0)) for dil in DILATIONS]


def _to_residues(val, out_ref, scr_ref, dil, dtype):
    if dil == 1:
        out_ref[0] = val.astype(dtype)
        return
    n = val.shape[0] // dil
    scr_ref[0] = val[:, :LANES]
    scr_ref[1] = val[:, LANES:]
    for r in range(dil):
        out_ref[r] = jnp.concatenate([scr_ref[0, pl.ds(r, n, stride=dil), :],
                                      scr_ref[1, pl.ds(r, n, stride=dil), :]], axis=1).astype(dtype)


def _from_residues(in_ref, scr_ref, dil):
    if dil == 1:
        return in_ref[0]
    n = in_ref.shape[1]
    for r in range(dil):
        v = in_ref[r]
        scr_ref[0, pl.ds(r, n, stride=dil), :] = v[:, :LANES]
        scr_ref[1, pl.ds(r, n, stride=dil), :] = v[:, LANES:]
    return jnp.concatenate([scr_ref[0], scr_ref[1]], axis=1)


def _dot_nt(a, b):
    return lax.dot_general(a, b, (((1,), (1,)), ((), ())), preferred_element_type=F32)


def _dot_nn(a, b):
    return lax.dot_general(a, b, (((1,), (0,)), ((), ())), preferred_element_type=F32)


def _dot_tn(a, b):
    return lax.dot_general(a, b, (((0,), (0,)), ((), ())), preferred_element_type=F32)


def _rstd(x):
    return lax.rsqrt(jnp.mean(x * x, axis=-1, keepdims=True) + NORM_EPS)


def _rms_bwd(dy, n, r, g):
    dn = dy * g
    return r * (dn - n * jnp.mean(dn * n, axis=-1, keepdims=True))


def _colsum(v):
    return jnp.sum(v, axis=0, keepdims=True)


def _sigmoid(v):
    return 1.0 / (1.0 + jnp.exp(-v))


def _rope_tables(s):
    half = HEAD_DIM_A // 2
    inv = np.power(np.float32(ROPE_THETA), -np.arange(0, half, 2, dtype=np.float32) / np.float32(half))
    t = np.arange(s)
    ang_r = (t // GRID_W).astype(np.float32)[:, None] * inv[None, :]
    ang_c = (t % GRID_W).astype(np.float32)[:, None] * inv[None, :]
    cr, sr, cc, sc = np.cos(ang_r), np.sin(ang_r), np.cos(ang_c), np.sin(ang_c)
    z = np.zeros_like(sr)
    cos = np.concatenate([cr, cr, cc, cc], axis=1)
    s1 = np.concatenate([z, sr, z, sc], axis=1)
    s2 = np.concatenate([-sr, z, -sc, z], axis=1)
    return [jnp.asarray(a, F32) for a in (cos, s1, s2)]


def _my_place():
    return lax.axis_index("x"), lax.axis_index("y"), lax.axis_index("c")


def _all_gather(shards, n_gather):
    n_all = len(shards)
    nw = n_gather

    def body(*refs):
        ins, outs = refs[:n_all], refs[n_all:2 * n_all]
        send_sems, recv_sems, local_sems = refs[2 * n_all:]
        x, y, c = _my_place()
        me, sibling = (x, y, c), (x, y, 1 - c)
        chips = [(1 - x, y), (x, 1 - y), (1 - x, 1 - y)]

        def rows(w, px, py, pc):
            n = ins[w].shape[0]
            return outs[w].at[pl.ds(pl.multiple_of((4 * px + 2 * py + pc) * n, 16), n), :]

        def copy(w, k, block, to, src=None):
            return pltpu.make_async_remote_copy(
                src_ref=rows(w, *block) if src is None else src, dst_ref=rows(w, *block),
                send_sem=send_sems.at[w, k], recv_sem=recv_sems.at[w, k], device_id=to, device_id_type=MESH)

        mine = [pltpu.make_async_copy(ins[w], rows(w, *me), local_sems.at[w]) for w in range(n_all)]
        for cp in mine:
            cp.start()
        first = []
        for w in range(nw):
            first.append(copy(w, 0, me, sibling, src=ins[w]))
            first += [copy(w, 1 + j, me, (*chip, c), src=ins[w]) for j, chip in enumerate(chips)]
        for cp in first:
            cp.start()
        passed = []
        for j, chip in enumerate(chips):
            for w in range(nw):
                copy(w, 1 + j, (*chip, c), me).wait_recv()
                fwd = copy(w, 4 + j, (*chip, c), sibling)
                fwd.start()
                passed.append(fwd)
        for w in range(nw):
            copy(w, 0, sibling, me).wait_recv()
        for j, chip in enumerate(chips):
            for w in range(nw):
                copy(w, 4 + j, (*chip, 1 - c), me).wait_recv()
        for cp in first + passed:
            cp.wait_send()
        for cp in mine:
            cp.wait()

    any_spec = pl.BlockSpec(memory_space=pl.ANY)
    return pl.pallas_call(
        body, name="weights_all_gather",
        out_shape=[jax.ShapeDtypeStruct((N_DEV * s.shape[0], s.shape[1]), s.dtype) for s in shards],
        in_specs=[any_spec] * n_all, out_specs=[any_spec] * n_all,
        scratch_shapes=[pltpu.SemaphoreType.DMA((nw, 7)), pltpu.SemaphoreType.DMA((nw, 7)),
                        pltpu.SemaphoreType.DMA((n_all,))],
    )(*shards)


def _place_own_rows(shards, my_idx):
    nw = len(shards)

    def body(idx_ref, *refs):
        for w in range(nw):
            refs[nw + w][...] = refs[w][...]

    grid_spec = pltpu.PrefetchScalarGridSpec(
        num_scalar_prefetch=1, grid=(1,),
        in_specs=[pl.BlockSpec(s.shape, lambda i, idx: (0, 0)) for s in shards],
        out_specs=[pl.BlockSpec(s.shape, lambda i, idx: (idx[0], 0)) for s in shards])
    return pl.pallas_call(
        body, name="place_own_rows", grid_spec=grid_spec,
        out_shape=[jax.ShapeDtypeStruct((N_DEV * s.shape[0], s.shape[1]), s.dtype) for s in shards],
        compiler_params=_cparams(("arbitrary",)))(my_idx.reshape(1).astype(jnp.int32), *shards)


_FLIPS = [(fx, fy, fc) for fx in (0, 1) for fy in (0, 1) for fc in (0, 1)][1:]


def _small_all_gather(v):
    def body(v_ref, out_ref, send_sems, recv_sems):
        x, y, c = _my_place()
        my_idx = 4 * x + 2 * y + c
        out_ref[my_idx] = v_ref[...]
        sends = []
        for k, (fx, fy, fc) in enumerate(_FLIPS):
            to = (1 - x if fx else x, 1 - y if fy else y, 1 - c if fc else c)
            sends.append(pltpu.make_async_remote_copy(
                src_ref=v_ref, dst_ref=out_ref.at[my_idx], send_sem=send_sems.at[k], recv_sem=recv_sems.at[k],
                device_id=to, device_id_type=MESH))
        for cp in sends:
            cp.start()
        for k, (fx, fy, fc) in enumerate(_FLIPS):
            frm_idx = 4 * (1 - x if fx else x) + 2 * (1 - y if fy else y) + (1 - c if fc else c)
            pltpu.make_async_remote_copy(
                src_ref=v_ref, dst_ref=out_ref.at[frm_idx], send_sem=send_sems.at[k], recv_sem=recv_sems.at[k],
                device_id=(x, y, c), device_id_type=MESH).wait_recv()
        for cp in sends:
            cp.wait_send()

    vm = pl.BlockSpec(memory_space=pltpu.VMEM)
    return pl.pallas_call(
        body, name="small_all_gather", out_shape=jax.ShapeDtypeStruct((N_DEV,) + v.shape, v.dtype),
        in_specs=[vm], out_specs=vm,
        scratch_shapes=[pltpu.SemaphoreType.DMA((7,)), pltpu.SemaphoreType.DMA((7,))],
    )(v)


_HBM = pl.BlockSpec(memory_space=pltpu.HBM)
_SEM = pl.BlockSpec(memory_space=pltpu.SEMAPHORE)
_ANY = pl.BlockSpec(memory_space=pl.ANY)
_SPLIT_COPY = dict(has_side_effects=pltpu.SideEffectType.DATAFLOW_SIDE_EFFECTING)


def _peer(x, y, c, k):
    fx, fy, fc = _FLIPS[k]
    return (1 - x if fx else x, 1 - y if fy else y, 1 - c if fc else c)


def _in_hbm(a):
    return pltpu.with_memory_space_constraint(a, pltpu.HBM)


def _split_copies(srcs, lands, send_sems, recv_sems, gather, arriving):
    x, y, c = _my_place()
    my_idx = 4 * x + 2 * y + c
    out = []
    for k in range(7):
        to = _peer(x, y, c, k)
        to_idx = 4 * to[0] + 2 * to[1] + to[2]
        for w in range(len(srcs)):
            if gather:
                n = srcs[w].shape[0]
                src = srcs[w]
                dst = lands[w].at[pl.ds(pl.multiple_of((to_idx if arriving else my_idx) * n, 16), n), :]
            else:
                n = lands[w].shape[1]
                src = srcs[w].at[pl.ds(pl.multiple_of(to_idx * n, 16), n), :]
                dst = lands[w].at[k]
            out.append(pltpu.make_async_remote_copy(
                src_ref=src, dst_ref=dst, send_sem=send_sems.at[7 * w + k], recv_sem=recv_sems.at[7 * w + k],
                device_id=to, device_id_type=MESH))
    return out


def _copies_start(name, srcs, lands, after, gather):
    nw = len(srcs)

    def body(*refs):
        send_sems, recv_sems = refs[2 * nw + 1], refs[2 * nw + 2]
        for cp in _split_copies(refs[:nw], refs[nw:2 * nw], send_sems, recv_sems, gather, False):
            cp.start()
        refs[-1][...] = jnp.zeros_like(refs[-1])

    sems = pltpu.SemaphoreType.DMA((7 * nw,))
    thru = [pltpu.HBM(a.shape, a.dtype) for a in list(srcs) + list(lands)]
    res = pl.pallas_call(
        body, name=name, out_shape=(sems, sems, *thru, jax.ShapeDtypeStruct((8, LANES), F32)),
        in_specs=[_HBM] * (2 * nw) + [_ANY], out_specs=(_SEM, _SEM, *[_HBM] * (2 * nw), pl.BlockSpec(memory_space=pltpu.VMEM)),
        input_output_aliases={i: 2 + i for i in range(2 * nw)},
        compiler_params=pltpu.CompilerParams(**_SPLIT_COPY),
    )(*[_in_hbm(a) for a in srcs], *[_in_hbm(a) for a in lands], after)
    return res[0], res[1], list(res[2:2 + nw]), list(res[2 + nw:2 + 2 * nw]), res[-1]


def _copies_wait(name, send_sems, recv_sems, srcs, lands, after, gather):
    nw = len(srcs)

    def body(*refs):
        for cp in _split_copies(refs[:nw], refs[nw:2 * nw], refs[2 * nw], refs[2 * nw + 1], gather, False):
            cp.wait_send()
        for cp in _split_copies(refs[:nw], refs[nw:2 * nw], refs[2 * nw], refs[2 * nw + 1], gather, True):
            cp.wait_recv()

    thru = [pltpu.HBM(a.shape, a.dtype) for a in list(srcs) + list(lands)]
    res = pl.pallas_call(
        body, name=name, out_shape=tuple(thru),
        in_specs=[_HBM] * (2 * nw) + [_SEM, _SEM, _ANY], out_specs=tuple([_HBM] * (2 * nw)),
        input_output_aliases={i: i for i in range(2 * nw)},
        compiler_params=pltpu.CompilerParams(**_SPLIT_COPY),
    )(*srcs, *lands, send_sems, recv_sems, after)
    return list(res[:nw]), list(res[nw:])


def _in_proj(x, tabs, w_in_t, g_mix, b_gate, q_g, k_g, tb, after):
    s, d = x.shape
    n_gate_chunks = d // 256
    q_scale = HEAD_DIM_A ** -0.5 * LOG2_E
    b_scale = HEAD_DIM_B ** -0.5 * LOG2_E

    def body(x_ref, c_ref, s1_ref, s2_ref, w_ref, gmix_ref, bg_ref, qg_ref, kg_ref, after_ref,
             h1_ref, qraw_ref, kraw_ref, qrot_ref, krot_ref, va_ref, *rest):
        qb_refs, kb_refs, vb_refs = rest[0:3], rest[3:6], rest[6:9]
        ga_ref, gb_ref, scr_ref = rest[9:]
        xv = x_ref[...]
        hb = (xv * _rstd(xv) * gmix_ref[...]).astype(BF16)
        h1_ref[...] = hb
        cos, s1, s2 = c_ref[...], s1_ref[...], s2_ref[...]

        def proj(lo, width):
            return _dot_nt(hb, w_ref[lo:lo + width, :])

        def norm_rope(z, g):
            n = z * _rstd(z) * g
            return n * cos + pltpu.roll(n, 32, 1) * s1 + pltpu.roll(n, 96, 1) * s2

        for j in range(QA_W // 256):
            z = proj(OFF_QA + 256 * j, 256)
            qraw_ref[:, 256 * j:256 * j + 256] = z
            for hh in range(2):
                lo = 256 * j + 128 * hh
                qrot_ref[:, lo:lo + 128] = (norm_rope(z[:, 128 * hh:128 * hh + 128], qg_ref[...]) * q_scale).astype(BF16)
        z = proj(OFF_KA, 256)
        kraw_ref[...] = z
        for hh in range(2):
            krot_ref[:, 128 * hh:128 * hh + 128] = norm_rope(z[:, 128 * hh:128 * hh + 128], kg_ref[...]).astype(BF16)
        va_ref[...] = proj(OFF_VA, 256).astype(BF16)
        for g, dil in enumerate(DILATIONS):
            _to_residues(proj(OFF_QB + GB_W * g, GB_W) * b_scale, qb_refs[g], scr_ref, dil, BF16)
            _to_residues(proj(OFF_KB + GB_W * g, GB_W), kb_refs[g], scr_ref, dil, BF16)
            _to_residues(proj(OFF_VB + GB_W * g, GB_W), vb_refs[g], scr_ref, dil, BF16)
        for j in range(n_gate_chunks):
            sl = slice(256 * j, 256 * j + 256)
            ga_ref[:, sl] = _sigmoid(proj(OFF_GA + 256 * j, 256) + bg_ref[:, sl]).astype(BF16)
            gb_ref[:, sl] = _sigmoid(
                proj(OFF_GA + d + 256 * j, 256) + bg_ref[:, d + 256 * j:d + 256 * j + 256]).astype(BF16)

    sd = jax.ShapeDtypeStruct
    outs = [sd((s, d), BF16), sd((s, QA_W), F32), sd((s, KA_W), F32), sd((s, QA_W), BF16), sd((s, KA_W), BF16),
            sd((s, KA_W), BF16)] + _dil_shapes(s, BF16) * 3 + [sd((s, d), BF16), sd((s, d), BF16)]
    out_specs = [_rows(tb, d), _rows(tb, QA_W), _rows(tb, KA_W), _rows(tb, QA_W), _rows(tb, KA_W), _rows(tb, KA_W)
                 ] + _dil_specs(tb) * 3 + [_rows(tb, d), _rows(tb, d)]
    in_specs = [_rows(tb, d), _rows(tb, LANES), _rows(tb, LANES), _rows(tb, LANES), _resident(w_in_t.shape),
                _resident(g_mix.shape), _resident(b_gate.shape), _resident(q_g.shape), _resident(k_g.shape), _ANY]
    res = list(pl.pallas_call(body, name="in_proj", grid=(s // tb,), in_specs=in_specs, out_specs=out_specs,
                              out_shape=outs, scratch_shapes=[pltpu.VMEM((2, tb, LANES), F32)],
                              compiler_params=_cparams(("arbitrary",)))(
        x, *tabs, w_in_t, g_mix, b_gate, q_g, k_g, after))
    return res[:6] + [res[6:9], res[9:12], res[12:15]] + res[15:]


def _attn_a_fwd(qrot, krot, va, tq, tk):
    s = qrot.shape[0]
    n_kv = s // tk
    gw = Q_PER_KV * HEAD_DIM_A

    def body(q_ref, k_ref, v_ref, o_ref, lse_ref):
        q4 = jnp.concatenate([q_ref[:, 128 * h:128 * h + 128] for h in range(Q_PER_KV)], axis=0)

        def step(j, carry):
            m, l, acc = carry
            sl = pl.ds(pl.multiple_of(j * tk, tk), tk)
            kj, vj = k_ref[sl, :], v_ref[sl, :]
            sc = _dot_nt(kj, q4)
            m_new = jnp.maximum(m, jnp.max(sc, axis=0, keepdims=True))
            p = jnp.exp2(sc - m_new)
            alpha = jnp.exp2(m - m_new)
            l = alpha * l + jnp.sum(p, axis=0, keepdims=True)
            acc = alpha * acc + _dot_tn(vj, p.astype(BF16))
            return m_new, l, acc

        rows = Q_PER_KV * tq
        m, l, acc = lax.fori_loop(0, n_kv, step, (jnp.full((1, rows), NEG_INF, F32), jnp.zeros((1, rows), F32),
                                                  jnp.zeros((HEAD_DIM_A, rows), F32)))
        o = (acc / l).T
        lse = m + jnp.log2(l)
        for h in range(Q_PER_KV):
            o_ref[:, 128 * h:128 * h + 128] = o[h * tq:(h + 1) * tq].astype(BF16)
            lse_ref[0, h:h + 1, :] = lse[:, h * tq:(h + 1) * tq]

    return pl.pallas_call(
        body, name="attn_a_fwd", grid=(N_KV_HEADS_A, s // tq),
        in_specs=[pl.BlockSpec((tq, gw), lambda g, i: (i, g)),
                  pl.BlockSpec((s, HEAD_DIM_A), lambda g, i: (0, g)),
                  pl.BlockSpec((s, HEAD_DIM_A), lambda g, i: (0, g))],
        out_specs=[pl.BlockSpec((tq, gw), lambda g, i: (i, g)),
                   pl.BlockSpec((1, Q_PER_KV, tq), lambda g, i: (g, 0, i))],
        out_shape=[jax.ShapeDtypeStruct((s, QA_W), BF16), jax.ShapeDtypeStruct((N_KV_HEADS_A, Q_PER_KV, s), F32)],
        compiler_params=_cparams(("arbitrary", "arbitrary")))(qrot, krot, va)


def _attn_a_bwd(qrot, krot, va, oa, doa, lse, tq, tk, after):
    s = qrot.shape[0]
    n_kv = s // tk
    gw = Q_PER_KV * HEAD_DIM_A

    def body(q_ref, do_ref, o_ref, lse_ref, k_ref, v_ref, after_ref, dq_ref, dk_ref, dv_ref):
        @pl.when(pl.program_id(1) == 0)
        def _():
            dk_ref[...] = jnp.zeros_like(dk_ref)
            dv_ref[...] = jnp.zeros_like(dv_ref)

        def stack(ref):
            return jnp.concatenate([ref[:, 128 * h:128 * h + 128] for h in range(Q_PER_KV)], axis=0)

        q4, do4, o4 = stack(q_ref), stack(do_ref), stack(o_ref)
        q4t, do4t = q4.T, do4.T
        delta = jnp.sum((do4.astype(F32) * o4.astype(F32)).T, axis=0, keepdims=True)
        lse4 = jnp.concatenate([lse_ref[0, h:h + 1, :] for h in range(Q_PER_KV)], axis=1)

        def step(j, dq):
            sl = pl.ds(pl.multiple_of(j * tk, tk), tk)
            kj, vj = k_ref[sl, :], v_ref[sl, :]
            p = jnp.exp2(_dot_nt(kj, q4) - lse4)
            ds = (p * (_dot_nt(vj, do4) - delta)).astype(BF16)
            dk_ref[:, sl] += _dot_nt(q4t, ds)
            dv_ref[:, sl] += _dot_nt(do4t, p.astype(BF16))
            return dq + _dot_tn(kj, ds)

        dq = lax.fori_loop(0, n_kv, step, jnp.zeros((HEAD_DIM_A, Q_PER_KV * tq), F32)).T
        for h in range(Q_PER_KV):
            dq_ref[:, 128 * h:128 * h + 128] = dq[h * tq:(h + 1) * tq]

    qspec = pl.BlockSpec((tq, gw), lambda g, i: (i, g))
    kspec = pl.BlockSpec((s, HEAD_DIM_A), lambda g, i: (0, g))
    ktspec = pl.BlockSpec((HEAD_DIM_A, s), lambda g, i: (g, 0))
    return pl.pallas_call(
        body, name="attn_a_bwd", grid=(N_KV_HEADS_A, s // tq),
        in_specs=[qspec, qspec, qspec, pl.BlockSpec((1, Q_PER_KV, tq), lambda g, i: (g, 0, i)), kspec, kspec, _ANY],
        out_specs=[qspec, ktspec, ktspec],
        out_shape=[jax.ShapeDtypeStruct((s, QA_W), F32), jax.ShapeDtypeStruct((KA_W, s), F32),
                   jax.ShapeDtypeStruct((KA_W, s), F32)],
        compiler_params=_cparams(("arbitrary", "arbitrary")))(qrot, doa, oa, lse, krot, va, after)


BAND_QB = 128
BAND_WIN = BAND_QB + 2 * BAND


def _band_specs(s, cb):
    per = cb // BAND
    last = s // BAND - 1
    cur = pl.BlockSpec((cb, GB_W), lambda i: (i, 0))
    prev = pl.BlockSpec((BAND, GB_W), lambda i: (jnp.maximum(i * per - 1, 0), 0))
    nxt = pl.BlockSpec((BAND, GB_W), lambda i: (jnp.minimum(i * per + per, last), 0))
    return cur, prev, nxt


def _window(prev_ref, cur_ref, next_ref):
    return jnp.concatenate([prev_ref[...], cur_ref[...], next_ref[...]], axis=0)


def _band_mask(base, seg_shift, window_rows):
    shape = (BAND_WIN, BAND_QB) if window_rows else (BAND_QB, BAND_WIN)
    a = lax.broadcasted_iota(jnp.int32, shape, 0)
    b = lax.broadcasted_iota(jnp.int32, shape, 1)
    rq, rk = (base - BAND + a, base + b) if window_rows else (base + a, base - BAND + b)
    same_segment = lax.shift_right_arithmetic(rq, jnp.int32(seg_shift)) == lax.shift_right_arithmetic(rk, jnp.int32(seg_shift))
    return (jnp.abs(rk - rq) <= BAND) & same_segment


def _build_bias(bmap_ref, tab_ref, bias_ref):
    bm = bmap_ref[...]
    acc = [jnp.full(bm.shape, NEG_INF, F32) for _ in range(N_HEADS_PER_DIL)]
    for b in range(N_REL_BUCKETS):
        hit = bm == b
        for h in range(N_HEADS_PER_DIL):
            acc[h] = jnp.where(hit, tab_ref[b, h] * LOG2_E, acc[h])
    rows = bm.shape[0]
    for h in range(N_HEADS_PER_DIL):
        bias_ref[h * rows:(h + 1) * rows, :] = acc[h]


def _segment_mask(base, seg_len, seg_shift, window_rows):
    if seg_len % BAND_QB:
        return _band_mask(base, seg_shift, window_rows)
    pos = lax.rem(base, seg_len)
    shape, dim = ((BAND_WIN, 1), 0) if window_rows else ((1, BAND_WIN), 1)
    w = lax.broadcasted_iota(jnp.int32, shape, dim)
    return ((w >= BAND) | (pos != 0)) & ((w < BAND + BAND_QB) | (pos != seg_len - BAND_QB))


def _head_lane_masks():
    lane = lax.broadcasted_iota(jnp.int32, (1, LANES), 1)
    return [lane < HEAD_DIM_B, lane >= HEAD_DIM_B]


def _rows4(mask):
    return mask if mask.shape[0] == 1 else jnp.concatenate([mask] * N_HEADS_PER_DIL, axis=0)


def _head_scores(a, b):
    hm = _head_lane_masks()
    out = []
    for hp in range(2):
        ls = slice(LANES * hp, LANES * hp + LANES)
        ah = a[:, ls]
        both = jnp.concatenate([jnp.where(hm[0], ah, jnp.zeros_like(ah)), jnp.where(hm[1], ah, jnp.zeros_like(ah))],
                               axis=0)
        out.append(_dot_nt(both, b[:, ls]))
    return jnp.concatenate(out, axis=0)


def _head_combine(p, v, scale=None, transposed=False):
    hm = _head_lane_masks()
    rows = p.shape[0] // N_HEADS_PER_DIL
    halves = []
    for hp in range(2):
        vh = v[:, LANES * hp:LANES * hp + LANES]
        acc = None
        for hh in range(2):
            h = 2 * hp + hh
            ph = p[h * rows:(h + 1) * rows]
            vm = jnp.where(hm[hh], vh, jnp.zeros_like(vh))
            t = _dot_tn(ph, vm) if transposed else _dot_nn(ph, vm)
            if scale is not None:
                t = t * scale[h * rows:(h + 1) * rows]
            acc = t if acc is None else acc + t
        halves.append(acc)
    return jnp.concatenate(halves, axis=1)


def _head_spread(col):
    rows = col.shape[0] // N_HEADS_PER_DIL
    lane = lax.broadcasted_iota(jnp.int32, (1, GB_W), 1)
    out = jnp.zeros((rows, GB_W), F32)
    for h in range(N_HEADS_PER_DIL):
        out = jnp.where((lane >= HEAD_DIM_B * h) & (lane < HEAD_DIM_B * (h + 1)), col[h * rows:(h + 1) * rows], out)
    return out


def _head_cols(v):
    return jnp.concatenate([v[:, HEAD_DIM_B * h:HEAD_DIM_B * h + 1] for h in range(N_HEADS_PER_DIL)], axis=0)


def _seg_shift(s, dil):
    seg = s // dil
    assert seg & (seg - 1) == 0, "segment length must be a power of two"
    return seg.bit_length() - 1


def _band_fwd(dil, qb, kb, vb, bmap, tab, cb):
    s = qb.shape[0]
    shift = _seg_shift(s, dil)

    def body(q_ref, kp_ref, kc_ref, kn_ref, vp_ref, vc_ref, vn_ref, bmap_ref, tab_ref, o_ref, lse_ref, bias_ref):
        @pl.when(pl.program_id(0) == 0)
        def _():
            _build_bias(bmap_ref, tab_ref, bias_ref)

        kw, vw = _window(kp_ref, kc_ref, kn_ref), _window(vp_ref, vc_ref, vn_ref)
        for jj in range(cb // BAND_QB):
            r0 = BAND_QB * jj
            mask = _rows4(_segment_mask(pl.program_id(0) * cb + r0, s // dil, shift, False))
            sc = _head_scores(q_ref[r0:r0 + BAND_QB, :], kw[r0:r0 + BAND_WIN, :]) + bias_ref[...]
            sc = jnp.where(mask, sc, NEG_INF)
            m = jnp.max(sc, axis=-1, keepdims=True)
            e = jnp.exp2(sc - m)
            l = jnp.sum(e, axis=-1, keepdims=True)
            o = _head_combine(e.astype(BF16), vw[r0:r0 + BAND_WIN, :], 1.0 / l)
            o_ref[r0:r0 + BAND_QB, :] = o
            lse_ref[r0:r0 + BAND_QB, :] = _head_spread(m + jnp.log2(l))

    cur, prev, nxt = _band_specs(s, cb)
    return pl.pallas_call(
        body, name=f"band_fwd_d{dil}", grid=(s // cb,),
        in_specs=[cur, prev, cur, nxt, prev, cur, nxt, _resident(bmap.shape), pl.BlockSpec(memory_space=pltpu.SMEM)],
        out_specs=[cur, cur],
        out_shape=[jax.ShapeDtypeStruct(qb.shape, F32), jax.ShapeDtypeStruct(qb.shape, F32)],
        scratch_shapes=[pltpu.VMEM((N_HEADS_PER_DIL * BAND_QB, BAND_WIN), F32)],
        compiler_params=_cparams(("arbitrary",)))(qb, kb, kb, kb, vb, vb, vb, bmap, tab)


def _band_bwd_q(dil, qb, kb, vb, dob, lse, dd, bmap, tab, cb):
    s = qb.shape[0]
    shift = _seg_shift(s, dil)
    n_steps = s // cb

    def body(q_ref, do_ref, lse_ref, dd_ref, kp_ref, kc_ref, kn_ref, vp_ref, vc_ref, vn_ref, bmap_ref, tab_ref,
             dq_ref, dtab_ref, bias_ref, dsum_ref):
        @pl.when(pl.program_id(0) == 0)
        def _():
            _build_bias(bmap_ref, tab_ref, bias_ref)
            dsum_ref[...] = jnp.zeros_like(dsum_ref)

        kw, vw = _window(kp_ref, kc_ref, kn_ref), _window(vp_ref, vc_ref, vn_ref)
        for jj in range(cb // BAND_QB):
            r0 = BAND_QB * jj
            mask = _rows4(_segment_mask(pl.program_id(0) * cb + r0, s // dil, shift, False))
            k3, v3 = kw[r0:r0 + BAND_WIN, :], vw[r0:r0 + BAND_WIN, :]
            sc = _head_scores(q_ref[r0:r0 + BAND_QB, :], k3) + bias_ref[...]
            sc = jnp.where(mask, sc, NEG_INF)
            p = jnp.exp2(sc - _head_cols(lse_ref[r0:r0 + BAND_QB, :]))
            dp = _head_scores(do_ref[r0:r0 + BAND_QB, :], v3)
            ds = p * (dp - _head_cols(dd_ref[r0:r0 + BAND_QB, :]))
            dsum_ref[...] += ds
            dq_ref[r0:r0 + BAND_QB, :] = _head_combine(ds.astype(BF16), k3)

        @pl.when(pl.program_id(0) == n_steps - 1)
        def _():
            bm = bmap_ref[...]
            lane = lax.broadcasted_iota(jnp.int32, (1, LANES), 1)
            for b in range(N_REL_BUCKETS):
                hit = bm == b
                row = jnp.zeros((1, LANES), F32)
                for h in range(N_HEADS_PER_DIL):
                    part = dsum_ref[h * BAND_QB:(h + 1) * BAND_QB, :]
                    row = jnp.where(lane == h, jnp.sum(jnp.where(hit, part, 0.0)), row)
                dtab_ref[b:b + 1, :] = row

    cur, prev, nxt = _band_specs(s, cb)
    return pl.pallas_call(
        body, name=f"band_bwd_q_d{dil}", grid=(n_steps,),
        in_specs=[cur, cur, cur, cur, prev, cur, nxt, prev, cur, nxt, _resident(bmap.shape),
                  pl.BlockSpec(memory_space=pltpu.SMEM)],
        out_specs=[cur, _acc_spec((N_REL_BUCKETS, LANES))],
        out_shape=[jax.ShapeDtypeStruct(qb.shape, F32), jax.ShapeDtypeStruct((N_REL_BUCKETS, LANES), F32)],
        scratch_shapes=[pltpu.VMEM((N_HEADS_PER_DIL * BAND_QB, BAND_WIN), F32),
                        pltpu.VMEM((N_HEADS_PER_DIL * BAND_QB, BAND_WIN), F32)],
        compiler_params=_cparams(("arbitrary",)))(qb, dob, lse, dd, kb, kb, kb, vb, vb, vb, bmap, tab)


def _band_bwd_kv(dil, qb, kb, vb, dob, lse, dd, bmap_t, tab, cb):
    s = qb.shape[0]
    shift = _seg_shift(s, dil)

    def body(k_ref, v_ref, qp_ref, qc_ref, qn_ref, dp_ref, dc_ref, dn_ref, lp_ref, lc_ref, ln_ref,
             ep_ref, ec_ref, en_ref, bmap_ref, tab_ref, dk_ref, dv_ref, bias_ref):
        @pl.when(pl.program_id(0) == 0)
        def _():
            _build_bias(bmap_ref, tab_ref, bias_ref)

        qw, dow = _window(qp_ref, qc_ref, qn_ref), _window(dp_ref, dc_ref, dn_ref)
        lw, ew = _window(lp_ref, lc_ref, ln_ref), _window(ep_ref, ec_ref, en_ref)
        for jj in range(cb // BAND_QB):
            r0 = BAND_QB * jj
            mask = _rows4(_segment_mask(pl.program_id(0) * cb + r0, s // dil, shift, True))
            q3, do3 = qw[r0:r0 + BAND_WIN, :], dow[r0:r0 + BAND_WIN, :]
            sc = _head_scores(q3, k_ref[r0:r0 + BAND_QB, :]) + bias_ref[...]
            sc = jnp.where(mask, sc, NEG_INF)
            p = jnp.exp2(sc - _head_cols(lw[r0:r0 + BAND_WIN, :]))
            ds = p * (_head_scores(do3, v_ref[r0:r0 + BAND_QB, :]) - _head_cols(ew[r0:r0 + BAND_WIN, :]))
            dk_ref[r0:r0 + BAND_QB, :] = _head_combine(ds.astype(BF16), q3, transposed=True)
            dv_ref[r0:r0 + BAND_QB, :] = _head_combine(p.astype(BF16), do3, transposed=True)

    cur, prev, nxt = _band_specs(s, cb)
    win = [prev, cur, nxt]
    return pl.pallas_call(
        body, name=f"band_bwd_kv_d{dil}", grid=(s // cb,),
        in_specs=[cur, cur] + win * 4 + [_resident(bmap_t.shape), pl.BlockSpec(memory_space=pltpu.SMEM)],
        out_specs=[cur, cur],
        out_shape=[jax.ShapeDtypeStruct(qb.shape, F32), jax.ShapeDtypeStruct(qb.shape, F32)],
        scratch_shapes=[pltpu.VMEM((N_HEADS_PER_DIL * BAND_WIN, BAND_QB), F32)],
        compiler_params=_cparams(("arbitrary",)))(
        kb, vb, qb, qb, qb, dob, dob, dob, lse, lse, lse, dd, dd, dd, bmap_t, tab)


def _t5_bucket(rel):
    nb = N_REL_BUCKETS // 2
    ret = (rel > 0).astype(np.int32) * nb
    n = np.abs(rel)
    max_exact = nb // 2
    large = max_exact + (np.log(np.maximum(n, 1) / max_exact) / math.log(REL_MAX_DIST / max_exact)
                         * (nb - max_exact)).astype(np.int32)
    large = np.minimum(large, nb - 1)
    return ret + np.where(n < max_exact, n, large).astype(np.int32)


def _bucket_maps(dil):
    off_qk = np.arange(BAND_WIN)[None, :] - BAND - np.arange(BAND_QB)[:, None]
    off_kq = np.arange(BAND_QB)[None, :] + BAND - np.arange(BAND_WIN)[:, None]
    return [np.where(np.abs(off) <= BAND, _t5_bucket(off * dil), -1).astype(np.int32) for off in (off_qk, off_kq)]


def _seg_sum(v):
    lane = lax.broadcasted_iota(jnp.int32, (1, v.shape[1]), 1)
    out = jnp.zeros_like(v)
    for h in range(v.shape[1] // HEAD_DIM_B):
        m = (lane >= HEAD_DIM_B * h) & (lane < HEAD_DIM_B * (h + 1))
        out = jnp.where(m, jnp.sum(jnp.where(m, v, 0.0), axis=-1, keepdims=True), out)
    return out


def _mix_out(x, oa, og, lg, ga, gb, w_oa, w_ob_t, w_o, tb):
    s, d = x.shape

    def body(x_ref, oa_ref, og0_ref, og1_ref, og2_ref, lg0_ref, lg1_ref, lg2_ref, ga_ref, gb_ref,
             woa_ref, wob_ref, wo_ref, x2_ref, ob_ref, lse0_ref, lse1_ref, lse2_ref, ya_ref, yb_ref, u_ref, scr_ref):
        og_refs, lg_refs = (og0_ref, og1_ref, og2_ref), (lg0_ref, lg1_ref, lg2_ref)
        l0, l1, l2 = [_from_residues(lg_refs[g], scr_ref, dil) for g, dil in enumerate(DILATIONS)]
        lmax = jnp.maximum(jnp.maximum(l0, l1), l2)
        w0, w1, w2 = jnp.exp2(l0 - lmax), jnp.exp2(l1 - lmax), jnp.exp2(l2 - lmax)
        den = w0 + w1 + w2
        o0, o1, o2 = [_from_residues(og_refs[g], scr_ref, dil) for g, dil in enumerate(DILATIONS)]
        ob = ((w0 * o0 + w1 * o1 + w2 * o2) / den).astype(BF16)
        ob_ref[...] = ob
        lse = lmax + jnp.log2(den)
        for g, (dil, ref) in enumerate(zip(DILATIONS, (lse0_ref, lse1_ref, lse2_ref))):
            _to_residues(lse, ref, scr_ref, dil, F32)
        ya = _dot_nn(oa_ref[...], woa_ref[...])
        yb = _dot_nt(ob, wob_ref[...])
        ya_ref[...] = ya.astype(BF16)
        yb_ref[...] = yb.astype(BF16)
        u = (ga_ref[...].astype(F32) * ya + gb_ref[...].astype(F32) * yb).astype(BF16)
        u_ref[...] = u
        x2_ref[...] = x_ref[...] + _dot_nn(u, wo_ref[...])

    sd = jax.ShapeDtypeStruct
    res = list(pl.pallas_call(
        body, name="mix_out", grid=(s // tb,),
        in_specs=[_rows(tb, d), _rows(tb, QA_W)] + _dil_specs(tb) * 2 + [
            _rows(tb, d), _rows(tb, d), _resident(w_oa.shape), _resident(w_ob_t.shape), _resident(w_o.shape)],
        out_specs=[_rows(tb, d), _rows(tb, GB_W)] + _dil_specs(tb) + [_rows(tb, d), _rows(tb, d), _rows(tb, d)],
        out_shape=[sd((s, d), F32), sd((s, GB_W), BF16)] + _dil_shapes(s, F32) + [
            sd((s, d), BF16), sd((s, d), BF16), sd((s, d), BF16)],
        scratch_shapes=[pltpu.VMEM((2, tb, LANES), F32)],
        compiler_params=_cparams(("arbitrary",)))(x, oa, *og, *lg, ga, gb, w_oa, w_ob_t, w_o))
    return res[:2] + [res[2:5]] + res[5:]


def _mlp_fwd(x2, w1_t, w2, g_mlp, tb, tc):
    s, d = x2.shape
    dff = w1_t.shape[0]

    def body(x_ref, w1_ref, w2_ref, g_ref, x3_ref, r_ref, h_ref):
        xv = x_ref[...]
        hb = (xv * _rstd(xv) * g_ref[...]).astype(BF16)
        h_ref[...] = hb
        x3_ref[...] = xv
        for c in range(dff // tc):
            sl = slice(tc * c, tc * c + tc)
            r = jnp.maximum(_dot_nt(hb, w1_ref[sl, :]), 0.0)
            r_ref[:, sl] = r.astype(BF16)
            x3_ref[...] += _dot_nn((r * r).astype(BF16), w2_ref[sl, :])

    sd = jax.ShapeDtypeStruct
    return pl.pallas_call(
        body, name="mlp_fwd", grid=(s // tb,),
        in_specs=[_rows(tb, d), _resident(w1_t.shape), _resident(w2.shape), _resident(g_mlp.shape)],
        out_specs=[_rows(tb, d), _rows(tb, dff), _rows(tb, d)],
        out_shape=[sd((s, d), F32), sd((s, dff), BF16), sd((s, d), BF16)],
        compiler_params=_cparams(("arbitrary",)))(x2, w1_t, w2, g_mlp)


def _ple_loss(x3, p, target, w_pg, w_p_t, g_ple, g_fin, tb):
    s, d = x3.shape
    dp = p.shape[1]

    def body(x_ref, p_ref, t_ref, wpg_ref, wp_ref, gple_ref, gfin_ref,
             dx3_ref, h3_ref, dpre_ref, dpe_ref, pb_ref, loss_ref, dgfin_ref, dgple_ref):
        @pl.when(pl.program_id(0) == 0)
        def _():
            loss_ref[...] = jnp.zeros_like(loss_ref)
            dgfin_ref[...] = jnp.zeros_like(dgfin_ref)
            dgple_ref[...] = jnp.zeros_like(dgple_ref)

        x3v = x_ref[...]
        r3 = _rstd(x3v)
        n3 = x3v * r3
        h3 = (n3 * gple_ref[...]).astype(BF16)
        h3_ref[...] = h3
        gp = _sigmoid(_dot_nn(h3, wpg_ref[...]))
        pb = p_ref[...].astype(BF16)
        pb_ref[...] = pb
        pe = _dot_nt(pb, wp_ref[...])
        x4 = x3v + gp * pe
        r4 = _rstd(x4)
        n4 = x4 * r4
        err = n4 * gfin_ref[...] - t_ref[...]
        loss_ref[...] += jnp.sum(0.5 * jnp.mean(err * err, axis=-1, keepdims=True), axis=0, keepdims=True)
        dy = err / d
        dgfin_ref[...] += _colsum(dy * n4)
        dx4 = _rms_bwd(dy, n4, r4, gfin_ref[...])
        dpe_ref[...] = (dx4 * gp).astype(BF16)
        dpre = (dx4 * pe * gp * (1.0 - gp)).astype(BF16)
        dpre_ref[...] = dpre
        dh3 = _dot_nt(dpre, wpg_ref[...])
        dgple_ref[...] += _colsum(dh3 * n3)
        dx3_ref[...] = dx4 + _rms_bwd(dh3, n3, r3, gple_ref[...])

    sd = jax.ShapeDtypeStruct
    return pl.pallas_call(
        body, name="ple_loss", grid=(s // tb,),
        in_specs=[_rows(tb, d), _rows(tb, dp), _rows(tb, d), _resident(w_pg.shape), _resident(w_p_t.shape),
                  _resident(g_ple.shape), _resident(g_fin.shape)],
        out_specs=[_rows(tb, d), _rows(tb, d), _rows(tb, d), _rows(tb, d), _rows(tb, dp),
                   _acc_spec((1, LANES)), _acc_spec((1, d)), _acc_spec((1, d))],
        out_shape=[sd((s, d), F32), sd((s, d), BF16), sd((s, d), BF16), sd((s, d), BF16), sd((s, dp), BF16),
                   sd((1, LANES), F32), sd((1, d), F32), sd((1, d), F32)],
        compiler_params=_cparams(("arbitrary",)))(x3, p, target, w_pg, w_p_t, g_ple, g_fin)


def _mlp_bwd(dx3, x2, r, w1_t, w2, g_mlp, tb, tc):
    s, d = x2.shape
    dff = w1_t.shape[0]

    def body(dx3_ref, x_ref, r_ref, w1_ref, w2_ref, g_ref, dx2_ref, df_ref, dg_ref, dh_ref):
        @pl.when(pl.program_id(0) == 0)
        def _():
            dg_ref[...] = jnp.zeros_like(dg_ref)

        dx3v = dx3_ref[...]
        dx3b = dx3v.astype(BF16)
        dh_ref[...] = jnp.zeros_like(dh_ref)
        for c in range(dff // tc):
            sl = slice(tc * c, tc * c + tc)
            df = (_dot_nt(dx3b, w2_ref[sl, :]) * (2.0 * r_ref[:, sl].astype(F32))).astype(BF16)
            df_ref[:, sl] = df
            dh_ref[...] += _dot_nn(df, w1_ref[sl, :])
        xv = x_ref[...]
        r2 = _rstd(xv)
        n2 = xv * r2
        dh = dh_ref[...]
        dg_ref[...] += _colsum(dh * n2)
        dx2_ref[...] = dx3v + _rms_bwd(dh, n2, r2, g_ref[...])

    sd = jax.ShapeDtypeStruct
    return pl.pallas_call(
        body, name="mlp_bwd", grid=(s // tb,),
        in_specs=[_rows(tb, d), _rows(tb, d), _rows(tb, dff), _resident(w1_t.shape), _resident(w2.shape),
                  _resident(g_mlp.shape)],
        out_specs=[_rows(tb, d), _rows(tb, dff), _acc_spec((1, d))],
        out_shape=[sd((s, d), F32), sd((s, dff), BF16), sd((1, d), F32)],
        scratch_shapes=[pltpu.VMEM((tb, d), F32)],
        compiler_params=_cparams(("arbitrary",)))(dx3, x2, r, w1_t, w2, g_mlp)


def _mix_out_bwd(dx2, ya, yb, ga, gb, ob, w_oa, w_ob_t, w_o, tb, after):
    s, d = dx2.shape

    def body(dx_ref, ya_ref, yb_ref, ga_ref, gb_ref, ob_ref, woa_ref, wob_ref, wo_ref, after_ref,
             doa_ref, dob0_ref, dob1_ref, dob2_ref, dd0_ref, dd1_ref, dd2_ref, dga_ref, dgb_ref, dya_ref, dyb_ref,
             dbg_ref, scr_ref):
        @pl.when(pl.program_id(0) == 0)
        def _():
            dbg_ref[...] = jnp.zeros_like(dbg_ref)

        du = _dot_nt(dx_ref[...].astype(BF16), wo_ref[...])
        gav, gbv = ga_ref[...].astype(F32), gb_ref[...].astype(F32)
        dya = (du * gav).astype(BF16)
        dyb = (du * gbv).astype(BF16)
        dya_ref[...] = dya
        dyb_ref[...] = dyb
        dga = du * ya_ref[...].astype(F32) * gav * (1.0 - gav)
        dgb = du * yb_ref[...].astype(F32) * gbv * (1.0 - gbv)
        dga_ref[...] = dga.astype(BF16)
        dgb_ref[...] = dgb.astype(BF16)
        dbg_ref[:, 0:d] += _colsum(dga)
        dbg_ref[:, d:2 * d] += _colsum(dgb)
        doa_ref[...] = _dot_nt(dya, woa_ref[...]).astype(BF16)
        dob = _dot_nn(dyb, wob_ref[...])
        dd = _seg_sum(dob * ob_ref[...].astype(F32))
        for dil, dob_ref, dd_ref in zip(DILATIONS, (dob0_ref, dob1_ref, dob2_ref), (dd0_ref, dd1_ref, dd2_ref)):
            _to_residues(dob, dob_ref, scr_ref, dil, BF16)
            _to_residues(dd, dd_ref, scr_ref, dil, F32)

    sd = jax.ShapeDtypeStruct
    res = list(pl.pallas_call(
        body, name="mix_out_bwd", grid=(s // tb,),
        in_specs=[_rows(tb, d)] * 5 + [_rows(tb, GB_W), _resident(w_oa.shape), _resident(w_ob_t.shape),
                                       _resident(w_o.shape), _ANY],
        out_specs=[_rows(tb, QA_W)] + _dil_specs(tb) * 2 + [_rows(tb, d), _rows(tb, d), _rows(tb, d),
                                                           _rows(tb, d), _acc_spec((1, 2 * d))],
        out_shape=[sd((s, QA_W), BF16)] + _dil_shapes(s, BF16) + _dil_shapes(s, F32) + [
            sd((s, d), BF16), sd((s, d), BF16), sd((s, d), BF16), sd((s, d), BF16), sd((1, 2 * d), F32)],
        scratch_shapes=[pltpu.VMEM((2, tb, LANES), F32)],
        compiler_params=_cparams(("arbitrary",)))(dx2, ya, yb, ga, gb, ob, w_oa, w_ob_t, w_o, after))
    return res[:1] + [res[1:4], res[4:7]] + res[7:]


def _in_proj_bwd(dx2, x, dqrot, dkrot, dva, qraw, kraw, tabs, dqb, dkb, dvb, dga, dgb, w_in_t, g_mix, q_g, k_g, tb):
    s, d = x.shape
    din = w_in_t.shape[0]
    q_scale = HEAD_DIM_A ** -0.5
    b_scale = HEAD_DIM_B ** -0.5
    tc = 256

    def body(dx2_ref, x_ref, dq_ref, dk_ref, dv_ref, qraw_ref, kraw_ref, c_ref, s1_ref, s2_ref, *rest):
        dqb_refs, dkb_refs, dvb_refs = rest[0:3], rest[3:6], rest[6:9]
        (dga_ref, dgb_ref, w_ref, gmix_ref, qg_ref, kg_ref,
         dx_ref, dz_ref, dgmix_ref, dqg_ref, dkg_ref, dh_ref, scr_ref) = rest[9:]

        @pl.when(pl.program_id(0) == 0)
        def _():
            dgmix_ref[...] = jnp.zeros_like(dgmix_ref)
            dqg_ref[...] = jnp.zeros_like(dqg_ref)
            dkg_ref[...] = jnp.zeros_like(dkg_ref)

        cos, s1, s2 = c_ref[...][None], s1_ref[...][None], s2_ref[...][None]

        def heads_bwd(drot, z, g_ref, acc_ref):
            dn = drot * cos + pltpu.roll(drot * s1, 96, 2) + pltpu.roll(drot * s2, 32, 2)
            rr = _rstd(z)
            nn = z * rr
            acc_ref[...] += jnp.sum(jnp.sum(dn * nn, axis=0), axis=0, keepdims=True)
            return _rms_bwd(dn, nn, rr, g_ref[...][None]).astype(BF16)

        dh_ref[...] = jnp.zeros_like(dh_ref)

        def emit(off, piece):
            dz_ref[:, off:off + tc] = piece
            dh_ref[...] += _dot_nn(piece, w_ref[off:off + tc, :])

        for j in range(d // tc):
            emit(OFF_GA + tc * j, dga_ref[:, tc * j:tc * j + tc])
            emit(OFF_GA + d + tc * j, dgb_ref[:, tc * j:tc * j + tc])
        emit(OFF_VA, dv_ref[...].T.astype(BF16))
        for g, dil in enumerate(DILATIONS):
            emit(OFF_QB + GB_W * g, (_from_residues(dqb_refs[g], scr_ref, dil) * b_scale).astype(BF16))
            emit(OFF_KB + GB_W * g, (_from_residues(dkb_refs[g], scr_ref, dil) * LN_2).astype(BF16))
            emit(OFF_VB + GB_W * g, _from_residues(dvb_refs[g], scr_ref, dil).astype(BF16))
        stack = lambda ref, n: jnp.stack([ref[:, 128 * h:128 * h + 128] for h in range(n)], axis=0)
        dzq = heads_bwd(stack(dq_ref, N_Q_HEADS_A) * q_scale, stack(qraw_ref, N_Q_HEADS_A), qg_ref, dqg_ref)
        dkt = jnp.stack([dk_ref[128 * h:128 * h + 128, :].T for h in range(N_KV_HEADS_A)], axis=0) * LN_2
        dzk = heads_bwd(dkt, stack(kraw_ref, N_KV_HEADS_A), kg_ref, dkg_ref)
        for j in range(N_Q_HEADS_A // 2):
            emit(OFF_QA + tc * j, jnp.concatenate([dzq[2 * j], dzq[2 * j + 1]], axis=1))
        emit(OFF_KA, jnp.concatenate([dzk[0], dzk[1]], axis=1))
        xv = x_ref[...]
        r1 = _rstd(xv)
        n1 = xv * r1
        dh = dh_ref[...]
        dgmix_ref[...] += _colsum(dh * n1)
        dx_ref[...] = dx2_ref[...] + _rms_bwd(dh, n1, r1, gmix_ref[...])

    sd = jax.ShapeDtypeStruct
    return pl.pallas_call(
        body, name="in_proj_bwd", grid=(s // tb,),
        in_specs=[_rows(tb, d), _rows(tb, d), _rows(tb, QA_W), pl.BlockSpec((KA_W, tb), lambda i: (0, i)),
                  pl.BlockSpec((KA_W, tb), lambda i: (0, i)), _rows(tb, QA_W),
                  _rows(tb, KA_W), _rows(tb, LANES), _rows(tb, LANES), _rows(tb, LANES),
                  ] + _dil_specs(tb) * 3 + [_rows(tb, d), _rows(tb, d),
                  _resident(w_in_t.shape), _resident(g_mix.shape), _resident(q_g.shape), _resident(k_g.shape)],
        out_specs=[_rows(tb, d), _rows(tb, din), _acc_spec((1, d)), _acc_spec((1, HEAD_DIM_A)),
                   _acc_spec((1, HEAD_DIM_A))],
        out_shape=[sd((s, d), F32), sd((s, din), BF16), sd((1, d), F32), sd((1, HEAD_DIM_A), F32),
                   sd((1, HEAD_DIM_A), F32)],
        scratch_shapes=[pltpu.VMEM((tb, d), F32), pltpu.VMEM((2, tb, LANES), F32)],
        compiler_params=_cparams(("arbitrary",)))(
        dx2, x, dqrot, dkrot, dva, qraw, kraw, *tabs, *dqb, *dkb, *dvb, dga, dgb, w_in_t, g_mix, q_g, k_g)


def _identity(v):
    return v


def _to_bf16(v):
    return v.astype(BF16)


def _square_bf16(v):
    vf = v.astype(F32)
    return (vf * vf).astype(BF16)


def _weight_grad(name, a, b, ti, tj, tk, a_fn=_identity, b_fn=_identity, col0=0, n=None, after=None):
    t, m = a.shape
    n = b.shape[1] if n is None else n
    n_k = t // tk
    after = a if after is None else after

    def body(a_ref, b_ref, after_ref, o_ref, acc_ref):
        k = pl.program_id(2)

        @pl.when(k == 0)
        def _():
            acc_ref[...] = jnp.zeros_like(acc_ref)

        acc_ref[...] += _dot_tn(a_fn(a_ref[...]), b_fn(b_ref[...]))

        @pl.when(k == n_k - 1)
        def _():
            o_ref[...] = acc_ref[...].astype(BF16)

    return pl.pallas_call(
        body, name=name, grid=(m // ti, n // tj, n_k),
        in_specs=[pl.BlockSpec((tk, ti), lambda i, j, k: (k, i)),
                  pl.BlockSpec((tk, tj), lambda i, j, k: (k, j + col0 // tj)), _ANY],
        out_specs=pl.BlockSpec((ti, tj), lambda i, j, k: (i, j)),
        out_shape=jax.ShapeDtypeStruct((m, n), BF16),
        scratch_shapes=[pltpu.VMEM((ti, tj), F32)],
        compiler_params=_cparams(("arbitrary", "arbitrary", "arbitrary")))(a, b, after)


def _sum_slots(name, recv, own):
    m, n, k = recv.shape
    tc = min(k, 256)

    def body(own_ref, r_ref, o_ref):
        acc = own_ref[...].astype(F32)
        for i in range(m):
            acc = acc + r_ref[i].astype(F32)
        o_ref[...] = acc

    return pl.pallas_call(
        body, name=name, grid=(k // tc,),
        in_specs=[pl.BlockSpec((n, tc), lambda j: (0, j)), pl.BlockSpec((m, n, tc), lambda j: (0, 0, j))],
        out_specs=pl.BlockSpec((n, tc), lambda j: (0, j)),
        out_shape=jax.ShapeDtypeStruct((n, k), F32),
        compiler_params=_cparams(("arbitrary",)))(own, recv)


def _adamw_math(w, g, m, v):
    m = ADAM_B1 * m + (1.0 - ADAM_B1) * g
    v = ADAM_B2 * v + (1.0 - ADAM_B2) * (g * g)
    m_hat = m / (1.0 - ADAM_B1 ** ADAM_STEP)
    v_hat = v / (1.0 - ADAM_B2 ** ADAM_STEP)
    delta = -ADAM_LR * (m_hat / (jnp.sqrt(v_hat) + ADAM_EPS) + ADAM_WD * w)
    return delta, m, v


def _adamw(name, w, g, m, v):
    r, c = w.shape
    tr = min(r, 256)

    def body(w_ref, g_ref, m_ref, v_ref, d_ref, mo_ref, vo_ref):
        d_ref[...], mo_ref[...], vo_ref[...] = _adamw_math(w_ref[...], g_ref[...], m_ref[...], v_ref[...])

    spec = pl.BlockSpec((tr, c), lambda i: (i, 0))
    return pl.pallas_call(
        body, name=name, grid=(r // tr,), in_specs=[spec] * 4, out_specs=[spec] * 3,
        out_shape=[jax.ShapeDtypeStruct((r, c), F32)] * 3,
        compiler_params=_cparams(("arbitrary",)))(w, g, m, v)


def _small_update(parts, w, m, v):
    def body(p_ref, w_ref, m_ref, v_ref, g_ref, d_ref, mo_ref, vo_ref):
        g = p_ref[0]
        for i in range(1, N_DEV):
            g = g + p_ref[i]
        g_ref[...] = g
        d_ref[...], mo_ref[...], vo_ref[...] = _adamw_math(w_ref[...], g, m_ref[...], v_ref[...])

    return pl.pallas_call(body, name="small_update", out_shape=[jax.ShapeDtypeStruct(w.shape, F32)] * 4)(
        parts, w, m, v)


def _pack_rows(vectors, n_rows):
    flat = jnp.concatenate([v.reshape(-1).astype(F32) for v in vectors])
    flat = jnp.pad(flat, (0, n_rows * LANES - flat.shape[0]))
    return flat.reshape(n_rows, LANES)


def _pick_tile(n, prefs):
    for t in prefs:
        if n % t == 0:
            return t
    return n


def kernel(x, p, norm_mix_g, w_in, b_gate, q_norm_g, k_norm_g, rel_bias, w_out_a, w_out_b, w_out, norm_mlp_g, w_ff1, w_ff2, norm_ple_g, w_ple_gate, w_ple, final_norm_g, loss_target, m_norm_mix_g, m_w_in, m_b_gate, m_q_norm_g, m_k_norm_g, m_rel_bias, m_w_out_a, m_w_out_b, m_w_out, m_norm_mlp_g, m_w_ff1, m_w_ff2, m_norm_ple_g, m_w_ple_gate, m_w_ple, m_final_norm_g, v_norm_mix_g, v_w_in, v_b_gate, v_q_norm_g, v_k_norm_g, v_rel_bias, v_w_out_a, v_w_out_b, v_w_out, v_norm_mlp_g, v_w_ff1, v_w_ff2, v_norm_ple_g, v_w_ple_gate, v_w_ple, v_final_norm_g):
    s, d = x.shape[1], x.shape[2]
    xs, ps, ts = x[0], p[0, 0], loss_target[0]
    tb = _pick_tile(s, (512, 256))
    tq = _pick_tile(s, (256,))
    tk = _pick_tile(s, (1024, 512))
    cb = _pick_tile(s, (1024, 512))
    fin_g = final_norm_g.reshape(1, d)

    col_sharded = {"w_in": w_in[0], "w_out_b": w_out_b[0], "w_ff1": w_ff1[0], "w_ple": w_ple[0]}
    row_sharded = {"w_out_a": w_out_a[0], "w_out": w_out[0], "w_ff2": w_ff2[0], "w_ple_gate": w_ple_gate[0]}
    order = ["w_in", "w_out_a", "w_out_b", "w_out", "w_ff1", "w_ff2", "w_ple_gate", "w_ple"]
    shards = [(col_sharded[n].T if n in col_sharded else row_sharded[n]).astype(BF16) for n in order]
    my_idx = 4 * lax.axis_index("x") + 2 * lax.axis_index("y") + lax.axis_index("c")
    (w_in_t,) = _all_gather(shards[:1], 1)
    zones = _place_own_rows(shards[1:], my_idx)
    ag = _copies_start("weights_gather_start", shards[1:], zones, w_in_t, True)

    tabs = _rope_tables(s)
    (h1, qraw, kraw, qrot, krot, va, qb, kb, vb, ga, gb) = _in_proj(
        xs, tabs, w_in_t, norm_mix_g, b_gate, q_norm_g, k_norm_g, tb, ag[4])
    oa, lse_a = _attn_a_fwd(qrot, krot, va, tq, tk)
    _, (w_oa, w_ob_t, w_o, w_ff1_t, w_ff2_f, w_pg, w_p_t) = _copies_wait(
        "weights_gather_wait", ag[0], ag[1], ag[2], ag[3], lse_a, True)
    flat = lambda arrs: [a.reshape(s, GB_W) for a in arrs]
    split = lambda arrs: [a.reshape(dil, s // dil, GB_W) for a, dil in zip(arrs, DILATIONS)]
    qb_r, kb_r, vb_r = flat(qb), flat(kb), flat(vb)
    bmaps = [[jnp.asarray(m) for m in _bucket_maps(dil)] for dil in DILATIONS]
    bias_tabs = [rel_bias[:, N_HEADS_PER_DIL * g:N_HEADS_PER_DIL * (g + 1)] for g in range(3)]
    band_out = [_band_fwd(dil, qb_r[g], kb_r[g], vb_r[g], bmaps[g][0], bias_tabs[g], cb)
                for g, dil in enumerate(DILATIONS)]
    og, lg = split([o for o, _ in band_out]), split([l for _, l in band_out])
    x2, ob, lse_b, ya, yb, u = _mix_out(xs, oa, og, lg, ga, gb, w_oa, w_ob_t, w_o, tb)
    tc = _pick_tile(w_ff1_t.shape[0], (512,))
    x3, r_act, h2 = _mlp_fwd(x2, w_ff1_t, w_ff2_f, norm_mlp_g, tb, tc)

    dx3, h3, dpre, dpe, pb, loss_part, dg_fin, dg_ple = _ple_loss(
        x3, ps, ts, w_pg, w_p_t, norm_ple_g, fin_g, tb)
    dx2, df, dg_mlp = _mlp_bwd(dx3, x2, r_act, w_ff1_t, w_ff2_f, norm_mlp_g, tb, tc)

    tkk = _pick_tile(s, (1024, 512))
    dff = w_ff1_t.shape[0]
    t1k = lambda n: _pick_tile(n, (1024, 512, 256))
    slots = lambda parts: [lax.empty((7, a.shape[0] // N_DEV, a.shape[1]), BF16) for a in parts]
    part1 = [_weight_grad("grad_w_ff1", df, h2, t1k(dff), t1k(d), tkk),
             _weight_grad("grad_w_ff2", r_act, dx3, t1k(dff), t1k(d), tkk, a_fn=_square_bf16, b_fn=_to_bf16),
             _weight_grad("grad_w_ple_gate", h3, dpre, t1k(d), t1k(d), tkk),
             _weight_grad("grad_w_ple", dpe, pb, t1k(d), ps.shape[1], tkk)]
    doa, dob, dd, dga, dgb, dya, dyb, dbg = _mix_out_bwd(dx2, ya, yb, ga, gb, ob, w_oa, w_ob_t, w_o, tb, dx2)
    part1 += [_weight_grad("grad_w_out_a", oa, dya, t1k(QA_W), t1k(d), tkk),
              _weight_grad("grad_w_out_b", dyb, ob, t1k(d), GB_W, tkk),
              _weight_grad("grad_w_out", u, dx2, t1k(d), t1k(d), tkk, b_fn=_to_bf16)]
    rs1 = _copies_start("grads1_start", part1, slots(part1), doa, False)
    dqrot, dkrot, dva = _attn_a_bwd(qrot, krot, va, oa, doa, lse_a, tq, tk, rs1[4])
    dob_r, lse_r, dd_r = flat(dob), flat(lse_b), flat(dd)
    bwd_q = [_band_bwd_q(dil, qb_r[g], kb_r[g], vb_r[g], dob_r[g], lse_r[g], dd_r[g], bmaps[g][0], bias_tabs[g], cb)
             for g, dil in enumerate(DILATIONS)]
    bwd_kv = [_band_bwd_kv(dil, qb_r[g], kb_r[g], vb_r[g], dob_r[g], lse_r[g], dd_r[g], bmaps[g][1], bias_tabs[g], cb)
              for g, dil in enumerate(DILATIONS)]
    dqb, dkb, dvb = split([r[0] for r in bwd_q]), split([r[0] for r in bwd_kv]), split([r[1] for r in bwd_kv])
    grad_x, dz, dg_mix, dg_q, dg_k = _in_proj_bwd(
        dx2, xs, dqrot, dkrot, dva, qraw, kraw, tabs, dqb, dkb, dvb, dga, dgb, w_in_t, norm_mix_g,
        q_norm_g, k_norm_g, _pick_tile(s, (256,)))
    d_rel = jnp.concatenate([r[1][:, :N_HEADS_PER_DIL] for r in bwd_q], axis=1)

    din = w_in_t.shape[0]
    ti_in = _pick_tile(din, (din // 2,)) if (din // 2) % LANES == 0 else din
    hd_ = d // 2
    part3 = [_weight_grad("grad_w_in_lo", dz, h1, ti_in, t1k(hd_), tkk, n=hd_)]
    rs3 = _copies_start("grads3_start", part3, slots(part3), grad_x, False)
    part4 = [_weight_grad("grad_w_in_hi", dz, h1, ti_in, t1k(hd_), tkk, col0=hd_, n=hd_, after=rs3[4])]
    rs4 = _copies_start("grads4_start", part4, slots(part4), rs3[4], False)

    def own_rows(a):
        n = a.shape[0] // N_DEV
        return lax.dynamic_slice(a, (my_idx * n, 0), (n, a.shape[1]))

    sums = {}
    src1, got1 = _copies_wait("grads1_wait", rs1[0], rs1[1], rs1[2], rs1[3], rs4[4], False)
    for n, a, r in zip(["w_ff1", "w_ff2", "w_ple_gate", "w_ple", "w_out_a", "w_out_b", "w_out"], src1, got1):
        sums[n] = _sum_slots("sum_" + n, r, own_rows(a))
    given_w = dict(w_in=w_in, w_out_a=w_out_a, w_out_b=w_out_b, w_out=w_out, w_ff1=w_ff1, w_ff2=w_ff2,
                   w_ple_gate=w_ple_gate, w_ple=w_ple)
    given_m = dict(w_in=m_w_in, w_out_a=m_w_out_a, w_out_b=m_w_out_b, w_out=m_w_out, w_ff1=m_w_ff1, w_ff2=m_w_ff2,
                   w_ple_gate=m_w_ple_gate, w_ple=m_w_ple)
    given_v = dict(w_in=v_w_in, w_out_a=v_w_out_a, w_out_b=v_w_out_b, w_out=v_w_out, w_ff1=v_w_ff1, w_ff2=v_w_ff2,
                   w_ple_gate=v_w_ple_gate, w_ple=v_w_ple)
    big = {}

    def update(n):
        g = sums[n].T if n in col_sharded else sums[n]
        delta, new_m, new_v = _adamw("adamw_" + n, given_w[n][0], g, given_m[n][0], given_v[n][0])
        big[n] = tuple(a[None] for a in (g, delta, new_m, new_v))

    for n in order[1:]:
        update(n)
    src3, got3 = _copies_wait("grads3_wait", rs3[0], rs3[1], rs3[2], rs3[3], big["w_ple"][1], False)
    src4, got4 = _copies_wait("grads4_wait", rs4[0], rs4[1], rs4[2], rs4[3], big["w_ple"][1], False)
    sums["w_in"] = jnp.concatenate([_sum_slots("sum_w_in_lo", got3[0], own_rows(src3[0])),
                                    _sum_slots("sum_w_in_hi", got4[0], own_rows(src4[0]))], axis=1)
    update("w_in")

    small_names = ["norm_mix_g", "b_gate", "q_norm_g", "k_norm_g", "rel_bias", "norm_mlp_g", "norm_ple_g",
                   "final_norm_g"]
    small_w = [norm_mix_g, b_gate, q_norm_g, k_norm_g, rel_bias, norm_mlp_g, norm_ple_g, final_norm_g]
    small_m = [m_norm_mix_g, m_b_gate, m_q_norm_g, m_k_norm_g, m_rel_bias, m_norm_mlp_g, m_norm_ple_g,
               m_final_norm_g]
    small_v = [v_norm_mix_g, v_b_gate, v_q_norm_g, v_k_norm_g, v_rel_bias, v_norm_mlp_g, v_norm_ple_g,
               v_final_norm_g]
    small_g = [dg_mix, dbg, dg_q, dg_k, d_rel, dg_mlp, dg_ple, dg_fin]
    sizes = [int(np.prod(w.shape)) for w in small_w]
    n_rows = -(-(sum(-(-sz // LANES) for sz in sizes) + 1) // 8) * 8
    pad = lambda v: jnp.pad(v.reshape(-1).astype(F32), (0, -v.size % LANES))
    pack = lambda vs, last: _pack_rows([pad(v) for v in vs] + [last], n_rows)
    zero_row = jnp.zeros((LANES,), F32)
    parts = _small_all_gather(pack(small_g, loss_part.reshape(-1) * (jnp.arange(LANES) == 0)))
    g_all, d_all, m_all, v_all = _small_update(parts, pack(small_w, zero_row), pack(small_m, zero_row),
                                               pack(small_v, zero_row))
    small = {}
    row = 0
    for n, w, sz in zip(small_names, small_w, sizes):
        nr = -(-sz // LANES)
        small[n] = tuple(a[row:row + nr].reshape(-1)[:sz].reshape(w.shape) for a in (g_all, d_all, m_all, v_all))
        row += nr
    loss = g_all[row, 0]

    names = ["norm_mix_g", "w_in", "b_gate", "q_norm_g", "k_norm_g", "rel_bias", "w_out_a", "w_out_b", "w_out",
             "norm_mlp_g", "w_ff1", "w_ff2", "norm_ple_g", "w_ple_gate", "w_ple", "final_norm_g"]
    res = {n: (big[n] if n in big else small[n]) for n in names}
    return (loss, grad_x[None], *[res[n][0] for n in names], *[res[n][1] for n in names],
            *[res[n][2] for n in names], *[res[n][3] for n in names])
```

```python
import functools
import math

import numpy as np
import jax
import jax.numpy as jnp
from jax import lax
from jax.experimental import pallas as pl
from jax.experimental.pallas import tpu as pltpu

F32 = jnp.float32
BF16 = jnp.bfloat16
MESH = pl.DeviceIdType.MESH

NORM_EPS = 1e-6
NEG_INF = -1e30
LOG2_E = math.log2(math.e)
LN_2 = math.log(2.0)
GRID_W = 64
ROPE_THETA = 10000.0
HEAD_DIM_A = 128
N_Q_HEADS_A = 8
N_KV_HEADS_A = 2
Q_PER_KV = N_Q_HEADS_A // N_KV_HEADS_A
HEAD_DIM_B = 64
N_HEADS_PER_DIL = 4
DILATIONS = (1, 4, 16)
BAND = 64
N_REL_BUCKETS = 32
REL_MAX_DIST = 1024
QA_W = N_Q_HEADS_A * HEAD_DIM_A
KA_W = N_KV_HEADS_A * HEAD_DIM_A
GB_W = N_HEADS_PER_DIL * HEAD_DIM_B
QB_W = GB_W * len(DILATIONS)
OFF_QA, OFF_KA, OFF_VA = 0, QA_W, QA_W + KA_W
OFF_QB = QA_W + 2 * KA_W
OFF_KB = OFF_QB + QB_W
OFF_VB = OFF_KB + QB_W
OFF_GA = OFF_VB + QB_W
N_DEV = 8
LANES = 128
VMEM_LIMIT = 56 * 2 ** 20

ADAM_LR, ADAM_B1, ADAM_B2, ADAM_EPS, ADAM_WD, ADAM_STEP = 0.001, 0.9, 0.999, 1e-08, 0.01, 10


def _cparams(sem):
    return pltpu.CompilerParams(dimension_semantics=sem, vmem_limit_bytes=VMEM_LIMIT)


def _resident(shape):
    nd = len(shape)
    return pl.BlockSpec(shape, lambda *_: (0,) * nd, pipeline_mode=pl.Buffered(1))


def _acc_spec(shape):
    nd = len(shape)
    return pl.BlockSpec(shape, lambda *_: (0,) * nd)


def _rows(tb, c):
    return pl.BlockSpec((tb, c), lambda i: (i, 0))


def _dil_shapes(s, dtype):
    return [jax.ShapeDtypeStruct((dil, s // dil, GB_W), dtype) for dil in DILATIONS]


def _dil_specs(tb):
    return [pl.BlockSpec((dil, tb // dil, GB_W), lambda i: (0, i, 0)) for dil in DILATIONS]


def _to_residues(val, out_ref, scr_ref, dil, dtype):
    if dil == 1:
        out_ref[0] = val.astype(dtype)
        return
    n = val.shape[0] // dil
    scr_ref[0] = val[:, :LANES]
    scr_ref[1] = val[:, LANES:]
    for r in range(dil):
        out_ref[r] = jnp.concatenate([scr_ref[0, pl.ds(r, n, stride=dil), :],
                                      scr_ref[1, pl.ds(r, n, stride=dil), :]], axis=1).astype(dtype)


def _from_residues(in_ref, scr_ref, dil):
    if dil == 1:
        return in_ref[0]
    n = in_ref.shape[1]
    for r in range(dil):
        v = in_ref[r]
        scr_ref[0, pl.ds(r, n, stride=dil), :] = v[:, :LANES]
        scr_ref[1, pl.ds(r, n, stride=dil), :] = v[:, LANES:]
    return jnp.concatenate([scr_ref[0], scr_ref[1]], axis=1)


def _dot_nt(a, b):
    return lax.dot_general(a, b, (((1,), (1,)), ((), ())), preferred_element_type=F32)


def _dot_nn(a, b):
    return lax.dot_general(a, b, (((1,), (0,)), ((), ())), preferred_element_type=F32)


def _dot_tn(a, b):
    return lax.dot_general(a, b, (((0,), (0,)), ((), ())), preferred_element_type=F32)


def _rstd(x):
    return lax.rsqrt(jnp.mean(x * x, axis=-1, keepdims=True) + NORM_EPS)


def _rms_bwd(dy, n, r, g):
    dn = dy * g
    return r * (dn - n * jnp.mean(dn * n, axis=-1, keepdims=True))


def _colsum(v):
    return jnp.sum(v, axis=0, keepdims=True)


def _sigmoid(v):
    return 1.0 / (1.0 + jnp.exp(-v))


def _rope_tables(s):
    half = HEAD_DIM_A // 2
    inv = np.power(np.float32(ROPE_THETA), -np.arange(0, half, 2, dtype=np.float32) / np.float32(half))
    t = np.arange(s)
    ang_r = (t // GRID_W).astype(np.float32)[:, None] * inv[None, :]
    ang_c = (t % GRID_W).astype(np.float32)[:, None] * inv[None, :]
    cr, sr, cc, sc = np.cos(ang_r), np.sin(ang_r), np.cos(ang_c), np.sin(ang_c)
    z = np.zeros_like(sr)
    cos = np.concatenate([cr, cr, cc, cc], axis=1)
    s1 = np.concatenate([z, sr, z, sc], axis=1)
    s2 = np.concatenate([-sr, z, -sc, z], axis=1)
    return [jnp.asarray(a, F32) for a in (cos, s1, s2)]


def _my_place():
    return lax.axis_index("x"), lax.axis_index("y"), lax.axis_index("c")


def _all_gather(shards, n_gather):
    n_all = len(shards)
    nw = n_gather

    def body(*refs):
        ins, outs = refs[:n_all], refs[n_all:2 * n_all]
        send_sems, recv_sems, local_sems = refs[2 * n_all:]
        x, y, c = _my_place()
        me, sibling = (x, y, c), (x, y, 1 - c)
        chips = [(1 - x, y), (x, 1 - y), (1 - x, 1 - y)]

        def rows(w, px, py, pc):
            n = ins[w].shape[0]
            return outs[w].at[pl.ds(pl.multiple_of((4 * px + 2 * py + pc) * n, 16), n), :]

        def copy(w, k, block, to, src=None):
            return pltpu.make_async_remote_copy(
                src_ref=rows(w, *block) if src is None else src, dst_ref=rows(w, *block),
                send_sem=send_sems.at[w, k], recv_sem=recv_sems.at[w, k], device_id=to, device_id_type=MESH)

        mine = [pltpu.make_async_copy(ins[w], rows(w, *me), local_sems.at[w]) for w in range(n_all)]
        for cp in mine:
            cp.start()
        first = []
        for w in range(nw):
            first.append(copy(w, 0, me, sibling, src=ins[w]))
            first += [copy(w, 1 + j, me, (*chip, c), src=ins[w]) for j, chip in enumerate(chips)]
        for cp in first:
            cp.start()
        passed = []
        for j, chip in enumerate(chips):
            for w in range(nw):
                copy(w, 1 + j, (*chip, c), me).wait_recv()
                fwd = copy(w, 4 + j, (*chip, c), sibling)
                fwd.start()
                passed.append(fwd)
        for w in range(nw):
            copy(w, 0, sibling, me).wait_recv()
        for j, chip in enumerate(chips):
            for w in range(nw):
                copy(w, 4 + j, (*chip, 1 - c), me).wait_recv()
        for cp in first + passed:
            cp.wait_send()
        for cp in mine:
            cp.wait()

    any_spec = pl.BlockSpec(memory_space=pl.ANY)
    return pl.pallas_call(
        body, name="weights_all_gather",
        out_shape=[jax.ShapeDtypeStruct((N_DEV * s.shape[0], s.shape[1]), s.dtype) for s in shards],
        in_specs=[any_spec] * n_all, out_specs=[any_spec] * n_all,
        scratch_shapes=[pltpu.SemaphoreType.DMA((nw, 7)), pltpu.SemaphoreType.DMA((nw, 7)),
                        pltpu.SemaphoreType.DMA((n_all,))],
    )(*shards)


def _place_own_rows(shards, my_idx):
    nw = len(shards)

    def body(idx_ref, *refs):
        for w in range(nw):
            refs[nw + w][...] = refs[w][...]

    grid_spec = pltpu.PrefetchScalarGridSpec(
        num_scalar_prefetch=1, grid=(1,),
        in_specs=[pl.BlockSpec(s.shape, lambda i, idx: (0, 0)) for s in shards],
        out_specs=[pl.BlockSpec(s.shape, lambda i, idx: (idx[0], 0)) for s in shards])
    return pl.pallas_call(
        body, name="place_own_rows", grid_spec=grid_spec,
        out_shape=[jax.ShapeDtypeStruct((N_DEV * s.shape[0], s.shape[1]), s.dtype) for s in shards],
        compiler_params=_cparams(("arbitrary",)))(my_idx.reshape(1).astype(jnp.int32), *shards)


_FLIPS = [(fx, fy, fc) for fx in (0, 1) for fy in (0, 1) for fc in (0, 1)][1:]


def _small_all_gather(v, after):
    def body(v_ref, after_ref, out_ref, send_sems, recv_sems):
        x, y, c = _my_place()
        my_idx = 4 * x + 2 * y + c
        out_ref[my_idx] = v_ref[...]
        sends = []
        for k, (fx, fy, fc) in enumerate(_FLIPS):
            to = (1 - x if fx else x, 1 - y if fy else y, 1 - c if fc else c)
            sends.append(pltpu.make_async_remote_copy(
                src_ref=v_ref, dst_ref=out_ref.at[my_idx], send_sem=send_sems.at[k], recv_sem=recv_sems.at[k],
                device_id=to, device_id_type=MESH))
        for cp in sends:
            cp.start()
        for k, (fx, fy, fc) in enumerate(_FLIPS):
            frm_idx = 4 * (1 - x if fx else x) + 2 * (1 - y if fy else y) + (1 - c if fc else c)
            pltpu.make_async_remote_copy(
                src_ref=v_ref, dst_ref=out_ref.at[frm_idx], send_sem=send_sems.at[k], recv_sem=recv_sems.at[k],
                device_id=(x, y, c), device_id_type=MESH).wait_recv()
        for cp in sends:
            cp.wait_send()

    vm = pl.BlockSpec(memory_space=pltpu.VMEM)
    return pl.pallas_call(
        body, name="small_all_gather", out_shape=jax.ShapeDtypeStruct((N_DEV,) + v.shape, v.dtype),
        in_specs=[vm, pl.BlockSpec(memory_space=pl.ANY)], out_specs=vm,
        scratch_shapes=[pltpu.SemaphoreType.DMA((7,)), pltpu.SemaphoreType.DMA((7,))],
    )(v, after)


_HBM = pl.BlockSpec(memory_space=pltpu.HBM)
_SEM = pl.BlockSpec(memory_space=pltpu.SEMAPHORE)
_ANY = pl.BlockSpec(memory_space=pl.ANY)
_SPLIT_COPY = dict(has_side_effects=pltpu.SideEffectType.DATAFLOW_SIDE_EFFECTING)


def _peer(x, y, c, k):
    fx, fy, fc = _FLIPS[k]
    return (1 - x if fx else x, 1 - y if fy else y, 1 - c if fc else c)


def _in_hbm(a):
    return pltpu.with_memory_space_constraint(a, pltpu.HBM)


def _split_copies(srcs, lands, send_sems, recv_sems, gather, arriving):
    x, y, c = _my_place()
    my_idx = 4 * x + 2 * y + c
    out = []
    for k in range(7):
        to = _peer(x, y, c, k)
        to_idx = 4 * to[0] + 2 * to[1] + to[2]
        for w in range(len(srcs)):
            if gather:
                n = srcs[w].shape[0]
                src = srcs[w]
                dst = lands[w].at[pl.ds(pl.multiple_of((to_idx if arriving else my_idx) * n, 16), n), :]
            else:
                n = lands[w].shape[1]
                src = srcs[w].at[pl.ds(pl.multiple_of(to_idx * n, 16), n), :]
                dst = lands[w].at[k]
            out.append(pltpu.make_async_remote_copy(
                src_ref=src, dst_ref=dst, send_sem=send_sems.at[7 * w + k], recv_sem=recv_sems.at[7 * w + k],
                device_id=to, device_id_type=MESH))
    return out


def _copies_start(name, srcs, lands, after, gather):
    nw = len(srcs)

    def body(*refs):
        send_sems, recv_sems = refs[2 * nw + 1], refs[2 * nw + 2]
        for cp in _split_copies(refs[:nw], refs[nw:2 * nw], send_sems, recv_sems, gather, False):
            cp.start()
        refs[-1][...] = jnp.zeros_like(refs[-1])

    sems = pltpu.SemaphoreType.DMA((7 * nw,))
    thru = [pltpu.HBM(a.shape, a.dtype) for a in list(srcs) + list(lands)]
    res = pl.pallas_call(
        body, name=name, out_shape=(sems, sems, *thru, jax.ShapeDtypeStruct((8, LANES), F32)),
        in_specs=[_HBM] * (2 * nw) + [_ANY], out_specs=(_SEM, _SEM, *[_HBM] * (2 * nw), pl.BlockSpec(memory_space=pltpu.VMEM)),
        input_output_aliases={i: 2 + i for i in range(2 * nw)},
        compiler_params=pltpu.CompilerParams(**_SPLIT_COPY),
    )(*[_in_hbm(a) for a in srcs], *[_in_hbm(a) for a in lands], after)
    return res[0], res[1], list(res[2:2 + nw]), list(res[2 + nw:2 + 2 * nw]), res[-1]


def _copies_wait(name, send_sems, recv_sems, srcs, lands, after, gather):
    nw = len(srcs)

    def body(*refs):
        for cp in _split_copies(refs[:nw], refs[nw:2 * nw], refs[2 * nw], refs[2 * nw + 1], gather, False):
            cp.wait_send()
        for cp in _split_copies(refs[:nw], refs[nw:2 * nw], refs[2 * nw], refs[2 * nw + 1], gather, True):
            cp.wait_recv()

    thru = [pltpu.HBM(a.shape, a.dtype) for a in list(srcs) + list(lands)]
    res = pl.pallas_call(
        body, name=name, out_shape=tuple(thru),
        in_specs=[_HBM] * (2 * nw) + [_SEM, _SEM, _ANY], out_specs=tuple([_HBM] * (2 * nw)),
        input_output_aliases={i: i for i in range(2 * nw)},
        compiler_params=pltpu.CompilerParams(**_SPLIT_COPY),
    )(*srcs, *lands, send_sems, recv_sems, after)
    return list(res[:nw]), list(res[nw:])


def _in_proj(x, tabs, w_in_t, g_mix, b_gate, q_g, k_g, tb, after):
    s, d = x.shape
    n_gate_chunks = d // 256
    q_scale = HEAD_DIM_A ** -0.5 * LOG2_E
    b_scale = HEAD_DIM_B ** -0.5 * LOG2_E

    def body(x_ref, c_ref, s1_ref, s2_ref, w_ref, gmix_ref, bg_ref, qg_ref, kg_ref, after_ref,
             h1_ref, qraw_ref, kraw_ref, qrot_ref, krot_ref, va_ref, *rest):
        qb_refs, kb_refs, vb_refs = rest[0:3], rest[3:6], rest[6:9]
        ga_ref, gb_ref, scr_ref = rest[9:]
        xv = x_ref[...]
        hb = (xv * _rstd(xv) * gmix_ref[...]).astype(BF16)
        h1_ref[...] = hb
        cos, s1, s2 = c_ref[...], s1_ref[...], s2_ref[...]

        def proj(lo, width):
            return _dot_nt(hb, w_ref[lo:lo + width, :])

        def norm_rope(z, g):
            n = z * _rstd(z) * g
            return n * cos + pltpu.roll(n, 32, 1) * s1 + pltpu.roll(n, 96, 1) * s2

        for j in range(QA_W // 256):
            z = proj(OFF_QA + 256 * j, 256)
            qraw_ref[:, 256 * j:256 * j + 256] = z
            for hh in range(2):
                lo = 256 * j + 128 * hh
                qrot_ref[:, lo:lo + 128] = (norm_rope(z[:, 128 * hh:128 * hh + 128], qg_ref[...]) * q_scale).astype(BF16)
        z = proj(OFF_KA, 256)
        kraw_ref[...] = z
        for hh in range(2):
            krot_ref[:, 128 * hh:128 * hh + 128] = norm_rope(z[:, 128 * hh:128 * hh + 128], kg_ref[...]).astype(BF16)
        va_ref[...] = proj(OFF_VA, 256).astype(BF16)
        for g, dil in enumerate(DILATIONS):
            _to_residues(proj(OFF_QB + GB_W * g, GB_W) * b_scale, qb_refs[g], scr_ref, dil, BF16)
            _to_residues(proj(OFF_KB + GB_W * g, GB_W), kb_refs[g], scr_ref, dil, BF16)
            _to_residues(proj(OFF_VB + GB_W * g, GB_W), vb_refs[g], scr_ref, dil, BF16)
        for j in range(n_gate_chunks):
            sl = slice(256 * j, 256 * j + 256)
            ga_ref[:, sl] = _sigmoid(proj(OFF_GA + 256 * j, 256) + bg_ref[:, sl]).astype(BF16)
            gb_ref[:, sl] = _sigmoid(
                proj(OFF_GA + d + 256 * j, 256) + bg_ref[:, d + 256 * j:d + 256 * j + 256]).astype(BF16)

    sd = jax.ShapeDtypeStruct
    outs = [sd((s, d), BF16), sd((s, QA_W), F32), sd((s, KA_W), F32), sd((s, QA_W), BF16), sd((s, KA_W), BF16),
            sd((s, KA_W), BF16)] + _dil_shapes(s, BF16) * 3 + [sd((s, d), BF16), sd((s, d), BF16)]
    out_specs = [_rows(tb, d), _rows(tb, QA_W), _rows(tb, KA_W), _rows(tb, QA_W), _rows(tb, KA_W), _rows(tb, KA_W)
                 ] + _dil_specs(tb) * 3 + [_rows(tb, d), _rows(tb, d)]
    in_specs = [_rows(tb, d), _rows(tb, LANES), _rows(tb, LANES), _rows(tb, LANES), _resident(w_in_t.shape),
                _resident(g_mix.shape), _resident(b_gate.shape), _resident(q_g.shape), _resident(k_g.shape), _ANY]
    res = list(pl.pallas_call(body, name="in_proj", grid=(s // tb,), in_specs=in_specs, out_specs=out_specs,
                              out_shape=outs, scratch_shapes=[pltpu.VMEM((2, tb, LANES), F32)],
                              compiler_params=_cparams(("arbitrary",)))(
        x, *tabs, w_in_t, g_mix, b_gate, q_g, k_g, after))
    return res[:6] + [res[6:9], res[9:12], res[12:15]] + res[15:]


def _attn_a_fwd(qrot, krot, va, tq, tk):
    s = qrot.shape[0]
    n_kv = s // tk
    gw = Q_PER_KV * HEAD_DIM_A

    def body(q_ref, k_ref, v_ref, o_ref, lse_ref):
        q4 = jnp.concatenate([q_ref[:, 128 * h:128 * h + 128] for h in range(Q_PER_KV)], axis=0)

        def step(j, carry):
            m, l, acc = carry
            sl = pl.ds(pl.multiple_of(j * tk, tk), tk)
            kj, vj = k_ref[sl, :], v_ref[sl, :]
            sc = _dot_nt(kj, q4)
            m_new = jnp.maximum(m, jnp.max(sc, axis=0, keepdims=True))
            p = jnp.exp2(sc - m_new)
            alpha = jnp.exp2(m - m_new)
            l = alpha * l + jnp.sum(p, axis=0, keepdims=True)
            acc = alpha * acc + _dot_tn(vj, p.astype(BF16))
            return m_new, l, acc

        rows = Q_PER_KV * tq
        m, l, acc = lax.fori_loop(0, n_kv, step, (jnp.full((1, rows), NEG_INF, F32), jnp.zeros((1, rows), F32),
                                                  jnp.zeros((HEAD_DIM_A, rows), F32)))
        o = (acc / l).T
        lse = m + jnp.log2(l)
        for h in range(Q_PER_KV):
            o_ref[:, 128 * h:128 * h + 128] = o[h * tq:(h + 1) * tq].astype(BF16)
            lse_ref[0, h:h + 1, :] = lse[:, h * tq:(h + 1) * tq]

    return pl.pallas_call(
        body, name="attn_a_fwd", grid=(N_KV_HEADS_A, s // tq),
        in_specs=[pl.BlockSpec((tq, gw), lambda g, i: (i, g)),
                  pl.BlockSpec((s, HEAD_DIM_A), lambda g, i: (0, g)),
                  pl.BlockSpec((s, HEAD_DIM_A), lambda g, i: (0, g))],
        out_specs=[pl.BlockSpec((tq, gw), lambda g, i: (i, g)),
                   pl.BlockSpec((1, Q_PER_KV, tq), lambda g, i: (g, 0, i))],
        out_shape=[jax.ShapeDtypeStruct((s, QA_W), BF16), jax.ShapeDtypeStruct((N_KV_HEADS_A, Q_PER_KV, s), F32)],
        compiler_params=_cparams(("arbitrary", "arbitrary")))(qrot, krot, va)


def _attn_a_bwd(qrot, krot, va, oa, doa, lse, tq, tk, after):
    s = qrot.shape[0]
    n_kv = s // tk
    gw = Q_PER_KV * HEAD_DIM_A

    def body(q_ref, do_ref, o_ref, lse_ref, k_ref, v_ref, after_ref, dq_ref, dk_ref, dv_ref):
        @pl.when(pl.program_id(1) == 0)
        def _():
            dk_ref[...] = jnp.zeros_like(dk_ref)
            dv_ref[...] = jnp.zeros_like(dv_ref)

        def stack(ref):
            return jnp.concatenate([ref[:, 128 * h:128 * h + 128] for h in range(Q_PER_KV)], axis=0)

        q4, do4, o4 = stack(q_ref), stack(do_ref), stack(o_ref)
        q4t, do4t = q4.T, do4.T
        delta = jnp.sum((do4.astype(F32) * o4.astype(F32)).T, axis=0, keepdims=True)
        lse4 = jnp.concatenate([lse_ref[0, h:h + 1, :] for h in range(Q_PER_KV)], axis=1)

        def step(j, dq):
            sl = pl.ds(pl.multiple_of(j * tk, tk), tk)
            kj, vj = k_ref[sl, :], v_ref[sl, :]
            p = jnp.exp2(_dot_nt(kj, q4) - lse4)
            ds = (p * (_dot_nt(vj, do4) - delta)).astype(BF16)
            dk_ref[:, sl] += _dot_nt(q4t, ds)
            dv_ref[:, sl] += _dot_nt(do4t, p.astype(BF16))
            return dq + _dot_tn(kj, ds)

        dq = lax.fori_loop(0, n_kv, step, jnp.zeros((HEAD_DIM_A, Q_PER_KV * tq), F32)).T
        for h in range(Q_PER_KV):
            dq_ref[:, 128 * h:128 * h + 128] = dq[h * tq:(h + 1) * tq]

    qspec = pl.BlockSpec((tq, gw), lambda g, i: (i, g))
    kspec = pl.BlockSpec((s, HEAD_DIM_A), lambda g, i: (0, g))
    ktspec = pl.BlockSpec((HEAD_DIM_A, s), lambda g, i: (g, 0))
    return pl.pallas_call(
        body, name="attn_a_bwd", grid=(N_KV_HEADS_A, s // tq),
        in_specs=[qspec, qspec, qspec, pl.BlockSpec((1, Q_PER_KV, tq), lambda g, i: (g, 0, i)), kspec, kspec, _ANY],
        out_specs=[qspec, ktspec, ktspec],
        out_shape=[jax.ShapeDtypeStruct((s, QA_W), F32), jax.ShapeDtypeStruct((KA_W, s), F32),
                   jax.ShapeDtypeStruct((KA_W, s), F32)],
        compiler_params=_cparams(("arbitrary", "arbitrary")))(qrot, doa, oa, lse, krot, va, after)


BAND_QB = 128
BAND_WIN = BAND_QB + 2 * BAND


def _band_specs(s, cb):
    per = cb // BAND
    last = s // BAND - 1
    cur = pl.BlockSpec((cb, GB_W), lambda i: (i, 0))
    prev = pl.BlockSpec((BAND, GB_W), lambda i: (jnp.maximum(i * per - 1, 0), 0))
    nxt = pl.BlockSpec((BAND, GB_W), lambda i: (jnp.minimum(i * per + per, last), 0))
    return cur, prev, nxt


def _window(prev_ref, cur_ref, next_ref):
    return jnp.concatenate([prev_ref[...], cur_ref[...], next_ref[...]], axis=0)


def _band_mask(base, seg_shift):
    rq = base + lax.broadcasted_iota(jnp.int32, (BAND_QB, BAND_WIN), 0)
    rk = base - BAND + lax.broadcasted_iota(jnp.int32, (BAND_QB, BAND_WIN), 1)
    same_segment = lax.shift_right_arithmetic(rq, jnp.int32(seg_shift)) == lax.shift_right_arithmetic(rk, jnp.int32(seg_shift))
    return (jnp.abs(rk - rq) <= BAND) & same_segment


def _build_bias(bmap_ref, tab_ref, bias_ref):
    bm = bmap_ref[...]
    acc = [jnp.full(bm.shape, NEG_INF, F32) for _ in range(N_HEADS_PER_DIL)]
    for b in range(N_REL_BUCKETS):
        hit = bm == b
        for h in range(N_HEADS_PER_DIL):
            acc[h] = jnp.where(hit, tab_ref[b, h] * LOG2_E, acc[h])
    rows = bm.shape[0]
    for h in range(N_HEADS_PER_DIL):
        bias_ref[h * rows:(h + 1) * rows, :] = acc[h]


def _segment_mask(base, seg_len, seg_shift):
    if seg_len % BAND_QB:
        return _band_mask(base, seg_shift)
    pos = lax.rem(base, seg_len)
    w = lax.broadcasted_iota(jnp.int32, (1, BAND_WIN), 1)
    return ((w >= BAND) | (pos != 0)) & ((w < BAND + BAND_QB) | (pos != seg_len - BAND_QB))


def _head_lane_masks():
    lane = lax.broadcasted_iota(jnp.int32, (1, LANES), 1)
    return [lane < HEAD_DIM_B, lane >= HEAD_DIM_B]


def _rows4(mask):
    return mask if mask.shape[0] == 1 else jnp.concatenate([mask] * N_HEADS_PER_DIL, axis=0)


def _head_scores(a, b):
    hm = _head_lane_masks()
    out = []
    for hp in range(2):
        ls = slice(LANES * hp, LANES * hp + LANES)
        ah = a[:, ls]
        both = jnp.concatenate([jnp.where(hm[0], ah, jnp.zeros_like(ah)), jnp.where(hm[1], ah, jnp.zeros_like(ah))],
                               axis=0)
        out.append(_dot_nt(both, b[:, ls]))
    return jnp.concatenate(out, axis=0)


def _head_combine(p, v, scale=None, transposed=False):
    hm = _head_lane_masks()
    rows = p.shape[0] // N_HEADS_PER_DIL
    halves = []
    for hp in range(2):
        vh = v[:, LANES * hp:LANES * hp + LANES]
        acc = None
        for hh in range(2):
            h = 2 * hp + hh
            ph = p[h * rows:(h + 1) * rows]
            vm = jnp.where(hm[hh], vh, jnp.zeros_like(vh))
            t = _dot_tn(ph, vm) if transposed else _dot_nn(ph, vm)
            if scale is not None:
                t = t * scale[h * rows:(h + 1) * rows]
            acc = t if acc is None else acc + t
        halves.append(acc)
    return jnp.concatenate(halves, axis=1)


def _head_spread(col):
    rows = col.shape[0] // N_HEADS_PER_DIL
    lane = lax.broadcasted_iota(jnp.int32, (1, GB_W), 1)
    out = jnp.zeros((rows, GB_W), F32)
    for h in range(N_HEADS_PER_DIL):
        out = jnp.where((lane >= HEAD_DIM_B * h) & (lane < HEAD_DIM_B * (h + 1)), col[h * rows:(h + 1) * rows], out)
    return out


def _head_cols(v):
    return jnp.concatenate([v[:, HEAD_DIM_B * h:HEAD_DIM_B * h + 1] for h in range(N_HEADS_PER_DIL)], axis=0)


def _seg_shift(s, dil):
    seg = s // dil
    assert seg & (seg - 1) == 0, "segment length must be a power of two"
    return seg.bit_length() - 1


def _band_fwd(dil, qb, kb, vb, bmap, tab, cb):
    s = qb.shape[0]
    shift = _seg_shift(s, dil)

    def body(q_ref, kp_ref, kc_ref, kn_ref, vp_ref, vc_ref, vn_ref, bmap_ref, tab_ref, o_ref, lse_ref, bias_ref):
        @pl.when(pl.program_id(0) == 0)
        def _():
            _build_bias(bmap_ref, tab_ref, bias_ref)

        kw, vw = _window(kp_ref, kc_ref, kn_ref), _window(vp_ref, vc_ref, vn_ref)
        for jj in range(cb // BAND_QB):
            r0 = BAND_QB * jj
            mask = _rows4(_segment_mask(pl.program_id(0) * cb + r0, s // dil, shift))
            sc = _head_scores(q_ref[r0:r0 + BAND_QB, :], kw[r0:r0 + BAND_WIN, :]) + bias_ref[...]
            sc = jnp.where(mask, sc, NEG_INF)
            m = jnp.max(sc, axis=-1, keepdims=True)
            e = jnp.exp2(sc - m)
            l = jnp.sum(e, axis=-1, keepdims=True)
            o = _head_combine(e.astype(BF16), vw[r0:r0 + BAND_WIN, :], 1.0 / l)
            o_ref[r0:r0 + BAND_QB, :] = o
            lse_ref[r0:r0 + BAND_QB, :] = _head_spread(m + jnp.log2(l))

    cur, prev, nxt = _band_specs(s, cb)
    return pl.pallas_call(
        body, name=f"band_fwd_d{dil}", grid=(s // cb,),
        in_specs=[cur, prev, cur, nxt, prev, cur, nxt, _resident(bmap.shape), pl.BlockSpec(memory_space=pltpu.SMEM)],
        out_specs=[cur, cur],
        out_shape=[jax.ShapeDtypeStruct(qb.shape, F32), jax.ShapeDtypeStruct(qb.shape, F32)],
        scratch_shapes=[pltpu.VMEM((N_HEADS_PER_DIL * BAND_QB, BAND_WIN), F32)],
        compiler_params=_cparams(("arbitrary",)))(qb, kb, kb, kb, vb, vb, vb, bmap, tab)


def _band_bwd(dil, qb, kb, vb, dob, lse, dd, bmap, tab, cb):
    s = qb.shape[0]
    shift = _seg_shift(s, dil)
    n_steps = s // cb

    def body(q_ref, do_ref, lse_ref, dd_ref, kp_ref, kc_ref, kn_ref, vp_ref, vc_ref, vn_ref, bmap_ref, tab_ref,
             dq_ref, dk_ref, dv_ref, dtab_ref, bias_ref, dsum_ref):
        @pl.when(pl.program_id(0) == 0)
        def _():
            _build_bias(bmap_ref, tab_ref, bias_ref)
            dsum_ref[...] = jnp.zeros_like(dsum_ref)
            dk_ref[...] = jnp.zeros_like(dk_ref)
            dv_ref[...] = jnp.zeros_like(dv_ref)

        kw, vw = _window(kp_ref, kc_ref, kn_ref), _window(vp_ref, vc_ref, vn_ref)
        for jj in range(cb // BAND_QB):
            r0 = BAND_QB * jj
            base = pl.program_id(0) * cb + r0
            mask = _rows4(_segment_mask(base, s // dil, shift))
            qh, doh = q_ref[r0:r0 + BAND_QB, :], do_ref[r0:r0 + BAND_QB, :]
            k3, v3 = kw[r0:r0 + BAND_WIN, :], vw[r0:r0 + BAND_WIN, :]
            sc = _head_scores(qh, k3) + bias_ref[...]
            sc = jnp.where(mask, sc, NEG_INF)
            p = jnp.exp2(sc - _head_cols(lse_ref[r0:r0 + BAND_QB, :]))
            dp = _head_scores(doh, v3)
            ds = p * (dp - _head_cols(dd_ref[r0:r0 + BAND_QB, :]))
            dsum_ref[...] += ds
            dsb = ds.astype(BF16)
            dq_ref[r0:r0 + BAND_QB, :] = _head_combine(dsb, k3)
            dk_win = _head_combine(dsb, qh, transposed=True)
            dv_win = _head_combine(p.astype(BF16), doh, transposed=True)
            own = pl.ds(pl.multiple_of(base, BAND), BAND_QB)
            dk_ref[own, :] += dk_win[BAND:BAND + BAND_QB]
            dv_ref[own, :] += dv_win[BAND:BAND + BAND_QB]

            @pl.when(base > 0)
            def _():
                before = pl.ds(pl.multiple_of(base - BAND, BAND), BAND)
                dk_ref[before, :] += dk_win[:BAND]
                dv_ref[before, :] += dv_win[:BAND]

            @pl.when(base + BAND_QB < s)
            def _():
                after = pl.ds(pl.multiple_of(base + BAND_QB, BAND), BAND)
                dk_ref[after, :] += dk_win[BAND + BAND_QB:]
                dv_ref[after, :] += dv_win[BAND + BAND_QB:]

        @pl.when(pl.program_id(0) == n_steps - 1)
        def _():
            bm = bmap_ref[...]
            lane = lax.broadcasted_iota(jnp.int32, (1, LANES), 1)
            for b in range(N_REL_BUCKETS):
                hit = bm == b
                row = jnp.zeros((1, LANES), F32)
                for h in range(N_HEADS_PER_DIL):
                    part = dsum_ref[h * BAND_QB:(h + 1) * BAND_QB, :]
                    row = jnp.where(lane == h, jnp.sum(jnp.where(hit, part, 0.0)), row)
                dtab_ref[b:b + 1, :] = row

    cur, prev, nxt = _band_specs(s, cb)
    whole = _acc_spec(qb.shape)
    return pl.pallas_call(
        body, name=f"band_bwd_d{dil}", grid=(n_steps,),
        in_specs=[cur, cur, cur, cur, prev, cur, nxt, prev, cur, nxt, _resident(bmap.shape),
                  pl.BlockSpec(memory_space=pltpu.SMEM)],
        out_specs=[cur, whole, whole, _acc_spec((N_REL_BUCKETS, LANES))],
        out_shape=[jax.ShapeDtypeStruct(qb.shape, F32)] * 3 + [jax.ShapeDtypeStruct((N_REL_BUCKETS, LANES), F32)],
        scratch_shapes=[pltpu.VMEM((N_HEADS_PER_DIL * BAND_QB, BAND_WIN), F32),
                        pltpu.VMEM((N_HEADS_PER_DIL * BAND_QB, BAND_WIN), F32)],
        compiler_params=_cparams(("arbitrary",)))(qb, dob, lse, dd, kb, kb, kb, vb, vb, vb, bmap, tab)


def _t5_bucket(rel):
    nb = N_REL_BUCKETS // 2
    ret = (rel > 0).astype(np.int32) * nb
    n = np.abs(rel)
    max_exact = nb // 2
    large = max_exact + (np.log(np.maximum(n, 1) / max_exact) / math.log(REL_MAX_DIST / max_exact)
                         * (nb - max_exact)).astype(np.int32)
    large = np.minimum(large, nb - 1)
    return ret + np.where(n < max_exact, n, large).astype(np.int32)


def _bucket_map(dil):
    off = np.arange(BAND_WIN)[None, :] - BAND - np.arange(BAND_QB)[:, None]
    return np.where(np.abs(off) <= BAND, _t5_bucket(off * dil), -1).astype(np.int32)


def _seg_sum(v):
    lane = lax.broadcasted_iota(jnp.int32, (1, v.shape[1]), 1)
    out = jnp.zeros_like(v)
    for h in range(v.shape[1] // HEAD_DIM_B):
        m = (lane >= HEAD_DIM_B * h) & (lane < HEAD_DIM_B * (h + 1))
        out = jnp.where(m, jnp.sum(jnp.where(m, v, 0.0), axis=-1, keepdims=True), out)
    return out


def _mix_out(x, oa, og, lg, ga, gb, w_oa, w_ob_t, w_o, tb):
    s, d = x.shape

    def body(x_ref, oa_ref, og0_ref, og1_ref, og2_ref, lg0_ref, lg1_ref, lg2_ref, ga_ref, gb_ref,
             woa_ref, wob_ref, wo_ref, x2_ref, ob_ref, lse0_ref, lse1_ref, lse2_ref, ya_ref, yb_ref, u_ref, scr_ref):
        og_refs, lg_refs = (og0_ref, og1_ref, og2_ref), (lg0_ref, lg1_ref, lg2_ref)
        l0, l1, l2 = [_from_residues(lg_refs[g], scr_ref, dil) for g, dil in enumerate(DILATIONS)]
        lmax = jnp.maximum(jnp.maximum(l0, l1), l2)
        w0, w1, w2 = jnp.exp2(l0 - lmax), jnp.exp2(l1 - lmax), jnp.exp2(l2 - lmax)
        den = w0 + w1 + w2
        o0, o1, o2 = [_from_residues(og_refs[g], scr_ref, dil) for g, dil in enumerate(DILATIONS)]
        ob = ((w0 * o0 + w1 * o1 + w2 * o2) / den).astype(BF16)
        ob_ref[...] = ob
        lse = lmax + jnp.log2(den)
        for g, (dil, ref) in enumerate(zip(DILATIONS, (lse0_ref, lse1_ref, lse2_ref))):
            _to_residues(lse, ref, scr_ref, dil, F32)
        ya = _dot_nn(oa_ref[...], woa_ref[...])
        yb = _dot_nt(ob, wob_ref[...])
        ya_ref[...] = ya.astype(BF16)
        yb_ref[...] = yb.astype(BF16)
        u = (ga_ref[...].astype(F32) * ya + gb_ref[...].astype(F32) * yb).astype(BF16)
        u_ref[...] = u
        x2_ref[...] = x_ref[...] + _dot_nn(u, wo_ref[...])

    sd = jax.ShapeDtypeStruct
    res = list(pl.pallas_call(
        body, name="mix_out", grid=(s // tb,),
        in_specs=[_rows(tb, d), _rows(tb, QA_W)] + _dil_specs(tb) * 2 + [
            _rows(tb, d), _rows(tb, d), _resident(w_oa.shape), _resident(w_ob_t.shape), _resident(w_o.shape)],
        out_specs=[_rows(tb, d), _rows(tb, GB_W)] + _dil_specs(tb) + [_rows(tb, d), _rows(tb, d), _rows(tb, d)],
        out_shape=[sd((s, d), F32), sd((s, GB_W), BF16)] + _dil_shapes(s, F32) + [
            sd((s, d), BF16), sd((s, d), BF16), sd((s, d), BF16)],
        scratch_shapes=[pltpu.VMEM((2, tb, LANES), F32)],
        compiler_params=_cparams(("arbitrary",)))(x, oa, *og, *lg, ga, gb, w_oa, w_ob_t, w_o))
    return res[:2] + [res[2:5]] + res[5:]


def _mlp_fwd(x2, w1_t, w2, g_mlp, tb, tc):
    s, d = x2.shape
    dff = w1_t.shape[0]

    def body(x_ref, w1_ref, w2_ref, g_ref, x3_ref, r_ref, h_ref):
        xv = x_ref[...]
        hb = (xv * _rstd(xv) * g_ref[...]).astype(BF16)
        h_ref[...] = hb
        x3_ref[...] = xv
        for c in range(dff // tc):
            sl = slice(tc * c, tc * c + tc)
            r = jnp.maximum(_dot_nt(hb, w1_ref[sl, :]), 0.0)
            r_ref[:, sl] = r.astype(BF16)
            x3_ref[...] += _dot_nn((r * r).astype(BF16), w2_ref[sl, :])

    sd = jax.ShapeDtypeStruct
    return pl.pallas_call(
        body, name="mlp_fwd", grid=(s // tb,),
        in_specs=[_rows(tb, d), _resident(w1_t.shape), _resident(w2.shape), _resident(g_mlp.shape)],
        out_specs=[_rows(tb, d), _rows(tb, dff), _rows(tb, d)],
        out_shape=[sd((s, d), F32), sd((s, dff), BF16), sd((s, d), BF16)],
        compiler_params=_cparams(("arbitrary",)))(x2, w1_t, w2, g_mlp)


def _ple_loss(x3, p, target, w_pg, w_p_t, g_ple, g_fin, tb):
    s, d = x3.shape
    dp = p.shape[1]

    def body(x_ref, p_ref, t_ref, wpg_ref, wp_ref, gple_ref, gfin_ref,
             dx3_ref, h3_ref, dpre_ref, dpe_ref, pb_ref, loss_ref, dgfin_ref, dgple_ref):
        @pl.when(pl.program_id(0) == 0)
        def _():
            loss_ref[...] = jnp.zeros_like(loss_ref)
            dgfin_ref[...] = jnp.zeros_like(dgfin_ref)
            dgple_ref[...] = jnp.zeros_like(dgple_ref)

        x3v = x_ref[...]
        r3 = _rstd(x3v)
        n3 = x3v * r3
        h3 = (n3 * gple_ref[...]).astype(BF16)
        h3_ref[...] = h3
        gp = _sigmoid(_dot_nn(h3, wpg_ref[...]))
        pb = p_ref[...].astype(BF16)
        pb_ref[...] = pb
        pe = _dot_nt(pb, wp_ref[...])
        x4 = x3v + gp * pe
        r4 = _rstd(x4)
        n4 = x4 * r4
        err = n4 * gfin_ref[...] - t_ref[...]
        loss_ref[...] += jnp.sum(0.5 * jnp.mean(err * err, axis=-1, keepdims=True), axis=0, keepdims=True)
        dy = err / d
        dgfin_ref[...] += _colsum(dy * n4)
        dx4 = _rms_bwd(dy, n4, r4, gfin_ref[...])
        dpe_ref[...] = (dx4 * gp).astype(BF16)
        dpre = (dx4 * pe * gp * (1.0 - gp)).astype(BF16)
        dpre_ref[...] = dpre
        dh3 = _dot_nt(dpre, wpg_ref[...])
        dgple_ref[...] += _colsum(dh3 * n3)
        dx3_ref[...] = dx4 + _rms_bwd(dh3, n3, r3, gple_ref[...])

    sd = jax.ShapeDtypeStruct
    return pl.pallas_call(
        body, name="ple_loss", grid=(s // tb,),
        in_specs=[_rows(tb, d), _rows(tb, dp), _rows(tb, d), _resident(w_pg.shape), _resident(w_p_t.shape),
                  _resident(g_ple.shape), _resident(g_fin.shape)],
        out_specs=[_rows(tb, d), _rows(tb, d), _rows(tb, d), _rows(tb, d), _rows(tb, dp),
                   _acc_spec((1, LANES)), _acc_spec((1, d)), _acc_spec((1, d))],
        out_shape=[sd((s, d), F32), sd((s, d), BF16), sd((s, d), BF16), sd((s, d), BF16), sd((s, dp), BF16),
                   sd((1, LANES), F32), sd((1, d), F32), sd((1, d), F32)],
        compiler_params=_cparams(("arbitrary",)))(x3, p, target, w_pg, w_p_t, g_ple, g_fin)


def _mlp_bwd(dx3, x2, r, w1_t, w2, g_mlp, tb, tc):
    s, d = x2.shape
    dff = w1_t.shape[0]

    def body(dx3_ref, x_ref, r_ref, w1_ref, w2_ref, g_ref, dx2_ref, df_ref, dg_ref, dh_ref):
        @pl.when(pl.program_id(0) == 0)
        def _():
            dg_ref[...] = jnp.zeros_like(dg_ref)

        dx3v = dx3_ref[...]
        dx3b = dx3v.astype(BF16)
        dh_ref[...] = jnp.zeros_like(dh_ref)
        for c in range(dff // tc):
            sl = slice(tc * c, tc * c + tc)
            df = (_dot_nt(dx3b, w2_ref[sl, :]) * (2.0 * r_ref[:, sl].astype(F32))).astype(BF16)
            df_ref[:, sl] = df
            dh_ref[...] += _dot_nn(df, w1_ref[sl, :])
        xv = x_ref[...]
        r2 = _rstd(xv)
        n2 = xv * r2
        dh = dh_ref[...]
        dg_ref[...] += _colsum(dh * n2)
        dx2_ref[...] = dx3v + _rms_bwd(dh, n2, r2, g_ref[...])

    sd = jax.ShapeDtypeStruct
    return pl.pallas_call(
        body, name="mlp_bwd", grid=(s // tb,),
        in_specs=[_rows(tb, d), _rows(tb, d), _rows(tb, dff), _resident(w1_t.shape), _resident(w2.shape),
                  _resident(g_mlp.shape)],
        out_specs=[_rows(tb, d), _rows(tb, dff), _acc_spec((1, d))],
        out_shape=[sd((s, d), F32), sd((s, dff), BF16), sd((1, d), F32)],
        scratch_shapes=[pltpu.VMEM((tb, d), F32)],
        compiler_params=_cparams(("arbitrary",)))(dx3, x2, r, w1_t, w2, g_mlp)


def _mix_out_bwd(dx2, ya, yb, ga, gb, ob, w_oa, w_ob_t, w_o, tb, after):
    s, d = dx2.shape

    def body(dx_ref, ya_ref, yb_ref, ga_ref, gb_ref, ob_ref, woa_ref, wob_ref, wo_ref, after_ref,
             doa_ref, dob0_ref, dob1_ref, dob2_ref, dd0_ref, dd1_ref, dd2_ref, dga_ref, dgb_ref, dya_ref, dyb_ref,
             dbg_ref, scr_ref):
        @pl.when(pl.program_id(0) == 0)
        def _():
            dbg_ref[...] = jnp.zeros_like(dbg_ref)

        du = _dot_nt(dx_ref[...].astype(BF16), wo_ref[...])
        gav, gbv = ga_ref[...].astype(F32), gb_ref[...].astype(F32)
        dya = (du * gav).astype(BF16)
        dyb = (du * gbv).astype(BF16)
        dya_ref[...] = dya
        dyb_ref[...] = dyb
        dga = du * ya_ref[...].astype(F32) * gav * (1.0 - gav)
        dgb = du * yb_ref[...].astype(F32) * gbv * (1.0 - gbv)
        dga_ref[...] = dga.astype(BF16)
        dgb_ref[...] = dgb.astype(BF16)
        dbg_ref[:, 0:d] += _colsum(dga)
        dbg_ref[:, d:2 * d] += _colsum(dgb)
        doa_ref[...] = _dot_nt(dya, woa_ref[...]).astype(BF16)
        dob = _dot_nn(dyb, wob_ref[...])
        dd = _seg_sum(dob * ob_ref[...].astype(F32))
        for dil, dob_ref, dd_ref in zip(DILATIONS, (dob0_ref, dob1_ref, dob2_ref), (dd0_ref, dd1_ref, dd2_ref)):
            _to_residues(dob, dob_ref, scr_ref, dil, BF16)
            _to_residues(dd, dd_ref, scr_ref, dil, F32)

    sd = jax.ShapeDtypeStruct
    res = list(pl.pallas_call(
        body, name="mix_out_bwd", grid=(s // tb,),
        in_specs=[_rows(tb, d)] * 5 + [_rows(tb, GB_W), _resident(w_oa.shape), _resident(w_ob_t.shape),
                                       _resident(w_o.shape), _ANY],
        out_specs=[_rows(tb, QA_W)] + _dil_specs(tb) * 2 + [_rows(tb, d), _rows(tb, d), _rows(tb, d),
                                                           _rows(tb, d), _acc_spec((1, 2 * d))],
        out_shape=[sd((s, QA_W), BF16)] + _dil_shapes(s, BF16) + _dil_shapes(s, F32) + [
            sd((s, d), BF16), sd((s, d), BF16), sd((s, d), BF16), sd((s, d), BF16), sd((1, 2 * d), F32)],
        scratch_shapes=[pltpu.VMEM((2, tb, LANES), F32)],
        compiler_params=_cparams(("arbitrary",)))(dx2, ya, yb, ga, gb, ob, w_oa, w_ob_t, w_o, after))
    return res[:1] + [res[1:4], res[4:7]] + res[7:]


def _in_proj_bwd(dx2, x, dqrot, dkrot, dva, qraw, kraw, tabs, dqb, dkb, dvb, dga, dgb, w_in_t, g_mix, q_g, k_g, tb):
    s, d = x.shape
    din = w_in_t.shape[0]
    q_scale = HEAD_DIM_A ** -0.5
    b_scale = HEAD_DIM_B ** -0.5
    tc = 256

    def body(dx2_ref, x_ref, dq_ref, dk_ref, dv_ref, qraw_ref, kraw_ref, c_ref, s1_ref, s2_ref, *rest):
        dqb_refs, dkb_refs, dvb_refs = rest[0:3], rest[3:6], rest[6:9]
        (dga_ref, dgb_ref, w_ref, gmix_ref, qg_ref, kg_ref,
         dx_ref, dz_ref, dgmix_ref, dqg_ref, dkg_ref, dh_ref, scr_ref) = rest[9:]

        @pl.when(pl.program_id(0) == 0)
        def _():
            dgmix_ref[...] = jnp.zeros_like(dgmix_ref)
            dqg_ref[...] = jnp.zeros_like(dqg_ref)
            dkg_ref[...] = jnp.zeros_like(dkg_ref)

        cos, s1, s2 = c_ref[...][None], s1_ref[...][None], s2_ref[...][None]

        def heads_bwd(drot, z, g_ref, acc_ref):
            dn = drot * cos + pltpu.roll(drot * s1, 96, 2) + pltpu.roll(drot * s2, 32, 2)
            rr = _rstd(z)
            nn = z * rr
            acc_ref[...] += jnp.sum(jnp.sum(dn * nn, axis=0), axis=0, keepdims=True)
            return _rms_bwd(dn, nn, rr, g_ref[...][None]).astype(BF16)

        dh_ref[...] = jnp.zeros_like(dh_ref)

        def emit(off, piece):
            dz_ref[:, off:off + tc] = piece
            dh_ref[...] += _dot_nn(piece, w_ref[off:off + tc, :])

        for j in range(d // tc):
            emit(OFF_GA + tc * j, dga_ref[:, tc * j:tc * j + tc])
            emit(OFF_GA + d + tc * j, dgb_ref[:, tc * j:tc * j + tc])
        emit(OFF_VA, dv_ref[...].T.astype(BF16))
        for g, dil in enumerate(DILATIONS):
            emit(OFF_QB + GB_W * g, (_from_residues(dqb_refs[g], scr_ref, dil) * b_scale).astype(BF16))
            emit(OFF_KB + GB_W * g, (_from_residues(dkb_refs[g], scr_ref, dil) * LN_2).astype(BF16))
            emit(OFF_VB + GB_W * g, _from_residues(dvb_refs[g], scr_ref, dil).astype(BF16))
        stack = lambda ref, n: jnp.stack([ref[:, 128 * h:128 * h + 128] for h in range(n)], axis=0)
        dzq = heads_bwd(stack(dq_ref, N_Q_HEADS_A) * q_scale, stack(qraw_ref, N_Q_HEADS_A), qg_ref, dqg_ref)
        dkt = jnp.stack([dk_ref[128 * h:128 * h + 128, :].T for h in range(N_KV_HEADS_A)], axis=0) * LN_2
        dzk = heads_bwd(dkt, stack(kraw_ref, N_KV_HEADS_A), kg_ref, dkg_ref)
        for j in range(N_Q_HEADS_A // 2):
            emit(OFF_QA + tc * j, jnp.concatenate([dzq[2 * j], dzq[2 * j + 1]], axis=1))
        emit(OFF_KA, jnp.concatenate([dzk[0], dzk[1]], axis=1))
        xv = x_ref[...]
        r1 = _rstd(xv)
        n1 = xv * r1
        dh = dh_ref[...]
        dgmix_ref[...] += _colsum(dh * n1)
        dx_ref[...] = dx2_ref[...] + _rms_bwd(dh, n1, r1, gmix_ref[...])

    sd = jax.ShapeDtypeStruct
    return pl.pallas_call(
        body, name="in_proj_bwd", grid=(s // tb,),
        in_specs=[_rows(tb, d), _rows(tb, d), _rows(tb, QA_W), pl.BlockSpec((KA_W, tb), lambda i: (0, i)),
                  pl.BlockSpec((KA_W, tb), lambda i: (0, i)), _rows(tb, QA_W),
                  _rows(tb, KA_W), _rows(tb, LANES), _rows(tb, LANES), _rows(tb, LANES),
                  ] + _dil_specs(tb) * 3 + [_rows(tb, d), _rows(tb, d),
                  _resident(w_in_t.shape), _resident(g_mix.shape), _resident(q_g.shape), _resident(k_g.shape)],
        out_specs=[_rows(tb, d), _rows(tb, din), _acc_spec((1, d)), _acc_spec((1, HEAD_DIM_A)),
                   _acc_spec((1, HEAD_DIM_A))],
        out_shape=[sd((s, d), F32), sd((s, din), BF16), sd((1, d), F32), sd((1, HEAD_DIM_A), F32),
                   sd((1, HEAD_DIM_A), F32)],
        scratch_shapes=[pltpu.VMEM((tb, d), F32), pltpu.VMEM((2, tb, LANES), F32)],
        compiler_params=_cparams(("arbitrary",)))(
        dx2, x, dqrot, dkrot, dva, qraw, kraw, *tabs, *dqb, *dkb, *dvb, dga, dgb, w_in_t, g_mix, q_g, k_g)


def _identity(v):
    return v


def _to_bf16(v):
    return v.astype(BF16)


def _square_bf16(v):
    vf = v.astype(F32)
    return (vf * vf).astype(BF16)


def _weight_grad(name, a, b, ti, tj, tk, a_fn=_identity, b_fn=_identity, col0=0, n=None, after=None):
    t, m = a.shape
    n = b.shape[1] if n is None else n
    n_k = t // tk
    after = a if after is None else after

    def body(a_ref, b_ref, after_ref, o_ref, acc_ref):
        k = pl.program_id(2)

        @pl.when(k == 0)
        def _():
            acc_ref[...] = jnp.zeros_like(acc_ref)

        acc_ref[...] += _dot_tn(a_fn(a_ref[...]), b_fn(b_ref[...]))

        @pl.when(k == n_k - 1)
        def _():
            o_ref[...] = acc_ref[...].astype(BF16)

    return pl.pallas_call(
        body, name=name, grid=(m // ti, n // tj, n_k),
        in_specs=[pl.BlockSpec((tk, ti), lambda i, j, k: (k, i)),
                  pl.BlockSpec((tk, tj), lambda i, j, k: (k, j + col0 // tj)), _ANY],
        out_specs=pl.BlockSpec((ti, tj), lambda i, j, k: (i, j)),
        out_shape=jax.ShapeDtypeStruct((m, n), BF16),
        scratch_shapes=[pltpu.VMEM((ti, tj), F32)],
        compiler_params=_cparams(("arbitrary", "arbitrary", "arbitrary")))(a, b, after)


def _sum_slots(name, recv, own):
    m, n, k = recv.shape
    tc = min(k, 256)

    def body(own_ref, r_ref, o_ref):
        acc = own_ref[...].astype(F32)
        for i in range(m):
            acc = acc + r_ref[i].astype(F32)
        o_ref[...] = acc

    return pl.pallas_call(
        body, name=name, grid=(k // tc,),
        in_specs=[pl.BlockSpec((n, tc), lambda j: (0, j)), pl.BlockSpec((m, n, tc), lambda j: (0, 0, j))],
        out_specs=pl.BlockSpec((n, tc), lambda j: (0, j)),
        out_shape=jax.ShapeDtypeStruct((n, k), F32),
        compiler_params=_cparams(("arbitrary",)))(own, recv)


def _adamw_math(w, g, m, v):
    m = ADAM_B1 * m + (1.0 - ADAM_B1) * g
    v = ADAM_B2 * v + (1.0 - ADAM_B2) * (g * g)
    m_hat = m / (1.0 - ADAM_B1 ** ADAM_STEP)
    v_hat = v / (1.0 - ADAM_B2 ** ADAM_STEP)
    delta = -ADAM_LR * (m_hat / (jnp.sqrt(v_hat) + ADAM_EPS) + ADAM_WD * w)
    return delta, m, v


def _adamw(name, w, g, m, v):
    r, c = w.shape
    tr = min(r, 256)

    def body(w_ref, g_ref, m_ref, v_ref, d_ref, mo_ref, vo_ref):
        d_ref[...], mo_ref[...], vo_ref[...] = _adamw_math(w_ref[...], g_ref[...], m_ref[...], v_ref[...])

    spec = pl.BlockSpec((tr, c), lambda i: (i, 0))
    return pl.pallas_call(
        body, name=name, grid=(r // tr,), in_specs=[spec] * 4, out_specs=[spec] * 3,
        out_shape=[jax.ShapeDtypeStruct((r, c), F32)] * 3,
        compiler_params=_cparams(("arbitrary",)))(w, g, m, v)


def _small_update(parts, w, m, v):
    def body(p_ref, w_ref, m_ref, v_ref, g_ref, d_ref, mo_ref, vo_ref):
        g = p_ref[0]
        for i in range(1, N_DEV):
            g = g + p_ref[i]
        g_ref[...] = g
        d_ref[...], mo_ref[...], vo_ref[...] = _adamw_math(w_ref[...], g, m_ref[...], v_ref[...])

    return pl.pallas_call(body, name="small_update", out_shape=[jax.ShapeDtypeStruct(w.shape, F32)] * 4)(
        parts, w, m, v)


def _pack_rows(vectors, n_rows):
    flat = jnp.concatenate([v.reshape(-1).astype(F32) for v in vectors])
    flat = jnp.pad(flat, (0, n_rows * LANES - flat.shape[0]))
    return flat.reshape(n_rows, LANES)


def _pick_tile(n, prefs):
    for t in prefs:
        if n % t == 0:
            return t
    return n


def kernel(x, p, norm_mix_g, w_in, b_gate, q_norm_g, k_norm_g, rel_bias, w_out_a, w_out_b, w_out, norm_mlp_g, w_ff1, w_ff2, norm_ple_g, w_ple_gate, w_ple, final_norm_g, loss_target, m_norm_mix_g, m_w_in, m_b_gate, m_q_norm_g, m_k_norm_g, m_rel_bias, m_w_out_a, m_w_out_b, m_w_out, m_norm_mlp_g, m_w_ff1, m_w_ff2, m_norm_ple_g, m_w_ple_gate, m_w_ple, m_final_norm_g, v_norm_mix_g, v_w_in, v_b_gate, v_q_norm_g, v_k_norm_g, v_rel_bias, v_w_out_a, v_w_out_b, v_w_out, v_norm_mlp_g, v_w_ff1, v_w_ff2, v_norm_ple_g, v_w_ple_gate, v_w_ple, v_final_norm_g):
    s, d = x.shape[1], x.shape[2]
    xs, ps, ts = x[0], p[0, 0], loss_target[0]
    tb = _pick_tile(s, (512, 256))
    tq = _pick_tile(s, (256,))
    tk = _pick_tile(s, (1024, 512))
    cb = _pick_tile(s, (1024, 512))
    fin_g = final_norm_g.reshape(1, d)

    col_sharded = {"w_in": w_in[0], "w_out_b": w_out_b[0], "w_ff1": w_ff1[0], "w_ple": w_ple[0]}
    row_sharded = {"w_out_a": w_out_a[0], "w_out": w_out[0], "w_ff2": w_ff2[0], "w_ple_gate": w_ple_gate[0]}
    order = ["w_in", "w_out_a", "w_out_b", "w_out", "w_ff1", "w_ff2", "w_ple_gate", "w_ple"]
    shards = [(col_sharded[n].T if n in col_sharded else row_sharded[n]).astype(BF16) for n in order]
    my_idx = 4 * lax.axis_index("x") + 2 * lax.axis_index("y") + lax.axis_index("c")
    (w_in_t,) = _all_gather(shards[:1], 1)
    zones = _place_own_rows(shards[1:], my_idx)
    ag = _copies_start("weights_gather_start", shards[1:], zones, w_in_t, True)

    tabs = _rope_tables(s)
    (h1, qraw, kraw, qrot, krot, va, qb, kb, vb, ga, gb) = _in_proj(
        xs, tabs, w_in_t, norm_mix_g, b_gate, q_norm_g, k_norm_g, tb, ag[4])
    oa, lse_a = _attn_a_fwd(qrot, krot, va, tq, tk)
    _, (w_oa, w_ob_t, w_o, w_ff1_t, w_ff2_f, w_pg, w_p_t) = _copies_wait(
        "weights_gather_wait", ag[0], ag[1], ag[2], ag[3], lse_a, True)
    flat = lambda arrs: [a.reshape(s, GB_W) for a in arrs]
    split = lambda arrs: [a.reshape(dil, s // dil, GB_W) for a, dil in zip(arrs, DILATIONS)]
    qb_r, kb_r, vb_r = flat(qb), flat(kb), flat(vb)
    bmaps = [jnp.asarray(_bucket_map(dil)) for dil in DILATIONS]
    bias_tabs = [rel_bias[:, N_HEADS_PER_DIL * g:N_HEADS_PER_DIL * (g + 1)] for g in range(3)]
    band_out = [_band_fwd(dil, qb_r[g], kb_r[g], vb_r[g], bmaps[g], bias_tabs[g], cb)
                for g, dil in enumerate(DILATIONS)]
    og, lg = split([o for o, _ in band_out]), split([l for _, l in band_out])
    x2, ob, lse_b, ya, yb, u = _mix_out(xs, oa, og, lg, ga, gb, w_oa, w_ob_t, w_o, tb)
    tc = _pick_tile(w_ff1_t.shape[0], (512,))
    x3, r_act, h2 = _mlp_fwd(x2, w_ff1_t, w_ff2_f, norm_mlp_g, tb, tc)

    dx3, h3, dpre, dpe, pb, loss_part, dg_fin, dg_ple = _ple_loss(
        x3, ps, ts, w_pg, w_p_t, norm_ple_g, fin_g, tb)
    dx2, df, dg_mlp = _mlp_bwd(dx3, x2, r_act, w_ff1_t, w_ff2_f, norm_mlp_g, tb, tc)

    tkk = _pick_tile(s, (1024, 512))
    dff = w_ff1_t.shape[0]
    t1k = lambda n: _pick_tile(n, (1024, 512, 256))
    slots = lambda parts: [lax.empty((7, a.shape[0] // N_DEV, a.shape[1]), BF16) for a in parts]
    part1 = [_weight_grad("grad_w_ff1", df, h2, t1k(dff), t1k(d), tkk),
             _weight_grad("grad_w_ff2", r_act, dx3, t1k(dff), t1k(d), tkk, a_fn=_square_bf16, b_fn=_to_bf16),
             _weight_grad("grad_w_ple_gate", h3, dpre, t1k(d), t1k(d), tkk),
             _weight_grad("grad_w_ple", dpe, pb, t1k(d), ps.shape[1], tkk)]
    doa, dob, dd, dga, dgb, dya, dyb, dbg = _mix_out_bwd(dx2, ya, yb, ga, gb, ob, w_oa, w_ob_t, w_o, tb, dx2)
    part1 += [_weight_grad("grad_w_out_a", oa, dya, t1k(QA_W), t1k(d), tkk),
              _weight_grad("grad_w_out_b", dyb, ob, t1k(d), GB_W, tkk),
              _weight_grad("grad_w_out", u, dx2, t1k(d), t1k(d), tkk, b_fn=_to_bf16)]
    rs1 = _copies_start("grads1_start", part1, slots(part1), doa, False)
    dqrot, dkrot, dva = _attn_a_bwd(qrot, krot, va, oa, doa, lse_a, tq, tk, rs1[4])
    dob_r, lse_r, dd_r = flat(dob), flat(lse_b), flat(dd)
    band_bwd = [_band_bwd(dil, qb_r[g], kb_r[g], vb_r[g], dob_r[g], lse_r[g], dd_r[g], bmaps[g], bias_tabs[g], cb)
                for g, dil in enumerate(DILATIONS)]
    dqb, dkb, dvb = [split([r[j] for r in band_bwd]) for j in range(3)]
    grad_x, dz, dg_mix, dg_q, dg_k = _in_proj_bwd(
        dx2, xs, dqrot, dkrot, dva, qraw, kraw, tabs, dqb, dkb, dvb, dga, dgb, w_in_t, norm_mix_g,
        q_norm_g, k_norm_g, _pick_tile(s, (256,)))
    d_rel = jnp.concatenate([r[3][:, :N_HEADS_PER_DIL] for r in band_bwd], axis=1)

    din = w_in_t.shape[0]
    ti_in = _pick_tile(din, (din // 2,)) if (din // 2) % LANES == 0 else din
    hd_ = d // 2
    part3 = [_weight_grad("grad_w_in_lo", dz, h1, ti_in, t1k(hd_), tkk, n=hd_)]
    rs3 = _copies_start("grads3_start", part3, slots(part3), grad_x, False)
    part4 = [_weight_grad("grad_w_in_hi", dz, h1, ti_in, t1k(hd_), tkk, col0=hd_, n=hd_, after=rs3[4])]
    rs4 = _copies_start("grads4_start", part4, slots(part4), rs3[4], False)

    def own_rows(a):
        n = a.shape[0] // N_DEV
        return lax.dynamic_slice(a, (my_idx * n, 0), (n, a.shape[1]))

    sums = {}
    src1, got1 = _copies_wait("grads1_wait", rs1[0], rs1[1], rs1[2], rs1[3], rs4[4], False)
    for n, a, r in zip(["w_ff1", "w_ff2", "w_ple_gate", "w_ple", "w_out_a", "w_out_b", "w_out"], src1, got1):
        sums[n] = _sum_slots("sum_" + n, r, own_rows(a))
    given_w = dict(w_in=w_in, w_out_a=w_out_a, w_out_b=w_out_b, w_out=w_out, w_ff1=w_ff1, w_ff2=w_ff2,
                   w_ple_gate=w_ple_gate, w_ple=w_ple)
    given_m = dict(w_in=m_w_in, w_out_a=m_w_out_a, w_out_b=m_w_out_b, w_out=m_w_out, w_ff1=m_w_ff1, w_ff2=m_w_ff2,
                   w_ple_gate=m_w_ple_gate, w_ple=m_w_ple)
    given_v = dict(w_in=v_w_in, w_out_a=v_w_out_a, w_out_b=v_w_out_b, w_out=v_w_out, w_ff1=v_w_ff1, w_ff2=v_w_ff2,
                   w_ple_gate=v_w_ple_gate, w_ple=v_w_ple)
    big = {}

    def update(n):
        g = sums[n].T if n in col_sharded else sums[n]
        delta, new_m, new_v = _adamw("adamw_" + n, given_w[n][0], g, given_m[n][0], given_v[n][0])
        big[n] = tuple(a[None] for a in (g, delta, new_m, new_v))

    for n in order[1:]:
        update(n)

    small_names = ["norm_mix_g", "b_gate", "q_norm_g", "k_norm_g", "rel_bias", "norm_mlp_g", "norm_ple_g",
                   "final_norm_g"]
    small_w = [norm_mix_g, b_gate, q_norm_g, k_norm_g, rel_bias, norm_mlp_g, norm_ple_g, final_norm_g]
    small_m = [m_norm_mix_g, m_b_gate, m_q_norm_g, m_k_norm_g, m_rel_bias, m_norm_mlp_g, m_norm_ple_g,
               m_final_norm_g]
    small_v = [v_norm_mix_g, v_b_gate, v_q_norm_g, v_k_norm_g, v_rel_bias, v_norm_mlp_g, v_norm_ple_g,
               v_final_norm_g]
    small_g = [dg_mix, dbg, dg_q, dg_k, d_rel, dg_mlp, dg_ple, dg_fin]
    sizes = [int(np.prod(w.shape)) for w in small_w]
    n_rows = -(-(sum(-(-sz // LANES) for sz in sizes) + 1) // 8) * 8
    pad = lambda v: jnp.pad(v.reshape(-1).astype(F32), (0, -v.size % LANES))
    pack = lambda vs, last: _pack_rows([pad(v) for v in vs] + [last], n_rows)
    zero_row = jnp.zeros((LANES,), F32)
    parts = _small_all_gather(pack(small_g, loss_part.reshape(-1) * (jnp.arange(LANES) == 0)), big["w_ple"][1])
    g_all, d_all, m_all, v_all = _small_update(parts, pack(small_w, zero_row), pack(small_m, zero_row),
                                               pack(small_v, zero_row))
    small = {}
    row = 0
    for n, w, sz in zip(small_names, small_w, sizes):
        nr = -(-sz // LANES)
        small[n] = tuple(a[row:row + nr].reshape(-1)[:sz].reshape(w.shape) for a in (g_all, d_all, m_all, v_all))
        row += nr
    loss = g_all[row, 0]

    src3, got3 = _copies_wait("grads3_wait", rs3[0], rs3[1], rs3[2], rs3[3], g_all, False)
    src4, got4 = _copies_wait("grads4_wait", rs4[0], rs4[1], rs4[2], rs4[3], g_all, False)
    sums["w_in"] = jnp.concatenate([_sum_slots("sum_w_in_lo", got3[0], own_rows(src3[0])),
                                    _sum_slots("sum_w_in_hi", got4[0], own_rows(src4[0]))], axis=1)
    update("w_in")

    names = ["norm_mix_g", "w_in", "b_gate", "q_norm_g", "k_norm_g", "rel_bias", "w_out_a", "w_out_b", "w_out",
             "norm_mlp_g", "w_ff1", "w_ff2", "norm_ple_g", "w_ple_gate", "w_ple", "final_norm_g"]
    res = {n: (big[n] if n in big else small[n]) for n in names}
    return (loss, grad_x[None], *[res[n][0] for n in names], *[res[n][1] for n in names],
            *[res[n][2] for n in names], *[res[n][3] for n in names])
```

```python
import functools
import math

import numpy as np
import jax
import jax.numpy as jnp
from jax import lax
from jax.experimental import pallas as pl
from jax.experimental.pallas import tpu as pltpu

F32 = jnp.float32
BF16 = jnp.bfloat16
MESH = pl.DeviceIdType.MESH

NORM_EPS = 1e-6
NEG_INF = -1e30
LOG2_E = math.log2(math.e)
LN_2 = math.log(2.0)
GRID_W = 64
ROPE_THETA = 10000.0
HEAD_DIM_A = 128
N_Q_HEADS_A = 8
N_KV_HEADS_A = 2
Q_PER_KV = N_Q_HEADS_A // N_KV_HEADS_A
HEAD_DIM_B = 64
N_HEADS_PER_DIL = 4
DILATIONS = (1, 4, 16)
BAND = 64
N_REL_BUCKETS = 32
REL_MAX_DIST = 1024
QA_W = N_Q_HEADS_A * HEAD_DIM_A
KA_W = N_KV_HEADS_A * HEAD_DIM_A
GB_W = N_HEADS_PER_DIL * HEAD_DIM_B
QB_W = GB_W * len(DILATIONS)
OFF_QA, OFF_KA, OFF_VA = 0, QA_W, QA_W + KA_W
OFF_QB = QA_W + 2 * KA_W
OFF_KB = OFF_QB + QB_W
OFF_VB = OFF_KB + QB_W
OFF_GA = OFF_VB + QB_W
N_DEV = 8
LANES = 128
VMEM_LIMIT = 56 * 2 ** 20

ADAM_LR, ADAM_B1, ADAM_B2, ADAM_EPS, ADAM_WD, ADAM_STEP = 0.001, 0.9, 0.999, 1e-08, 0.01, 10


def _cparams(sem):
    return pltpu.CompilerParams(dimension_semantics=sem, vmem_limit_bytes=VMEM_LIMIT)


def _resident(shape):
    nd = len(shape)
    return pl.BlockSpec(shape, lambda *_: (0,) * nd, pipeline_mode=pl.Buffered(1))


def _acc_spec(shape):
    nd = len(shape)
    return pl.BlockSpec(shape, lambda *_: (0,) * nd)


def _rows(tb, c):
    return pl.BlockSpec((tb, c), lambda i: (i, 0))


def _dil_shapes(s, dtype):
    return [jax.ShapeDtypeStruct((dil, s // dil, GB_W), dtype) for dil in DILATIONS]


def _dil_specs(tb):
    return [pl.BlockSpec((dil, tb // dil, GB_W), lambda i: (0, i, 0)) for dil in DILATIONS]


def _to_residues(val, out_ref, scr_ref, dil, dtype):
    if dil == 1:
        out_ref[0] = val.astype(dtype)
        return
    n = val.shape[0] // dil
    scr_ref[0] = val[:, :LANES]
    scr_ref[1] = val[:, LANES:]
    for r in range(dil):
        out_ref[r] = jnp.concatenate([scr_ref[0, pl.ds(r, n, stride=dil), :],
                                      scr_ref[1, pl.ds(r, n, stride=dil), :]], axis=1).astype(dtype)


def _from_residues(in_ref, scr_ref, dil):
    if dil == 1:
        return in_ref[0]
    n = in_ref.shape[1]
    for r in range(dil):
        v = in_ref[r]
        scr_ref[0, pl.ds(r, n, stride=dil), :] = v[:, :LANES]
        scr_ref[1, pl.ds(r, n, stride=dil), :] = v[:, LANES:]
    return jnp.concatenate([scr_ref[0], scr_ref[1]], axis=1)


def _dot_nt(a, b):
    return lax.dot_general(a, b, (((1,), (1,)), ((), ())), preferred_element_type=F32)


def _dot_nn(a, b):
    return lax.dot_general(a, b, (((1,), (0,)), ((), ())), preferred_element_type=F32)


def _dot_tn(a, b):
    return lax.dot_general(a, b, (((0,), (0,)), ((), ())), preferred_element_type=F32)


def _rstd(x):
    return lax.rsqrt(jnp.mean(x * x, axis=-1, keepdims=True) + NORM_EPS)


def _rms_bwd(dy, n, r, g):
    dn = dy * g
    return r * (dn - n * jnp.mean(dn * n, axis=-1, keepdims=True))


def _colsum(v):
    return jnp.sum(v, axis=0, keepdims=True)


def _sigmoid(v):
    return 1.0 / (1.0 + jnp.exp(-v))


def _rope_tables(s):
    half = HEAD_DIM_A // 2
    inv = np.power(np.float32(ROPE_THETA), -np.arange(0, half, 2, dtype=np.float32) / np.float32(half))
    t = np.arange(s)
    ang_r = (t // GRID_W).astype(np.float32)[:, None] * inv[None, :]
    ang_c = (t % GRID_W).astype(np.float32)[:, None] * inv[None, :]
    cr, sr, cc, sc = np.cos(ang_r), np.sin(ang_r), np.cos(ang_c), np.sin(ang_c)
    z = np.zeros_like(sr)
    cos = np.concatenate([cr, cr, cc, cc], axis=1)
    s1 = np.concatenate([z, sr, z, sc], axis=1)
    s2 = np.concatenate([-sr, z, -sc, z], axis=1)
    return [jnp.asarray(a, F32) for a in (cos, s1, s2)]


def _my_place():
    return lax.axis_index("x"), lax.axis_index("y"), lax.axis_index("c")


def _all_gather(shards, n_gather):
    n_all = len(shards)
    nw = n_gather

    def body(*refs):
        ins, outs = refs[:n_all], refs[n_all:2 * n_all]
        send_sems, recv_sems, local_sems = refs[2 * n_all:]
        x, y, c = _my_place()
        me, sibling = (x, y, c), (x, y, 1 - c)
        chips = [(1 - x, y), (x, 1 - y), (1 - x, 1 - y)]

        def rows(w, px, py, pc):
            n = ins[w].shape[0]
            return outs[w].at[pl.ds(pl.multiple_of((4 * px + 2 * py + pc) * n, 16), n), :]

        def copy(w, k, block, to, src=None):
            return pltpu.make_async_remote_copy(
                src_ref=rows(w, *block) if src is None else src, dst_ref=rows(w, *block),
                send_sem=send_sems.at[w, k], recv_sem=recv_sems.at[w, k], device_id=to, device_id_type=MESH)

        mine = [pltpu.make_async_copy(ins[w], rows(w, *me), local_sems.at[w]) for w in range(n_all)]
        for cp in mine:
            cp.start()
        first = []
        for w in range(nw):
            first.append(copy(w, 0, me, sibling, src=ins[w]))
            first += [copy(w, 1 + j, me, (*chip, c), src=ins[w]) for j, chip in enumerate(chips)]
        for cp in first:
            cp.start()
        passed = []
        for j, chip in enumerate(chips):
            for w in range(nw):
                copy(w, 1 + j, (*chip, c), me).wait_recv()
                fwd = copy(w, 4 + j, (*chip, c), sibling)
                fwd.start()
                passed.append(fwd)
        for w in range(nw):
            copy(w, 0, sibling, me).wait_recv()
        for j, chip in enumerate(chips):
            for w in range(nw):
                copy(w, 4 + j, (*chip, 1 - c), me).wait_recv()
        for cp in first + passed:
            cp.wait_send()
        for cp in mine:
            cp.wait()

    any_spec = pl.BlockSpec(memory_space=pl.ANY)
    return pl.pallas_call(
        body, name="weights_all_gather",
        out_shape=[jax.ShapeDtypeStruct((N_DEV * s.shape[0], s.shape[1]), s.dtype) for s in shards],
        in_specs=[any_spec] * n_all, out_specs=[any_spec] * n_all,
        scratch_shapes=[pltpu.SemaphoreType.DMA((nw, 7)), pltpu.SemaphoreType.DMA((nw, 7)),
                        pltpu.SemaphoreType.DMA((n_all,))],
    )(*shards)


def _place_own_rows(shards, my_idx):
    nw = len(shards)

    def body(idx_ref, *refs):
        for w in range(nw):
            refs[nw + w][...] = refs[w][...]

    grid_spec = pltpu.PrefetchScalarGridSpec(
        num_scalar_prefetch=1, grid=(1,),
        in_specs=[pl.BlockSpec(s.shape, lambda i, idx: (0, 0)) for s in shards],
        out_specs=[pl.BlockSpec(s.shape, lambda i, idx: (idx[0], 0)) for s in shards])
    return pl.pallas_call(
        body, name="place_own_rows", grid_spec=grid_spec,
        out_shape=[jax.ShapeDtypeStruct((N_DEV * s.shape[0], s.shape[1]), s.dtype) for s in shards],
        compiler_params=_cparams(("arbitrary",)))(my_idx.reshape(1).astype(jnp.int32), *shards)


_FLIPS = [(fx, fy, fc) for fx in (0, 1) for fy in (0, 1) for fc in (0, 1)][1:]


def _small_all_gather(v, after):
    def body(v_ref, after_ref, out_ref, send_sems, recv_sems):
        x, y, c = _my_place()
        my_idx = 4 * x + 2 * y + c
        out_ref[my_idx] = v_ref[...]
        sends = []
        for k, (fx, fy, fc) in enumerate(_FLIPS):
            to = (1 - x if fx else x, 1 - y if fy else y, 1 - c if fc else c)
            sends.append(pltpu.make_async_remote_copy(
                src_ref=v_ref, dst_ref=out_ref.at[my_idx], send_sem=send_sems.at[k], recv_sem=recv_sems.at[k],
                device_id=to, device_id_type=MESH))
        for cp in sends:
            cp.start()
        for k, (fx, fy, fc) in enumerate(_FLIPS):
            frm_idx = 4 * (1 - x if fx else x) + 2 * (1 - y if fy else y) + (1 - c if fc else c)
            pltpu.make_async_remote_copy(
                src_ref=v_ref, dst_ref=out_ref.at[frm_idx], send_sem=send_sems.at[k], recv_sem=recv_sems.at[k],
                device_id=(x, y, c), device_id_type=MESH).wait_recv()
        for cp in sends:
            cp.wait_send()

    vm = pl.BlockSpec(memory_space=pltpu.VMEM)
    return pl.pallas_call(
        body, name="small_all_gather", out_shape=jax.ShapeDtypeStruct((N_DEV,) + v.shape, v.dtype),
        in_specs=[vm, pl.BlockSpec(memory_space=pl.ANY)], out_specs=vm,
        scratch_shapes=[pltpu.SemaphoreType.DMA((7,)), pltpu.SemaphoreType.DMA((7,))],
    )(v, after)


_HBM = pl.BlockSpec(memory_space=pltpu.HBM)
_SEM = pl.BlockSpec(memory_space=pltpu.SEMAPHORE)
_ANY = pl.BlockSpec(memory_space=pl.ANY)
_SPLIT_COPY = dict(has_side_effects=pltpu.SideEffectType.DATAFLOW_SIDE_EFFECTING)


def _peer(x, y, c, k):
    fx, fy, fc = _FLIPS[k]
    return (1 - x if fx else x, 1 - y if fy else y, 1 - c if fc else c)


def _in_hbm(a):
    return pltpu.with_memory_space_constraint(a, pltpu.HBM)


def _split_copies(srcs, lands, send_sems, recv_sems, gather, arriving):
    x, y, c = _my_place()
    my_idx = 4 * x + 2 * y + c
    out = []
    for k in range(7):
        to = _peer(x, y, c, k)
        to_idx = 4 * to[0] + 2 * to[1] + to[2]
        for w in range(len(srcs)):
            if gather:
                n = srcs[w].shape[0]
                src = srcs[w]
                dst = lands[w].at[pl.ds(pl.multiple_of((to_idx if arriving else my_idx) * n, 16), n), :]
            else:
                n = lands[w].shape[1]
                src = srcs[w].at[pl.ds(pl.multiple_of(to_idx * n, 16), n), :]
                dst = lands[w].at[k]
            out.append(pltpu.make_async_remote_copy(
                src_ref=src, dst_ref=dst, send_sem=send_sems.at[7 * w + k], recv_sem=recv_sems.at[7 * w + k],
                device_id=to, device_id_type=MESH))
    return out


def _copies_start(name, srcs, lands, after, gather):
    nw = len(srcs)

    def body(*refs):
        send_sems, recv_sems = refs[2 * nw + 1], refs[2 * nw + 2]
        for cp in _split_copies(refs[:nw], refs[nw:2 * nw], send_sems, recv_sems, gather, False):
            cp.start()
        refs[-1][...] = jnp.zeros_like(refs[-1])

    sems = pltpu.SemaphoreType.DMA((7 * nw,))
    thru = [pltpu.HBM(a.shape, a.dtype) for a in list(srcs) + list(lands)]
    res = pl.pallas_call(
        body, name=name, out_shape=(sems, sems, *thru, jax.ShapeDtypeStruct((8, LANES), F32)),
        in_specs=[_HBM] * (2 * nw) + [_ANY], out_specs=(_SEM, _SEM, *[_HBM] * (2 * nw), pl.BlockSpec(memory_space=pltpu.VMEM)),
        input_output_aliases={i: 2 + i for i in range(2 * nw)},
        compiler_params=pltpu.CompilerParams(**_SPLIT_COPY),
    )(*[_in_hbm(a) for a in srcs], *[_in_hbm(a) for a in lands], after)
    return res[0], res[1], list(res[2:2 + nw]), list(res[2 + nw:2 + 2 * nw]), res[-1]


def _copies_wait(name, send_sems, recv_sems, srcs, lands, after, gather):
    nw = len(srcs)

    def body(*refs):
        for cp in _split_copies(refs[:nw], refs[nw:2 * nw], refs[2 * nw], refs[2 * nw + 1], gather, False):
            cp.wait_send()
        for cp in _split_copies(refs[:nw], refs[nw:2 * nw], refs[2 * nw], refs[2 * nw + 1], gather, True):
            cp.wait_recv()

    thru = [pltpu.HBM(a.shape, a.dtype) for a in list(srcs) + list(lands)]
    res = pl.pallas_call(
        body, name=name, out_shape=tuple(thru),
        in_specs=[_HBM] * (2 * nw) + [_SEM, _SEM, _ANY], out_specs=tuple([_HBM] * (2 * nw)),
        input_output_aliases={i: i for i in range(2 * nw)},
        compiler_params=pltpu.CompilerParams(**_SPLIT_COPY),
    )(*srcs, *lands, send_sems, recv_sems, after)
    return list(res[:nw]), list(res[nw:])


def _in_proj(x, tabs, w_in_t, g_mix, b_gate, q_g, k_g, tb, after):
    s, d = x.shape
    n_gate_chunks = d // 256
    q_scale = HEAD_DIM_A ** -0.5 * LOG2_E
    b_scale = HEAD_DIM_B ** -0.5 * LOG2_E

    def body(x_ref, c_ref, s1_ref, s2_ref, w_ref, gmix_ref, bg_ref, qg_ref, kg_ref, after_ref,
             h1_ref, qraw_ref, kraw_ref, qrot_ref, krot_ref, va_ref, *rest):
        qb_refs, kb_refs, vb_refs = rest[0:3], rest[3:6], rest[6:9]
        ga_ref, gb_ref, scr_ref = rest[9:]
        xv = x_ref[...]
        hb = (xv * _rstd(xv) * gmix_ref[...]).astype(BF16)
        h1_ref[...] = hb
        cos, s1, s2 = c_ref[...], s1_ref[...], s2_ref[...]

        def proj(lo, width):
            return _dot_nt(hb, w_ref[lo:lo + width, :])

        def norm_rope(z, g):
            n = z * _rstd(z) * g
            return n * cos + pltpu.roll(n, 32, 1) * s1 + pltpu.roll(n, 96, 1) * s2

        for j in range(QA_W // 256):
            z = proj(OFF_QA + 256 * j, 256)
            qraw_ref[:, 256 * j:256 * j + 256] = z
            for hh in range(2):
                lo = 256 * j + 128 * hh
                qrot_ref[:, lo:lo + 128] = (norm_rope(z[:, 128 * hh:128 * hh + 128], qg_ref[...]) * q_scale).astype(BF16)
        z = proj(OFF_KA, 256)
        kraw_ref[...] = z
        for hh in range(2):
            krot_ref[:, 128 * hh:128 * hh + 128] = norm_rope(z[:, 128 * hh:128 * hh + 128], kg_ref[...]).astype(BF16)
        va_ref[...] = proj(OFF_VA, 256).astype(BF16)
        for g, dil in enumerate(DILATIONS):
            _to_residues(proj(OFF_QB + GB_W * g, GB_W) * b_scale, qb_refs[g], scr_ref, dil, BF16)
            _to_residues(proj(OFF_KB + GB_W * g, GB_W), kb_refs[g], scr_ref, dil, BF16)
            _to_residues(proj(OFF_VB + GB_W * g, GB_W), vb_refs[g], scr_ref, dil, BF16)
        for j in range(n_gate_chunks):
            sl = slice(256 * j, 256 * j + 256)
            ga_ref[:, sl] = _sigmoid(proj(OFF_GA + 256 * j, 256) + bg_ref[:, sl]).astype(BF16)
            gb_ref[:, sl] = _sigmoid(
                proj(OFF_GA + d + 256 * j, 256) + bg_ref[:, d + 256 * j:d + 256 * j + 256]).astype(BF16)

    sd = jax.ShapeDtypeStruct
    outs = [sd((s, d), BF16), sd((s, QA_W), F32), sd((s, KA_W), F32), sd((s, QA_W), BF16), sd((s, KA_W), BF16),
            sd((s, KA_W), BF16)] + _dil_shapes(s, BF16) * 3 + [sd((s, d), BF16), sd((s, d), BF16)]
    out_specs = [_rows(tb, d), _rows(tb, QA_W), _rows(tb, KA_W), _rows(tb, QA_W), _rows(tb, KA_W), _rows(tb, KA_W)
                 ] + _dil_specs(tb) * 3 + [_rows(tb, d), _rows(tb, d)]
    in_specs = [_rows(tb, d), _rows(tb, LANES), _rows(tb, LANES), _rows(tb, LANES), _resident(w_in_t.shape),
                _resident(g_mix.shape), _resident(b_gate.shape), _resident(q_g.shape), _resident(k_g.shape), _ANY]
    res = list(pl.pallas_call(body, name="in_proj", grid=(s // tb,), in_specs=in_specs, out_specs=out_specs,
                              out_shape=outs, scratch_shapes=[pltpu.VMEM((2, tb, LANES), F32)],
                              compiler_params=_cparams(("arbitrary",)))(
        x, *tabs, w_in_t, g_mix, b_gate, q_g, k_g, after))
    return res[:6] + [res[6:9], res[9:12], res[12:15]] + res[15:]


def _attn_a_fwd(qrot, krot, va, tq, tk):
    s = qrot.shape[0]
    n_kv = s // tk
    gw = Q_PER_KV * HEAD_DIM_A

    def body(q_ref, k_ref, v_ref, o_ref, lse_ref):
        q4 = jnp.concatenate([q_ref[:, 128 * h:128 * h + 128] for h in range(Q_PER_KV)], axis=0)

        def step(j, carry):
            m, l, acc = carry
            sl = pl.ds(pl.multiple_of(j * tk, tk), tk)
            kj, vj = k_ref[sl, :], v_ref[sl, :]
            sc = _dot_nt(kj, q4)
            m_new = jnp.maximum(m, jnp.max(sc, axis=0, keepdims=True))
            p = jnp.exp2(sc - m_new)
            alpha = jnp.exp2(m - m_new)
            l = alpha * l + jnp.sum(p, axis=0, keepdims=True)
            acc = alpha * acc + _dot_tn(vj, p.astype(BF16))
            return m_new, l, acc

        rows = Q_PER_KV * tq
        m, l, acc = lax.fori_loop(0, n_kv, step, (jnp.full((1, rows), NEG_INF, F32), jnp.zeros((1, rows), F32),
                                                  jnp.zeros((HEAD_DIM_A, rows), F32)))
        o = (acc / l).T
        lse = m + jnp.log2(l)
        for h in range(Q_PER_KV):
            o_ref[:, 128 * h:128 * h + 128] = o[h * tq:(h + 1) * tq].astype(BF16)
            lse_ref[0, h:h + 1, :] = lse[:, h * tq:(h + 1) * tq]

    return pl.pallas_call(
        body, name="attn_a_fwd", grid=(N_KV_HEADS_A, s // tq),
        in_specs=[pl.BlockSpec((tq, gw), lambda g, i: (i, g)),
                  pl.BlockSpec((s, HEAD_DIM_A), lambda g, i: (0, g)),
                  pl.BlockSpec((s, HEAD_DIM_A), lambda g, i: (0, g))],
        out_specs=[pl.BlockSpec((tq, gw), lambda g, i: (i, g)),
                   pl.BlockSpec((1, Q_PER_KV, tq), lambda g, i: (g, 0, i))],
        out_shape=[jax.ShapeDtypeStruct((s, QA_W), BF16), jax.ShapeDtypeStruct((N_KV_HEADS_A, Q_PER_KV, s), F32)],
        compiler_params=_cparams(("arbitrary", "arbitrary")))(qrot, krot, va)


def _attn_a_bwd(qrot, krot, va, oa, doa, lse, tq, tk, after):
    s = qrot.shape[0]
    n_kv = s // tk
    gw = Q_PER_KV * HEAD_DIM_A

    def body(q_ref, do_ref, o_ref, lse_ref, k_ref, v_ref, after_ref, dq_ref, dk_ref, dv_ref):
        @pl.when(pl.program_id(1) == 0)
        def _():
            dk_ref[...] = jnp.zeros_like(dk_ref)
            dv_ref[...] = jnp.zeros_like(dv_ref)

        def stack(ref):
            return jnp.concatenate([ref[:, 128 * h:128 * h + 128] for h in range(Q_PER_KV)], axis=0)

        q4, do4, o4 = stack(q_ref), stack(do_ref), stack(o_ref)
        delta = jnp.sum(do4.astype(F32) * o4.astype(F32), axis=-1, keepdims=True)
        lse_cols = jnp.concatenate([lse_ref[0], jnp.zeros_like(lse_ref[0])], axis=0).T
        lse4 = jnp.concatenate([lse_cols[:, h:h + 1] for h in range(Q_PER_KV)], axis=0)

        def step(j, dq):
            sl = pl.ds(pl.multiple_of(j * tk, tk), tk)
            kj, vj = k_ref[sl, :], v_ref[sl, :]
            p = jnp.exp2(_dot_nt(q4, kj) - lse4)
            ds = (p * (_dot_nt(do4, vj) - delta)).astype(BF16)
            dk_ref[sl, :] += _dot_tn(ds, q4)
            dv_ref[sl, :] += _dot_tn(p.astype(BF16), do4)
            return dq + _dot_nn(ds, kj)

        dq = lax.fori_loop(0, n_kv, step, jnp.zeros((Q_PER_KV * tq, HEAD_DIM_A), F32))
        for h in range(Q_PER_KV):
            dq_ref[:, 128 * h:128 * h + 128] = dq[h * tq:(h + 1) * tq]

    qspec = pl.BlockSpec((tq, gw), lambda g, i: (i, g))
    kspec = pl.BlockSpec((s, HEAD_DIM_A), lambda g, i: (0, g))
    return pl.pallas_call(
        body, name="attn_a_bwd", grid=(N_KV_HEADS_A, s // tq),
        in_specs=[qspec, qspec, qspec, pl.BlockSpec((1, Q_PER_KV, tq), lambda g, i: (g, 0, i)), kspec, kspec, _ANY],
        out_specs=[qspec, kspec, kspec],
        out_shape=[jax.ShapeDtypeStruct((s, QA_W), F32), jax.ShapeDtypeStruct((s, KA_W), F32),
                   jax.ShapeDtypeStruct((s, KA_W), F32)],
        compiler_params=_cparams(("arbitrary", "arbitrary")))(qrot, doa, oa, lse, krot, va, after)


BAND_QB = 128
BAND_WIN = BAND_QB + 2 * BAND


def _band_specs(s, cb):
    per = cb // BAND
    last = s // BAND - 1
    cur = pl.BlockSpec((cb, GB_W), lambda i: (i, 0))
    prev = pl.BlockSpec((BAND, GB_W), lambda i: (jnp.maximum(i * per - 1, 0), 0))
    nxt = pl.BlockSpec((BAND, GB_W), lambda i: (jnp.minimum(i * per + per, last), 0))
    return cur, prev, nxt


def _window(prev_ref, cur_ref, next_ref):
    return jnp.concatenate([prev_ref[...], cur_ref[...], next_ref[...]], axis=0)


def _band_mask(base, seg_shift):
    rq = base + lax.broadcasted_iota(jnp.int32, (BAND_QB, BAND_WIN), 0)
    rk = base - BAND + lax.broadcasted_iota(jnp.int32, (BAND_QB, BAND_WIN), 1)
    same_segment = lax.shift_right_arithmetic(rq, jnp.int32(seg_shift)) == lax.shift_right_arithmetic(rk, jnp.int32(seg_shift))
    return (jnp.abs(rk - rq) <= BAND) & same_segment


def _build_bias(bmap_ref, tab_ref, bias_ref):
    bm = bmap_ref[...]
    acc = [jnp.full(bm.shape, NEG_INF, F32) for _ in range(N_HEADS_PER_DIL)]
    for b in range(N_REL_BUCKETS):
        hit = bm == b
        for h in range(N_HEADS_PER_DIL):
            acc[h] = jnp.where(hit, tab_ref[b, h] * LOG2_E, acc[h])
    rows = bm.shape[0]
    for h in range(N_HEADS_PER_DIL):
        bias_ref[h * rows:(h + 1) * rows, :] = acc[h]


def _segment_mask(base, seg_len, seg_shift):
    if seg_len % BAND_QB:
        return _band_mask(base, seg_shift)
    pos = lax.rem(base, seg_len)
    w = lax.broadcasted_iota(jnp.int32, (1, BAND_WIN), 1)
    return ((w >= BAND) | (pos != 0)) & ((w < BAND + BAND_QB) | (pos != seg_len - BAND_QB))


def _head_lane_masks():
    lane = lax.broadcasted_iota(jnp.int32, (1, LANES), 1)
    return [lane < HEAD_DIM_B, lane >= HEAD_DIM_B]


def _rows4(mask):
    return mask if mask.shape[0] == 1 else jnp.concatenate([mask] * N_HEADS_PER_DIL, axis=0)


def _head_scores(a, b):
    hm = _head_lane_masks()
    out = []
    for hp in range(2):
        ls = slice(LANES * hp, LANES * hp + LANES)
        ah = a[:, ls]
        both = jnp.concatenate([jnp.where(hm[0], ah, jnp.zeros_like(ah)), jnp.where(hm[1], ah, jnp.zeros_like(ah))],
                               axis=0)
        out.append(_dot_nt(both, b[:, ls]))
    return jnp.concatenate(out, axis=0)


def _head_combine(p, v, scale=None, transposed=False):
    hm = _head_lane_masks()
    rows = p.shape[0] // N_HEADS_PER_DIL
    halves = []
    for hp in range(2):
        vh = v[:, LANES * hp:LANES * hp + LANES]
        acc = None
        for hh in range(2):
            h = 2 * hp + hh
            ph = p[h * rows:(h + 1) * rows]
            vm = jnp.where(hm[hh], vh, jnp.zeros_like(vh))
            t = _dot_tn(ph, vm) if transposed else _dot_nn(ph, vm)
            if scale is not None:
                t = t * scale[h * rows:(h + 1) * rows]
            acc = t if acc is None else acc + t
        halves.append(acc)
    return jnp.concatenate(halves, axis=1)


def _head_spread(col):
    rows = col.shape[0] // N_HEADS_PER_DIL
    lane = lax.broadcasted_iota(jnp.int32, (1, GB_W), 1)
    out = jnp.zeros((rows, GB_W), F32)
    for h in range(N_HEADS_PER_DIL):
        out = jnp.where((lane >= HEAD_DIM_B * h) & (lane < HEAD_DIM_B * (h + 1)), col[h * rows:(h + 1) * rows], out)
    return out


def _head_cols(v):
    return jnp.concatenate([v[:, HEAD_DIM_B * h:HEAD_DIM_B * h + 1] for h in range(N_HEADS_PER_DIL)], axis=0)


def _seg_shift(s, dil):
    seg = s // dil
    assert seg & (seg - 1) == 0, "segment length must be a power of two"
    return seg.bit_length() - 1


def _band_fwd(dil, qb, kb, vb, bmap, tab, cb):
    s = qb.shape[0]
    shift = _seg_shift(s, dil)

    def body(q_ref, kp_ref, kc_ref, kn_ref, vp_ref, vc_ref, vn_ref, bmap_ref, tab_ref, o_ref, lse_ref, bias_ref):
        @pl.when(pl.program_id(0) == 0)
        def _():
            _build_bias(bmap_ref, tab_ref, bias_ref)

        kw, vw = _window(kp_ref, kc_ref, kn_ref), _window(vp_ref, vc_ref, vn_ref)
        for jj in range(cb // BAND_QB):
            r0 = BAND_QB * jj
            mask = _rows4(_segment_mask(pl.program_id(0) * cb + r0, s // dil, shift))
            sc = _head_scores(q_ref[r0:r0 + BAND_QB, :], kw[r0:r0 + BAND_WIN, :]) + bias_ref[...]
            sc = jnp.where(mask, sc, NEG_INF)
            m = jnp.max(sc, axis=-1, keepdims=True)
            e = jnp.exp2(sc - m)
            l = jnp.sum(e, axis=-1, keepdims=True)
            o = _head_combine(e.astype(BF16), vw[r0:r0 + BAND_WIN, :], 1.0 / l)
            o_ref[r0:r0 + BAND_QB, :] = o
            lse_ref[r0:r0 + BAND_QB, :] = _head_spread(m + jnp.log2(l))

    cur, prev, nxt = _band_specs(s, cb)
    return pl.pallas_call(
        body, name=f"band_fwd_d{dil}", grid=(s // cb,),
        in_specs=[cur, prev, cur, nxt, prev, cur, nxt, _resident(bmap.shape), pl.BlockSpec(memory_space=pltpu.SMEM)],
        out_specs=[cur, cur],
        out_shape=[jax.ShapeDtypeStruct(qb.shape, F32), jax.ShapeDtypeStruct(qb.shape, F32)],
        scratch_shapes=[pltpu.VMEM((N_HEADS_PER_DIL * BAND_QB, BAND_WIN), F32)],
        compiler_params=_cparams(("arbitrary",)))(qb, kb, kb, kb, vb, vb, vb, bmap, tab)


def _band_bwd(dil, qb, kb, vb, dob, lse, dd, bmap, tab, cb):
    s = qb.shape[0]
    shift = _seg_shift(s, dil)
    n_steps = s // cb

    def body(q_ref, do_ref, lse_ref, dd_ref, kp_ref, kc_ref, kn_ref, vp_ref, vc_ref, vn_ref, bmap_ref, tab_ref,
             dq_ref, dk_ref, dv_ref, dtab_ref, bias_ref, dsum_ref):
        @pl.when(pl.program_id(0) == 0)
        def _():
            _build_bias(bmap_ref, tab_ref, bias_ref)
            dsum_ref[...] = jnp.zeros_like(dsum_ref)
            dk_ref[...] = jnp.zeros_like(dk_ref)
            dv_ref[...] = jnp.zeros_like(dv_ref)

        kw, vw = _window(kp_ref, kc_ref, kn_ref), _window(vp_ref, vc_ref, vn_ref)
        for jj in range(cb // BAND_QB):
            r0 = BAND_QB * jj
            base = pl.program_id(0) * cb + r0
            mask = _rows4(_segment_mask(base, s // dil, shift))
            qh, doh = q_ref[r0:r0 + BAND_QB, :], do_ref[r0:r0 + BAND_QB, :]
            k3, v3 = kw[r0:r0 + BAND_WIN, :], vw[r0:r0 + BAND_WIN, :]
            sc = _head_scores(qh, k3) + bias_ref[...]
            sc = jnp.where(mask, sc, NEG_INF)
            p = jnp.exp2(sc - _head_cols(lse_ref[r0:r0 + BAND_QB, :]))
            dp = _head_scores(doh, v3)
            ds = p * (dp - _head_cols(dd_ref[r0:r0 + BAND_QB, :]))
            dsum_ref[...] += ds
            dsb = ds.astype(BF16)
            dq_ref[r0:r0 + BAND_QB, :] = _head_combine(dsb, k3)
            dk_win = _head_combine(dsb, qh, transposed=True)
            dv_win = _head_combine(p.astype(BF16), doh, transposed=True)
            own = pl.ds(pl.multiple_of(base, BAND), BAND_QB)
            dk_ref[own, :] += dk_win[BAND:BAND + BAND_QB]
            dv_ref[own, :] += dv_win[BAND:BAND + BAND_QB]

            @pl.when(base > 0)
            def _():
                before = pl.ds(pl.multiple_of(base - BAND, BAND), BAND)
                dk_ref[before, :] += dk_win[:BAND]
                dv_ref[before, :] += dv_win[:BAND]

            @pl.when(base + BAND_QB < s)
            def _():
                after = pl.ds(pl.multiple_of(base + BAND_QB, BAND), BAND)
                dk_ref[after, :] += dk_win[BAND + BAND_QB:]
                dv_ref[after, :] += dv_win[BAND + BAND_QB:]

        @pl.when(pl.program_id(0) == n_steps - 1)
        def _():
            bm = bmap_ref[...]
            lane = lax.broadcasted_iota(jnp.int32, (1, LANES), 1)
            for b in range(N_REL_BUCKETS):
                hit = bm == b
                row = jnp.zeros((1, LANES), F32)
                for h in range(N_HEADS_PER_DIL):
                    part = dsum_ref[h * BAND_QB:(h + 1) * BAND_QB, :]
                    row = jnp.where(lane == h, jnp.sum(jnp.where(hit, part, 0.0)), row)
                dtab_ref[b:b + 1, :] = row

    cur, prev, nxt = _band_specs(s, cb)
    whole = _acc_spec(qb.shape)
    return pl.pallas_call(
        body, name=f"band_bwd_d{dil}", grid=(n_steps,),
        in_specs=[cur, cur, cur, cur, prev, cur, nxt, prev, cur, nxt, _resident(bmap.shape),
                  pl.BlockSpec(memory_space=pltpu.SMEM)],
        out_specs=[cur, whole, whole, _acc_spec((N_REL_BUCKETS, LANES))],
        out_shape=[jax.ShapeDtypeStruct(qb.shape, F32)] * 3 + [jax.ShapeDtypeStruct((N_REL_BUCKETS, LANES), F32)],
        scratch_shapes=[pltpu.VMEM((N_HEADS_PER_DIL * BAND_QB, BAND_WIN), F32),
                        pltpu.VMEM((N_HEADS_PER_DIL * BAND_QB, BAND_WIN), F32)],
        compiler_params=_cparams(("arbitrary",)))(qb, dob, lse, dd, kb, kb, kb, vb, vb, vb, bmap, tab)


def _t5_bucket(rel):
    nb = N_REL_BUCKETS // 2
    ret = (rel > 0).astype(np.int32) * nb
    n = np.abs(rel)
    max_exact = nb // 2
    large = max_exact + (np.log(np.maximum(n, 1) / max_exact) / math.log(REL_MAX_DIST / max_exact)
                         * (nb - max_exact)).astype(np.int32)
    large = np.minimum(large, nb - 1)
    return ret + np.where(n < max_exact, n, large).astype(np.int32)


def _bucket_map(dil):
    off = np.arange(BAND_WIN)[None, :] - BAND - np.arange(BAND_QB)[:, None]
    return np.where(np.abs(off) <= BAND, _t5_bucket(off * dil), -1).astype(np.int32)


def _seg_sum(v):
    lane = lax.broadcasted_iota(jnp.int32, (1, v.shape[1]), 1)
    out = jnp.zeros_like(v)
    for h in range(v.shape[1] // HEAD_DIM_B):
        m = (lane >= HEAD_DIM_B * h) & (lane < HEAD_DIM_B * (h + 1))
        out = jnp.where(m, jnp.sum(jnp.where(m, v, 0.0), axis=-1, keepdims=True), out)
    return out


def _mix_out(x, oa, og, lg, ga, gb, w_oa, w_ob_t, w_o, tb):
    s, d = x.shape

    def body(x_ref, oa_ref, og0_ref, og1_ref, og2_ref, lg0_ref, lg1_ref, lg2_ref, ga_ref, gb_ref,
             woa_ref, wob_ref, wo_ref, x2_ref, ob_ref, lse0_ref, lse1_ref, lse2_ref, ya_ref, yb_ref, u_ref, scr_ref):
        og_refs, lg_refs = (og0_ref, og1_ref, og2_ref), (lg0_ref, lg1_ref, lg2_ref)
        l0, l1, l2 = [_from_residues(lg_refs[g], scr_ref, dil) for g, dil in enumerate(DILATIONS)]
        lmax = jnp.maximum(jnp.maximum(l0, l1), l2)
        w0, w1, w2 = jnp.exp2(l0 - lmax), jnp.exp2(l1 - lmax), jnp.exp2(l2 - lmax)
        den = w0 + w1 + w2
        o0, o1, o2 = [_from_residues(og_refs[g], scr_ref, dil) for g, dil in enumerate(DILATIONS)]
        ob = ((w0 * o0 + w1 * o1 + w2 * o2) / den).astype(BF16)
        ob_ref[...] = ob
        lse = lmax + jnp.log2(den)
        for g, (dil, ref) in enumerate(zip(DILATIONS, (lse0_ref, lse1_ref, lse2_ref))):
            _to_residues(lse, ref, scr_ref, dil, F32)
        ya = _dot_nn(oa_ref[...], woa_ref[...])
        yb = _dot_nt(ob, wob_ref[...])
        ya_ref[...] = ya.astype(BF16)
        yb_ref[...] = yb.astype(BF16)
        u = (ga_ref[...].astype(F32) * ya + gb_ref[...].astype(F32) * yb).astype(BF16)
        u_ref[...] = u
        x2_ref[...] = x_ref[...] + _dot_nn(u, wo_ref[...])

    sd = jax.ShapeDtypeStruct
    res = list(pl.pallas_call(
        body, name="mix_out", grid=(s // tb,),
        in_specs=[_rows(tb, d), _rows(tb, QA_W)] + _dil_specs(tb) * 2 + [
            _rows(tb, d), _rows(tb, d), _resident(w_oa.shape), _resident(w_ob_t.shape), _resident(w_o.shape)],
        out_specs=[_rows(tb, d), _rows(tb, GB_W)] + _dil_specs(tb) + [_rows(tb, d), _rows(tb, d), _rows(tb, d)],
        out_shape=[sd((s, d), F32), sd((s, GB_W), BF16)] + _dil_shapes(s, F32) + [
            sd((s, d), BF16), sd((s, d), BF16), sd((s, d), BF16)],
        scratch_shapes=[pltpu.VMEM((2, tb, LANES), F32)],
        compiler_params=_cparams(("arbitrary",)))(x, oa, *og, *lg, ga, gb, w_oa, w_ob_t, w_o))
    return res[:2] + [res[2:5]] + res[5:]


def _mlp_fwd(x2, w1_t, w2, g_mlp, tb, tc):
    s, d = x2.shape
    dff = w1_t.shape[0]

    def body(x_ref, w1_ref, w2_ref, g_ref, x3_ref, r_ref, h_ref):
        xv = x_ref[...]
        hb = (xv * _rstd(xv) * g_ref[...]).astype(BF16)
        h_ref[...] = hb
        x3_ref[...] = xv
        for c in range(dff // tc):
            sl = slice(tc * c, tc * c + tc)
            r = jnp.maximum(_dot_nt(hb, w1_ref[sl, :]), 0.0)
            r_ref[:, sl] = r.astype(BF16)
            x3_ref[...] += _dot_nn((r * r).astype(BF16), w2_ref[sl, :])

    sd = jax.ShapeDtypeStruct
    return pl.pallas_call(
        body, name="mlp_fwd", grid=(s // tb,),
        in_specs=[_rows(tb, d), _resident(w1_t.shape), _resident(w2.shape), _resident(g_mlp.shape)],
        out_specs=[_rows(tb, d), _rows(tb, dff), _rows(tb, d)],
        out_shape=[sd((s, d), F32), sd((s, dff), BF16), sd((s, d), BF16)],
        compiler_params=_cparams(("arbitrary",)))(x2, w1_t, w2, g_mlp)


def _ple_loss(x3, p, target, w_pg, w_p_t, g_ple, g_fin, tb):
    s, d = x3.shape
    dp = p.shape[1]

    def body(x_ref, p_ref, t_ref, wpg_ref, wp_ref, gple_ref, gfin_ref,
             dx3_ref, h3_ref, dpre_ref, dpe_ref, pb_ref, loss_ref, dgfin_ref, dgple_ref):
        @pl.when(pl.program_id(0) == 0)
        def _():
            loss_ref[...] = jnp.zeros_like(loss_ref)
            dgfin_ref[...] = jnp.zeros_like(dgfin_ref)
            dgple_ref[...] = jnp.zeros_like(dgple_ref)

        x3v = x_ref[...]
        r3 = _rstd(x3v)
        n3 = x3v * r3
        h3 = (n3 * gple_ref[...]).astype(BF16)
        h3_ref[...] = h3
        gp = _sigmoid(_dot_nn(h3, wpg_ref[...]))
        pb = p_ref[...].astype(BF16)
        pb_ref[...] = pb
        pe = _dot_nt(pb, wp_ref[...])
        x4 = x3v + gp * pe
        r4 = _rstd(x4)
        n4 = x4 * r4
        err = n4 * gfin_ref[...] - t_ref[...]
        loss_ref[...] += jnp.sum(0.5 * jnp.mean(err * err, axis=-1, keepdims=True), axis=0, keepdims=True)
        dy = err / d
        dgfin_ref[...] += _colsum(dy * n4)
        dx4 = _rms_bwd(dy, n4, r4, gfin_ref[...])
        dpe_ref[...] = (dx4 * gp).astype(BF16)
        dpre = (dx4 * pe * gp * (1.0 - gp)).astype(BF16)
        dpre_ref[...] = dpre
        dh3 = _dot_nt(dpre, wpg_ref[...])
        dgple_ref[...] += _colsum(dh3 * n3)
        dx3_ref[...] = dx4 + _rms_bwd(dh3, n3, r3, gple_ref[...])

    sd = jax.ShapeDtypeStruct
    return pl.pallas_call(
        body, name="ple_loss", grid=(s // tb,),
        in_specs=[_rows(tb, d), _rows(tb, dp), _rows(tb, d), _resident(w_pg.shape), _resident(w_p_t.shape),
                  _resident(g_ple.shape), _resident(g_fin.shape)],
        out_specs=[_rows(tb, d), _rows(tb, d), _rows(tb, d), _rows(tb, d), _rows(tb, dp),
                   _acc_spec((1, LANES)), _acc_spec((1, d)), _acc_spec((1, d))],
        out_shape=[sd((s, d), F32), sd((s, d), BF16), sd((s, d), BF16), sd((s, d), BF16), sd((s, dp), BF16),
                   sd((1, LANES), F32), sd((1, d), F32), sd((1, d), F32)],
        compiler_params=_cparams(("arbitrary",)))(x3, p, target, w_pg, w_p_t, g_ple, g_fin)


def _mlp_bwd(dx3, x2, r, w1_t, w2, g_mlp, tb, tc):
    s, d = x2.shape
    dff = w1_t.shape[0]

    def body(dx3_ref, x_ref, r_ref, w1_ref, w2_ref, g_ref, dx2_ref, df_ref, dg_ref, dh_ref):
        @pl.when(pl.program_id(0) == 0)
        def _():
            dg_ref[...] = jnp.zeros_like(dg_ref)

        dx3v = dx3_ref[...]
        dx3b = dx3v.astype(BF16)
        dh_ref[...] = jnp.zeros_like(dh_ref)
        for c in range(dff // tc):
            sl = slice(tc * c, tc * c + tc)
            df = (_dot_nt(dx3b, w2_ref[sl, :]) * (2.0 * r_ref[:, sl].astype(F32))).astype(BF16)
            df_ref[:, sl] = df
            dh_ref[...] += _dot_nn(df, w1_ref[sl, :])
        xv = x_ref[...]
        r2 = _rstd(xv)
        n2 = xv * r2
        dh = dh_ref[...]
        dg_ref[...] += _colsum(dh * n2)
        dx2_ref[...] = dx3v + _rms_bwd(dh, n2, r2, g_ref[...])

    sd = jax.ShapeDtypeStruct
    return pl.pallas_call(
        body, name="mlp_bwd", grid=(s // tb,),
        in_specs=[_rows(tb, d), _rows(tb, d), _rows(tb, dff), _resident(w1_t.shape), _resident(w2.shape),
                  _resident(g_mlp.shape)],
        out_specs=[_rows(tb, d), _rows(tb, dff), _acc_spec((1, d))],
        out_shape=[sd((s, d), F32), sd((s, dff), BF16), sd((1, d), F32)],
        scratch_shapes=[pltpu.VMEM((tb, d), F32)],
        compiler_params=_cparams(("arbitrary",)))(dx3, x2, r, w1_t, w2, g_mlp)


def _mix_out_bwd(dx2, ya, yb, ga, gb, ob, w_oa, w_ob_t, w_o, tb, after):
    s, d = dx2.shape

    def body(dx_ref, ya_ref, yb_ref, ga_ref, gb_ref, ob_ref, woa_ref, wob_ref, wo_ref, after_ref,
             doa_ref, dob0_ref, dob1_ref, dob2_ref, dd0_ref, dd1_ref, dd2_ref, dga_ref, dgb_ref, dya_ref, dyb_ref,
             dbg_ref, scr_ref):
        @pl.when(pl.program_id(0) == 0)
        def _():
            dbg_ref[...] = jnp.zeros_like(dbg_ref)

        du = _dot_nt(dx_ref[...].astype(BF16), wo_ref[...])
        gav, gbv = ga_ref[...].astype(F32), gb_ref[...].astype(F32)
        dya = (du * gav).astype(BF16)
        dyb = (du * gbv).astype(BF16)
        dya_ref[...] = dya
        dyb_ref[...] = dyb
        dga = du * ya_ref[...].astype(F32) * gav * (1.0 - gav)
        dgb = du * yb_ref[...].astype(F32) * gbv * (1.0 - gbv)
        dga_ref[...] = dga.astype(BF16)
        dgb_ref[...] = dgb.astype(BF16)
        dbg_ref[:, 0:d] += _colsum(dga)
        dbg_ref[:, d:2 * d] += _colsum(dgb)
        doa_ref[...] = _dot_nt(dya, woa_ref[...]).astype(BF16)
        dob = _dot_nn(dyb, wob_ref[...])
        dd = _seg_sum(dob * ob_ref[...].astype(F32))
        for dil, dob_ref, dd_ref in zip(DILATIONS, (dob0_ref, dob1_ref, dob2_ref), (dd0_ref, dd1_ref, dd2_ref)):
            _to_residues(dob, dob_ref, scr_ref, dil, BF16)
            _to_residues(dd, dd_ref, scr_ref, dil, F32)

    sd = jax.ShapeDtypeStruct
    res = list(pl.pallas_call(
        body, name="mix_out_bwd", grid=(s // tb,),
        in_specs=[_rows(tb, d)] * 5 + [_rows(tb, GB_W), _resident(w_oa.shape), _resident(w_ob_t.shape),
                                       _resident(w_o.shape), _ANY],
        out_specs=[_rows(tb, QA_W)] + _dil_specs(tb) * 2 + [_rows(tb, d), _rows(tb, d), _rows(tb, d),
                                                           _rows(tb, d), _acc_spec((1, 2 * d))],
        out_shape=[sd((s, QA_W), BF16)] + _dil_shapes(s, BF16) + _dil_shapes(s, F32) + [
            sd((s, d), BF16), sd((s, d), BF16), sd((s, d), BF16), sd((s, d), BF16), sd((1, 2 * d), F32)],
        scratch_shapes=[pltpu.VMEM((2, tb, LANES), F32)],
        compiler_params=_cparams(("arbitrary",)))(dx2, ya, yb, ga, gb, ob, w_oa, w_ob_t, w_o, after))
    return res[:1] + [res[1:4], res[4:7]] + res[7:]


def _in_proj_bwd(dx2, x, dqrot, dkrot, dva, qraw, kraw, tabs, dqb, dkb, dvb, dga, dgb, w_in_t, g_mix, q_g, k_g, tb):
    s, d = x.shape
    din = w_in_t.shape[0]
    q_scale = HEAD_DIM_A ** -0.5
    b_scale = HEAD_DIM_B ** -0.5
    tc = 256

    def body(dx2_ref, x_ref, dq_ref, dk_ref, dv_ref, qraw_ref, kraw_ref, c_ref, s1_ref, s2_ref, *rest):
        dqb_refs, dkb_refs, dvb_refs = rest[0:3], rest[3:6], rest[6:9]
        (dga_ref, dgb_ref, w_ref, gmix_ref, qg_ref, kg_ref,
         dx_ref, dz_ref, dgmix_ref, dqg_ref, dkg_ref, dh_ref, scr_ref) = rest[9:]

        @pl.when(pl.program_id(0) == 0)
        def _():
            dgmix_ref[...] = jnp.zeros_like(dgmix_ref)
            dqg_ref[...] = jnp.zeros_like(dqg_ref)
            dkg_ref[...] = jnp.zeros_like(dkg_ref)

        cos, s1, s2 = c_ref[...][None], s1_ref[...][None], s2_ref[...][None]

        def heads_bwd(drot, z, g_ref, acc_ref):
            dn = drot * cos + pltpu.roll(drot * s1, 96, 2) + pltpu.roll(drot * s2, 32, 2)
            rr = _rstd(z)
            nn = z * rr
            acc_ref[...] += jnp.sum(jnp.sum(dn * nn, axis=0), axis=0, keepdims=True)
            return _rms_bwd(dn, nn, rr, g_ref[...][None]).astype(BF16)

        dh_ref[...] = jnp.zeros_like(dh_ref)

        def emit(off, piece):
            dz_ref[:, off:off + tc] = piece
            dh_ref[...] += _dot_nn(piece, w_ref[off:off + tc, :])

        for j in range(d // tc):
            emit(OFF_GA + tc * j, dga_ref[:, tc * j:tc * j + tc])
            emit(OFF_GA + d + tc * j, dgb_ref[:, tc * j:tc * j + tc])
        emit(OFF_VA, dv_ref[...].astype(BF16))
        for g, dil in enumerate(DILATIONS):
            emit(OFF_QB + GB_W * g, (_from_residues(dqb_refs[g], scr_ref, dil) * b_scale).astype(BF16))
            emit(OFF_KB + GB_W * g, (_from_residues(dkb_refs[g], scr_ref, dil) * LN_2).astype(BF16))
            emit(OFF_VB + GB_W * g, _from_residues(dvb_refs[g], scr_ref, dil).astype(BF16))
        stack = lambda ref, n: jnp.stack([ref[:, 128 * h:128 * h + 128] for h in range(n)], axis=0)
        dzq = heads_bwd(stack(dq_ref, N_Q_HEADS_A) * q_scale, stack(qraw_ref, N_Q_HEADS_A), qg_ref, dqg_ref)
        dzk = heads_bwd(stack(dk_ref, N_KV_HEADS_A) * LN_2, stack(kraw_ref, N_KV_HEADS_A), kg_ref, dkg_ref)
        for j in range(N_Q_HEADS_A // 2):
            emit(OFF_QA + tc * j, jnp.concatenate([dzq[2 * j], dzq[2 * j + 1]], axis=1))
        emit(OFF_KA, jnp.concatenate([dzk[0], dzk[1]], axis=1))
        xv = x_ref[...]
        r1 = _rstd(xv)
        n1 = xv * r1
        dh = dh_ref[...]
        dgmix_ref[...] += _colsum(dh * n1)
        dx_ref[...] = dx2_ref[...] + _rms_bwd(dh, n1, r1, gmix_ref[...])

    sd = jax.ShapeDtypeStruct
    return pl.pallas_call(
        body, name="in_proj_bwd", grid=(s // tb,),
        in_specs=[_rows(tb, d), _rows(tb, d), _rows(tb, QA_W), _rows(tb, KA_W), _rows(tb, KA_W), _rows(tb, QA_W),
                  _rows(tb, KA_W), _rows(tb, LANES), _rows(tb, LANES), _rows(tb, LANES),
                  ] + _dil_specs(tb) * 3 + [_rows(tb, d), _rows(tb, d),
                  _resident(w_in_t.shape), _resident(g_mix.shape), _resident(q_g.shape), _resident(k_g.shape)],
        out_specs=[_rows(tb, d), _rows(tb, din), _acc_spec((1, d)), _acc_spec((1, HEAD_DIM_A)),
                   _acc_spec((1, HEAD_DIM_A))],
        out_shape=[sd((s, d), F32), sd((s, din), BF16), sd((1, d), F32), sd((1, HEAD_DIM_A), F32),
                   sd((1, HEAD_DIM_A), F32)],
        scratch_shapes=[pltpu.VMEM((tb, d), F32), pltpu.VMEM((2, tb, LANES), F32)],
        compiler_params=_cparams(("arbitrary",)))(
        dx2, x, dqrot, dkrot, dva, qraw, kraw, *tabs, *dqb, *dkb, *dvb, dga, dgb, w_in_t, g_mix, q_g, k_g)


def _identity(v):
    return v


def _to_bf16(v):
    return v.astype(BF16)


def _square_bf16(v):
    vf = v.astype(F32)
    return (vf * vf).astype(BF16)


def _weight_grad(name, a, b, ti, tj, tk, a_fn=_identity, b_fn=_identity, col0=0, n=None, after=None):
    t, m = a.shape
    n = b.shape[1] if n is None else n
    n_k = t // tk
    after = a if after is None else after

    def body(a_ref, b_ref, after_ref, o_ref, acc_ref):
        k = pl.program_id(2)

        @pl.when(k == 0)
        def _():
            acc_ref[...] = jnp.zeros_like(acc_ref)

        acc_ref[...] += _dot_tn(a_fn(a_ref[...]), b_fn(b_ref[...]))

        @pl.when(k == n_k - 1)
        def _():
            o_ref[...] = acc_ref[...].astype(BF16)

    return pl.pallas_call(
        body, name=name, grid=(m // ti, n // tj, n_k),
        in_specs=[pl.BlockSpec((tk, ti), lambda i, j, k: (k, i)),
                  pl.BlockSpec((tk, tj), lambda i, j, k: (k, j + col0 // tj)), _ANY],
        out_specs=pl.BlockSpec((ti, tj), lambda i, j, k: (i, j)),
        out_shape=jax.ShapeDtypeStruct((m, n), BF16),
        scratch_shapes=[pltpu.VMEM((ti, tj), F32)],
        compiler_params=_cparams(("arbitrary", "arbitrary", "arbitrary")))(a, b, after)


def _sum_slots(name, recv, own):
    m, n, k = recv.shape
    tc = min(k, 256)

    def body(own_ref, r_ref, o_ref):
        acc = own_ref[...].astype(F32)
        for i in range(m):
            acc = acc + r_ref[i].astype(F32)
        o_ref[...] = acc

    return pl.pallas_call(
        body, name=name, grid=(k // tc,),
        in_specs=[pl.BlockSpec((n, tc), lambda j: (0, j)), pl.BlockSpec((m, n, tc), lambda j: (0, 0, j))],
        out_specs=pl.BlockSpec((n, tc), lambda j: (0, j)),
        out_shape=jax.ShapeDtypeStruct((n, k), F32),
        compiler_params=_cparams(("arbitrary",)))(own, recv)


def _adamw_math(w, g, m, v):
    m = ADAM_B1 * m + (1.0 - ADAM_B1) * g
    v = ADAM_B2 * v + (1.0 - ADAM_B2) * (g * g)
    m_hat = m / (1.0 - ADAM_B1 ** ADAM_STEP)
    v_hat = v / (1.0 - ADAM_B2 ** ADAM_STEP)
    delta = -ADAM_LR * (m_hat / (jnp.sqrt(v_hat) + ADAM_EPS) + ADAM_WD * w)
    return delta, m, v


def _adamw(name, w, g, m, v):
    r, c = w.shape
    tr = min(r, 256)

    def body(w_ref, g_ref, m_ref, v_ref, d_ref, mo_ref, vo_ref):
        d_ref[...], mo_ref[...], vo_ref[...] = _adamw_math(w_ref[...], g_ref[...], m_ref[...], v_ref[...])

    spec = pl.BlockSpec((tr, c), lambda i: (i, 0))
    return pl.pallas_call(
        body, name=name, grid=(r // tr,), in_specs=[spec] * 4, out_specs=[spec] * 3,
        out_shape=[jax.ShapeDtypeStruct((r, c), F32)] * 3,
        compiler_params=_cparams(("arbitrary",)))(w, g, m, v)


def _small_update(parts, w, m, v):
    def body(p_ref, w_ref, m_ref, v_ref, g_ref, d_ref, mo_ref, vo_ref):
        g = p_ref[0]
        for i in range(1, N_DEV):
            g = g + p_ref[i]
        g_ref[...] = g
        d_ref[...], mo_ref[...], vo_ref[...] = _adamw_math(w_ref[...], g, m_ref[...], v_ref[...])

    return pl.pallas_call(body, name="small_update", out_shape=[jax.ShapeDtypeStruct(w.shape, F32)] * 4)(
        parts, w, m, v)


def _pack_rows(vectors, n_rows):
    flat = jnp.concatenate([v.reshape(-1).astype(F32) for v in vectors])
    flat = jnp.pad(flat, (0, n_rows * LANES - flat.shape[0]))
    return flat.reshape(n_rows, LANES)


def _pick_tile(n, prefs):
    for t in prefs:
        if n % t == 0:
            return t
    return n


def kernel(x, p, norm_mix_g, w_in, b_gate, q_norm_g, k_norm_g, rel_bias, w_out_a, w_out_b, w_out, norm_mlp_g, w_ff1, w_ff2, norm_ple_g, w_ple_gate, w_ple, final_norm_g, loss_target, m_norm_mix_g, m_w_in, m_b_gate, m_q_norm_g, m_k_norm_g, m_rel_bias, m_w_out_a, m_w_out_b, m_w_out, m_norm_mlp_g, m_w_ff1, m_w_ff2, m_norm_ple_g, m_w_ple_gate, m_w_ple, m_final_norm_g, v_norm_mix_g, v_w_in, v_b_gate, v_q_norm_g, v_k_norm_g, v_rel_bias, v_w_out_a, v_w_out_b, v_w_out, v_norm_mlp_g, v_w_ff1, v_w_ff2, v_norm_ple_g, v_w_ple_gate, v_w_ple, v_final_norm_g):
    s, d = x.shape[1], x.shape[2]
    xs, ps, ts = x[0], p[0, 0], loss_target[0]
    tb = _pick_tile(s, (512, 256))
    tq = _pick_tile(s, (256,))
    tk = _pick_tile(s, (1024, 512))
    cb = _pick_tile(s, (1024, 512))
    fin_g = final_norm_g.reshape(1, d)

    col_sharded = {"w_in": w_in[0], "w_out_b": w_out_b[0], "w_ff1": w_ff1[0], "w_ple": w_ple[0]}
    row_sharded = {"w_out_a": w_out_a[0], "w_out": w_out[0], "w_ff2": w_ff2[0], "w_ple_gate": w_ple_gate[0]}
    order = ["w_in", "w_out_a", "w_out_b", "w_out", "w_ff1", "w_ff2", "w_ple_gate", "w_ple"]
    shards = [(col_sharded[n].T if n in col_sharded else row_sharded[n]).astype(BF16) for n in order]
    my_idx = 4 * lax.axis_index("x") + 2 * lax.axis_index("y") + lax.axis_index("c")
    (w_in_t,) = _all_gather(shards[:1], 1)
    zones = _place_own_rows(shards[1:], my_idx)
    ag = _copies_start("weights_gather_start", shards[1:], zones, w_in_t, True)

    tabs = _rope_tables(s)
    (h1, qraw, kraw, qrot, krot, va, qb, kb, vb, ga, gb) = _in_proj(
        xs, tabs, w_in_t, norm_mix_g, b_gate, q_norm_g, k_norm_g, tb, ag[4])
    oa, lse_a = _attn_a_fwd(qrot, krot, va, tq, tk)
    _, (w_oa, w_ob_t, w_o, w_ff1_t, w_ff2_f, w_pg, w_p_t) = _copies_wait(
        "weights_gather_wait", ag[0], ag[1], ag[2], ag[3], lse_a, True)
    flat = lambda arrs: [a.reshape(s, GB_W) for a in arrs]
    split = lambda arrs: [a.reshape(dil, s // dil, GB_W) for a, dil in zip(arrs, DILATIONS)]
    qb_r, kb_r, vb_r = flat(qb), flat(kb), flat(vb)
    bmaps = [jnp.asarray(_bucket_map(dil)) for dil in DILATIONS]
    bias_tabs = [rel_bias[:, N_HEADS_PER_DIL * g:N_HEADS_PER_DIL * (g + 1)] for g in range(3)]
    band_out = [_band_fwd(dil, qb_r[g], kb_r[g], vb_r[g], bmaps[g], bias_tabs[g], cb)
                for g, dil in enumerate(DILATIONS)]
    og, lg = split([o for o, _ in band_out]), split([l for _, l in band_out])
    x2, ob, lse_b, ya, yb, u = _mix_out(xs, oa, og, lg, ga, gb, w_oa, w_ob_t, w_o, tb)
    tc = _pick_tile(w_ff1_t.shape[0], (512,))
    x3, r_act, h2 = _mlp_fwd(x2, w_ff1_t, w_ff2_f, norm_mlp_g, tb, tc)

    dx3, h3, dpre, dpe, pb, loss_part, dg_fin, dg_ple = _ple_loss(
        x3, ps, ts, w_pg, w_p_t, norm_ple_g, fin_g, tb)
    dx2, df, dg_mlp = _mlp_bwd(dx3, x2, r_act, w_ff1_t, w_ff2_f, norm_mlp_g, tb, tc)

    tkk = _pick_tile(s, (1024, 512))
    tk2 = _pick_tile(s, (2048, 1024, 512))
    dff = w_ff1_t.shape[0]
    t1k = lambda n: _pick_tile(n, (1024, 512, 256))
    slots = lambda parts: [lax.empty((7, a.shape[0] // N_DEV, a.shape[1]), BF16) for a in parts]
    part1 = [_weight_grad("grad_w_ff1", df, h2, t1k(dff), t1k(d), tkk),
             _weight_grad("grad_w_ff2", r_act, dx3, t1k(dff), t1k(d), tkk, a_fn=_square_bf16, b_fn=_to_bf16),
             _weight_grad("grad_w_ple_gate", h3, dpre, t1k(d), t1k(d), tk2),
             _weight_grad("grad_w_ple", dpe, pb, t1k(d), ps.shape[1], tk2)]
    doa, dob, dd, dga, dgb, dya, dyb, dbg = _mix_out_bwd(dx2, ya, yb, ga, gb, ob, w_oa, w_ob_t, w_o, tb, dx2)
    part1 += [_weight_grad("grad_w_out_a", oa, dya, t1k(QA_W), t1k(d), tk2),
              _weight_grad("grad_w_out_b", dyb, ob, t1k(d), GB_W, tk2),
              _weight_grad("grad_w_out", u, dx2, t1k(d), t1k(d), tkk, b_fn=_to_bf16)]
    rs1 = _copies_start("grads1_start", part1, slots(part1), doa, False)
    dqrot, dkrot, dva = _attn_a_bwd(qrot, krot, va, oa, doa, lse_a, tq, tk, rs1[4])
    dob_r, lse_r, dd_r = flat(dob), flat(lse_b), flat(dd)
    band_bwd = [_band_bwd(dil, qb_r[g], kb_r[g], vb_r[g], dob_r[g], lse_r[g], dd_r[g], bmaps[g], bias_tabs[g], cb)
                for g, dil in enumerate(DILATIONS)]
    dqb, dkb, dvb = [split([r[j] for r in band_bwd]) for j in range(3)]
    grad_x, dz, dg_mix, dg_q, dg_k = _in_proj_bwd(
        dx2, xs, dqrot, dkrot, dva, qraw, kraw, tabs, dqb, dkb, dvb, dga, dgb, w_in_t, norm_mix_g,
        q_norm_g, k_norm_g, _pick_tile(s, (256,)))
    d_rel = jnp.concatenate([r[3][:, :N_HEADS_PER_DIL] for r in band_bwd], axis=1)

    din = w_in_t.shape[0]
    ti_in = _pick_tile(din, (din // 2,)) if (din // 2) % LANES == 0 else din
    hd_ = d // 2
    part3 = [_weight_grad("grad_w_in_lo", dz, h1, ti_in, t1k(hd_), tkk, n=hd_)]
    rs3 = _copies_start("grads3_start", part3, slots(part3), grad_x, False)
    part4 = [_weight_grad("grad_w_in_hi", dz, h1, ti_in, t1k(hd_), tkk, col0=hd_, n=hd_, after=rs3[4])]
    rs4 = _copies_start("grads4_start", part4, slots(part4), rs3[4], False)

    def own_rows(a):
        n = a.shape[0] // N_DEV
        return lax.dynamic_slice(a, (my_idx * n, 0), (n, a.shape[1]))

    sums = {}
    src1, got1 = _copies_wait("grads1_wait", rs1[0], rs1[1], rs1[2], rs1[3], rs4[4], False)
    for n, a, r in zip(["w_ff1", "w_ff2", "w_ple_gate", "w_ple", "w_out_a", "w_out_b", "w_out"], src1, got1):
        sums[n] = _sum_slots("sum_" + n, r, own_rows(a))
    given_w = dict(w_in=w_in, w_out_a=w_out_a, w_out_b=w_out_b, w_out=w_out, w_ff1=w_ff1, w_ff2=w_ff2,
                   w_ple_gate=w_ple_gate, w_ple=w_ple)
    given_m = dict(w_in=m_w_in, w_out_a=m_w_out_a, w_out_b=m_w_out_b, w_out=m_w_out, w_ff1=m_w_ff1, w_ff2=m_w_ff2,
                   w_ple_gate=m_w_ple_gate, w_ple=m_w_ple)
    given_v = dict(w_in=v_w_in, w_out_a=v_w_out_a, w_out_b=v_w_out_b, w_out=v_w_out, w_ff1=v_w_ff1, w_ff2=v_w_ff2,
                   w_ple_gate=v_w_ple_gate, w_ple=v_w_ple)
    big = {}

    def update(n):
        g = sums[n].T if n in col_sharded else sums[n]
        delta, new_m, new_v = _adamw("adamw_" + n, given_w[n][0], g, given_m[n][0], given_v[n][0])
        big[n] = tuple(a[None] for a in (g, delta, new_m, new_v))

    for n in order[1:]:
        update(n)

    small_names = ["norm_mix_g", "b_gate", "q_norm_g", "k_norm_g", "rel_bias", "norm_mlp_g", "norm_ple_g",
                   "final_norm_g"]
    small_w = [norm_mix_g, b_gate, q_norm_g, k_norm_g, rel_bias, norm_mlp_g, norm_ple_g, final_norm_g]
    small_m = [m_norm_mix_g, m_b_gate, m_q_norm_g, m_k_norm_g, m_rel_bias, m_norm_mlp_g, m_norm_ple_g,
               m_final_norm_g]
    small_v = [v_norm_mix_g, v_b_gate, v_q_norm_g, v_k_norm_g, v_rel_bias, v_norm_mlp_g, v_norm_ple_g,
               v_final_norm_g]
    small_g = [dg_mix, dbg, dg_q, dg_k, d_rel, dg_mlp, dg_ple, dg_fin]
    sizes = [int(np.prod(w.shape)) for w in small_w]
    n_rows = -(-(sum(-(-sz // LANES) for sz in sizes) + 1) // 8) * 8
    pad = lambda v: jnp.pad(v.reshape(-1).astype(F32), (0, -v.size % LANES))
    pack = lambda vs, last: _pack_rows([pad(v) for v in vs] + [last], n_rows)
    zero_row = jnp.zeros((LANES,), F32)
    parts = _small_all_gather(pack(small_g, loss_part.reshape(-1) * (jnp.arange(LANES) == 0)), big["w_ple"][1])
    g_all, d_all, m_all, v_all = _small_update(parts, pack(small_w, zero_row), pack(small_m, zero_row),
                                               pack(small_v, zero_row))
    small = {}
    row = 0
    for n, w, sz in zip(small_names, small_w, sizes):
        nr = -(-sz // LANES)
        small[n] = tuple(a[row:row + nr].reshape(-1)[:sz].reshape(w.shape) for a in (g_all, d_all, m_all, v_all))
        row += nr
    loss = g_all[row, 0]

    src3, got3 = _copies_wait("grads3_wait", rs3[0], rs3[1], rs3[2], rs3[3], g_all, False)
    src4, got4 = _copies_wait("grads4_wait", rs4[0], rs4[1], rs4[2], rs4[3], g_all, False)
    sums["w_in"] = jnp.concatenate([_sum_slots("sum_w_in_lo", got3[0], own_rows(src3[0])),
                                    _sum_slots("sum_w_in_hi", got4[0], own_rows(src4[0]))], axis=1)
    update("w_in")

    names = ["norm_mix_g", "w_in", "b_gate", "q_norm_g", "k_norm_g", "rel_bias", "w_out_a", "w_out_b", "w_out",
             "norm_mlp_g", "w_ff1", "w_ff2", "norm_ple_g", "w_ple_gate", "w_ple", "final_norm_g"]
    res = {n: (big[n] if n in big else small[n]) for n in names}
    return (loss, grad_x[None], *[res[n][0] for n in names], *[res[n][1] for n in names],
            *[res[n][2] for n in names], *[res[n][3] for n in names])
```

```python
import math

import numpy as np
import jax
import jax.numpy as jnp
from jax import lax
from jax.experimental import pallas as pl
from jax.experimental.pallas import tpu as pltpu

F32 = jnp.float32
BF16 = jnp.bfloat16
MESH = pl.DeviceIdType.MESH

NORM_EPS = 1e-6
NEG_INF = -1e30
LOG2_E = math.log2(math.e)
LN_2 = math.log(2.0)
GRID_W = 64
ROPE_THETA = 10000.0
HEAD_DIM_A = 128
N_Q_HEADS_A = 8
N_KV_HEADS_A = 2
Q_PER_KV = N_Q_HEADS_A // N_KV_HEADS_A
HEAD_DIM_B = 64
N_HEADS_PER_DIL = 4
DILATIONS = (1, 4, 16)
BAND = 64
N_REL_BUCKETS = 32
REL_MAX_DIST = 1024
QA_W = N_Q_HEADS_A * HEAD_DIM_A
KA_W = N_KV_HEADS_A * HEAD_DIM_A
GB_W = N_HEADS_PER_DIL * HEAD_DIM_B
QB_W = GB_W * len(DILATIONS)
OFF_QA, OFF_KA, OFF_VA = 0, QA_W, QA_W + KA_W
OFF_QB = QA_W + 2 * KA_W
OFF_KB = OFF_QB + QB_W
OFF_VB = OFF_KB + QB_W
OFF_GA = OFF_VB + QB_W
N_DEV = 8
LANES = 128
VMEM_LIMIT = 56 * 2 ** 20

ADAM_LR, ADAM_B1, ADAM_B2, ADAM_EPS, ADAM_WD, ADAM_STEP = 0.001, 0.9, 0.999, 1e-08, 0.01, 10


def _cparams(sem):
    return pltpu.CompilerParams(dimension_semantics=sem, vmem_limit_bytes=VMEM_LIMIT)


def _resident(shape):
    nd = len(shape)
    return pl.BlockSpec(shape, lambda *_: (0,) * nd, pipeline_mode=pl.Buffered(1))


def _acc_spec(shape):
    nd = len(shape)
    return pl.BlockSpec(shape, lambda *_: (0,) * nd)


def _rows(tb, c):
    return pl.BlockSpec((tb, c), lambda i: (i, 0))


def _dil_shapes(s, dtype):
    return [jax.ShapeDtypeStruct((dil, s // dil, GB_W), dtype) for dil in DILATIONS]


def _dil_specs(tb):
    return [pl.BlockSpec((dil, tb // dil, GB_W), lambda i: (0, i, 0)) for dil in DILATIONS]


def _to_residues(val, out_ref, scr_ref, dil, dtype):
    if dil == 1:
        out_ref[0] = val.astype(dtype)
        return
    n = val.shape[0] // dil
    scr_ref[0] = val[:, :LANES]
    scr_ref[1] = val[:, LANES:]
    for r in range(dil):
        out_ref[r] = jnp.concatenate([scr_ref[0, pl.ds(r, n, stride=dil), :],
                                      scr_ref[1, pl.ds(r, n, stride=dil), :]], axis=1).astype(dtype)


def _from_residues(in_ref, scr_ref, dil):
    if dil == 1:
        return in_ref[0]
    n = in_ref.shape[1]
    for r in range(dil):
        v = in_ref[r]
        scr_ref[0, pl.ds(r, n, stride=dil), :] = v[:, :LANES]
        scr_ref[1, pl.ds(r, n, stride=dil), :] = v[:, LANES:]
    return jnp.concatenate([scr_ref[0], scr_ref[1]], axis=1)


def _dot_nt(a, b):
    return lax.dot_general(a, b, (((1,), (1,)), ((), ())), preferred_element_type=F32)


def _dot_nn(a, b):
    return lax.dot_general(a, b, (((1,), (0,)), ((), ())), preferred_element_type=F32)


def _dot_tn(a, b):
    return lax.dot_general(a, b, (((0,), (0,)), ((), ())), preferred_element_type=F32)


def _rstd(x):
    return lax.rsqrt(jnp.mean(x * x, axis=-1, keepdims=True) + NORM_EPS)


def _rms_bwd(dy, n, r, g):
    dn = dy * g
    return r * (dn - n * jnp.mean(dn * n, axis=-1, keepdims=True))


def _colsum(v):
    return jnp.sum(v, axis=0, keepdims=True)


def _sigmoid(v):
    return 1.0 / (1.0 + jnp.exp(-v))


def _rope_tables(s):
    half = HEAD_DIM_A // 2
    inv = np.power(np.float32(ROPE_THETA), -np.arange(0, half, 2, dtype=np.float32) / np.float32(half))
    t = np.arange(s)
    ang_r = (t // GRID_W).astype(np.float32)[:, None] * inv[None, :]
    ang_c = (t % GRID_W).astype(np.float32)[:, None] * inv[None, :]
    cr, sr, cc, sc = np.cos(ang_r), np.sin(ang_r), np.cos(ang_c), np.sin(ang_c)
    z = np.zeros_like(sr)
    cos = np.concatenate([cr, cr, cc, cc], axis=1)
    s1 = np.concatenate([z, sr, z, sc], axis=1)
    s2 = np.concatenate([-sr, z, -sc, z], axis=1)
    return [jnp.asarray(a, F32) for a in (cos, s1, s2)]


def _my_place():
    return lax.axis_index("x"), lax.axis_index("y"), lax.axis_index("c")


def _all_gather(shards, n_gather):
    n_all = len(shards)
    nw = n_gather

    def body(*refs):
        ins, outs = refs[:n_all], refs[n_all:2 * n_all]
        send_sems, recv_sems, local_sems = refs[2 * n_all:]
        x, y, c = _my_place()
        me, sibling = (x, y, c), (x, y, 1 - c)
        chips = [(1 - x, y), (x, 1 - y), (1 - x, 1 - y)]

        def rows(w, px, py, pc):
            n = ins[w].shape[0]
            return outs[w].at[pl.ds(pl.multiple_of((4 * px + 2 * py + pc) * n, 16), n), :]

        def copy(w, k, block, to, src=None):
            return pltpu.make_async_remote_copy(
                src_ref=rows(w, *block) if src is None else src, dst_ref=rows(w, *block),
                send_sem=send_sems.at[w, k], recv_sem=recv_sems.at[w, k], device_id=to, device_id_type=MESH)

        mine = [pltpu.make_async_copy(ins[w], rows(w, *me), local_sems.at[w]) for w in range(n_all)]
        for cp in mine:
            cp.start()
        first = []
        for w in range(nw):
            first.append(copy(w, 0, me, sibling, src=ins[w]))
            first += [copy(w, 1 + j, me, (*chip, c), src=ins[w]) for j, chip in enumerate(chips)]
        for cp in first:
            cp.start()
        passed = []
        for j, chip in enumerate(chips):
            for w in range(nw):
                copy(w, 1 + j, (*chip, c), me).wait_recv()
                fwd = copy(w, 4 + j, (*chip, c), sibling)
                fwd.start()
                passed.append(fwd)
        for w in range(nw):
            copy(w, 0, sibling, me).wait_recv()
        for j, chip in enumerate(chips):
            for w in range(nw):
                copy(w, 4 + j, (*chip, 1 - c), me).wait_recv()
        for cp in first + passed:
            cp.wait_send()
        for cp in mine:
            cp.wait()

    any_spec = pl.BlockSpec(memory_space=pl.ANY)
    return pl.pallas_call(
        body, name="weights_all_gather",
        out_shape=[jax.ShapeDtypeStruct((N_DEV * s.shape[0], s.shape[1]), s.dtype) for s in shards],
        in_specs=[any_spec] * n_all, out_specs=[any_spec] * n_all,
        scratch_shapes=[pltpu.SemaphoreType.DMA((nw, 7)), pltpu.SemaphoreType.DMA((nw, 7)),
                        pltpu.SemaphoreType.DMA((n_all,))],
    )(*shards)


def _place_own_rows(shards, my_idx):
    nw = len(shards)

    def body(idx_ref, *refs):
        for w in range(nw):
            refs[nw + w][...] = refs[w][...]

    grid_spec = pltpu.PrefetchScalarGridSpec(
        num_scalar_prefetch=1, grid=(1,),
        in_specs=[pl.BlockSpec(s.shape, lambda i, idx: (0, 0)) for s in shards],
        out_specs=[pl.BlockSpec(s.shape, lambda i, idx: (idx[0], 0)) for s in shards])
    return pl.pallas_call(
        body, name="place_own_rows", grid_spec=grid_spec,
        out_shape=[jax.ShapeDtypeStruct((N_DEV * s.shape[0], s.shape[1]), s.dtype) for s in shards],
        compiler_params=_cparams(("arbitrary",)))(my_idx.reshape(1).astype(jnp.int32), *shards)


_FLIPS = [(fx, fy, fc) for fx in (0, 1) for fy in (0, 1) for fc in (0, 1)][1:]


def _small_all_gather(v, after):
    def body(v_ref, after_ref, out_ref, send_sems, recv_sems):
        x, y, c = _my_place()
        my_idx = 4 * x + 2 * y + c
        out_ref[my_idx] = v_ref[...]
        sends = []
        for k, (fx, fy, fc) in enumerate(_FLIPS):
            to = (1 - x if fx else x, 1 - y if fy else y, 1 - c if fc else c)
            sends.append(pltpu.make_async_remote_copy(
                src_ref=v_ref, dst_ref=out_ref.at[my_idx], send_sem=send_sems.at[k], recv_sem=recv_sems.at[k],
                device_id=to, device_id_type=MESH))
        for cp in sends:
            cp.start()
        for k, (fx, fy, fc) in enumerate(_FLIPS):
            frm_idx = 4 * (1 - x if fx else x) + 2 * (1 - y if fy else y) + (1 - c if fc else c)
            pltpu.make_async_remote_copy(
                src_ref=v_ref, dst_ref=out_ref.at[frm_idx], send_sem=send_sems.at[k], recv_sem=recv_sems.at[k],
                device_id=(x, y, c), device_id_type=MESH).wait_recv()
        for cp in sends:
            cp.wait_send()

    vm = pl.BlockSpec(memory_space=pltpu.VMEM)
    return pl.pallas_call(
        body, name="small_all_gather", out_shape=jax.ShapeDtypeStruct((N_DEV,) + v.shape, v.dtype),
        in_specs=[vm, pl.BlockSpec(memory_space=pl.ANY)], out_specs=vm,
        scratch_shapes=[pltpu.SemaphoreType.DMA((7,)), pltpu.SemaphoreType.DMA((7,))],
    )(v, after)


_HBM = pl.BlockSpec(memory_space=pltpu.HBM)
_SEM = pl.BlockSpec(memory_space=pltpu.SEMAPHORE)
_ANY = pl.BlockSpec(memory_space=pl.ANY)
_SPLIT_COPY = dict(has_side_effects=pltpu.SideEffectType.DATAFLOW_SIDE_EFFECTING)


def _peer(x, y, c, k):
    fx, fy, fc = _FLIPS[k]
    return (1 - x if fx else x, 1 - y if fy else y, 1 - c if fc else c)


def _in_hbm(a):
    return pltpu.with_memory_space_constraint(a, pltpu.HBM)


def _split_copies(srcs, lands, send_sems, recv_sems, gather, arriving):
    x, y, c = _my_place()
    my_idx = 4 * x + 2 * y + c
    out = []
    for k in range(7):
        to = _peer(x, y, c, k)
        to_idx = 4 * to[0] + 2 * to[1] + to[2]
        for w in range(len(srcs)):
            if gather:
                n = srcs[w].shape[0]
                src = srcs[w]
                dst = lands[w].at[pl.ds(pl.multiple_of((to_idx if arriving else my_idx) * n, 16), n), :]
            else:
                n = lands[w].shape[1]
                src = srcs[w].at[pl.ds(pl.multiple_of(to_idx * n, 16), n), :]
                dst = lands[w].at[k]
            out.append(pltpu.make_async_remote_copy(
                src_ref=src, dst_ref=dst, send_sem=send_sems.at[7 * w + k], recv_sem=recv_sems.at[7 * w + k],
                device_id=to, device_id_type=MESH))
    return out


def _copies_start(name, srcs, lands, after, gather):
    nw = len(srcs)

    def body(*refs):
        send_sems, recv_sems = refs[2 * nw + 1], refs[2 * nw + 2]
        for cp in _split_copies(refs[:nw], refs[nw:2 * nw], send_sems, recv_sems, gather, False):
            cp.start()
        refs[-1][...] = jnp.zeros_like(refs[-1])

    sems = pltpu.SemaphoreType.DMA((7 * nw,))
    thru = [pltpu.HBM(a.shape, a.dtype) for a in list(srcs) + list(lands)]
    res = pl.pallas_call(
        body, name=name, out_shape=(sems, sems, *thru, jax.ShapeDtypeStruct((8, LANES), F32)),
        in_specs=[_HBM] * (2 * nw) + [_ANY], out_specs=(_SEM, _SEM, *[_HBM] * (2 * nw), pl.BlockSpec(memory_space=pltpu.VMEM)),
        input_output_aliases={i: 2 + i for i in range(2 * nw)},
        compiler_params=pltpu.CompilerParams(**_SPLIT_COPY),
    )(*[_in_hbm(a) for a in srcs], *[_in_hbm(a) for a in lands], after)
    return res[0], res[1], list(res[2:2 + nw]), list(res[2 + nw:2 + 2 * nw]), res[-1]


def _copies_wait(name, send_sems, recv_sems, srcs, lands, after, gather):
    nw = len(srcs)

    def body(*refs):
        for cp in _split_copies(refs[:nw], refs[nw:2 * nw], refs[2 * nw], refs[2 * nw + 1], gather, False):
            cp.wait_send()
        for cp in _split_copies(refs[:nw], refs[nw:2 * nw], refs[2 * nw], refs[2 * nw + 1], gather, True):
            cp.wait_recv()

    thru = [pltpu.HBM(a.shape, a.dtype) for a in list(srcs) + list(lands)]
    res = pl.pallas_call(
        body, name=name, out_shape=tuple(thru),
        in_specs=[_HBM] * (2 * nw) + [_SEM, _SEM, _ANY], out_specs=tuple([_HBM] * (2 * nw)),
        input_output_aliases={i: i for i in range(2 * nw)},
        compiler_params=pltpu.CompilerParams(**_SPLIT_COPY),
    )(*srcs, *lands, send_sems, recv_sems, after)
    return list(res[:nw]), list(res[nw:])


def _in_proj(x, tabs, w_in_t, g_mix, b_gate, q_g, k_g, tb, after):
    s, d = x.shape
    n_gate_chunks = d // 256
    q_scale = HEAD_DIM_A ** -0.5 * LOG2_E
    b_scale = HEAD_DIM_B ** -0.5 * LOG2_E

    def body(x_ref, c_ref, s1_ref, s2_ref, w_ref, gmix_ref, bg_ref, qg_ref, kg_ref, after_ref,
             h1_ref, qraw_ref, kraw_ref, qrot_ref, krot_ref, va_ref, *rest):
        qb_refs, kb_refs, vb_refs = rest[0:3], rest[3:6], rest[6:9]
        ga_ref, gb_ref, scr_ref = rest[9:]
        xv = x_ref[...]
        hb = (xv * _rstd(xv) * gmix_ref[...]).astype(BF16)
        h1_ref[...] = hb
        cos, s1, s2 = c_ref[...], s1_ref[...], s2_ref[...]

        def proj(lo, width):
            return _dot_nt(hb, w_ref[lo:lo + width, :])

        def norm_rope(z, g):
            n = z * _rstd(z) * g
            return n * cos + pltpu.roll(n, 32, 1) * s1 + pltpu.roll(n, 96, 1) * s2

        for j in range(QA_W // 256):
            z = proj(OFF_QA + 256 * j, 256)
            qraw_ref[:, 256 * j:256 * j + 256] = z
            for hh in range(2):
                lo = 256 * j + 128 * hh
                qrot_ref[:, lo:lo + 128] = (norm_rope(z[:, 128 * hh:128 * hh + 128], qg_ref[...]) * q_scale).astype(BF16)
        z = proj(OFF_KA, 256)
        kraw_ref[...] = z
        for hh in range(2):
            krot_ref[:, 128 * hh:128 * hh + 128] = norm_rope(z[:, 128 * hh:128 * hh + 128], kg_ref[...]).astype(BF16)
        va_ref[...] = proj(OFF_VA, 256).astype(BF16)
        for g, dil in enumerate(DILATIONS):
            _to_residues(proj(OFF_QB + GB_W * g, GB_W) * b_scale, qb_refs[g], scr_ref, dil, BF16)
            _to_residues(proj(OFF_KB + GB_W * g, GB_W), kb_refs[g], scr_ref, dil, BF16)
            _to_residues(proj(OFF_VB + GB_W * g, GB_W), vb_refs[g], scr_ref, dil, BF16)
        for j in range(n_gate_chunks):
            sl = slice(256 * j, 256 * j + 256)
            ga_ref[:, sl] = _sigmoid(proj(OFF_GA + 256 * j, 256) + bg_ref[:, sl]).astype(BF16)
            gb_ref[:, sl] = _sigmoid(
                proj(OFF_GA + d + 256 * j, 256) + bg_ref[:, d + 256 * j:d + 256 * j + 256]).astype(BF16)

    sd = jax.ShapeDtypeStruct
    outs = [sd((s, d), BF16), sd((s, QA_W), F32), sd((s, KA_W), F32), sd((s, QA_W), BF16), sd((s, KA_W), BF16),
            sd((s, KA_W), BF16)] + _dil_shapes(s, BF16) * 3 + [sd((s, d), BF16), sd((s, d), BF16)]
    out_specs = [_rows(tb, d), _rows(tb, QA_W), _rows(tb, KA_W), _rows(tb, QA_W), _rows(tb, KA_W), _rows(tb, KA_W)
                 ] + _dil_specs(tb) * 3 + [_rows(tb, d), _rows(tb, d)]
    in_specs = [_rows(tb, d), _rows(tb, LANES), _rows(tb, LANES), _rows(tb, LANES), _resident(w_in_t.shape),
                _resident(g_mix.shape), _resident(b_gate.shape), _resident(q_g.shape), _resident(k_g.shape), _ANY]
    res = list(pl.pallas_call(body, name="in_proj", grid=(s // tb,), in_specs=in_specs, out_specs=out_specs,
                              out_shape=outs, scratch_shapes=[pltpu.VMEM((2, tb, LANES), F32)],
                              compiler_params=_cparams(("arbitrary",)))(
        x, *tabs, w_in_t, g_mix, b_gate, q_g, k_g, after))
    return res[:6] + [res[6:9], res[9:12], res[12:15]] + res[15:]


def _attn_a_fwd(qrot, krot, va, tq, tk):
    s = qrot.shape[0]
    n_kv = s // tk
    gw = Q_PER_KV * HEAD_DIM_A

    def body(q_ref, k_ref, v_ref, o_ref, lse_ref):
        q4 = jnp.concatenate([q_ref[:, 128 * h:128 * h + 128] for h in range(Q_PER_KV)], axis=0)

        def step(j, carry):
            m, l, acc = carry
            sl = pl.ds(pl.multiple_of(j * tk, tk), tk)
            kj, vj = k_ref[sl, :], v_ref[sl, :]
            sc = _dot_nt(kj, q4)
            m_new = jnp.maximum(m, jnp.max(sc, axis=0, keepdims=True))
            p = jnp.exp2(sc - m_new)
            alpha = jnp.exp2(m - m_new)
            l = alpha * l + jnp.sum(p, axis=0, keepdims=True)
            acc = alpha * acc + _dot_tn(vj, p.astype(BF16))
            return m_new, l, acc

        rows = Q_PER_KV * tq
        m, l, acc = lax.fori_loop(0, n_kv, step, (jnp.full((1, rows), NEG_INF, F32), jnp.zeros((1, rows), F32),
                                                  jnp.zeros((HEAD_DIM_A, rows), F32)))
        o = (acc / l).T
        lse = m + jnp.log2(l)
        for h in range(Q_PER_KV):
            o_ref[:, 128 * h:128 * h + 128] = o[h * tq:(h + 1) * tq].astype(BF16)
            lse_ref[0, h:h + 1, :] = lse[:, h * tq:(h + 1) * tq]

    return pl.pallas_call(
        body, name="attn_a_fwd", grid=(N_KV_HEADS_A, s // tq),
        in_specs=[pl.BlockSpec((tq, gw), lambda g, i: (i, g)),
                  pl.BlockSpec((s, HEAD_DIM_A), lambda g, i: (0, g)),
                  pl.BlockSpec((s, HEAD_DIM_A), lambda g, i: (0, g))],
        out_specs=[pl.BlockSpec((tq, gw), lambda g, i: (i, g)),
                   pl.BlockSpec((1, Q_PER_KV, tq), lambda g, i: (g, 0, i))],
        out_shape=[jax.ShapeDtypeStruct((s, QA_W), BF16), jax.ShapeDtypeStruct((N_KV_HEADS_A, Q_PER_KV, s), F32)],
        compiler_params=_cparams(("arbitrary", "arbitrary")))(qrot, krot, va)


def _attn_a_bwd(qrot, krot, va, oa, doa, lse, tq, tk, after):
    s = qrot.shape[0]
    n_kv = s // tk
    gw = Q_PER_KV * HEAD_DIM_A

    def body(q_ref, do_ref, o_ref, lse_ref, k_ref, v_ref, after_ref, dq_ref, dk_ref, dv_ref):
        @pl.when(pl.program_id(1) == 0)
        def _():
            dk_ref[...] = jnp.zeros_like(dk_ref)
            dv_ref[...] = jnp.zeros_like(dv_ref)

        def stack(ref):
            return jnp.concatenate([ref[:, 128 * h:128 * h + 128] for h in range(Q_PER_KV)], axis=0)

        q4, do4, o4 = stack(q_ref), stack(do_ref), stack(o_ref)
        delta = jnp.sum(do4.astype(F32) * o4.astype(F32), axis=-1, keepdims=True)
        lse_cols = jnp.concatenate([lse_ref[0], jnp.zeros_like(lse_ref[0])], axis=0).T
        lse4 = jnp.concatenate([lse_cols[:, h:h + 1] for h in range(Q_PER_KV)], axis=0)

        def step(j, dq):
            sl = pl.ds(pl.multiple_of(j * tk, tk), tk)
            kj, vj = k_ref[sl, :], v_ref[sl, :]
            p = jnp.exp2(_dot_nt(q4, kj) - lse4)
            ds = (p * (_dot_nt(do4, vj) - delta)).astype(BF16)
            dk_ref[sl, :] += _dot_tn(ds, q4)
            dv_ref[sl, :] += _dot_tn(p.astype(BF16), do4)
            return dq + _dot_nn(ds, kj)

        dq = lax.fori_loop(0, n_kv, step, jnp.zeros((Q_PER_KV * tq, HEAD_DIM_A), F32))
        for h in range(Q_PER_KV):
            dq_ref[:, 128 * h:128 * h + 128] = dq[h * tq:(h + 1) * tq]

    qspec = pl.BlockSpec((tq, gw), lambda g, i: (i, g))
    kspec = pl.BlockSpec((s, HEAD_DIM_A), lambda g, i: (0, g))
    return pl.pallas_call(
        body, name="attn_a_bwd", grid=(N_KV_HEADS_A, s // tq),
        in_specs=[qspec, qspec, qspec, pl.BlockSpec((1, Q_PER_KV, tq), lambda g, i: (g, 0, i)), kspec, kspec, _ANY],
        out_specs=[qspec, kspec, kspec],
        out_shape=[jax.ShapeDtypeStruct((s, QA_W), F32), jax.ShapeDtypeStruct((s, KA_W), F32),
                   jax.ShapeDtypeStruct((s, KA_W), F32)],
        compiler_params=_cparams(("arbitrary", "arbitrary")))(qrot, doa, oa, lse, krot, va, after)


BAND_QB = 128
BAND_WIN = BAND_QB + 2 * BAND


def _band_specs(s, cb):
    per = cb // BAND
    last = s // BAND - 1
    cur = pl.BlockSpec((cb, GB_W), lambda i: (i, 0))
    prev = pl.BlockSpec((BAND, GB_W), lambda i: (jnp.maximum(i * per - 1, 0), 0))
    nxt = pl.BlockSpec((BAND, GB_W), lambda i: (jnp.minimum(i * per + per, last), 0))
    return cur, prev, nxt


def _window(prev_ref, cur_ref, next_ref):
    return jnp.concatenate([prev_ref[...], cur_ref[...], next_ref[...]], axis=0)


def _band_mask(base, seg_shift):
    rq = base + lax.broadcasted_iota(jnp.int32, (BAND_QB, BAND_WIN), 0)
    rk = base - BAND + lax.broadcasted_iota(jnp.int32, (BAND_QB, BAND_WIN), 1)
    same_segment = lax.shift_right_arithmetic(rq, jnp.int32(seg_shift)) == lax.shift_right_arithmetic(rk, jnp.int32(seg_shift))
    return (jnp.abs(rk - rq) <= BAND) & same_segment


def _build_bias(bmap_ref, tab_ref, bias_ref):
    bm = bmap_ref[...]
    acc = [jnp.full(bm.shape, NEG_INF, F32) for _ in range(N_HEADS_PER_DIL)]
    for b in range(N_REL_BUCKETS):
        hit = bm == b
        for h in range(N_HEADS_PER_DIL):
            acc[h] = jnp.where(hit, tab_ref[b, h] * LOG2_E, acc[h])
    rows = bm.shape[0]
    for h in range(N_HEADS_PER_DIL):
        bias_ref[h * rows:(h + 1) * rows, :] = acc[h]


def _segment_mask(base, seg_len, seg_shift):
    if seg_len % BAND_QB:
        return _band_mask(base, seg_shift)
    pos = lax.rem(base, seg_len)
    w = lax.broadcasted_iota(jnp.int32, (1, BAND_WIN), 1)
    return ((w >= BAND) | (pos != 0)) & ((w < BAND + BAND_QB) | (pos != seg_len - BAND_QB))


def _head_lane_masks():
    lane = lax.broadcasted_iota(jnp.int32, (1, LANES), 1)
    return [lane < HEAD_DIM_B, lane >= HEAD_DIM_B]


def _rows4(mask):
    return mask if mask.shape[0] == 1 else jnp.concatenate([mask] * N_HEADS_PER_DIL, axis=0)


def _head_scores(a, b):
    hm = _head_lane_masks()
    out = []
    for hp in range(2):
        ls = slice(LANES * hp, LANES * hp + LANES)
        ah = a[:, ls]
        both = jnp.concatenate([jnp.where(hm[0], ah, jnp.zeros_like(ah)), jnp.where(hm[1], ah, jnp.zeros_like(ah))],
                               axis=0)
        out.append(_dot_nt(both, b[:, ls]))
    return jnp.concatenate(out, axis=0)


def _head_combine(p, v, scale=None, transposed=False):
    hm = _head_lane_masks()
    rows = p.shape[0] // N_HEADS_PER_DIL
    halves = []
    for hp in range(2):
        vh = v[:, LANES * hp:LANES * hp + LANES]
        acc = None
        for hh in range(2):
            h = 2 * hp + hh
            ph = p[h * rows:(h + 1) * rows]
            vm = jnp.where(hm[hh], vh, jnp.zeros_like(vh))
            t = _dot_tn(ph, vm) if transposed else _dot_nn(ph, vm)
            if scale is not None:
                t = t * scale[h * rows:(h + 1) * rows]
            acc = t if acc is None else acc + t
        halves.append(acc)
    return jnp.concatenate(halves, axis=1)


def _head_spread(col):
    rows = col.shape[0] // N_HEADS_PER_DIL
    lane = lax.broadcasted_iota(jnp.int32, (1, GB_W), 1)
    out = jnp.zeros((rows, GB_W), F32)
    for h in range(N_HEADS_PER_DIL):
        out = jnp.where((lane >= HEAD_DIM_B * h) & (lane < HEAD_DIM_B * (h + 1)), col[h * rows:(h + 1) * rows], out)
    return out


def _head_cols(v):
    return jnp.concatenate([v[:, HEAD_DIM_B * h:HEAD_DIM_B * h + 1] for h in range(N_HEADS_PER_DIL)], axis=0)


def _seg_shift(s, dil):
    seg = s // dil
    assert seg & (seg - 1) == 0, "segment length must be a power of two"
    return seg.bit_length() - 1


def _band_fwd(dil, qb, kb, vb, bmap, tab, cb):
    s = qb.shape[0]
    shift = _seg_shift(s, dil)

    def body(q_ref, kp_ref, kc_ref, kn_ref, vp_ref, vc_ref, vn_ref, bmap_ref, tab_ref, o_ref, lse_ref, bias_ref):
        @pl.when(pl.program_id(0) == 0)
        def _():
            _build_bias(bmap_ref, tab_ref, bias_ref)

        kw, vw = _window(kp_ref, kc_ref, kn_ref), _window(vp_ref, vc_ref, vn_ref)
        for jj in range(cb // BAND_QB):
            r0 = BAND_QB * jj
            mask = _rows4(_segment_mask(pl.program_id(0) * cb + r0, s // dil, shift))
            sc = _head_scores(q_ref[r0:r0 + BAND_QB, :], kw[r0:r0 + BAND_WIN, :]) + bias_ref[...]
            sc = jnp.where(mask, sc, NEG_INF)
            m = jnp.max(sc, axis=-1, keepdims=True)
            e = jnp.exp2(sc - m)
            l = jnp.sum(e, axis=-1, keepdims=True)
            o = _head_combine(e.astype(BF16), vw[r0:r0 + BAND_WIN, :], 1.0 / l)
            o_ref[r0:r0 + BAND_QB, :] = o
            lse_ref[r0:r0 + BAND_QB, :] = _head_spread(m + jnp.log2(l))

    cur, prev, nxt = _band_specs(s, cb)
    return pl.pallas_call(
        body, name=f"band_fwd_d{dil}", grid=(s // cb,),
        in_specs=[cur, prev, cur, nxt, prev, cur, nxt, _resident(bmap.shape), pl.BlockSpec(memory_space=pltpu.SMEM)],
        out_specs=[cur, cur],
        out_shape=[jax.ShapeDtypeStruct(qb.shape, F32), jax.ShapeDtypeStruct(qb.shape, F32)],
        scratch_shapes=[pltpu.VMEM((N_HEADS_PER_DIL * BAND_QB, BAND_WIN), F32)],
        compiler_params=_cparams(("arbitrary",)))(qb, kb, kb, kb, vb, vb, vb, bmap, tab)


def _band_bwd(dil, qb, kb, vb, dob, lse, dd, bmap, tab, cb):
    s = qb.shape[0]
    shift = _seg_shift(s, dil)
    n_steps = s // cb

    def body(q_ref, do_ref, lse_ref, dd_ref, kp_ref, kc_ref, kn_ref, vp_ref, vc_ref, vn_ref, bmap_ref, tab_ref,
             dq_ref, dk_ref, dv_ref, dtab_ref, bias_ref, dsum_ref):
        @pl.when(pl.program_id(0) == 0)
        def _():
            _build_bias(bmap_ref, tab_ref, bias_ref)
            dsum_ref[...] = jnp.zeros_like(dsum_ref)
            dk_ref[...] = jnp.zeros_like(dk_ref)
            dv_ref[...] = jnp.zeros_like(dv_ref)

        kw, vw = _window(kp_ref, kc_ref, kn_ref), _window(vp_ref, vc_ref, vn_ref)
        for jj in range(cb // BAND_QB):
            r0 = BAND_QB * jj
            base = pl.program_id(0) * cb + r0
            mask = _rows4(_segment_mask(base, s // dil, shift))
            qh, doh = q_ref[r0:r0 + BAND_QB, :], do_ref[r0:r0 + BAND_QB, :]
            k3, v3 = kw[r0:r0 + BAND_WIN, :], vw[r0:r0 + BAND_WIN, :]
            sc = _head_scores(qh, k3) + bias_ref[...]
            sc = jnp.where(mask, sc, NEG_INF)
            p = jnp.exp2(sc - _head_cols(lse_ref[r0:r0 + BAND_QB, :]))
            dp = _head_scores(doh, v3)
            ds = p * (dp - _head_cols(dd_ref[r0:r0 + BAND_QB, :]))
            dsum_ref[...] += ds
            dsb = ds.astype(BF16)
            dq_ref[r0:r0 + BAND_QB, :] = _head_combine(dsb, k3)
            dk_win = _head_combine(dsb, qh, transposed=True)
            dv_win = _head_combine(p.astype(BF16), doh, transposed=True)
            own = pl.ds(pl.multiple_of(base, BAND), BAND_QB)
            dk_ref[own, :] += dk_win[BAND:BAND + BAND_QB]
            dv_ref[own, :] += dv_win[BAND:BAND + BAND_QB]

            @pl.when(base > 0)
            def _():
                before = pl.ds(pl.multiple_of(base - BAND, BAND), BAND)
                dk_ref[before, :] += dk_win[:BAND]
                dv_ref[before, :] += dv_win[:BAND]

            @pl.when(base + BAND_QB < s)
            def _():
                after = pl.ds(pl.multiple_of(base + BAND_QB, BAND), BAND)
                dk_ref[after, :] += dk_win[BAND + BAND_QB:]
                dv_ref[after, :] += dv_win[BAND + BAND_QB:]

        @pl.when(pl.program_id(0) == n_steps - 1)
        def _():
            bm = bmap_ref[...]
            lane = lax.broadcasted_iota(jnp.int32, (1, LANES), 1)
            for b in range(N_REL_BUCKETS):
                hit = bm == b
                row = jnp.zeros((1, LANES), F32)
                for h in range(N_HEADS_PER_DIL):
                    part = dsum_ref[h * BAND_QB:(h + 1) * BAND_QB, :]
                    row = jnp.where(lane == h, jnp.sum(jnp.where(hit, part, 0.0)), row)
                dtab_ref[b:b + 1, :] = row

    cur, prev, nxt = _band_specs(s, cb)
    whole = _acc_spec(qb.shape)
    return pl.pallas_call(
        body, name=f"band_bwd_d{dil}", grid=(n_steps,),
        in_specs=[cur, cur, cur, cur, prev, cur, nxt, prev, cur, nxt, _resident(bmap.shape),
                  pl.BlockSpec(memory_space=pltpu.SMEM)],
        out_specs=[cur, whole, whole, _acc_spec((N_REL_BUCKETS, LANES))],
        out_shape=[jax.ShapeDtypeStruct(qb.shape, F32)] * 3 + [jax.ShapeDtypeStruct((N_REL_BUCKETS, LANES), F32)],
        scratch_shapes=[pltpu.VMEM((N_HEADS_PER_DIL * BAND_QB, BAND_WIN), F32),
                        pltpu.VMEM((N_HEADS_PER_DIL * BAND_QB, BAND_WIN), F32)],
        compiler_params=_cparams(("arbitrary",)))(qb, dob, lse, dd, kb, kb, kb, vb, vb, vb, bmap, tab)


def _t5_bucket(rel):
    nb = N_REL_BUCKETS // 2
    ret = (rel > 0).astype(np.int32) * nb
    n = np.abs(rel)
    max_exact = nb // 2
    large = max_exact + (np.log(np.maximum(n, 1) / max_exact) / math.log(REL_MAX_DIST / max_exact)
                         * (nb - max_exact)).astype(np.int32)
    large = np.minimum(large, nb - 1)
    return ret + np.where(n < max_exact, n, large).astype(np.int32)


def _bucket_map(dil):
    off = np.arange(BAND_WIN)[None, :] - BAND - np.arange(BAND_QB)[:, None]
    return np.where(np.abs(off) <= BAND, _t5_bucket(off * dil), -1).astype(np.int32)


def _seg_sum(v):
    lane = lax.broadcasted_iota(jnp.int32, (1, v.shape[1]), 1)
    out = jnp.zeros_like(v)
    for h in range(v.shape[1] // HEAD_DIM_B):
        m = (lane >= HEAD_DIM_B * h) & (lane < HEAD_DIM_B * (h + 1))
        out = jnp.where(m, jnp.sum(jnp.where(m, v, 0.0), axis=-1, keepdims=True), out)
    return out


def _mix_out(x, oa, og, lg, ga, gb, w_oa, w_ob_t, w_o, tb):
    s, d = x.shape

    def body(x_ref, oa_ref, og0_ref, og1_ref, og2_ref, lg0_ref, lg1_ref, lg2_ref, ga_ref, gb_ref,
             woa_ref, wob_ref, wo_ref, x2_ref, ob_ref, lse0_ref, lse1_ref, lse2_ref, ya_ref, yb_ref, u_ref, scr_ref):
        og_refs, lg_refs = (og0_ref, og1_ref, og2_ref), (lg0_ref, lg1_ref, lg2_ref)
        l0, l1, l2 = [_from_residues(lg_refs[g], scr_ref, dil) for g, dil in enumerate(DILATIONS)]
        lmax = jnp.maximum(jnp.maximum(l0, l1), l2)
        w0, w1, w2 = jnp.exp2(l0 - lmax), jnp.exp2(l1 - lmax), jnp.exp2(l2 - lmax)
        den = w0 + w1 + w2
        o0, o1, o2 = [_from_residues(og_refs[g], scr_ref, dil) for g, dil in enumerate(DILATIONS)]
        ob = ((w0 * o0 + w1 * o1 + w2 * o2) / den).astype(BF16)
        ob_ref[...] = ob
        lse = lmax + jnp.log2(den)
        for g, (dil, ref) in enumerate(zip(DILATIONS, (lse0_ref, lse1_ref, lse2_ref))):
            _to_residues(lse, ref, scr_ref, dil, F32)
        ya = _dot_nn(oa_ref[...], woa_ref[...])
        yb = _dot_nt(ob, wob_ref[...])
        ya_ref[...] = ya.astype(BF16)
        yb_ref[...] = yb.astype(BF16)
        u = (ga_ref[...].astype(F32) * ya + gb_ref[...].astype(F32) * yb).astype(BF16)
        u_ref[...] = u
        x2_ref[...] = x_ref[...] + _dot_nn(u, wo_ref[...])

    sd = jax.ShapeDtypeStruct
    res = list(pl.pallas_call(
        body, name="mix_out", grid=(s // tb,),
        in_specs=[_rows(tb, d), _rows(tb, QA_W)] + _dil_specs(tb) * 2 + [
            _rows(tb, d), _rows(tb, d), _resident(w_oa.shape), _resident(w_ob_t.shape), _resident(w_o.shape)],
        out_specs=[_rows(tb, d), _rows(tb, GB_W)] + _dil_specs(tb) + [_rows(tb, d), _rows(tb, d), _rows(tb, d)],
        out_shape=[sd((s, d), F32), sd((s, GB_W), BF16)] + _dil_shapes(s, F32) + [
            sd((s, d), BF16), sd((s, d), BF16), sd((s, d), BF16)],
        scratch_shapes=[pltpu.VMEM((2, tb, LANES), F32)],
        compiler_params=_cparams(("arbitrary",)))(x, oa, *og, *lg, ga, gb, w_oa, w_ob_t, w_o))
    return res[:2] + [res[2:5]] + res[5:]


def _mlp_fwd(x2, w1_t, w2, g_mlp, tb, tc):
    s, d = x2.shape
    dff = w1_t.shape[0]

    def body(x_ref, w1_ref, w2_ref, g_ref, x3_ref, r_ref, h_ref):
        xv = x_ref[...]
        hb = (xv * _rstd(xv) * g_ref[...]).astype(BF16)
        h_ref[...] = hb
        x3_ref[...] = xv
        for c in range(dff // tc):
            sl = slice(tc * c, tc * c + tc)
            r = jnp.maximum(_dot_nt(hb, w1_ref[sl, :]), 0.0)
            r_ref[:, sl] = r.astype(BF16)
            x3_ref[...] += _dot_nn((r * r).astype(BF16), w2_ref[sl, :])

    sd = jax.ShapeDtypeStruct
    return pl.pallas_call(
        body, name="mlp_fwd", grid=(s // tb,),
        in_specs=[_rows(tb, d), _resident(w1_t.shape), _resident(w2.shape), _resident(g_mlp.shape)],
        out_specs=[_rows(tb, d), _rows(tb, dff), _rows(tb, d)],
        out_shape=[sd((s, d), F32), sd((s, dff), BF16), sd((s, d), BF16)],
        compiler_params=_cparams(("arbitrary",)))(x2, w1_t, w2, g_mlp)


def _ple_loss(x3, p, target, w_pg, w_p_t, g_ple, g_fin, tb):
    s, d = x3.shape
    dp = p.shape[1]

    def body(x_ref, p_ref, t_ref, wpg_ref, wp_ref, gple_ref, gfin_ref,
             dx3_ref, h3_ref, dpre_ref, dpe_ref, pb_ref, loss_ref, dgfin_ref, dgple_ref):
        @pl.when(pl.program_id(0) == 0)
        def _():
            loss_ref[...] = jnp.zeros_like(loss_ref)
            dgfin_ref[...] = jnp.zeros_like(dgfin_ref)
            dgple_ref[...] = jnp.zeros_like(dgple_ref)

        x3v = x_ref[...]
        r3 = _rstd(x3v)
        n3 = x3v * r3
        h3 = (n3 * gple_ref[...]).astype(BF16)
        h3_ref[...] = h3
        gp = _sigmoid(_dot_nn(h3, wpg_ref[...]))
        pb = p_ref[...].astype(BF16)
        pb_ref[...] = pb
        pe = _dot_nt(pb, wp_ref[...])
        x4 = x3v + gp * pe
        r4 = _rstd(x4)
        n4 = x4 * r4
        err = n4 * gfin_ref[...] - t_ref[...]
        loss_ref[...] += jnp.sum(0.5 * jnp.mean(err * err, axis=-1, keepdims=True), axis=0, keepdims=True)
        dy = err * (1.0 / d)
        dgfin_ref[...] += _colsum(dy * n4)
        dx4 = _rms_bwd(dy, n4, r4, gfin_ref[...])
        dpe_ref[...] = (dx4 * gp).astype(BF16)
        dpre = (dx4 * pe * gp * (1.0 - gp)).astype(BF16)
        dpre_ref[...] = dpre
        dh3 = _dot_nt(dpre, wpg_ref[...])
        dgple_ref[...] += _colsum(dh3 * n3)
        dx3_ref[...] = dx4 + _rms_bwd(dh3, n3, r3, gple_ref[...])

    sd = jax.ShapeDtypeStruct
    return pl.pallas_call(
        body, name="ple_loss", grid=(s // tb,),
        in_specs=[_rows(tb, d), _rows(tb, dp), _rows(tb, d), _resident(w_pg.shape), _resident(w_p_t.shape),
                  _resident(g_ple.shape), _resident(g_fin.shape)],
        out_specs=[_rows(tb, d), _rows(tb, d), _rows(tb, d), _rows(tb, d), _rows(tb, dp),
                   _acc_spec((1, LANES)), _acc_spec((1, d)), _acc_spec((1, d))],
        out_shape=[sd((s, d), F32), sd((s, d), BF16), sd((s, d), BF16), sd((s, d), BF16), sd((s, dp), BF16),
                   sd((1, LANES), F32), sd((1, d), F32), sd((1, d), F32)],
        compiler_params=_cparams(("arbitrary",)))(x3, p, target, w_pg, w_p_t, g_ple, g_fin)


def _mlp_bwd(dx3, x2, r, w1_t, w2, g_mlp, tb, tc):
    s, d = x2.shape
    dff = w1_t.shape[0]

    def body(dx3_ref, x_ref, r_ref, w1_ref, w2_ref, g_ref, dx2_ref, df_ref, dg_ref, dh_ref):
        @pl.when(pl.program_id(0) == 0)
        def _():
            dg_ref[...] = jnp.zeros_like(dg_ref)

        dx3v = dx3_ref[...]
        dx3b = dx3v.astype(BF16)
        dh_ref[...] = jnp.zeros_like(dh_ref)
        for c in range(dff // tc):
            sl = slice(tc * c, tc * c + tc)
            df = (_dot_nt(dx3b, w2_ref[sl, :]) * (2.0 * r_ref[:, sl].astype(F32))).astype(BF16)
            df_ref[:, sl] = df
            dh_ref[...] += _dot_nn(df, w1_ref[sl, :])
        xv = x_ref[...]
        r2 = _rstd(xv)
        n2 = xv * r2
        dh = dh_ref[...]
        dg_ref[...] += _colsum(dh * n2)
        dx2_ref[...] = dx3v + _rms_bwd(dh, n2, r2, g_ref[...])

    sd = jax.ShapeDtypeStruct
    return pl.pallas_call(
        body, name="mlp_bwd", grid=(s // tb,),
        in_specs=[_rows(tb, d), _rows(tb, d), _rows(tb, dff), _resident(w1_t.shape), _resident(w2.shape),
                  _resident(g_mlp.shape)],
        out_specs=[_rows(tb, d), _rows(tb, dff), _acc_spec((1, d))],
        out_shape=[sd((s, d), F32), sd((s, dff), BF16), sd((1, d), F32)],
        scratch_shapes=[pltpu.VMEM((tb, d), F32)],
        compiler_params=_cparams(("arbitrary",)))(dx3, x2, r, w1_t, w2, g_mlp)


def _mix_out_bwd(dx2, ya, yb, ga, gb, ob, w_oa, w_ob_t, w_o, tb, after):
    s, d = dx2.shape

    def body(dx_ref, ya_ref, yb_ref, ga_ref, gb_ref, ob_ref, woa_ref, wob_ref, wo_ref, after_ref,
             doa_ref, dob0_ref, dob1_ref, dob2_ref, dd0_ref, dd1_ref, dd2_ref, dga_ref, dgb_ref, dya_ref, dyb_ref,
             dbg_ref, scr_ref):
        @pl.when(pl.program_id(0) == 0)
        def _():
            dbg_ref[...] = jnp.zeros_like(dbg_ref)

        du = _dot_nt(dx_ref[...].astype(BF16), wo_ref[...])
        gav, gbv = ga_ref[...].astype(F32), gb_ref[...].astype(F32)
        dya = (du * gav).astype(BF16)
        dyb = (du * gbv).astype(BF16)
        dya_ref[...] = dya
        dyb_ref[...] = dyb
        dga = du * ya_ref[...].astype(F32) * gav * (1.0 - gav)
        dgb = du * yb_ref[...].astype(F32) * gbv * (1.0 - gbv)
        dga_ref[...] = dga.astype(BF16)
        dgb_ref[...] = dgb.astype(BF16)
        dbg_ref[:, 0:d] += _colsum(dga)
        dbg_ref[:, d:2 * d] += _colsum(dgb)
        doa_ref[...] = _dot_nt(dya, woa_ref[...]).astype(BF16)
        dob = _dot_nn(dyb, wob_ref[...])
        dd = _seg_sum(dob * ob_ref[...].astype(F32))
        for dil, dob_ref, dd_ref in zip(DILATIONS, (dob0_ref, dob1_ref, dob2_ref), (dd0_ref, dd1_ref, dd2_ref)):
            _to_residues(dob, dob_ref, scr_ref, dil, BF16)
            _to_residues(dd, dd_ref, scr_ref, dil, F32)

    sd = jax.ShapeDtypeStruct
    res = list(pl.pallas_call(
        body, name="mix_out_bwd", grid=(s // tb,),
        in_specs=[_rows(tb, d)] * 5 + [_rows(tb, GB_W), _resident(w_oa.shape), _resident(w_ob_t.shape),
                                       _resident(w_o.shape), _ANY],
        out_specs=[_rows(tb, QA_W)] + _dil_specs(tb) * 2 + [_rows(tb, d), _rows(tb, d), _rows(tb, d),
                                                           _rows(tb, d), _acc_spec((1, 2 * d))],
        out_shape=[sd((s, QA_W), BF16)] + _dil_shapes(s, BF16) + _dil_shapes(s, F32) + [
            sd((s, d), BF16), sd((s, d), BF16), sd((s, d), BF16), sd((s, d), BF16), sd((1, 2 * d), F32)],
        scratch_shapes=[pltpu.VMEM((2, tb, LANES), F32)],
        compiler_params=_cparams(("arbitrary",)))(dx2, ya, yb, ga, gb, ob, w_oa, w_ob_t, w_o, after))
    return res[:1] + [res[1:4], res[4:7]] + res[7:]


def _in_proj_bwd(dx2, x, dqrot, dkrot, dva, qraw, kraw, tabs, dqb, dkb, dvb, dga, dgb, w_in_t, g_mix, q_g, k_g, tb):
    s, d = x.shape
    din = w_in_t.shape[0]
    q_scale = HEAD_DIM_A ** -0.5
    b_scale = HEAD_DIM_B ** -0.5
    tc = 256

    def body(dx2_ref, x_ref, dq_ref, dk_ref, dv_ref, qraw_ref, kraw_ref, c_ref, s1_ref, s2_ref, *rest):
        dqb_refs, dkb_refs, dvb_refs = rest[0:3], rest[3:6], rest[6:9]
        (dga_ref, dgb_ref, w_ref, gmix_ref, qg_ref, kg_ref,
         dx_ref, dz_ref, dgmix_ref, dqg_ref, dkg_ref, dh_ref, scr_ref) = rest[9:]

        @pl.when(pl.program_id(0) == 0)
        def _():
            dgmix_ref[...] = jnp.zeros_like(dgmix_ref)
            dqg_ref[...] = jnp.zeros_like(dqg_ref)
            dkg_ref[...] = jnp.zeros_like(dkg_ref)

        cos, s1, s2 = c_ref[...][None], s1_ref[...][None], s2_ref[...][None]

        def heads_bwd(drot, z, g_ref, acc_ref):
            dn = drot * cos + pltpu.roll(drot * s1, 96, 2) + pltpu.roll(drot * s2, 32, 2)
            rr = _rstd(z)
            nn = z * rr
            acc_ref[...] += jnp.sum(jnp.sum(dn * nn, axis=0), axis=0, keepdims=True)
            return _rms_bwd(dn, nn, rr, g_ref[...][None]).astype(BF16)

        dh_ref[...] = jnp.zeros_like(dh_ref)

        def emit(off, piece):
            dz_ref[:, off:off + tc] = piece
            dh_ref[...] += _dot_nn(piece, w_ref[off:off + tc, :])

        for j in range(d // tc):
            emit(OFF_GA + tc * j, dga_ref[:, tc * j:tc * j + tc])
            emit(OFF_GA + d + tc * j, dgb_ref[:, tc * j:tc * j + tc])
        emit(OFF_VA, dv_ref[...].astype(BF16))
        for g, dil in enumerate(DILATIONS):
            emit(OFF_QB + GB_W * g, (_from_residues(dqb_refs[g], scr_ref, dil) * b_scale).astype(BF16))
            emit(OFF_KB + GB_W * g, (_from_residues(dkb_refs[g], scr_ref, dil) * LN_2).astype(BF16))
            emit(OFF_VB + GB_W * g, _from_residues(dvb_refs[g], scr_ref, dil).astype(BF16))
        stack = lambda ref, n: jnp.stack([ref[:, 128 * h:128 * h + 128] for h in range(n)], axis=0)
        dzq = heads_bwd(stack(dq_ref, N_Q_HEADS_A) * q_scale, stack(qraw_ref, N_Q_HEADS_A), qg_ref, dqg_ref)
        dzk = heads_bwd(stack(dk_ref, N_KV_HEADS_A) * LN_2, stack(kraw_ref, N_KV_HEADS_A), kg_ref, dkg_ref)
        for j in range(N_Q_HEADS_A // 2):
            emit(OFF_QA + tc * j, jnp.concatenate([dzq[2 * j], dzq[2 * j + 1]], axis=1))
        emit(OFF_KA, jnp.concatenate([dzk[0], dzk[1]], axis=1))
        xv = x_ref[...]
        r1 = _rstd(xv)
        n1 = xv * r1
        dh = dh_ref[...]
        dgmix_ref[...] += _colsum(dh * n1)
        dx_ref[...] = dx2_ref[...] + _rms_bwd(dh, n1, r1, gmix_ref[...])

    sd = jax.ShapeDtypeStruct
    return pl.pallas_call(
        body, name="in_proj_bwd", grid=(s // tb,),
        in_specs=[_rows(tb, d), _rows(tb, d), _rows(tb, QA_W), _rows(tb, KA_W), _rows(tb, KA_W), _rows(tb, QA_W),
                  _rows(tb, KA_W), _rows(tb, LANES), _rows(tb, LANES), _rows(tb, LANES),
                  ] + _dil_specs(tb) * 3 + [_rows(tb, d), _rows(tb, d),
                  _resident(w_in_t.shape), _resident(g_mix.shape), _resident(q_g.shape), _resident(k_g.shape)],
        out_specs=[_rows(tb, d), _rows(tb, din), _acc_spec((1, d)), _acc_spec((1, HEAD_DIM_A)),
                   _acc_spec((1, HEAD_DIM_A))],
        out_shape=[sd((s, d), F32), sd((s, din), BF16), sd((1, d), F32), sd((1, HEAD_DIM_A), F32),
                   sd((1, HEAD_DIM_A), F32)],
        scratch_shapes=[pltpu.VMEM((tb, d), F32), pltpu.VMEM((2, tb, LANES), F32)],
        compiler_params=_cparams(("arbitrary",)))(
        dx2, x, dqrot, dkrot, dva, qraw, kraw, *tabs, *dqb, *dkb, *dvb, dga, dgb, w_in_t, g_mix, q_g, k_g)


def _identity(v):
    return v


def _to_bf16(v):
    return v.astype(BF16)


def _square_bf16(v):
    vf = v.astype(F32)
    return (vf * vf).astype(BF16)


def _weight_grad(name, a, b, ti, tj, tk, a_fn=_identity, b_fn=_identity, col0=0, n=None, after=None):
    t, m = a.shape
    n = b.shape[1] if n is None else n
    n_k = t // tk
    after = a if after is None else after

    def body(a_ref, b_ref, after_ref, o_ref, acc_ref):
        k = pl.program_id(2)

        @pl.when(k == 0)
        def _():
            acc_ref[...] = jnp.zeros_like(acc_ref)

        acc_ref[...] += _dot_tn(a_fn(a_ref[...]), b_fn(b_ref[...]))

        @pl.when(k == n_k - 1)
        def _():
            o_ref[...] = acc_ref[...].astype(BF16)

    return pl.pallas_call(
        body, name=name, grid=(m // ti, n // tj, n_k),
        in_specs=[pl.BlockSpec((tk, ti), lambda i, j, k: (k, i)),
                  pl.BlockSpec((tk, tj), lambda i, j, k: (k, j + col0 // tj)), _ANY],
        out_specs=pl.BlockSpec((ti, tj), lambda i, j, k: (i, j)),
        out_shape=jax.ShapeDtypeStruct((m, n), BF16),
        scratch_shapes=[pltpu.VMEM((ti, tj), F32)],
        compiler_params=_cparams(("arbitrary", "arbitrary", "arbitrary")))(a, b, after)


def _sum_slots(name, recv, own, transposed):
    m, n, k = recv.shape
    tc = min(k, 256)
    n_pad = -(-n // LANES) * LANES

    def body(own_ref, r_ref, o_ref):
        acc = own_ref[...].astype(F32)
        for i in range(m):
            acc = acc + r_ref[i].astype(F32)
        if transposed:
            if n_pad != n:
                acc = jnp.concatenate([acc, jnp.zeros((n_pad - n, tc), F32)], axis=0)
            acc = acc.T[:, :n]
        o_ref[...] = acc

    out_spec, out_shape = ((pl.BlockSpec((tc, n), lambda j: (j, 0)), (k, n)) if transposed
                           else (pl.BlockSpec((n, tc), lambda j: (0, j)), (n, k)))
    return pl.pallas_call(
        body, name=name, grid=(k // tc,),
        in_specs=[pl.BlockSpec((n, tc), lambda j: (0, j)), pl.BlockSpec((m, n, tc), lambda j: (0, 0, j))],
        out_specs=out_spec, out_shape=jax.ShapeDtypeStruct(out_shape, F32),
        compiler_params=_cparams(("arbitrary",)))(own, recv)


def _adamw_math(w, g, m, v):
    m = ADAM_B1 * m + (1.0 - ADAM_B1) * g
    v = ADAM_B2 * v + (1.0 - ADAM_B2) * (g * g)
    m_hat = m / (1.0 - ADAM_B1 ** ADAM_STEP)
    v_hat = v / (1.0 - ADAM_B2 ** ADAM_STEP)
    delta = -ADAM_LR * (m_hat / (jnp.sqrt(v_hat) + ADAM_EPS) + ADAM_WD * w)
    return delta, m, v


def _adamw(name, w, g, m, v):
    r, c = w.shape
    tr = max(t for t in range(8, min(r, 256) + 1, 8) if r % t == 0)

    def body(w_ref, g_ref, m_ref, v_ref, d_ref, mo_ref, vo_ref):
        d_ref[...], mo_ref[...], vo_ref[...] = _adamw_math(w_ref[...], g_ref[...], m_ref[...], v_ref[...])

    spec = pl.BlockSpec((tr, c), lambda i: (i, 0))
    return pl.pallas_call(
        body, name=name, grid=(r // tr,), in_specs=[spec] * 4, out_specs=[spec] * 3,
        out_shape=[jax.ShapeDtypeStruct((r, c), F32)] * 3,
        compiler_params=_cparams(("arbitrary",)))(w, g, m, v)


def _small_update(parts, w, m, v):
    def body(p_ref, w_ref, m_ref, v_ref, g_ref, d_ref, mo_ref, vo_ref):
        g = p_ref[0]
        for i in range(1, N_DEV):
            g = g + p_ref[i]
        g_ref[...] = g
        d_ref[...], mo_ref[...], vo_ref[...] = _adamw_math(w_ref[...], g, m_ref[...], v_ref[...])

    return pl.pallas_call(body, name="small_update", out_shape=[jax.ShapeDtypeStruct(w.shape, F32)] * 4)(
        parts, w, m, v)


def _pack_rows(vectors, n_rows):
    flat = jnp.concatenate([v.reshape(-1).astype(F32) for v in vectors])
    flat = jnp.pad(flat, (0, n_rows * LANES - flat.shape[0]))
    return flat.reshape(n_rows, LANES)


def _pick_tile(n, prefs):
    for t in prefs:
        if n % t == 0:
            return t
    return n


def kernel(x, p, norm_mix_g, w_in, b_gate, q_norm_g, k_norm_g, rel_bias, w_out_a, w_out_b, w_out, norm_mlp_g, w_ff1, w_ff2, norm_ple_g, w_ple_gate, w_ple, final_norm_g, loss_target, m_norm_mix_g, m_w_in, m_b_gate, m_q_norm_g, m_k_norm_g, m_rel_bias, m_w_out_a, m_w_out_b, m_w_out, m_norm_mlp_g, m_w_ff1, m_w_ff2, m_norm_ple_g, m_w_ple_gate, m_w_ple, m_final_norm_g, v_norm_mix_g, v_w_in, v_b_gate, v_q_norm_g, v_k_norm_g, v_rel_bias, v_w_out_a, v_w_out_b, v_w_out, v_norm_mlp_g, v_w_ff1, v_w_ff2, v_norm_ple_g, v_w_ple_gate, v_w_ple, v_final_norm_g):
    s, d = x.shape[1], x.shape[2]
    xs, ps, ts = x[0], p[0, 0], loss_target[0]
    tb = _pick_tile(s, (512, 256))
    tq = _pick_tile(s, (256,))
    tk = _pick_tile(s, (1024, 512))
    cb = _pick_tile(s, (1024, 512))
    fin_g = final_norm_g.reshape(1, d)

    col_sharded = {"w_in": w_in[0], "w_out_b": w_out_b[0], "w_ff1": w_ff1[0], "w_ple": w_ple[0]}
    row_sharded = {"w_out_a": w_out_a[0], "w_out": w_out[0], "w_ff2": w_ff2[0], "w_ple_gate": w_ple_gate[0]}
    order = ["w_in", "w_out_a", "w_out_b", "w_out", "w_ff1", "w_ff2", "w_ple_gate", "w_ple"]
    shards = [(col_sharded[n].T if n in col_sharded else row_sharded[n]).astype(BF16) for n in order]
    my_idx = 4 * lax.axis_index("x") + 2 * lax.axis_index("y") + lax.axis_index("c")
    (w_in_t,) = _all_gather(shards[:1], 1)
    zones = _place_own_rows(shards[1:], my_idx)
    ag = _copies_start("weights_gather_start", shards[1:], zones, w_in_t, True)

    tabs = _rope_tables(s)
    (h1, qraw, kraw, qrot, krot, va, qb, kb, vb, ga, gb) = _in_proj(
        xs, tabs, w_in_t, norm_mix_g, b_gate, q_norm_g, k_norm_g, tb, ag[4])
    oa, lse_a = _attn_a_fwd(qrot, krot, va, tq, tk)
    _, (w_oa, w_ob_t, w_o, w_ff1_t, w_ff2_f, w_pg, w_p_t) = _copies_wait(
        "weights_gather_wait", ag[0], ag[1], ag[2], ag[3], lse_a, True)
    flat = lambda arrs: [a.reshape(s, GB_W) for a in arrs]
    split = lambda arrs: [a.reshape(dil, s // dil, GB_W) for a, dil in zip(arrs, DILATIONS)]
    qb_r, kb_r, vb_r = flat(qb), flat(kb), flat(vb)
    bmaps = [jnp.asarray(_bucket_map(dil)) for dil in DILATIONS]
    bias_tabs = [rel_bias[:, N_HEADS_PER_DIL * g:N_HEADS_PER_DIL * (g + 1)] for g in range(3)]
    band_out = [_band_fwd(dil, qb_r[g], kb_r[g], vb_r[g], bmaps[g], bias_tabs[g], cb)
                for g, dil in enumerate(DILATIONS)]
    og, lg = split([o for o, _ in band_out]), split([l for _, l in band_out])
    x2, ob, lse_b, ya, yb, u = _mix_out(xs, oa, og, lg, ga, gb, w_oa, w_ob_t, w_o, tb)
    tc = _pick_tile(w_ff1_t.shape[0], (512,))
    x3, r_act, h2 = _mlp_fwd(x2, w_ff1_t, w_ff2_f, norm_mlp_g, tb, tc)

    dx3, h3, dpre, dpe, pb, loss_part, dg_fin, dg_ple = _ple_loss(
        x3, ps, ts, w_pg, w_p_t, norm_ple_g, fin_g, tb)
    dx2, df, dg_mlp = _mlp_bwd(dx3, x2, r_act, w_ff1_t, w_ff2_f, norm_mlp_g, tb, tc)

    tkk = _pick_tile(s, (1024, 512))
    tk2 = _pick_tile(s, (2048, 1024, 512))
    dff = w_ff1_t.shape[0]
    t1k = lambda n: _pick_tile(n, (1024, 512, 256))
    slots = lambda parts: [lax.empty((7, a.shape[0] // N_DEV, a.shape[1]), BF16) for a in parts]
    part1 = [_weight_grad("grad_w_ff1", df, h2, t1k(dff), t1k(d), tkk),
             _weight_grad("grad_w_ff2", r_act, dx3, t1k(dff), t1k(d), tkk, a_fn=_square_bf16, b_fn=_to_bf16),
             _weight_grad("grad_w_ple_gate", h3, dpre, t1k(d), t1k(d), tk2),
             _weight_grad("grad_w_ple", dpe, pb, t1k(d), ps.shape[1], tk2)]
    doa, dob, dd, dga, dgb, dya, dyb, dbg = _mix_out_bwd(dx2, ya, yb, ga, gb, ob, w_oa, w_ob_t, w_o, tb, dx2)
    part1 += [_weight_grad("grad_w_out_a", oa, dya, t1k(QA_W), t1k(d), tk2),
              _weight_grad("grad_w_out_b", dyb, ob, t1k(d), GB_W, tk2),
              _weight_grad("grad_w_out", u, dx2, t1k(d), t1k(d), tkk, b_fn=_to_bf16)]
    rs1 = _copies_start("grads1_start", part1, slots(part1), doa, False)
    dqrot, dkrot, dva = _attn_a_bwd(qrot, krot, va, oa, doa, lse_a, tq, tk, rs1[4])
    dob_r, lse_r, dd_r = flat(dob), flat(lse_b), flat(dd)
    band_bwd = [_band_bwd(dil, qb_r[g], kb_r[g], vb_r[g], dob_r[g], lse_r[g], dd_r[g], bmaps[g], bias_tabs[g], cb)
                for g, dil in enumerate(DILATIONS)]
    dqb, dkb, dvb = [split([r[j] for r in band_bwd]) for j in range(3)]
    grad_x, dz, dg_mix, dg_q, dg_k = _in_proj_bwd(
        dx2, xs, dqrot, dkrot, dva, qraw, kraw, tabs, dqb, dkb, dvb, dga, dgb, w_in_t, norm_mix_g,
        q_norm_g, k_norm_g, _pick_tile(s, (256,)))
    d_rel = jnp.concatenate([r[3][:, :N_HEADS_PER_DIL] for r in band_bwd], axis=1)

    din = w_in_t.shape[0]
    ti_in = _pick_tile(din, (din // 2,)) if (din // 2) % LANES == 0 else din
    hd_ = d // 2
    part3 = [_weight_grad("grad_w_in_lo", dz, h1, ti_in, t1k(hd_), tkk, n=hd_)]
    rs3 = _copies_start("grads3_start", part3, slots(part3), grad_x, False)
    part4 = [_weight_grad("grad_w_in_hi", dz, h1, ti_in, t1k(hd_), tkk, col0=hd_, n=hd_, after=rs3[4])]
    rs4 = _copies_start("grads4_start", part4, slots(part4), rs3[4], False)

    def own_rows(a):
        n = a.shape[0] // N_DEV
        return lax.dynamic_slice(a, (my_idx * n, 0), (n, a.shape[1]))

    sums = {}
    src1, got1 = _copies_wait("grads1_wait", rs1[0], rs1[1], rs1[2], rs1[3], rs4[4], False)
    for n, a, r in zip(["w_ff1", "w_ff2", "w_ple_gate", "w_ple", "w_out_a", "w_out_b", "w_out"], src1, got1):
        sums[n] = _sum_slots("sum_" + n, r, own_rows(a), n in col_sharded)
    given_w = dict(w_in=w_in, w_out_a=w_out_a, w_out_b=w_out_b, w_out=w_out, w_ff1=w_ff1, w_ff2=w_ff2,
                   w_ple_gate=w_ple_gate, w_ple=w_ple)
    given_m = dict(w_in=m_w_in, w_out_a=m_w_out_a, w_out_b=m_w_out_b, w_out=m_w_out, w_ff1=m_w_ff1, w_ff2=m_w_ff2,
                   w_ple_gate=m_w_ple_gate, w_ple=m_w_ple)
    given_v = dict(w_in=v_w_in, w_out_a=v_w_out_a, w_out_b=v_w_out_b, w_out=v_w_out, w_ff1=v_w_ff1, w_ff2=v_w_ff2,
                   w_ple_gate=v_w_ple_gate, w_ple=v_w_ple)
    big = {}

    def update(n, transposed=False):
        view = (lambda a: a.T) if transposed else (lambda a: a)
        g = sums[n]
        delta, new_m, new_v = _adamw("adamw_" + n, view(given_w[n][0]), g, view(given_m[n][0]), view(given_v[n][0]))
        big[n] = tuple(view(a)[None] for a in (g, delta, new_m, new_v))

    for n in order[1:]:
        update(n)

    small_names = ["norm_mix_g", "b_gate", "q_norm_g", "k_norm_g", "rel_bias", "norm_mlp_g", "norm_ple_g",
                   "final_norm_g"]
    small_w = [norm_mix_g, b_gate, q_norm_g, k_norm_g, rel_bias, norm_mlp_g, norm_ple_g, final_norm_g]
    small_m = [m_norm_mix_g, m_b_gate, m_q_norm_g, m_k_norm_g, m_rel_bias, m_norm_mlp_g, m_norm_ple_g,
               m_final_norm_g]
    small_v = [v_norm_mix_g, v_b_gate, v_q_norm_g, v_k_norm_g, v_rel_bias, v_norm_mlp_g, v_norm_ple_g,
               v_final_norm_g]
    small_g = [dg_mix, dbg, dg_q, dg_k, d_rel, dg_mlp, dg_ple, dg_fin]
    sizes = [int(np.prod(w.shape)) for w in small_w]
    n_rows = -(-(sum(-(-sz // LANES) for sz in sizes) + 1) // 8) * 8
    pad = lambda v: jnp.pad(v.reshape(-1).astype(F32), (0, -v.size % LANES))
    pack = lambda vs, last: _pack_rows([pad(v) for v in vs] + [last], n_rows)
    zero_row = jnp.zeros((LANES,), F32)
    parts = _small_all_gather(pack(small_g, loss_part.reshape(-1) * (jnp.arange(LANES) == 0)), big["w_ple"][1])
    g_all, d_all, m_all, v_all = _small_update(parts, pack(small_w, zero_row), pack(small_m, zero_row),
                                               pack(small_v, zero_row))
    small = {}
    row = 0
    for n, w, sz in zip(small_names, small_w, sizes):
        nr = -(-sz // LANES)
        small[n] = tuple(a[row:row + nr].reshape(-1)[:sz].reshape(w.shape) for a in (g_all, d_all, m_all, v_all))
        row += nr
    loss = g_all[row, 0]

    src3, got3 = _copies_wait("grads3_wait", rs3[0], rs3[1], rs3[2], rs3[3], g_all, False)
    src4, got4 = _copies_wait("grads4_wait", rs4[0], rs4[1], rs4[2], rs4[3], g_all, False)
    sums["w_in"] = jnp.concatenate([_sum_slots("sum_w_in_lo", got3[0], own_rows(src3[0]), False),
                                    _sum_slots("sum_w_in_hi", got4[0], own_rows(src4[0]), False)], axis=1)
    update("w_in", transposed=True)

    names = ["norm_mix_g", "w_in", "b_gate", "q_norm_g", "k_norm_g", "rel_bias", "w_out_a", "w_out_b", "w_out",
             "norm_mlp_g", "w_ff1", "w_ff2", "norm_ple_g", "w_ple_gate", "w_ple", "final_norm_g"]
    res = {n: (big[n] if n in big else small[n]) for n in names}
    return (loss, grad_x[None], *[res[n][0] for n in names], *[res[n][1] for n in names],
            *[res[n][2] for n in names], *[res[n][3] for n in names])
```

```python
import math

import numpy as np
import jax
import jax.numpy as jnp
from jax import lax
from jax.experimental import pallas as pl
from jax.experimental.pallas import tpu as pltpu

F32 = jnp.float32
BF16 = jnp.bfloat16
MESH = pl.DeviceIdType.MESH

NORM_EPS = 1e-6
NEG_INF = -1e30
LOG2_E = math.log2(math.e)
LN_2 = math.log(2.0)
GRID_W = 64
ROPE_THETA = 10000.0
HEAD_DIM_A = 128
N_Q_HEADS_A = 8
N_KV_HEADS_A = 2
Q_PER_KV = N_Q_HEADS_A // N_KV_HEADS_A
HEAD_DIM_B = 64
N_HEADS_PER_DIL = 4
DILATIONS = (1, 4, 16)
BAND = 64
N_REL_BUCKETS = 32
REL_MAX_DIST = 1024
QA_W = N_Q_HEADS_A * HEAD_DIM_A
KA_W = N_KV_HEADS_A * HEAD_DIM_A
GB_W = N_HEADS_PER_DIL * HEAD_DIM_B
QB_W = GB_W * len(DILATIONS)
OFF_QA, OFF_KA, OFF_VA = 0, QA_W, QA_W + KA_W
OFF_QB = QA_W + 2 * KA_W
OFF_KB = OFF_QB + QB_W
OFF_VB = OFF_KB + QB_W
OFF_GA = OFF_VB + QB_W
N_DEV = 8
LANES = 128
VMEM_LIMIT = 56 * 2 ** 20

ADAM_LR, ADAM_B1, ADAM_B2, ADAM_EPS, ADAM_WD, ADAM_STEP = 0.001, 0.9, 0.999, 1e-08, 0.01, 10


def _cparams(sem):
    return pltpu.CompilerParams(dimension_semantics=sem, vmem_limit_bytes=VMEM_LIMIT)


def _resident(shape):
    nd = len(shape)
    return pl.BlockSpec(shape, lambda *_: (0,) * nd, pipeline_mode=pl.Buffered(1))


def _acc_spec(shape):
    nd = len(shape)
    return pl.BlockSpec(shape, lambda *_: (0,) * nd)


def _rows(tb, c):
    return pl.BlockSpec((tb, c), lambda i: (i, 0))


def _dil_shapes(s, dtype):
    return [jax.ShapeDtypeStruct((dil, s // dil, GB_W), dtype) for dil in DILATIONS]


def _dil_specs(tb):
    return [pl.BlockSpec((dil, tb // dil, GB_W), lambda i: (0, i, 0)) for dil in DILATIONS]


def _to_residues(val, out_ref, scr_ref, dil, dtype):
    if dil == 1:
        out_ref[0] = val.astype(dtype)
        return
    n = val.shape[0] // dil
    scr_ref[0] = val[:, :LANES]
    scr_ref[1] = val[:, LANES:]
    for r in range(dil):
        out_ref[r] = jnp.concatenate([scr_ref[0, pl.ds(r, n, stride=dil), :],
                                      scr_ref[1, pl.ds(r, n, stride=dil), :]], axis=1).astype(dtype)


def _from_residues(in_ref, scr_ref, dil):
    if dil == 1:
        return in_ref[0]
    n = in_ref.shape[1]
    for r in range(dil):
        v = in_ref[r]
        scr_ref[0, pl.ds(r, n, stride=dil), :] = v[:, :LANES]
        scr_ref[1, pl.ds(r, n, stride=dil), :] = v[:, LANES:]
    return jnp.concatenate([scr_ref[0], scr_ref[1]], axis=1)


def _dot_nt(a, b):
    return lax.dot_general(a, b, (((1,), (1,)), ((), ())), preferred_element_type=F32)


def _dot_nn(a, b):
    return lax.dot_general(a, b, (((1,), (0,)), ((), ())), preferred_element_type=F32)


def _dot_tn(a, b):
    return lax.dot_general(a, b, (((0,), (0,)), ((), ())), preferred_element_type=F32)


def _rstd(x):
    return lax.rsqrt(jnp.mean(x * x, axis=-1, keepdims=True) + NORM_EPS)


def _rms_bwd(dy, n, r, g):
    dn = dy * g
    return r * (dn - n * jnp.mean(dn * n, axis=-1, keepdims=True))


def _colsum(v):
    return jnp.sum(v, axis=0, keepdims=True)


def _sigmoid(v):
    return 1.0 / (1.0 + jnp.exp(-v))


def _rope_tables(s):
    half = HEAD_DIM_A // 2
    inv = np.power(np.float32(ROPE_THETA), -np.arange(0, half, 2, dtype=np.float32) / np.float32(half))
    t = np.arange(s)
    ang_r = (t // GRID_W).astype(np.float32)[:, None] * inv[None, :]
    ang_c = (t % GRID_W).astype(np.float32)[:, None] * inv[None, :]
    cr, sr, cc, sc = np.cos(ang_r), np.sin(ang_r), np.cos(ang_c), np.sin(ang_c)
    z = np.zeros_like(sr)
    cos = np.concatenate([cr, cr, cc, cc], axis=1)
    s1 = np.concatenate([z, sr, z, sc], axis=1)
    s2 = np.concatenate([-sr, z, -sc, z], axis=1)
    return [jnp.asarray(a, F32) for a in (cos, s1, s2)]


def _my_place():
    return lax.axis_index("x"), lax.axis_index("y"), lax.axis_index("c")


def _all_gather(shards, n_gather):
    n_all = len(shards)
    nw = n_gather

    def body(*refs):
        ins, outs = refs[:n_all], refs[n_all:2 * n_all]
        send_sems, recv_sems, local_sems = refs[2 * n_all:]
        x, y, c = _my_place()
        me, sibling = (x, y, c), (x, y, 1 - c)
        chips = [(1 - x, y), (x, 1 - y), (1 - x, 1 - y)]

        def rows(w, px, py, pc):
            n = ins[w].shape[0]
            return outs[w].at[pl.ds(pl.multiple_of((4 * px + 2 * py + pc) * n, 16), n), :]

        def copy(w, k, block, to, src=None):
            return pltpu.make_async_remote_copy(
                src_ref=rows(w, *block) if src is None else src, dst_ref=rows(w, *block),
                send_sem=send_sems.at[w, k], recv_sem=recv_sems.at[w, k], device_id=to, device_id_type=MESH)

        mine = [pltpu.make_async_copy(ins[w], rows(w, *me), local_sems.at[w]) for w in range(n_all)]
        for cp in mine:
            cp.start()
        first = []
        for w in range(nw):
            first.append(copy(w, 0, me, sibling, src=ins[w]))
            first += [copy(w, 1 + j, me, (*chip, c), src=ins[w]) for j, chip in enumerate(chips)]
        for cp in first:
            cp.start()
        passed = []
        for j, chip in enumerate(chips):
            for w in range(nw):
                copy(w, 1 + j, (*chip, c), me).wait_recv()
                fwd = copy(w, 4 + j, (*chip, c), sibling)
                fwd.start()
                passed.append(fwd)
        for w in range(nw):
            copy(w, 0, sibling, me).wait_recv()
        for j, chip in enumerate(chips):
            for w in range(nw):
                copy(w, 4 + j, (*chip, 1 - c), me).wait_recv()
        for cp in first + passed:
            cp.wait_send()
        for cp in mine:
            cp.wait()

    any_spec = pl.BlockSpec(memory_space=pl.ANY)
    return pl.pallas_call(
        body, name="weights_all_gather",
        out_shape=[jax.ShapeDtypeStruct((N_DEV * s.shape[0], s.shape[1]), s.dtype) for s in shards],
        in_specs=[any_spec] * n_all, out_specs=[any_spec] * n_all,
        scratch_shapes=[pltpu.SemaphoreType.DMA((nw, 7)), pltpu.SemaphoreType.DMA((nw, 7)),
                        pltpu.SemaphoreType.DMA((n_all,))],
    )(*shards)


def _place_own_rows(shards, my_idx):
    nw = len(shards)

    def body(idx_ref, *refs):
        for w in range(nw):
            refs[nw + w][...] = refs[w][...]

    grid_spec = pltpu.PrefetchScalarGridSpec(
        num_scalar_prefetch=1, grid=(1,),
        in_specs=[pl.BlockSpec(s.shape, lambda i, idx: (0, 0)) for s in shards],
        out_specs=[pl.BlockSpec(s.shape, lambda i, idx: (idx[0], 0)) for s in shards])
    return pl.pallas_call(
        body, name="place_own_rows", grid_spec=grid_spec,
        out_shape=[jax.ShapeDtypeStruct((N_DEV * s.shape[0], s.shape[1]), s.dtype) for s in shards],
        compiler_params=_cparams(("arbitrary",)))(my_idx.reshape(1).astype(jnp.int32), *shards)


_FLIPS = [(fx, fy, fc) for fx in (0, 1) for fy in (0, 1) for fc in (0, 1)][1:]


def _small_all_gather(v, after):
    def body(v_ref, after_ref, out_ref, send_sems, recv_sems):
        x, y, c = _my_place()
        my_idx = 4 * x + 2 * y + c
        out_ref[my_idx] = v_ref[...]
        sends = []
        for k, (fx, fy, fc) in enumerate(_FLIPS):
            to = (1 - x if fx else x, 1 - y if fy else y, 1 - c if fc else c)
            sends.append(pltpu.make_async_remote_copy(
                src_ref=v_ref, dst_ref=out_ref.at[my_idx], send_sem=send_sems.at[k], recv_sem=recv_sems.at[k],
                device_id=to, device_id_type=MESH))
        for cp in sends:
            cp.start()
        for k, (fx, fy, fc) in enumerate(_FLIPS):
            frm_idx = 4 * (1 - x if fx else x) + 2 * (1 - y if fy else y) + (1 - c if fc else c)
            pltpu.make_async_remote_copy(
                src_ref=v_ref, dst_ref=out_ref.at[frm_idx], send_sem=send_sems.at[k], recv_sem=recv_sems.at[k],
                device_id=(x, y, c), device_id_type=MESH).wait_recv()
        for cp in sends:
            cp.wait_send()

    vm = pl.BlockSpec(memory_space=pltpu.VMEM)
    return pl.pallas_call(
        body, name="small_all_gather", out_shape=jax.ShapeDtypeStruct((N_DEV,) + v.shape, v.dtype),
        in_specs=[vm, pl.BlockSpec(memory_space=pl.ANY)], out_specs=vm,
        scratch_shapes=[pltpu.SemaphoreType.DMA((7,)), pltpu.SemaphoreType.DMA((7,))],
    )(v, after)


_HBM = pl.BlockSpec(memory_space=pltpu.HBM)
_SEM = pl.BlockSpec(memory_space=pltpu.SEMAPHORE)
_ANY = pl.BlockSpec(memory_space=pl.ANY)
_SPLIT_COPY = dict(has_side_effects=pltpu.SideEffectType.DATAFLOW_SIDE_EFFECTING)


def _peer(x, y, c, k):
    fx, fy, fc = _FLIPS[k]
    return (1 - x if fx else x, 1 - y if fy else y, 1 - c if fc else c)


def _in_hbm(a):
    return pltpu.with_memory_space_constraint(a, pltpu.HBM)


def _split_copies(srcs, lands, send_sems, recv_sems, gather, arriving):
    x, y, c = _my_place()
    my_idx = 4 * x + 2 * y + c
    out = []
    for k in range(7):
        to = _peer(x, y, c, k)
        to_idx = 4 * to[0] + 2 * to[1] + to[2]
        for w in range(len(srcs)):
            if gather:
                n = srcs[w].shape[0]
                src = srcs[w]
                dst = lands[w].at[pl.ds(pl.multiple_of((to_idx if arriving else my_idx) * n, 16), n), :]
            else:
                n = lands[w].shape[1]
                src = srcs[w].at[pl.ds(pl.multiple_of(to_idx * n, 16), n), :]
                dst = lands[w].at[k]
            out.append(pltpu.make_async_remote_copy(
                src_ref=src, dst_ref=dst, send_sem=send_sems.at[7 * w + k], recv_sem=recv_sems.at[7 * w + k],
                device_id=to, device_id_type=MESH))
    return out


def _copies_start(name, srcs, lands, after, gather):
    nw = len(srcs)

    def body(*refs):
        send_sems, recv_sems = refs[2 * nw + 1], refs[2 * nw + 2]
        for cp in _split_copies(refs[:nw], refs[nw:2 * nw], send_sems, recv_sems, gather, False):
            cp.start()
        refs[-1][...] = jnp.zeros_like(refs[-1])

    sems = pltpu.SemaphoreType.DMA((7 * nw,))
    thru = [pltpu.HBM(a.shape, a.dtype) for a in list(srcs) + list(lands)]
    res = pl.pallas_call(
        body, name=name, out_shape=(sems, sems, *thru, jax.ShapeDtypeStruct((8, LANES), F32)),
        in_specs=[_HBM] * (2 * nw) + [_ANY], out_specs=(_SEM, _SEM, *[_HBM] * (2 * nw), pl.BlockSpec(memory_space=pltpu.VMEM)),
        input_output_aliases={i: 2 + i for i in range(2 * nw)},
        compiler_params=pltpu.CompilerParams(**_SPLIT_COPY),
    )(*[_in_hbm(a) for a in srcs], *[_in_hbm(a) for a in lands], after)
    return res[0], res[1], list(res[2:2 + nw]), list(res[2 + nw:2 + 2 * nw]), res[-1]


def _copies_wait(name, send_sems, recv_sems, srcs, lands, after, gather):
    nw = len(srcs)

    def body(*refs):
        for cp in _split_copies(refs[:nw], refs[nw:2 * nw], refs[2 * nw], refs[2 * nw + 1], gather, False):
            cp.wait_send()
        for cp in _split_copies(refs[:nw], refs[nw:2 * nw], refs[2 * nw], refs[2 * nw + 1], gather, True):
            cp.wait_recv()

    thru = [pltpu.HBM(a.shape, a.dtype) for a in list(srcs) + list(lands)]
    res = pl.pallas_call(
        body, name=name, out_shape=tuple(thru),
        in_specs=[_HBM] * (2 * nw) + [_SEM, _SEM, _ANY], out_specs=tuple([_HBM] * (2 * nw)),
        input_output_aliases={i: i for i in range(2 * nw)},
        compiler_params=pltpu.CompilerParams(**_SPLIT_COPY),
    )(*srcs, *lands, send_sems, recv_sems, after)
    return list(res[:nw]), list(res[nw:])


def _in_proj(x, tabs, w_in_t, g_mix, b_gate, q_g, k_g, tb, after):
    s, d = x.shape
    n_gate_chunks = d // 256
    q_scale = HEAD_DIM_A ** -0.5 * LOG2_E
    b_scale = HEAD_DIM_B ** -0.5 * LOG2_E

    def body(x_ref, c_ref, s1_ref, s2_ref, w_ref, gmix_ref, bg_ref, qg_ref, kg_ref, after_ref,
             h1_ref, qraw_ref, kraw_ref, qrot_ref, krot_ref, va_ref, *rest):
        qb_refs, kb_refs, vb_refs = rest[0:3], rest[3:6], rest[6:9]
        ga_ref, gb_ref, scr_ref = rest[9:]
        xv = x_ref[...]
        hb = (xv * _rstd(xv) * gmix_ref[...]).astype(BF16)
        h1_ref[...] = hb
        cos, s1, s2 = c_ref[...], s1_ref[...], s2_ref[...]

        def proj(lo, width):
            return _dot_nt(hb, w_ref[lo:lo + width, :])

        def norm_rope(z, g):
            n = z * _rstd(z) * g
            return n * cos + pltpu.roll(n, 32, 1) * s1 + pltpu.roll(n, 96, 1) * s2

        for j in range(QA_W // 256):
            z = proj(OFF_QA + 256 * j, 256)
            qraw_ref[:, 256 * j:256 * j + 256] = z
            for hh in range(2):
                lo = 256 * j + 128 * hh
                qrot_ref[:, lo:lo + 128] = (norm_rope(z[:, 128 * hh:128 * hh + 128], qg_ref[...]) * q_scale).astype(BF16)
        z = proj(OFF_KA, 256)
        kraw_ref[...] = z
        for hh in range(2):
            krot_ref[:, 128 * hh:128 * hh + 128] = norm_rope(z[:, 128 * hh:128 * hh + 128], kg_ref[...]).astype(BF16)
        va_ref[...] = proj(OFF_VA, 256).astype(BF16)
        for g, dil in enumerate(DILATIONS):
            _to_residues(proj(OFF_QB + GB_W * g, GB_W) * b_scale, qb_refs[g], scr_ref, dil, BF16)
            _to_residues(proj(OFF_KB + GB_W * g, GB_W), kb_refs[g], scr_ref, dil, BF16)
            _to_residues(proj(OFF_VB + GB_W * g, GB_W), vb_refs[g], scr_ref, dil, BF16)
        for j in range(n_gate_chunks):
            sl = slice(256 * j, 256 * j + 256)
            ga_ref[:, sl] = _sigmoid(proj(OFF_GA + 256 * j, 256) + bg_ref[:, sl]).astype(BF16)
            gb_ref[:, sl] = _sigmoid(
                proj(OFF_GA + d + 256 * j, 256) + bg_ref[:, d + 256 * j:d + 256 * j + 256]).astype(BF16)

    sd = jax.ShapeDtypeStruct
    outs = [sd((s, d), BF16), sd((s, QA_W), F32), sd((s, KA_W), F32), sd((s, QA_W), BF16), sd((s, KA_W), BF16),
            sd((s, KA_W), BF16)] + _dil_shapes(s, BF16) * 3 + [sd((s, d), BF16), sd((s, d), BF16)]
    out_specs = [_rows(tb, d), _rows(tb, QA_W), _rows(tb, KA_W), _rows(tb, QA_W), _rows(tb, KA_W), _rows(tb, KA_W)
                 ] + _dil_specs(tb) * 3 + [_rows(tb, d), _rows(tb, d)]
    in_specs = [_rows(tb, d), _rows(tb, LANES), _rows(tb, LANES), _rows(tb, LANES), _resident(w_in_t.shape),
                _resident(g_mix.shape), _resident(b_gate.shape), _resident(q_g.shape), _resident(k_g.shape), _ANY]
    res = list(pl.pallas_call(body, name="in_proj", grid=(s // tb,), in_specs=in_specs, out_specs=out_specs,
                              out_shape=outs, scratch_shapes=[pltpu.VMEM((2, tb, LANES), F32)],
                              compiler_params=_cparams(("arbitrary",)))(
        x, *tabs, w_in_t, g_mix, b_gate, q_g, k_g, after))
    return res[:6] + [res[6:9], res[9:12], res[12:15]] + res[15:]


def _attn_a_fwd(qrot, krot, va, tq, tk):
    s = qrot.shape[0]
    n_kv = s // tk
    gw = Q_PER_KV * HEAD_DIM_A

    def body(q_ref, k_ref, v_ref, o_ref, lse_ref):
        q4 = jnp.concatenate([q_ref[:, 128 * h:128 * h + 128] for h in range(Q_PER_KV)], axis=0)

        def step(j, carry):
            m, l, acc = carry
            sl = pl.ds(pl.multiple_of(j * tk, tk), tk)
            kj, vj = k_ref[sl, :], v_ref[sl, :]
            sc = _dot_nt(kj, q4)
            m_new = jnp.maximum(m, jnp.max(sc, axis=0, keepdims=True))
            p = jnp.exp2(sc - m_new)
            alpha = jnp.exp2(m - m_new)
            l = alpha * l + jnp.sum(p, axis=0, keepdims=True)
            acc = alpha * acc + _dot_tn(vj, p.astype(BF16))
            return m_new, l, acc

        rows = Q_PER_KV * tq
        m, l, acc = lax.fori_loop(0, n_kv, step, (jnp.full((1, rows), NEG_INF, F32), jnp.zeros((1, rows), F32),
                                                  jnp.zeros((HEAD_DIM_A, rows), F32)))
        o = (acc / l).T
        lse = m + jnp.log2(l)
        for h in range(Q_PER_KV):
            o_ref[:, 128 * h:128 * h + 128] = o[h * tq:(h + 1) * tq].astype(BF16)
            lse_ref[0, h:h + 1, :] = lse[:, h * tq:(h + 1) * tq]

    return pl.pallas_call(
        body, name="attn_a_fwd", grid=(N_KV_HEADS_A, s // tq),
        in_specs=[pl.BlockSpec((tq, gw), lambda g, i: (i, g)),
                  pl.BlockSpec((s, HEAD_DIM_A), lambda g, i: (0, g)),
                  pl.BlockSpec((s, HEAD_DIM_A), lambda g, i: (0, g))],
        out_specs=[pl.BlockSpec((tq, gw), lambda g, i: (i, g)),
                   pl.BlockSpec((1, Q_PER_KV, tq), lambda g, i: (g, 0, i))],
        out_shape=[jax.ShapeDtypeStruct((s, QA_W), BF16), jax.ShapeDtypeStruct((N_KV_HEADS_A, Q_PER_KV, s), F32)],
        compiler_params=_cparams(("arbitrary", "arbitrary")))(qrot, krot, va)


def _attn_a_bwd(qrot, krot, va, oa, doa, lse, tq, tk, after):
    s = qrot.shape[0]
    n_kv = s // tk
    gw = Q_PER_KV * HEAD_DIM_A

    def body(q_ref, do_ref, o_ref, lse_ref, k_ref, v_ref, after_ref, dq_ref, dk_ref, dv_ref):
        @pl.when(pl.program_id(1) == 0)
        def _():
            dk_ref[...] = jnp.zeros_like(dk_ref)
            dv_ref[...] = jnp.zeros_like(dv_ref)

        def stack(ref):
            return jnp.concatenate([ref[:, 128 * h:128 * h + 128] for h in range(Q_PER_KV)], axis=0)

        q4, do4, o4 = stack(q_ref), stack(do_ref), stack(o_ref)
        delta = jnp.sum(do4.astype(F32) * o4.astype(F32), axis=-1, keepdims=True)
        lse_cols = jnp.concatenate([lse_ref[0], jnp.zeros_like(lse_ref[0])], axis=0).T
        lse4 = jnp.concatenate([lse_cols[:, h:h + 1] for h in range(Q_PER_KV)], axis=0)

        def step(j, dq):
            sl = pl.ds(pl.multiple_of(j * tk, tk), tk)
            kj, vj = k_ref[sl, :], v_ref[sl, :]
            p = jnp.exp2(_dot_nt(q4, kj) - lse4)
            ds = (p * (_dot_nt(do4, vj) - delta)).astype(BF16)
            dk_ref[sl, :] += _dot_tn(ds, q4)
            dv_ref[sl, :] += _dot_tn(p.astype(BF16), do4)
            return dq + _dot_nn(ds, kj)

        dq = lax.fori_loop(0, n_kv, step, jnp.zeros((Q_PER_KV * tq, HEAD_DIM_A), F32))
        for h in range(Q_PER_KV):
            dq_ref[:, 128 * h:128 * h + 128] = dq[h * tq:(h + 1) * tq]

    qspec = pl.BlockSpec((tq, gw), lambda g, i: (i, g))
    kspec = pl.BlockSpec((s, HEAD_DIM_A), lambda g, i: (0, g))
    return pl.pallas_call(
        body, name="attn_a_bwd", grid=(N_KV_HEADS_A, s // tq),
        in_specs=[qspec, qspec, qspec, pl.BlockSpec((1, Q_PER_KV, tq), lambda g, i: (g, 0, i)), kspec, kspec, _ANY],
        out_specs=[qspec, kspec, kspec],
        out_shape=[jax.ShapeDtypeStruct((s, QA_W), F32), jax.ShapeDtypeStruct((s, KA_W), F32),
                   jax.ShapeDtypeStruct((s, KA_W), F32)],
        compiler_params=_cparams(("arbitrary", "arbitrary")))(qrot, doa, oa, lse, krot, va, after)


BAND_QB = 128
BAND_WIN = BAND_QB + 2 * BAND


def _band_specs(s, cb):
    per = cb // BAND
    last = s // BAND - 1
    cur = pl.BlockSpec((cb, GB_W), lambda i: (i, 0))
    prev = pl.BlockSpec((BAND, GB_W), lambda i: (jnp.maximum(i * per - 1, 0), 0))
    nxt = pl.BlockSpec((BAND, GB_W), lambda i: (jnp.minimum(i * per + per, last), 0))
    return cur, prev, nxt


def _window(prev_ref, cur_ref, next_ref):
    return jnp.concatenate([prev_ref[...], cur_ref[...], next_ref[...]], axis=0)


def _band_mask(base, seg_shift):
    rq = base + lax.broadcasted_iota(jnp.int32, (BAND_QB, BAND_WIN), 0)
    rk = base - BAND + lax.broadcasted_iota(jnp.int32, (BAND_QB, BAND_WIN), 1)
    same_segment = lax.shift_right_arithmetic(rq, jnp.int32(seg_shift)) == lax.shift_right_arithmetic(rk, jnp.int32(seg_shift))
    return (jnp.abs(rk - rq) <= BAND) & same_segment


def _build_bias(bmap_ref, tab_ref, bias_ref):
    bm = bmap_ref[...]
    acc = [jnp.full(bm.shape, NEG_INF, F32) for _ in range(N_HEADS_PER_DIL)]
    for b in range(N_REL_BUCKETS):
        hit = bm == b
        for h in range(N_HEADS_PER_DIL):
            acc[h] = jnp.where(hit, tab_ref[b, h] * LOG2_E, acc[h])
    rows = bm.shape[0]
    for h in range(N_HEADS_PER_DIL):
        bias_ref[h * rows:(h + 1) * rows, :] = acc[h]


def _segment_mask(base, seg_len, seg_shift):
    if seg_len % BAND_QB:
        return _band_mask(base, seg_shift)
    pos = lax.rem(base, seg_len)
    w = lax.broadcasted_iota(jnp.int32, (1, BAND_WIN), 1)
    return ((w >= BAND) | (pos != 0)) & ((w < BAND + BAND_QB) | (pos != seg_len - BAND_QB))


def _head_lane_masks():
    lane = lax.broadcasted_iota(jnp.int32, (1, LANES), 1)
    return [lane < HEAD_DIM_B, lane >= HEAD_DIM_B]


def _rows4(mask):
    return mask if mask.shape[0] == 1 else jnp.concatenate([mask] * N_HEADS_PER_DIL, axis=0)


def _head_scores(a, b):
    hm = _head_lane_masks()
    out = []
    for hp in range(2):
        ls = slice(LANES * hp, LANES * hp + LANES)
        ah = a[:, ls]
        both = jnp.concatenate([jnp.where(hm[0], ah, jnp.zeros_like(ah)), jnp.where(hm[1], ah, jnp.zeros_like(ah))],
                               axis=0)
        out.append(_dot_nt(both, b[:, ls]))
    return jnp.concatenate(out, axis=0)


def _head_combine(p, v, scale=None, transposed=False):
    hm = _head_lane_masks()
    rows = p.shape[0] // N_HEADS_PER_DIL
    halves = []
    for hp in range(2):
        vh = v[:, LANES * hp:LANES * hp + LANES]
        acc = None
        for hh in range(2):
            h = 2 * hp + hh
            ph = p[h * rows:(h + 1) * rows]
            vm = jnp.where(hm[hh], vh, jnp.zeros_like(vh))
            t = _dot_tn(ph, vm) if transposed else _dot_nn(ph, vm)
            if scale is not None:
                t = t * scale[h * rows:(h + 1) * rows]
            acc = t if acc is None else acc + t
        halves.append(acc)
    return jnp.concatenate(halves, axis=1)


def _head_spread(col):
    rows = col.shape[0] // N_HEADS_PER_DIL
    lane = lax.broadcasted_iota(jnp.int32, (1, GB_W), 1)
    out = jnp.zeros((rows, GB_W), F32)
    for h in range(N_HEADS_PER_DIL):
        out = jnp.where((lane >= HEAD_DIM_B * h) & (lane < HEAD_DIM_B * (h + 1)), col[h * rows:(h + 1) * rows], out)
    return out


def _head_cols(v):
    return jnp.concatenate([v[:, HEAD_DIM_B * h:HEAD_DIM_B * h + 1] for h in range(N_HEADS_PER_DIL)], axis=0)


def _seg_shift(s, dil):
    seg = s // dil
    assert seg & (seg - 1) == 0, "segment length must be a power of two"
    return seg.bit_length() - 1


def _band_fwd(dil, qb, kb, vb, bmap, tab, cb):
    s = qb.shape[0]
    shift = _seg_shift(s, dil)

    def body(q_ref, kp_ref, kc_ref, kn_ref, vp_ref, vc_ref, vn_ref, bmap_ref, tab_ref, o_ref, lse_ref, bias_ref):
        @pl.when(pl.program_id(0) == 0)
        def _():
            _build_bias(bmap_ref, tab_ref, bias_ref)

        kw, vw = _window(kp_ref, kc_ref, kn_ref), _window(vp_ref, vc_ref, vn_ref)
        for jj in range(cb // BAND_QB):
            r0 = BAND_QB * jj
            mask = _rows4(_segment_mask(pl.program_id(0) * cb + r0, s // dil, shift))
            sc = _head_scores(q_ref[r0:r0 + BAND_QB, :], kw[r0:r0 + BAND_WIN, :]) + bias_ref[...]
            sc = jnp.where(mask, sc, NEG_INF)
            m = jnp.max(sc, axis=-1, keepdims=True)
            e = jnp.exp2(sc - m)
            l = jnp.sum(e, axis=-1, keepdims=True)
            o = _head_combine(e.astype(BF16), vw[r0:r0 + BAND_WIN, :], 1.0 / l)
            o_ref[r0:r0 + BAND_QB, :] = o
            lse_ref[r0:r0 + BAND_QB, :] = _head_spread(m + jnp.log2(l))

    cur, prev, nxt = _band_specs(s, cb)
    return pl.pallas_call(
        body, name=f"band_fwd_d{dil}", grid=(s // cb,),
        in_specs=[cur, prev, cur, nxt, prev, cur, nxt, _resident(bmap.shape), pl.BlockSpec(memory_space=pltpu.SMEM)],
        out_specs=[cur, cur],
        out_shape=[jax.ShapeDtypeStruct(qb.shape, F32), jax.ShapeDtypeStruct(qb.shape, F32)],
        scratch_shapes=[pltpu.VMEM((N_HEADS_PER_DIL * BAND_QB, BAND_WIN), F32)],
        compiler_params=_cparams(("arbitrary",)))(qb, kb, kb, kb, vb, vb, vb, bmap, tab)


def _band_bwd(dil, qb, kb, vb, dob, lse, dd, bmap, tab, cb):
    s = qb.shape[0]
    shift = _seg_shift(s, dil)
    n_steps = s // cb

    def body(q_ref, do_ref, lse_ref, dd_ref, kp_ref, kc_ref, kn_ref, vp_ref, vc_ref, vn_ref, bmap_ref, tab_ref,
             dq_ref, dk_ref, dv_ref, dtab_ref, bias_ref, dsum_ref):
        @pl.when(pl.program_id(0) == 0)
        def _():
            _build_bias(bmap_ref, tab_ref, bias_ref)
            dsum_ref[...] = jnp.zeros_like(dsum_ref)
            dk_ref[...] = jnp.zeros_like(dk_ref)
            dv_ref[...] = jnp.zeros_like(dv_ref)

        kw, vw = _window(kp_ref, kc_ref, kn_ref), _window(vp_ref, vc_ref, vn_ref)
        for jj in range(cb // BAND_QB):
            r0 = BAND_QB * jj
            base = pl.program_id(0) * cb + r0
            mask = _rows4(_segment_mask(base, s // dil, shift))
            qh, doh = q_ref[r0:r0 + BAND_QB, :], do_ref[r0:r0 + BAND_QB, :]
            k3, v3 = kw[r0:r0 + BAND_WIN, :], vw[r0:r0 + BAND_WIN, :]
            sc = _head_scores(qh, k3) + bias_ref[...]
            sc = jnp.where(mask, sc, NEG_INF)
            p = jnp.exp2(sc - _head_cols(lse_ref[r0:r0 + BAND_QB, :]))
            dp = _head_scores(doh, v3)
            ds = p * (dp - _head_cols(dd_ref[r0:r0 + BAND_QB, :]))
            dsum_ref[...] += ds
            dsb = ds.astype(BF16)
            dq_ref[r0:r0 + BAND_QB, :] = _head_combine(dsb, k3)
            dk_win = _head_combine(dsb, qh, transposed=True)
            dv_win = _head_combine(p.astype(BF16), doh, transposed=True)
            own = pl.ds(pl.multiple_of(base, BAND), BAND_QB)
            dk_ref[own, :] += dk_win[BAND:BAND + BAND_QB]
            dv_ref[own, :] += dv_win[BAND:BAND + BAND_QB]

            @pl.when(base > 0)
            def _():
                before = pl.ds(pl.multiple_of(base - BAND, BAND), BAND)
                dk_ref[before, :] += dk_win[:BAND]
                dv_ref[before, :] += dv_win[:BAND]

            @pl.when(base + BAND_QB < s)
            def _():
                after = pl.ds(pl.multiple_of(base + BAND_QB, BAND), BAND)
                dk_ref[after, :] += dk_win[BAND + BAND_QB:]
                dv_ref[after, :] += dv_win[BAND + BAND_QB:]

        @pl.when(pl.program_id(0) == n_steps - 1)
        def _():
            bm = bmap_ref[...]
            lane = lax.broadcasted_iota(jnp.int32, (1, LANES), 1)
            for b in range(N_REL_BUCKETS):
                hit = bm == b
                row = jnp.zeros((1, LANES), F32)
                for h in range(N_HEADS_PER_DIL):
                    part = dsum_ref[h * BAND_QB:(h + 1) * BAND_QB, :]
                    row = jnp.where(lane == h, jnp.sum(jnp.where(hit, part, 0.0)), row)
                dtab_ref[b:b + 1, :] = row

    cur, prev, nxt = _band_specs(s, cb)
    whole = _acc_spec(qb.shape)
    return pl.pallas_call(
        body, name=f"band_bwd_d{dil}", grid=(n_steps,),
        in_specs=[cur, cur, cur, cur, prev, cur, nxt, prev, cur, nxt, _resident(bmap.shape),
                  pl.BlockSpec(memory_space=pltpu.SMEM)],
        out_specs=[cur, whole, whole, _acc_spec((N_REL_BUCKETS, LANES))],
        out_shape=[jax.ShapeDtypeStruct(qb.shape, F32)] * 3 + [jax.ShapeDtypeStruct((N_REL_BUCKETS, LANES), F32)],
        scratch_shapes=[pltpu.VMEM((N_HEADS_PER_DIL * BAND_QB, BAND_WIN), F32),
                        pltpu.VMEM((N_HEADS_PER_DIL * BAND_QB, BAND_WIN), F32)],
        compiler_params=_cparams(("arbitrary",)))(qb, dob, lse, dd, kb, kb, kb, vb, vb, vb, bmap, tab)


def _t5_bucket(rel):
    nb = N_REL_BUCKETS // 2
    ret = (rel > 0).astype(np.int32) * nb
    n = np.abs(rel)
    max_exact = nb // 2
    large = max_exact + (np.log(np.maximum(n, 1) / max_exact) / math.log(REL_MAX_DIST / max_exact)
                         * (nb - max_exact)).astype(np.int32)
    large = np.minimum(large, nb - 1)
    return ret + np.where(n < max_exact, n, large).astype(np.int32)


def _bucket_map(dil):
    off = np.arange(BAND_WIN)[None, :] - BAND - np.arange(BAND_QB)[:, None]
    return np.where(np.abs(off) <= BAND, _t5_bucket(off * dil), -1).astype(np.int32)


def _seg_sum(v):
    lane = lax.broadcasted_iota(jnp.int32, (1, v.shape[1]), 1)
    out = jnp.zeros_like(v)
    for h in range(v.shape[1] // HEAD_DIM_B):
        m = (lane >= HEAD_DIM_B * h) & (lane < HEAD_DIM_B * (h + 1))
        out = jnp.where(m, jnp.sum(jnp.where(m, v, 0.0), axis=-1, keepdims=True), out)
    return out


def _mix_out(x, oa, og, lg, ga, gb, w_oa, w_ob_t, w_o, tb):
    s, d = x.shape

    def body(x_ref, oa_ref, og0_ref, og1_ref, og2_ref, lg0_ref, lg1_ref, lg2_ref, ga_ref, gb_ref,
             woa_ref, wob_ref, wo_ref, x2_ref, ob_ref, lse0_ref, lse1_ref, lse2_ref, ya_ref, yb_ref, u_ref, scr_ref):
        og_refs, lg_refs = (og0_ref, og1_ref, og2_ref), (lg0_ref, lg1_ref, lg2_ref)
        l0, l1, l2 = [_from_residues(lg_refs[g], scr_ref, dil) for g, dil in enumerate(DILATIONS)]
        lmax = jnp.maximum(jnp.maximum(l0, l1), l2)
        w0, w1, w2 = jnp.exp2(l0 - lmax), jnp.exp2(l1 - lmax), jnp.exp2(l2 - lmax)
        den = w0 + w1 + w2
        o0, o1, o2 = [_from_residues(og_refs[g], scr_ref, dil) for g, dil in enumerate(DILATIONS)]
        ob = ((w0 * o0 + w1 * o1 + w2 * o2) / den).astype(BF16)
        ob_ref[...] = ob
        lse = lmax + jnp.log2(den)
        for g, (dil, ref) in enumerate(zip(DILATIONS, (lse0_ref, lse1_ref, lse2_ref))):
            _to_residues(lse, ref, scr_ref, dil, F32)
        ya = _dot_nn(oa_ref[...], woa_ref[...])
        yb = _dot_nt(ob, wob_ref[...])
        ya_ref[...] = ya.astype(BF16)
        yb_ref[...] = yb.astype(BF16)
        u = (ga_ref[...].astype(F32) * ya + gb_ref[...].astype(F32) * yb).astype(BF16)
        u_ref[...] = u
        x2_ref[...] = x_ref[...] + _dot_nn(u, wo_ref[...])

    sd = jax.ShapeDtypeStruct
    res = list(pl.pallas_call(
        body, name="mix_out", grid=(s // tb,),
        in_specs=[_rows(tb, d), _rows(tb, QA_W)] + _dil_specs(tb) * 2 + [
            _rows(tb, d), _rows(tb, d), _resident(w_oa.shape), _resident(w_ob_t.shape), _resident(w_o.shape)],
        out_specs=[_rows(tb, d), _rows(tb, GB_W)] + _dil_specs(tb) + [_rows(tb, d), _rows(tb, d), _rows(tb, d)],
        out_shape=[sd((s, d), F32), sd((s, GB_W), BF16)] + _dil_shapes(s, F32) + [
            sd((s, d), BF16), sd((s, d), BF16), sd((s, d), BF16)],
        scratch_shapes=[pltpu.VMEM((2, tb, LANES), F32)],
        compiler_params=_cparams(("arbitrary",)))(x, oa, *og, *lg, ga, gb, w_oa, w_ob_t, w_o))
    return res[:2] + [res[2:5]] + res[5:]


def _mlp_fwd(x2, w1_t, w2, g_mlp, tb, tc):
    s, d = x2.shape
    dff = w1_t.shape[0]

    def body(x_ref, w1_ref, w2_ref, g_ref, x3_ref, r_ref, h_ref):
        xv = x_ref[...]
        hb = (xv * _rstd(xv) * g_ref[...]).astype(BF16)
        h_ref[...] = hb
        x3_ref[...] = xv
        for c in range(dff // tc):
            sl = slice(tc * c, tc * c + tc)
            r = jnp.maximum(_dot_nt(hb, w1_ref[sl, :]), 0.0)
            r_ref[:, sl] = r.astype(BF16)
            x3_ref[...] += _dot_nn((r * r).astype(BF16), w2_ref[sl, :])

    sd = jax.ShapeDtypeStruct
    return pl.pallas_call(
        body, name="mlp_fwd", grid=(s // tb,),
        in_specs=[_rows(tb, d), _resident(w1_t.shape), _resident(w2.shape), _resident(g_mlp.shape)],
        out_specs=[_rows(tb, d), _rows(tb, dff), _rows(tb, d)],
        out_shape=[sd((s, d), F32), sd((s, dff), BF16), sd((s, d), BF16)],
        compiler_params=_cparams(("arbitrary",)))(x2, w1_t, w2, g_mlp)


def _ple_loss(x3, p, target, w_pg, w_p_t, g_ple, g_fin, tb):
    s, d = x3.shape
    dp = p.shape[1]

    def body(x_ref, p_ref, t_ref, wpg_ref, wp_ref, gple_ref, gfin_ref,
             dx3_ref, h3_ref, dpre_ref, dpe_ref, pb_ref, loss_ref, dgfin_ref, dgple_ref):
        @pl.when(pl.program_id(0) == 0)
        def _():
            loss_ref[...] = jnp.zeros_like(loss_ref)
            dgfin_ref[...] = jnp.zeros_like(dgfin_ref)
            dgple_ref[...] = jnp.zeros_like(dgple_ref)

        x3v = x_ref[...]
        r3 = _rstd(x3v)
        n3 = x3v * r3
        h3 = (n3 * gple_ref[...]).astype(BF16)
        h3_ref[...] = h3
        gp = _sigmoid(_dot_nn(h3, wpg_ref[...]))
        pb = p_ref[...].astype(BF16)
        pb_ref[...] = pb
        pe = _dot_nt(pb, wp_ref[...])
        x4 = x3v + gp * pe
        r4 = _rstd(x4)
        n4 = x4 * r4
        err = n4 * gfin_ref[...] - t_ref[...]
        loss_ref[...] += jnp.sum(0.5 * jnp.mean(err * err, axis=-1, keepdims=True), axis=0, keepdims=True)
        dy = err * (1.0 / d)
        dgfin_ref[...] += _colsum(dy * n4)
        dx4 = _rms_bwd(dy, n4, r4, gfin_ref[...])
        dpe_ref[...] = (dx4 * gp).astype(BF16)
        dpre = (dx4 * pe * gp * (1.0 - gp)).astype(BF16)
        dpre_ref[...] = dpre
        dh3 = _dot_nt(dpre, wpg_ref[...])
        dgple_ref[...] += _colsum(dh3 * n3)
        dx3_ref[...] = dx4 + _rms_bwd(dh3, n3, r3, gple_ref[...])

    sd = jax.ShapeDtypeStruct
    return pl.pallas_call(
        body, name="ple_loss", grid=(s // tb,),
        in_specs=[_rows(tb, d), _rows(tb, dp), _rows(tb, d), _resident(w_pg.shape), _resident(w_p_t.shape),
                  _resident(g_ple.shape), _resident(g_fin.shape)],
        out_specs=[_rows(tb, d), _rows(tb, d), _rows(tb, d), _rows(tb, d), _rows(tb, dp),
                   _acc_spec((1, LANES)), _acc_spec((1, d)), _acc_spec((1, d))],
        out_shape=[sd((s, d), F32), sd((s, d), BF16), sd((s, d), BF16), sd((s, d), BF16), sd((s, dp), BF16),
                   sd((1, LANES), F32), sd((1, d), F32), sd((1, d), F32)],
        compiler_params=_cparams(("arbitrary",)))(x3, p, target, w_pg, w_p_t, g_ple, g_fin)


def _mlp_bwd(dx3, x2, r, w1_t, w2, g_mlp, tb, tc):
    s, d = x2.shape
    dff = w1_t.shape[0]

    def body(dx3_ref, x_ref, r_ref, w1_ref, w2_ref, g_ref, dx2_ref, df_ref, dg_ref, dh_ref):
        @pl.when(pl.program_id(0) == 0)
        def _():
            dg_ref[...] = jnp.zeros_like(dg_ref)

        dx3v = dx3_ref[...]
        dx3b = dx3v.astype(BF16)
        dh_ref[...] = jnp.zeros_like(dh_ref)
        for c in range(dff // tc):
            sl = slice(tc * c, tc * c + tc)
            df = (_dot_nt(dx3b, w2_ref[sl, :]) * (2.0 * r_ref[:, sl].astype(F32))).astype(BF16)
            df_ref[:, sl] = df
            dh_ref[...] += _dot_nn(df, w1_ref[sl, :])
        xv = x_ref[...]
        r2 = _rstd(xv)
        n2 = xv * r2
        dh = dh_ref[...]
        dg_ref[...] += _colsum(dh * n2)
        dx2_ref[...] = dx3v + _rms_bwd(dh, n2, r2, g_ref[...])

    sd = jax.ShapeDtypeStruct
    return pl.pallas_call(
        body, name="mlp_bwd", grid=(s // tb,),
        in_specs=[_rows(tb, d), _rows(tb, d), _rows(tb, dff), _resident(w1_t.shape), _resident(w2.shape),
                  _resident(g_mlp.shape)],
        out_specs=[_rows(tb, d), _rows(tb, dff), _acc_spec((1, d))],
        out_shape=[sd((s, d), F32), sd((s, dff), BF16), sd((1, d), F32)],
        scratch_shapes=[pltpu.VMEM((tb, d), F32)],
        compiler_params=_cparams(("arbitrary",)))(dx3, x2, r, w1_t, w2, g_mlp)


def _mix_out_bwd(dx2, ya, yb, ga, gb, ob, w_oa, w_ob_t, w_o, tb, after):
    s, d = dx2.shape

    def body(dx_ref, ya_ref, yb_ref, ga_ref, gb_ref, ob_ref, woa_ref, wob_ref, wo_ref, after_ref,
             doa_ref, dob0_ref, dob1_ref, dob2_ref, dd0_ref, dd1_ref, dd2_ref, dga_ref, dgb_ref, dya_ref, dyb_ref,
             dbg_ref, scr_ref):
        @pl.when(pl.program_id(0) == 0)
        def _():
            dbg_ref[...] = jnp.zeros_like(dbg_ref)

        du = _dot_nt(dx_ref[...].astype(BF16), wo_ref[...])
        gav, gbv = ga_ref[...].astype(F32), gb_ref[...].astype(F32)
        dya = (du * gav).astype(BF16)
        dyb = (du * gbv).astype(BF16)
        dya_ref[...] = dya
        dyb_ref[...] = dyb
        dga = du * ya_ref[...].astype(F32) * gav * (1.0 - gav)
        dgb = du * yb_ref[...].astype(F32) * gbv * (1.0 - gbv)
        dga_ref[...] = dga.astype(BF16)
        dgb_ref[...] = dgb.astype(BF16)
        dbg_ref[:, 0:d] += _colsum(dga)
        dbg_ref[:, d:2 * d] += _colsum(dgb)
        doa_ref[...] = _dot_nt(dya, woa_ref[...]).astype(BF16)
        dob = _dot_nn(dyb, wob_ref[...])
        dd = _seg_sum(dob * ob_ref[...].astype(F32))
        for dil, dob_ref, dd_ref in zip(DILATIONS, (dob0_ref, dob1_ref, dob2_ref), (dd0_ref, dd1_ref, dd2_ref)):
            _to_residues(dob, dob_ref, scr_ref, dil, BF16)
            _to_residues(dd, dd_ref, scr_ref, dil, F32)

    sd = jax.ShapeDtypeStruct
    res = list(pl.pallas_call(
        body, name="mix_out_bwd", grid=(s // tb,),
        in_specs=[_rows(tb, d)] * 5 + [_rows(tb, GB_W), _resident(w_oa.shape), _resident(w_ob_t.shape),
                                       _resident(w_o.shape), _ANY],
        out_specs=[_rows(tb, QA_W)] + _dil_specs(tb) * 2 + [_rows(tb, d), _rows(tb, d), _rows(tb, d),
                                                           _rows(tb, d), _acc_spec((1, 2 * d))],
        out_shape=[sd((s, QA_W), BF16)] + _dil_shapes(s, BF16) + _dil_shapes(s, F32) + [
            sd((s, d), BF16), sd((s, d), BF16), sd((s, d), BF16), sd((s, d), BF16), sd((1, 2 * d), F32)],
        scratch_shapes=[pltpu.VMEM((2, tb, LANES), F32)],
        compiler_params=_cparams(("arbitrary",)))(dx2, ya, yb, ga, gb, ob, w_oa, w_ob_t, w_o, after))
    return res[:1] + [res[1:4], res[4:7]] + res[7:]


def _in_proj_bwd(dx2, x, dqrot, dkrot, dva, qraw, kraw, tabs, dqb, dkb, dvb, dga, dgb, w_in_t, g_mix, q_g, k_g, tb):
    s, d = x.shape
    din = w_in_t.shape[0]
    q_scale = HEAD_DIM_A ** -0.5
    b_scale = HEAD_DIM_B ** -0.5
    tc = 256

    def body(dx2_ref, x_ref, dq_ref, dk_ref, dv_ref, qraw_ref, kraw_ref, c_ref, s1_ref, s2_ref, *rest):
        dqb_refs, dkb_refs, dvb_refs = rest[0:3], rest[3:6], rest[6:9]
        (dga_ref, dgb_ref, w_ref, gmix_ref, qg_ref, kg_ref,
         dx_ref, dz_ref, dgmix_ref, dqg_ref, dkg_ref, dh_ref, scr_ref) = rest[9:]

        @pl.when(pl.program_id(0) == 0)
        def _():
            dgmix_ref[...] = jnp.zeros_like(dgmix_ref)
            dqg_ref[...] = jnp.zeros_like(dqg_ref)
            dkg_ref[...] = jnp.zeros_like(dkg_ref)

        cos, s1, s2 = c_ref[...][None], s1_ref[...][None], s2_ref[...][None]

        def heads_bwd(drot, z, g_ref, acc_ref):
            dn = drot * cos + pltpu.roll(drot * s1, 96, 2) + pltpu.roll(drot * s2, 32, 2)
            rr = _rstd(z)
            nn = z * rr
            acc_ref[...] += jnp.sum(jnp.sum(dn * nn, axis=0), axis=0, keepdims=True)
            return _rms_bwd(dn, nn, rr, g_ref[...][None]).astype(BF16)

        dh_ref[...] = jnp.zeros_like(dh_ref)

        def emit(off, piece):
            dz_ref[:, off:off + tc] = piece
            dh_ref[...] += _dot_nn(piece, w_ref[off:off + tc, :])

        for j in range(d // tc):
            emit(OFF_GA + tc * j, dga_ref[:, tc * j:tc * j + tc])
            emit(OFF_GA + d + tc * j, dgb_ref[:, tc * j:tc * j + tc])
        emit(OFF_VA, dv_ref[...].astype(BF16))
        for g, dil in enumerate(DILATIONS):
            emit(OFF_QB + GB_W * g, (_from_residues(dqb_refs[g], scr_ref, dil) * b_scale).astype(BF16))
            emit(OFF_KB + GB_W * g, (_from_residues(dkb_refs[g], scr_ref, dil) * LN_2).astype(BF16))
            emit(OFF_VB + GB_W * g, _from_residues(dvb_refs[g], scr_ref, dil).astype(BF16))
        stack = lambda ref, n: jnp.stack([ref[:, 128 * h:128 * h + 128] for h in range(n)], axis=0)
        dzq = heads_bwd(stack(dq_ref, N_Q_HEADS_A) * q_scale, stack(qraw_ref, N_Q_HEADS_A), qg_ref, dqg_ref)
        dzk = heads_bwd(stack(dk_ref, N_KV_HEADS_A) * LN_2, stack(kraw_ref, N_KV_HEADS_A), kg_ref, dkg_ref)
        for j in range(N_Q_HEADS_A // 2):
            emit(OFF_QA + tc * j, jnp.concatenate([dzq[2 * j], dzq[2 * j + 1]], axis=1))
        emit(OFF_KA, jnp.concatenate([dzk[0], dzk[1]], axis=1))
        xv = x_ref[...]
        r1 = _rstd(xv)
        n1 = xv * r1
        dh = dh_ref[...]
        dgmix_ref[...] += _colsum(dh * n1)
        dx_ref[...] = dx2_ref[...] + _rms_bwd(dh, n1, r1, gmix_ref[...])

    sd = jax.ShapeDtypeStruct
    return pl.pallas_call(
        body, name="in_proj_bwd", grid=(s // tb,),
        in_specs=[_rows(tb, d), _rows(tb, d), _rows(tb, QA_W), _rows(tb, KA_W), _rows(tb, KA_W), _rows(tb, QA_W),
                  _rows(tb, KA_W), _rows(tb, LANES), _rows(tb, LANES), _rows(tb, LANES),
                  ] + _dil_specs(tb) * 3 + [_rows(tb, d), _rows(tb, d),
                  _resident(w_in_t.shape), _resident(g_mix.shape), _resident(q_g.shape), _resident(k_g.shape)],
        out_specs=[_rows(tb, d), _rows(tb, din), _acc_spec((1, d)), _acc_spec((1, HEAD_DIM_A)),
                   _acc_spec((1, HEAD_DIM_A))],
        out_shape=[sd((s, d), F32), sd((s, din), BF16), sd((1, d), F32), sd((1, HEAD_DIM_A), F32),
                   sd((1, HEAD_DIM_A), F32)],
        scratch_shapes=[pltpu.VMEM((tb, d), F32), pltpu.VMEM((2, tb, LANES), F32)],
        compiler_params=_cparams(("arbitrary",)))(
        dx2, x, dqrot, dkrot, dva, qraw, kraw, *tabs, *dqb, *dkb, *dvb, dga, dgb, w_in_t, g_mix, q_g, k_g)


def _identity(v):
    return v


def _to_bf16(v):
    return v.astype(BF16)


def _square_bf16(v):
    vf = v.astype(F32)
    return (vf * vf).astype(BF16)


def _weight_grad(name, a, b, ti, tj, tk, a_fn=_identity, b_fn=_identity, col0=0, n=None, after=None):
    t, m = a.shape
    n = b.shape[1] if n is None else n
    n_k = t // tk
    after = a if after is None else after

    def body(a_ref, b_ref, after_ref, o_ref, acc_ref):
        k = pl.program_id(2)

        @pl.when(k == 0)
        def _():
            acc_ref[...] = jnp.zeros_like(acc_ref)

        acc_ref[...] += _dot_tn(a_fn(a_ref[...]), b_fn(b_ref[...]))

        @pl.when(k == n_k - 1)
        def _():
            o_ref[...] = acc_ref[...].astype(BF16)

    return pl.pallas_call(
        body, name=name, grid=(m // ti, n // tj, n_k),
        in_specs=[pl.BlockSpec((tk, ti), lambda i, j, k: (k, i)),
                  pl.BlockSpec((tk, tj), lambda i, j, k: (k, j + col0 // tj)), _ANY],
        out_specs=pl.BlockSpec((ti, tj), lambda i, j, k: (i, j)),
        out_shape=jax.ShapeDtypeStruct((m, n), BF16),
        scratch_shapes=[pltpu.VMEM((ti, tj), F32)],
        compiler_params=_cparams(("arbitrary", "arbitrary", "arbitrary")))(a, b, after)


def _sum_slots(name, recv, partial, my_idx, transposed):
    m, n, k = recv.shape
    tc = min(k, 256)
    n_pad = -(-n // LANES) * LANES

    def body(idx_ref, own_ref, r_ref, o_ref):
        acc = own_ref[...].astype(F32)
        for i in range(m):
            acc = acc + r_ref[i].astype(F32)
        if transposed:
            if n_pad != n:
                acc = jnp.concatenate([acc, jnp.zeros((n_pad - n, tc), F32)], axis=0)
            acc = acc.T[:, :n]
        o_ref[...] = acc

    out_spec, out_shape = ((pl.BlockSpec((tc, n), lambda j, idx: (j, 0)), (k, n)) if transposed
                           else (pl.BlockSpec((n, tc), lambda j, idx: (0, j)), (n, k)))
    grid_spec = pltpu.PrefetchScalarGridSpec(
        num_scalar_prefetch=1, grid=(k // tc,),
        in_specs=[pl.BlockSpec((n, tc), lambda j, idx: (idx[0], j)),
                  pl.BlockSpec((m, n, tc), lambda j, idx: (0, 0, j))],
        out_specs=out_spec)
    return pl.pallas_call(
        body, name=name, grid_spec=grid_spec, out_shape=jax.ShapeDtypeStruct(out_shape, F32),
        compiler_params=_cparams(("arbitrary",)))(my_idx.reshape(1).astype(jnp.int32), partial, recv)


def _adamw_math(w, g, m, v):
    m = ADAM_B1 * m + (1.0 - ADAM_B1) * g
    v = ADAM_B2 * v + (1.0 - ADAM_B2) * (g * g)
    m_hat = m / (1.0 - ADAM_B1 ** ADAM_STEP)
    v_hat = v / (1.0 - ADAM_B2 ** ADAM_STEP)
    delta = -ADAM_LR * (m_hat / (jnp.sqrt(v_hat) + ADAM_EPS) + ADAM_WD * w)
    return delta, m, v


def _adamw(name, w, g, m, v):
    r, c = w.shape
    tr = max(t for t in range(8, min(r, 256) + 1, 8) if r % t == 0)

    def body(w_ref, g_ref, m_ref, v_ref, d_ref, mo_ref, vo_ref):
        d_ref[...], mo_ref[...], vo_ref[...] = _adamw_math(w_ref[...], g_ref[...], m_ref[...], v_ref[...])

    spec = pl.BlockSpec((tr, c), lambda i: (i, 0))
    return pl.pallas_call(
        body, name=name, grid=(r // tr,), in_specs=[spec] * 4, out_specs=[spec] * 3,
        out_shape=[jax.ShapeDtypeStruct((r, c), F32)] * 3,
        compiler_params=_cparams(("arbitrary",)))(w, g, m, v)


def _small_update(parts, w, m, v):
    def body(p_ref, w_ref, m_ref, v_ref, g_ref, d_ref, mo_ref, vo_ref):
        g = p_ref[0]
        for i in range(1, N_DEV):
            g = g + p_ref[i]
        g_ref[...] = g
        d_ref[...], mo_ref[...], vo_ref[...] = _adamw_math(w_ref[...], g, m_ref[...], v_ref[...])

    return pl.pallas_call(body, name="small_update", out_shape=[jax.ShapeDtypeStruct(w.shape, F32)] * 4)(
        parts, w, m, v)


def _pack_rows(vectors, n_rows):
    flat = jnp.concatenate([v.reshape(-1).astype(F32) for v in vectors])
    flat = jnp.pad(flat, (0, n_rows * LANES - flat.shape[0]))
    return flat.reshape(n_rows, LANES)


def _pick_tile(n, prefs):
    for t in prefs:
        if n % t == 0:
            return t
    return n


def kernel(x, p, norm_mix_g, w_in, b_gate, q_norm_g, k_norm_g, rel_bias, w_out_a, w_out_b, w_out, norm_mlp_g, w_ff1, w_ff2, norm_ple_g, w_ple_gate, w_ple, final_norm_g, loss_target, m_norm_mix_g, m_w_in, m_b_gate, m_q_norm_g, m_k_norm_g, m_rel_bias, m_w_out_a, m_w_out_b, m_w_out, m_norm_mlp_g, m_w_ff1, m_w_ff2, m_norm_ple_g, m_w_ple_gate, m_w_ple, m_final_norm_g, v_norm_mix_g, v_w_in, v_b_gate, v_q_norm_g, v_k_norm_g, v_rel_bias, v_w_out_a, v_w_out_b, v_w_out, v_norm_mlp_g, v_w_ff1, v_w_ff2, v_norm_ple_g, v_w_ple_gate, v_w_ple, v_final_norm_g):
    s, d = x.shape[1], x.shape[2]
    xs, ps, ts = x[0], p[0, 0], loss_target[0]
    tb = _pick_tile(s, (512, 256))
    tq = _pick_tile(s, (256,))
    tk = _pick_tile(s, (1024, 512))
    cb = _pick_tile(s, (1024, 512))
    fin_g = final_norm_g.reshape(1, d)

    col_sharded = {"w_in": w_in[0], "w_out_b": w_out_b[0], "w_ff1": w_ff1[0], "w_ple": w_ple[0]}
    row_sharded = {"w_out_a": w_out_a[0], "w_out": w_out[0], "w_ff2": w_ff2[0], "w_ple_gate": w_ple_gate[0]}
    order = ["w_in", "w_out_a", "w_out_b", "w_out", "w_ff1", "w_ff2", "w_ple_gate", "w_ple"]
    shards = [(col_sharded[n].T if n in col_sharded else row_sharded[n]).astype(BF16) for n in order]
    my_idx = 4 * lax.axis_index("x") + 2 * lax.axis_index("y") + lax.axis_index("c")
    (w_in_t,) = _all_gather(shards[:1], 1)
    zones = _place_own_rows(shards[1:], my_idx)
    ag = _copies_start("weights_gather_start", shards[1:], zones, w_in_t, True)

    tabs = _rope_tables(s)
    (h1, qraw, kraw, qrot, krot, va, qb, kb, vb, ga, gb) = _in_proj(
        xs, tabs, w_in_t, norm_mix_g, b_gate, q_norm_g, k_norm_g, tb, ag[4])
    oa, lse_a = _attn_a_fwd(qrot, krot, va, tq, tk)
    _, (w_oa, w_ob_t, w_o, w_ff1_t, w_ff2_f, w_pg, w_p_t) = _copies_wait(
        "weights_gather_wait", ag[0], ag[1], ag[2], ag[3], lse_a, True)
    flat = lambda arrs: [a.reshape(s, GB_W) for a in arrs]
    split = lambda arrs: [a.reshape(dil, s // dil, GB_W) for a, dil in zip(arrs, DILATIONS)]
    qb_r, kb_r, vb_r = flat(qb), flat(kb), flat(vb)
    bmaps = [jnp.asarray(_bucket_map(dil)) for dil in DILATIONS]
    bias_tabs = [rel_bias[:, N_HEADS_PER_DIL * g:N_HEADS_PER_DIL * (g + 1)] for g in range(3)]
    band_out = [_band_fwd(dil, qb_r[g], kb_r[g], vb_r[g], bmaps[g], bias_tabs[g], cb)
                for g, dil in enumerate(DILATIONS)]
    og, lg = split([o for o, _ in band_out]), split([l for _, l in band_out])
    x2, ob, lse_b, ya, yb, u = _mix_out(xs, oa, og, lg, ga, gb, w_oa, w_ob_t, w_o, tb)
    tc = _pick_tile(w_ff1_t.shape[0], (512,))
    x3, r_act, h2 = _mlp_fwd(x2, w_ff1_t, w_ff2_f, norm_mlp_g, tb, tc)

    dx3, h3, dpre, dpe, pb, loss_part, dg_fin, dg_ple = _ple_loss(
        x3, ps, ts, w_pg, w_p_t, norm_ple_g, fin_g, tb)
    dx2, df, dg_mlp = _mlp_bwd(dx3, x2, r_act, w_ff1_t, w_ff2_f, norm_mlp_g, tb, tc)

    tkk = _pick_tile(s, (1024, 512))
    tk2 = _pick_tile(s, (2048, 1024, 512))
    dff = w_ff1_t.shape[0]
    t1k = lambda n: _pick_tile(n, (1024, 512, 256))
    slots = lambda parts: [lax.empty((7, a.shape[0] // N_DEV, a.shape[1]), BF16) for a in parts]
    part1 = [_weight_grad("grad_w_ff1", df, h2, t1k(dff), t1k(d), tkk),
             _weight_grad("grad_w_ff2", r_act, dx3, t1k(dff), t1k(d), tkk, a_fn=_square_bf16, b_fn=_to_bf16),
             _weight_grad("grad_w_ple_gate", h3, dpre, t1k(d), t1k(d), tk2),
             _weight_grad("grad_w_ple", dpe, pb, t1k(d), ps.shape[1], tk2)]
    doa, dob, dd, dga, dgb, dya, dyb, dbg = _mix_out_bwd(dx2, ya, yb, ga, gb, ob, w_oa, w_ob_t, w_o, tb, dx2)
    part1 += [_weight_grad("grad_w_out_a", oa, dya, t1k(QA_W), t1k(d), tk2),
              _weight_grad("grad_w_out_b", dyb, ob, t1k(d), GB_W, tk2),
              _weight_grad("grad_w_out", u, dx2, t1k(d), t1k(d), tkk, b_fn=_to_bf16)]
    rs1 = _copies_start("grads1_start", part1, slots(part1), doa, False)
    dqrot, dkrot, dva = _attn_a_bwd(qrot, krot, va, oa, doa, lse_a, tq, tk, rs1[4])
    dob_r, lse_r, dd_r = flat(dob), flat(lse_b), flat(dd)
    band_bwd = [_band_bwd(dil, qb_r[g], kb_r[g], vb_r[g], dob_r[g], lse_r[g], dd_r[g], bmaps[g], bias_tabs[g], cb)
                for g, dil in enumerate(DILATIONS)]
    dqb, dkb, dvb = [split([r[j] for r in band_bwd]) for j in range(3)]
    grad_x, dz, dg_mix, dg_q, dg_k = _in_proj_bwd(
        dx2, xs, dqrot, dkrot, dva, qraw, kraw, tabs, dqb, dkb, dvb, dga, dgb, w_in_t, norm_mix_g,
        q_norm_g, k_norm_g, _pick_tile(s, (256,)))
    d_rel = jnp.concatenate([r[3][:, :N_HEADS_PER_DIL] for r in band_bwd], axis=1)

    din = w_in_t.shape[0]
    ti_in = _pick_tile(din, (din // 2,)) if (din // 2) % LANES == 0 else din
    hd_ = d // 2
    part3 = [_weight_grad("grad_w_in_lo", dz, h1, ti_in, t1k(hd_), tkk, n=hd_)]
    rs3 = _copies_start("grads3_start", part3, slots(part3), grad_x, False)
    part4 = [_weight_grad("grad_w_in_hi", dz, h1, ti_in, t1k(hd_), tkk, col0=hd_, n=hd_, after=rs3[4])]
    rs4 = _copies_start("grads4_start", part4, slots(part4), rs3[4], False)

    sums = {}
    src1, got1 = _copies_wait("grads1_wait", rs1[0], rs1[1], rs1[2], rs1[3], rs4[4], False)
    for n, a, r in zip(["w_ff1", "w_ff2", "w_ple_gate", "w_ple", "w_out_a", "w_out_b", "w_out"], src1, got1):
        sums[n] = _sum_slots("sum_" + n, r, a, my_idx, n in col_sharded)
    given_w = dict(w_in=w_in, w_out_a=w_out_a, w_out_b=w_out_b, w_out=w_out, w_ff1=w_ff1, w_ff2=w_ff2,
                   w_ple_gate=w_ple_gate, w_ple=w_ple)
    given_m = dict(w_in=m_w_in, w_out_a=m_w_out_a, w_out_b=m_w_out_b, w_out=m_w_out, w_ff1=m_w_ff1, w_ff2=m_w_ff2,
                   w_ple_gate=m_w_ple_gate, w_ple=m_w_ple)
    given_v = dict(w_in=v_w_in, w_out_a=v_w_out_a, w_out_b=v_w_out_b, w_out=v_w_out, w_ff1=v_w_ff1, w_ff2=v_w_ff2,
                   w_ple_gate=v_w_ple_gate, w_ple=v_w_ple)
    big = {}

    def update(n, transposed=False):
        view = (lambda a: a.T) if transposed else (lambda a: a)
        g = sums[n]
        delta, new_m, new_v = _adamw("adamw_" + n, view(given_w[n][0]), g, view(given_m[n][0]), view(given_v[n][0]))
        big[n] = tuple(view(a)[None] for a in (g, delta, new_m, new_v))

    for n in order[1:]:
        update(n)

    small_names = ["norm_mix_g", "b_gate", "q_norm_g", "k_norm_g", "rel_bias", "norm_mlp_g", "norm_ple_g",
                   "final_norm_g"]
    small_w = [norm_mix_g, b_gate, q_norm_g, k_norm_g, rel_bias, norm_mlp_g, norm_ple_g, final_norm_g]
    small_m = [m_norm_mix_g, m_b_gate, m_q_norm_g, m_k_norm_g, m_rel_bias, m_norm_mlp_g, m_norm_ple_g,
               m_final_norm_g]
    small_v = [v_norm_mix_g, v_b_gate, v_q_norm_g, v_k_norm_g, v_rel_bias, v_norm_mlp_g, v_norm_ple_g,
               v_final_norm_g]
    small_g = [dg_mix, dbg, dg_q, dg_k, d_rel, dg_mlp, dg_ple, dg_fin]
    sizes = [int(np.prod(w.shape)) for w in small_w]
    n_rows = -(-(sum(-(-sz // LANES) for sz in sizes) + 1) // 8) * 8
    pad = lambda v: jnp.pad(v.reshape(-1).astype(F32), (0, -v.size % LANES))
    pack = lambda vs, last: _pack_rows([pad(v) for v in vs] + [last], n_rows)
    zero_row = jnp.zeros((LANES,), F32)
    parts = _small_all_gather(pack(small_g, loss_part.reshape(-1) * (jnp.arange(LANES) == 0)), big["w_ple"][1])
    g_all, d_all, m_all, v_all = _small_update(parts, pack(small_w, zero_row), pack(small_m, zero_row),
                                               pack(small_v, zero_row))
    small = {}
    row = 0
    for n, w, sz in zip(small_names, small_w, sizes):
        nr = -(-sz // LANES)
        small[n] = tuple(a[row:row + nr].reshape(-1)[:sz].reshape(w.shape) for a in (g_all, d_all, m_all, v_all))
        row += nr
    loss = g_all[row, 0]

    src3, got3 = _copies_wait("grads3_wait", rs3[0], rs3[1], rs3[2], rs3[3], g_all, False)
    src4, got4 = _copies_wait("grads4_wait", rs4[0], rs4[1], rs4[2], rs4[3], g_all, False)
    sums["w_in"] = jnp.concatenate([_sum_slots("sum_w_in_lo", got3[0], src3[0], my_idx, False),
                                    _sum_slots("sum_w_in_hi", got4[0], src4[0], my_idx, False)], axis=1)
    update("w_in", transposed=True)

    names = ["norm_mix_g", "w_in", "b_gate", "q_norm_g", "k_norm_g", "rel_bias", "w_out_a", "w_out_b", "w_out",
             "norm_mlp_g", "w_ff1", "w_ff2", "norm_ple_g", "w_ple_gate", "w_ple", "final_norm_g"]
    res = {n: (big[n] if n in big else small[n]) for n in names}
    return (loss, grad_x[None], *[res[n][0] for n in names], *[res[n][1] for n in names],
            *[res[n][2] for n in names], *[res[n][3] for n in names])
```

```python
import math

import numpy as np
import jax
import jax.numpy as jnp
from jax import lax
from jax.experimental import pallas as pl
from jax.experimental.pallas import tpu as pltpu

F32 = jnp.float32
BF16 = jnp.bfloat16
MESH = pl.DeviceIdType.MESH

NORM_EPS = 1e-6
NEG_INF = -1e30
LOG2_E = math.log2(math.e)
LN_2 = math.log(2.0)
GRID_W = 64
ROPE_THETA = 10000.0
HEAD_DIM_A = 128
N_Q_HEADS_A = 8
N_KV_HEADS_A = 2
Q_PER_KV = N_Q_HEADS_A // N_KV_HEADS_A
HEAD_DIM_B = 64
N_HEADS_PER_DIL = 4
DILATIONS = (1, 4, 16)
BAND = 64
N_REL_BUCKETS = 32
REL_MAX_DIST = 1024
QA_W = N_Q_HEADS_A * HEAD_DIM_A
KA_W = N_KV_HEADS_A * HEAD_DIM_A
GB_W = N_HEADS_PER_DIL * HEAD_DIM_B
QB_W = GB_W * len(DILATIONS)
OFF_QA, OFF_KA, OFF_VA = 0, QA_W, QA_W + KA_W
OFF_QB = QA_W + 2 * KA_W
OFF_KB = OFF_QB + QB_W
OFF_VB = OFF_KB + QB_W
OFF_GA = OFF_VB + QB_W
N_DEV = 8
LANES = 128
VMEM_LIMIT = 56 * 2 ** 20

ADAM_LR, ADAM_B1, ADAM_B2, ADAM_EPS, ADAM_WD, ADAM_STEP = 0.001, 0.9, 0.999, 1e-08, 0.01, 10


def _cparams(sem):
    return pltpu.CompilerParams(dimension_semantics=sem, vmem_limit_bytes=VMEM_LIMIT)


def _resident(shape):
    nd = len(shape)
    return pl.BlockSpec(shape, lambda *_: (0,) * nd, pipeline_mode=pl.Buffered(1))


def _acc_spec(shape):
    nd = len(shape)
    return pl.BlockSpec(shape, lambda *_: (0,) * nd)


def _rows(tb, c):
    return pl.BlockSpec((tb, c), lambda i: (i, 0))


def _dil_shapes(s, dtype):
    return [jax.ShapeDtypeStruct((dil, s // dil, GB_W), dtype) for dil in DILATIONS]


def _dil_specs(tb):
    return [pl.BlockSpec((dil, tb // dil, GB_W), lambda i: (0, i, 0)) for dil in DILATIONS]


def _to_residues(val, out_ref, scr_ref, dil, dtype):
    if dil == 1:
        out_ref[0] = val.astype(dtype)
        return
    n = val.shape[0] // dil
    scr_ref[0] = val[:, :LANES]
    scr_ref[1] = val[:, LANES:]
    for r in range(dil):
        out_ref[r] = jnp.concatenate([scr_ref[0, pl.ds(r, n, stride=dil), :],
                                      scr_ref[1, pl.ds(r, n, stride=dil), :]], axis=1).astype(dtype)


def _from_residues(in_ref, scr_ref, dil):
    if dil == 1:
        return in_ref[0]
    n = in_ref.shape[1]
    for r in range(dil):
        v = in_ref[r]
        scr_ref[0, pl.ds(r, n, stride=dil), :] = v[:, :LANES]
        scr_ref[1, pl.ds(r, n, stride=dil), :] = v[:, LANES:]
    return jnp.concatenate([scr_ref[0], scr_ref[1]], axis=1)


def _dot_nt(a, b):
    return lax.dot_general(a, b, (((1,), (1,)), ((), ())), preferred_element_type=F32)


def _dot_nn(a, b):
    return lax.dot_general(a, b, (((1,), (0,)), ((), ())), preferred_element_type=F32)


def _dot_tn(a, b):
    return lax.dot_general(a, b, (((0,), (0,)), ((), ())), preferred_element_type=F32)


def _rstd(x):
    return lax.rsqrt(jnp.mean(x * x, axis=-1, keepdims=True) + NORM_EPS)


def _rms_bwd(dy, n, r, g):
    dn = dy * g
    return r * (dn - n * jnp.mean(dn * n, axis=-1, keepdims=True))


def _colsum(v):
    return jnp.sum(v, axis=0, keepdims=True)


def _sigmoid(v):
    return 1.0 / (1.0 + jnp.exp(-v))


def _rope_tables(s):
    half = HEAD_DIM_A // 2
    inv = np.power(np.float32(ROPE_THETA), -np.arange(0, half, 2, dtype=np.float32) / np.float32(half))
    t = np.arange(s)
    ang_r = (t // GRID_W).astype(np.float32)[:, None] * inv[None, :]
    ang_c = (t % GRID_W).astype(np.float32)[:, None] * inv[None, :]
    cr, sr, cc, sc = np.cos(ang_r), np.sin(ang_r), np.cos(ang_c), np.sin(ang_c)
    z = np.zeros_like(sr)
    cos = np.concatenate([cr, cr, cc, cc], axis=1)
    s1 = np.concatenate([z, sr, z, sc], axis=1)
    s2 = np.concatenate([-sr, z, -sc, z], axis=1)
    return [jnp.asarray(a, F32) for a in (cos, s1, s2)]


def _my_place():
    return lax.axis_index("x"), lax.axis_index("y"), lax.axis_index("c")


def _all_gather(shards, n_gather):
    n_all = len(shards)
    nw = n_gather

    def body(*refs):
        ins, outs = refs[:n_all], refs[n_all:2 * n_all]
        send_sems, recv_sems, local_sems = refs[2 * n_all:]
        x, y, c = _my_place()
        me, sibling = (x, y, c), (x, y, 1 - c)
        chips = [(1 - x, y), (x, 1 - y), (1 - x, 1 - y)]

        def rows(w, px, py, pc):
            n = ins[w].shape[0]
            return outs[w].at[pl.ds(pl.multiple_of((4 * px + 2 * py + pc) * n, 16), n), :]

        def copy(w, k, block, to, src=None):
            return pltpu.make_async_remote_copy(
                src_ref=rows(w, *block) if src is None else src, dst_ref=rows(w, *block),
                send_sem=send_sems.at[w, k], recv_sem=recv_sems.at[w, k], device_id=to, device_id_type=MESH)

        mine = [pltpu.make_async_copy(ins[w], rows(w, *me), local_sems.at[w]) for w in range(n_all)]
        for cp in mine:
            cp.start()
        first = []
        for w in range(nw):
            first.append(copy(w, 0, me, sibling, src=ins[w]))
            first += [copy(w, 1 + j, me, (*chip, c), src=ins[w]) for j, chip in enumerate(chips)]
        for cp in first:
            cp.start()
        passed = []
        for j, chip in enumerate(chips):
            for w in range(nw):
                copy(w, 1 + j, (*chip, c), me).wait_recv()
                fwd = copy(w, 4 + j, (*chip, c), sibling)
                fwd.start()
                passed.append(fwd)
        for w in range(nw):
            copy(w, 0, sibling, me).wait_recv()
        for j, chip in enumerate(chips):
            for w in range(nw):
                copy(w, 4 + j, (*chip, 1 - c), me).wait_recv()
        for cp in first + passed:
            cp.wait_send()
        for cp in mine:
            cp.wait()

    any_spec = pl.BlockSpec(memory_space=pl.ANY)
    return pl.pallas_call(
        body, name="weights_all_gather",
        out_shape=[jax.ShapeDtypeStruct((N_DEV * s.shape[0], s.shape[1]), s.dtype) for s in shards],
        in_specs=[any_spec] * n_all, out_specs=[any_spec] * n_all,
        scratch_shapes=[pltpu.SemaphoreType.DMA((nw, 7)), pltpu.SemaphoreType.DMA((nw, 7)),
                        pltpu.SemaphoreType.DMA((n_all,))],
    )(*shards)


def _place_own_rows(shards, my_idx):
    nw = len(shards)

    def body(idx_ref, *refs):
        for w in range(nw):
            refs[nw + w][...] = refs[w][...]

    grid_spec = pltpu.PrefetchScalarGridSpec(
        num_scalar_prefetch=1, grid=(1,),
        in_specs=[pl.BlockSpec(s.shape, lambda i, idx: (0, 0)) for s in shards],
        out_specs=[pl.BlockSpec(s.shape, lambda i, idx: (idx[0], 0)) for s in shards])
    return pl.pallas_call(
        body, name="place_own_rows", grid_spec=grid_spec,
        out_shape=[jax.ShapeDtypeStruct((N_DEV * s.shape[0], s.shape[1]), s.dtype) for s in shards],
        compiler_params=_cparams(("arbitrary",)))(my_idx.reshape(1).astype(jnp.int32), *shards)


_FLIPS = [(fx, fy, fc) for fx in (0, 1) for fy in (0, 1) for fc in (0, 1)][1:]


def _small_all_gather(v, after):
    def body(v_ref, after_ref, out_ref, send_sems, recv_sems):
        x, y, c = _my_place()
        my_idx = 4 * x + 2 * y + c
        out_ref[my_idx] = v_ref[...]
        sends = []
        for k, (fx, fy, fc) in enumerate(_FLIPS):
            to = (1 - x if fx else x, 1 - y if fy else y, 1 - c if fc else c)
            sends.append(pltpu.make_async_remote_copy(
                src_ref=v_ref, dst_ref=out_ref.at[my_idx], send_sem=send_sems.at[k], recv_sem=recv_sems.at[k],
                device_id=to, device_id_type=MESH))
        for cp in sends:
            cp.start()
        for k, (fx, fy, fc) in enumerate(_FLIPS):
            frm_idx = 4 * (1 - x if fx else x) + 2 * (1 - y if fy else y) + (1 - c if fc else c)
            pltpu.make_async_remote_copy(
                src_ref=v_ref, dst_ref=out_ref.at[frm_idx], send_sem=send_sems.at[k], recv_sem=recv_sems.at[k],
                device_id=(x, y, c), device_id_type=MESH).wait_recv()
        for cp in sends:
            cp.wait_send()

    vm = pl.BlockSpec(memory_space=pltpu.VMEM)
    return pl.pallas_call(
        body, name="small_all_gather", out_shape=jax.ShapeDtypeStruct((N_DEV,) + v.shape, v.dtype),
        in_specs=[vm, pl.BlockSpec(memory_space=pl.ANY)], out_specs=vm,
        scratch_shapes=[pltpu.SemaphoreType.DMA((7,)), pltpu.SemaphoreType.DMA((7,))],
    )(v, after)


_HBM = pl.BlockSpec(memory_space=pltpu.HBM)
_SEM = pl.BlockSpec(memory_space=pltpu.SEMAPHORE)
_ANY = pl.BlockSpec(memory_space=pl.ANY)
_SPLIT_COPY = dict(has_side_effects=pltpu.SideEffectType.DATAFLOW_SIDE_EFFECTING)


def _peer(x, y, c, k):
    fx, fy, fc = _FLIPS[k]
    return (1 - x if fx else x, 1 - y if fy else y, 1 - c if fc else c)


def _in_hbm(a):
    return pltpu.with_memory_space_constraint(a, pltpu.HBM)


def _split_copies(srcs, lands, send_sems, recv_sems, gather, arriving):
    x, y, c = _my_place()
    my_idx = 4 * x + 2 * y + c
    out = []
    for k in range(7):
        to = _peer(x, y, c, k)
        to_idx = 4 * to[0] + 2 * to[1] + to[2]
        for w in range(len(srcs)):
            if gather:
                n = srcs[w].shape[0]
                src = srcs[w]
                dst = lands[w].at[pl.ds(pl.multiple_of((to_idx if arriving else my_idx) * n, 16), n), :]
            else:
                n = lands[w].shape[1]
                src = srcs[w].at[pl.ds(pl.multiple_of(to_idx * n, 16), n), :]
                dst = lands[w].at[k]
            out.append(pltpu.make_async_remote_copy(
                src_ref=src, dst_ref=dst, send_sem=send_sems.at[7 * w + k], recv_sem=recv_sems.at[7 * w + k],
                device_id=to, device_id_type=MESH))
    return out


def _copies_start(name, srcs, lands, after, gather):
    nw = len(srcs)

    def body(*refs):
        send_sems, recv_sems = refs[2 * nw + 1], refs[2 * nw + 2]
        for cp in _split_copies(refs[:nw], refs[nw:2 * nw], send_sems, recv_sems, gather, False):
            cp.start()
        refs[-1][...] = jnp.zeros_like(refs[-1])

    sems = pltpu.SemaphoreType.DMA((7 * nw,))
    thru = [pltpu.HBM(a.shape, a.dtype) for a in list(srcs) + list(lands)]
    res = pl.pallas_call(
        body, name=name, out_shape=(sems, sems, *thru, jax.ShapeDtypeStruct((8, LANES), F32)),
        in_specs=[_HBM] * (2 * nw) + [_ANY], out_specs=(_SEM, _SEM, *[_HBM] * (2 * nw), pl.BlockSpec(memory_space=pltpu.VMEM)),
        input_output_aliases={i: 2 + i for i in range(2 * nw)},
        compiler_params=pltpu.CompilerParams(**_SPLIT_COPY),
    )(*[_in_hbm(a) for a in srcs], *[_in_hbm(a) for a in lands], after)
    return res[0], res[1], list(res[2:2 + nw]), list(res[2 + nw:2 + 2 * nw]), res[-1]


def _copies_wait(name, send_sems, recv_sems, srcs, lands, after, gather):
    nw = len(srcs)

    def body(*refs):
        for cp in _split_copies(refs[:nw], refs[nw:2 * nw], refs[2 * nw], refs[2 * nw + 1], gather, False):
            cp.wait_send()
        for cp in _split_copies(refs[:nw], refs[nw:2 * nw], refs[2 * nw], refs[2 * nw + 1], gather, True):
            cp.wait_recv()

    thru = [pltpu.HBM(a.shape, a.dtype) for a in list(srcs) + list(lands)]
    res = pl.pallas_call(
        body, name=name, out_shape=tuple(thru),
        in_specs=[_HBM] * (2 * nw) + [_SEM, _SEM, _ANY], out_specs=tuple([_HBM] * (2 * nw)),
        input_output_aliases={i: i for i in range(2 * nw)},
        compiler_params=pltpu.CompilerParams(**_SPLIT_COPY),
    )(*srcs, *lands, send_sems, recv_sems, after)
    return list(res[:nw]), list(res[nw:])


def _in_proj(x, tabs, w_in_t, g_mix, b_gate, q_g, k_g, tb, after):
    s, d = x.shape
    n_gate_chunks = d // 256
    q_scale = HEAD_DIM_A ** -0.5 * LOG2_E
    b_scale = HEAD_DIM_B ** -0.5 * LOG2_E

    def body(x_ref, c_ref, s1_ref, s2_ref, w_ref, gmix_ref, bg_ref, qg_ref, kg_ref, after_ref,
             h1_ref, qraw_ref, kraw_ref, qrot_ref, krot_ref, va_ref, *rest):
        qb_refs, kb_refs, vb_refs = rest[0:3], rest[3:6], rest[6:9]
        ga_ref, gb_ref, scr_ref = rest[9:]
        xv = x_ref[...]
        hb = (xv * _rstd(xv) * gmix_ref[...]).astype(BF16)
        h1_ref[...] = hb
        cos, s1, s2 = c_ref[...], s1_ref[...], s2_ref[...]

        def proj(lo, width):
            return _dot_nt(hb, w_ref[lo:lo + width, :])

        def norm_rope(z, g):
            n = z * _rstd(z) * g
            return n * cos + pltpu.roll(n, 32, 1) * s1 + pltpu.roll(n, 96, 1) * s2

        for j in range(QA_W // 256):
            z = proj(OFF_QA + 256 * j, 256)
            qraw_ref[:, 256 * j:256 * j + 256] = z
            for hh in range(2):
                lo = 256 * j + 128 * hh
                qrot_ref[:, lo:lo + 128] = (norm_rope(z[:, 128 * hh:128 * hh + 128], qg_ref[...]) * q_scale).astype(BF16)
        z = proj(OFF_KA, 256)
        kraw_ref[...] = z
        for hh in range(2):
            krot_ref[:, 128 * hh:128 * hh + 128] = norm_rope(z[:, 128 * hh:128 * hh + 128], kg_ref[...]).astype(BF16)
        va_ref[...] = proj(OFF_VA, 256).astype(BF16)
        for g, dil in enumerate(DILATIONS):
            _to_residues(proj(OFF_QB + GB_W * g, GB_W) * b_scale, qb_refs[g], scr_ref, dil, BF16)
            _to_residues(proj(OFF_KB + GB_W * g, GB_W), kb_refs[g], scr_ref, dil, BF16)
            _to_residues(proj(OFF_VB + GB_W * g, GB_W), vb_refs[g], scr_ref, dil, BF16)
        for j in range(n_gate_chunks):
            sl = slice(256 * j, 256 * j + 256)
            ga_ref[:, sl] = _sigmoid(proj(OFF_GA + 256 * j, 256) + bg_ref[:, sl]).astype(BF16)
            gb_ref[:, sl] = _sigmoid(
                proj(OFF_GA + d + 256 * j, 256) + bg_ref[:, d + 256 * j:d + 256 * j + 256]).astype(BF16)

    sd = jax.ShapeDtypeStruct
    outs = [sd((s, d), BF16), sd((s, QA_W), F32), sd((s, KA_W), F32), sd((s, QA_W), BF16), sd((s, KA_W), BF16),
            sd((s, KA_W), BF16)] + _dil_shapes(s, BF16) * 3 + [sd((s, d), BF16), sd((s, d), BF16)]
    out_specs = [_rows(tb, d), _rows(tb, QA_W), _rows(tb, KA_W), _rows(tb, QA_W), _rows(tb, KA_W), _rows(tb, KA_W)
                 ] + _dil_specs(tb) * 3 + [_rows(tb, d), _rows(tb, d)]
    in_specs = [_rows(tb, d), _rows(tb, LANES), _rows(tb, LANES), _rows(tb, LANES), _resident(w_in_t.shape),
                _resident(g_mix.shape), _resident(b_gate.shape), _resident(q_g.shape), _resident(k_g.shape), _ANY]
    res = list(pl.pallas_call(body, name="in_proj", grid=(s // tb,), in_specs=in_specs, out_specs=out_specs,
                              out_shape=outs, scratch_shapes=[pltpu.VMEM((2, tb, LANES), F32)],
                              compiler_params=_cparams(("arbitrary",)))(
        x, *tabs, w_in_t, g_mix, b_gate, q_g, k_g, after))
    return res[:6] + [res[6:9], res[9:12], res[12:15]] + res[15:]


def _attn_a_fwd(qrot, krot, va, tq, tk):
    s = qrot.shape[0]
    n_kv = s // tk
    gw = Q_PER_KV * HEAD_DIM_A

    def body(q_ref, k_ref, v_ref, o_ref, lse_ref):
        q4 = jnp.concatenate([q_ref[:, 128 * h:128 * h + 128] for h in range(Q_PER_KV)], axis=0)

        def step(j, carry):
            m, l, acc = carry
            sl = pl.ds(pl.multiple_of(j * tk, tk), tk)
            kj, vj = k_ref[sl, :], v_ref[sl, :]
            sc = _dot_nt(kj, q4)
            m_new = jnp.maximum(m, jnp.max(sc, axis=0, keepdims=True))
            p = jnp.exp2(sc - m_new)
            alpha = jnp.exp2(m - m_new)
            l = alpha * l + jnp.sum(p, axis=0, keepdims=True)
            acc = alpha * acc + _dot_tn(vj, p.astype(BF16))
            return m_new, l, acc

        rows = Q_PER_KV * tq
        m, l, acc = lax.fori_loop(0, n_kv, step, (jnp.full((1, rows), NEG_INF, F32), jnp.zeros((1, rows), F32),
                                                  jnp.zeros((HEAD_DIM_A, rows), F32)))
        o = (acc / l).T
        lse = m + jnp.log2(l)
        for h in range(Q_PER_KV):
            o_ref[:, 128 * h:128 * h + 128] = o[h * tq:(h + 1) * tq].astype(BF16)
            lse_ref[0, h:h + 1, :] = lse[:, h * tq:(h + 1) * tq]

    return pl.pallas_call(
        body, name="attn_a_fwd", grid=(N_KV_HEADS_A, s // tq),
        in_specs=[pl.BlockSpec((tq, gw), lambda g, i: (i, g)),
                  pl.BlockSpec((s, HEAD_DIM_A), lambda g, i: (0, g)),
                  pl.BlockSpec((s, HEAD_DIM_A), lambda g, i: (0, g))],
        out_specs=[pl.BlockSpec((tq, gw), lambda g, i: (i, g)),
                   pl.BlockSpec((1, Q_PER_KV, tq), lambda g, i: (g, 0, i))],
        out_shape=[jax.ShapeDtypeStruct((s, QA_W), BF16), jax.ShapeDtypeStruct((N_KV_HEADS_A, Q_PER_KV, s), F32)],
        compiler_params=_cparams(("arbitrary", "arbitrary")))(qrot, krot, va)


def _attn_a_bwd(qrot, krot, va, oa, doa, lse, tq, tk, after):
    s = qrot.shape[0]
    n_kv = s // tk
    gw = Q_PER_KV * HEAD_DIM_A

    def body(q_ref, do_ref, o_ref, lse_ref, k_ref, v_ref, after_ref, dq_ref, dk_ref, dv_ref):
        @pl.when(pl.program_id(1) == 0)
        def _():
            dk_ref[...] = jnp.zeros_like(dk_ref)
            dv_ref[...] = jnp.zeros_like(dv_ref)

        def stack(ref):
            return jnp.concatenate([ref[:, 128 * h:128 * h + 128] for h in range(Q_PER_KV)], axis=0)

        q4, do4, o4 = stack(q_ref), stack(do_ref), stack(o_ref)
        delta = jnp.sum(do4.astype(F32) * o4.astype(F32), axis=-1, keepdims=True)
        lse_cols = jnp.concatenate([lse_ref[0], jnp.zeros_like(lse_ref[0])], axis=0).T
        lse4 = jnp.concatenate([lse_cols[:, h:h + 1] for h in range(Q_PER_KV)], axis=0)

        def step(j, dq):
            sl = pl.ds(pl.multiple_of(j * tk, tk), tk)
            kj, vj = k_ref[sl, :], v_ref[sl, :]
            p = jnp.exp2(_dot_nt(q4, kj) - lse4)
            ds = (p * (_dot_nt(do4, vj) - delta)).astype(BF16)
            dk_ref[sl, :] += _dot_tn(ds, q4)
            dv_ref[sl, :] += _dot_tn(p.astype(BF16), do4)
            return dq + _dot_nn(ds, kj)

        dq = lax.fori_loop(0, n_kv, step, jnp.zeros((Q_PER_KV * tq, HEAD_DIM_A), F32))
        for h in range(Q_PER_KV):
            dq_ref[:, 128 * h:128 * h + 128] = dq[h * tq:(h + 1) * tq]

    qspec = pl.BlockSpec((tq, gw), lambda g, i: (i, g))
    kspec = pl.BlockSpec((s, HEAD_DIM_A), lambda g, i: (0, g))
    return pl.pallas_call(
        body, name="attn_a_bwd", grid=(N_KV_HEADS_A, s // tq),
        in_specs=[qspec, qspec, qspec, pl.BlockSpec((1, Q_PER_KV, tq), lambda g, i: (g, 0, i)), kspec, kspec, _ANY],
        out_specs=[qspec, kspec, kspec],
        out_shape=[jax.ShapeDtypeStruct((s, QA_W), F32), jax.ShapeDtypeStruct((s, KA_W), F32),
                   jax.ShapeDtypeStruct((s, KA_W), F32)],
        compiler_params=_cparams(("arbitrary", "arbitrary")))(qrot, doa, oa, lse, krot, va, after)


BAND_QB = 128
BAND_WIN = BAND_QB + 2 * BAND


def _band_specs(s, cb):
    per = cb // BAND
    last = s // BAND - 1
    cur = pl.BlockSpec((cb, GB_W), lambda i: (i, 0))
    prev = pl.BlockSpec((BAND, GB_W), lambda i: (jnp.maximum(i * per - 1, 0), 0))
    nxt = pl.BlockSpec((BAND, GB_W), lambda i: (jnp.minimum(i * per + per, last), 0))
    return cur, prev, nxt


def _window(prev_ref, cur_ref, next_ref):
    return jnp.concatenate([prev_ref[...], cur_ref[...], next_ref[...]], axis=0)


def _band_mask(base, seg_shift):
    rq = base + lax.broadcasted_iota(jnp.int32, (BAND_QB, BAND_WIN), 0)
    rk = base - BAND + lax.broadcasted_iota(jnp.int32, (BAND_QB, BAND_WIN), 1)
    same_segment = lax.shift_right_arithmetic(rq, jnp.int32(seg_shift)) == lax.shift_right_arithmetic(rk, jnp.int32(seg_shift))
    return (jnp.abs(rk - rq) <= BAND) & same_segment


def _build_bias(bmap_ref, tab_ref, bias_ref):
    bm = bmap_ref[...]
    acc = [jnp.full(bm.shape, NEG_INF, F32) for _ in range(N_HEADS_PER_DIL)]
    for b in range(N_REL_BUCKETS):
        hit = bm == b
        for h in range(N_HEADS_PER_DIL):
            acc[h] = jnp.where(hit, tab_ref[b, h] * LOG2_E, acc[h])
    rows = bm.shape[0]
    for h in range(N_HEADS_PER_DIL):
        bias_ref[h * rows:(h + 1) * rows, :] = acc[h]


def _segment_mask(base, seg_len, seg_shift):
    if seg_len % BAND_QB:
        return _band_mask(base, seg_shift)
    pos = lax.rem(base, seg_len)
    w = lax.broadcasted_iota(jnp.int32, (1, BAND_WIN), 1)
    return ((w >= BAND) | (pos != 0)) & ((w < BAND + BAND_QB) | (pos != seg_len - BAND_QB))


def _head_lane_masks():
    lane = lax.broadcasted_iota(jnp.int32, (1, LANES), 1)
    return [lane < HEAD_DIM_B, lane >= HEAD_DIM_B]


def _rows4(mask):
    return mask if mask.shape[0] == 1 else jnp.concatenate([mask] * N_HEADS_PER_DIL, axis=0)


def _head_scores(a, b):
    hm = _head_lane_masks()
    out = []
    for hp in range(2):
        ls = slice(LANES * hp, LANES * hp + LANES)
        ah = a[:, ls]
        both = jnp.concatenate([jnp.where(hm[0], ah, jnp.zeros_like(ah)), jnp.where(hm[1], ah, jnp.zeros_like(ah))],
                               axis=0)
        out.append(_dot_nt(both, b[:, ls]))
    return jnp.concatenate(out, axis=0)


def _head_combine(p, v, scale=None, transposed=False):
    hm = _head_lane_masks()
    rows = p.shape[0] // N_HEADS_PER_DIL
    halves = []
    for hp in range(2):
        vh = v[:, LANES * hp:LANES * hp + LANES]
        acc = None
        for hh in range(2):
            h = 2 * hp + hh
            ph = p[h * rows:(h + 1) * rows]
            vm = jnp.where(hm[hh], vh, jnp.zeros_like(vh))
            t = _dot_tn(ph, vm) if transposed else _dot_nn(ph, vm)
            if scale is not None:
                t = t * scale[h * rows:(h + 1) * rows]
            acc = t if acc is None else acc + t
        halves.append(acc)
    return jnp.concatenate(halves, axis=1)


def _head_spread(col):
    rows = col.shape[0] // N_HEADS_PER_DIL
    lane = lax.broadcasted_iota(jnp.int32, (1, GB_W), 1)
    out = jnp.zeros((rows, GB_W), F32)
    for h in range(N_HEADS_PER_DIL):
        out = jnp.where((lane >= HEAD_DIM_B * h) & (lane < HEAD_DIM_B * (h + 1)), col[h * rows:(h + 1) * rows], out)
    return out


def _head_cols(v):
    return jnp.concatenate([v[:, HEAD_DIM_B * h:HEAD_DIM_B * h + 1] for h in range(N_HEADS_PER_DIL)], axis=0)


def _seg_shift(s, dil):
    seg = s // dil
    assert seg & (seg - 1) == 0, "segment length must be a power of two"
    return seg.bit_length() - 1


def _band_fwd(dil, qb, kb, vb, bmap, tab, cb):
    s = qb.shape[0]
    shift = _seg_shift(s, dil)

    def body(q_ref, kp_ref, kc_ref, kn_ref, vp_ref, vc_ref, vn_ref, bmap_ref, tab_ref, o_ref, lse_ref, bias_ref):
        @pl.when(pl.program_id(0) == 0)
        def _():
            _build_bias(bmap_ref, tab_ref, bias_ref)

        kw, vw = _window(kp_ref, kc_ref, kn_ref), _window(vp_ref, vc_ref, vn_ref)
        for jj in range(cb // BAND_QB):
            r0 = BAND_QB * jj
            mask = _rows4(_segment_mask(pl.program_id(0) * cb + r0, s // dil, shift))
            sc = _head_scores(q_ref[r0:r0 + BAND_QB, :], kw[r0:r0 + BAND_WIN, :]) + bias_ref[...]
            sc = jnp.where(mask, sc, NEG_INF)
            m = jnp.max(sc, axis=-1, keepdims=True)
            e = jnp.exp2(sc - m)
            l = jnp.sum(e, axis=-1, keepdims=True)
            o = _head_combine(e.astype(BF16), vw[r0:r0 + BAND_WIN, :], 1.0 / l)
            o_ref[r0:r0 + BAND_QB, :] = o
            lse_ref[r0:r0 + BAND_QB, :] = _head_spread(m + jnp.log2(l))

    cur, prev, nxt = _band_specs(s, cb)
    return pl.pallas_call(
        body, name=f"band_fwd_d{dil}", grid=(s // cb,),
        in_specs=[cur, prev, cur, nxt, prev, cur, nxt, _resident(bmap.shape), pl.BlockSpec(memory_space=pltpu.SMEM)],
        out_specs=[cur, cur],
        out_shape=[jax.ShapeDtypeStruct(qb.shape, F32), jax.ShapeDtypeStruct(qb.shape, F32)],
        scratch_shapes=[pltpu.VMEM((N_HEADS_PER_DIL * BAND_QB, BAND_WIN), F32)],
        compiler_params=_cparams(("arbitrary",)))(qb, kb, kb, kb, vb, vb, vb, bmap, tab)


def _band_bwd(dil, qb, kb, vb, dob, lse, dd, bmap, tab, cb):
    s = qb.shape[0]
    shift = _seg_shift(s, dil)
    n_steps = s // cb

    def body(q_ref, do_ref, lse_ref, dd_ref, kp_ref, kc_ref, kn_ref, vp_ref, vc_ref, vn_ref, bmap_ref, tab_ref,
             dq_ref, dk_ref, dv_ref, dtab_ref, bias_ref, dsum_ref):
        @pl.when(pl.program_id(0) == 0)
        def _():
            _build_bias(bmap_ref, tab_ref, bias_ref)
            dsum_ref[...] = jnp.zeros_like(dsum_ref)
            dk_ref[...] = jnp.zeros_like(dk_ref)
            dv_ref[...] = jnp.zeros_like(dv_ref)

        kw, vw = _window(kp_ref, kc_ref, kn_ref), _window(vp_ref, vc_ref, vn_ref)
        for jj in range(cb // BAND_QB):
            r0 = BAND_QB * jj
            base = pl.program_id(0) * cb + r0
            mask = _rows4(_segment_mask(base, s // dil, shift))
            qh, doh = q_ref[r0:r0 + BAND_QB, :], do_ref[r0:r0 + BAND_QB, :]
            k3, v3 = kw[r0:r0 + BAND_WIN, :], vw[r0:r0 + BAND_WIN, :]
            sc = _head_scores(qh, k3) + bias_ref[...]
            sc = jnp.where(mask, sc, NEG_INF)
            p = jnp.exp2(sc - _head_cols(lse_ref[r0:r0 + BAND_QB, :]))
            dp = _head_scores(doh, v3)
            ds = p * (dp - _head_cols(dd_ref[r0:r0 + BAND_QB, :]))
            dsum_ref[...] += ds
            dsb = ds.astype(BF16)
            dq_ref[r0:r0 + BAND_QB, :] = _head_combine(dsb, k3)
            dk_win = _head_combine(dsb, qh, transposed=True)
            dv_win = _head_combine(p.astype(BF16), doh, transposed=True)
            own = pl.ds(pl.multiple_of(base, BAND), BAND_QB)
            dk_ref[own, :] += dk_win[BAND:BAND + BAND_QB]
            dv_ref[own, :] += dv_win[BAND:BAND + BAND_QB]

            @pl.when(base > 0)
            def _():
                before = pl.ds(pl.multiple_of(base - BAND, BAND), BAND)
                dk_ref[before, :] += dk_win[:BAND]
                dv_ref[before, :] += dv_win[:BAND]

            @pl.when(base + BAND_QB < s)
            def _():
                after = pl.ds(pl.multiple_of(base + BAND_QB, BAND), BAND)
                dk_ref[after, :] += dk_win[BAND + BAND_QB:]
                dv_ref[after, :] += dv_win[BAND + BAND_QB:]

        @pl.when(pl.program_id(0) == n_steps - 1)
        def _():
            bm = bmap_ref[...]
            lane = lax.broadcasted_iota(jnp.int32, (1, LANES), 1)
            for b in range(N_REL_BUCKETS):
                hit = bm == b
                row = jnp.zeros((1, LANES), F32)
                for h in range(N_HEADS_PER_DIL):
                    part = dsum_ref[h * BAND_QB:(h + 1) * BAND_QB, :]
                    row = jnp.where(lane == h, jnp.sum(jnp.where(hit, part, 0.0)), row)
                dtab_ref[b:b + 1, :] = row

    cur, prev, nxt = _band_specs(s, cb)
    whole = _acc_spec(qb.shape)
    return pl.pallas_call(
        body, name=f"band_bwd_d{dil}", grid=(n_steps,),
        in_specs=[cur, cur, cur, cur, prev, cur, nxt, prev, cur, nxt, _resident(bmap.shape),
                  pl.BlockSpec(memory_space=pltpu.SMEM)],
        out_specs=[cur, whole, whole, _acc_spec((N_REL_BUCKETS, LANES))],
        out_shape=[jax.ShapeDtypeStruct(qb.shape, F32)] * 3 + [jax.ShapeDtypeStruct((N_REL_BUCKETS, LANES), F32)],
        scratch_shapes=[pltpu.VMEM((N_HEADS_PER_DIL * BAND_QB, BAND_WIN), F32),
                        pltpu.VMEM((N_HEADS_PER_DIL * BAND_QB, BAND_WIN), F32)],
        compiler_params=_cparams(("arbitrary",)))(qb, dob, lse, dd, kb, kb, kb, vb, vb, vb, bmap, tab)


def _t5_bucket(rel):
    nb = N_REL_BUCKETS // 2
    ret = (rel > 0).astype(np.int32) * nb
    n = np.abs(rel)
    max_exact = nb // 2
    large = max_exact + (np.log(np.maximum(n, 1) / max_exact) / math.log(REL_MAX_DIST / max_exact)
                         * (nb - max_exact)).astype(np.int32)
    large = np.minimum(large, nb - 1)
    return ret + np.where(n < max_exact, n, large).astype(np.int32)


def _bucket_map(dil):
    off = np.arange(BAND_WIN)[None, :] - BAND - np.arange(BAND_QB)[:, None]
    return np.where(np.abs(off) <= BAND, _t5_bucket(off * dil), -1).astype(np.int32)


def _seg_sum(v):
    lane = lax.broadcasted_iota(jnp.int32, (1, v.shape[1]), 1)
    out = jnp.zeros_like(v)
    for h in range(v.shape[1] // HEAD_DIM_B):
        m = (lane >= HEAD_DIM_B * h) & (lane < HEAD_DIM_B * (h + 1))
        out = jnp.where(m, jnp.sum(jnp.where(m, v, 0.0), axis=-1, keepdims=True), out)
    return out


def _mix_out(x, oa, og, lg, ga, gb, w_oa, w_ob_t, w_o, tb):
    s, d = x.shape

    def body(x_ref, oa_ref, og0_ref, og1_ref, og2_ref, lg0_ref, lg1_ref, lg2_ref, ga_ref, gb_ref,
             woa_ref, wob_ref, wo_ref, x2_ref, ob_ref, lse0_ref, lse1_ref, lse2_ref, ya_ref, yb_ref, u_ref, scr_ref):
        og_refs, lg_refs = (og0_ref, og1_ref, og2_ref), (lg0_ref, lg1_ref, lg2_ref)
        l0, l1, l2 = [_from_residues(lg_refs[g], scr_ref, dil) for g, dil in enumerate(DILATIONS)]
        lmax = jnp.maximum(jnp.maximum(l0, l1), l2)
        w0, w1, w2 = jnp.exp2(l0 - lmax), jnp.exp2(l1 - lmax), jnp.exp2(l2 - lmax)
        den = w0 + w1 + w2
        o0, o1, o2 = [_from_residues(og_refs[g], scr_ref, dil) for g, dil in enumerate(DILATIONS)]
        ob = ((w0 * o0 + w1 * o1 + w2 * o2) / den).astype(BF16)
        ob_ref[...] = ob
        lse = lmax + jnp.log2(den)
        for g, (dil, ref) in enumerate(zip(DILATIONS, (lse0_ref, lse1_ref, lse2_ref))):
            _to_residues(lse, ref, scr_ref, dil, F32)
        ya = _dot_nn(oa_ref[...], woa_ref[...])
        yb = _dot_nt(ob, wob_ref[...])
        ya_ref[...] = ya.astype(BF16)
        yb_ref[...] = yb.astype(BF16)
        u = (ga_ref[...].astype(F32) * ya + gb_ref[...].astype(F32) * yb).astype(BF16)
        u_ref[...] = u
        x2_ref[...] = x_ref[...] + _dot_nn(u, wo_ref[...])

    sd = jax.ShapeDtypeStruct
    res = list(pl.pallas_call(
        body, name="mix_out", grid=(s // tb,),
        in_specs=[_rows(tb, d), _rows(tb, QA_W)] + _dil_specs(tb) * 2 + [
            _rows(tb, d), _rows(tb, d), _resident(w_oa.shape), _resident(w_ob_t.shape), _resident(w_o.shape)],
        out_specs=[_rows(tb, d), _rows(tb, GB_W)] + _dil_specs(tb) + [_rows(tb, d), _rows(tb, d), _rows(tb, d)],
        out_shape=[sd((s, d), F32), sd((s, GB_W), BF16)] + _dil_shapes(s, F32) + [
            sd((s, d), BF16), sd((s, d), BF16), sd((s, d), BF16)],
        scratch_shapes=[pltpu.VMEM((2, tb, LANES), F32)],
        compiler_params=_cparams(("arbitrary",)))(x, oa, *og, *lg, ga, gb, w_oa, w_ob_t, w_o))
    return res[:2] + [res[2:5]] + res[5:]


def _mlp_fwd(x2, w1_t, w2, g_mlp, tb, tc):
    s, d = x2.shape
    dff = w1_t.shape[0]

    def body(x_ref, w1_ref, w2_ref, g_ref, x3_ref, r_ref, h_ref):
        xv = x_ref[...]
        hb = (xv * _rstd(xv) * g_ref[...]).astype(BF16)
        h_ref[...] = hb
        x3_ref[...] = xv
        for c in range(dff // tc):
            sl = slice(tc * c, tc * c + tc)
            r = jnp.maximum(_dot_nt(hb, w1_ref[sl, :]), 0.0)
            r_ref[:, sl] = r.astype(BF16)
            x3_ref[...] += _dot_nn((r * r).astype(BF16), w2_ref[sl, :])

    sd = jax.ShapeDtypeStruct
    return pl.pallas_call(
        body, name="mlp_fwd", grid=(s // tb,),
        in_specs=[_rows(tb, d), _resident(w1_t.shape), _resident(w2.shape), _resident(g_mlp.shape)],
        out_specs=[_rows(tb, d), _rows(tb, dff), _rows(tb, d)],
        out_shape=[sd((s, d), F32), sd((s, dff), BF16), sd((s, d), BF16)],
        compiler_params=_cparams(("arbitrary",)))(x2, w1_t, w2, g_mlp)


def _ple_loss(x3, p, target, w_pg, w_p_t, g_ple, g_fin, tb):
    s, d = x3.shape
    dp = p.shape[1]

    def body(x_ref, p_ref, t_ref, wpg_ref, wp_ref, gple_ref, gfin_ref,
             dx3_ref, h3_ref, dpre_ref, dpe_ref, pb_ref, loss_ref, dgfin_ref, dgple_ref):
        @pl.when(pl.program_id(0) == 0)
        def _():
            loss_ref[...] = jnp.zeros_like(loss_ref)
            dgfin_ref[...] = jnp.zeros_like(dgfin_ref)
            dgple_ref[...] = jnp.zeros_like(dgple_ref)

        x3v = x_ref[...]
        r3 = _rstd(x3v)
        n3 = x3v * r3
        h3 = (n3 * gple_ref[...]).astype(BF16)
        h3_ref[...] = h3
        gp = _sigmoid(_dot_nn(h3, wpg_ref[...]))
        pb = p_ref[...].astype(BF16)
        pb_ref[...] = pb
        pe = _dot_nt(pb, wp_ref[...])
        x4 = x3v + gp * pe
        r4 = _rstd(x4)
        n4 = x4 * r4
        err = n4 * gfin_ref[...] - t_ref[...]
        loss_ref[...] += jnp.sum(0.5 * jnp.mean(err * err, axis=-1, keepdims=True), axis=0, keepdims=True)
        dy = err * (1.0 / d)
        dgfin_ref[...] += _colsum(dy * n4)
        dx4 = _rms_bwd(dy, n4, r4, gfin_ref[...])
        dpe_ref[...] = (dx4 * gp).astype(BF16)
        dpre = (dx4 * pe * gp * (1.0 - gp)).astype(BF16)
        dpre_ref[...] = dpre
        dh3 = _dot_nt(dpre, wpg_ref[...])
        dgple_ref[...] += _colsum(dh3 * n3)
        dx3_ref[...] = dx4 + _rms_bwd(dh3, n3, r3, gple_ref[...])

    sd = jax.ShapeDtypeStruct
    return pl.pallas_call(
        body, name="ple_loss", grid=(s // tb,),
        in_specs=[_rows(tb, d), _rows(tb, dp), _rows(tb, d), _resident(w_pg.shape), _resident(w_p_t.shape),
                  _resident(g_ple.shape), _resident(g_fin.shape)],
        out_specs=[_rows(tb, d), _rows(tb, d), _rows(tb, d), _rows(tb, d), _rows(tb, dp),
                   _acc_spec((1, LANES)), _acc_spec((1, d)), _acc_spec((1, d))],
        out_shape=[sd((s, d), F32), sd((s, d), BF16), sd((s, d), BF16), sd((s, d), BF16), sd((s, dp), BF16),
                   sd((1, LANES), F32), sd((1, d), F32), sd((1, d), F32)],
        compiler_params=_cparams(("arbitrary",)))(x3, p, target, w_pg, w_p_t, g_ple, g_fin)


def _mlp_bwd(dx3, x2, r, w1_t, w2, g_mlp, tb, tc):
    s, d = x2.shape
    dff = w1_t.shape[0]

    def body(dx3_ref, x_ref, r_ref, w1_ref, w2_ref, g_ref, dx2_ref, df_ref, dg_ref, dh_ref):
        @pl.when(pl.program_id(0) == 0)
        def _():
            dg_ref[...] = jnp.zeros_like(dg_ref)

        dx3v = dx3_ref[...]
        dx3b = dx3v.astype(BF16)
        dh_ref[...] = jnp.zeros_like(dh_ref)
        for c in range(dff // tc):
            sl = slice(tc * c, tc * c + tc)
            df = (_dot_nt(dx3b, w2_ref[sl, :]) * (2.0 * r_ref[:, sl].astype(F32))).astype(BF16)
            df_ref[:, sl] = df
            dh_ref[...] += _dot_nn(df, w1_ref[sl, :])
        xv = x_ref[...]
        r2 = _rstd(xv)
        n2 = xv * r2
        dh = dh_ref[...]
        dg_ref[...] += _colsum(dh * n2)
        dx2_ref[...] = dx3v + _rms_bwd(dh, n2, r2, g_ref[...])

    sd = jax.ShapeDtypeStruct
    return pl.pallas_call(
        body, name="mlp_bwd", grid=(s // tb,),
        in_specs=[_rows(tb, d), _rows(tb, d), _rows(tb, dff), _resident(w1_t.shape), _resident(w2.shape),
                  _resident(g_mlp.shape)],
        out_specs=[_rows(tb, d), _rows(tb, dff), _acc_spec((1, d))],
        out_shape=[sd((s, d), F32), sd((s, dff), BF16), sd((1, d), F32)],
        scratch_shapes=[pltpu.VMEM((tb, d), F32)],
        compiler_params=_cparams(("arbitrary",)))(dx3, x2, r, w1_t, w2, g_mlp)


def _mix_out_bwd(dx2, ya, yb, ga, gb, ob, w_oa, w_ob_t, w_o, tb, after):
    s, d = dx2.shape

    def body(dx_ref, ya_ref, yb_ref, ga_ref, gb_ref, ob_ref, woa_ref, wob_ref, wo_ref, after_ref,
             doa_ref, dob0_ref, dob1_ref, dob2_ref, dd0_ref, dd1_ref, dd2_ref, dga_ref, dgb_ref, dya_ref, dyb_ref,
             dbg_ref, scr_ref):
        @pl.when(pl.program_id(0) == 0)
        def _():
            dbg_ref[...] = jnp.zeros_like(dbg_ref)

        du = _dot_nt(dx_ref[...].astype(BF16), wo_ref[...])
        gav, gbv = ga_ref[...].astype(F32), gb_ref[...].astype(F32)
        dya = (du * gav).astype(BF16)
        dyb = (du * gbv).astype(BF16)
        dya_ref[...] = dya
        dyb_ref[...] = dyb
        dga = du * ya_ref[...].astype(F32) * gav * (1.0 - gav)
        dgb = du * yb_ref[...].astype(F32) * gbv * (1.0 - gbv)
        dga_ref[...] = dga.astype(BF16)
        dgb_ref[...] = dgb.astype(BF16)
        dbg_ref[:, 0:d] += _colsum(dga)
        dbg_ref[:, d:2 * d] += _colsum(dgb)
        doa_ref[...] = _dot_nt(dya, woa_ref[...]).astype(BF16)
        dob = _dot_nn(dyb, wob_ref[...])
        dd = _seg_sum(dob * ob_ref[...].astype(F32))
        for dil, dob_ref, dd_ref in zip(DILATIONS, (dob0_ref, dob1_ref, dob2_ref), (dd0_ref, dd1_ref, dd2_ref)):
            _to_residues(dob, dob_ref, scr_ref, dil, BF16)
            _to_residues(dd, dd_ref, scr_ref, dil, F32)

    sd = jax.ShapeDtypeStruct
    res = list(pl.pallas_call(
        body, name="mix_out_bwd", grid=(s // tb,),
        in_specs=[_rows(tb, d)] * 5 + [_rows(tb, GB_W), _resident(w_oa.shape), _resident(w_ob_t.shape),
                                       _resident(w_o.shape), _ANY],
        out_specs=[_rows(tb, QA_W)] + _dil_specs(tb) * 2 + [_rows(tb, d), _rows(tb, d), _rows(tb, d),
                                                           _rows(tb, d), _acc_spec((1, 2 * d))],
        out_shape=[sd((s, QA_W), BF16)] + _dil_shapes(s, BF16) + _dil_shapes(s, F32) + [
            sd((s, d), BF16), sd((s, d), BF16), sd((s, d), BF16), sd((s, d), BF16), sd((1, 2 * d), F32)],
        scratch_shapes=[pltpu.VMEM((2, tb, LANES), F32)],
        compiler_params=_cparams(("arbitrary",)))(dx2, ya, yb, ga, gb, ob, w_oa, w_ob_t, w_o, after))
    return res[:1] + [res[1:4], res[4:7]] + res[7:]


def _in_proj_bwd(dx2, x, dqrot, dkrot, dva, qraw, kraw, tabs, dqb, dkb, dvb, dga, dgb, w_in_t, g_mix, q_g, k_g, tb):
    s, d = x.shape
    din = w_in_t.shape[0]
    q_scale = HEAD_DIM_A ** -0.5
    b_scale = HEAD_DIM_B ** -0.5
    tc = 256

    def body(dx2_ref, x_ref, dq_ref, dk_ref, dv_ref, qraw_ref, kraw_ref, c_ref, s1_ref, s2_ref, *rest):
        dqb_refs, dkb_refs, dvb_refs = rest[0:3], rest[3:6], rest[6:9]
        (dga_ref, dgb_ref, w_ref, gmix_ref, qg_ref, kg_ref,
         dx_ref, dz_ref, dgmix_ref, dqg_ref, dkg_ref, dh_ref, scr_ref) = rest[9:]

        @pl.when(pl.program_id(0) == 0)
        def _():
            dgmix_ref[...] = jnp.zeros_like(dgmix_ref)
            dqg_ref[...] = jnp.zeros_like(dqg_ref)
            dkg_ref[...] = jnp.zeros_like(dkg_ref)

        cos, s1, s2 = c_ref[...][None], s1_ref[...][None], s2_ref[...][None]

        def heads_bwd(drot, z, g_ref, acc_ref):
            dn = drot * cos + pltpu.roll(drot * s1, 96, 2) + pltpu.roll(drot * s2, 32, 2)
            rr = _rstd(z)
            nn = z * rr
            acc_ref[...] += jnp.sum(jnp.sum(dn * nn, axis=0), axis=0, keepdims=True)
            return _rms_bwd(dn, nn, rr, g_ref[...][None]).astype(BF16)

        dh_ref[...] = jnp.zeros_like(dh_ref)

        def emit(off, piece):
            dz_ref[:, off:off + tc] = piece
            dh_ref[...] += _dot_nn(piece, w_ref[off:off + tc, :])

        for j in range(d // tc):
            emit(OFF_GA + tc * j, dga_ref[:, tc * j:tc * j + tc])
            emit(OFF_GA + d + tc * j, dgb_ref[:, tc * j:tc * j + tc])
        emit(OFF_VA, dv_ref[...].astype(BF16))
        for g, dil in enumerate(DILATIONS):
            emit(OFF_QB + GB_W * g, (_from_residues(dqb_refs[g], scr_ref, dil) * b_scale).astype(BF16))
            emit(OFF_KB + GB_W * g, (_from_residues(dkb_refs[g], scr_ref, dil) * LN_2).astype(BF16))
            emit(OFF_VB + GB_W * g, _from_residues(dvb_refs[g], scr_ref, dil).astype(BF16))
        stack = lambda ref, n: jnp.stack([ref[:, 128 * h:128 * h + 128] for h in range(n)], axis=0)
        dzq = heads_bwd(stack(dq_ref, N_Q_HEADS_A) * q_scale, stack(qraw_ref, N_Q_HEADS_A), qg_ref, dqg_ref)
        dzk = heads_bwd(stack(dk_ref, N_KV_HEADS_A) * LN_2, stack(kraw_ref, N_KV_HEADS_A), kg_ref, dkg_ref)
        for j in range(N_Q_HEADS_A // 2):
            emit(OFF_QA + tc * j, jnp.concatenate([dzq[2 * j], dzq[2 * j + 1]], axis=1))
        emit(OFF_KA, jnp.concatenate([dzk[0], dzk[1]], axis=1))
        xv = x_ref[...]
        r1 = _rstd(xv)
        n1 = xv * r1
        dh = dh_ref[...]
        dgmix_ref[...] += _colsum(dh * n1)
        dx_ref[...] = dx2_ref[...] + _rms_bwd(dh, n1, r1, gmix_ref[...])

    sd = jax.ShapeDtypeStruct
    return pl.pallas_call(
        body, name="in_proj_bwd", grid=(s // tb,),
        in_specs=[_rows(tb, d), _rows(tb, d), _rows(tb, QA_W), _rows(tb, KA_W), _rows(tb, KA_W), _rows(tb, QA_W),
                  _rows(tb, KA_W), _rows(tb, LANES), _rows(tb, LANES), _rows(tb, LANES),
                  ] + _dil_specs(tb) * 3 + [_rows(tb, d), _rows(tb, d),
                  _resident(w_in_t.shape), _resident(g_mix.shape), _resident(q_g.shape), _resident(k_g.shape)],
        out_specs=[_rows(tb, d), _rows(tb, din), _acc_spec((1, d)), _acc_spec((1, HEAD_DIM_A)),
                   _acc_spec((1, HEAD_DIM_A))],
        out_shape=[sd((s, d), F32), sd((s, din), BF16), sd((1, d), F32), sd((1, HEAD_DIM_A), F32),
                   sd((1, HEAD_DIM_A), F32)],
        scratch_shapes=[pltpu.VMEM((tb, d), F32), pltpu.VMEM((2, tb, LANES), F32)],
        compiler_params=_cparams(("arbitrary",)))(
        dx2, x, dqrot, dkrot, dva, qraw, kraw, *tabs, *dqb, *dkb, *dvb, dga, dgb, w_in_t, g_mix, q_g, k_g)


def _identity(v):
    return v


def _to_bf16(v):
    return v.astype(BF16)


def _square_bf16(v):
    vf = v.astype(F32)
    return (vf * vf).astype(BF16)


def _weight_grad(name, a, b, ti, tj, tk, a_fn=_identity, b_fn=_identity, col0=0, n=None, after=None):
    t, m = a.shape
    n = b.shape[1] if n is None else n
    n_k = t // tk
    after = a if after is None else after

    def body(a_ref, b_ref, after_ref, o_ref, acc_ref):
        k = pl.program_id(2)

        @pl.when(k == 0)
        def _():
            acc_ref[...] = jnp.zeros_like(acc_ref)

        acc_ref[...] += _dot_tn(a_fn(a_ref[...]), b_fn(b_ref[...]))

        @pl.when(k == n_k - 1)
        def _():
            o_ref[...] = acc_ref[...].astype(BF16)

    return pl.pallas_call(
        body, name=name, grid=(m // ti, n // tj, n_k),
        in_specs=[pl.BlockSpec((tk, ti), lambda i, j, k: (k, i)),
                  pl.BlockSpec((tk, tj), lambda i, j, k: (k, j + col0 // tj)), _ANY],
        out_specs=pl.BlockSpec((ti, tj), lambda i, j, k: (i, j)),
        out_shape=jax.ShapeDtypeStruct((m, n), BF16),
        scratch_shapes=[pltpu.VMEM((ti, tj), F32)],
        compiler_params=_cparams(("arbitrary", "arbitrary", "arbitrary")))(a, b, after)


def _sum_slots(name, recv, partial, my_idx, transposed):
    m, n, k = recv.shape
    tc = min(k, 256)
    n_pad = -(-n // LANES) * LANES

    def body(idx_ref, own_ref, r_ref, o_ref):
        acc = own_ref[...].astype(F32)
        for i in range(m):
            acc = acc + r_ref[i].astype(F32)
        if transposed:
            if n_pad != n:
                acc = jnp.concatenate([acc, jnp.zeros((n_pad - n, tc), F32)], axis=0)
            acc = acc.T[:, :n]
        o_ref[...] = acc

    out_spec, out_shape = ((pl.BlockSpec((tc, n), lambda j, idx: (j, 0)), (k, n)) if transposed
                           else (pl.BlockSpec((n, tc), lambda j, idx: (0, j)), (n, k)))
    grid_spec = pltpu.PrefetchScalarGridSpec(
        num_scalar_prefetch=1, grid=(k // tc,),
        in_specs=[pl.BlockSpec((n, tc), lambda j, idx: (idx[0], j)),
                  pl.BlockSpec((m, n, tc), lambda j, idx: (0, 0, j))],
        out_specs=out_spec)
    return pl.pallas_call(
        body, name=name, grid_spec=grid_spec, out_shape=jax.ShapeDtypeStruct(out_shape, F32),
        compiler_params=_cparams(("arbitrary",)))(my_idx.reshape(1).astype(jnp.int32), partial, recv)


def _adamw_math(w, g, m, v):
    m = ADAM_B1 * m + (1.0 - ADAM_B1) * g
    v = ADAM_B2 * v + (1.0 - ADAM_B2) * (g * g)
    m_hat = m / (1.0 - ADAM_B1 ** ADAM_STEP)
    v_hat = v / (1.0 - ADAM_B2 ** ADAM_STEP)
    delta = -ADAM_LR * (m_hat / (jnp.sqrt(v_hat) + ADAM_EPS) + ADAM_WD * w)
    return delta, m, v


def _adamw(name, w, g, m, v):
    r, c = w.shape
    tr = max(t for t in range(8, min(r, 256) + 1, 8) if r % t == 0)

    def body(w_ref, g_ref, m_ref, v_ref, d_ref, mo_ref, vo_ref):
        d_ref[...], mo_ref[...], vo_ref[...] = _adamw_math(w_ref[...], g_ref[...], m_ref[...], v_ref[...])

    spec = pl.BlockSpec((tr, c), lambda i: (i, 0))
    return pl.pallas_call(
        body, name=name, grid=(r // tr,), in_specs=[spec] * 4, out_specs=[spec] * 3,
        out_shape=[jax.ShapeDtypeStruct((r, c), F32)] * 3,
        compiler_params=_cparams(("arbitrary",)))(w, g, m, v)


def _small_update(parts, w, m, v):
    def body(p_ref, w_ref, m_ref, v_ref, g_ref, d_ref, mo_ref, vo_ref):
        g = p_ref[0]
        for i in range(1, N_DEV):
            g = g + p_ref[i]
        g_ref[...] = g
        d_ref[...], mo_ref[...], vo_ref[...] = _adamw_math(w_ref[...], g, m_ref[...], v_ref[...])

    return pl.pallas_call(body, name="small_update", out_shape=[jax.ShapeDtypeStruct(w.shape, F32)] * 4)(
        parts, w, m, v)


def _pack_rows(vectors, n_rows):
    flat = jnp.concatenate([v.reshape(-1).astype(F32) for v in vectors])
    flat = jnp.pad(flat, (0, n_rows * LANES - flat.shape[0]))
    return flat.reshape(n_rows, LANES)


def _pick_tile(n, prefs):
    for t in prefs:
        if n % t == 0:
            return t
    return n


def kernel(x, p, norm_mix_g, w_in, b_gate, q_norm_g, k_norm_g, rel_bias, w_out_a, w_out_b, w_out, norm_mlp_g, w_ff1, w_ff2, norm_ple_g, w_ple_gate, w_ple, final_norm_g, loss_target, m_norm_mix_g, m_w_in, m_b_gate, m_q_norm_g, m_k_norm_g, m_rel_bias, m_w_out_a, m_w_out_b, m_w_out, m_norm_mlp_g, m_w_ff1, m_w_ff2, m_norm_ple_g, m_w_ple_gate, m_w_ple, m_final_norm_g, v_norm_mix_g, v_w_in, v_b_gate, v_q_norm_g, v_k_norm_g, v_rel_bias, v_w_out_a, v_w_out_b, v_w_out, v_norm_mlp_g, v_w_ff1, v_w_ff2, v_norm_ple_g, v_w_ple_gate, v_w_ple, v_final_norm_g):
    s, d = x.shape[1], x.shape[2]
    xs, ps, ts = x[0], p[0, 0], loss_target[0]
    tb = _pick_tile(s, (512, 256))
    tq = _pick_tile(s, (256,))
    tk = _pick_tile(s, (1024, 512))
    cb = _pick_tile(s, (1024, 512))
    fin_g = final_norm_g.reshape(1, d)

    col_sharded = {"w_in": w_in[0], "w_out_b": w_out_b[0], "w_ff1": w_ff1[0], "w_ple": w_ple[0]}
    row_sharded = {"w_out_a": w_out_a[0], "w_out": w_out[0], "w_ff2": w_ff2[0], "w_ple_gate": w_ple_gate[0]}
    order = ["w_in", "w_out_a", "w_out_b", "w_out", "w_ff1", "w_ff2", "w_ple_gate", "w_ple"]
    shards = [(col_sharded[n].T if n in col_sharded else row_sharded[n]).astype(BF16) for n in order]
    my_idx = 4 * lax.axis_index("x") + 2 * lax.axis_index("y") + lax.axis_index("c")
    (w_in_t,) = _all_gather(shards[:1], 1)
    zones = _place_own_rows(shards[1:], my_idx)
    ag = _copies_start("weights_gather_start", shards[1:], zones, w_in_t, True)

    tabs = _rope_tables(s)
    (h1, qraw, kraw, qrot, krot, va, qb, kb, vb, ga, gb) = _in_proj(
        xs, tabs, w_in_t, norm_mix_g, b_gate, q_norm_g, k_norm_g, tb, ag[4])
    oa, lse_a = _attn_a_fwd(qrot, krot, va, tq, tk)
    _, (w_oa, w_ob_t, w_o, w_ff1_t, w_ff2_f, w_pg, w_p_t) = _copies_wait(
        "weights_gather_wait", ag[0], ag[1], ag[2], ag[3], lse_a, True)
    flat = lambda arrs: [a.reshape(s, GB_W) for a in arrs]
    split = lambda arrs: [a.reshape(dil, s // dil, GB_W) for a, dil in zip(arrs, DILATIONS)]
    qb_r, kb_r, vb_r = flat(qb), flat(kb), flat(vb)
    bmaps = [jnp.asarray(_bucket_map(dil)) for dil in DILATIONS]
    bias_tabs = [rel_bias[:, N_HEADS_PER_DIL * g:N_HEADS_PER_DIL * (g + 1)] for g in range(3)]
    band_out = [_band_fwd(dil, qb_r[g], kb_r[g], vb_r[g], bmaps[g], bias_tabs[g], cb)
                for g, dil in enumerate(DILATIONS)]
    og, lg = split([o for o, _ in band_out]), split([l for _, l in band_out])
    x2, ob, lse_b, ya, yb, u = _mix_out(xs, oa, og, lg, ga, gb, w_oa, w_ob_t, w_o, tb)
    tc = _pick_tile(w_ff1_t.shape[0], (512,))
    x3, r_act, h2 = _mlp_fwd(x2, w_ff1_t, w_ff2_f, norm_mlp_g, tb, tc)

    dx3, h3, dpre, dpe, pb, loss_part, dg_fin, dg_ple = _ple_loss(
        x3, ps, ts, w_pg, w_p_t, norm_ple_g, fin_g, tb)
    dx2, df, dg_mlp = _mlp_bwd(dx3, x2, r_act, w_ff1_t, w_ff2_f, norm_mlp_g, tb, tc)

    tkk = _pick_tile(s, (1024, 512))
    tk2 = _pick_tile(s, (2048, 1024, 512))
    dff = w_ff1_t.shape[0]
    t1k = lambda n: _pick_tile(n, (1024, 512, 256))
    slots = lambda parts: [lax.empty((7, a.shape[0] // N_DEV, a.shape[1]), BF16) for a in parts]
    part1 = [_weight_grad("grad_w_ff1", df, h2, t1k(dff), t1k(d), tkk),
             _weight_grad("grad_w_ff2", r_act, dx3, t1k(dff), t1k(d), tkk, a_fn=_square_bf16, b_fn=_to_bf16),
             _weight_grad("grad_w_ple_gate", h3, dpre, t1k(d), t1k(d), tk2),
             _weight_grad("grad_w_ple", dpe, pb, t1k(d), ps.shape[1], tk2)]
    doa, dob, dd, dga, dgb, dya, dyb, dbg = _mix_out_bwd(dx2, ya, yb, ga, gb, ob, w_oa, w_ob_t, w_o, tb, dx2)
    part1 += [_weight_grad("grad_w_out_a", oa, dya, t1k(QA_W), t1k(d), tk2),
              _weight_grad("grad_w_out_b", dyb, ob, t1k(d), GB_W, tk2),
              _weight_grad("grad_w_out", u, dx2, t1k(d), t1k(d), tkk, b_fn=_to_bf16)]
    rs1 = _copies_start("grads1_start", part1, slots(part1), doa, False)
    dqrot, dkrot, dva = _attn_a_bwd(qrot, krot, va, oa, doa, lse_a, tq, tk, rs1[4])
    dob_r, lse_r, dd_r = flat(dob), flat(lse_b), flat(dd)
    band_bwd = [_band_bwd(dil, qb_r[g], kb_r[g], vb_r[g], dob_r[g], lse_r[g], dd_r[g], bmaps[g], bias_tabs[g], cb)
                for g, dil in enumerate(DILATIONS)]
    dqb, dkb, dvb = [split([r[j] for r in band_bwd]) for j in range(3)]
    grad_x, dz, dg_mix, dg_q, dg_k = _in_proj_bwd(
        dx2, xs, dqrot, dkrot, dva, qraw, kraw, tabs, dqb, dkb, dvb, dga, dgb, w_in_t, norm_mix_g,
        q_norm_g, k_norm_g, _pick_tile(s, (256,)))
    d_rel = jnp.concatenate([r[3][:, :N_HEADS_PER_DIL] for r in band_bwd], axis=1)

    din = w_in_t.shape[0]
    ti_in = _pick_tile(din, (din // 2,)) if (din // 2) % LANES == 0 else din
    n_pieces = next(n for n in (4, 2, 1) if (d // n) % LANES == 0)
    wc = d // n_pieces
    rs_in, token = [], grad_x
    for j in range(n_pieces):
        piece = [_weight_grad(f"grad_w_in_{j}", dz, h1, ti_in, t1k(wc), tkk, col0=wc * j, n=wc, after=token)]
        rs_in.append(_copies_start(f"grads_in{j}_start", piece, slots(piece), token, False))
        token = rs_in[-1][4]

    sums = {}
    src1, got1 = _copies_wait("grads1_wait", rs1[0], rs1[1], rs1[2], rs1[3], token, False)
    for n, a, r in zip(["w_ff1", "w_ff2", "w_ple_gate", "w_ple", "w_out_a", "w_out_b", "w_out"], src1, got1):
        sums[n] = _sum_slots("sum_" + n, r, a, my_idx, n in col_sharded)
    given_w = dict(w_in=w_in, w_out_a=w_out_a, w_out_b=w_out_b, w_out=w_out, w_ff1=w_ff1, w_ff2=w_ff2,
                   w_ple_gate=w_ple_gate, w_ple=w_ple)
    given_m = dict(w_in=m_w_in, w_out_a=m_w_out_a, w_out_b=m_w_out_b, w_out=m_w_out, w_ff1=m_w_ff1, w_ff2=m_w_ff2,
                   w_ple_gate=m_w_ple_gate, w_ple=m_w_ple)
    given_v = dict(w_in=v_w_in, w_out_a=v_w_out_a, w_out_b=v_w_out_b, w_out=v_w_out, w_ff1=v_w_ff1, w_ff2=v_w_ff2,
                   w_ple_gate=v_w_ple_gate, w_ple=v_w_ple)
    big = {}

    def update(n, transposed=False):
        view = (lambda a: a.T) if transposed else (lambda a: a)
        g = sums[n]
        delta, new_m, new_v = _adamw("adamw_" + n, view(given_w[n][0]), g, view(given_m[n][0]), view(given_v[n][0]))
        big[n] = tuple(view(a)[None] for a in (g, delta, new_m, new_v))

    for n in order[1:]:
        update(n)

    small_names = ["norm_mix_g", "b_gate", "q_norm_g", "k_norm_g", "rel_bias", "norm_mlp_g", "norm_ple_g",
                   "final_norm_g"]
    small_w = [norm_mix_g, b_gate, q_norm_g, k_norm_g, rel_bias, norm_mlp_g, norm_ple_g, final_norm_g]
    small_m = [m_norm_mix_g, m_b_gate, m_q_norm_g, m_k_norm_g, m_rel_bias, m_norm_mlp_g, m_norm_ple_g,
               m_final_norm_g]
    small_v = [v_norm_mix_g, v_b_gate, v_q_norm_g, v_k_norm_g, v_rel_bias, v_norm_mlp_g, v_norm_ple_g,
               v_final_norm_g]
    small_g = [dg_mix, dbg, dg_q, dg_k, d_rel, dg_mlp, dg_ple, dg_fin]
    sizes = [int(np.prod(w.shape)) for w in small_w]
    n_rows = -(-(sum(-(-sz // LANES) for sz in sizes) + 1) // 8) * 8
    pad = lambda v: jnp.pad(v.reshape(-1).astype(F32), (0, -v.size % LANES))
    pack = lambda vs, last: _pack_rows([pad(v) for v in vs] + [last], n_rows)
    zero_row = jnp.zeros((LANES,), F32)
    parts = _small_all_gather(pack(small_g, loss_part.reshape(-1) * (jnp.arange(LANES) == 0)), big["w_ple"][1])
    g_all, d_all, m_all, v_all = _small_update(parts, pack(small_w, zero_row), pack(small_m, zero_row),
                                               pack(small_v, zero_row))
    small = {}
    row = 0
    for n, w, sz in zip(small_names, small_w, sizes):
        nr = -(-sz // LANES)
        small[n] = tuple(a[row:row + nr].reshape(-1)[:sz].reshape(w.shape) for a in (g_all, d_all, m_all, v_all))
        row += nr
    loss = g_all[row, 0]

    pieces = []
    for j, rs in enumerate(rs_in):
        src, got = _copies_wait(f"grads_in{j}_wait", rs[0], rs[1], rs[2], rs[3], g_all, False)
        pieces.append(_sum_slots(f"sum_w_in_{j}", got[0], src[0], my_idx, False))
    sums["w_in"] = jnp.concatenate(pieces, axis=1)
    update("w_in", transposed=True)

    names = ["norm_mix_g", "w_in", "b_gate", "q_norm_g", "k_norm_g", "rel_bias", "w_out_a", "w_out_b", "w_out",
             "norm_mlp_g", "w_ff1", "w_ff2", "norm_ple_g", "w_ple_gate", "w_ple", "final_norm_g"]
    res = {n: (big[n] if n in big else small[n]) for n in names}
    return (loss, grad_x[None], *[res[n][0] for n in names], *[res[n][1] for n in names],
            *[res[n][2] for n in names], *[res[n][3] for n in names])
```

```python
import math

import numpy as np
import jax
import jax.numpy as jnp
from jax import lax
from jax.experimental import pallas as pl
from jax.experimental.pallas import tpu as pltpu

F32 = jnp.float32
BF16 = jnp.bfloat16
MESH = pl.DeviceIdType.MESH

NORM_EPS = 1e-6
NEG_INF = -1e30
LOG2_E = math.log2(math.e)
LN_2 = math.log(2.0)
GRID_W = 64
ROPE_THETA = 10000.0
HEAD_DIM_A = 128
N_Q_HEADS_A = 8
N_KV_HEADS_A = 2
Q_PER_KV = N_Q_HEADS_A // N_KV_HEADS_A
HEAD_DIM_B = 64
N_HEADS_PER_DIL = 4
DILATIONS = (1, 4, 16)
BAND = 64
N_REL_BUCKETS = 32
REL_MAX_DIST = 1024
QA_W = N_Q_HEADS_A * HEAD_DIM_A
KA_W = N_KV_HEADS_A * HEAD_DIM_A
GB_W = N_HEADS_PER_DIL * HEAD_DIM_B
QB_W = GB_W * len(DILATIONS)
OFF_QA, OFF_KA, OFF_VA = 0, QA_W, QA_W + KA_W
OFF_QB = QA_W + 2 * KA_W
OFF_KB = OFF_QB + QB_W
OFF_VB = OFF_KB + QB_W
OFF_GA = OFF_VB + QB_W
N_DEV = 8
LANES = 128
VMEM_LIMIT = 56 * 2 ** 20

ADAM_LR, ADAM_B1, ADAM_B2, ADAM_EPS, ADAM_WD, ADAM_STEP = 0.001, 0.9, 0.999, 1e-08, 0.01, 10


def _cparams(sem):
    return pltpu.CompilerParams(dimension_semantics=sem, vmem_limit_bytes=VMEM_LIMIT)


def _resident(shape):
    nd = len(shape)
    return pl.BlockSpec(shape, lambda *_: (0,) * nd, pipeline_mode=pl.Buffered(1))


def _acc_spec(shape):
    nd = len(shape)
    return pl.BlockSpec(shape, lambda *_: (0,) * nd)


def _rows(tb, c):
    return pl.BlockSpec((tb, c), lambda i: (i, 0))


def _dil_shapes(s, dtype):
    return [jax.ShapeDtypeStruct((dil, s // dil, GB_W), dtype) for dil in DILATIONS]


def _dil_specs(tb):
    return [pl.BlockSpec((dil, tb // dil, GB_W), lambda i: (0, i, 0)) for dil in DILATIONS]


def _to_residues(val, out_ref, scr_ref, dil, dtype):
    if dil == 1:
        out_ref[0] = val.astype(dtype)
        return
    n = val.shape[0] // dil
    scr_ref[0] = val[:, :LANES]
    scr_ref[1] = val[:, LANES:]
    for r in range(dil):
        out_ref[r] = jnp.concatenate([scr_ref[0, pl.ds(r, n, stride=dil), :],
                                      scr_ref[1, pl.ds(r, n, stride=dil), :]], axis=1).astype(dtype)


def _from_residues(in_ref, scr_ref, dil):
    if dil == 1:
        return in_ref[0]
    n = in_ref.shape[1]
    for r in range(dil):
        v = in_ref[r]
        scr_ref[0, pl.ds(r, n, stride=dil), :] = v[:, :LANES]
        scr_ref[1, pl.ds(r, n, stride=dil), :] = v[:, LANES:]
    return jnp.concatenate([scr_ref[0], scr_ref[1]], axis=1)


def _dot_nt(a, b):
    return lax.dot_general(a, b, (((1,), (1,)), ((), ())), preferred_element_type=F32)


def _dot_nn(a, b):
    return lax.dot_general(a, b, (((1,), (0,)), ((), ())), preferred_element_type=F32)


def _dot_tn(a, b):
    return lax.dot_general(a, b, (((0,), (0,)), ((), ())), preferred_element_type=F32)


def _rstd(x):
    return lax.rsqrt(jnp.mean(x * x, axis=-1, keepdims=True) + NORM_EPS)


def _rms_bwd(dy, n, r, g):
    dn = dy * g
    return r * (dn - n * jnp.mean(dn * n, axis=-1, keepdims=True))


def _colsum(v):
    return jnp.sum(v, axis=0, keepdims=True)


def _sigmoid(v):
    return 1.0 / (1.0 + jnp.exp(-v))


def _rope_tables(s):
    half = HEAD_DIM_A // 2
    inv = np.power(np.float32(ROPE_THETA), -np.arange(0, half, 2, dtype=np.float32) / np.float32(half))
    t = np.arange(s)
    ang_r = (t // GRID_W).astype(np.float32)[:, None] * inv[None, :]
    ang_c = (t % GRID_W).astype(np.float32)[:, None] * inv[None, :]
    cr, sr, cc, sc = np.cos(ang_r), np.sin(ang_r), np.cos(ang_c), np.sin(ang_c)
    z = np.zeros_like(sr)
    cos = np.concatenate([cr, cr, cc, cc], axis=1)
    s1 = np.concatenate([z, sr, z, sc], axis=1)
    s2 = np.concatenate([-sr, z, -sc, z], axis=1)
    return [jnp.asarray(a, F32) for a in (cos, s1, s2)]


def _my_place():
    return lax.axis_index("x"), lax.axis_index("y"), lax.axis_index("c")


def _all_gather(shards, n_gather):
    n_all = len(shards)
    nw = n_gather

    def body(*refs):
        ins, outs = refs[:n_all], refs[n_all:2 * n_all]
        send_sems, recv_sems, local_sems = refs[2 * n_all:]
        x, y, c = _my_place()
        me, sibling = (x, y, c), (x, y, 1 - c)
        chips = [(1 - x, y), (x, 1 - y), (1 - x, 1 - y)]

        def rows(w, px, py, pc):
            n = ins[w].shape[0]
            return outs[w].at[pl.ds(pl.multiple_of((4 * px + 2 * py + pc) * n, 16), n), :]

        def copy(w, k, block, to, src=None):
            return pltpu.make_async_remote_copy(
                src_ref=rows(w, *block) if src is None else src, dst_ref=rows(w, *block),
                send_sem=send_sems.at[w, k], recv_sem=recv_sems.at[w, k], device_id=to, device_id_type=MESH)

        mine = [pltpu.make_async_copy(ins[w], rows(w, *me), local_sems.at[w]) for w in range(n_all)]
        for cp in mine:
            cp.start()
        first = []
        for w in range(nw):
            first.append(copy(w, 0, me, sibling, src=ins[w]))
            first += [copy(w, 1 + j, me, (*chip, c), src=ins[w]) for j, chip in enumerate(chips)]
        for cp in first:
            cp.start()
        passed = []
        for j, chip in enumerate(chips):
            for w in range(nw):
                copy(w, 1 + j, (*chip, c), me).wait_recv()
                fwd = copy(w, 4 + j, (*chip, c), sibling)
                fwd.start()
                passed.append(fwd)
        for w in range(nw):
            copy(w, 0, sibling, me).wait_recv()
        for j, chip in enumerate(chips):
            for w in range(nw):
                copy(w, 4 + j, (*chip, 1 - c), me).wait_recv()
        for cp in first + passed:
            cp.wait_send()
        for cp in mine:
            cp.wait()

    any_spec = pl.BlockSpec(memory_space=pl.ANY)
    return pl.pallas_call(
        body, name="weights_all_gather",
        out_shape=[jax.ShapeDtypeStruct((N_DEV * s.shape[0], s.shape[1]), s.dtype) for s in shards],
        in_specs=[any_spec] * n_all, out_specs=[any_spec] * n_all,
        scratch_shapes=[pltpu.SemaphoreType.DMA((nw, 7)), pltpu.SemaphoreType.DMA((nw, 7)),
                        pltpu.SemaphoreType.DMA((n_all,))],
    )(*shards)


def _place_own_rows(shards, my_idx):
    nw = len(shards)

    def body(idx_ref, *refs):
        for w in range(nw):
            refs[nw + w][...] = refs[w][...]

    grid_spec = pltpu.PrefetchScalarGridSpec(
        num_scalar_prefetch=1, grid=(1,),
        in_specs=[pl.BlockSpec(s.shape, lambda i, idx: (0, 0)) for s in shards],
        out_specs=[pl.BlockSpec(s.shape, lambda i, idx: (idx[0], 0)) for s in shards])
    return pl.pallas_call(
        body, name="place_own_rows", grid_spec=grid_spec,
        out_shape=[jax.ShapeDtypeStruct((N_DEV * s.shape[0], s.shape[1]), s.dtype) for s in shards],
        compiler_params=_cparams(("arbitrary",)))(my_idx.reshape(1).astype(jnp.int32), *shards)


_FLIPS = [(fx, fy, fc) for fx in (0, 1) for fy in (0, 1) for fc in (0, 1)][1:]


def _small_all_gather(v, after):
    def body(v_ref, after_ref, out_ref, send_sems, recv_sems):
        x, y, c = _my_place()
        my_idx = 4 * x + 2 * y + c
        out_ref[my_idx] = v_ref[...]
        sends = []
        for k, (fx, fy, fc) in enumerate(_FLIPS):
            to = (1 - x if fx else x, 1 - y if fy else y, 1 - c if fc else c)
            sends.append(pltpu.make_async_remote_copy(
                src_ref=v_ref, dst_ref=out_ref.at[my_idx], send_sem=send_sems.at[k], recv_sem=recv_sems.at[k],
                device_id=to, device_id_type=MESH))
        for cp in sends:
            cp.start()
        for k, (fx, fy, fc) in enumerate(_FLIPS):
            frm_idx = 4 * (1 - x if fx else x) + 2 * (1 - y if fy else y) + (1 - c if fc else c)
            pltpu.make_async_remote_copy(
                src_ref=v_ref, dst_ref=out_ref.at[frm_idx], send_sem=send_sems.at[k], recv_sem=recv_sems.at[k],
                device_id=(x, y, c), device_id_type=MESH).wait_recv()
        for cp in sends:
            cp.wait_send()

    vm = pl.BlockSpec(memory_space=pltpu.VMEM)
    return pl.pallas_call(
        body, name="small_all_gather", out_shape=jax.ShapeDtypeStruct((N_DEV,) + v.shape, v.dtype),
        in_specs=[vm, pl.BlockSpec(memory_space=pl.ANY)], out_specs=vm,
        scratch_shapes=[pltpu.SemaphoreType.DMA((7,)), pltpu.SemaphoreType.DMA((7,))],
    )(v, after)


_HBM = pl.BlockSpec(memory_space=pltpu.HBM)
_SEM = pl.BlockSpec(memory_space=pltpu.SEMAPHORE)
_ANY = pl.BlockSpec(memory_space=pl.ANY)
_SPLIT_COPY = dict(has_side_effects=pltpu.SideEffectType.DATAFLOW_SIDE_EFFECTING)


def _peer(x, y, c, k):
    fx, fy, fc = _FLIPS[k]
    return (1 - x if fx else x, 1 - y if fy else y, 1 - c if fc else c)


def _in_hbm(a):
    return pltpu.with_memory_space_constraint(a, pltpu.HBM)


def _split_copies(srcs, lands, send_sems, recv_sems, gather, arriving):
    x, y, c = _my_place()
    my_idx = 4 * x + 2 * y + c
    out = []
    for k in range(7):
        to = _peer(x, y, c, k)
        to_idx = 4 * to[0] + 2 * to[1] + to[2]
        for w in range(len(srcs)):
            if gather:
                n = srcs[w].shape[0]
                src = srcs[w]
                dst = lands[w].at[pl.ds(pl.multiple_of((to_idx if arriving else my_idx) * n, 16), n), :]
            else:
                n = lands[w].shape[1]
                src = srcs[w].at[pl.ds(pl.multiple_of(to_idx * n, 16), n), :]
                dst = lands[w].at[k]
            out.append(pltpu.make_async_remote_copy(
                src_ref=src, dst_ref=dst, send_sem=send_sems.at[7 * w + k], recv_sem=recv_sems.at[7 * w + k],
                device_id=to, device_id_type=MESH))
    return out


def _copies_start(name, srcs, lands, after, gather):
    nw = len(srcs)

    def body(*refs):
        send_sems, recv_sems = refs[2 * nw + 1], refs[2 * nw + 2]
        for cp in _split_copies(refs[:nw], refs[nw:2 * nw], send_sems, recv_sems, gather, False):
            cp.start()
        refs[-1][...] = jnp.zeros_like(refs[-1])

    sems = pltpu.SemaphoreType.DMA((7 * nw,))
    thru = [pltpu.HBM(a.shape, a.dtype) for a in list(srcs) + list(lands)]
    res = pl.pallas_call(
        body, name=name, out_shape=(sems, sems, *thru, jax.ShapeDtypeStruct((8, LANES), F32)),
        in_specs=[_HBM] * (2 * nw) + [_ANY], out_specs=(_SEM, _SEM, *[_HBM] * (2 * nw), pl.BlockSpec(memory_space=pltpu.VMEM)),
        input_output_aliases={i: 2 + i for i in range(2 * nw)},
        compiler_params=pltpu.CompilerParams(**_SPLIT_COPY),
    )(*[_in_hbm(a) for a in srcs], *[_in_hbm(a) for a in lands], after)
    return res[0], res[1], list(res[2:2 + nw]), list(res[2 + nw:2 + 2 * nw]), res[-1]


def _copies_wait(name, send_sems, recv_sems, srcs, lands, after, gather):
    nw = len(srcs)

    def body(*refs):
        for cp in _split_copies(refs[:nw], refs[nw:2 * nw], refs[2 * nw], refs[2 * nw + 1], gather, False):
            cp.wait_send()
        for cp in _split_copies(refs[:nw], refs[nw:2 * nw], refs[2 * nw], refs[2 * nw + 1], gather, True):
            cp.wait_recv()

    thru = [pltpu.HBM(a.shape, a.dtype) for a in list(srcs) + list(lands)]
    res = pl.pallas_call(
        body, name=name, out_shape=tuple(thru),
        in_specs=[_HBM] * (2 * nw) + [_SEM, _SEM, _ANY], out_specs=tuple([_HBM] * (2 * nw)),
        input_output_aliases={i: i for i in range(2 * nw)},
        compiler_params=pltpu.CompilerParams(**_SPLIT_COPY),
    )(*srcs, *lands, send_sems, recv_sems, after)
    return list(res[:nw]), list(res[nw:])


def _in_proj(x, tabs, w_in_t, g_mix, b_gate, q_g, k_g, tb, after):
    s, d = x.shape
    n_gate_chunks = d // 256
    q_scale = HEAD_DIM_A ** -0.5 * LOG2_E
    b_scale = HEAD_DIM_B ** -0.5 * LOG2_E

    def body(x_ref, c_ref, s1_ref, s2_ref, w_ref, gmix_ref, bg_ref, qg_ref, kg_ref, after_ref,
             h1_ref, qraw_ref, kraw_ref, qrot_ref, krot_ref, va_ref, *rest):
        qb_refs, kb_refs, vb_refs = rest[0:3], rest[3:6], rest[6:9]
        ga_ref, gb_ref, scr_ref = rest[9:]
        xv = x_ref[...]
        hb = (xv * _rstd(xv) * gmix_ref[...]).astype(BF16)
        h1_ref[...] = hb
        cos, s1, s2 = c_ref[...], s1_ref[...], s2_ref[...]

        def proj(lo, width):
            return _dot_nt(hb, w_ref[lo:lo + width, :])

        def norm_rope(z, g):
            n = z * _rstd(z) * g
            return n * cos + pltpu.roll(n, 32, 1) * s1 + pltpu.roll(n, 96, 1) * s2

        for j in range(QA_W // 256):
            z = proj(OFF_QA + 256 * j, 256)
            qraw_ref[:, 256 * j:256 * j + 256] = z
            for hh in range(2):
                lo = 256 * j + 128 * hh
                qrot_ref[:, lo:lo + 128] = (norm_rope(z[:, 128 * hh:128 * hh + 128], qg_ref[...]) * q_scale).astype(BF16)
        z = proj(OFF_KA, 256)
        kraw_ref[...] = z
        for hh in range(2):
            krot_ref[:, 128 * hh:128 * hh + 128] = norm_rope(z[:, 128 * hh:128 * hh + 128], kg_ref[...]).astype(BF16)
        va_ref[...] = proj(OFF_VA, 256).astype(BF16)
        for g, dil in enumerate(DILATIONS):
            _to_residues(proj(OFF_QB + GB_W * g, GB_W) * b_scale, qb_refs[g], scr_ref, dil, BF16)
            _to_residues(proj(OFF_KB + GB_W * g, GB_W), kb_refs[g], scr_ref, dil, BF16)
            _to_residues(proj(OFF_VB + GB_W * g, GB_W), vb_refs[g], scr_ref, dil, BF16)
        for j in range(n_gate_chunks):
            sl = slice(256 * j, 256 * j + 256)
            ga_ref[:, sl] = _sigmoid(proj(OFF_GA + 256 * j, 256) + bg_ref[:, sl]).astype(BF16)
            gb_ref[:, sl] = _sigmoid(
                proj(OFF_GA + d + 256 * j, 256) + bg_ref[:, d + 256 * j:d + 256 * j + 256]).astype(BF16)

    sd = jax.ShapeDtypeStruct
    outs = [sd((s, d), BF16), sd((s, QA_W), F32), sd((s, KA_W), F32), sd((s, QA_W), BF16), sd((s, KA_W), BF16),
            sd((s, KA_W), BF16)] + _dil_shapes(s, BF16) * 3 + [sd((s, d), BF16), sd((s, d), BF16)]
    out_specs = [_rows(tb, d), _rows(tb, QA_W), _rows(tb, KA_W), _rows(tb, QA_W), _rows(tb, KA_W), _rows(tb, KA_W)
                 ] + _dil_specs(tb) * 3 + [_rows(tb, d), _rows(tb, d)]
    in_specs = [_rows(tb, d), _rows(tb, LANES), _rows(tb, LANES), _rows(tb, LANES), _resident(w_in_t.shape),
                _resident(g_mix.shape), _resident(b_gate.shape), _resident(q_g.shape), _resident(k_g.shape), _ANY]
    res = list(pl.pallas_call(body, name="in_proj", grid=(s // tb,), in_specs=in_specs, out_specs=out_specs,
                              out_shape=outs, scratch_shapes=[pltpu.VMEM((2, tb, LANES), F32)],
                              compiler_params=_cparams(("arbitrary",)))(
        x, *tabs, w_in_t, g_mix, b_gate, q_g, k_g, after))
    return res[:6] + [res[6:9], res[9:12], res[12:15]] + res[15:]


def _attn_a_fwd(qrot, krot, va, tq, tk):
    s = qrot.shape[0]
    n_kv = s // tk
    gw = Q_PER_KV * HEAD_DIM_A

    def body(q_ref, k_ref, v_ref, o_ref, lse_ref):
        q4 = jnp.concatenate([q_ref[:, 128 * h:128 * h + 128] for h in range(Q_PER_KV)], axis=0)

        def step(j, carry):
            m, l, acc = carry
            sl = pl.ds(pl.multiple_of(j * tk, tk), tk)
            kj, vj = k_ref[sl, :], v_ref[sl, :]
            sc = _dot_nt(kj, q4)
            m_new = jnp.maximum(m, jnp.max(sc, axis=0, keepdims=True))
            p = jnp.exp2(sc - m_new)
            alpha = jnp.exp2(m - m_new)
            l = alpha * l + jnp.sum(p, axis=0, keepdims=True)
            acc = alpha * acc + _dot_tn(vj, p.astype(BF16))
            return m_new, l, acc

        rows = Q_PER_KV * tq
        m, l, acc = lax.fori_loop(0, n_kv, step, (jnp.full((1, rows), NEG_INF, F32), jnp.zeros((1, rows), F32),
                                                  jnp.zeros((HEAD_DIM_A, rows), F32)))
        o = (acc / l).T
        lse = m + jnp.log2(l)
        for h in range(Q_PER_KV):
            o_ref[:, 128 * h:128 * h + 128] = o[h * tq:(h + 1) * tq].astype(BF16)
            lse_ref[0, h:h + 1, :] = lse[:, h * tq:(h + 1) * tq]

    return pl.pallas_call(
        body, name="attn_a_fwd", grid=(N_KV_HEADS_A, s // tq),
        in_specs=[pl.BlockSpec((tq, gw), lambda g, i: (i, g)),
                  pl.BlockSpec((s, HEAD_DIM_A), lambda g, i: (0, g)),
                  pl.BlockSpec((s, HEAD_DIM_A), lambda g, i: (0, g))],
        out_specs=[pl.BlockSpec((tq, gw), lambda g, i: (i, g)),
                   pl.BlockSpec((1, Q_PER_KV, tq), lambda g, i: (g, 0, i))],
        out_shape=[jax.ShapeDtypeStruct((s, QA_W), BF16), jax.ShapeDtypeStruct((N_KV_HEADS_A, Q_PER_KV, s), F32)],
        compiler_params=_cparams(("arbitrary", "arbitrary")))(qrot, krot, va)


def _attn_a_bwd(qrot, krot, va, oa, doa, lse, tq, tk, after):
    s = qrot.shape[0]
    n_kv = s // tk
    gw = Q_PER_KV * HEAD_DIM_A

    def body(q_ref, do_ref, o_ref, lse_ref, k_ref, v_ref, after_ref, dq_ref, dk_ref, dv_ref):
        @pl.when(pl.program_id(1) == 0)
        def _():
            dk_ref[...] = jnp.zeros_like(dk_ref)
            dv_ref[...] = jnp.zeros_like(dv_ref)

        def stack(ref):
            return jnp.concatenate([ref[:, 128 * h:128 * h + 128] for h in range(Q_PER_KV)], axis=0)

        q4, do4, o4 = stack(q_ref), stack(do_ref), stack(o_ref)
        delta = jnp.sum(do4.astype(F32) * o4.astype(F32), axis=-1, keepdims=True)
        lse_cols = jnp.concatenate([lse_ref[0], jnp.zeros_like(lse_ref[0])], axis=0).T
        lse4 = jnp.concatenate([lse_cols[:, h:h + 1] for h in range(Q_PER_KV)], axis=0)

        def step(j, dq):
            sl = pl.ds(pl.multiple_of(j * tk, tk), tk)
            kj, vj = k_ref[sl, :], v_ref[sl, :]
            p = jnp.exp2(_dot_nt(q4, kj) - lse4)
            ds = (p * (_dot_nt(do4, vj) - delta)).astype(BF16)
            dk_ref[sl, :] += _dot_tn(ds, q4)
            dv_ref[sl, :] += _dot_tn(p.astype(BF16), do4)
            return dq + _dot_nn(ds, kj)

        dq = lax.fori_loop(0, n_kv, step, jnp.zeros((Q_PER_KV * tq, HEAD_DIM_A), F32))
        for h in range(Q_PER_KV):
            dq_ref[:, 128 * h:128 * h + 128] = dq[h * tq:(h + 1) * tq]

    qspec = pl.BlockSpec((tq, gw), lambda g, i: (i, g))
    kspec = pl.BlockSpec((s, HEAD_DIM_A), lambda g, i: (0, g))
    return pl.pallas_call(
        body, name="attn_a_bwd", grid=(N_KV_HEADS_A, s // tq),
        in_specs=[qspec, qspec, qspec, pl.BlockSpec((1, Q_PER_KV, tq), lambda g, i: (g, 0, i)), kspec, kspec, _ANY],
        out_specs=[qspec, kspec, kspec],
        out_shape=[jax.ShapeDtypeStruct((s, QA_W), F32), jax.ShapeDtypeStruct((s, KA_W), F32),
                   jax.ShapeDtypeStruct((s, KA_W), F32)],
        compiler_params=_cparams(("arbitrary", "arbitrary")))(qrot, doa, oa, lse, krot, va, after)


BAND_QB = 128
BAND_WIN = BAND_QB + 2 * BAND


def _band_specs(s, cb):
    per = cb // BAND
    last = s // BAND - 1
    cur = pl.BlockSpec((cb, GB_W), lambda i: (i, 0))
    prev = pl.BlockSpec((BAND, GB_W), lambda i: (jnp.maximum(i * per - 1, 0), 0))
    nxt = pl.BlockSpec((BAND, GB_W), lambda i: (jnp.minimum(i * per + per, last), 0))
    return cur, prev, nxt


def _window(prev_ref, cur_ref, next_ref):
    return jnp.concatenate([prev_ref[...], cur_ref[...], next_ref[...]], axis=0)


def _band_mask(base, seg_shift):
    rq = base + lax.broadcasted_iota(jnp.int32, (BAND_QB, BAND_WIN), 0)
    rk = base - BAND + lax.broadcasted_iota(jnp.int32, (BAND_QB, BAND_WIN), 1)
    same_segment = lax.shift_right_arithmetic(rq, jnp.int32(seg_shift)) == lax.shift_right_arithmetic(rk, jnp.int32(seg_shift))
    return (jnp.abs(rk - rq) <= BAND) & same_segment


def _build_bias(bmap_ref, tab_ref, bias_ref):
    bm = bmap_ref[...]
    acc = [jnp.full(bm.shape, NEG_INF, F32) for _ in range(N_HEADS_PER_DIL)]
    for b in range(N_REL_BUCKETS):
        hit = bm == b
        for h in range(N_HEADS_PER_DIL):
            acc[h] = jnp.where(hit, tab_ref[b, h] * LOG2_E, acc[h])
    rows = bm.shape[0]
    for h in range(N_HEADS_PER_DIL):
        bias_ref[h * rows:(h + 1) * rows, :] = acc[h]


def _segment_mask(base, seg_len, seg_shift):
    if seg_len % BAND_QB:
        return _band_mask(base, seg_shift)
    pos = lax.rem(base, seg_len)
    w = lax.broadcasted_iota(jnp.int32, (1, BAND_WIN), 1)
    return ((w >= BAND) | (pos != 0)) & ((w < BAND + BAND_QB) | (pos != seg_len - BAND_QB))


def _head_lane_masks():
    lane = lax.broadcasted_iota(jnp.int32, (1, LANES), 1)
    return [lane < HEAD_DIM_B, lane >= HEAD_DIM_B]


def _rows4(mask):
    return mask if mask.shape[0] == 1 else jnp.concatenate([mask] * N_HEADS_PER_DIL, axis=0)


def _head_scores(a, b):
    hm = _head_lane_masks()
    out = []
    for hp in range(2):
        ls = slice(LANES * hp, LANES * hp + LANES)
        ah = a[:, ls]
        both = jnp.concatenate([jnp.where(hm[0], ah, jnp.zeros_like(ah)), jnp.where(hm[1], ah, jnp.zeros_like(ah))],
                               axis=0)
        out.append(_dot_nt(both, b[:, ls]))
    return jnp.concatenate(out, axis=0)


def _head_combine(p, v, scale=None, transposed=False):
    hm = _head_lane_masks()
    rows = p.shape[0] // N_HEADS_PER_DIL
    halves = []
    for hp in range(2):
        vh = v[:, LANES * hp:LANES * hp + LANES]
        acc = None
        for hh in range(2):
            h = 2 * hp + hh
            ph = p[h * rows:(h + 1) * rows]
            vm = jnp.where(hm[hh], vh, jnp.zeros_like(vh))
            t = _dot_tn(ph, vm) if transposed else _dot_nn(ph, vm)
            if scale is not None:
                t = t * scale[h * rows:(h + 1) * rows]
            acc = t if acc is None else acc + t
        halves.append(acc)
    return jnp.concatenate(halves, axis=1)


def _head_spread(col):
    rows = col.shape[0] // N_HEADS_PER_DIL
    lane = lax.broadcasted_iota(jnp.int32, (1, GB_W), 1)
    out = jnp.zeros((rows, GB_W), F32)
    for h in range(N_HEADS_PER_DIL):
        out = jnp.where((lane >= HEAD_DIM_B * h) & (lane < HEAD_DIM_B * (h + 1)), col[h * rows:(h + 1) * rows], out)
    return out


def _head_cols(v):
    return jnp.concatenate([v[:, HEAD_DIM_B * h:HEAD_DIM_B * h + 1] for h in range(N_HEADS_PER_DIL)], axis=0)


def _seg_shift(s, dil):
    seg = s // dil
    assert seg & (seg - 1) == 0, "segment length must be a power of two"
    return seg.bit_length() - 1


def _band_fwd(dil, qb, kb, vb, bmap, tab, cb):
    s = qb.shape[0]
    shift = _seg_shift(s, dil)

    def body(q_ref, kp_ref, kc_ref, kn_ref, vp_ref, vc_ref, vn_ref, bmap_ref, tab_ref, o_ref, lse_ref, bias_ref):
        @pl.when(pl.program_id(0) == 0)
        def _():
            _build_bias(bmap_ref, tab_ref, bias_ref)

        kw, vw = _window(kp_ref, kc_ref, kn_ref), _window(vp_ref, vc_ref, vn_ref)
        for jj in range(cb // BAND_QB):
            r0 = BAND_QB * jj
            mask = _rows4(_segment_mask(pl.program_id(0) * cb + r0, s // dil, shift))
            sc = _head_scores(q_ref[r0:r0 + BAND_QB, :], kw[r0:r0 + BAND_WIN, :]) + bias_ref[...]
            sc = jnp.where(mask, sc, NEG_INF)
            m = jnp.max(sc, axis=-1, keepdims=True)
            e = jnp.exp2(sc - m)
            l = jnp.sum(e, axis=-1, keepdims=True)
            o = _head_combine(e.astype(BF16), vw[r0:r0 + BAND_WIN, :], 1.0 / l)
            o_ref[r0:r0 + BAND_QB, :] = o
            lse_ref[r0:r0 + BAND_QB, :] = _head_spread(m + jnp.log2(l))

    cur, prev, nxt = _band_specs(s, cb)
    return pl.pallas_call(
        body, name=f"band_fwd_d{dil}", grid=(s // cb,),
        in_specs=[cur, prev, cur, nxt, prev, cur, nxt, _resident(bmap.shape), pl.BlockSpec(memory_space=pltpu.SMEM)],
        out_specs=[cur, cur],
        out_shape=[jax.ShapeDtypeStruct(qb.shape, F32), jax.ShapeDtypeStruct(qb.shape, F32)],
        scratch_shapes=[pltpu.VMEM((N_HEADS_PER_DIL * BAND_QB, BAND_WIN), F32)],
        compiler_params=_cparams(("arbitrary",)))(qb, kb, kb, kb, vb, vb, vb, bmap, tab)


def _band_bwd(dil, qb, kb, vb, dob, lse, dd, bmap, tab, cb):
    s = qb.shape[0]
    shift = _seg_shift(s, dil)
    n_steps = s // cb

    def body(q_ref, do_ref, lse_ref, dd_ref, kp_ref, kc_ref, kn_ref, vp_ref, vc_ref, vn_ref, bmap_ref, tab_ref,
             dq_ref, dk_ref, dv_ref, dtab_ref, bias_ref, dsum_ref):
        @pl.when(pl.program_id(0) == 0)
        def _():
            _build_bias(bmap_ref, tab_ref, bias_ref)
            dsum_ref[...] = jnp.zeros_like(dsum_ref)
            dk_ref[...] = jnp.zeros_like(dk_ref)
            dv_ref[...] = jnp.zeros_like(dv_ref)

        kw, vw = _window(kp_ref, kc_ref, kn_ref), _window(vp_ref, vc_ref, vn_ref)
        for jj in range(cb // BAND_QB):
            r0 = BAND_QB * jj
            base = pl.program_id(0) * cb + r0
            mask = _rows4(_segment_mask(base, s // dil, shift))
            qh, doh = q_ref[r0:r0 + BAND_QB, :], do_ref[r0:r0 + BAND_QB, :]
            k3, v3 = kw[r0:r0 + BAND_WIN, :], vw[r0:r0 + BAND_WIN, :]
            sc = _head_scores(qh, k3) + bias_ref[...]
            sc = jnp.where(mask, sc, NEG_INF)
            p = jnp.exp2(sc - _head_cols(lse_ref[r0:r0 + BAND_QB, :]))
            dp = _head_scores(doh, v3)
            ds = p * (dp - _head_cols(dd_ref[r0:r0 + BAND_QB, :]))
            dsum_ref[...] += ds
            dsb = ds.astype(BF16)
            dq_ref[r0:r0 + BAND_QB, :] = _head_combine(dsb, k3)
            dk_win = _head_combine(dsb, qh, transposed=True)
            dv_win = _head_combine(p.astype(BF16), doh, transposed=True)
            own = pl.ds(pl.multiple_of(base, BAND), BAND_QB)
            dk_ref[own, :] += dk_win[BAND:BAND + BAND_QB]
            dv_ref[own, :] += dv_win[BAND:BAND + BAND_QB]

            @pl.when(base > 0)
            def _():
                before = pl.ds(pl.multiple_of(base - BAND, BAND), BAND)
                dk_ref[before, :] += dk_win[:BAND]
                dv_ref[before, :] += dv_win[:BAND]

            @pl.when(base + BAND_QB < s)
            def _():
                after = pl.ds(pl.multiple_of(base + BAND_QB, BAND), BAND)
                dk_ref[after, :] += dk_win[BAND + BAND_QB:]
                dv_ref[after, :] += dv_win[BAND + BAND_QB:]

        @pl.when(pl.program_id(0) == n_steps - 1)
        def _():
            bm = bmap_ref[...]
            lane = lax.broadcasted_iota(jnp.int32, (1, LANES), 1)
            for b in range(N_REL_BUCKETS):
                hit = bm == b
                row = jnp.zeros((1, LANES), F32)
                for h in range(N_HEADS_PER_DIL):
                    part = dsum_ref[h * BAND_QB:(h + 1) * BAND_QB, :]
                    row = jnp.where(lane == h, jnp.sum(jnp.where(hit, part, 0.0)), row)
                dtab_ref[b:b + 1, :] = row

    cur, prev, nxt = _band_specs(s, cb)
    whole = _acc_spec(qb.shape)
    return pl.pallas_call(
        body, name=f"band_bwd_d{dil}", grid=(n_steps,),
        in_specs=[cur, cur, cur, cur, prev, cur, nxt, prev, cur, nxt, _resident(bmap.shape),
                  pl.BlockSpec(memory_space=pltpu.SMEM)],
        out_specs=[cur, whole, whole, _acc_spec((N_REL_BUCKETS, LANES))],
        out_shape=[jax.ShapeDtypeStruct(qb.shape, F32)] * 3 + [jax.ShapeDtypeStruct((N_REL_BUCKETS, LANES), F32)],
        scratch_shapes=[pltpu.VMEM((N_HEADS_PER_DIL * BAND_QB, BAND_WIN), F32),
                        pltpu.VMEM((N_HEADS_PER_DIL * BAND_QB, BAND_WIN), F32)],
        compiler_params=_cparams(("arbitrary",)))(qb, dob, lse, dd, kb, kb, kb, vb, vb, vb, bmap, tab)


def _t5_bucket(rel):
    nb = N_REL_BUCKETS // 2
    ret = (rel > 0).astype(np.int32) * nb
    n = np.abs(rel)
    max_exact = nb // 2
    large = max_exact + (np.log(np.maximum(n, 1) / max_exact) / math.log(REL_MAX_DIST / max_exact)
                         * (nb - max_exact)).astype(np.int32)
    large = np.minimum(large, nb - 1)
    return ret + np.where(n < max_exact, n, large).astype(np.int32)


def _bucket_map(dil):
    off = np.arange(BAND_WIN)[None, :] - BAND - np.arange(BAND_QB)[:, None]
    return np.where(np.abs(off) <= BAND, _t5_bucket(off * dil), -1).astype(np.int32)


def _seg_sum(v):
    lane = lax.broadcasted_iota(jnp.int32, (1, v.shape[1]), 1)
    out = jnp.zeros_like(v)
    for h in range(v.shape[1] // HEAD_DIM_B):
        m = (lane >= HEAD_DIM_B * h) & (lane < HEAD_DIM_B * (h + 1))
        out = jnp.where(m, jnp.sum(jnp.where(m, v, 0.0), axis=-1, keepdims=True), out)
    return out


def _mix_out(x, oa, og, lg, ga, gb, w_oa, w_ob_t, w_o, tb):
    s, d = x.shape

    def body(x_ref, oa_ref, og0_ref, og1_ref, og2_ref, lg0_ref, lg1_ref, lg2_ref, ga_ref, gb_ref,
             woa_ref, wob_ref, wo_ref, x2_ref, ob_ref, lse0_ref, lse1_ref, lse2_ref, ya_ref, yb_ref, u_ref, scr_ref):
        og_refs, lg_refs = (og0_ref, og1_ref, og2_ref), (lg0_ref, lg1_ref, lg2_ref)
        l0, l1, l2 = [_from_residues(lg_refs[g], scr_ref, dil) for g, dil in enumerate(DILATIONS)]
        lmax = jnp.maximum(jnp.maximum(l0, l1), l2)
        w0, w1, w2 = jnp.exp2(l0 - lmax), jnp.exp2(l1 - lmax), jnp.exp2(l2 - lmax)
        den = w0 + w1 + w2
        o0, o1, o2 = [_from_residues(og_refs[g], scr_ref, dil) for g, dil in enumerate(DILATIONS)]
        ob = ((w0 * o0 + w1 * o1 + w2 * o2) / den).astype(BF16)
        ob_ref[...] = ob
        lse = lmax + jnp.log2(den)
        for g, (dil, ref) in enumerate(zip(DILATIONS, (lse0_ref, lse1_ref, lse2_ref))):
            _to_residues(lse, ref, scr_ref, dil, F32)
        ya = _dot_nn(oa_ref[...], woa_ref[...])
        yb = _dot_nt(ob, wob_ref[...])
        ya_ref[...] = ya.astype(BF16)
        yb_ref[...] = yb.astype(BF16)
        u = (ga_ref[...].astype(F32) * ya + gb_ref[...].astype(F32) * yb).astype(BF16)
        u_ref[...] = u
        x2_ref[...] = x_ref[...] + _dot_nn(u, wo_ref[...])

    sd = jax.ShapeDtypeStruct
    res = list(pl.pallas_call(
        body, name="mix_out", grid=(s // tb,),
        in_specs=[_rows(tb, d), _rows(tb, QA_W)] + _dil_specs(tb) * 2 + [
            _rows(tb, d), _rows(tb, d), _resident(w_oa.shape), _resident(w_ob_t.shape), _resident(w_o.shape)],
        out_specs=[_rows(tb, d), _rows(tb, GB_W)] + _dil_specs(tb) + [_rows(tb, d), _rows(tb, d), _rows(tb, d)],
        out_shape=[sd((s, d), F32), sd((s, GB_W), BF16)] + _dil_shapes(s, F32) + [
            sd((s, d), BF16), sd((s, d), BF16), sd((s, d), BF16)],
        scratch_shapes=[pltpu.VMEM((2, tb, LANES), F32)],
        compiler_params=_cparams(("arbitrary",)))(x, oa, *og, *lg, ga, gb, w_oa, w_ob_t, w_o))
    return res[:2] + [res[2:5]] + res[5:]


def _mlp_fwd(x2, w1_t, w2, g_mlp, tb, tc):
    s, d = x2.shape
    dff = w1_t.shape[0]

    def body(x_ref, w1_ref, w2_ref, g_ref, x3_ref, r_ref, h_ref):
        xv = x_ref[...]
        hb = (xv * _rstd(xv) * g_ref[...]).astype(BF16)
        h_ref[...] = hb
        x3_ref[...] = xv
        for c in range(dff // tc):
            sl = slice(tc * c, tc * c + tc)
            r = jnp.maximum(_dot_nt(hb, w1_ref[sl, :]), 0.0)
            r_ref[:, sl] = r.astype(BF16)
            x3_ref[...] += _dot_nn((r * r).astype(BF16), w2_ref[sl, :])

    sd = jax.ShapeDtypeStruct
    return pl.pallas_call(
        body, name="mlp_fwd", grid=(s // tb,),
        in_specs=[_rows(tb, d), _resident(w1_t.shape), _resident(w2.shape), _resident(g_mlp.shape)],
        out_specs=[_rows(tb, d), _rows(tb, dff), _rows(tb, d)],
        out_shape=[sd((s, d), F32), sd((s, dff), BF16), sd((s, d), BF16)],
        compiler_params=_cparams(("arbitrary",)))(x2, w1_t, w2, g_mlp)


def _ple_loss(x3, p, target, w_pg, w_p_t, g_ple, g_fin, tb):
    s, d = x3.shape
    dp = p.shape[1]

    def body(x_ref, p_ref, t_ref, wpg_ref, wp_ref, gple_ref, gfin_ref,
             dx3_ref, h3_ref, dpre_ref, dpe_ref, pb_ref, loss_ref, dgfin_ref, dgple_ref):
        @pl.when(pl.program_id(0) == 0)
        def _():
            loss_ref[...] = jnp.zeros_like(loss_ref)
            dgfin_ref[...] = jnp.zeros_like(dgfin_ref)
            dgple_ref[...] = jnp.zeros_like(dgple_ref)

        x3v = x_ref[...]
        r3 = _rstd(x3v)
        n3 = x3v * r3
        h3 = (n3 * gple_ref[...]).astype(BF16)
        h3_ref[...] = h3
        gp = _sigmoid(_dot_nn(h3, wpg_ref[...]))
        pb = p_ref[...].astype(BF16)
        pb_ref[...] = pb
        pe = _dot_nt(pb, wp_ref[...])
        x4 = x3v + gp * pe
        r4 = _rstd(x4)
        n4 = x4 * r4
        err = n4 * gfin_ref[...] - t_ref[...]
        loss_ref[...] += jnp.sum(0.5 * jnp.mean(err * err, axis=-1, keepdims=True), axis=0, keepdims=True)
        dy = err * (1.0 / d)
        dgfin_ref[...] += _colsum(dy * n4)
        dx4 = _rms_bwd(dy, n4, r4, gfin_ref[...])
        dpe_ref[...] = (dx4 * gp).astype(BF16)
        dpre = (dx4 * pe * gp * (1.0 - gp)).astype(BF16)
        dpre_ref[...] = dpre
        dh3 = _dot_nt(dpre, wpg_ref[...])
        dgple_ref[...] += _colsum(dh3 * n3)
        dx3_ref[...] = dx4 + _rms_bwd(dh3, n3, r3, gple_ref[...])

    sd = jax.ShapeDtypeStruct
    return pl.pallas_call(
        body, name="ple_loss", grid=(s // tb,),
        in_specs=[_rows(tb, d), _rows(tb, dp), _rows(tb, d), _resident(w_pg.shape), _resident(w_p_t.shape),
                  _resident(g_ple.shape), _resident(g_fin.shape)],
        out_specs=[_rows(tb, d), _rows(tb, d), _rows(tb, d), _rows(tb, d), _rows(tb, dp),
                   _acc_spec((1, LANES)), _acc_spec((1, d)), _acc_spec((1, d))],
        out_shape=[sd((s, d), F32), sd((s, d), BF16), sd((s, d), BF16), sd((s, d), BF16), sd((s, dp), BF16),
                   sd((1, LANES), F32), sd((1, d), F32), sd((1, d), F32)],
        compiler_params=_cparams(("arbitrary",)))(x3, p, target, w_pg, w_p_t, g_ple, g_fin)


def _mlp_bwd(dx3, x2, r, w1_t, w2, g_mlp, tb, tc):
    s, d = x2.shape
    dff = w1_t.shape[0]

    def body(dx3_ref, x_ref, r_ref, w1_ref, w2_ref, g_ref, dx2_ref, df_ref, dg_ref, dh_ref):
        @pl.when(pl.program_id(0) == 0)
        def _():
            dg_ref[...] = jnp.zeros_like(dg_ref)

        dx3v = dx3_ref[...]
        dx3b = dx3v.astype(BF16)
        dh_ref[...] = jnp.zeros_like(dh_ref)
        for c in range(dff // tc):
            sl = slice(tc * c, tc * c + tc)
            df = (_dot_nt(dx3b, w2_ref[sl, :]) * (2.0 * r_ref[:, sl].astype(F32))).astype(BF16)
            df_ref[:, sl] = df
            dh_ref[...] += _dot_nn(df, w1_ref[sl, :])
        xv = x_ref[...]
        r2 = _rstd(xv)
        n2 = xv * r2
        dh = dh_ref[...]
        dg_ref[...] += _colsum(dh * n2)
        dx2_ref[...] = dx3v + _rms_bwd(dh, n2, r2, g_ref[...])

    sd = jax.ShapeDtypeStruct
    return pl.pallas_call(
        body, name="mlp_bwd", grid=(s // tb,),
        in_specs=[_rows(tb, d), _rows(tb, d), _rows(tb, dff), _resident(w1_t.shape), _resident(w2.shape),
                  _resident(g_mlp.shape)],
        out_specs=[_rows(tb, d), _rows(tb, dff), _acc_spec((1, d))],
        out_shape=[sd((s, d), F32), sd((s, dff), BF16), sd((1, d), F32)],
        scratch_shapes=[pltpu.VMEM((tb, d), F32)],
        compiler_params=_cparams(("arbitrary",)))(dx3, x2, r, w1_t, w2, g_mlp)


def _mix_out_bwd(dx2, ya, yb, ga, gb, ob, w_oa, w_ob_t, w_o, tb, after):
    s, d = dx2.shape

    def body(dx_ref, ya_ref, yb_ref, ga_ref, gb_ref, ob_ref, woa_ref, wob_ref, wo_ref, after_ref,
             doa_ref, dob0_ref, dob1_ref, dob2_ref, dd0_ref, dd1_ref, dd2_ref, dga_ref, dgb_ref, dya_ref, dyb_ref,
             dbg_ref, scr_ref):
        @pl.when(pl.program_id(0) == 0)
        def _():
            dbg_ref[...] = jnp.zeros_like(dbg_ref)

        du = _dot_nt(dx_ref[...].astype(BF16), wo_ref[...])
        gav, gbv = ga_ref[...].astype(F32), gb_ref[...].astype(F32)
        dya = (du * gav).astype(BF16)
        dyb = (du * gbv).astype(BF16)
        dya_ref[...] = dya
        dyb_ref[...] = dyb
        dga = du * ya_ref[...].astype(F32) * gav * (1.0 - gav)
        dgb = du * yb_ref[...].astype(F32) * gbv * (1.0 - gbv)
        dga_ref[...] = dga.astype(BF16)
        dgb_ref[...] = dgb.astype(BF16)
        dbg_ref[:, 0:d] += _colsum(dga)
        dbg_ref[:, d:2 * d] += _colsum(dgb)
        doa_ref[...] = _dot_nt(dya, woa_ref[...]).astype(BF16)
        dob = _dot_nn(dyb, wob_ref[...])
        dd = _seg_sum(dob * ob_ref[...].astype(F32))
        for dil, dob_ref, dd_ref in zip(DILATIONS, (dob0_ref, dob1_ref, dob2_ref), (dd0_ref, dd1_ref, dd2_ref)):
            _to_residues(dob, dob_ref, scr_ref, dil, BF16)
            _to_residues(dd, dd_ref, scr_ref, dil, F32)

    sd = jax.ShapeDtypeStruct
    res = list(pl.pallas_call(
        body, name="mix_out_bwd", grid=(s // tb,),
        in_specs=[_rows(tb, d)] * 5 + [_rows(tb, GB_W), _resident(w_oa.shape), _resident(w_ob_t.shape),
                                       _resident(w_o.shape), _ANY],
        out_specs=[_rows(tb, QA_W)] + _dil_specs(tb) * 2 + [_rows(tb, d), _rows(tb, d), _rows(tb, d),
                                                           _rows(tb, d), _acc_spec((1, 2 * d))],
        out_shape=[sd((s, QA_W), BF16)] + _dil_shapes(s, BF16) + _dil_shapes(s, F32) + [
            sd((s, d), BF16), sd((s, d), BF16), sd((s, d), BF16), sd((s, d), BF16), sd((1, 2 * d), F32)],
        scratch_shapes=[pltpu.VMEM((2, tb, LANES), F32)],
        compiler_params=_cparams(("arbitrary",)))(dx2, ya, yb, ga, gb, ob, w_oa, w_ob_t, w_o, after))
    return res[:1] + [res[1:4], res[4:7]] + res[7:]


def _in_proj_bwd(dx2, x, dqrot, dkrot, dva, qraw, kraw, tabs, dqb, dkb, dvb, dga, dgb, w_in_t, g_mix, q_g, k_g, tb):
    s, d = x.shape
    din = w_in_t.shape[0]
    q_scale = HEAD_DIM_A ** -0.5
    b_scale = HEAD_DIM_B ** -0.5
    tc = 256

    def body(dx2_ref, x_ref, dq_ref, dk_ref, dv_ref, qraw_ref, kraw_ref, c_ref, s1_ref, s2_ref, *rest):
        dqb_refs, dkb_refs, dvb_refs = rest[0:3], rest[3:6], rest[6:9]
        (dga_ref, dgb_ref, w_ref, gmix_ref, qg_ref, kg_ref,
         dx_ref, dz_ref, dgmix_ref, dqg_ref, dkg_ref, dh_ref, scr_ref) = rest[9:]

        @pl.when(pl.program_id(0) == 0)
        def _():
            dgmix_ref[...] = jnp.zeros_like(dgmix_ref)
            dqg_ref[...] = jnp.zeros_like(dqg_ref)
            dkg_ref[...] = jnp.zeros_like(dkg_ref)

        cos, s1, s2 = c_ref[...][None], s1_ref[...][None], s2_ref[...][None]

        def heads_bwd(drot, z, g_ref, acc_ref):
            dn = drot * cos + pltpu.roll(drot * s1, 96, 2) + pltpu.roll(drot * s2, 32, 2)
            rr = _rstd(z)
            nn = z * rr
            acc_ref[...] += jnp.sum(jnp.sum(dn * nn, axis=0), axis=0, keepdims=True)
            return _rms_bwd(dn, nn, rr, g_ref[...][None]).astype(BF16)

        dh_ref[...] = jnp.zeros_like(dh_ref)

        def emit(off, piece):
            dz_ref[:, off:off + tc] = piece
            dh_ref[...] += _dot_nn(piece, w_ref[off:off + tc, :])

        for j in range(d // tc):
            emit(OFF_GA + tc * j, dga_ref[:, tc * j:tc * j + tc])
            emit(OFF_GA + d + tc * j, dgb_ref[:, tc * j:tc * j + tc])
        emit(OFF_VA, dv_ref[...].astype(BF16))
        for g, dil in enumerate(DILATIONS):
            emit(OFF_QB + GB_W * g, (_from_residues(dqb_refs[g], scr_ref, dil) * b_scale).astype(BF16))
            emit(OFF_KB + GB_W * g, (_from_residues(dkb_refs[g], scr_ref, dil) * LN_2).astype(BF16))
            emit(OFF_VB + GB_W * g, _from_residues(dvb_refs[g], scr_ref, dil).astype(BF16))
        stack = lambda ref, n: jnp.stack([ref[:, 128 * h:128 * h + 128] for h in range(n)], axis=0)
        dzq = heads_bwd(stack(dq_ref, N_Q_HEADS_A) * q_scale, stack(qraw_ref, N_Q_HEADS_A), qg_ref, dqg_ref)
        dzk = heads_bwd(stack(dk_ref, N_KV_HEADS_A) * LN_2, stack(kraw_ref, N_KV_HEADS_A), kg_ref, dkg_ref)
        for j in range(N_Q_HEADS_A // 2):
            emit(OFF_QA + tc * j, jnp.concatenate([dzq[2 * j], dzq[2 * j + 1]], axis=1))
        emit(OFF_KA, jnp.concatenate([dzk[0], dzk[1]], axis=1))
        xv = x_ref[...]
        r1 = _rstd(xv)
        n1 = xv * r1
        dh = dh_ref[...]
        dgmix_ref[...] += _colsum(dh * n1)
        dx_ref[...] = dx2_ref[...] + _rms_bwd(dh, n1, r1, gmix_ref[...])

    sd = jax.ShapeDtypeStruct
    return pl.pallas_call(
        body, name="in_proj_bwd", grid=(s // tb,),
        in_specs=[_rows(tb, d), _rows(tb, d), _rows(tb, QA_W), _rows(tb, KA_W), _rows(tb, KA_W), _rows(tb, QA_W),
                  _rows(tb, KA_W), _rows(tb, LANES), _rows(tb, LANES), _rows(tb, LANES),
                  ] + _dil_specs(tb) * 3 + [_rows(tb, d), _rows(tb, d),
                  _resident(w_in_t.shape), _resident(g_mix.shape), _resident(q_g.shape), _resident(k_g.shape)],
        out_specs=[_rows(tb, d), _rows(tb, din), _acc_spec((1, d)), _acc_spec((1, HEAD_DIM_A)),
                   _acc_spec((1, HEAD_DIM_A))],
        out_shape=[sd((s, d), F32), sd((s, din), BF16), sd((1, d), F32), sd((1, HEAD_DIM_A), F32),
                   sd((1, HEAD_DIM_A), F32)],
        scratch_shapes=[pltpu.VMEM((tb, d), F32), pltpu.VMEM((2, tb, LANES), F32)],
        compiler_params=_cparams(("arbitrary",)))(
        dx2, x, dqrot, dkrot, dva, qraw, kraw, *tabs, *dqb, *dkb, *dvb, dga, dgb, w_in_t, g_mix, q_g, k_g)


def _identity(v):
    return v


def _to_bf16(v):
    return v.astype(BF16)


def _square_bf16(v):
    vf = v.astype(F32)
    return (vf * vf).astype(BF16)


def _weight_grad(name, a, b, ti, tj, tk, a_fn=_identity, b_fn=_identity, col0=0, n=None, after=None):
    t, m = a.shape
    n = b.shape[1] if n is None else n
    n_k = t // tk
    after = a if after is None else after

    def body(a_ref, b_ref, after_ref, o_ref, acc_ref):
        k = pl.program_id(2)

        @pl.when(k == 0)
        def _():
            acc_ref[...] = jnp.zeros_like(acc_ref)

        acc_ref[...] += _dot_tn(a_fn(a_ref[...]), b_fn(b_ref[...]))

        @pl.when(k == n_k - 1)
        def _():
            o_ref[...] = acc_ref[...].astype(BF16)

    return pl.pallas_call(
        body, name=name, grid=(m // ti, n // tj, n_k),
        in_specs=[pl.BlockSpec((tk, ti), lambda i, j, k: (k, i)),
                  pl.BlockSpec((tk, tj), lambda i, j, k: (k, j + col0 // tj)), _ANY],
        out_specs=pl.BlockSpec((ti, tj), lambda i, j, k: (i, j)),
        out_shape=jax.ShapeDtypeStruct((m, n), BF16),
        scratch_shapes=[pltpu.VMEM((ti, tj), F32)],
        compiler_params=_cparams(("arbitrary", "arbitrary", "arbitrary")))(a, b, after)


def _sum_slots(name, recv, own, transposed):
    m, n, k = recv.shape
    tc = min(k, 256)
    n_pad = -(-n // LANES) * LANES

    def body(own_ref, r_ref, o_ref):
        acc = own_ref[...].astype(F32)
        for i in range(m):
            acc = acc + r_ref[i].astype(F32)
        if transposed:
            if n_pad != n:
                acc = jnp.concatenate([acc, jnp.zeros((n_pad - n, tc), F32)], axis=0)
            acc = acc.T[:, :n]
        o_ref[...] = acc

    out_spec, out_shape = ((pl.BlockSpec((tc, n), lambda j: (j, 0)), (k, n)) if transposed
                           else (pl.BlockSpec((n, tc), lambda j: (0, j)), (n, k)))
    return pl.pallas_call(
        body, name=name, grid=(k // tc,),
        in_specs=[pl.BlockSpec((n, tc), lambda j: (0, j)), pl.BlockSpec((m, n, tc), lambda j: (0, 0, j))],
        out_specs=out_spec, out_shape=jax.ShapeDtypeStruct(out_shape, F32),
        compiler_params=_cparams(("arbitrary",)))(own, recv)


def _adamw_math(w, g, m, v):
    m = ADAM_B1 * m + (1.0 - ADAM_B1) * g
    v = ADAM_B2 * v + (1.0 - ADAM_B2) * (g * g)
    m_hat = m / (1.0 - ADAM_B1 ** ADAM_STEP)
    v_hat = v / (1.0 - ADAM_B2 ** ADAM_STEP)
    delta = -ADAM_LR * (m_hat / (jnp.sqrt(v_hat) + ADAM_EPS) + ADAM_WD * w)
    return delta, m, v


def _adamw(name, w, g, m, v):
    r, c = w.shape
    tr = max(t for t in range(8, min(r, 256) + 1, 8) if r % t == 0)

    def body(w_ref, g_ref, m_ref, v_ref, d_ref, mo_ref, vo_ref):
        d_ref[...], mo_ref[...], vo_ref[...] = _adamw_math(w_ref[...], g_ref[...], m_ref[...], v_ref[...])

    spec = pl.BlockSpec((tr, c), lambda i: (i, 0))
    return pl.pallas_call(
        body, name=name, grid=(r // tr,), in_specs=[spec] * 4, out_specs=[spec] * 3,
        out_shape=[jax.ShapeDtypeStruct((r, c), F32)] * 3,
        compiler_params=_cparams(("arbitrary",)))(w, g, m, v)


def _small_update(parts, w, m, v):
    def body(p_ref, w_ref, m_ref, v_ref, g_ref, d_ref, mo_ref, vo_ref):
        g = p_ref[0]
        for i in range(1, N_DEV):
            g = g + p_ref[i]
        g_ref[...] = g
        d_ref[...], mo_ref[...], vo_ref[...] = _adamw_math(w_ref[...], g, m_ref[...], v_ref[...])

    return pl.pallas_call(body, name="small_update", out_shape=[jax.ShapeDtypeStruct(w.shape, F32)] * 4)(
        parts, w, m, v)


def _pack_rows(vectors, n_rows):
    flat = jnp.concatenate([v.reshape(-1).astype(F32) for v in vectors])
    flat = jnp.pad(flat, (0, n_rows * LANES - flat.shape[0]))
    return flat.reshape(n_rows, LANES)


def _pick_tile(n, prefs):
    for t in prefs:
        if n % t == 0:
            return t
    return n


def kernel(x, p, norm_mix_g, w_in, b_gate, q_norm_g, k_norm_g, rel_bias, w_out_a, w_out_b, w_out, norm_mlp_g, w_ff1, w_ff2, norm_ple_g, w_ple_gate, w_ple, final_norm_g, loss_target, m_norm_mix_g, m_w_in, m_b_gate, m_q_norm_g, m_k_norm_g, m_rel_bias, m_w_out_a, m_w_out_b, m_w_out, m_norm_mlp_g, m_w_ff1, m_w_ff2, m_norm_ple_g, m_w_ple_gate, m_w_ple, m_final_norm_g, v_norm_mix_g, v_w_in, v_b_gate, v_q_norm_g, v_k_norm_g, v_rel_bias, v_w_out_a, v_w_out_b, v_w_out, v_norm_mlp_g, v_w_ff1, v_w_ff2, v_norm_ple_g, v_w_ple_gate, v_w_ple, v_final_norm_g):
    s, d = x.shape[1], x.shape[2]
    xs, ps, ts = x[0], p[0, 0], loss_target[0]
    tb = _pick_tile(s, (512, 256))
    tq = _pick_tile(s, (256,))
    tk = _pick_tile(s, (1024, 512))
    cb = _pick_tile(s, (1024, 512))
    fin_g = final_norm_g.reshape(1, d)

    col_sharded = {"w_in": w_in[0], "w_out_b": w_out_b[0], "w_ff1": w_ff1[0], "w_ple": w_ple[0]}
    row_sharded = {"w_out_a": w_out_a[0], "w_out": w_out[0], "w_ff2": w_ff2[0], "w_ple_gate": w_ple_gate[0]}
    order = ["w_in", "w_out_a", "w_out_b", "w_out", "w_ff1", "w_ff2", "w_ple_gate", "w_ple"]
    shards = [(col_sharded[n].T if n in col_sharded else row_sharded[n]).astype(BF16) for n in order]
    my_idx = 4 * lax.axis_index("x") + 2 * lax.axis_index("y") + lax.axis_index("c")
    (w_in_t,) = _all_gather(shards[:1], 1)
    zones = _place_own_rows(shards[1:], my_idx)
    ag = _copies_start("weights_gather_start", shards[1:], zones, w_in_t, True)

    tabs = _rope_tables(s)
    (h1, qraw, kraw, qrot, krot, va, qb, kb, vb, ga, gb) = _in_proj(
        xs, tabs, w_in_t, norm_mix_g, b_gate, q_norm_g, k_norm_g, tb, ag[4])
    oa, lse_a = _attn_a_fwd(qrot, krot, va, _pick_tile(s, (512, 256)), tk)
    _, (w_oa, w_ob_t, w_o, w_ff1_t, w_ff2_f, w_pg, w_p_t) = _copies_wait(
        "weights_gather_wait", ag[0], ag[1], ag[2], ag[3], lse_a, True)
    flat = lambda arrs: [a.reshape(s, GB_W) for a in arrs]
    split = lambda arrs: [a.reshape(dil, s // dil, GB_W) for a, dil in zip(arrs, DILATIONS)]
    qb_r, kb_r, vb_r = flat(qb), flat(kb), flat(vb)
    bmaps = [jnp.asarray(_bucket_map(dil)) for dil in DILATIONS]
    bias_tabs = [rel_bias[:, N_HEADS_PER_DIL * g:N_HEADS_PER_DIL * (g + 1)] for g in range(3)]
    band_out = [_band_fwd(dil, qb_r[g], kb_r[g], vb_r[g], bmaps[g], bias_tabs[g], cb)
                for g, dil in enumerate(DILATIONS)]
    og, lg = split([o for o, _ in band_out]), split([l for _, l in band_out])
    x2, ob, lse_b, ya, yb, u = _mix_out(xs, oa, og, lg, ga, gb, w_oa, w_ob_t, w_o, tb)
    tc = _pick_tile(w_ff1_t.shape[0], (512,))
    x3, r_act, h2 = _mlp_fwd(x2, w_ff1_t, w_ff2_f, norm_mlp_g, tb, tc)

    dx3, h3, dpre, dpe, pb, loss_part, dg_fin, dg_ple = _ple_loss(
        x3, ps, ts, w_pg, w_p_t, norm_ple_g, fin_g, tb)
    dx2, df, dg_mlp = _mlp_bwd(dx3, x2, r_act, w_ff1_t, w_ff2_f, norm_mlp_g, tb, tc)

    tkk = _pick_tile(s, (1024, 512))
    tk2 = _pick_tile(s, (2048, 1024, 512))
    dff = w_ff1_t.shape[0]
    t1k = lambda n: _pick_tile(n, (1024, 512, 256))
    slots = lambda parts: [lax.empty((7, a.shape[0] // N_DEV, a.shape[1]), BF16) for a in parts]
    part1 = [_weight_grad("grad_w_ff1", df, h2, t1k(dff), t1k(d), tkk),
             _weight_grad("grad_w_ff2", r_act, dx3, t1k(dff), t1k(d), tkk, a_fn=_square_bf16, b_fn=_to_bf16),
             _weight_grad("grad_w_ple_gate", h3, dpre, t1k(d), t1k(d), tk2),
             _weight_grad("grad_w_ple", dpe, pb, t1k(d), ps.shape[1], tk2)]
    doa, dob, dd, dga, dgb, dya, dyb, dbg = _mix_out_bwd(dx2, ya, yb, ga, gb, ob, w_oa, w_ob_t, w_o, tb, dx2)
    part1 += [_weight_grad("grad_w_out_a", oa, dya, t1k(QA_W), t1k(d), tk2),
              _weight_grad("grad_w_out_b", dyb, ob, t1k(d), GB_W, tk2),
              _weight_grad("grad_w_out", u, dx2, t1k(d), t1k(d), tkk, b_fn=_to_bf16)]
    rs1 = _copies_start("grads1_start", part1, slots(part1), doa, False)
    dqrot, dkrot, dva = _attn_a_bwd(qrot, krot, va, oa, doa, lse_a, tq, tk, rs1[4])
    dob_r, lse_r, dd_r = flat(dob), flat(lse_b), flat(dd)
    band_bwd = [_band_bwd(dil, qb_r[g], kb_r[g], vb_r[g], dob_r[g], lse_r[g], dd_r[g], bmaps[g], bias_tabs[g], cb)
                for g, dil in enumerate(DILATIONS)]
    dqb, dkb, dvb = [split([r[j] for r in band_bwd]) for j in range(3)]
    grad_x, dz, dg_mix, dg_q, dg_k = _in_proj_bwd(
        dx2, xs, dqrot, dkrot, dva, qraw, kraw, tabs, dqb, dkb, dvb, dga, dgb, w_in_t, norm_mix_g,
        q_norm_g, k_norm_g, _pick_tile(s, (256,)))
    d_rel = jnp.concatenate([r[3][:, :N_HEADS_PER_DIL] for r in band_bwd], axis=1)

    din = w_in_t.shape[0]
    ti_in = _pick_tile(din, (din // 2,)) if (din // 2) % LANES == 0 else din
    hd_ = d // 2
    part3 = [_weight_grad("grad_w_in_lo", dz, h1, ti_in, t1k(hd_), tkk, n=hd_)]
    rs3 = _copies_start("grads3_start", part3, slots(part3), grad_x, False)
    part4 = [_weight_grad("grad_w_in_hi", dz, h1, ti_in, t1k(hd_), tkk, col0=hd_, n=hd_, after=rs3[4])]
    rs4 = _copies_start("grads4_start", part4, slots(part4), rs3[4], False)

    def own_rows(a):
        n = a.shape[0] // N_DEV
        return lax.dynamic_slice(a, (my_idx * n, 0), (n, a.shape[1]))

    sums = {}
    src1, got1 = _copies_wait("grads1_wait", rs1[0], rs1[1], rs1[2], rs1[3], rs4[4], False)
    for n, a, r in zip(["w_ff1", "w_ff2", "w_ple_gate", "w_ple", "w_out_a", "w_out_b", "w_out"], src1, got1):
        sums[n] = _sum_slots("sum_" + n, r, own_rows(a), n in col_sharded)
    given_w = dict(w_in=w_in, w_out_a=w_out_a, w_out_b=w_out_b, w_out=w_out, w_ff1=w_ff1, w_ff2=w_ff2,
                   w_ple_gate=w_ple_gate, w_ple=w_ple)
    given_m = dict(w_in=m_w_in, w_out_a=m_w_out_a, w_out_b=m_w_out_b, w_out=m_w_out, w_ff1=m_w_ff1, w_ff2=m_w_ff2,
                   w_ple_gate=m_w_ple_gate, w_ple=m_w_ple)
    given_v = dict(w_in=v_w_in, w_out_a=v_w_out_a, w_out_b=v_w_out_b, w_out=v_w_out, w_ff1=v_w_ff1, w_ff2=v_w_ff2,
                   w_ple_gate=v_w_ple_gate, w_ple=v_w_ple)
    big = {}

    def update(n, transposed=False):
        view = (lambda a: a.T) if transposed else (lambda a: a)
        g = sums[n]
        delta, new_m, new_v = _adamw("adamw_" + n, view(given_w[n][0]), g, view(given_m[n][0]), view(given_v[n][0]))
        big[n] = tuple(view(a)[None] for a in (g, delta, new_m, new_v))

    for n in order[1:]:
        update(n)

    small_names = ["norm_mix_g", "b_gate", "q_norm_g", "k_norm_g", "rel_bias", "norm_mlp_g", "norm_ple_g",
                   "final_norm_g"]
    small_w = [norm_mix_g, b_gate, q_norm_g, k_norm_g, rel_bias, norm_mlp_g, norm_ple_g, final_norm_g]
    small_m = [m_norm_mix_g, m_b_gate, m_q_norm_g, m_k_norm_g, m_rel_bias, m_norm_mlp_g, m_norm_ple_g,
               m_final_norm_g]
    small_v = [v_norm_mix_g, v_b_gate, v_q_norm_g, v_k_norm_g, v_rel_bias, v_norm_mlp_g, v_norm_ple_g,
               v_final_norm_g]
    small_g = [dg_mix, dbg, dg_q, dg_k, d_rel, dg_mlp, dg_ple, dg_fin]
    sizes = [int(np.prod(w.shape)) for w in small_w]
    n_rows = -(-(sum(-(-sz // LANES) for sz in sizes) + 1) // 8) * 8
    pad = lambda v: jnp.pad(v.reshape(-1).astype(F32), (0, -v.size % LANES))
    pack = lambda vs, last: _pack_rows([pad(v) for v in vs] + [last], n_rows)
    zero_row = jnp.zeros((LANES,), F32)
    parts = _small_all_gather(pack(small_g, loss_part.reshape(-1) * (jnp.arange(LANES) == 0)), big["w_ple"][1])
    g_all, d_all, m_all, v_all = _small_update(parts, pack(small_w, zero_row), pack(small_m, zero_row),
                                               pack(small_v, zero_row))
    small = {}
    row = 0
    for n, w, sz in zip(small_names, small_w, sizes):
        nr = -(-sz // LANES)
        small[n] = tuple(a[row:row + nr].reshape(-1)[:sz].reshape(w.shape) for a in (g_all, d_all, m_all, v_all))
        row += nr
    loss = g_all[row, 0]

    src3, got3 = _copies_wait("grads3_wait", rs3[0], rs3[1], rs3[2], rs3[3], g_all, False)
    src4, got4 = _copies_wait("grads4_wait", rs4[0], rs4[1], rs4[2], rs4[3], g_all, False)
    sums["w_in"] = jnp.concatenate([_sum_slots("sum_w_in_lo", got3[0], own_rows(src3[0]), False),
                                    _sum_slots("sum_w_in_hi", got4[0], own_rows(src4[0]), False)], axis=1)
    update("w_in", transposed=True)

    names = ["norm_mix_g", "w_in", "b_gate", "q_norm_g", "k_norm_g", "rel_bias", "w_out_a", "w_out_b", "w_out",
             "norm_mlp_g", "w_ff1", "w_ff2", "norm_ple_g", "w_ple_gate", "w_ple", "final_norm_g"]
    res = {n: (big[n] if n in big else small[n]) for n in names}
    return (loss, grad_x[None], *[res[n][0] for n in names], *[res[n][1] for n in names],
            *[res[n][2] for n in names], *[res[n][3] for n in names])
```

```python
import math

import numpy as np
import jax
import jax.numpy as jnp
from jax import lax
from jax.experimental import pallas as pl
from jax.experimental.pallas import tpu as pltpu

F32 = jnp.float32
BF16 = jnp.bfloat16
MESH = pl.DeviceIdType.MESH

NORM_EPS = 1e-6
NEG_INF = -1e30
LOG2_E = math.log2(math.e)
LN_2 = math.log(2.0)
GRID_W = 64
ROPE_THETA = 10000.0
HEAD_DIM_A = 128
N_Q_HEADS_A = 8
N_KV_HEADS_A = 2
Q_PER_KV = N_Q_HEADS_A // N_KV_HEADS_A
HEAD_DIM_B = 64
N_HEADS_PER_DIL = 4
DILATIONS = (1, 4, 16)
BAND = 64
N_REL_BUCKETS = 32
REL_MAX_DIST = 1024
QA_W = N_Q_HEADS_A * HEAD_DIM_A
KA_W = N_KV_HEADS_A * HEAD_DIM_A
GB_W = N_HEADS_PER_DIL * HEAD_DIM_B
QB_W = GB_W * len(DILATIONS)
OFF_QA, OFF_KA, OFF_VA = 0, QA_W, QA_W + KA_W
OFF_QB = QA_W + 2 * KA_W
OFF_KB = OFF_QB + QB_W
OFF_VB = OFF_KB + QB_W
OFF_GA = OFF_VB + QB_W
N_DEV = 8
LANES = 128
VMEM_LIMIT = 56 * 2 ** 20

ADAM_LR, ADAM_B1, ADAM_B2, ADAM_EPS, ADAM_WD, ADAM_STEP = 0.001, 0.9, 0.999, 1e-08, 0.01, 10


def _cparams(sem):
    return pltpu.CompilerParams(dimension_semantics=sem, vmem_limit_bytes=VMEM_LIMIT)


def _resident(shape):
    nd = len(shape)
    return pl.BlockSpec(shape, lambda *_: (0,) * nd, pipeline_mode=pl.Buffered(1))


def _acc_spec(shape):
    nd = len(shape)
    return pl.BlockSpec(shape, lambda *_: (0,) * nd)


def _rows(tb, c):
    return pl.BlockSpec((tb, c), lambda i: (i, 0))


def _dil_shapes(s, dtype):
    return [jax.ShapeDtypeStruct((dil, s // dil, GB_W), dtype) for dil in DILATIONS]


def _dil_specs(tb):
    return [pl.BlockSpec((dil, tb // dil, GB_W), lambda i: (0, i, 0)) for dil in DILATIONS]


def _to_residues(val, out_ref, scr_ref, dil, dtype):
    if dil == 1:
        out_ref[0] = val.astype(dtype)
        return
    n = val.shape[0] // dil
    scr_ref[0] = val[:, :LANES]
    scr_ref[1] = val[:, LANES:]
    for r in range(dil):
        out_ref[r] = jnp.concatenate([scr_ref[0, pl.ds(r, n, stride=dil), :],
                                      scr_ref[1, pl.ds(r, n, stride=dil), :]], axis=1).astype(dtype)


def _from_residues(in_ref, scr_ref, dil):
    if dil == 1:
        return in_ref[0]
    n = in_ref.shape[1]
    for r in range(dil):
        v = in_ref[r]
        scr_ref[0, pl.ds(r, n, stride=dil), :] = v[:, :LANES]
        scr_ref[1, pl.ds(r, n, stride=dil), :] = v[:, LANES:]
    return jnp.concatenate([scr_ref[0], scr_ref[1]], axis=1)


def _dot_nt(a, b):
    return lax.dot_general(a, b, (((1,), (1,)), ((), ())), preferred_element_type=F32)


def _dot_nn(a, b):
    return lax.dot_general(a, b, (((1,), (0,)), ((), ())), preferred_element_type=F32)


def _dot_tn(a, b):
    return lax.dot_general(a, b, (((0,), (0,)), ((), ())), preferred_element_type=F32)


def _rstd(x):
    return lax.rsqrt(jnp.mean(x * x, axis=-1, keepdims=True) + NORM_EPS)


def _rms_bwd(dy, n, r, g):
    dn = dy * g
    return r * (dn - n * jnp.mean(dn * n, axis=-1, keepdims=True))


def _colsum(v):
    return jnp.sum(v, axis=0, keepdims=True)


def _sigmoid(v):
    return 1.0 / (1.0 + jnp.exp(-v))


def _rope_tables(s):
    half = HEAD_DIM_A // 2
    inv = np.power(np.float32(ROPE_THETA), -np.arange(0, half, 2, dtype=np.float32) / np.float32(half))
    t = np.arange(s)
    ang_r = (t // GRID_W).astype(np.float32)[:, None] * inv[None, :]
    ang_c = (t % GRID_W).astype(np.float32)[:, None] * inv[None, :]
    cr, sr, cc, sc = np.cos(ang_r), np.sin(ang_r), np.cos(ang_c), np.sin(ang_c)
    z = np.zeros_like(sr)
    cos = np.concatenate([cr, cr, cc, cc], axis=1)
    s1 = np.concatenate([z, sr, z, sc], axis=1)
    s2 = np.concatenate([-sr, z, -sc, z], axis=1)
    return [jnp.asarray(a, F32) for a in (cos, s1, s2)]


def _my_place():
    return lax.axis_index("x"), lax.axis_index("y"), lax.axis_index("c")


def _all_gather(shards, n_gather):
    n_all = len(shards)
    nw = n_gather

    def body(*refs):
        ins, outs = refs[:n_all], refs[n_all:2 * n_all]
        send_sems, recv_sems, local_sems = refs[2 * n_all:]
        x, y, c = _my_place()
        me, sibling = (x, y, c), (x, y, 1 - c)
        chips = [(1 - x, y), (x, 1 - y), (1 - x, 1 - y)]

        def rows(w, px, py, pc):
            n = ins[w].shape[0]
            return outs[w].at[pl.ds(pl.multiple_of((4 * px + 2 * py + pc) * n, 16), n), :]

        def copy(w, k, block, to, src=None):
            return pltpu.make_async_remote_copy(
                src_ref=rows(w, *block) if src is None else src, dst_ref=rows(w, *block),
                send_sem=send_sems.at[w, k], recv_sem=recv_sems.at[w, k], device_id=to, device_id_type=MESH)

        mine = [pltpu.make_async_copy(ins[w], rows(w, *me), local_sems.at[w]) for w in range(n_all)]
        for cp in mine:
            cp.start()
        first = []
        for w in range(nw):
            first.append(copy(w, 0, me, sibling, src=ins[w]))
            first += [copy(w, 1 + j, me, (*chip, c), src=ins[w]) for j, chip in enumerate(chips)]
        for cp in first:
            cp.start()
        passed = []
        for j, chip in enumerate(chips):
            for w in range(nw):
                copy(w, 1 + j, (*chip, c), me).wait_recv()
                fwd = copy(w, 4 + j, (*chip, c), sibling)
                fwd.start()
                passed.append(fwd)
        for w in range(nw):
            copy(w, 0, sibling, me).wait_recv()
        for j, chip in enumerate(chips):
            for w in range(nw):
                copy(w, 4 + j, (*chip, 1 - c), me).wait_recv()
        for cp in first + passed:
            cp.wait_send()
        for cp in mine:
            cp.wait()

    any_spec = pl.BlockSpec(memory_space=pl.ANY)
    return pl.pallas_call(
        body, name="weights_all_gather",
        out_shape=[jax.ShapeDtypeStruct((N_DEV * s.shape[0], s.shape[1]), s.dtype) for s in shards],
        in_specs=[any_spec] * n_all, out_specs=[any_spec] * n_all,
        scratch_shapes=[pltpu.SemaphoreType.DMA((nw, 7)), pltpu.SemaphoreType.DMA((nw, 7)),
                        pltpu.SemaphoreType.DMA((n_all,))],
    )(*shards)


def _place_own_rows(shards, my_idx):
    nw = len(shards)

    def body(idx_ref, *refs):
        for w in range(nw):
            refs[nw + w][...] = refs[w][...]

    grid_spec = pltpu.PrefetchScalarGridSpec(
        num_scalar_prefetch=1, grid=(1,),
        in_specs=[pl.BlockSpec(s.shape, lambda i, idx: (0, 0)) for s in shards],
        out_specs=[pl.BlockSpec(s.shape, lambda i, idx: (idx[0], 0)) for s in shards])
    return pl.pallas_call(
        body, name="place_own_rows", grid_spec=grid_spec,
        out_shape=[jax.ShapeDtypeStruct((N_DEV * s.shape[0], s.shape[1]), s.dtype) for s in shards],
        compiler_params=_cparams(("arbitrary",)))(my_idx.reshape(1).astype(jnp.int32), *shards)


_FLIPS = [(fx, fy, fc) for fx in (0, 1) for fy in (0, 1) for fc in (0, 1)][1:]


def _small_all_gather(v, after):
    def body(v_ref, after_ref, out_ref, send_sems, recv_sems):
        x, y, c = _my_place()
        my_idx = 4 * x + 2 * y + c
        out_ref[my_idx] = v_ref[...]
        sends = []
        for k, (fx, fy, fc) in enumerate(_FLIPS):
            to = (1 - x if fx else x, 1 - y if fy else y, 1 - c if fc else c)
            sends.append(pltpu.make_async_remote_copy(
                src_ref=v_ref, dst_ref=out_ref.at[my_idx], send_sem=send_sems.at[k], recv_sem=recv_sems.at[k],
                device_id=to, device_id_type=MESH))
        for cp in sends:
            cp.start()
        for k, (fx, fy, fc) in enumerate(_FLIPS):
            frm_idx = 4 * (1 - x if fx else x) + 2 * (1 - y if fy else y) + (1 - c if fc else c)
            pltpu.make_async_remote_copy(
                src_ref=v_ref, dst_ref=out_ref.at[frm_idx], send_sem=send_sems.at[k], recv_sem=recv_sems.at[k],
                device_id=(x, y, c), device_id_type=MESH).wait_recv()
        for cp in sends:
            cp.wait_send()

    vm = pl.BlockSpec(memory_space=pltpu.VMEM)
    return pl.pallas_call(
        body, name="small_all_gather", out_shape=jax.ShapeDtypeStruct((N_DEV,) + v.shape, v.dtype),
        in_specs=[vm, pl.BlockSpec(memory_space=pl.ANY)], out_specs=vm,
        scratch_shapes=[pltpu.SemaphoreType.DMA((7,)), pltpu.SemaphoreType.DMA((7,))],
    )(v, after)


_HBM = pl.BlockSpec(memory_space=pltpu.HBM)
_SEM = pl.BlockSpec(memory_space=pltpu.SEMAPHORE)
_ANY = pl.BlockSpec(memory_space=pl.ANY)
_SPLIT_COPY = dict(has_side_effects=pltpu.SideEffectType.DATAFLOW_SIDE_EFFECTING)


def _peer(x, y, c, k):
    fx, fy, fc = _FLIPS[k]
    return (1 - x if fx else x, 1 - y if fy else y, 1 - c if fc else c)


def _in_hbm(a):
    return pltpu.with_memory_space_constraint(a, pltpu.HBM)


def _split_copies(srcs, lands, send_sems, recv_sems, gather, arriving):
    x, y, c = _my_place()
    my_idx = 4 * x + 2 * y + c
    out = []
    for k in range(7):
        to = _peer(x, y, c, k)
        to_idx = 4 * to[0] + 2 * to[1] + to[2]
        for w in range(len(srcs)):
            if gather:
                n = srcs[w].shape[0]
                src = srcs[w]
                dst = lands[w].at[pl.ds(pl.multiple_of((to_idx if arriving else my_idx) * n, 16), n), :]
            else:
                n = lands[w].shape[1]
                src = srcs[w].at[pl.ds(pl.multiple_of(to_idx * n, 16), n), :]
                dst = lands[w].at[k]
            out.append(pltpu.make_async_remote_copy(
                src_ref=src, dst_ref=dst, send_sem=send_sems.at[7 * w + k], recv_sem=recv_sems.at[7 * w + k],
                device_id=to, device_id_type=MESH))
    return out


def _copies_start(name, srcs, lands, after, gather):
    nw = len(srcs)

    def body(*refs):
        send_sems, recv_sems = refs[2 * nw + 1], refs[2 * nw + 2]
        for cp in _split_copies(refs[:nw], refs[nw:2 * nw], send_sems, recv_sems, gather, False):
            cp.start()
        refs[-1][...] = jnp.zeros_like(refs[-1])

    sems = pltpu.SemaphoreType.DMA((7 * nw,))
    thru = [pltpu.HBM(a.shape, a.dtype) for a in list(srcs) + list(lands)]
    res = pl.pallas_call(
        body, name=name, out_shape=(sems, sems, *thru, jax.ShapeDtypeStruct((8, LANES), F32)),
        in_specs=[_HBM] * (2 * nw) + [_ANY], out_specs=(_SEM, _SEM, *[_HBM] * (2 * nw), pl.BlockSpec(memory_space=pltpu.VMEM)),
        input_output_aliases={i: 2 + i for i in range(2 * nw)},
        compiler_params=pltpu.CompilerParams(**_SPLIT_COPY),
    )(*[_in_hbm(a) for a in srcs], *[_in_hbm(a) for a in lands], after)
    return res[0], res[1], list(res[2:2 + nw]), list(res[2 + nw:2 + 2 * nw]), res[-1]


def _copies_wait(name, send_sems, recv_sems, srcs, lands, after, gather):
    nw = len(srcs)

    def body(*refs):
        for cp in _split_copies(refs[:nw], refs[nw:2 * nw], refs[2 * nw], refs[2 * nw + 1], gather, False):
            cp.wait_send()
        for cp in _split_copies(refs[:nw], refs[nw:2 * nw], refs[2 * nw], refs[2 * nw + 1], gather, True):
            cp.wait_recv()

    thru = [pltpu.HBM(a.shape, a.dtype) for a in list(srcs) + list(lands)]
    res = pl.pallas_call(
        body, name=name, out_shape=tuple(thru),
        in_specs=[_HBM] * (2 * nw) + [_SEM, _SEM, _ANY], out_specs=tuple([_HBM] * (2 * nw)),
        input_output_aliases={i: i for i in range(2 * nw)},
        compiler_params=pltpu.CompilerParams(**_SPLIT_COPY),
    )(*srcs, *lands, send_sems, recv_sems, after)
    return list(res[:nw]), list(res[nw:])


def _in_proj(x, tabs, w_in_t, g_mix, b_gate, q_g, k_g, tb, after):
    s, d = x.shape
    n_gate_chunks = d // 256
    q_scale = HEAD_DIM_A ** -0.5 * LOG2_E
    b_scale = HEAD_DIM_B ** -0.5 * LOG2_E

    def body(x_ref, c_ref, s1_ref, s2_ref, w_ref, gmix_ref, bg_ref, qg_ref, kg_ref, after_ref,
             h1_ref, qraw_ref, kraw_ref, qrot_ref, krot_ref, va_ref, *rest):
        qb_refs, kb_refs, vb_refs = rest[0:3], rest[3:6], rest[6:9]
        ga_ref, gb_ref, scr_ref = rest[9:]
        xv = x_ref[...]
        hb = (xv * _rstd(xv) * gmix_ref[...]).astype(BF16)
        h1_ref[...] = hb
        cos, s1, s2 = c_ref[...], s1_ref[...], s2_ref[...]

        def proj(lo, width):
            return _dot_nt(hb, w_ref[lo:lo + width, :])

        def norm_rope(z, g):
            n = z * _rstd(z) * g
            return n * cos + pltpu.roll(n, 32, 1) * s1 + pltpu.roll(n, 96, 1) * s2

        for j in range(QA_W // 256):
            z = proj(OFF_QA + 256 * j, 256)
            qraw_ref[:, 256 * j:256 * j + 256] = z
            for hh in range(2):
                lo = 256 * j + 128 * hh
                qrot_ref[:, lo:lo + 128] = (norm_rope(z[:, 128 * hh:128 * hh + 128], qg_ref[...]) * q_scale).astype(BF16)
        z = proj(OFF_KA, 256)
        kraw_ref[...] = z
        for hh in range(2):
            krot_ref[:, 128 * hh:128 * hh + 128] = norm_rope(z[:, 128 * hh:128 * hh + 128], kg_ref[...]).astype(BF16)
        va_ref[...] = proj(OFF_VA, 256).astype(BF16)
        for g, dil in enumerate(DILATIONS):
            _to_residues(proj(OFF_QB + GB_W * g, GB_W) * b_scale, qb_refs[g], scr_ref, dil, BF16)
            _to_residues(proj(OFF_KB + GB_W * g, GB_W), kb_refs[g], scr_ref, dil, BF16)
            _to_residues(proj(OFF_VB + GB_W * g, GB_W), vb_refs[g], scr_ref, dil, BF16)
        for j in range(n_gate_chunks):
            sl = slice(256 * j, 256 * j + 256)
            ga_ref[:, sl] = _sigmoid(proj(OFF_GA + 256 * j, 256) + bg_ref[:, sl]).astype(BF16)
            gb_ref[:, sl] = _sigmoid(
                proj(OFF_GA + d + 256 * j, 256) + bg_ref[:, d + 256 * j:d + 256 * j + 256]).astype(BF16)

    sd = jax.ShapeDtypeStruct
    outs = [sd((s, d), BF16), sd((s, QA_W), F32), sd((s, KA_W), F32), sd((s, QA_W), BF16), sd((s, KA_W), BF16),
            sd((s, KA_W), BF16)] + _dil_shapes(s, BF16) * 3 + [sd((s, d), BF16), sd((s, d), BF16)]
    out_specs = [_rows(tb, d), _rows(tb, QA_W), _rows(tb, KA_W), _rows(tb, QA_W), _rows(tb, KA_W), _rows(tb, KA_W)
                 ] + _dil_specs(tb) * 3 + [_rows(tb, d), _rows(tb, d)]
    in_specs = [_rows(tb, d), _rows(tb, LANES), _rows(tb, LANES), _rows(tb, LANES), _resident(w_in_t.shape),
                _resident(g_mix.shape), _resident(b_gate.shape), _resident(q_g.shape), _resident(k_g.shape), _ANY]
    res = list(pl.pallas_call(body, name="in_proj", grid=(s // tb,), in_specs=in_specs, out_specs=out_specs,
                              out_shape=outs, scratch_shapes=[pltpu.VMEM((2, tb, LANES), F32)],
                              compiler_params=_cparams(("arbitrary",)))(
        x, *tabs, w_in_t, g_mix, b_gate, q_g, k_g, after))
    return res[:6] + [res[6:9], res[9:12], res[12:15]] + res[15:]


def _attn_a_fwd(qrot, krot, va, tq, tk):
    s = qrot.shape[0]
    n_kv = s // tk
    gw = Q_PER_KV * HEAD_DIM_A

    def body(q_ref, k_ref, v_ref, o_ref, lse_ref):
        q4 = jnp.concatenate([q_ref[:, 128 * h:128 * h + 128] for h in range(Q_PER_KV)], axis=0)

        def step(j, carry):
            m, l, acc = carry
            sl = pl.ds(pl.multiple_of(j * tk, tk), tk)
            kj, vj = k_ref[sl, :], v_ref[sl, :]
            sc = _dot_nt(kj, q4)
            m_new = jnp.maximum(m, jnp.max(sc, axis=0, keepdims=True))
            p = jnp.exp2(sc - m_new)
            alpha = jnp.exp2(m - m_new)
            l = alpha * l + jnp.sum(p, axis=0, keepdims=True)
            acc = alpha * acc + _dot_tn(vj, p.astype(BF16))
            return m_new, l, acc

        rows = Q_PER_KV * tq
        m, l, acc = lax.fori_loop(0, n_kv, step, (jnp.full((1, rows), NEG_INF, F32), jnp.zeros((1, rows), F32),
                                                  jnp.zeros((HEAD_DIM_A, rows), F32)))
        o = (acc / l).T
        lse = m + jnp.log2(l)
        for h in range(Q_PER_KV):
            o_ref[:, 128 * h:128 * h + 128] = o[h * tq:(h + 1) * tq].astype(BF16)
            lse_ref[0, h:h + 1, :] = lse[:, h * tq:(h + 1) * tq]

    return pl.pallas_call(
        body, name="attn_a_fwd", grid=(N_KV_HEADS_A, s // tq),
        in_specs=[pl.BlockSpec((tq, gw), lambda g, i: (i, g)),
                  pl.BlockSpec((s, HEAD_DIM_A), lambda g, i: (0, g)),
                  pl.BlockSpec((s, HEAD_DIM_A), lambda g, i: (0, g))],
        out_specs=[pl.BlockSpec((tq, gw), lambda g, i: (i, g)),
                   pl.BlockSpec((1, Q_PER_KV, tq), lambda g, i: (g, 0, i))],
        out_shape=[jax.ShapeDtypeStruct((s, QA_W), BF16), jax.ShapeDtypeStruct((N_KV_HEADS_A, Q_PER_KV, s), F32)],
        compiler_params=_cparams(("arbitrary", "arbitrary")))(qrot, krot, va)


def _attn_a_bwd(qrot, krot, va, oa, doa, lse, tq, tk, after):
    s = qrot.shape[0]
    n_kv = s // tk
    gw = Q_PER_KV * HEAD_DIM_A

    def body(q_ref, do_ref, o_ref, lse_ref, k_ref, v_ref, after_ref, dq_ref, dk_ref, dv_ref):
        @pl.when(pl.program_id(1) == 0)
        def _():
            dk_ref[...] = jnp.zeros_like(dk_ref)
            dv_ref[...] = jnp.zeros_like(dv_ref)

        def stack(ref):
            return jnp.concatenate([ref[:, 128 * h:128 * h + 128] for h in range(Q_PER_KV)], axis=0)

        q4, do4, o4 = stack(q_ref), stack(do_ref), stack(o_ref)
        delta = jnp.sum(do4.astype(F32) * o4.astype(F32), axis=-1, keepdims=True)
        lse_cols = jnp.concatenate([lse_ref[0], jnp.zeros_like(lse_ref[0])], axis=0).T
        lse4 = jnp.concatenate([lse_cols[:, h:h + 1] for h in range(Q_PER_KV)], axis=0)

        def step(j, dq):
            sl = pl.ds(pl.multiple_of(j * tk, tk), tk)
            kj, vj = k_ref[sl, :], v_ref[sl, :]
            p = jnp.exp2(_dot_nt(q4, kj) - lse4)
            ds = (p * (_dot_nt(do4, vj) - delta)).astype(BF16)
            dk_ref[sl, :] += _dot_tn(ds, q4)
            dv_ref[sl, :] += _dot_tn(p.astype(BF16), do4)
            return dq + _dot_nn(ds, kj)

        dq = lax.fori_loop(0, n_kv, step, jnp.zeros((Q_PER_KV * tq, HEAD_DIM_A), F32))
        for h in range(Q_PER_KV):
            dq_ref[:, 128 * h:128 * h + 128] = dq[h * tq:(h + 1) * tq]

    qspec = pl.BlockSpec((tq, gw), lambda g, i: (i, g))
    kspec = pl.BlockSpec((s, HEAD_DIM_A), lambda g, i: (0, g))
    return pl.pallas_call(
        body, name="attn_a_bwd", grid=(N_KV_HEADS_A, s // tq),
        in_specs=[qspec, qspec, qspec, pl.BlockSpec((1, Q_PER_KV, tq), lambda g, i: (g, 0, i)), kspec, kspec, _ANY],
        out_specs=[qspec, kspec, kspec],
        out_shape=[jax.ShapeDtypeStruct((s, QA_W), F32), jax.ShapeDtypeStruct((s, KA_W), F32),
                   jax.ShapeDtypeStruct((s, KA_W), F32)],
        compiler_params=_cparams(("arbitrary", "arbitrary")))(qrot, doa, oa, lse, krot, va, after)


BAND_QB = 128
BAND_WIN = BAND_QB + 2 * BAND


def _band_specs(s, cb):
    per = cb // BAND
    last = s // BAND - 1
    cur = pl.BlockSpec((cb, GB_W), lambda i: (i, 0))
    prev = pl.BlockSpec((BAND, GB_W), lambda i: (jnp.maximum(i * per - 1, 0), 0))
    nxt = pl.BlockSpec((BAND, GB_W), lambda i: (jnp.minimum(i * per + per, last), 0))
    return cur, prev, nxt


def _window(prev_ref, cur_ref, next_ref):
    return jnp.concatenate([prev_ref[...], cur_ref[...], next_ref[...]], axis=0)


def _band_mask(base, seg_shift):
    rq = base + lax.broadcasted_iota(jnp.int32, (BAND_QB, BAND_WIN), 0)
    rk = base - BAND + lax.broadcasted_iota(jnp.int32, (BAND_QB, BAND_WIN), 1)
    same_segment = lax.shift_right_arithmetic(rq, jnp.int32(seg_shift)) == lax.shift_right_arithmetic(rk, jnp.int32(seg_shift))
    return (jnp.abs(rk - rq) <= BAND) & same_segment


def _build_bias(bmap_ref, tab_ref, bias_ref):
    bm = bmap_ref[...]
    acc = [jnp.full(bm.shape, NEG_INF, F32) for _ in range(N_HEADS_PER_DIL)]
    for b in range(N_REL_BUCKETS):
        hit = bm == b
        for h in range(N_HEADS_PER_DIL):
            acc[h] = jnp.where(hit, tab_ref[b, h] * LOG2_E, acc[h])
    rows = bm.shape[0]
    for h in range(N_HEADS_PER_DIL):
        bias_ref[h * rows:(h + 1) * rows, :] = acc[h]


def _segment_mask(base, seg_len, seg_shift):
    if seg_len % BAND_QB:
        return _band_mask(base, seg_shift)
    pos = lax.rem(base, seg_len)
    w = lax.broadcasted_iota(jnp.int32, (1, BAND_WIN), 1)
    return ((w >= BAND) | (pos != 0)) & ((w < BAND + BAND_QB) | (pos != seg_len - BAND_QB))


def _head_lane_masks():
    lane = lax.broadcasted_iota(jnp.int32, (1, LANES), 1)
    return [lane < HEAD_DIM_B, lane >= HEAD_DIM_B]


def _rows4(mask):
    return mask if mask.shape[0] == 1 else jnp.concatenate([mask] * N_HEADS_PER_DIL, axis=0)


def _head_scores(a, b):
    hm = _head_lane_masks()
    out = []
    for hp in range(2):
        ls = slice(LANES * hp, LANES * hp + LANES)
        ah = a[:, ls]
        both = jnp.concatenate([jnp.where(hm[0], ah, jnp.zeros_like(ah)), jnp.where(hm[1], ah, jnp.zeros_like(ah))],
                               axis=0)
        out.append(_dot_nt(both, b[:, ls]))
    return jnp.concatenate(out, axis=0)


def _head_combine(p, v, scale=None, transposed=False):
    hm = _head_lane_masks()
    rows = p.shape[0] // N_HEADS_PER_DIL
    halves = []
    for hp in range(2):
        vh = v[:, LANES * hp:LANES * hp + LANES]
        acc = None
        for hh in range(2):
            h = 2 * hp + hh
            ph = p[h * rows:(h + 1) * rows]
            vm = jnp.where(hm[hh], vh, jnp.zeros_like(vh))
            t = _dot_tn(ph, vm) if transposed else _dot_nn(ph, vm)
            if scale is not None:
                t = t * scale[h * rows:(h + 1) * rows]
            acc = t if acc is None else acc + t
        halves.append(acc)
    return jnp.concatenate(halves, axis=1)


def _head_spread(col):
    rows = col.shape[0] // N_HEADS_PER_DIL
    lane = lax.broadcasted_iota(jnp.int32, (1, GB_W), 1)
    out = jnp.zeros((rows, GB_W), F32)
    for h in range(N_HEADS_PER_DIL):
        out = jnp.where((lane >= HEAD_DIM_B * h) & (lane < HEAD_DIM_B * (h + 1)), col[h * rows:(h + 1) * rows], out)
    return out


def _head_cols(v):
    return jnp.concatenate([v[:, HEAD_DIM_B * h:HEAD_DIM_B * h + 1] for h in range(N_HEADS_PER_DIL)], axis=0)


def _seg_shift(s, dil):
    seg = s // dil
    assert seg & (seg - 1) == 0, "segment length must be a power of two"
    return seg.bit_length() - 1


def _band_fwd(dil, qb, kb, vb, bmap, tab, cb):
    s = qb.shape[0]
    shift = _seg_shift(s, dil)

    def body(q_ref, kp_ref, kc_ref, kn_ref, vp_ref, vc_ref, vn_ref, bmap_ref, tab_ref, o_ref, lse_ref, bias_ref):
        @pl.when(pl.program_id(0) == 0)
        def _():
            _build_bias(bmap_ref, tab_ref, bias_ref)

        kw, vw = _window(kp_ref, kc_ref, kn_ref), _window(vp_ref, vc_ref, vn_ref)
        for jj in range(cb // BAND_QB):
            r0 = BAND_QB * jj
            mask = _rows4(_segment_mask(pl.program_id(0) * cb + r0, s // dil, shift))
            sc = _head_scores(q_ref[r0:r0 + BAND_QB, :], kw[r0:r0 + BAND_WIN, :]) + bias_ref[...]
            sc = jnp.where(mask, sc, NEG_INF)
            m = jnp.max(sc, axis=-1, keepdims=True)
            e = jnp.exp2(sc - m)
            l = jnp.sum(e, axis=-1, keepdims=True)
            o = _head_combine(e.astype(BF16), vw[r0:r0 + BAND_WIN, :], 1.0 / l)
            o_ref[r0:r0 + BAND_QB, :] = o
            lse_ref[r0:r0 + BAND_QB, :] = _head_spread(m + jnp.log2(l))

    cur, prev, nxt = _band_specs(s, cb)
    return pl.pallas_call(
        body, name=f"band_fwd_d{dil}", grid=(s // cb,),
        in_specs=[cur, prev, cur, nxt, prev, cur, nxt, _resident(bmap.shape), pl.BlockSpec(memory_space=pltpu.SMEM)],
        out_specs=[cur, cur],
        out_shape=[jax.ShapeDtypeStruct(qb.shape, F32), jax.ShapeDtypeStruct(qb.shape, F32)],
        scratch_shapes=[pltpu.VMEM((N_HEADS_PER_DIL * BAND_QB, BAND_WIN), F32)],
        compiler_params=_cparams(("arbitrary",)))(qb, kb, kb, kb, vb, vb, vb, bmap, tab)


def _band_bwd(dil, qb, kb, vb, dob, lse, dd, bmap, tab, cb):
    s = qb.shape[0]
    shift = _seg_shift(s, dil)
    n_steps = s // cb

    def body(q_ref, do_ref, lse_ref, dd_ref, kp_ref, kc_ref, kn_ref, vp_ref, vc_ref, vn_ref, bmap_ref, tab_ref,
             dq_ref, dk_ref, dv_ref, dtab_ref, bias_ref, dsum_ref):
        @pl.when(pl.program_id(0) == 0)
        def _():
            _build_bias(bmap_ref, tab_ref, bias_ref)
            dsum_ref[...] = jnp.zeros_like(dsum_ref)
            dk_ref[...] = jnp.zeros_like(dk_ref)
            dv_ref[...] = jnp.zeros_like(dv_ref)

        kw, vw = _window(kp_ref, kc_ref, kn_ref), _window(vp_ref, vc_ref, vn_ref)
        for jj in range(cb // BAND_QB):
            r0 = BAND_QB * jj
            base = pl.program_id(0) * cb + r0
            mask = _rows4(_segment_mask(base, s // dil, shift))
            qh, doh = q_ref[r0:r0 + BAND_QB, :], do_ref[r0:r0 + BAND_QB, :]
            k3, v3 = kw[r0:r0 + BAND_WIN, :], vw[r0:r0 + BAND_WIN, :]
            sc = _head_scores(qh, k3) + bias_ref[...]
            sc = jnp.where(mask, sc, NEG_INF)
            p = jnp.exp2(sc - _head_cols(lse_ref[r0:r0 + BAND_QB, :]))
            dp = _head_scores(doh, v3)
            ds = p * (dp - _head_cols(dd_ref[r0:r0 + BAND_QB, :]))
            dsum_ref[...] += ds
            dsb = ds.astype(BF16)
            dq_ref[r0:r0 + BAND_QB, :] = _head_combine(dsb, k3)
            dk_win = _head_combine(dsb, qh, transposed=True)
            dv_win = _head_combine(p.astype(BF16), doh, transposed=True)
            own = pl.ds(pl.multiple_of(base, BAND), BAND_QB)
            dk_ref[own, :] += dk_win[BAND:BAND + BAND_QB]
            dv_ref[own, :] += dv_win[BAND:BAND + BAND_QB]

            @pl.when(base > 0)
            def _():
                before = pl.ds(pl.multiple_of(base - BAND, BAND), BAND)
                dk_ref[before, :] += dk_win[:BAND]
                dv_ref[before, :] += dv_win[:BAND]

            @pl.when(base + BAND_QB < s)
            def _():
                after = pl.ds(pl.multiple_of(base + BAND_QB, BAND), BAND)
                dk_ref[after, :] += dk_win[BAND + BAND_QB:]
                dv_ref[after, :] += dv_win[BAND + BAND_QB:]

        @pl.when(pl.program_id(0) == n_steps - 1)
        def _():
            bm = bmap_ref[...]
            lane = lax.broadcasted_iota(jnp.int32, (1, LANES), 1)
            for b in range(N_REL_BUCKETS):
                hit = bm == b
                row = jnp.zeros((1, LANES), F32)
                for h in range(N_HEADS_PER_DIL):
                    part = dsum_ref[h * BAND_QB:(h + 1) * BAND_QB, :]
                    row = jnp.where(lane == h, jnp.sum(jnp.where(hit, part, 0.0)), row)
                dtab_ref[b:b + 1, :] = row

    cur, prev, nxt = _band_specs(s, cb)
    whole = _acc_spec(qb.shape)
    return pl.pallas_call(
        body, name=f"band_bwd_d{dil}", grid=(n_steps,),
        in_specs=[cur, cur, cur, cur, prev, cur, nxt, prev, cur, nxt, _resident(bmap.shape),
                  pl.BlockSpec(memory_space=pltpu.SMEM)],
        out_specs=[cur, whole, whole, _acc_spec((N_REL_BUCKETS, LANES))],
        out_shape=[jax.ShapeDtypeStruct(qb.shape, F32)] * 3 + [jax.ShapeDtypeStruct((N_REL_BUCKETS, LANES), F32)],
        scratch_shapes=[pltpu.VMEM((N_HEADS_PER_DIL * BAND_QB, BAND_WIN), F32),
                        pltpu.VMEM((N_HEADS_PER_DIL * BAND_QB, BAND_WIN), F32)],
        compiler_params=_cparams(("arbitrary",)))(qb, dob, lse, dd, kb, kb, kb, vb, vb, vb, bmap, tab)


def _t5_bucket(rel):
    nb = N_REL_BUCKETS // 2
    ret = (rel > 0).astype(np.int32) * nb
    n = np.abs(rel)
    max_exact = nb // 2
    large = max_exact + (np.log(np.maximum(n, 1) / max_exact) / math.log(REL_MAX_DIST / max_exact)
                         * (nb - max_exact)).astype(np.int32)
    large = np.minimum(large, nb - 1)
    return ret + np.where(n < max_exact, n, large).astype(np.int32)


def _bucket_map(dil):
    off = np.arange(BAND_WIN)[None, :] - BAND - np.arange(BAND_QB)[:, None]
    return np.where(np.abs(off) <= BAND, _t5_bucket(off * dil), -1).astype(np.int32)


def _seg_sum(v):
    lane = lax.broadcasted_iota(jnp.int32, (1, v.shape[1]), 1)
    out = jnp.zeros_like(v)
    for h in range(v.shape[1] // HEAD_DIM_B):
        m = (lane >= HEAD_DIM_B * h) & (lane < HEAD_DIM_B * (h + 1))
        out = jnp.where(m, jnp.sum(jnp.where(m, v, 0.0), axis=-1, keepdims=True), out)
    return out


def _mix_out(x, oa, og, lg, ga, gb, w_oa, w_ob_t, w_o, tb):
    s, d = x.shape

    def body(x_ref, oa_ref, og0_ref, og1_ref, og2_ref, lg0_ref, lg1_ref, lg2_ref, ga_ref, gb_ref,
             woa_ref, wob_ref, wo_ref, x2_ref, ob_ref, lse0_ref, lse1_ref, lse2_ref, ya_ref, yb_ref, u_ref, scr_ref):
        og_refs, lg_refs = (og0_ref, og1_ref, og2_ref), (lg0_ref, lg1_ref, lg2_ref)
        l0, l1, l2 = [_from_residues(lg_refs[g], scr_ref, dil) for g, dil in enumerate(DILATIONS)]
        lmax = jnp.maximum(jnp.maximum(l0, l1), l2)
        w0, w1, w2 = jnp.exp2(l0 - lmax), jnp.exp2(l1 - lmax), jnp.exp2(l2 - lmax)
        den = w0 + w1 + w2
        o0, o1, o2 = [_from_residues(og_refs[g], scr_ref, dil) for g, dil in enumerate(DILATIONS)]
        ob = ((w0 * o0 + w1 * o1 + w2 * o2) / den).astype(BF16)
        ob_ref[...] = ob
        lse = lmax + jnp.log2(den)
        for g, (dil, ref) in enumerate(zip(DILATIONS, (lse0_ref, lse1_ref, lse2_ref))):
            _to_residues(lse, ref, scr_ref, dil, F32)
        ya = _dot_nn(oa_ref[...], woa_ref[...])
        yb = _dot_nt(ob, wob_ref[...])
        ya_ref[...] = ya.astype(BF16)
        yb_ref[...] = yb.astype(BF16)
        u = (ga_ref[...].astype(F32) * ya + gb_ref[...].astype(F32) * yb).astype(BF16)
        u_ref[...] = u
        x2_ref[...] = x_ref[...] + _dot_nn(u, wo_ref[...])

    sd = jax.ShapeDtypeStruct
    res = list(pl.pallas_call(
        body, name="mix_out", grid=(s // tb,),
        in_specs=[_rows(tb, d), _rows(tb, QA_W)] + _dil_specs(tb) * 2 + [
            _rows(tb, d), _rows(tb, d), _resident(w_oa.shape), _resident(w_ob_t.shape), _resident(w_o.shape)],
        out_specs=[_rows(tb, d), _rows(tb, GB_W)] + _dil_specs(tb) + [_rows(tb, d), _rows(tb, d), _rows(tb, d)],
        out_shape=[sd((s, d), F32), sd((s, GB_W), BF16)] + _dil_shapes(s, F32) + [
            sd((s, d), BF16), sd((s, d), BF16), sd((s, d), BF16)],
        scratch_shapes=[pltpu.VMEM((2, tb, LANES), F32)],
        compiler_params=_cparams(("arbitrary",)))(x, oa, *og, *lg, ga, gb, w_oa, w_ob_t, w_o))
    return res[:2] + [res[2:5]] + res[5:]


def _mlp_fwd(x2, w1_t, w2, g_mlp, tb, tc):
    s, d = x2.shape
    dff = w1_t.shape[0]

    def body(x_ref, w1_ref, w2_ref, g_ref, x3_ref, r_ref, h_ref):
        xv = x_ref[...]
        hb = (xv * _rstd(xv) * g_ref[...]).astype(BF16)
        h_ref[...] = hb
        x3_ref[...] = xv
        for c in range(dff // tc):
            sl = slice(tc * c, tc * c + tc)
            r = jnp.maximum(_dot_nt(hb, w1_ref[sl, :]), 0.0)
            r_ref[:, sl] = r.astype(BF16)
            x3_ref[...] += _dot_nn((r * r).astype(BF16), w2_ref[sl, :])

    sd = jax.ShapeDtypeStruct
    return pl.pallas_call(
        body, name="mlp_fwd", grid=(s // tb,),
        in_specs=[_rows(tb, d), _resident(w1_t.shape), _resident(w2.shape), _resident(g_mlp.shape)],
        out_specs=[_rows(tb, d), _rows(tb, dff), _rows(tb, d)],
        out_shape=[sd((s, d), F32), sd((s, dff), BF16), sd((s, d), BF16)],
        compiler_params=_cparams(("arbitrary",)))(x2, w1_t, w2, g_mlp)


def _ple_loss(x3, p, target, w_pg, w_p_t, g_ple, g_fin, tb):
    s, d = x3.shape
    dp = p.shape[1]

    def body(x_ref, p_ref, t_ref, wpg_ref, wp_ref, gple_ref, gfin_ref,
             dx3_ref, h3_ref, dpre_ref, dpe_ref, pb_ref, loss_ref, dgfin_ref, dgple_ref):
        @pl.when(pl.program_id(0) == 0)
        def _():
            loss_ref[...] = jnp.zeros_like(loss_ref)
            dgfin_ref[...] = jnp.zeros_like(dgfin_ref)
            dgple_ref[...] = jnp.zeros_like(dgple_ref)

        x3v = x_ref[...]
        r3 = _rstd(x3v)
        n3 = x3v * r3
        h3 = (n3 * gple_ref[...]).astype(BF16)
        h3_ref[...] = h3
        gp = _sigmoid(_dot_nn(h3, wpg_ref[...]))
        pb = p_ref[...].astype(BF16)
        pb_ref[...] = pb
        pe = _dot_nt(pb, wp_ref[...])
        x4 = x3v + gp * pe
        r4 = _rstd(x4)
        n4 = x4 * r4
        err = n4 * gfin_ref[...] - t_ref[...]
        loss_ref[...] += jnp.sum(0.5 * jnp.mean(err * err, axis=-1, keepdims=True), axis=0, keepdims=True)
        dy = err * (1.0 / d)
        dgfin_ref[...] += _colsum(dy * n4)
        dx4 = _rms_bwd(dy, n4, r4, gfin_ref[...])
        dpe_ref[...] = (dx4 * gp).astype(BF16)
        dpre = (dx4 * pe * gp * (1.0 - gp)).astype(BF16)
        dpre_ref[...] = dpre
        dh3 = _dot_nt(dpre, wpg_ref[...])
        dgple_ref[...] += _colsum(dh3 * n3)
        dx3_ref[...] = dx4 + _rms_bwd(dh3, n3, r3, gple_ref[...])

    sd = jax.ShapeDtypeStruct
    return pl.pallas_call(
        body, name="ple_loss", grid=(s // tb,),
        in_specs=[_rows(tb, d), _rows(tb, dp), _rows(tb, d), _resident(w_pg.shape), _resident(w_p_t.shape),
                  _resident(g_ple.shape), _resident(g_fin.shape)],
        out_specs=[_rows(tb, d), _rows(tb, d), _rows(tb, d), _rows(tb, d), _rows(tb, dp),
                   _acc_spec((1, LANES)), _acc_spec((1, d)), _acc_spec((1, d))],
        out_shape=[sd((s, d), F32), sd((s, d), BF16), sd((s, d), BF16), sd((s, d), BF16), sd((s, dp), BF16),
                   sd((1, LANES), F32), sd((1, d), F32), sd((1, d), F32)],
        compiler_params=_cparams(("arbitrary",)))(x3, p, target, w_pg, w_p_t, g_ple, g_fin)


def _mlp_bwd(dx3, x2, r, w1_t, w2, g_mlp, tb, tc):
    s, d = x2.shape
    dff = w1_t.shape[0]

    def body(dx3_ref, x_ref, r_ref, w1_ref, w2_ref, g_ref, dx2_ref, df_ref, dg_ref, dh_ref):
        @pl.when(pl.program_id(0) == 0)
        def _():
            dg_ref[...] = jnp.zeros_like(dg_ref)

        dx3v = dx3_ref[...]
        dx3b = dx3v.astype(BF16)
        dh_ref[...] = jnp.zeros_like(dh_ref)
        for c in range(dff // tc):
            sl = slice(tc * c, tc * c + tc)
            df = (_dot_nt(dx3b, w2_ref[sl, :]) * (2.0 * r_ref[:, sl].astype(F32))).astype(BF16)
            df_ref[:, sl] = df
            dh_ref[...] += _dot_nn(df, w1_ref[sl, :])
        xv = x_ref[...]
        r2 = _rstd(xv)
        n2 = xv * r2
        dh = dh_ref[...]
        dg_ref[...] += _colsum(dh * n2)
        dx2_ref[...] = dx3v + _rms_bwd(dh, n2, r2, g_ref[...])

    sd = jax.ShapeDtypeStruct
    return pl.pallas_call(
        body, name="mlp_bwd", grid=(s // tb,),
        in_specs=[_rows(tb, d), _rows(tb, d), _rows(tb, dff), _resident(w1_t.shape), _resident(w2.shape),
                  _resident(g_mlp.shape)],
        out_specs=[_rows(tb, d), _rows(tb, dff), _acc_spec((1, d))],
        out_shape=[sd((s, d), F32), sd((s, dff), BF16), sd((1, d), F32)],
        scratch_shapes=[pltpu.VMEM((tb, d), F32)],
        compiler_params=_cparams(("arbitrary",)))(dx3, x2, r, w1_t, w2, g_mlp)


def _mix_out_bwd(dx2, ya, yb, ga, gb, ob, w_oa, w_ob_t, w_o, tb, after):
    s, d = dx2.shape

    def body(dx_ref, ya_ref, yb_ref, ga_ref, gb_ref, ob_ref, woa_ref, wob_ref, wo_ref, after_ref,
             doa_ref, dob0_ref, dob1_ref, dob2_ref, dd0_ref, dd1_ref, dd2_ref, dga_ref, dgb_ref, dya_ref, dyb_ref,
             dbg_ref, scr_ref):
        @pl.when(pl.program_id(0) == 0)
        def _():
            dbg_ref[...] = jnp.zeros_like(dbg_ref)

        du = _dot_nt(dx_ref[...].astype(BF16), wo_ref[...])
        gav, gbv = ga_ref[...].astype(F32), gb_ref[...].astype(F32)
        dya = (du * gav).astype(BF16)
        dyb = (du * gbv).astype(BF16)
        dya_ref[...] = dya
        dyb_ref[...] = dyb
        dga = du * ya_ref[...].astype(F32) * gav * (1.0 - gav)
        dgb = du * yb_ref[...].astype(F32) * gbv * (1.0 - gbv)
        dga_ref[...] = dga.astype(BF16)
        dgb_ref[...] = dgb.astype(BF16)
        dbg_ref[:, 0:d] += _colsum(dga)
        dbg_ref[:, d:2 * d] += _colsum(dgb)
        doa_ref[...] = _dot_nt(dya, woa_ref[...]).astype(BF16)
        dob = _dot_nn(dyb, wob_ref[...])
        dd = _seg_sum(dob * ob_ref[...].astype(F32))
        for dil, dob_ref, dd_ref in zip(DILATIONS, (dob0_ref, dob1_ref, dob2_ref), (dd0_ref, dd1_ref, dd2_ref)):
            _to_residues(dob, dob_ref, scr_ref, dil, BF16)
            _to_residues(dd, dd_ref, scr_ref, dil, F32)

    sd = jax.ShapeDtypeStruct
    res = list(pl.pallas_call(
        body, name="mix_out_bwd", grid=(s // tb,),
        in_specs=[_rows(tb, d)] * 5 + [_rows(tb, GB_W), _resident(w_oa.shape), _resident(w_ob_t.shape),
                                       _resident(w_o.shape), _ANY],
        out_specs=[_rows(tb, QA_W)] + _dil_specs(tb) * 2 + [_rows(tb, d), _rows(tb, d), _rows(tb, d),
                                                           _rows(tb, d), _acc_spec((1, 2 * d))],
        out_shape=[sd((s, QA_W), BF16)] + _dil_shapes(s, BF16) + _dil_shapes(s, F32) + [
            sd((s, d), BF16), sd((s, d), BF16), sd((s, d), BF16), sd((s, d), BF16), sd((1, 2 * d), F32)],
        scratch_shapes=[pltpu.VMEM((2, tb, LANES), F32)],
        compiler_params=_cparams(("arbitrary",)))(dx2, ya, yb, ga, gb, ob, w_oa, w_ob_t, w_o, after))
    return res[:1] + [res[1:4], res[4:7]] + res[7:]


def _in_proj_bwd(dx2, x, dqrot, dkrot, dva, qraw, kraw, tabs, dqb, dkb, dvb, dga, dgb, w_in_t, g_mix, q_g, k_g, tb):
    s, d = x.shape
    din = w_in_t.shape[0]
    q_scale = HEAD_DIM_A ** -0.5
    b_scale = HEAD_DIM_B ** -0.5
    tc = 256

    def body(dx2_ref, x_ref, dq_ref, dk_ref, dv_ref, qraw_ref, kraw_ref, c_ref, s1_ref, s2_ref, *rest):
        dqb_refs, dkb_refs, dvb_refs = rest[0:3], rest[3:6], rest[6:9]
        (dga_ref, dgb_ref, w_ref, gmix_ref, qg_ref, kg_ref,
         dx_ref, dz_ref, dgmix_ref, dqg_ref, dkg_ref, dh_ref, scr_ref) = rest[9:]

        @pl.when(pl.program_id(0) == 0)
        def _():
            dgmix_ref[...] = jnp.zeros_like(dgmix_ref)
            dqg_ref[...] = jnp.zeros_like(dqg_ref)
            dkg_ref[...] = jnp.zeros_like(dkg_ref)

        cos, s1, s2 = c_ref[...][None], s1_ref[...][None], s2_ref[...][None]

        def heads_bwd(drot, z, g_ref, acc_ref):
            dn = drot * cos + pltpu.roll(drot * s1, 96, 2) + pltpu.roll(drot * s2, 32, 2)
            rr = _rstd(z)
            nn = z * rr
            acc_ref[...] += jnp.sum(jnp.sum(dn * nn, axis=0), axis=0, keepdims=True)
            return _rms_bwd(dn, nn, rr, g_ref[...][None]).astype(BF16)

        dh_ref[...] = jnp.zeros_like(dh_ref)

        def emit(off, piece):
            dz_ref[:, off:off + tc] = piece
            dh_ref[...] += _dot_nn(piece, w_ref[off:off + tc, :])

        for j in range(d // tc):
            emit(OFF_GA + tc * j, dga_ref[:, tc * j:tc * j + tc])
            emit(OFF_GA + d + tc * j, dgb_ref[:, tc * j:tc * j + tc])
        emit(OFF_VA, dv_ref[...].astype(BF16))
        for g, dil in enumerate(DILATIONS):
            emit(OFF_QB + GB_W * g, (_from_residues(dqb_refs[g], scr_ref, dil) * b_scale).astype(BF16))
            emit(OFF_KB + GB_W * g, (_from_residues(dkb_refs[g], scr_ref, dil) * LN_2).astype(BF16))
            emit(OFF_VB + GB_W * g, _from_residues(dvb_refs[g], scr_ref, dil).astype(BF16))
        stack = lambda ref, n: jnp.stack([ref[:, 128 * h:128 * h + 128] for h in range(n)], axis=0)
        dzq = heads_bwd(stack(dq_ref, N_Q_HEADS_A) * q_scale, stack(qraw_ref, N_Q_HEADS_A), qg_ref, dqg_ref)
        dzk = heads_bwd(stack(dk_ref, N_KV_HEADS_A) * LN_2, stack(kraw_ref, N_KV_HEADS_A), kg_ref, dkg_ref)
        for j in range(N_Q_HEADS_A // 2):
            emit(OFF_QA + tc * j, jnp.concatenate([dzq[2 * j], dzq[2 * j + 1]], axis=1))
        emit(OFF_KA, jnp.concatenate([dzk[0], dzk[1]], axis=1))
        xv = x_ref[...]
        r1 = _rstd(xv)
        n1 = xv * r1
        dh = dh_ref[...]
        dgmix_ref[...] += _colsum(dh * n1)
        dx_ref[...] = dx2_ref[...] + _rms_bwd(dh, n1, r1, gmix_ref[...])

    sd = jax.ShapeDtypeStruct
    return pl.pallas_call(
        body, name="in_proj_bwd", grid=(s // tb,),
        in_specs=[_rows(tb, d), _rows(tb, d), _rows(tb, QA_W), _rows(tb, KA_W), _rows(tb, KA_W), _rows(tb, QA_W),
                  _rows(tb, KA_W), _rows(tb, LANES), _rows(tb, LANES), _rows(tb, LANES),
                  ] + _dil_specs(tb) * 3 + [_rows(tb, d), _rows(tb, d),
                  _resident(w_in_t.shape), _resident(g_mix.shape), _resident(q_g.shape), _resident(k_g.shape)],
        out_specs=[_rows(tb, d), _rows(tb, din), _acc_spec((1, d)), _acc_spec((1, HEAD_DIM_A)),
                   _acc_spec((1, HEAD_DIM_A))],
        out_shape=[sd((s, d), F32), sd((s, din), BF16), sd((1, d), F32), sd((1, HEAD_DIM_A), F32),
                   sd((1, HEAD_DIM_A), F32)],
        scratch_shapes=[pltpu.VMEM((tb, d), F32), pltpu.VMEM((2, tb, LANES), F32)],
        compiler_params=_cparams(("arbitrary",)))(
        dx2, x, dqrot, dkrot, dva, qraw, kraw, *tabs, *dqb, *dkb, *dvb, dga, dgb, w_in_t, g_mix, q_g, k_g)


def _identity(v):
    return v


def _to_bf16(v):
    return v.astype(BF16)


def _square_bf16(v):
    vf = v.astype(F32)
    return (vf * vf).astype(BF16)


def _weight_grad(name, a, b, ti, tj, tk, a_fn=_identity, b_fn=_identity, col0=0, n=None, after=None):
    t, m = a.shape
    n = b.shape[1] if n is None else n
    n_k = t // tk
    after = a if after is None else after

    def body(a_ref, b_ref, after_ref, o_ref, acc_ref):
        k = pl.program_id(2)

        @pl.when(k == 0)
        def _():
            acc_ref[...] = jnp.zeros_like(acc_ref)

        acc_ref[...] += _dot_tn(a_fn(a_ref[...]), b_fn(b_ref[...]))

        @pl.when(k == n_k - 1)
        def _():
            o_ref[...] = acc_ref[...].astype(BF16)

    return pl.pallas_call(
        body, name=name, grid=(m // ti, n // tj, n_k),
        in_specs=[pl.BlockSpec((tk, ti), lambda i, j, k: (k, i)),
                  pl.BlockSpec((tk, tj), lambda i, j, k: (k, j + col0 // tj)), _ANY],
        out_specs=pl.BlockSpec((ti, tj), lambda i, j, k: (i, j)),
        out_shape=jax.ShapeDtypeStruct((m, n), BF16),
        scratch_shapes=[pltpu.VMEM((ti, tj), F32)],
        compiler_params=_cparams(("arbitrary", "arbitrary", "arbitrary")))(a, b, after)


def _sum_slots(name, recv, own, transposed):
    m, n, k = recv.shape
    tc = min(k, 256)
    n_pad = -(-n // LANES) * LANES

    def body(own_ref, r_ref, o_ref):
        acc = own_ref[...].astype(F32)
        for i in range(m):
            acc = acc + r_ref[i].astype(F32)
        if transposed:
            if n_pad != n:
                acc = jnp.concatenate([acc, jnp.zeros((n_pad - n, tc), F32)], axis=0)
            acc = acc.T[:, :n]
        o_ref[...] = acc

    out_spec, out_shape = ((pl.BlockSpec((tc, n), lambda j: (j, 0)), (k, n)) if transposed
                           else (pl.BlockSpec((n, tc), lambda j: (0, j)), (n, k)))
    return pl.pallas_call(
        body, name=name, grid=(k // tc,),
        in_specs=[pl.BlockSpec((n, tc), lambda j: (0, j)), pl.BlockSpec((m, n, tc), lambda j: (0, 0, j))],
        out_specs=out_spec, out_shape=jax.ShapeDtypeStruct(out_shape, F32),
        compiler_params=_cparams(("arbitrary",)))(own, recv)


def _adamw_math(w, g, m, v):
    m = ADAM_B1 * m + (1.0 - ADAM_B1) * g
    v = ADAM_B2 * v + (1.0 - ADAM_B2) * (g * g)
    m_hat = m / (1.0 - ADAM_B1 ** ADAM_STEP)
    v_hat = v / (1.0 - ADAM_B2 ** ADAM_STEP)
    delta = -ADAM_LR * (m_hat / (jnp.sqrt(v_hat) + ADAM_EPS) + ADAM_WD * w)
    return delta, m, v


def _adamw(name, w, g, m, v):
    r, c = w.shape
    tr = max(t for t in range(8, min(r, 256) + 1, 8) if r % t == 0)

    def body(w_ref, g_ref, m_ref, v_ref, d_ref, mo_ref, vo_ref):
        d_ref[...], mo_ref[...], vo_ref[...] = _adamw_math(w_ref[...], g_ref[...], m_ref[...], v_ref[...])

    spec = pl.BlockSpec((tr, c), lambda i: (i, 0))
    return pl.pallas_call(
        body, name=name, grid=(r // tr,), in_specs=[spec] * 4, out_specs=[spec] * 3,
        out_shape=[jax.ShapeDtypeStruct((r, c), F32)] * 3,
        compiler_params=_cparams(("arbitrary",)))(w, g, m, v)


def _small_update(parts, w, m, v):
    def body(p_ref, w_ref, m_ref, v_ref, g_ref, d_ref, mo_ref, vo_ref):
        g = p_ref[0]
        for i in range(1, N_DEV):
            g = g + p_ref[i]
        g_ref[...] = g
        d_ref[...], mo_ref[...], vo_ref[...] = _adamw_math(w_ref[...], g, m_ref[...], v_ref[...])

    return pl.pallas_call(body, name="small_update", out_shape=[jax.ShapeDtypeStruct(w.shape, F32)] * 4)(
        parts, w, m, v)


def _pack_rows(vectors, n_rows):
    flat = jnp.concatenate([v.reshape(-1).astype(F32) for v in vectors])
    flat = jnp.pad(flat, (0, n_rows * LANES - flat.shape[0]))
    return flat.reshape(n_rows, LANES)


def _pick_tile(n, prefs):
    for t in prefs:
        if n % t == 0:
            return t
    return n


def kernel(x, p, norm_mix_g, w_in, b_gate, q_norm_g, k_norm_g, rel_bias, w_out_a, w_out_b, w_out, norm_mlp_g, w_ff1, w_ff2, norm_ple_g, w_ple_gate, w_ple, final_norm_g, loss_target, m_norm_mix_g, m_w_in, m_b_gate, m_q_norm_g, m_k_norm_g, m_rel_bias, m_w_out_a, m_w_out_b, m_w_out, m_norm_mlp_g, m_w_ff1, m_w_ff2, m_norm_ple_g, m_w_ple_gate, m_w_ple, m_final_norm_g, v_norm_mix_g, v_w_in, v_b_gate, v_q_norm_g, v_k_norm_g, v_rel_bias, v_w_out_a, v_w_out_b, v_w_out, v_norm_mlp_g, v_w_ff1, v_w_ff2, v_norm_ple_g, v_w_ple_gate, v_w_ple, v_final_norm_g):
    s, d = x.shape[1], x.shape[2]
    xs, ps, ts = x[0], p[0, 0], loss_target[0]
    tb = _pick_tile(s, (512, 256))
    tq = _pick_tile(s, (256,))
    tk = _pick_tile(s, (1024, 512))
    cb = _pick_tile(s, (1024, 512))
    fin_g = final_norm_g.reshape(1, d)

    col_sharded = {"w_in": w_in[0], "w_out_b": w_out_b[0], "w_ff1": w_ff1[0], "w_ple": w_ple[0]}
    row_sharded = {"w_out_a": w_out_a[0], "w_out": w_out[0], "w_ff2": w_ff2[0], "w_ple_gate": w_ple_gate[0]}
    order = ["w_in", "w_out_a", "w_out_b", "w_out", "w_ff1", "w_ff2", "w_ple_gate", "w_ple"]
    shards = [(col_sharded[n].T if n in col_sharded else row_sharded[n]).astype(BF16) for n in order]
    my_idx = 4 * lax.axis_index("x") + 2 * lax.axis_index("y") + lax.axis_index("c")
    (w_in_t,) = _all_gather(shards[:1], 1)
    zones = _place_own_rows(shards[1:], my_idx)
    ag = _copies_start("weights_gather_start", shards[1:], zones, w_in_t, True)

    tabs = _rope_tables(s)
    (h1, qraw, kraw, qrot, krot, va, qb, kb, vb, ga, gb) = _in_proj(
        xs, tabs, w_in_t, norm_mix_g, b_gate, q_norm_g, k_norm_g, tb, ag[4])
    oa, lse_a = _attn_a_fwd(qrot, krot, va, _pick_tile(s, (1024, 512, 256)), tk)
    _, (w_oa, w_ob_t, w_o, w_ff1_t, w_ff2_f, w_pg, w_p_t) = _copies_wait(
        "weights_gather_wait", ag[0], ag[1], ag[2], ag[3], lse_a, True)
    flat = lambda arrs: [a.reshape(s, GB_W) for a in arrs]
    split = lambda arrs: [a.reshape(dil, s // dil, GB_W) for a, dil in zip(arrs, DILATIONS)]
    qb_r, kb_r, vb_r = flat(qb), flat(kb), flat(vb)
    bmaps = [jnp.asarray(_bucket_map(dil)) for dil in DILATIONS]
    bias_tabs = [rel_bias[:, N_HEADS_PER_DIL * g:N_HEADS_PER_DIL * (g + 1)] for g in range(3)]
    band_out = [_band_fwd(dil, qb_r[g], kb_r[g], vb_r[g], bmaps[g], bias_tabs[g], cb)
                for g, dil in enumerate(DILATIONS)]
    og, lg = split([o for o, _ in band_out]), split([l for _, l in band_out])
    x2, ob, lse_b, ya, yb, u = _mix_out(xs, oa, og, lg, ga, gb, w_oa, w_ob_t, w_o, tb)
    tc = _pick_tile(w_ff1_t.shape[0], (512,))
    x3, r_act, h2 = _mlp_fwd(x2, w_ff1_t, w_ff2_f, norm_mlp_g, tb, tc)

    dx3, h3, dpre, dpe, pb, loss_part, dg_fin, dg_ple = _ple_loss(
        x3, ps, ts, w_pg, w_p_t, norm_ple_g, fin_g, tb)
    dx2, df, dg_mlp = _mlp_bwd(dx3, x2, r_act, w_ff1_t, w_ff2_f, norm_mlp_g, tb, tc)

    tkk = _pick_tile(s, (1024, 512))
    tk2 = _pick_tile(s, (2048, 1024, 512))
    dff = w_ff1_t.shape[0]
    t1k = lambda n: _pick_tile(n, (1024, 512, 256))
    slots = lambda parts: [lax.empty((7, a.shape[0] // N_DEV, a.shape[1]), BF16) for a in parts]
    part1 = [_weight_grad("grad_w_ff1", df, h2, t1k(dff), t1k(d), tkk),
             _weight_grad("grad_w_ff2", r_act, dx3, t1k(dff), t1k(d), tkk, a_fn=_square_bf16, b_fn=_to_bf16),
             _weight_grad("grad_w_ple_gate", h3, dpre, t1k(d), t1k(d), tk2),
             _weight_grad("grad_w_ple", dpe, pb, t1k(d), ps.shape[1], tk2)]
    doa, dob, dd, dga, dgb, dya, dyb, dbg = _mix_out_bwd(dx2, ya, yb, ga, gb, ob, w_oa, w_ob_t, w_o, tb, dx2)
    part1 += [_weight_grad("grad_w_out_a", oa, dya, t1k(QA_W), t1k(d), tk2),
              _weight_grad("grad_w_out_b", dyb, ob, t1k(d), GB_W, tk2),
              _weight_grad("grad_w_out", u, dx2, t1k(d), t1k(d), tkk, b_fn=_to_bf16)]
    rs1 = _copies_start("grads1_start", part1, slots(part1), doa, False)
    dqrot, dkrot, dva = _attn_a_bwd(qrot, krot, va, oa, doa, lse_a, tq, tk, rs1[4])
    dob_r, lse_r, dd_r = flat(dob), flat(lse_b), flat(dd)
    band_bwd = [_band_bwd(dil, qb_r[g], kb_r[g], vb_r[g], dob_r[g], lse_r[g], dd_r[g], bmaps[g], bias_tabs[g], cb)
                for g, dil in enumerate(DILATIONS)]
    dqb, dkb, dvb = [split([r[j] for r in band_bwd]) for j in range(3)]
    grad_x, dz, dg_mix, dg_q, dg_k = _in_proj_bwd(
        dx2, xs, dqrot, dkrot, dva, qraw, kraw, tabs, dqb, dkb, dvb, dga, dgb, w_in_t, norm_mix_g,
        q_norm_g, k_norm_g, _pick_tile(s, (256,)))
    d_rel = jnp.concatenate([r[3][:, :N_HEADS_PER_DIL] for r in band_bwd], axis=1)

    din = w_in_t.shape[0]
    ti_in = _pick_tile(din, (din // 2,)) if (din // 2) % LANES == 0 else din
    hd_ = d // 2
    part3 = [_weight_grad("grad_w_in_lo", dz, h1, ti_in, t1k(hd_), tkk, n=hd_)]
    rs3 = _copies_start("grads3_start", part3, slots(part3), grad_x, False)
    part4 = [_weight_grad("grad_w_in_hi", dz, h1, ti_in, t1k(hd_), tkk, col0=hd_, n=hd_, after=rs3[4])]
    rs4 = _copies_start("grads4_start", part4, slots(part4), rs3[4], False)

    def own_rows(a):
        n = a.shape[0] // N_DEV
        return lax.dynamic_slice(a, (my_idx * n, 0), (n, a.shape[1]))

    sums = {}
    src1, got1 = _copies_wait("grads1_wait", rs1[0], rs1[1], rs1[2], rs1[3], rs4[4], False)
    for n, a, r in zip(["w_ff1", "w_ff2", "w_ple_gate", "w_ple", "w_out_a", "w_out_b", "w_out"], src1, got1):
        sums[n] = _sum_slots("sum_" + n, r, own_rows(a), n in col_sharded)
    given_w = dict(w_in=w_in, w_out_a=w_out_a, w_out_b=w_out_b, w_out=w_out, w_ff1=w_ff1, w_ff2=w_ff2,
                   w_ple_gate=w_ple_gate, w_ple=w_ple)
    given_m = dict(w_in=m_w_in, w_out_a=m_w_out_a, w_out_b=m_w_out_b, w_out=m_w_out, w_ff1=m_w_ff1, w_ff2=m_w_ff2,
                   w_ple_gate=m_w_ple_gate, w_ple=m_w_ple)
    given_v = dict(w_in=v_w_in, w_out_a=v_w_out_a, w_out_b=v_w_out_b, w_out=v_w_out, w_ff1=v_w_ff1, w_ff2=v_w_ff2,
                   w_ple_gate=v_w_ple_gate, w_ple=v_w_ple)
    big = {}

    def update(n, transposed=False):
        view = (lambda a: a.T) if transposed else (lambda a: a)
        g = sums[n]
        delta, new_m, new_v = _adamw("adamw_" + n, view(given_w[n][0]), g, view(given_m[n][0]), view(given_v[n][0]))
        big[n] = tuple(view(a)[None] for a in (g, delta, new_m, new_v))

    for n in order[1:]:
        update(n)

    small_names = ["norm_mix_g", "b_gate", "q_norm_g", "k_norm_g", "rel_bias", "norm_mlp_g", "norm_ple_g",
                   "final_norm_g"]
    small_w = [norm_mix_g, b_gate, q_norm_g, k_norm_g, rel_bias, norm_mlp_g, norm_ple_g, final_norm_g]
    small_m = [m_norm_mix_g, m_b_gate, m_q_norm_g, m_k_norm_g, m_rel_bias, m_norm_mlp_g, m_norm_ple_g,
               m_final_norm_g]
    small_v = [v_norm_mix_g, v_b_gate, v_q_norm_g, v_k_norm_g, v_rel_bias, v_norm_mlp_g, v_norm_ple_g,
               v_final_norm_g]
    small_g = [dg_mix, dbg, dg_q, dg_k, d_rel, dg_mlp, dg_ple, dg_fin]
    sizes = [int(np.prod(w.shape)) for w in small_w]
    n_rows = -(-(sum(-(-sz // LANES) for sz in sizes) + 1) // 8) * 8
    pad = lambda v: jnp.pad(v.reshape(-1).astype(F32), (0, -v.size % LANES))
    pack = lambda vs, last: _pack_rows([pad(v) for v in vs] + [last], n_rows)
    zero_row = jnp.zeros((LANES,), F32)
    parts = _small_all_gather(pack(small_g, loss_part.reshape(-1) * (jnp.arange(LANES) == 0)), big["w_ple"][1])
    g_all, d_all, m_all, v_all = _small_update(parts, pack(small_w, zero_row), pack(small_m, zero_row),
                                               pack(small_v, zero_row))
    small = {}
    row = 0
    for n, w, sz in zip(small_names, small_w, sizes):
        nr = -(-sz // LANES)
        small[n] = tuple(a[row:row + nr].reshape(-1)[:sz].reshape(w.shape) for a in (g_all, d_all, m_all, v_all))
        row += nr
    loss = g_all[row, 0]

    src3, got3 = _copies_wait("grads3_wait", rs3[0], rs3[1], rs3[2], rs3[3], g_all, False)
    src4, got4 = _copies_wait("grads4_wait", rs4[0], rs4[1], rs4[2], rs4[3], g_all, False)
    sums["w_in"] = jnp.concatenate([_sum_slots("sum_w_in_lo", got3[0], own_rows(src3[0]), False),
                                    _sum_slots("sum_w_in_hi", got4[0], own_rows(src4[0]), False)], axis=1)
    update("w_in", transposed=True)

    names = ["norm_mix_g", "w_in", "b_gate", "q_norm_g", "k_norm_g", "rel_bias", "w_out_a", "w_out_b", "w_out",
             "norm_mlp_g", "w_ff1", "w_ff2", "norm_ple_g", "w_ple_gate", "w_ple", "final_norm_g"]
    res = {n: (big[n] if n in big else small[n]) for n in names}
    return (loss, grad_x[None], *[res[n][0] for n in names], *[res[n][1] for n in names],
            *[res[n][2] for n in names], *[res[n][3] for n in names])
```

```python
import math

import numpy as np
import jax
import jax.numpy as jnp
from jax import lax
from jax.experimental import pallas as pl
from jax.experimental.pallas import tpu as pltpu

F32 = jnp.float32
BF16 = jnp.bfloat16
MESH = pl.DeviceIdType.MESH

NORM_EPS = 1e-6
NEG_INF = -1e30
LOG2_E = math.log2(math.e)
LN_2 = math.log(2.0)
GRID_W = 64
ROPE_THETA = 10000.0
HEAD_DIM_A = 128
N_Q_HEADS_A = 8
N_KV_HEADS_A = 2
Q_PER_KV = N_Q_HEADS_A // N_KV_HEADS_A
HEAD_DIM_B = 64
N_HEADS_PER_DIL = 4
DILATIONS = (1, 4, 16)
BAND = 64
N_REL_BUCKETS = 32
REL_MAX_DIST = 1024
QA_W = N_Q_HEADS_A * HEAD_DIM_A
KA_W = N_KV_HEADS_A * HEAD_DIM_A
GB_W = N_HEADS_PER_DIL * HEAD_DIM_B
QB_W = GB_W * len(DILATIONS)
OFF_QA, OFF_KA, OFF_VA = 0, QA_W, QA_W + KA_W
OFF_QB = QA_W + 2 * KA_W
OFF_KB = OFF_QB + QB_W
OFF_VB = OFF_KB + QB_W
OFF_GA = OFF_VB + QB_W
N_DEV = 8
LANES = 128
VMEM_LIMIT = 56 * 2 ** 20

ADAM_LR, ADAM_B1, ADAM_B2, ADAM_EPS, ADAM_WD, ADAM_STEP = 0.001, 0.9, 0.999, 1e-08, 0.01, 10


def _cparams(sem):
    return pltpu.CompilerParams(dimension_semantics=sem, vmem_limit_bytes=VMEM_LIMIT)


def _resident(shape):
    nd = len(shape)
    return pl.BlockSpec(shape, lambda *_: (0,) * nd, pipeline_mode=pl.Buffered(1))


def _acc_spec(shape):
    nd = len(shape)
    return pl.BlockSpec(shape, lambda *_: (0,) * nd)


def _rows(tb, c):
    return pl.BlockSpec((tb, c), lambda i: (i, 0))


def _dil_shapes(s, dtype):
    return [jax.ShapeDtypeStruct((dil, s // dil, GB_W), dtype) for dil in DILATIONS]


def _dil_specs(tb):
    return [pl.BlockSpec((dil, tb // dil, GB_W), lambda i: (0, i, 0)) for dil in DILATIONS]


def _to_residues(val, out_ref, scr_ref, dil, dtype):
    if dil == 1:
        out_ref[0] = val.astype(dtype)
        return
    n = val.shape[0] // dil
    scr_ref[0] = val[:, :LANES]
    scr_ref[1] = val[:, LANES:]
    for r in range(dil):
        out_ref[r] = jnp.concatenate([scr_ref[0, pl.ds(r, n, stride=dil), :],
                                      scr_ref[1, pl.ds(r, n, stride=dil), :]], axis=1).astype(dtype)


def _from_residues(in_ref, scr_ref, dil):
    if dil == 1:
        return in_ref[0]
    n = in_ref.shape[1]
    for r in range(dil):
        v = in_ref[r]
        scr_ref[0, pl.ds(r, n, stride=dil), :] = v[:, :LANES]
        scr_ref[1, pl.ds(r, n, stride=dil), :] = v[:, LANES:]
    return jnp.concatenate([scr_ref[0], scr_ref[1]], axis=1)


def _dot_nt(a, b):
    return lax.dot_general(a, b, (((1,), (1,)), ((), ())), preferred_element_type=F32)


def _dot_nn(a, b):
    return lax.dot_general(a, b, (((1,), (0,)), ((), ())), preferred_element_type=F32)


def _dot_tn(a, b):
    return lax.dot_general(a, b, (((0,), (0,)), ((), ())), preferred_element_type=F32)


def _rstd(x):
    return lax.rsqrt(jnp.mean(x * x, axis=-1, keepdims=True) + NORM_EPS)


def _rms_bwd(dy, n, r, g):
    dn = dy * g
    return r * (dn - n * jnp.mean(dn * n, axis=-1, keepdims=True))


def _colsum(v):
    return jnp.sum(v, axis=0, keepdims=True)


def _sigmoid(v):
    return 1.0 / (1.0 + jnp.exp(-v))


def _rope_tables(s):
    half = HEAD_DIM_A // 2
    inv = np.power(np.float32(ROPE_THETA), -np.arange(0, half, 2, dtype=np.float32) / np.float32(half))
    t = np.arange(s)
    ang_r = (t // GRID_W).astype(np.float32)[:, None] * inv[None, :]
    ang_c = (t % GRID_W).astype(np.float32)[:, None] * inv[None, :]
    cr, sr, cc, sc = np.cos(ang_r), np.sin(ang_r), np.cos(ang_c), np.sin(ang_c)
    z = np.zeros_like(sr)
    cos = np.concatenate([cr, cr, cc, cc], axis=1)
    s1 = np.concatenate([z, sr, z, sc], axis=1)
    s2 = np.concatenate([-sr, z, -sc, z], axis=1)
    return [jnp.asarray(a, F32) for a in (cos, s1, s2)]


def _my_place():
    return lax.axis_index("x"), lax.axis_index("y"), lax.axis_index("c")


def _all_gather(shards, n_gather):
    n_all = len(shards)
    nw = n_gather

    def body(*refs):
        ins, outs = refs[:n_all], refs[n_all:2 * n_all]
        send_sems, recv_sems, local_sems = refs[2 * n_all:]
        x, y, c = _my_place()
        me, sibling = (x, y, c), (x, y, 1 - c)
        chips = [(1 - x, y), (x, 1 - y), (1 - x, 1 - y)]

        def rows(w, px, py, pc):
            n = ins[w].shape[0]
            return outs[w].at[pl.ds(pl.multiple_of((4 * px + 2 * py + pc) * n, 16), n), :]

        def copy(w, k, block, to, src=None):
            return pltpu.make_async_remote_copy(
                src_ref=rows(w, *block) if src is None else src, dst_ref=rows(w, *block),
                send_sem=send_sems.at[w, k], recv_sem=recv_sems.at[w, k], device_id=to, device_id_type=MESH)

        mine = [pltpu.make_async_copy(ins[w], rows(w, *me), local_sems.at[w]) for w in range(n_all)]
        for cp in mine:
            cp.start()
        first = []
        for w in range(nw):
            first.append(copy(w, 0, me, sibling, src=ins[w]))
            first += [copy(w, 1 + j, me, (*chip, c), src=ins[w]) for j, chip in enumerate(chips)]
        for cp in first:
            cp.start()
        passed = []
        for j, chip in enumerate(chips):
            for w in range(nw):
                copy(w, 1 + j, (*chip, c), me).wait_recv()
                fwd = copy(w, 4 + j, (*chip, c), sibling)
                fwd.start()
                passed.append(fwd)
        for w in range(nw):
            copy(w, 0, sibling, me).wait_recv()
        for j, chip in enumerate(chips):
            for w in range(nw):
                copy(w, 4 + j, (*chip, 1 - c), me).wait_recv()
        for cp in first + passed:
            cp.wait_send()
        for cp in mine:
            cp.wait()

    any_spec = pl.BlockSpec(memory_space=pl.ANY)
    return pl.pallas_call(
        body, name="weights_all_gather",
        out_shape=[jax.ShapeDtypeStruct((N_DEV * s.shape[0], s.shape[1]), s.dtype) for s in shards],
        in_specs=[any_spec] * n_all, out_specs=[any_spec] * n_all,
        scratch_shapes=[pltpu.SemaphoreType.DMA((nw, 7)), pltpu.SemaphoreType.DMA((nw, 7)),
                        pltpu.SemaphoreType.DMA((n_all,))],
    )(*shards)


def _place_own_rows(shards, my_idx):
    nw = len(shards)

    def body(idx_ref, *refs):
        for w in range(nw):
            refs[nw + w][...] = refs[w][...]

    grid_spec = pltpu.PrefetchScalarGridSpec(
        num_scalar_prefetch=1, grid=(1,),
        in_specs=[pl.BlockSpec(s.shape, lambda i, idx: (0, 0)) for s in shards],
        out_specs=[pl.BlockSpec(s.shape, lambda i, idx: (idx[0], 0)) for s in shards])
    return pl.pallas_call(
        body, name="place_own_rows", grid_spec=grid_spec,
        out_shape=[jax.ShapeDtypeStruct((N_DEV * s.shape[0], s.shape[1]), s.dtype) for s in shards],
        compiler_params=_cparams(("arbitrary",)))(my_idx.reshape(1).astype(jnp.int32), *shards)


_FLIPS = [(fx, fy, fc) for fx in (0, 1) for fy in (0, 1) for fc in (0, 1)][1:]


def _small_all_gather(v, after):
    def body(v_ref, after_ref, out_ref, send_sems, recv_sems):
        x, y, c = _my_place()
        my_idx = 4 * x + 2 * y + c
        out_ref[my_idx] = v_ref[...]
        sends = []
        for k, (fx, fy, fc) in enumerate(_FLIPS):
            to = (1 - x if fx else x, 1 - y if fy else y, 1 - c if fc else c)
            sends.append(pltpu.make_async_remote_copy(
                src_ref=v_ref, dst_ref=out_ref.at[my_idx], send_sem=send_sems.at[k], recv_sem=recv_sems.at[k],
                device_id=to, device_id_type=MESH))
        for cp in sends:
            cp.start()
        for k, (fx, fy, fc) in enumerate(_FLIPS):
            frm_idx = 4 * (1 - x if fx else x) + 2 * (1 - y if fy else y) + (1 - c if fc else c)
            pltpu.make_async_remote_copy(
                src_ref=v_ref, dst_ref=out_ref.at[frm_idx], send_sem=send_sems.at[k], recv_sem=recv_sems.at[k],
                device_id=(x, y, c), device_id_type=MESH).wait_recv()
        for cp in sends:
            cp.wait_send()

    vm = pl.BlockSpec(memory_space=pltpu.VMEM)
    return pl.pallas_call(
        body, name="small_all_gather", out_shape=jax.ShapeDtypeStruct((N_DEV,) + v.shape, v.dtype),
        in_specs=[vm, pl.BlockSpec(memory_space=pl.ANY)], out_specs=vm,
        scratch_shapes=[pltpu.SemaphoreType.DMA((7,)), pltpu.SemaphoreType.DMA((7,))],
    )(v, after)


_HBM = pl.BlockSpec(memory_space=pltpu.HBM)
_SEM = pl.BlockSpec(memory_space=pltpu.SEMAPHORE)
_ANY = pl.BlockSpec(memory_space=pl.ANY)
_SPLIT_COPY = dict(has_side_effects=pltpu.SideEffectType.DATAFLOW_SIDE_EFFECTING)


def _peer(x, y, c, k):
    fx, fy, fc = _FLIPS[k]
    return (1 - x if fx else x, 1 - y if fy else y, 1 - c if fc else c)


def _in_hbm(a):
    return pltpu.with_memory_space_constraint(a, pltpu.HBM)


def _split_copies(srcs, lands, send_sems, recv_sems, gather, arriving):
    x, y, c = _my_place()
    my_idx = 4 * x + 2 * y + c
    out = []
    for k in range(7):
        to = _peer(x, y, c, k)
        to_idx = 4 * to[0] + 2 * to[1] + to[2]
        for w in range(len(srcs)):
            if gather:
                n = srcs[w].shape[0]
                src = srcs[w]
                dst = lands[w].at[pl.ds(pl.multiple_of((to_idx if arriving else my_idx) * n, 16), n), :]
            else:
                n = lands[w].shape[1]
                src = srcs[w].at[pl.ds(pl.multiple_of(to_idx * n, 16), n), :]
                dst = lands[w].at[k]
            out.append(pltpu.make_async_remote_copy(
                src_ref=src, dst_ref=dst, send_sem=send_sems.at[7 * w + k], recv_sem=recv_sems.at[7 * w + k],
                device_id=to, device_id_type=MESH))
    return out


def _copies_start(name, srcs, lands, after, gather):
    nw = len(srcs)

    def body(*refs):
        send_sems, recv_sems = refs[2 * nw + 1], refs[2 * nw + 2]
        for cp in _split_copies(refs[:nw], refs[nw:2 * nw], send_sems, recv_sems, gather, False):
            cp.start()
        refs[-1][...] = jnp.zeros_like(refs[-1])

    sems = pltpu.SemaphoreType.DMA((7 * nw,))
    thru = [pltpu.HBM(a.shape, a.dtype) for a in list(srcs) + list(lands)]
    res = pl.pallas_call(
        body, name=name, out_shape=(sems, sems, *thru, jax.ShapeDtypeStruct((8, LANES), F32)),
        in_specs=[_HBM] * (2 * nw) + [_ANY], out_specs=(_SEM, _SEM, *[_HBM] * (2 * nw), pl.BlockSpec(memory_space=pltpu.VMEM)),
        input_output_aliases={i: 2 + i for i in range(2 * nw)},
        compiler_params=pltpu.CompilerParams(**_SPLIT_COPY),
    )(*[_in_hbm(a) for a in srcs], *[_in_hbm(a) for a in lands], after)
    return res[0], res[1], list(res[2:2 + nw]), list(res[2 + nw:2 + 2 * nw]), res[-1]


def _copies_wait(name, send_sems, recv_sems, srcs, lands, after, gather):
    nw = len(srcs)

    def body(*refs):
        for cp in _split_copies(refs[:nw], refs[nw:2 * nw], refs[2 * nw], refs[2 * nw + 1], gather, False):
            cp.wait_send()
        for cp in _split_copies(refs[:nw], refs[nw:2 * nw], refs[2 * nw], refs[2 * nw + 1], gather, True):
            cp.wait_recv()

    thru = [pltpu.HBM(a.shape, a.dtype) for a in list(srcs) + list(lands)]
    res = pl.pallas_call(
        body, name=name, out_shape=tuple(thru),
        in_specs=[_HBM] * (2 * nw) + [_SEM, _SEM, _ANY], out_specs=tuple([_HBM] * (2 * nw)),
        input_output_aliases={i: i for i in range(2 * nw)},
        compiler_params=pltpu.CompilerParams(**_SPLIT_COPY),
    )(*srcs, *lands, send_sems, recv_sems, after)
    return list(res[:nw]), list(res[nw:])


def _in_proj(x, tabs, w_in_t, g_mix, b_gate, q_g, k_g, tb, after):
    s, d = x.shape
    n_gate_chunks = d // 256
    q_scale = HEAD_DIM_A ** -0.5 * LOG2_E
    b_scale = HEAD_DIM_B ** -0.5 * LOG2_E

    def body(x_ref, c_ref, s1_ref, s2_ref, w_ref, gmix_ref, bg_ref, qg_ref, kg_ref, after_ref,
             h1_ref, qraw_ref, kraw_ref, qrot_ref, krot_ref, va_ref, *rest):
        qb_refs, kb_refs, vb_refs = rest[0:3], rest[3:6], rest[6:9]
        ga_ref, gb_ref, scr_ref = rest[9:]
        xv = x_ref[...]
        hb = (xv * _rstd(xv) * gmix_ref[...]).astype(BF16)
        h1_ref[...] = hb
        cos, s1, s2 = c_ref[...], s1_ref[...], s2_ref[...]

        def proj(lo, width):
            return _dot_nt(hb, w_ref[lo:lo + width, :])

        def norm_rope(z, g):
            n = z * _rstd(z) * g
            return n * cos + pltpu.roll(n, 32, 1) * s1 + pltpu.roll(n, 96, 1) * s2

        for j in range(QA_W // 256):
            z = proj(OFF_QA + 256 * j, 256)
            qraw_ref[:, 256 * j:256 * j + 256] = z
            for hh in range(2):
                lo = 256 * j + 128 * hh
                qrot_ref[:, lo:lo + 128] = (norm_rope(z[:, 128 * hh:128 * hh + 128], qg_ref[...]) * q_scale).astype(BF16)
        z = proj(OFF_KA, 256)
        kraw_ref[...] = z
        for hh in range(2):
            krot_ref[:, 128 * hh:128 * hh + 128] = norm_rope(z[:, 128 * hh:128 * hh + 128], kg_ref[...]).astype(BF16)
        va_ref[...] = proj(OFF_VA, 256).astype(BF16)
        for g, dil in enumerate(DILATIONS):
            _to_residues(proj(OFF_QB + GB_W * g, GB_W) * b_scale, qb_refs[g], scr_ref, dil, BF16)
            _to_residues(proj(OFF_KB + GB_W * g, GB_W), kb_refs[g], scr_ref, dil, BF16)
            _to_residues(proj(OFF_VB + GB_W * g, GB_W), vb_refs[g], scr_ref, dil, BF16)
        for j in range(n_gate_chunks):
            sl = slice(256 * j, 256 * j + 256)
            ga_ref[:, sl] = _sigmoid(proj(OFF_GA + 256 * j, 256) + bg_ref[:, sl]).astype(BF16)
            gb_ref[:, sl] = _sigmoid(
                proj(OFF_GA + d + 256 * j, 256) + bg_ref[:, d + 256 * j:d + 256 * j + 256]).astype(BF16)

    sd = jax.ShapeDtypeStruct
    outs = [sd((s, d), BF16), sd((s, QA_W), F32), sd((s, KA_W), F32), sd((s, QA_W), BF16), sd((s, KA_W), BF16),
            sd((s, KA_W), BF16)] + _dil_shapes(s, BF16) * 3 + [sd((s, d), BF16), sd((s, d), BF16)]
    out_specs = [_rows(tb, d), _rows(tb, QA_W), _rows(tb, KA_W), _rows(tb, QA_W), _rows(tb, KA_W), _rows(tb, KA_W)
                 ] + _dil_specs(tb) * 3 + [_rows(tb, d), _rows(tb, d)]
    in_specs = [_rows(tb, d), _rows(tb, LANES), _rows(tb, LANES), _rows(tb, LANES), _resident(w_in_t.shape),
                _resident(g_mix.shape), _resident(b_gate.shape), _resident(q_g.shape), _resident(k_g.shape), _ANY]
    res = list(pl.pallas_call(body, name="in_proj", grid=(s // tb,), in_specs=in_specs, out_specs=out_specs,
                              out_shape=outs, scratch_shapes=[pltpu.VMEM((2, tb, LANES), F32)],
                              compiler_params=_cparams(("arbitrary",)))(
        x, *tabs, w_in_t, g_mix, b_gate, q_g, k_g, after))
    return res[:6] + [res[6:9], res[9:12], res[12:15]] + res[15:]


def _attn_a_fwd(qrot, krot, va, tq, tk):
    s = qrot.shape[0]
    n_kv = s // tk
    gw = Q_PER_KV * HEAD_DIM_A

    def body(q_ref, k_ref, v_ref, o_ref, lse_ref):
        q4 = jnp.concatenate([q_ref[:, 128 * h:128 * h + 128] for h in range(Q_PER_KV)], axis=0)

        def step(j, carry):
            m, l, acc = carry
            sl = pl.ds(pl.multiple_of(j * tk, tk), tk)
            kj, vj = k_ref[sl, :], v_ref[sl, :]
            sc = _dot_nt(kj, q4)
            m_new = jnp.maximum(m, jnp.max(sc, axis=0, keepdims=True))
            p = jnp.exp2(sc - m_new)
            alpha = jnp.exp2(m - m_new)
            l = alpha * l + jnp.sum(p, axis=0, keepdims=True)
            acc = alpha * acc + _dot_tn(vj, p.astype(BF16))
            return m_new, l, acc

        rows = Q_PER_KV * tq
        m, l, acc = lax.fori_loop(0, n_kv, step, (jnp.full((1, rows), NEG_INF, F32), jnp.zeros((1, rows), F32),
                                                  jnp.zeros((HEAD_DIM_A, rows), F32)))
        o = (acc / l).T
        lse = m + jnp.log2(l)
        for h in range(Q_PER_KV):
            o_ref[:, 128 * h:128 * h + 128] = o[h * tq:(h + 1) * tq].astype(BF16)
            lse_ref[0, h:h + 1, :] = lse[:, h * tq:(h + 1) * tq]

    return pl.pallas_call(
        body, name="attn_a_fwd", grid=(N_KV_HEADS_A, s // tq),
        in_specs=[pl.BlockSpec((tq, gw), lambda g, i: (i, g)),
                  pl.BlockSpec((s, HEAD_DIM_A), lambda g, i: (0, g)),
                  pl.BlockSpec((s, HEAD_DIM_A), lambda g, i: (0, g))],
        out_specs=[pl.BlockSpec((tq, gw), lambda g, i: (i, g)),
                   pl.BlockSpec((1, Q_PER_KV, tq), lambda g, i: (g, 0, i))],
        out_shape=[jax.ShapeDtypeStruct((s, QA_W), BF16), jax.ShapeDtypeStruct((N_KV_HEADS_A, Q_PER_KV, s), F32)],
        compiler_params=_cparams(("arbitrary", "arbitrary")))(qrot, krot, va)


def _attn_a_bwd(qrot, krot, va, oa, doa, lse, tq, tk, after):
    s = qrot.shape[0]
    n_kv = s // tk
    gw = Q_PER_KV * HEAD_DIM_A

    def body(q_ref, do_ref, o_ref, lse_ref, k_ref, v_ref, after_ref, dq_ref, dk_ref, dv_ref):
        @pl.when(pl.program_id(1) == 0)
        def _():
            dk_ref[...] = jnp.zeros_like(dk_ref)
            dv_ref[...] = jnp.zeros_like(dv_ref)

        def stack(ref):
            return jnp.concatenate([ref[:, 128 * h:128 * h + 128] for h in range(Q_PER_KV)], axis=0)

        q4, do4, o4 = stack(q_ref), stack(do_ref), stack(o_ref)
        q4t, do4t = q4.T, do4.T
        delta = jnp.sum((do4.astype(F32) * o4.astype(F32)).T, axis=0, keepdims=True)
        lse4 = jnp.concatenate([lse_ref[0, h:h + 1, :] for h in range(Q_PER_KV)], axis=1)

        def step(j, dq):
            sl = pl.ds(pl.multiple_of(j * tk, tk), tk)
            kj, vj = k_ref[sl, :], v_ref[sl, :]
            p = jnp.exp2(_dot_nt(kj, q4) - lse4)
            ds = (p * (_dot_nt(vj, do4) - delta)).astype(BF16)
            dk_ref[:, sl] += _dot_nt(q4t, ds)
            dv_ref[:, sl] += _dot_nt(do4t, p.astype(BF16))
            return dq + _dot_tn(kj, ds)

        dq = lax.fori_loop(0, n_kv, step, jnp.zeros((HEAD_DIM_A, Q_PER_KV * tq), F32)).T
        for h in range(Q_PER_KV):
            dq_ref[:, 128 * h:128 * h + 128] = dq[h * tq:(h + 1) * tq]

    qspec = pl.BlockSpec((tq, gw), lambda g, i: (i, g))
    kspec = pl.BlockSpec((s, HEAD_DIM_A), lambda g, i: (0, g))
    ktspec = pl.BlockSpec((HEAD_DIM_A, s), lambda g, i: (g, 0))
    return pl.pallas_call(
        body, name="attn_a_bwd", grid=(N_KV_HEADS_A, s // tq),
        in_specs=[qspec, qspec, qspec, pl.BlockSpec((1, Q_PER_KV, tq), lambda g, i: (g, 0, i)), kspec, kspec, _ANY],
        out_specs=[qspec, ktspec, ktspec],
        out_shape=[jax.ShapeDtypeStruct((s, QA_W), F32), jax.ShapeDtypeStruct((KA_W, s), F32),
                   jax.ShapeDtypeStruct((KA_W, s), F32)],
        compiler_params=_cparams(("arbitrary", "arbitrary")))(qrot, doa, oa, lse, krot, va, after)


BAND_QB = 128
BAND_WIN = BAND_QB + 2 * BAND


def _band_specs(s, cb):
    per = cb // BAND
    last = s // BAND - 1
    cur = pl.BlockSpec((cb, GB_W), lambda i: (i, 0))
    prev = pl.BlockSpec((BAND, GB_W), lambda i: (jnp.maximum(i * per - 1, 0), 0))
    nxt = pl.BlockSpec((BAND, GB_W), lambda i: (jnp.minimum(i * per + per, last), 0))
    return cur, prev, nxt


def _window(prev_ref, cur_ref, next_ref):
    return jnp.concatenate([prev_ref[...], cur_ref[...], next_ref[...]], axis=0)


def _band_mask(base, seg_shift):
    rq = base + lax.broadcasted_iota(jnp.int32, (BAND_QB, BAND_WIN), 0)
    rk = base - BAND + lax.broadcasted_iota(jnp.int32, (BAND_QB, BAND_WIN), 1)
    same_segment = lax.shift_right_arithmetic(rq, jnp.int32(seg_shift)) == lax.shift_right_arithmetic(rk, jnp.int32(seg_shift))
    return (jnp.abs(rk - rq) <= BAND) & same_segment


def _build_bias(bmap_ref, tab_ref, bias_ref):
    bm = bmap_ref[...]
    acc = [jnp.full(bm.shape, NEG_INF, F32) for _ in range(N_HEADS_PER_DIL)]
    for b in range(N_REL_BUCKETS):
        hit = bm == b
        for h in range(N_HEADS_PER_DIL):
            acc[h] = jnp.where(hit, tab_ref[b, h] * LOG2_E, acc[h])
    rows = bm.shape[0]
    for h in range(N_HEADS_PER_DIL):
        bias_ref[h * rows:(h + 1) * rows, :] = acc[h]


def _segment_mask(base, seg_len, seg_shift):
    if seg_len % BAND_QB:
        return _band_mask(base, seg_shift)
    pos = lax.rem(base, seg_len)
    w = lax.broadcasted_iota(jnp.int32, (1, BAND_WIN), 1)
    return ((w >= BAND) | (pos != 0)) & ((w < BAND + BAND_QB) | (pos != seg_len - BAND_QB))


def _head_lane_masks():
    lane = lax.broadcasted_iota(jnp.int32, (1, LANES), 1)
    return [lane < HEAD_DIM_B, lane >= HEAD_DIM_B]


def _rows4(mask):
    return mask if mask.shape[0] == 1 else jnp.concatenate([mask] * N_HEADS_PER_DIL, axis=0)


def _head_scores(a, b):
    hm = _head_lane_masks()
    out = []
    for hp in range(2):
        ls = slice(LANES * hp, LANES * hp + LANES)
        ah = a[:, ls]
        both = jnp.concatenate([jnp.where(hm[0], ah, jnp.zeros_like(ah)), jnp.where(hm[1], ah, jnp.zeros_like(ah))],
                               axis=0)
        out.append(_dot_nt(both, b[:, ls]))
    return jnp.concatenate(out, axis=0)


def _head_combine(p, v, scale=None, transposed=False):
    hm = _head_lane_masks()
    rows = p.shape[0] // N_HEADS_PER_DIL
    halves = []
    for hp in range(2):
        vh = v[:, LANES * hp:LANES * hp + LANES]
        acc = None
        for hh in range(2):
            h = 2 * hp + hh
            ph = p[h * rows:(h + 1) * rows]
            vm = jnp.where(hm[hh], vh, jnp.zeros_like(vh))
            t = _dot_tn(ph, vm) if transposed else _dot_nn(ph, vm)
            if scale is not None:
                t = t * scale[h * rows:(h + 1) * rows]
            acc = t if acc is None else acc + t
        halves.append(acc)
    return jnp.concatenate(halves, axis=1)


def _head_spread(col):
    rows = col.shape[0] // N_HEADS_PER_DIL
    lane = lax.broadcasted_iota(jnp.int32, (1, GB_W), 1)
    out = jnp.zeros((rows, GB_W), F32)
    for h in range(N_HEADS_PER_DIL):
        out = jnp.where((lane >= HEAD_DIM_B * h) & (lane < HEAD_DIM_B * (h + 1)), col[h * rows:(h + 1) * rows], out)
    return out


def _head_cols(v):
    return jnp.concatenate([v[:, HEAD_DIM_B * h:HEAD_DIM_B * h + 1] for h in range(N_HEADS_PER_DIL)], axis=0)


def _seg_shift(s, dil):
    seg = s // dil
    assert seg & (seg - 1) == 0, "segment length must be a power of two"
    return seg.bit_length() - 1


def _band_fwd(dil, qb, kb, vb, bmap, tab, cb):
    s = qb.shape[0]
    shift = _seg_shift(s, dil)

    def body(q_ref, kp_ref, kc_ref, kn_ref, vp_ref, vc_ref, vn_ref, bmap_ref, tab_ref, o_ref, lse_ref, bias_ref):
        @pl.when(pl.program_id(0) == 0)
        def _():
            _build_bias(bmap_ref, tab_ref, bias_ref)

        kw, vw = _window(kp_ref, kc_ref, kn_ref), _window(vp_ref, vc_ref, vn_ref)
        for jj in range(cb // BAND_QB):
            r0 = BAND_QB * jj
            mask = _rows4(_segment_mask(pl.program_id(0) * cb + r0, s // dil, shift))
            sc = _head_scores(q_ref[r0:r0 + BAND_QB, :], kw[r0:r0 + BAND_WIN, :]) + bias_ref[...]
            sc = jnp.where(mask, sc, NEG_INF)
            m = jnp.max(sc, axis=-1, keepdims=True)
            e = jnp.exp2(sc - m)
            l = jnp.sum(e, axis=-1, keepdims=True)
            o = _head_combine(e.astype(BF16), vw[r0:r0 + BAND_WIN, :], 1.0 / l)
            o_ref[r0:r0 + BAND_QB, :] = o
            lse_ref[r0:r0 + BAND_QB, :] = _head_spread(m + jnp.log2(l))

    cur, prev, nxt = _band_specs(s, cb)
    return pl.pallas_call(
        body, name=f"band_fwd_d{dil}", grid=(s // cb,),
        in_specs=[cur, prev, cur, nxt, prev, cur, nxt, _resident(bmap.shape), pl.BlockSpec(memory_space=pltpu.SMEM)],
        out_specs=[cur, cur],
        out_shape=[jax.ShapeDtypeStruct(qb.shape, F32), jax.ShapeDtypeStruct(qb.shape, F32)],
        scratch_shapes=[pltpu.VMEM((N_HEADS_PER_DIL * BAND_QB, BAND_WIN), F32)],
        compiler_params=_cparams(("arbitrary",)))(qb, kb, kb, kb, vb, vb, vb, bmap, tab)


def _band_bwd(dil, qb, kb, vb, dob, lse, dd, bmap, tab, cb):
    s = qb.shape[0]
    shift = _seg_shift(s, dil)
    n_steps = s // cb

    def body(q_ref, do_ref, lse_ref, dd_ref, kp_ref, kc_ref, kn_ref, vp_ref, vc_ref, vn_ref, bmap_ref, tab_ref,
             dq_ref, dk_ref, dv_ref, dtab_ref, bias_ref, dsum_ref):
        @pl.when(pl.program_id(0) == 0)
        def _():
            _build_bias(bmap_ref, tab_ref, bias_ref)
            dsum_ref[...] = jnp.zeros_like(dsum_ref)
            dk_ref[...] = jnp.zeros_like(dk_ref)
            dv_ref[...] = jnp.zeros_like(dv_ref)

        kw, vw = _window(kp_ref, kc_ref, kn_ref), _window(vp_ref, vc_ref, vn_ref)
        for jj in range(cb // BAND_QB):
            r0 = BAND_QB * jj
            base = pl.program_id(0) * cb + r0
            mask = _rows4(_segment_mask(base, s // dil, shift))
            qh, doh = q_ref[r0:r0 + BAND_QB, :], do_ref[r0:r0 + BAND_QB, :]
            k3, v3 = kw[r0:r0 + BAND_WIN, :], vw[r0:r0 + BAND_WIN, :]
            sc = _head_scores(qh, k3) + bias_ref[...]
            sc = jnp.where(mask, sc, NEG_INF)
            p = jnp.exp2(sc - _head_cols(lse_ref[r0:r0 + BAND_QB, :]))
            dp = _head_scores(doh, v3)
            ds = p * (dp - _head_cols(dd_ref[r0:r0 + BAND_QB, :]))
            dsum_ref[...] += ds
            dsb = ds.astype(BF16)
            dq_ref[r0:r0 + BAND_QB, :] = _head_combine(dsb, k3)
            dk_win = _head_combine(dsb, qh, transposed=True)
            dv_win = _head_combine(p.astype(BF16), doh, transposed=True)
            own = pl.ds(pl.multiple_of(base, BAND), BAND_QB)
            dk_ref[own, :] += dk_win[BAND:BAND + BAND_QB]
            dv_ref[own, :] += dv_win[BAND:BAND + BAND_QB]

            @pl.when(base > 0)
            def _():
                before = pl.ds(pl.multiple_of(base - BAND, BAND), BAND)
                dk_ref[before, :] += dk_win[:BAND]
                dv_ref[before, :] += dv_win[:BAND]

            @pl.when(base + BAND_QB < s)
            def _():
                after = pl.ds(pl.multiple_of(base + BAND_QB, BAND), BAND)
                dk_ref[after, :] += dk_win[BAND + BAND_QB:]
                dv_ref[after, :] += dv_win[BAND + BAND_QB:]

        @pl.when(pl.program_id(0) == n_steps - 1)
        def _():
            bm = bmap_ref[...]
            lane = lax.broadcasted_iota(jnp.int32, (1, LANES), 1)
            for b in range(N_REL_BUCKETS):
                hit = bm == b
                row = jnp.zeros((1, LANES), F32)
                for h in range(N_HEADS_PER_DIL):
                    part = dsum_ref[h * BAND_QB:(h + 1) * BAND_QB, :]
                    row = jnp.where(lane == h, jnp.sum(jnp.where(hit, part, 0.0)), row)
                dtab_ref[b:b + 1, :] = row

    cur, prev, nxt = _band_specs(s, cb)
    whole = _acc_spec(qb.shape)
    return pl.pallas_call(
        body, name=f"band_bwd_d{dil}", grid=(n_steps,),
        in_specs=[cur, cur, cur, cur, prev, cur, nxt, prev, cur, nxt, _resident(bmap.shape),
                  pl.BlockSpec(memory_space=pltpu.SMEM)],
        out_specs=[cur, whole, whole, _acc_spec((N_REL_BUCKETS, LANES))],
        out_shape=[jax.ShapeDtypeStruct(qb.shape, F32)] * 3 + [jax.ShapeDtypeStruct((N_REL_BUCKETS, LANES), F32)],
        scratch_shapes=[pltpu.VMEM((N_HEADS_PER_DIL * BAND_QB, BAND_WIN), F32),
                        pltpu.VMEM((N_HEADS_PER_DIL * BAND_QB, BAND_WIN), F32)],
        compiler_params=_cparams(("arbitrary",)))(qb, dob, lse, dd, kb, kb, kb, vb, vb, vb, bmap, tab)


def _t5_bucket(rel):
    nb = N_REL_BUCKETS // 2
    ret = (rel > 0).astype(np.int32) * nb
    n = np.abs(rel)
    max_exact = nb // 2
    large = max_exact + (np.log(np.maximum(n, 1) / max_exact) / math.log(REL_MAX_DIST / max_exact)
                         * (nb - max_exact)).astype(np.int32)
    large = np.minimum(large, nb - 1)
    return ret + np.where(n < max_exact, n, large).astype(np.int32)


def _bucket_map(dil):
    off = np.arange(BAND_WIN)[None, :] - BAND - np.arange(BAND_QB)[:, None]
    return np.where(np.abs(off) <= BAND, _t5_bucket(off * dil), -1).astype(np.int32)


def _seg_sum(v):
    lane = lax.broadcasted_iota(jnp.int32, (1, v.shape[1]), 1)
    out = jnp.zeros_like(v)
    for h in range(v.shape[1] // HEAD_DIM_B):
        m = (lane >= HEAD_DIM_B * h) & (lane < HEAD_DIM_B * (h + 1))
        out = jnp.where(m, jnp.sum(jnp.where(m, v, 0.0), axis=-1, keepdims=True), out)
    return out


def _mix_out(x, oa, og, lg, ga, gb, w_oa, w_ob_t, w_o, tb):
    s, d = x.shape

    def body(x_ref, oa_ref, og0_ref, og1_ref, og2_ref, lg0_ref, lg1_ref, lg2_ref, ga_ref, gb_ref,
             woa_ref, wob_ref, wo_ref, x2_ref, ob_ref, lse0_ref, lse1_ref, lse2_ref, ya_ref, yb_ref, u_ref, scr_ref):
        og_refs, lg_refs = (og0_ref, og1_ref, og2_ref), (lg0_ref, lg1_ref, lg2_ref)
        l0, l1, l2 = [_from_residues(lg_refs[g], scr_ref, dil) for g, dil in enumerate(DILATIONS)]
        lmax = jnp.maximum(jnp.maximum(l0, l1), l2)
        w0, w1, w2 = jnp.exp2(l0 - lmax), jnp.exp2(l1 - lmax), jnp.exp2(l2 - lmax)
        den = w0 + w1 + w2
        o0, o1, o2 = [_from_residues(og_refs[g], scr_ref, dil) for g, dil in enumerate(DILATIONS)]
        ob = ((w0 * o0 + w1 * o1 + w2 * o2) / den).astype(BF16)
        ob_ref[...] = ob
        lse = lmax + jnp.log2(den)
        for g, (dil, ref) in enumerate(zip(DILATIONS, (lse0_ref, lse1_ref, lse2_ref))):
            _to_residues(lse, ref, scr_ref, dil, F32)
        ya = _dot_nn(oa_ref[...], woa_ref[...])
        yb = _dot_nt(ob, wob_ref[...])
        ya_ref[...] = ya.astype(BF16)
        yb_ref[...] = yb.astype(BF16)
        u = (ga_ref[...].astype(F32) * ya + gb_ref[...].astype(F32) * yb).astype(BF16)
        u_ref[...] = u
        x2_ref[...] = x_ref[...] + _dot_nn(u, wo_ref[...])

    sd = jax.ShapeDtypeStruct
    res = list(pl.pallas_call(
        body, name="mix_out", grid=(s // tb,),
        in_specs=[_rows(tb, d), _rows(tb, QA_W)] + _dil_specs(tb) * 2 + [
            _rows(tb, d), _rows(tb, d), _resident(w_oa.shape), _resident(w_ob_t.shape), _resident(w_o.shape)],
        out_specs=[_rows(tb, d), _rows(tb, GB_W)] + _dil_specs(tb) + [_rows(tb, d), _rows(tb, d), _rows(tb, d)],
        out_shape=[sd((s, d), F32), sd((s, GB_W), BF16)] + _dil_shapes(s, F32) + [
            sd((s, d), BF16), sd((s, d), BF16), sd((s, d), BF16)],
        scratch_shapes=[pltpu.VMEM((2, tb, LANES), F32)],
        compiler_params=_cparams(("arbitrary",)))(x, oa, *og, *lg, ga, gb, w_oa, w_ob_t, w_o))
    return res[:2] + [res[2:5]] + res[5:]


def _mlp_fwd(x2, w1_t, w2, g_mlp, tb, tc):
    s, d = x2.shape
    dff = w1_t.shape[0]

    def body(x_ref, w1_ref, w2_ref, g_ref, x3_ref, r_ref, h_ref):
        xv = x_ref[...]
        hb = (xv * _rstd(xv) * g_ref[...]).astype(BF16)
        h_ref[...] = hb
        x3_ref[...] = xv
        for c in range(dff // tc):
            sl = slice(tc * c, tc * c + tc)
            r = jnp.maximum(_dot_nt(hb, w1_ref[sl, :]), 0.0)
            r_ref[:, sl] = r.astype(BF16)
            x3_ref[...] += _dot_nn((r * r).astype(BF16), w2_ref[sl, :])

    sd = jax.ShapeDtypeStruct
    return pl.pallas_call(
        body, name="mlp_fwd", grid=(s // tb,),
        in_specs=[_rows(tb, d), _resident(w1_t.shape), _resident(w2.shape), _resident(g_mlp.shape)],
        out_specs=[_rows(tb, d), _rows(tb, dff), _rows(tb, d)],
        out_shape=[sd((s, d), F32), sd((s, dff), BF16), sd((s, d), BF16)],
        compiler_params=_cparams(("arbitrary",)))(x2, w1_t, w2, g_mlp)


def _ple_loss(x3, p, target, w_pg, w_p_t, g_ple, g_fin, tb):
    s, d = x3.shape
    dp = p.shape[1]

    def body(x_ref, p_ref, t_ref, wpg_ref, wp_ref, gple_ref, gfin_ref,
             dx3_ref, h3_ref, dpre_ref, dpe_ref, pb_ref, loss_ref, dgfin_ref, dgple_ref):
        @pl.when(pl.program_id(0) == 0)
        def _():
            loss_ref[...] = jnp.zeros_like(loss_ref)
            dgfin_ref[...] = jnp.zeros_like(dgfin_ref)
            dgple_ref[...] = jnp.zeros_like(dgple_ref)

        x3v = x_ref[...]
        r3 = _rstd(x3v)
        n3 = x3v * r3
        h3 = (n3 * gple_ref[...]).astype(BF16)
        h3_ref[...] = h3
        gp = _sigmoid(_dot_nn(h3, wpg_ref[...]))
        pb = p_ref[...].astype(BF16)
        pb_ref[...] = pb
        pe = _dot_nt(pb, wp_ref[...])
        x4 = x3v + gp * pe
        r4 = _rstd(x4)
        n4 = x4 * r4
        err = n4 * gfin_ref[...] - t_ref[...]
        loss_ref[...] += jnp.sum(0.5 * jnp.mean(err * err, axis=-1, keepdims=True), axis=0, keepdims=True)
        dy = err * (1.0 / d)
        dgfin_ref[...] += _colsum(dy * n4)
        dx4 = _rms_bwd(dy, n4, r4, gfin_ref[...])
        dpe_ref[...] = (dx4 * gp).astype(BF16)
        dpre = (dx4 * pe * gp * (1.0 - gp)).astype(BF16)
        dpre_ref[...] = dpre
        dh3 = _dot_nt(dpre, wpg_ref[...])
        dgple_ref[...] += _colsum(dh3 * n3)
        dx3_ref[...] = dx4 + _rms_bwd(dh3, n3, r3, gple_ref[...])

    sd = jax.ShapeDtypeStruct
    return pl.pallas_call(
        body, name="ple_loss", grid=(s // tb,),
        in_specs=[_rows(tb, d), _rows(tb, dp), _rows(tb, d), _resident(w_pg.shape), _resident(w_p_t.shape),
                  _resident(g_ple.shape), _resident(g_fin.shape)],
        out_specs=[_rows(tb, d), _rows(tb, d), _rows(tb, d), _rows(tb, d), _rows(tb, dp),
                   _acc_spec((1, LANES)), _acc_spec((1, d)), _acc_spec((1, d))],
        out_shape=[sd((s, d), F32), sd((s, d), BF16), sd((s, d), BF16), sd((s, d), BF16), sd((s, dp), BF16),
                   sd((1, LANES), F32), sd((1, d), F32), sd((1, d), F32)],
        compiler_params=_cparams(("arbitrary",)))(x3, p, target, w_pg, w_p_t, g_ple, g_fin)


def _mlp_bwd(dx3, x2, r, w1_t, w2, g_mlp, tb, tc):
    s, d = x2.shape
    dff = w1_t.shape[0]

    def body(dx3_ref, x_ref, r_ref, w1_ref, w2_ref, g_ref, dx2_ref, df_ref, dg_ref, dh_ref):
        @pl.when(pl.program_id(0) == 0)
        def _():
            dg_ref[...] = jnp.zeros_like(dg_ref)

        dx3v = dx3_ref[...]
        dx3b = dx3v.astype(BF16)
        dh_ref[...] = jnp.zeros_like(dh_ref)
        for c in range(dff // tc):
            sl = slice(tc * c, tc * c + tc)
            df = (_dot_nt(dx3b, w2_ref[sl, :]) * (2.0 * r_ref[:, sl].astype(F32))).astype(BF16)
            df_ref[:, sl] = df
            dh_ref[...] += _dot_nn(df, w1_ref[sl, :])
        xv = x_ref[...]
        r2 = _rstd(xv)
        n2 = xv * r2
        dh = dh_ref[...]
        dg_ref[...] += _colsum(dh * n2)
        dx2_ref[...] = dx3v + _rms_bwd(dh, n2, r2, g_ref[...])

    sd = jax.ShapeDtypeStruct
    return pl.pallas_call(
        body, name="mlp_bwd", grid=(s // tb,),
        in_specs=[_rows(tb, d), _rows(tb, d), _rows(tb, dff), _resident(w1_t.shape), _resident(w2.shape),
                  _resident(g_mlp.shape)],
        out_specs=[_rows(tb, d), _rows(tb, dff), _acc_spec((1, d))],
        out_shape=[sd((s, d), F32), sd((s, dff), BF16), sd((1, d), F32)],
        scratch_shapes=[pltpu.VMEM((tb, d), F32)],
        compiler_params=_cparams(("arbitrary",)))(dx3, x2, r, w1_t, w2, g_mlp)


def _mix_out_bwd(dx2, ya, yb, ga, gb, ob, w_oa, w_ob_t, w_o, tb, after):
    s, d = dx2.shape

    def body(dx_ref, ya_ref, yb_ref, ga_ref, gb_ref, ob_ref, woa_ref, wob_ref, wo_ref, after_ref,
             doa_ref, dob0_ref, dob1_ref, dob2_ref, dd0_ref, dd1_ref, dd2_ref, dga_ref, dgb_ref, dya_ref, dyb_ref,
             dbg_ref, scr_ref):
        @pl.when(pl.program_id(0) == 0)
        def _():
            dbg_ref[...] = jnp.zeros_like(dbg_ref)

        du = _dot_nt(dx_ref[...].astype(BF16), wo_ref[...])
        gav, gbv = ga_ref[...].astype(F32), gb_ref[...].astype(F32)
        dya = (du * gav).astype(BF16)
        dyb = (du * gbv).astype(BF16)
        dya_ref[...] = dya
        dyb_ref[...] = dyb
        dga = du * ya_ref[...].astype(F32) * gav * (1.0 - gav)
        dgb = du * yb_ref[...].astype(F32) * gbv * (1.0 - gbv)
        dga_ref[...] = dga.astype(BF16)
        dgb_ref[...] = dgb.astype(BF16)
        dbg_ref[:, 0:d] += _colsum(dga)
        dbg_ref[:, d:2 * d] += _colsum(dgb)
        doa_ref[...] = _dot_nt(dya, woa_ref[...]).astype(BF16)
        dob = _dot_nn(dyb, wob_ref[...])
        dd = _seg_sum(dob * ob_ref[...].astype(F32))
        for dil, dob_ref, dd_ref in zip(DILATIONS, (dob0_ref, dob1_ref, dob2_ref), (dd0_ref, dd1_ref, dd2_ref)):
            _to_residues(dob, dob_ref, scr_ref, dil, BF16)
            _to_residues(dd, dd_ref, scr_ref, dil, F32)

    sd = jax.ShapeDtypeStruct
    res = list(pl.pallas_call(
        body, name="mix_out_bwd", grid=(s // tb,),
        in_specs=[_rows(tb, d)] * 5 + [_rows(tb, GB_W), _resident(w_oa.shape), _resident(w_ob_t.shape),
                                       _resident(w_o.shape), _ANY],
        out_specs=[_rows(tb, QA_W)] + _dil_specs(tb) * 2 + [_rows(tb, d), _rows(tb, d), _rows(tb, d),
                                                           _rows(tb, d), _acc_spec((1, 2 * d))],
        out_shape=[sd((s, QA_W), BF16)] + _dil_shapes(s, BF16) + _dil_shapes(s, F32) + [
            sd((s, d), BF16), sd((s, d), BF16), sd((s, d), BF16), sd((s, d), BF16), sd((1, 2 * d), F32)],
        scratch_shapes=[pltpu.VMEM((2, tb, LANES), F32)],
        compiler_params=_cparams(("arbitrary",)))(dx2, ya, yb, ga, gb, ob, w_oa, w_ob_t, w_o, after))
    return res[:1] + [res[1:4], res[4:7]] + res[7:]


def _in_proj_bwd(dx2, x, dqrot, dkrot, dva, qraw, kraw, tabs, dqb, dkb, dvb, dga, dgb, w_in_t, g_mix, q_g, k_g, tb):
    s, d = x.shape
    din = w_in_t.shape[0]
    q_scale = HEAD_DIM_A ** -0.5
    b_scale = HEAD_DIM_B ** -0.5
    tc = 256

    def body(dx2_ref, x_ref, dq_ref, dk_ref, dv_ref, qraw_ref, kraw_ref, c_ref, s1_ref, s2_ref, *rest):
        dqb_refs, dkb_refs, dvb_refs = rest[0:3], rest[3:6], rest[6:9]
        (dga_ref, dgb_ref, w_ref, gmix_ref, qg_ref, kg_ref,
         dx_ref, dz_ref, dgmix_ref, dqg_ref, dkg_ref, dh_ref, scr_ref) = rest[9:]

        @pl.when(pl.program_id(0) == 0)
        def _():
            dgmix_ref[...] = jnp.zeros_like(dgmix_ref)
            dqg_ref[...] = jnp.zeros_like(dqg_ref)
            dkg_ref[...] = jnp.zeros_like(dkg_ref)

        cos, s1, s2 = c_ref[...][None], s1_ref[...][None], s2_ref[...][None]

        def heads_bwd(drot, z, g_ref, acc_ref):
            dn = drot * cos + pltpu.roll(drot * s1, 96, 2) + pltpu.roll(drot * s2, 32, 2)
            rr = _rstd(z)
            nn = z * rr
            acc_ref[...] += jnp.sum(jnp.sum(dn * nn, axis=0), axis=0, keepdims=True)
            return _rms_bwd(dn, nn, rr, g_ref[...][None]).astype(BF16)

        dh_ref[...] = jnp.zeros_like(dh_ref)

        def emit(off, piece):
            dz_ref[:, off:off + tc] = piece
            dh_ref[...] += _dot_nn(piece, w_ref[off:off + tc, :])

        for j in range(d // tc):
            emit(OFF_GA + tc * j, dga_ref[:, tc * j:tc * j + tc])
            emit(OFF_GA + d + tc * j, dgb_ref[:, tc * j:tc * j + tc])
        emit(OFF_VA, dv_ref[...].T.astype(BF16))
        for g, dil in enumerate(DILATIONS):
            emit(OFF_QB + GB_W * g, (_from_residues(dqb_refs[g], scr_ref, dil) * b_scale).astype(BF16))
            emit(OFF_KB + GB_W * g, (_from_residues(dkb_refs[g], scr_ref, dil) * LN_2).astype(BF16))
            emit(OFF_VB + GB_W * g, _from_residues(dvb_refs[g], scr_ref, dil).astype(BF16))
        stack = lambda ref, n: jnp.stack([ref[:, 128 * h:128 * h + 128] for h in range(n)], axis=0)
        dzq = heads_bwd(stack(dq_ref, N_Q_HEADS_A) * q_scale, stack(qraw_ref, N_Q_HEADS_A), qg_ref, dqg_ref)
        dkt = jnp.stack([dk_ref[128 * h:128 * h + 128, :].T for h in range(N_KV_HEADS_A)], axis=0) * LN_2
        dzk = heads_bwd(dkt, stack(kraw_ref, N_KV_HEADS_A), kg_ref, dkg_ref)
        for j in range(N_Q_HEADS_A // 2):
            emit(OFF_QA + tc * j, jnp.concatenate([dzq[2 * j], dzq[2 * j + 1]], axis=1))
        emit(OFF_KA, jnp.concatenate([dzk[0], dzk[1]], axis=1))
        xv = x_ref[...]
        r1 = _rstd(xv)
        n1 = xv * r1
        dh = dh_ref[...]
        dgmix_ref[...] += _colsum(dh * n1)
        dx_ref[...] = dx2_ref[...] + _rms_bwd(dh, n1, r1, gmix_ref[...])

    sd = jax.ShapeDtypeStruct
    return pl.pallas_call(
        body, name="in_proj_bwd", grid=(s // tb,),
        in_specs=[_rows(tb, d), _rows(tb, d), _rows(tb, QA_W), pl.BlockSpec((KA_W, tb), lambda i: (0, i)),
                  pl.BlockSpec((KA_W, tb), lambda i: (0, i)), _rows(tb, QA_W),
                  _rows(tb, KA_W), _rows(tb, LANES), _rows(tb, LANES), _rows(tb, LANES),
                  ] + _dil_specs(tb) * 3 + [_rows(tb, d), _rows(tb, d),
                  _resident(w_in_t.shape), _resident(g_mix.shape), _resident(q_g.shape), _resident(k_g.shape)],
        out_specs=[_rows(tb, d), _rows(tb, din), _acc_spec((1, d)), _acc_spec((1, HEAD_DIM_A)),
                   _acc_spec((1, HEAD_DIM_A))],
        out_shape=[sd((s, d), F32), sd((s, din), BF16), sd((1, d), F32), sd((1, HEAD_DIM_A), F32),
                   sd((1, HEAD_DIM_A), F32)],
        scratch_shapes=[pltpu.VMEM((tb, d), F32), pltpu.VMEM((2, tb, LANES), F32)],
        compiler_params=_cparams(("arbitrary",)))(
        dx2, x, dqrot, dkrot, dva, qraw, kraw, *tabs, *dqb, *dkb, *dvb, dga, dgb, w_in_t, g_mix, q_g, k_g)


def _identity(v):
    return v


def _to_bf16(v):
    return v.astype(BF16)


def _square_bf16(v):
    vf = v.astype(F32)
    return (vf * vf).astype(BF16)


def _weight_grad(name, a, b, ti, tj, tk, a_fn=_identity, b_fn=_identity, col0=0, n=None, after=None):
    t, m = a.shape
    n = b.shape[1] if n is None else n
    n_k = t // tk
    after = a if after is None else after

    def body(a_ref, b_ref, after_ref, o_ref, acc_ref):
        k = pl.program_id(2)

        @pl.when(k == 0)
        def _():
            acc_ref[...] = jnp.zeros_like(acc_ref)

        acc_ref[...] += _dot_tn(a_fn(a_ref[...]), b_fn(b_ref[...]))

        @pl.when(k == n_k - 1)
        def _():
            o_ref[...] = acc_ref[...].astype(BF16)

    return pl.pallas_call(
        body, name=name, grid=(m // ti, n // tj, n_k),
        in_specs=[pl.BlockSpec((tk, ti), lambda i, j, k: (k, i)),
                  pl.BlockSpec((tk, tj), lambda i, j, k: (k, j + col0 // tj)), _ANY],
        out_specs=pl.BlockSpec((ti, tj), lambda i, j, k: (i, j)),
        out_shape=jax.ShapeDtypeStruct((m, n), BF16),
        scratch_shapes=[pltpu.VMEM((ti, tj), F32)],
        compiler_params=_cparams(("arbitrary", "arbitrary", "arbitrary")))(a, b, after)


def _sum_slots(name, recv, own, transposed):
    m, n, k = recv.shape
    tc = min(k, 256)
    n_pad = -(-n // LANES) * LANES

    def body(own_ref, r_ref, o_ref):
        acc = own_ref[...].astype(F32)
        for i in range(m):
            acc = acc + r_ref[i].astype(F32)
        if transposed:
            if n_pad != n:
                acc = jnp.concatenate([acc, jnp.zeros((n_pad - n, tc), F32)], axis=0)
            acc = acc.T[:, :n]
        o_ref[...] = acc

    out_spec, out_shape = ((pl.BlockSpec((tc, n), lambda j: (j, 0)), (k, n)) if transposed
                           else (pl.BlockSpec((n, tc), lambda j: (0, j)), (n, k)))
    return pl.pallas_call(
        body, name=name, grid=(k // tc,),
        in_specs=[pl.BlockSpec((n, tc), lambda j: (0, j)), pl.BlockSpec((m, n, tc), lambda j: (0, 0, j))],
        out_specs=out_spec, out_shape=jax.ShapeDtypeStruct(out_shape, F32),
        compiler_params=_cparams(("arbitrary",)))(own, recv)


def _adamw_math(w, g, m, v):
    m = ADAM_B1 * m + (1.0 - ADAM_B1) * g
    v = ADAM_B2 * v + (1.0 - ADAM_B2) * (g * g)
    m_hat = m / (1.0 - ADAM_B1 ** ADAM_STEP)
    v_hat = v / (1.0 - ADAM_B2 ** ADAM_STEP)
    delta = -ADAM_LR * (m_hat / (jnp.sqrt(v_hat) + ADAM_EPS) + ADAM_WD * w)
    return delta, m, v


def _adamw(name, w, g, m, v):
    r, c = w.shape
    tr = max(t for t in range(8, min(r, 256) + 1, 8) if r % t == 0)

    def body(w_ref, g_ref, m_ref, v_ref, d_ref, mo_ref, vo_ref):
        d_ref[...], mo_ref[...], vo_ref[...] = _adamw_math(w_ref[...], g_ref[...], m_ref[...], v_ref[...])

    spec = pl.BlockSpec((tr, c), lambda i: (i, 0))
    return pl.pallas_call(
        body, name=name, grid=(r // tr,), in_specs=[spec] * 4, out_specs=[spec] * 3,
        out_shape=[jax.ShapeDtypeStruct((r, c), F32)] * 3,
        compiler_params=_cparams(("arbitrary",)))(w, g, m, v)


def _small_update(parts, w, m, v):
    def body(p_ref, w_ref, m_ref, v_ref, g_ref, d_ref, mo_ref, vo_ref):
        g = p_ref[0]
        for i in range(1, N_DEV):
            g = g + p_ref[i]
        g_ref[...] = g
        d_ref[...], mo_ref[...], vo_ref[...] = _adamw_math(w_ref[...], g, m_ref[...], v_ref[...])

    return pl.pallas_call(body, name="small_update", out_shape=[jax.ShapeDtypeStruct(w.shape, F32)] * 4)(
        parts, w, m, v)


def _pack_rows(vectors, n_rows):
    flat = jnp.concatenate([v.reshape(-1).astype(F32) for v in vectors])
    flat = jnp.pad(flat, (0, n_rows * LANES - flat.shape[0]))
    return flat.reshape(n_rows, LANES)


def _pick_tile(n, prefs):
    for t in prefs:
        if n % t == 0:
            return t
    return n


def kernel(x, p, norm_mix_g, w_in, b_gate, q_norm_g, k_norm_g, rel_bias, w_out_a, w_out_b, w_out, norm_mlp_g, w_ff1, w_ff2, norm_ple_g, w_ple_gate, w_ple, final_norm_g, loss_target, m_norm_mix_g, m_w_in, m_b_gate, m_q_norm_g, m_k_norm_g, m_rel_bias, m_w_out_a, m_w_out_b, m_w_out, m_norm_mlp_g, m_w_ff1, m_w_ff2, m_norm_ple_g, m_w_ple_gate, m_w_ple, m_final_norm_g, v_norm_mix_g, v_w_in, v_b_gate, v_q_norm_g, v_k_norm_g, v_rel_bias, v_w_out_a, v_w_out_b, v_w_out, v_norm_mlp_g, v_w_ff1, v_w_ff2, v_norm_ple_g, v_w_ple_gate, v_w_ple, v_final_norm_g):
    s, d = x.shape[1], x.shape[2]
    xs, ps, ts = x[0], p[0, 0], loss_target[0]
    tb = _pick_tile(s, (512, 256))
    tq = _pick_tile(s, (256,))
    tk = _pick_tile(s, (1024, 512))
    cb = _pick_tile(s, (1024, 512))
    fin_g = final_norm_g.reshape(1, d)

    col_sharded = {"w_in": w_in[0], "w_out_b": w_out_b[0], "w_ff1": w_ff1[0], "w_ple": w_ple[0]}
    row_sharded = {"w_out_a": w_out_a[0], "w_out": w_out[0], "w_ff2": w_ff2[0], "w_ple_gate": w_ple_gate[0]}
    order = ["w_in", "w_out_a", "w_out_b", "w_out", "w_ff1", "w_ff2", "w_ple_gate", "w_ple"]
    shards = [(col_sharded[n].T if n in col_sharded else row_sharded[n]).astype(BF16) for n in order]
    my_idx = 4 * lax.axis_index("x") + 2 * lax.axis_index("y") + lax.axis_index("c")
    (w_in_t,) = _all_gather(shards[:1], 1)
    zones = _place_own_rows(shards[1:], my_idx)
    ag = _copies_start("weights_gather_start", shards[1:], zones, w_in_t, True)

    tabs = _rope_tables(s)
    (h1, qraw, kraw, qrot, krot, va, qb, kb, vb, ga, gb) = _in_proj(
        xs, tabs, w_in_t, norm_mix_g, b_gate, q_norm_g, k_norm_g, tb, ag[4])
    oa, lse_a = _attn_a_fwd(qrot, krot, va, _pick_tile(s, (1024, 512, 256)), tk)
    _, (w_oa, w_ob_t, w_o, w_ff1_t, w_ff2_f, w_pg, w_p_t) = _copies_wait(
        "weights_gather_wait", ag[0], ag[1], ag[2], ag[3], lse_a, True)
    flat = lambda arrs: [a.reshape(s, GB_W) for a in arrs]
    split = lambda arrs: [a.reshape(dil, s // dil, GB_W) for a, dil in zip(arrs, DILATIONS)]
    qb_r, kb_r, vb_r = flat(qb), flat(kb), flat(vb)
    bmaps = [jnp.asarray(_bucket_map(dil)) for dil in DILATIONS]
    bias_tabs = [rel_bias[:, N_HEADS_PER_DIL * g:N_HEADS_PER_DIL * (g + 1)] for g in range(3)]
    band_out = [_band_fwd(dil, qb_r[g], kb_r[g], vb_r[g], bmaps[g], bias_tabs[g], cb)
                for g, dil in enumerate(DILATIONS)]
    og, lg = split([o for o, _ in band_out]), split([l for _, l in band_out])
    x2, ob, lse_b, ya, yb, u = _mix_out(xs, oa, og, lg, ga, gb, w_oa, w_ob_t, w_o, tb)
    tc = _pick_tile(w_ff1_t.shape[0], (512,))
    x3, r_act, h2 = _mlp_fwd(x2, w_ff1_t, w_ff2_f, norm_mlp_g, tb, tc)

    dx3, h3, dpre, dpe, pb, loss_part, dg_fin, dg_ple = _ple_loss(
        x3, ps, ts, w_pg, w_p_t, norm_ple_g, fin_g, tb)
    dx2, df, dg_mlp = _mlp_bwd(dx3, x2, r_act, w_ff1_t, w_ff2_f, norm_mlp_g, tb, tc)

    tkk = _pick_tile(s, (1024, 512))
    tk2 = _pick_tile(s, (2048, 1024, 512))
    dff = w_ff1_t.shape[0]
    t1k = lambda n: _pick_tile(n, (1024, 512, 256))
    slots = lambda parts: [lax.empty((7, a.shape[0] // N_DEV, a.shape[1]), BF16) for a in parts]
    part1 = [_weight_grad("grad_w_ff1", df, h2, t1k(dff), t1k(d), tkk),
             _weight_grad("grad_w_ff2", r_act, dx3, t1k(dff), t1k(d), tkk, a_fn=_square_bf16, b_fn=_to_bf16),
             _weight_grad("grad_w_ple_gate", h3, dpre, t1k(d), t1k(d), tk2),
             _weight_grad("grad_w_ple", dpe, pb, t1k(d), ps.shape[1], tk2)]
    doa, dob, dd, dga, dgb, dya, dyb, dbg = _mix_out_bwd(dx2, ya, yb, ga, gb, ob, w_oa, w_ob_t, w_o, tb, dx2)
    part1 += [_weight_grad("grad_w_out_a", oa, dya, t1k(QA_W), t1k(d), tk2),
              _weight_grad("grad_w_out_b", dyb, ob, t1k(d), GB_W, tk2),
              _weight_grad("grad_w_out", u, dx2, t1k(d), t1k(d), tkk, b_fn=_to_bf16)]
    rs1 = _copies_start("grads1_start", part1, slots(part1), doa, False)
    dqrot, dkrot, dva = _attn_a_bwd(qrot, krot, va, oa, doa, lse_a, _pick_tile(s, (512, 256)),
                                    _pick_tile(s, (512,)), rs1[4])
    dob_r, lse_r, dd_r = flat(dob), flat(lse_b), flat(dd)
    band_bwd = [_band_bwd(dil, qb_r[g], kb_r[g], vb_r[g], dob_r[g], lse_r[g], dd_r[g], bmaps[g], bias_tabs[g], cb)
                for g, dil in enumerate(DILATIONS)]
    dqb, dkb, dvb = [split([r[j] for r in band_bwd]) for j in range(3)]
    grad_x, dz, dg_mix, dg_q, dg_k = _in_proj_bwd(
        dx2, xs, dqrot, dkrot, dva, qraw, kraw, tabs, dqb, dkb, dvb, dga, dgb, w_in_t, norm_mix_g,
        q_norm_g, k_norm_g, _pick_tile(s, (256,)))
    d_rel = jnp.concatenate([r[3][:, :N_HEADS_PER_DIL] for r in band_bwd], axis=1)

    din = w_in_t.shape[0]
    ti_in = _pick_tile(din, (din // 2,)) if (din // 2) % LANES == 0 else din
    hd_ = d // 2
    part3 = [_weight_grad("grad_w_in_lo", dz, h1, ti_in, t1k(hd_), tkk, n=hd_)]
    rs3 = _copies_start("grads3_start", part3, slots(part3), grad_x, False)
    part4 = [_weight_grad("grad_w_in_hi", dz, h1, ti_in, t1k(hd_), tkk, col0=hd_, n=hd_, after=rs3[4])]
    rs4 = _copies_start("grads4_start", part4, slots(part4), rs3[4], False)

    def own_rows(a):
        n = a.shape[0] // N_DEV
        return lax.dynamic_slice(a, (my_idx * n, 0), (n, a.shape[1]))

    sums = {}
    src1, got1 = _copies_wait("grads1_wait", rs1[0], rs1[1], rs1[2], rs1[3], rs4[4], False)
    for n, a, r in zip(["w_ff1", "w_ff2", "w_ple_gate", "w_ple", "w_out_a", "w_out_b", "w_out"], src1, got1):
        sums[n] = _sum_slots("sum_" + n, r, own_rows(a), n in col_sharded)
    given_w = dict(w_in=w_in, w_out_a=w_out_a, w_out_b=w_out_b, w_out=w_out, w_ff1=w_ff1, w_ff2=w_ff2,
                   w_ple_gate=w_ple_gate, w_ple=w_ple)
    given_m = dict(w_in=m_w_in, w_out_a=m_w_out_a, w_out_b=m_w_out_b, w_out=m_w_out, w_ff1=m_w_ff1, w_ff2=m_w_ff2,
                   w_ple_gate=m_w_ple_gate, w_ple=m_w_ple)
    given_v = dict(w_in=v_w_in, w_out_a=v_w_out_a, w_out_b=v_w_out_b, w_out=v_w_out, w_ff1=v_w_ff1, w_ff2=v_w_ff2,
                   w_ple_gate=v_w_ple_gate, w_ple=v_w_ple)
    big = {}

    def update(n, transposed=False):
        view = (lambda a: a.T) if transposed else (lambda a: a)
        g = sums[n]
        delta, new_m, new_v = _adamw("adamw_" + n, view(given_w[n][0]), g, view(given_m[n][0]), view(given_v[n][0]))
        big[n] = tuple(view(a)[None] for a in (g, delta, new_m, new_v))

    for n in order[1:]:
        update(n)

    small_names = ["norm_mix_g", "b_gate", "q_norm_g", "k_norm_g", "rel_bias", "norm_mlp_g", "norm_ple_g",
                   "final_norm_g"]
    small_w = [norm_mix_g, b_gate, q_norm_g, k_norm_g, rel_bias, norm_mlp_g, norm_ple_g, final_norm_g]
    small_m = [m_norm_mix_g, m_b_gate, m_q_norm_g, m_k_norm_g, m_rel_bias, m_norm_mlp_g, m_norm_ple_g,
               m_final_norm_g]
    small_v = [v_norm_mix_g, v_b_gate, v_q_norm_g, v_k_norm_g, v_rel_bias, v_norm_mlp_g, v_norm_ple_g,
               v_final_norm_g]
    small_g = [dg_mix, dbg, dg_q, dg_k, d_rel, dg_mlp, dg_ple, dg_fin]
    sizes = [int(np.prod(w.shape)) for w in small_w]
    n_rows = -(-(sum(-(-sz // LANES) for sz in sizes) + 1) // 8) * 8
    pad = lambda v: jnp.pad(v.reshape(-1).astype(F32), (0, -v.size % LANES))
    pack = lambda vs, last: _pack_rows([pad(v) for v in vs] + [last], n_rows)
    zero_row = jnp.zeros((LANES,), F32)
    parts = _small_all_gather(pack(small_g, loss_part.reshape(-1) * (jnp.arange(LANES) == 0)), big["w_ple"][1])
    g_all, d_all, m_all, v_all = _small_update(parts, pack(small_w, zero_row), pack(small_m, zero_row),
                                               pack(small_v, zero_row))
    small = {}
    row = 0
    for n, w, sz in zip(small_names, small_w, sizes):
        nr = -(-sz // LANES)
        small[n] = tuple(a[row:row + nr].reshape(-1)[:sz].reshape(w.shape) for a in (g_all, d_all, m_all, v_all))
        row += nr
    loss = g_all[row, 0]

    src3, got3 = _copies_wait("grads3_wait", rs3[0], rs3[1], rs3[2], rs3[3], g_all, False)
    src4, got4 = _copies_wait("grads4_wait", rs4[0], rs4[1], rs4[2], rs4[3], g_all, False)
    sums["w_in"] = jnp.concatenate([_sum_slots("sum_w_in_lo", got3[0], own_rows(src3[0]), False),
                                    _sum_slots("sum_w_in_hi", got4[0], own_rows(src4[0]), False)], axis=1)
    update("w_in", transposed=True)

    names = ["norm_mix_g", "w_in", "b_gate", "q_norm_g", "k_norm_g", "rel_bias", "w_out_a", "w_out_b", "w_out",
             "norm_mlp_g", "w_ff1", "w_ff2", "norm_ple_g", "w_ple_gate", "w_ple", "final_norm_g"]
    res = {n: (big[n] if n in big else small[n]) for n in names}
    return (loss, grad_x[None], *[res[n][0] for n in names], *[res[n][1] for n in names],
            *[res[n][2] for n in names], *[res[n][3] for n in names])
```

```python
import math

import numpy as np
import jax
import jax.numpy as jnp
from jax import lax
from jax.experimental import pallas as pl
from jax.experimental.pallas import tpu as pltpu

F32 = jnp.float32
BF16 = jnp.bfloat16
MESH = pl.DeviceIdType.MESH

NORM_EPS = 1e-6
NEG_INF = -1e30
LOG2_E = math.log2(math.e)
LN_2 = math.log(2.0)
GRID_W = 64
ROPE_THETA = 10000.0
HEAD_DIM_A = 128
N_Q_HEADS_A = 8
N_KV_HEADS_A = 2
Q_PER_KV = N_Q_HEADS_A // N_KV_HEADS_A
HEAD_DIM_B = 64
N_HEADS_PER_DIL = 4
DILATIONS = (1, 4, 16)
BAND = 64
N_REL_BUCKETS = 32
REL_MAX_DIST = 1024
QA_W = N_Q_HEADS_A * HEAD_DIM_A
KA_W = N_KV_HEADS_A * HEAD_DIM_A
GB_W = N_HEADS_PER_DIL * HEAD_DIM_B
QB_W = GB_W * len(DILATIONS)
OFF_QA, OFF_KA, OFF_VA = 0, QA_W, QA_W + KA_W
OFF_QB = QA_W + 2 * KA_W
OFF_KB = OFF_QB + QB_W
OFF_VB = OFF_KB + QB_W
OFF_GA = OFF_VB + QB_W
N_DEV = 8
LANES = 128
VMEM_LIMIT = 56 * 2 ** 20

ADAM_LR, ADAM_B1, ADAM_B2, ADAM_EPS, ADAM_WD, ADAM_STEP = 0.001, 0.9, 0.999, 1e-08, 0.01, 10


def _cparams(sem):
    return pltpu.CompilerParams(dimension_semantics=sem, vmem_limit_bytes=VMEM_LIMIT)


def _resident(shape):
    nd = len(shape)
    return pl.BlockSpec(shape, lambda *_: (0,) * nd, pipeline_mode=pl.Buffered(1))


def _acc_spec(shape):
    nd = len(shape)
    return pl.BlockSpec(shape, lambda *_: (0,) * nd)


def _rows(tb, c):
    return pl.BlockSpec((tb, c), lambda i: (i, 0))


def _dil_shapes(s, dtype):
    return [jax.ShapeDtypeStruct((dil, s // dil, GB_W), dtype) for dil in DILATIONS]


def _dil_specs(tb):
    return [pl.BlockSpec((dil, tb // dil, GB_W), lambda i: (0, i, 0)) for dil in DILATIONS]


def _to_residues(val, out_ref, scr_ref, dil, dtype):
    if dil == 1:
        out_ref[0] = val.astype(dtype)
        return
    n = val.shape[0] // dil
    scr_ref[0] = val[:, :LANES]
    scr_ref[1] = val[:, LANES:]
    for r in range(dil):
        out_ref[r] = jnp.concatenate([scr_ref[0, pl.ds(r, n, stride=dil), :],
                                      scr_ref[1, pl.ds(r, n, stride=dil), :]], axis=1).astype(dtype)


def _from_residues(in_ref, scr_ref, dil):
    if dil == 1:
        return in_ref[0]
    n = in_ref.shape[1]
    for r in range(dil):
        v = in_ref[r]
        scr_ref[0, pl.ds(r, n, stride=dil), :] = v[:, :LANES]
        scr_ref[1, pl.ds(r, n, stride=dil), :] = v[:, LANES:]
    return jnp.concatenate([scr_ref[0], scr_ref[1]], axis=1)


def _dot_nt(a, b):
    return lax.dot_general(a, b, (((1,), (1,)), ((), ())), preferred_element_type=F32)


def _dot_nn(a, b):
    return lax.dot_general(a, b, (((1,), (0,)), ((), ())), preferred_element_type=F32)


def _dot_tn(a, b):
    return lax.dot_general(a, b, (((0,), (0,)), ((), ())), preferred_element_type=F32)


def _rstd(x):
    return lax.rsqrt(jnp.mean(x * x, axis=-1, keepdims=True) + NORM_EPS)


def _rms_bwd(dy, n, r, g):
    dn = dy * g
    return r * (dn - n * jnp.mean(dn * n, axis=-1, keepdims=True))


def _colsum(v):
    return jnp.sum(v, axis=0, keepdims=True)


def _sigmoid(v):
    return 1.0 / (1.0 + jnp.exp(-v))


def _rope_tables(s):
    half = HEAD_DIM_A // 2
    inv = np.power(np.float32(ROPE_THETA), -np.arange(0, half, 2, dtype=np.float32) / np.float32(half))
    t = np.arange(s)
    ang_r = (t // GRID_W).astype(np.float32)[:, None] * inv[None, :]
    ang_c = (t % GRID_W).astype(np.float32)[:, None] * inv[None, :]
    cr, sr, cc, sc = np.cos(ang_r), np.sin(ang_r), np.cos(ang_c), np.sin(ang_c)
    z = np.zeros_like(sr)
    cos = np.concatenate([cr, cr, cc, cc], axis=1)
    s1 = np.concatenate([z, sr, z, sc], axis=1)
    s2 = np.concatenate([-sr, z, -sc, z], axis=1)
    return [jnp.asarray(a, F32) for a in (cos, s1, s2)]


def _my_place():
    return lax.axis_index("x"), lax.axis_index("y"), lax.axis_index("c")


def _all_gather(shards, n_gather):
    n_all = len(shards)
    nw = n_gather

    def body(*refs):
        ins, outs = refs[:n_all], refs[n_all:2 * n_all]
        send_sems, recv_sems, local_sems = refs[2 * n_all:]
        x, y, c = _my_place()
        me, sibling = (x, y, c), (x, y, 1 - c)
        chips = [(1 - x, y), (x, 1 - y), (1 - x, 1 - y)]

        def rows(w, px, py, pc):
            n = ins[w].shape[0]
            return outs[w].at[pl.ds(pl.multiple_of((4 * px + 2 * py + pc) * n, 16), n), :]

        def copy(w, k, block, to, src=None):
            return pltpu.make_async_remote_copy(
                src_ref=rows(w, *block) if src is None else src, dst_ref=rows(w, *block),
                send_sem=send_sems.at[w, k], recv_sem=recv_sems.at[w, k], device_id=to, device_id_type=MESH)

        mine = [pltpu.make_async_copy(ins[w], rows(w, *me), local_sems.at[w]) for w in range(n_all)]
        for cp in mine:
            cp.start()
        first = []
        for w in range(nw):
            first.append(copy(w, 0, me, sibling, src=ins[w]))
            first += [copy(w, 1 + j, me, (*chip, c), src=ins[w]) for j, chip in enumerate(chips)]
        for cp in first:
            cp.start()
        passed = []
        for j, chip in enumerate(chips):
            for w in range(nw):
                copy(w, 1 + j, (*chip, c), me).wait_recv()
                fwd = copy(w, 4 + j, (*chip, c), sibling)
                fwd.start()
                passed.append(fwd)
        for w in range(nw):
            copy(w, 0, sibling, me).wait_recv()
        for j, chip in enumerate(chips):
            for w in range(nw):
                copy(w, 4 + j, (*chip, 1 - c), me).wait_recv()
        for cp in first + passed:
            cp.wait_send()
        for cp in mine:
            cp.wait()

    any_spec = pl.BlockSpec(memory_space=pl.ANY)
    return pl.pallas_call(
        body, name="weights_all_gather",
        out_shape=[jax.ShapeDtypeStruct((N_DEV * s.shape[0], s.shape[1]), s.dtype) for s in shards],
        in_specs=[any_spec] * n_all, out_specs=[any_spec] * n_all,
        scratch_shapes=[pltpu.SemaphoreType.DMA((nw, 7)), pltpu.SemaphoreType.DMA((nw, 7)),
                        pltpu.SemaphoreType.DMA((n_all,))],
    )(*shards)


def _place_own_rows(shards, my_idx):
    nw = len(shards)

    def body(idx_ref, *refs):
        for w in range(nw):
            refs[nw + w][...] = refs[w][...]

    grid_spec = pltpu.PrefetchScalarGridSpec(
        num_scalar_prefetch=1, grid=(1,),
        in_specs=[pl.BlockSpec(s.shape, lambda i, idx: (0, 0)) for s in shards],
        out_specs=[pl.BlockSpec(s.shape, lambda i, idx: (idx[0], 0)) for s in shards])
    return pl.pallas_call(
        body, name="place_own_rows", grid_spec=grid_spec,
        out_shape=[jax.ShapeDtypeStruct((N_DEV * s.shape[0], s.shape[1]), s.dtype) for s in shards],
        compiler_params=_cparams(("arbitrary",)))(my_idx.reshape(1).astype(jnp.int32), *shards)


_FLIPS = [(fx, fy, fc) for fx in (0, 1) for fy in (0, 1) for fc in (0, 1)][1:]


def _small_all_gather(v, after):
    def body(v_ref, after_ref, out_ref, send_sems, recv_sems):
        x, y, c = _my_place()
        my_idx = 4 * x + 2 * y + c
        out_ref[my_idx] = v_ref[...]
        sends = []
        for k, (fx, fy, fc) in enumerate(_FLIPS):
            to = (1 - x if fx else x, 1 - y if fy else y, 1 - c if fc else c)
            sends.append(pltpu.make_async_remote_copy(
                src_ref=v_ref, dst_ref=out_ref.at[my_idx], send_sem=send_sems.at[k], recv_sem=recv_sems.at[k],
                device_id=to, device_id_type=MESH))
        for cp in sends:
            cp.start()
        for k, (fx, fy, fc) in enumerate(_FLIPS):
            frm_idx = 4 * (1 - x if fx else x) + 2 * (1 - y if fy else y) + (1 - c if fc else c)
            pltpu.make_async_remote_copy(
                src_ref=v_ref, dst_ref=out_ref.at[frm_idx], send_sem=send_sems.at[k], recv_sem=recv_sems.at[k],
                device_id=(x, y, c), device_id_type=MESH).wait_recv()
        for cp in sends:
            cp.wait_send()

    vm = pl.BlockSpec(memory_space=pltpu.VMEM)
    return pl.pallas_call(
        body, name="small_all_gather", out_shape=jax.ShapeDtypeStruct((N_DEV,) + v.shape, v.dtype),
        in_specs=[vm, pl.BlockSpec(memory_space=pl.ANY)], out_specs=vm,
        scratch_shapes=[pltpu.SemaphoreType.DMA((7,)), pltpu.SemaphoreType.DMA((7,))],
    )(v, after)


_HBM = pl.BlockSpec(memory_space=pltpu.HBM)
_SEM = pl.BlockSpec(memory_space=pltpu.SEMAPHORE)
_ANY = pl.BlockSpec(memory_space=pl.ANY)
_SPLIT_COPY = dict(has_side_effects=pltpu.SideEffectType.DATAFLOW_SIDE_EFFECTING)


def _peer(x, y, c, k):
    fx, fy, fc = _FLIPS[k]
    return (1 - x if fx else x, 1 - y if fy else y, 1 - c if fc else c)


def _in_hbm(a):
    return pltpu.with_memory_space_constraint(a, pltpu.HBM)


def _split_copies(srcs, lands, send_sems, recv_sems, gather, arriving):
    x, y, c = _my_place()
    my_idx = 4 * x + 2 * y + c
    out = []
    for k in range(7):
        to = _peer(x, y, c, k)
        to_idx = 4 * to[0] + 2 * to[1] + to[2]
        for w in range(len(srcs)):
            if gather:
                n = srcs[w].shape[0]
                src = srcs[w]
                dst = lands[w].at[pl.ds(pl.multiple_of((to_idx if arriving else my_idx) * n, 16), n), :]
            else:
                n = lands[w].shape[1]
                src = srcs[w].at[pl.ds(pl.multiple_of(to_idx * n, 16), n), :]
                dst = lands[w].at[k]
            out.append(pltpu.make_async_remote_copy(
                src_ref=src, dst_ref=dst, send_sem=send_sems.at[7 * w + k], recv_sem=recv_sems.at[7 * w + k],
                device_id=to, device_id_type=MESH))
    return out


def _copies_start(name, srcs, lands, after, gather):
    nw = len(srcs)

    def body(*refs):
        send_sems, recv_sems = refs[2 * nw + 1], refs[2 * nw + 2]
        for cp in _split_copies(refs[:nw], refs[nw:2 * nw], send_sems, recv_sems, gather, False):
            cp.start()
        refs[-1][...] = jnp.zeros_like(refs[-1])

    sems = pltpu.SemaphoreType.DMA((7 * nw,))
    thru = [pltpu.HBM(a.shape, a.dtype) for a in list(srcs) + list(lands)]
    res = pl.pallas_call(
        body, name=name, out_shape=(sems, sems, *thru, jax.ShapeDtypeStruct((8, LANES), F32)),
        in_specs=[_HBM] * (2 * nw) + [_ANY], out_specs=(_SEM, _SEM, *[_HBM] * (2 * nw), pl.BlockSpec(memory_space=pltpu.VMEM)),
        input_output_aliases={i: 2 + i for i in range(2 * nw)},
        compiler_params=pltpu.CompilerParams(**_SPLIT_COPY),
    )(*[_in_hbm(a) for a in srcs], *[_in_hbm(a) for a in lands], after)
    return res[0], res[1], list(res[2:2 + nw]), list(res[2 + nw:2 + 2 * nw]), res[-1]


def _copies_wait(name, send_sems, recv_sems, srcs, lands, after, gather):
    nw = len(srcs)

    def body(*refs):
        for cp in _split_copies(refs[:nw], refs[nw:2 * nw], refs[2 * nw], refs[2 * nw + 1], gather, False):
            cp.wait_send()
        for cp in _split_copies(refs[:nw], refs[nw:2 * nw], refs[2 * nw], refs[2 * nw + 1], gather, True):
            cp.wait_recv()

    thru = [pltpu.HBM(a.shape, a.dtype) for a in list(srcs) + list(lands)]
    res = pl.pallas_call(
        body, name=name, out_shape=tuple(thru),
        in_specs=[_HBM] * (2 * nw) + [_SEM, _SEM, _ANY], out_specs=tuple([_HBM] * (2 * nw)),
        input_output_aliases={i: i for i in range(2 * nw)},
        compiler_params=pltpu.CompilerParams(**_SPLIT_COPY),
    )(*srcs, *lands, send_sems, recv_sems, after)
    return list(res[:nw]), list(res[nw:])


def _in_proj(x, tabs, w_in_t, g_mix, b_gate, q_g, k_g, tb, after):
    s, d = x.shape
    n_gate_chunks = d // 256
    q_scale = HEAD_DIM_A ** -0.5 * LOG2_E
    b_scale = HEAD_DIM_B ** -0.5 * LOG2_E

    def body(x_ref, c_ref, s1_ref, s2_ref, w_ref, gmix_ref, bg_ref, qg_ref, kg_ref, after_ref,
             h1_ref, qraw_ref, kraw_ref, qrot_ref, krot_ref, va_ref, *rest):
        qb_refs, kb_refs, vb_refs = rest[0:3], rest[3:6], rest[6:9]
        ga_ref, gb_ref, scr_ref = rest[9:]
        xv = x_ref[...]
        hb = (xv * _rstd(xv) * gmix_ref[...]).astype(BF16)
        h1_ref[...] = hb
        cos, s1, s2 = c_ref[...], s1_ref[...], s2_ref[...]

        def proj(lo, width):
            return _dot_nt(hb, w_ref[lo:lo + width, :])

        def norm_rope(z, g):
            n = z * _rstd(z) * g
            return n * cos + pltpu.roll(n, 32, 1) * s1 + pltpu.roll(n, 96, 1) * s2

        for j in range(QA_W // 256):
            z = proj(OFF_QA + 256 * j, 256)
            qraw_ref[:, 256 * j:256 * j + 256] = z
            for hh in range(2):
                lo = 256 * j + 128 * hh
                qrot_ref[:, lo:lo + 128] = (norm_rope(z[:, 128 * hh:128 * hh + 128], qg_ref[...]) * q_scale).astype(BF16)
        z = proj(OFF_KA, 256)
        kraw_ref[...] = z
        for hh in range(2):
            krot_ref[:, 128 * hh:128 * hh + 128] = norm_rope(z[:, 128 * hh:128 * hh + 128], kg_ref[...]).astype(BF16)
        va_ref[...] = proj(OFF_VA, 256).astype(BF16)
        for g, dil in enumerate(DILATIONS):
            _to_residues(proj(OFF_QB + GB_W * g, GB_W) * b_scale, qb_refs[g], scr_ref, dil, BF16)
            _to_residues(proj(OFF_KB + GB_W * g, GB_W), kb_refs[g], scr_ref, dil, BF16)
            _to_residues(proj(OFF_VB + GB_W * g, GB_W), vb_refs[g], scr_ref, dil, BF16)
        for j in range(n_gate_chunks):
            sl = slice(256 * j, 256 * j + 256)
            ga_ref[:, sl] = _sigmoid(proj(OFF_GA + 256 * j, 256) + bg_ref[:, sl]).astype(BF16)
            gb_ref[:, sl] = _sigmoid(
                proj(OFF_GA + d + 256 * j, 256) + bg_ref[:, d + 256 * j:d + 256 * j + 256]).astype(BF16)

    sd = jax.ShapeDtypeStruct
    outs = [sd((s, d), BF16), sd((s, QA_W), F32), sd((s, KA_W), F32), sd((s, QA_W), BF16), sd((s, KA_W), BF16),
            sd((s, KA_W), BF16)] + _dil_shapes(s, BF16) * 3 + [sd((s, d), BF16), sd((s, d), BF16)]
    out_specs = [_rows(tb, d), _rows(tb, QA_W), _rows(tb, KA_W), _rows(tb, QA_W), _rows(tb, KA_W), _rows(tb, KA_W)
                 ] + _dil_specs(tb) * 3 + [_rows(tb, d), _rows(tb, d)]
    in_specs = [_rows(tb, d), _rows(tb, LANES), _rows(tb, LANES), _rows(tb, LANES), _resident(w_in_t.shape),
                _resident(g_mix.shape), _resident(b_gate.shape), _resident(q_g.shape), _resident(k_g.shape), _ANY]
    res = list(pl.pallas_call(body, name="in_proj", grid=(s // tb,), in_specs=in_specs, out_specs=out_specs,
                              out_shape=outs, scratch_shapes=[pltpu.VMEM((2, tb, LANES), F32)],
                              compiler_params=_cparams(("arbitrary",)))(
        x, *tabs, w_in_t, g_mix, b_gate, q_g, k_g, after))
    return res[:6] + [res[6:9], res[9:12], res[12:15]] + res[15:]


def _attn_a_fwd(qrot, krot, va, tq, tk):
    s = qrot.shape[0]
    n_kv = s // tk
    gw = Q_PER_KV * HEAD_DIM_A

    def body(q_ref, k_ref, v_ref, o_ref, lse_ref):
        q4 = jnp.concatenate([q_ref[:, 128 * h:128 * h + 128] for h in range(Q_PER_KV)], axis=0)

        def step(j, carry):
            m, l, acc = carry
            sl = pl.ds(pl.multiple_of(j * tk, tk), tk)
            kj, vj = k_ref[sl, :], v_ref[sl, :]
            sc = _dot_nt(kj, q4)
            m_new = jnp.maximum(m, jnp.max(sc, axis=0, keepdims=True))
            p = jnp.exp2(sc - m_new)
            alpha = jnp.exp2(m - m_new)
            l = alpha * l + jnp.sum(p, axis=0, keepdims=True)
            acc = alpha * acc + _dot_tn(vj, p.astype(BF16))
            return m_new, l, acc

        rows = Q_PER_KV * tq
        m, l, acc = lax.fori_loop(0, n_kv, step, (jnp.full((1, rows), NEG_INF, F32), jnp.zeros((1, rows), F32),
                                                  jnp.zeros((HEAD_DIM_A, rows), F32)))
        o = (acc / l).T
        lse = m + jnp.log2(l)
        for h in range(Q_PER_KV):
            o_ref[:, 128 * h:128 * h + 128] = o[h * tq:(h + 1) * tq].astype(BF16)
            lse_ref[0, h:h + 1, :] = lse[:, h * tq:(h + 1) * tq]

    return pl.pallas_call(
        body, name="attn_a_fwd", grid=(N_KV_HEADS_A, s // tq),
        in_specs=[pl.BlockSpec((tq, gw), lambda g, i: (i, g)),
                  pl.BlockSpec((s, HEAD_DIM_A), lambda g, i: (0, g)),
                  pl.BlockSpec((s, HEAD_DIM_A), lambda g, i: (0, g))],
        out_specs=[pl.BlockSpec((tq, gw), lambda g, i: (i, g)),
                   pl.BlockSpec((1, Q_PER_KV, tq), lambda g, i: (g, 0, i))],
        out_shape=[jax.ShapeDtypeStruct((s, QA_W), BF16), jax.ShapeDtypeStruct((N_KV_HEADS_A, Q_PER_KV, s), F32)],
        compiler_params=_cparams(("arbitrary", "arbitrary")))(qrot, krot, va)


def _attn_a_bwd(qrot, krot, va, oa, doa, lse, tq, tk, after):
    s = qrot.shape[0]
    n_kv = s // tk
    gw = Q_PER_KV * HEAD_DIM_A

    def body(q_ref, do_ref, o_ref, lse_ref, k_ref, v_ref, after_ref, dq_ref, dk_ref, dv_ref):
        @pl.when(pl.program_id(1) == 0)
        def _():
            dk_ref[...] = jnp.zeros_like(dk_ref)
            dv_ref[...] = jnp.zeros_like(dv_ref)

        def stack(ref):
            return jnp.concatenate([ref[:, 128 * h:128 * h + 128] for h in range(Q_PER_KV)], axis=0)

        q4, do4, o4 = stack(q_ref), stack(do_ref), stack(o_ref)
        delta = jnp.sum(do4.astype(F32) * o4.astype(F32), axis=-1, keepdims=True)
        lse_cols = jnp.concatenate([lse_ref[0], jnp.zeros_like(lse_ref[0])], axis=0).T
        lse4 = jnp.concatenate([lse_cols[:, h:h + 1] for h in range(Q_PER_KV)], axis=0)

        def step(j, dq):
            sl = pl.ds(pl.multiple_of(j * tk, tk), tk)
            kj, vj = k_ref[sl, :], v_ref[sl, :]
            p = jnp.exp2(_dot_nt(q4, kj) - lse4)
            ds = (p * (_dot_nt(do4, vj) - delta)).astype(BF16)
            dk_ref[sl, :] += _dot_tn(ds, q4)
            dv_ref[sl, :] += _dot_tn(p.astype(BF16), do4)
            return dq + _dot_nn(ds, kj)

        dq = lax.fori_loop(0, n_kv, step, jnp.zeros((Q_PER_KV * tq, HEAD_DIM_A), F32))
        for h in range(Q_PER_KV):
            dq_ref[:, 128 * h:128 * h + 128] = dq[h * tq:(h + 1) * tq]

    qspec = pl.BlockSpec((tq, gw), lambda g, i: (i, g))
    kspec = pl.BlockSpec((s, HEAD_DIM_A), lambda g, i: (0, g))
    return pl.pallas_call(
        body, name="attn_a_bwd", grid=(N_KV_HEADS_A, s // tq),
        in_specs=[qspec, qspec, qspec, pl.BlockSpec((1, Q_PER_KV, tq), lambda g, i: (g, 0, i)), kspec, kspec, _ANY],
        out_specs=[qspec, kspec, kspec],
        out_shape=[jax.ShapeDtypeStruct((s, QA_W), F32), jax.ShapeDtypeStruct((s, KA_W), F32),
                   jax.ShapeDtypeStruct((s, KA_W), F32)],
        compiler_params=_cparams(("arbitrary", "arbitrary")))(qrot, doa, oa, lse, krot, va, after)


BAND_QB = 128
BAND_WIN = BAND_QB + 2 * BAND


def _band_specs(s, cb):
    per = cb // BAND
    last = s // BAND - 1
    cur = pl.BlockSpec((cb, GB_W), lambda i: (i, 0))
    prev = pl.BlockSpec((BAND, GB_W), lambda i: (jnp.maximum(i * per - 1, 0), 0))
    nxt = pl.BlockSpec((BAND, GB_W), lambda i: (jnp.minimum(i * per + per, last), 0))
    return cur, prev, nxt


def _window(prev_ref, cur_ref, next_ref):
    return jnp.concatenate([prev_ref[...], cur_ref[...], next_ref[...]], axis=0)


def _band_mask(base, seg_shift):
    rq = base + lax.broadcasted_iota(jnp.int32, (BAND_QB, BAND_WIN), 0)
    rk = base - BAND + lax.broadcasted_iota(jnp.int32, (BAND_QB, BAND_WIN), 1)
    same_segment = lax.shift_right_arithmetic(rq, jnp.int32(seg_shift)) == lax.shift_right_arithmetic(rk, jnp.int32(seg_shift))
    return (jnp.abs(rk - rq) <= BAND) & same_segment


def _build_bias(bmap_ref, tab_ref, bias_ref):
    bm = bmap_ref[...]
    acc = [jnp.full(bm.shape, NEG_INF, F32) for _ in range(N_HEADS_PER_DIL)]
    for b in range(N_REL_BUCKETS):
        hit = bm == b
        for h in range(N_HEADS_PER_DIL):
            acc[h] = jnp.where(hit, tab_ref[b, h] * LOG2_E, acc[h])
    rows = bm.shape[0]
    for h in range(N_HEADS_PER_DIL):
        bias_ref[h * rows:(h + 1) * rows, :] = acc[h]


def _segment_mask(base, seg_len, seg_shift):
    if seg_len % BAND_QB:
        return _band_mask(base, seg_shift)
    pos = lax.rem(base, seg_len)
    w = lax.broadcasted_iota(jnp.int32, (1, BAND_WIN), 1)
    return ((w >= BAND) | (pos != 0)) & ((w < BAND + BAND_QB) | (pos != seg_len - BAND_QB))


def _head_lane_masks():
    lane = lax.broadcasted_iota(jnp.int32, (1, LANES), 1)
    return [lane < HEAD_DIM_B, lane >= HEAD_DIM_B]


def _rows4(mask):
    return mask if mask.shape[0] == 1 else jnp.concatenate([mask] * N_HEADS_PER_DIL, axis=0)


def _head_scores(a, b):
    hm = _head_lane_masks()
    out = []
    for hp in range(2):
        ls = slice(LANES * hp, LANES * hp + LANES)
        ah = a[:, ls]
        both = jnp.concatenate([jnp.where(hm[0], ah, jnp.zeros_like(ah)), jnp.where(hm[1], ah, jnp.zeros_like(ah))],
                               axis=0)
        out.append(_dot_nt(both, b[:, ls]))
    return jnp.concatenate(out, axis=0)


def _head_combine(p, v, scale=None, transposed=False):
    hm = _head_lane_masks()
    rows = p.shape[0] // N_HEADS_PER_DIL
    halves = []
    for hp in range(2):
        vh = v[:, LANES * hp:LANES * hp + LANES]
        acc = None
        for hh in range(2):
            h = 2 * hp + hh
            ph = p[h * rows:(h + 1) * rows]
            vm = jnp.where(hm[hh], vh, jnp.zeros_like(vh))
            t = _dot_tn(ph, vm) if transposed else _dot_nn(ph, vm)
            if scale is not None:
                t = t * scale[h * rows:(h + 1) * rows]
            acc = t if acc is None else acc + t
        halves.append(acc)
    return jnp.concatenate(halves, axis=1)


def _head_spread(col):
    rows = col.shape[0] // N_HEADS_PER_DIL
    lane = lax.broadcasted_iota(jnp.int32, (1, GB_W), 1)
    out = jnp.zeros((rows, GB_W), F32)
    for h in range(N_HEADS_PER_DIL):
        out = jnp.where((lane >= HEAD_DIM_B * h) & (lane < HEAD_DIM_B * (h + 1)), col[h * rows:(h + 1) * rows], out)
    return out


def _head_cols(v):
    return jnp.concatenate([v[:, HEAD_DIM_B * h:HEAD_DIM_B * h + 1] for h in range(N_HEADS_PER_DIL)], axis=0)


def _seg_shift(s, dil):
    seg = s // dil
    assert seg & (seg - 1) == 0, "segment length must be a power of two"
    return seg.bit_length() - 1


def _band_fwd(dil, qb, kb, vb, bmap, tab, cb):
    s = qb.shape[0]
    shift = _seg_shift(s, dil)

    def body(q_ref, kp_ref, kc_ref, kn_ref, vp_ref, vc_ref, vn_ref, bmap_ref, tab_ref, o_ref, lse_ref, bias_ref):
        @pl.when(pl.program_id(0) == 0)
        def _():
            _build_bias(bmap_ref, tab_ref, bias_ref)

        kw, vw = _window(kp_ref, kc_ref, kn_ref), _window(vp_ref, vc_ref, vn_ref)
        for jj in range(cb // BAND_QB):
            r0 = BAND_QB * jj
            mask = _rows4(_segment_mask(pl.program_id(0) * cb + r0, s // dil, shift))
            sc = _head_scores(q_ref[r0:r0 + BAND_QB, :], kw[r0:r0 + BAND_WIN, :]) + bias_ref[...]
            sc = jnp.where(mask, sc, NEG_INF)
            m = jnp.max(sc, axis=-1, keepdims=True)
            e = jnp.exp2(sc - m)
            l = jnp.sum(e, axis=-1, keepdims=True)
            o = _head_combine(e.astype(BF16), vw[r0:r0 + BAND_WIN, :], 1.0 / l)
            o_ref[r0:r0 + BAND_QB, :] = o
            lse_ref[r0:r0 + BAND_QB, :] = _head_spread(m + jnp.log2(l))

    cur, prev, nxt = _band_specs(s, cb)
    return pl.pallas_call(
        body, name=f"band_fwd_d{dil}", grid=(s // cb,),
        in_specs=[cur, prev, cur, nxt, prev, cur, nxt, _resident(bmap.shape), pl.BlockSpec(memory_space=pltpu.SMEM)],
        out_specs=[cur, cur],
        out_shape=[jax.ShapeDtypeStruct(qb.shape, F32), jax.ShapeDtypeStruct(qb.shape, F32)],
        scratch_shapes=[pltpu.VMEM((N_HEADS_PER_DIL * BAND_QB, BAND_WIN), F32)],
        compiler_params=_cparams(("arbitrary",)))(qb, kb, kb, kb, vb, vb, vb, bmap, tab)


def _band_bwd(dil, qb, kb, vb, dob, lse, dd, bmap, tab, cb):
    s = qb.shape[0]
    shift = _seg_shift(s, dil)
    n_steps = s // cb

    def body(q_ref, do_ref, lse_ref, dd_ref, kp_ref, kc_ref, kn_ref, vp_ref, vc_ref, vn_ref, bmap_ref, tab_ref,
             dq_ref, dk_ref, dv_ref, dtab_ref, bias_ref, dsum_ref):
        @pl.when(pl.program_id(0) == 0)
        def _():
            _build_bias(bmap_ref, tab_ref, bias_ref)
            dsum_ref[...] = jnp.zeros_like(dsum_ref)
            dk_ref[...] = jnp.zeros_like(dk_ref)
            dv_ref[...] = jnp.zeros_like(dv_ref)

        kw, vw = _window(kp_ref, kc_ref, kn_ref), _window(vp_ref, vc_ref, vn_ref)
        for jj in range(cb // BAND_QB):
            r0 = BAND_QB * jj
            base = pl.program_id(0) * cb + r0
            mask = _rows4(_segment_mask(base, s // dil, shift))
            qh, doh = q_ref[r0:r0 + BAND_QB, :], do_ref[r0:r0 + BAND_QB, :]
            k3, v3 = kw[r0:r0 + BAND_WIN, :], vw[r0:r0 + BAND_WIN, :]
            sc = _head_scores(qh, k3) + bias_ref[...]
            sc = jnp.where(mask, sc, NEG_INF)
            p = jnp.exp2(sc - _head_cols(lse_ref[r0:r0 + BAND_QB, :]))
            dp = _head_scores(doh, v3)
            ds = p * (dp - _head_cols(dd_ref[r0:r0 + BAND_QB, :]))
            dsum_ref[...] += ds
            dsb = ds.astype(BF16)
            dq_ref[r0:r0 + BAND_QB, :] = _head_combine(dsb, k3)
            dk_win = _head_combine(dsb, qh, transposed=True)
            dv_win = _head_combine(p.astype(BF16), doh, transposed=True)
            own = pl.ds(pl.multiple_of(base, BAND), BAND_QB)
            dk_ref[own, :] += dk_win[BAND:BAND + BAND_QB]
            dv_ref[own, :] += dv_win[BAND:BAND + BAND_QB]

            @pl.when(base > 0)
            def _():
                before = pl.ds(pl.multiple_of(base - BAND, BAND), BAND)
                dk_ref[before, :] += dk_win[:BAND]
                dv_ref[before, :] += dv_win[:BAND]

            @pl.when(base + BAND_QB < s)
            def _():
                after = pl.ds(pl.multiple_of(base + BAND_QB, BAND), BAND)
                dk_ref[after, :] += dk_win[BAND + BAND_QB:]
                dv_ref[after, :] += dv_win[BAND + BAND_QB:]

        @pl.when(pl.program_id(0) == n_steps - 1)
        def _():
            bm = bmap_ref[...]
            lane = lax.broadcasted_iota(jnp.int32, (1, LANES), 1)
            for b in range(N_REL_BUCKETS):
                hit = bm == b
                row = jnp.zeros((1, LANES), F32)
                for h in range(N_HEADS_PER_DIL):
                    part = dsum_ref[h * BAND_QB:(h + 1) * BAND_QB, :]
                    row = jnp.where(lane == h, jnp.sum(jnp.where(hit, part, 0.0)), row)
                dtab_ref[b:b + 1, :] = row

    cur, prev, nxt = _band_specs(s, cb)
    whole = _acc_spec(qb.shape)
    return pl.pallas_call(
        body, name=f"band_bwd_d{dil}", grid=(n_steps,),
        in_specs=[cur, cur, cur, cur, prev, cur, nxt, prev, cur, nxt, _resident(bmap.shape),
                  pl.BlockSpec(memory_space=pltpu.SMEM)],
        out_specs=[cur, whole, whole, _acc_spec((N_REL_BUCKETS, LANES))],
        out_shape=[jax.ShapeDtypeStruct(qb.shape, F32)] * 3 + [jax.ShapeDtypeStruct((N_REL_BUCKETS, LANES), F32)],
        scratch_shapes=[pltpu.VMEM((N_HEADS_PER_DIL * BAND_QB, BAND_WIN), F32),
                        pltpu.VMEM((N_HEADS_PER_DIL * BAND_QB, BAND_WIN), F32)],
        compiler_params=_cparams(("arbitrary",)))(qb, dob, lse, dd, kb, kb, kb, vb, vb, vb, bmap, tab)


def _t5_bucket(rel):
    nb = N_REL_BUCKETS // 2
    ret = (rel > 0).astype(np.int32) * nb
    n = np.abs(rel)
    max_exact = nb // 2
    large = max_exact + (np.log(np.maximum(n, 1) / max_exact) / math.log(REL_MAX_DIST / max_exact)
                         * (nb - max_exact)).astype(np.int32)
    large = np.minimum(large, nb - 1)
    return ret + np.where(n < max_exact, n, large).astype(np.int32)


def _bucket_map(dil):
    off = np.arange(BAND_WIN)[None, :] - BAND - np.arange(BAND_QB)[:, None]
    return np.where(np.abs(off) <= BAND, _t5_bucket(off * dil), -1).astype(np.int32)


def _seg_sum(v):
    lane = lax.broadcasted_iota(jnp.int32, (1, v.shape[1]), 1)
    out = jnp.zeros_like(v)
    for h in range(v.shape[1] // HEAD_DIM_B):
        m = (lane >= HEAD_DIM_B * h) & (lane < HEAD_DIM_B * (h + 1))
        out = jnp.where(m, jnp.sum(jnp.where(m, v, 0.0), axis=-1, keepdims=True), out)
    return out


def _mix_out(x, oa, og, lg, ga, gb, w_oa, w_ob_t, w_o, tb):
    s, d = x.shape

    def body(x_ref, oa_ref, og0_ref, og1_ref, og2_ref, lg0_ref, lg1_ref, lg2_ref, ga_ref, gb_ref,
             woa_ref, wob_ref, wo_ref, x2_ref, ob_ref, lse0_ref, lse1_ref, lse2_ref, ya_ref, yb_ref, u_ref, scr_ref):
        og_refs, lg_refs = (og0_ref, og1_ref, og2_ref), (lg0_ref, lg1_ref, lg2_ref)
        l0, l1, l2 = [_from_residues(lg_refs[g], scr_ref, dil) for g, dil in enumerate(DILATIONS)]
        lmax = jnp.maximum(jnp.maximum(l0, l1), l2)
        w0, w1, w2 = jnp.exp2(l0 - lmax), jnp.exp2(l1 - lmax), jnp.exp2(l2 - lmax)
        den = w0 + w1 + w2
        o0, o1, o2 = [_from_residues(og_refs[g], scr_ref, dil) for g, dil in enumerate(DILATIONS)]
        ob = ((w0 * o0 + w1 * o1 + w2 * o2) / den).astype(BF16)
        ob_ref[...] = ob
        lse = lmax + jnp.log2(den)
        for g, (dil, ref) in enumerate(zip(DILATIONS, (lse0_ref, lse1_ref, lse2_ref))):
            _to_residues(lse, ref, scr_ref, dil, F32)
        ya = _dot_nn(oa_ref[...], woa_ref[...])
        yb = _dot_nt(ob, wob_ref[...])
        ya_ref[...] = ya.astype(BF16)
        yb_ref[...] = yb.astype(BF16)
        u = (ga_ref[...].astype(F32) * ya + gb_ref[...].astype(F32) * yb).astype(BF16)
        u_ref[...] = u
        x2_ref[...] = x_ref[...] + _dot_nn(u, wo_ref[...])

    sd = jax.ShapeDtypeStruct
    res = list(pl.pallas_call(
        body, name="mix_out", grid=(s // tb,),
        in_specs=[_rows(tb, d), _rows(tb, QA_W)] + _dil_specs(tb) * 2 + [
            _rows(tb, d), _rows(tb, d), _resident(w_oa.shape), _resident(w_ob_t.shape), _resident(w_o.shape)],
        out_specs=[_rows(tb, d), _rows(tb, GB_W)] + _dil_specs(tb) + [_rows(tb, d), _rows(tb, d), _rows(tb, d)],
        out_shape=[sd((s, d), F32), sd((s, GB_W), BF16)] + _dil_shapes(s, F32) + [
            sd((s, d), BF16), sd((s, d), BF16), sd((s, d), BF16)],
        scratch_shapes=[pltpu.VMEM((2, tb, LANES), F32)],
        compiler_params=_cparams(("arbitrary",)))(x, oa, *og, *lg, ga, gb, w_oa, w_ob_t, w_o))
    return res[:2] + [res[2:5]] + res[5:]


def _mlp_fwd(x2, w1_t, w2, g_mlp, tb, tc):
    s, d = x2.shape
    dff = w1_t.shape[0]

    def body(x_ref, w1_ref, w2_ref, g_ref, x3_ref, r_ref, h_ref):
        xv = x_ref[...]
        hb = (xv * _rstd(xv) * g_ref[...]).astype(BF16)
        h_ref[...] = hb
        x3_ref[...] = xv
        for c in range(dff // tc):
            sl = slice(tc * c, tc * c + tc)
            r = jnp.maximum(_dot_nt(hb, w1_ref[sl, :]), 0.0)
            r_ref[:, sl] = r.astype(BF16)
            x3_ref[...] += _dot_nn((r * r).astype(BF16), w2_ref[sl, :])

    sd = jax.ShapeDtypeStruct
    return pl.pallas_call(
        body, name="mlp_fwd", grid=(s // tb,),
        in_specs=[_rows(tb, d), _resident(w1_t.shape), _resident(w2.shape), _resident(g_mlp.shape)],
        out_specs=[_rows(tb, d), _rows(tb, dff), _rows(tb, d)],
        out_shape=[sd((s, d), F32), sd((s, dff), BF16), sd((s, d), BF16)],
        compiler_params=_cparams(("arbitrary",)))(x2, w1_t, w2, g_mlp)


def _ple_loss(x3, p, target, w_pg, w_p_t, g_ple, g_fin, tb):
    s, d = x3.shape
    dp = p.shape[1]

    def body(x_ref, p_ref, t_ref, wpg_ref, wp_ref, gple_ref, gfin_ref,
             dx3_ref, h3_ref, dpre_ref, dpe_ref, pb_ref, loss_ref, dgfin_ref, dgple_ref):
        @pl.when(pl.program_id(0) == 0)
        def _():
            loss_ref[...] = jnp.zeros_like(loss_ref)
            dgfin_ref[...] = jnp.zeros_like(dgfin_ref)
            dgple_ref[...] = jnp.zeros_like(dgple_ref)

        x3v = x_ref[...]
        r3 = _rstd(x3v)
        n3 = x3v * r3
        h3 = (n3 * gple_ref[...]).astype(BF16)
        h3_ref[...] = h3
        gp = _sigmoid(_dot_nn(h3, wpg_ref[...]))
        pb = p_ref[...].astype(BF16)
        pb_ref[...] = pb
        pe = _dot_nt(pb, wp_ref[...])
        x4 = x3v + gp * pe
        r4 = _rstd(x4)
        n4 = x4 * r4
        err = n4 * gfin_ref[...] - t_ref[...]
        loss_ref[...] += jnp.sum(0.5 * jnp.mean(err * err, axis=-1, keepdims=True), axis=0, keepdims=True)
        dy = err * (1.0 / d)
        dgfin_ref[...] += _colsum(dy * n4)
        dx4 = _rms_bwd(dy, n4, r4, gfin_ref[...])
        dpe_ref[...] = (dx4 * gp).astype(BF16)
        dpre = (dx4 * pe * gp * (1.0 - gp)).astype(BF16)
        dpre_ref[...] = dpre
        dh3 = _dot_nt(dpre, wpg_ref[...])
        dgple_ref[...] += _colsum(dh3 * n3)
        dx3_ref[...] = dx4 + _rms_bwd(dh3, n3, r3, gple_ref[...])

    sd = jax.ShapeDtypeStruct
    return pl.pallas_call(
        body, name="ple_loss", grid=(s // tb,),
        in_specs=[_rows(tb, d), _rows(tb, dp), _rows(tb, d), _resident(w_pg.shape), _resident(w_p_t.shape),
                  _resident(g_ple.shape), _resident(g_fin.shape)],
        out_specs=[_rows(tb, d), _rows(tb, d), _rows(tb, d), _rows(tb, d), _rows(tb, dp),
                   _acc_spec((1, LANES)), _acc_spec((1, d)), _acc_spec((1, d))],
        out_shape=[sd((s, d), F32), sd((s, d), BF16), sd((s, d), BF16), sd((s, d), BF16), sd((s, dp), BF16),
                   sd((1, LANES), F32), sd((1, d), F32), sd((1, d), F32)],
        compiler_params=_cparams(("arbitrary",)))(x3, p, target, w_pg, w_p_t, g_ple, g_fin)


def _mlp_bwd(dx3, x2, r, w1_t, w2, g_mlp, tb, tc):
    s, d = x2.shape
    dff = w1_t.shape[0]

    def body(dx3_ref, x_ref, r_ref, w1_ref, w2_ref, g_ref, dx2_ref, df_ref, dg_ref, dh_ref):
        @pl.when(pl.program_id(0) == 0)
        def _():
            dg_ref[...] = jnp.zeros_like(dg_ref)

        dx3v = dx3_ref[...]
        dx3b = dx3v.astype(BF16)
        dh_ref[...] = jnp.zeros_like(dh_ref)
        for c in range(dff // tc):
            sl = slice(tc * c, tc * c + tc)
            df = (_dot_nt(dx3b, w2_ref[sl, :]) * (2.0 * r_ref[:, sl].astype(F32))).astype(BF16)
            df_ref[:, sl] = df
            dh_ref[...] += _dot_nn(df, w1_ref[sl, :])
        xv = x_ref[...]
        r2 = _rstd(xv)
        n2 = xv * r2
        dh = dh_ref[...]
        dg_ref[...] += _colsum(dh * n2)
        dx2_ref[...] = dx3v + _rms_bwd(dh, n2, r2, g_ref[...])

    sd = jax.ShapeDtypeStruct
    return pl.pallas_call(
        body, name="mlp_bwd", grid=(s // tb,),
        in_specs=[_rows(tb, d), _rows(tb, d), _rows(tb, dff), _resident(w1_t.shape), _resident(w2.shape),
                  _resident(g_mlp.shape)],
        out_specs=[_rows(tb, d), _rows(tb, dff), _acc_spec((1, d))],
        out_shape=[sd((s, d), F32), sd((s, dff), BF16), sd((1, d), F32)],
        scratch_shapes=[pltpu.VMEM((tb, d), F32)],
        compiler_params=_cparams(("arbitrary",)))(dx3, x2, r, w1_t, w2, g_mlp)


def _mix_out_bwd(dx2, ya, yb, ga, gb, ob, w_oa, w_ob_t, w_o, tb, after):
    s, d = dx2.shape

    def body(dx_ref, ya_ref, yb_ref, ga_ref, gb_ref, ob_ref, woa_ref, wob_ref, wo_ref, after_ref,
             doa_ref, dob0_ref, dob1_ref, dob2_ref, dd0_ref, dd1_ref, dd2_ref, dga_ref, dgb_ref, dya_ref, dyb_ref,
             dbg_ref, scr_ref):
        @pl.when(pl.program_id(0) == 0)
        def _():
            dbg_ref[...] = jnp.zeros_like(dbg_ref)

        du = _dot_nt(dx_ref[...].astype(BF16), wo_ref[...])
        gav, gbv = ga_ref[...].astype(F32), gb_ref[...].astype(F32)
        dya = (du * gav).astype(BF16)
        dyb = (du * gbv).astype(BF16)
        dya_ref[...] = dya
        dyb_ref[...] = dyb
        dga = du * ya_ref[...].astype(F32) * gav * (1.0 - gav)
        dgb = du * yb_ref[...].astype(F32) * gbv * (1.0 - gbv)
        dga_ref[...] = dga.astype(BF16)
        dgb_ref[...] = dgb.astype(BF16)
        dbg_ref[:, 0:d] += _colsum(dga)
        dbg_ref[:, d:2 * d] += _colsum(dgb)
        doa_ref[...] = _dot_nt(dya, woa_ref[...]).astype(BF16)
        dob = _dot_nn(dyb, wob_ref[...])
        dd = _seg_sum(dob * ob_ref[...].astype(F32))
        for dil, dob_ref, dd_ref in zip(DILATIONS, (dob0_ref, dob1_ref, dob2_ref), (dd0_ref, dd1_ref, dd2_ref)):
            _to_residues(dob, dob_ref, scr_ref, dil, BF16)
            _to_residues(dd, dd_ref, scr_ref, dil, F32)

    sd = jax.ShapeDtypeStruct
    res = list(pl.pallas_call(
        body, name="mix_out_bwd", grid=(s // tb,),
        in_specs=[_rows(tb, d)] * 5 + [_rows(tb, GB_W), _resident(w_oa.shape), _resident(w_ob_t.shape),
                                       _resident(w_o.shape), _ANY],
        out_specs=[_rows(tb, QA_W)] + _dil_specs(tb) * 2 + [_rows(tb, d), _rows(tb, d), _rows(tb, d),
                                                           _rows(tb, d), _acc_spec((1, 2 * d))],
        out_shape=[sd((s, QA_W), BF16)] + _dil_shapes(s, BF16) + _dil_shapes(s, F32) + [
            sd((s, d), BF16), sd((s, d), BF16), sd((s, d), BF16), sd((s, d), BF16), sd((1, 2 * d), F32)],
        scratch_shapes=[pltpu.VMEM((2, tb, LANES), F32)],
        compiler_params=_cparams(("arbitrary",)))(dx2, ya, yb, ga, gb, ob, w_oa, w_ob_t, w_o, after))
    return res[:1] + [res[1:4], res[4:7]] + res[7:]


def _in_proj_bwd(dx2, x, dqrot, dkrot, dva, qraw, kraw, tabs, dqb, dkb, dvb, dga, dgb, w_in_t, g_mix, q_g, k_g, tb):
    s, d = x.shape
    din = w_in_t.shape[0]
    q_scale = HEAD_DIM_A ** -0.5
    b_scale = HEAD_DIM_B ** -0.5
    tc = 256

    def body(dx2_ref, x_ref, dq_ref, dk_ref, dv_ref, qraw_ref, kraw_ref, c_ref, s1_ref, s2_ref, *rest):
        dqb_refs, dkb_refs, dvb_refs = rest[0:3], rest[3:6], rest[6:9]
        (dga_ref, dgb_ref, w_ref, gmix_ref, qg_ref, kg_ref,
         dx_ref, dz_ref, dgmix_ref, dqg_ref, dkg_ref, dh_ref, scr_ref) = rest[9:]

        @pl.when(pl.program_id(0) == 0)
        def _():
            dgmix_ref[...] = jnp.zeros_like(dgmix_ref)
            dqg_ref[...] = jnp.zeros_like(dqg_ref)
            dkg_ref[...] = jnp.zeros_like(dkg_ref)

        cos, s1, s2 = c_ref[...][None], s1_ref[...][None], s2_ref[...][None]

        def heads_bwd(drot, z, g_ref, acc_ref):
            dn = drot * cos + pltpu.roll(drot * s1, 96, 2) + pltpu.roll(drot * s2, 32, 2)
            rr = _rstd(z)
            nn = z * rr
            acc_ref[...] += jnp.sum(jnp.sum(dn * nn, axis=0), axis=0, keepdims=True)
            return _rms_bwd(dn, nn, rr, g_ref[...][None]).astype(BF16)

        dh_ref[...] = jnp.zeros_like(dh_ref)

        def emit(off, piece):
            dz_ref[:, off:off + tc] = piece
            dh_ref[...] += _dot_nn(piece, w_ref[off:off + tc, :])

        for j in range(d // tc):
            emit(OFF_GA + tc * j, dga_ref[:, tc * j:tc * j + tc])
            emit(OFF_GA + d + tc * j, dgb_ref[:, tc * j:tc * j + tc])
        emit(OFF_VA, dv_ref[...].astype(BF16))
        for g, dil in enumerate(DILATIONS):
            emit(OFF_QB + GB_W * g, (_from_residues(dqb_refs[g], scr_ref, dil) * b_scale).astype(BF16))
            emit(OFF_KB + GB_W * g, (_from_residues(dkb_refs[g], scr_ref, dil) * LN_2).astype(BF16))
            emit(OFF_VB + GB_W * g, _from_residues(dvb_refs[g], scr_ref, dil).astype(BF16))
        stack = lambda ref, n: jnp.stack([ref[:, 128 * h:128 * h + 128] for h in range(n)], axis=0)
        dzq = heads_bwd(stack(dq_ref, N_Q_HEADS_A) * q_scale, stack(qraw_ref, N_Q_HEADS_A), qg_ref, dqg_ref)
        dzk = heads_bwd(stack(dk_ref, N_KV_HEADS_A) * LN_2, stack(kraw_ref, N_KV_HEADS_A), kg_ref, dkg_ref)
        for j in range(N_Q_HEADS_A // 2):
            emit(OFF_QA + tc * j, jnp.concatenate([dzq[2 * j], dzq[2 * j + 1]], axis=1))
        emit(OFF_KA, jnp.concatenate([dzk[0], dzk[1]], axis=1))
        xv = x_ref[...]
        r1 = _rstd(xv)
        n1 = xv * r1
        dh = dh_ref[...]
        dgmix_ref[...] += _colsum(dh * n1)
        dx_ref[...] = dx2_ref[...] + _rms_bwd(dh, n1, r1, gmix_ref[...])

    sd = jax.ShapeDtypeStruct
    return pl.pallas_call(
        body, name="in_proj_bwd", grid=(s // tb,),
        in_specs=[_rows(tb, d), _rows(tb, d), _rows(tb, QA_W), _rows(tb, KA_W), _rows(tb, KA_W), _rows(tb, QA_W),
                  _rows(tb, KA_W), _rows(tb, LANES), _rows(tb, LANES), _rows(tb, LANES),
                  ] + _dil_specs(tb) * 3 + [_rows(tb, d), _rows(tb, d),
                  _resident(w_in_t.shape), _resident(g_mix.shape), _resident(q_g.shape), _resident(k_g.shape)],
        out_specs=[_rows(tb, d), _rows(tb, din), _acc_spec((1, d)), _acc_spec((1, HEAD_DIM_A)),
                   _acc_spec((1, HEAD_DIM_A))],
        out_shape=[sd((s, d), F32), sd((s, din), BF16), sd((1, d), F32), sd((1, HEAD_DIM_A), F32),
                   sd((1, HEAD_DIM_A), F32)],
        scratch_shapes=[pltpu.VMEM((tb, d), F32), pltpu.VMEM((2, tb, LANES), F32)],
        compiler_params=_cparams(("arbitrary",)))(
        dx2, x, dqrot, dkrot, dva, qraw, kraw, *tabs, *dqb, *dkb, *dvb, dga, dgb, w_in_t, g_mix, q_g, k_g)


def _identity(v):
    return v


def _to_bf16(v):
    return v.astype(BF16)


def _square_bf16(v):
    vf = v.astype(F32)
    return (vf * vf).astype(BF16)


def _weight_grad(name, a, b, ti, tj, tk, a_fn=_identity, b_fn=_identity, col0=0, n=None, after=None):
    t, m = a.shape
    n = b.shape[1] if n is None else n
    n_k = t // tk
    after = a if after is None else after

    def body(a_ref, b_ref, after_ref, o_ref, acc_ref):
        k = pl.program_id(2)

        @pl.when(k == 0)
        def _():
            acc_ref[...] = jnp.zeros_like(acc_ref)

        acc_ref[...] += _dot_tn(a_fn(a_ref[...]), b_fn(b_ref[...]))

        @pl.when(k == n_k - 1)
        def _():
            o_ref[...] = acc_ref[...].astype(BF16)

    return pl.pallas_call(
        body, name=name, grid=(m // ti, n // tj, n_k),
        in_specs=[pl.BlockSpec((tk, ti), lambda i, j, k: (k, i)),
                  pl.BlockSpec((tk, tj), lambda i, j, k: (k, j + col0 // tj)), _ANY],
        out_specs=pl.BlockSpec((ti, tj), lambda i, j, k: (i, j)),
        out_shape=jax.ShapeDtypeStruct((m, n), BF16),
        scratch_shapes=[pltpu.VMEM((ti, tj), F32)],
        compiler_params=_cparams(("arbitrary", "arbitrary", "arbitrary")))(a, b, after)


def _sum_slots(name, recv, own, transposed):
    m, n, k = recv.shape
    tc = min(k, 256)
    n_pad = -(-n // LANES) * LANES

    def body(own_ref, r_ref, o_ref):
        acc = own_ref[...].astype(F32)
        for i in range(m):
            acc = acc + r_ref[i].astype(F32)
        if transposed:
            if n_pad != n:
                acc = jnp.concatenate([acc, jnp.zeros((n_pad - n, tc), F32)], axis=0)
            acc = acc.T[:, :n]
        o_ref[...] = acc

    out_spec, out_shape = ((pl.BlockSpec((tc, n), lambda j: (j, 0)), (k, n)) if transposed
                           else (pl.BlockSpec((n, tc), lambda j: (0, j)), (n, k)))
    return pl.pallas_call(
        body, name=name, grid=(k // tc,),
        in_specs=[pl.BlockSpec((n, tc), lambda j: (0, j)), pl.BlockSpec((m, n, tc), lambda j: (0, 0, j))],
        out_specs=out_spec, out_shape=jax.ShapeDtypeStruct(out_shape, F32),
        compiler_params=_cparams(("arbitrary",)))(own, recv)


def _adamw_math(w, g, m, v):
    m = ADAM_B1 * m + (1.0 - ADAM_B1) * g
    v = ADAM_B2 * v + (1.0 - ADAM_B2) * (g * g)
    m_hat = m / (1.0 - ADAM_B1 ** ADAM_STEP)
    v_hat = v / (1.0 - ADAM_B2 ** ADAM_STEP)
    delta = -ADAM_LR * (m_hat / (jnp.sqrt(v_hat) + ADAM_EPS) + ADAM_WD * w)
    return delta, m, v


def _adamw(name, w, g, m, v):
    r, c = w.shape
    tr = max(t for t in range(8, min(r, 256) + 1, 8) if r % t == 0)

    def body(w_ref, g_ref, m_ref, v_ref, d_ref, mo_ref, vo_ref):
        d_ref[...], mo_ref[...], vo_ref[...] = _adamw_math(w_ref[...], g_ref[...], m_ref[...], v_ref[...])

    spec = pl.BlockSpec((tr, c), lambda i: (i, 0))
    return pl.pallas_call(
        body, name=name, grid=(r // tr,), in_specs=[spec] * 4, out_specs=[spec] * 3,
        out_shape=[jax.ShapeDtypeStruct((r, c), F32)] * 3,
        compiler_params=_cparams(("arbitrary",)))(w, g, m, v)


def _small_update(parts, w, m, v):
    def body(p_ref, w_ref, m_ref, v_ref, g_ref, d_ref, mo_ref, vo_ref):
        g = p_ref[0]
        for i in range(1, N_DEV):
            g = g + p_ref[i]
        g_ref[...] = g
        d_ref[...], mo_ref[...], vo_ref[...] = _adamw_math(w_ref[...], g, m_ref[...], v_ref[...])

    return pl.pallas_call(body, name="small_update", out_shape=[jax.ShapeDtypeStruct(w.shape, F32)] * 4)(
        parts, w, m, v)


def _pack_rows(vectors, n_rows):
    flat = jnp.concatenate([v.reshape(-1).astype(F32) for v in vectors])
    flat = jnp.pad(flat, (0, n_rows * LANES - flat.shape[0]))
    return flat.reshape(n_rows, LANES)


def _pick_tile(n, prefs):
    for t in prefs:
        if n % t == 0:
            return t
    return n


def kernel(x, p, norm_mix_g, w_in, b_gate, q_norm_g, k_norm_g, rel_bias, w_out_a, w_out_b, w_out, norm_mlp_g, w_ff1, w_ff2, norm_ple_g, w_ple_gate, w_ple, final_norm_g, loss_target, m_norm_mix_g, m_w_in, m_b_gate, m_q_norm_g, m_k_norm_g, m_rel_bias, m_w_out_a, m_w_out_b, m_w_out, m_norm_mlp_g, m_w_ff1, m_w_ff2, m_norm_ple_g, m_w_ple_gate, m_w_ple, m_final_norm_g, v_norm_mix_g, v_w_in, v_b_gate, v_q_norm_g, v_k_norm_g, v_rel_bias, v_w_out_a, v_w_out_b, v_w_out, v_norm_mlp_g, v_w_ff1, v_w_ff2, v_norm_ple_g, v_w_ple_gate, v_w_ple, v_final_norm_g):
    s, d = x.shape[1], x.shape[2]
    xs, ps, ts = x[0], p[0, 0], loss_target[0]
    tb = _pick_tile(s, (512, 256))
    tq = _pick_tile(s, (256,))
    tk = _pick_tile(s, (1024, 512))
    cb = _pick_tile(s, (1024, 512))
    fin_g = final_norm_g.reshape(1, d)

    col_sharded = {"w_in": w_in[0], "w_out_b": w_out_b[0], "w_ff1": w_ff1[0], "w_ple": w_ple[0]}
    row_sharded = {"w_out_a": w_out_a[0], "w_out": w_out[0], "w_ff2": w_ff2[0], "w_ple_gate": w_ple_gate[0]}
    order = ["w_in", "w_out_a", "w_out_b", "w_out", "w_ff1", "w_ff2", "w_ple_gate", "w_ple"]
    shards = [(col_sharded[n].T if n in col_sharded else row_sharded[n]).astype(BF16) for n in order]
    my_idx = 4 * lax.axis_index("x") + 2 * lax.axis_index("y") + lax.axis_index("c")
    (w_in_t,) = _all_gather(shards[:1], 1)
    zones = _place_own_rows(shards[1:], my_idx)
    ag = _copies_start("weights_gather_start", shards[1:], zones, w_in_t, True)

    tabs = _rope_tables(s)
    (h1, qraw, kraw, qrot, krot, va, qb, kb, vb, ga, gb) = _in_proj(
        xs, tabs, w_in_t, norm_mix_g, b_gate, q_norm_g, k_norm_g, tb, ag[4])
    oa, lse_a = _attn_a_fwd(qrot, krot, va, _pick_tile(s, (2048, 1024, 512, 256)), _pick_tile(s, (512,)))
    _, (w_oa, w_ob_t, w_o, w_ff1_t, w_ff2_f, w_pg, w_p_t) = _copies_wait(
        "weights_gather_wait", ag[0], ag[1], ag[2], ag[3], lse_a, True)
    flat = lambda arrs: [a.reshape(s, GB_W) for a in arrs]
    split = lambda arrs: [a.reshape(dil, s // dil, GB_W) for a, dil in zip(arrs, DILATIONS)]
    qb_r, kb_r, vb_r = flat(qb), flat(kb), flat(vb)
    bmaps = [jnp.asarray(_bucket_map(dil)) for dil in DILATIONS]
    bias_tabs = [rel_bias[:, N_HEADS_PER_DIL * g:N_HEADS_PER_DIL * (g + 1)] for g in range(3)]
    band_out = [_band_fwd(dil, qb_r[g], kb_r[g], vb_r[g], bmaps[g], bias_tabs[g], cb)
                for g, dil in enumerate(DILATIONS)]
    og, lg = split([o for o, _ in band_out]), split([l for _, l in band_out])
    x2, ob, lse_b, ya, yb, u = _mix_out(xs, oa, og, lg, ga, gb, w_oa, w_ob_t, w_o, tb)
    tc = _pick_tile(w_ff1_t.shape[0], (512,))
    x3, r_act, h2 = _mlp_fwd(x2, w_ff1_t, w_ff2_f, norm_mlp_g, tb, tc)

    dx3, h3, dpre, dpe, pb, loss_part, dg_fin, dg_ple = _ple_loss(
        x3, ps, ts, w_pg, w_p_t, norm_ple_g, fin_g, tb)
    dx2, df, dg_mlp = _mlp_bwd(dx3, x2, r_act, w_ff1_t, w_ff2_f, norm_mlp_g, tb, tc)

    tkk = _pick_tile(s, (1024, 512))
    tk2 = _pick_tile(s, (2048, 1024, 512))
    dff = w_ff1_t.shape[0]
    t1k = lambda n: _pick_tile(n, (1024, 512, 256))
    slots = lambda parts: [lax.empty((7, a.shape[0] // N_DEV, a.shape[1]), BF16) for a in parts]
    part1 = [_weight_grad("grad_w_ff1", df, h2, t1k(dff), t1k(d), tkk),
             _weight_grad("grad_w_ff2", r_act, dx3, t1k(dff), t1k(d), tkk, a_fn=_square_bf16, b_fn=_to_bf16),
             _weight_grad("grad_w_ple_gate", h3, dpre, t1k(d), t1k(d), tk2),
             _weight_grad("grad_w_ple", dpe, pb, t1k(d), ps.shape[1], tk2)]
    doa, dob, dd, dga, dgb, dya, dyb, dbg = _mix_out_bwd(dx2, ya, yb, ga, gb, ob, w_oa, w_ob_t, w_o, tb, dx2)
    part1 += [_weight_grad("grad_w_out_a", oa, dya, t1k(QA_W), t1k(d), tk2),
              _weight_grad("grad_w_out_b", dyb, ob, t1k(d), GB_W, tk2),
              _weight_grad("grad_w_out", u, dx2, t1k(d), t1k(d), tkk, b_fn=_to_bf16)]
    rs1 = _copies_start("grads1_start", part1, slots(part1), doa, False)
    dqrot, dkrot, dva = _attn_a_bwd(qrot, krot, va, oa, doa, lse_a, tq, tk, rs1[4])
    dob_r, lse_r, dd_r = flat(dob), flat(lse_b), flat(dd)
    band_bwd = [_band_bwd(dil, qb_r[g], kb_r[g], vb_r[g], dob_r[g], lse_r[g], dd_r[g], bmaps[g], bias_tabs[g], cb)
                for g, dil in enumerate(DILATIONS)]
    dqb, dkb, dvb = [split([r[j] for r in band_bwd]) for j in range(3)]
    grad_x, dz, dg_mix, dg_q, dg_k = _in_proj_bwd(
        dx2, xs, dqrot, dkrot, dva, qraw, kraw, tabs, dqb, dkb, dvb, dga, dgb, w_in_t, norm_mix_g,
        q_norm_g, k_norm_g, _pick_tile(s, (256,)))
    d_rel = jnp.concatenate([r[3][:, :N_HEADS_PER_DIL] for r in band_bwd], axis=1)

    din = w_in_t.shape[0]
    ti_in = _pick_tile(din, (din // 2,)) if (din // 2) % LANES == 0 else din
    hd_ = d // 2
    part3 = [_weight_grad("grad_w_in_lo", dz, h1, ti_in, t1k(hd_), tkk, n=hd_)]
    rs3 = _copies_start("grads3_start", part3, slots(part3), grad_x, False)
    part4 = [_weight_grad("grad_w_in_hi", dz, h1, ti_in, t1k(hd_), tkk, col0=hd_, n=hd_, after=rs3[4])]
    rs4 = _copies_start("grads4_start", part4, slots(part4), rs3[4], False)

    def own_rows(a):
        n = a.shape[0] // N_DEV
        return lax.dynamic_slice(a, (my_idx * n, 0), (n, a.shape[1]))

    sums = {}
    src1, got1 = _copies_wait("grads1_wait", rs1[0], rs1[1], rs1[2], rs1[3], rs4[4], False)
    for n, a, r in zip(["w_ff1", "w_ff2", "w_ple_gate", "w_ple", "w_out_a", "w_out_b", "w_out"], src1, got1):
        sums[n] = _sum_slots("sum_" + n, r, own_rows(a), n in col_sharded)
    given_w = dict(w_in=w_in, w_out_a=w_out_a, w_out_b=w_out_b, w_out=w_out, w_ff1=w_ff1, w_ff2=w_ff2,
                   w_ple_gate=w_ple_gate, w_ple=w_ple)
    given_m = dict(w_in=m_w_in, w_out_a=m_w_out_a, w_out_b=m_w_out_b, w_out=m_w_out, w_ff1=m_w_ff1, w_ff2=m_w_ff2,
                   w_ple_gate=m_w_ple_gate, w_ple=m_w_ple)
    given_v = dict(w_in=v_w_in, w_out_a=v_w_out_a, w_out_b=v_w_out_b, w_out=v_w_out, w_ff1=v_w_ff1, w_ff2=v_w_ff2,
                   w_ple_gate=v_w_ple_gate, w_ple=v_w_ple)
    big = {}

    def update(n, transposed=False):
        view = (lambda a: a.T) if transposed else (lambda a: a)
        g = sums[n]
        delta, new_m, new_v = _adamw("adamw_" + n, view(given_w[n][0]), g, view(given_m[n][0]), view(given_v[n][0]))
        big[n] = tuple(view(a)[None] for a in (g, delta, new_m, new_v))

    for n in order[1:]:
        update(n)

    small_names = ["norm_mix_g", "b_gate", "q_norm_g", "k_norm_g", "rel_bias", "norm_mlp_g", "norm_ple_g",
                   "final_norm_g"]
    small_w = [norm_mix_g, b_gate, q_norm_g, k_norm_g, rel_bias, norm_mlp_g, norm_ple_g, final_norm_g]
    small_m = [m_norm_mix_g, m_b_gate, m_q_norm_g, m_k_norm_g, m_rel_bias, m_norm_mlp_g, m_norm_ple_g,
               m_final_norm_g]
    small_v = [v_norm_mix_g, v_b_gate, v_q_norm_g, v_k_norm_g, v_rel_bias, v_norm_mlp_g, v_norm_ple_g,
               v_final_norm_g]
    small_g = [dg_mix, dbg, dg_q, dg_k, d_rel, dg_mlp, dg_ple, dg_fin]
    sizes = [int(np.prod(w.shape)) for w in small_w]
    n_rows = -(-(sum(-(-sz // LANES) for sz in sizes) + 1) // 8) * 8
    pad = lambda v: jnp.pad(v.reshape(-1).astype(F32), (0, -v.size % LANES))
    pack = lambda vs, last: _pack_rows([pad(v) for v in vs] + [last], n_rows)
    zero_row = jnp.zeros((LANES,), F32)
    parts = _small_all_gather(pack(small_g, loss_part.reshape(-1) * (jnp.arange(LANES) == 0)), big["w_ple"][1])
    g_all, d_all, m_all, v_all = _small_update(parts, pack(small_w, zero_row), pack(small_m, zero_row),
                                               pack(small_v, zero_row))
    small = {}
    row = 0
    for n, w, sz in zip(small_names, small_w, sizes):
        nr = -(-sz // LANES)
        small[n] = tuple(a[row:row + nr].reshape(-1)[:sz].reshape(w.shape) for a in (g_all, d_all, m_all, v_all))
        row += nr
    loss = g_all[row, 0]

    src3, got3 = _copies_wait("grads3_wait", rs3[0], rs3[1], rs3[2], rs3[3], g_all, False)
    src4, got4 = _copies_wait("grads4_wait", rs4[0], rs4[1], rs4[2], rs4[3], g_all, False)
    sums["w_in"] = jnp.concatenate([_sum_slots("sum_w_in_lo", got3[0], own_rows(src3[0]), False),
                                    _sum_slots("sum_w_in_hi", got4[0], own_rows(src4[0]), False)], axis=1)
    update("w_in", transposed=True)

    names = ["norm_mix_g", "w_in", "b_gate", "q_norm_g", "k_norm_g", "rel_bias", "w_out_a", "w_out_b", "w_out",
             "norm_mlp_g", "w_ff1", "w_ff2", "norm_ple_g", "w_ple_gate", "w_ple", "final_norm_g"]
    res = {n: (big[n] if n in big else small[n]) for n in names}
    return (loss, grad_x[None], *[res[n][0] for n in names], *[res[n][1] for n in names],
            *[res[n][2] for n in names], *[res[n][3] for n in names])
```
